```python
import math
import jax, jax.numpy as jnp
from jax import lax
import numpy as np

D_MODEL = 1024
BATCH = 4
SEQ = 4096
DEPTH = 4

HEAD_DIM = 64
A_HEADS = 4
A_VDIM = 2 * HEAD_DIM
B_HEADS = 8
B_KV_HEADS = 2
B_WINDOW = 128
C_HEADS = 8
C_KV_HEADS = 2
CMP_BLOCK = 32
CMP_STRIDE = 16
CMP_HIDDEN = 256
SEL_BLOCK = 64
N_SELECT = 16
C_WINDOW = 512
MIX_WIDTH = B_HEADS * HEAD_DIM
N_BRANCH = 3
D_FF = 4 * D_MODEL
NUM_BUCKETS = 32
MAX_DISTANCE = 128
TOTAL_HEADS = A_HEADS + B_HEADS + C_HEADS
Q_BLOCK = 128
SEL_Q_BLOCK = 64
NEG_INF = -1e30
FORCE_BONUS = 1e4
EPS = 1e-6

SPLIT_SIZES = (
    A_HEADS * 2 * HEAD_DIM,
    A_HEADS * 2 * HEAD_DIM,
    A_HEADS * A_VDIM,
    B_HEADS * HEAD_DIM,
    B_KV_HEADS * HEAD_DIM,
    B_KV_HEADS * HEAD_DIM,
    C_HEADS * HEAD_DIM,
    6 * C_KV_HEADS * HEAD_DIM,
    C_HEADS * 3,
    N_BRANCH * D_MODEL,
)
IN_COLS = sum(SPLIT_SIZES)

kernel_name = 'hybrid_gated_diffattn_swa_nsa'


def rms_norm(x, g):
    xf = x.astype(jnp.float32)
    y = xf * lax.rsqrt(jnp.mean(xf * xf, axis=-1, keepdims=True) + EPS)
    return (y * g.astype(jnp.float32)).astype(x.dtype)


def rel_bucket(dist):
    n = jnp.maximum(dist, 0)
    max_exact = NUM_BUCKETS // 2
    nf = jnp.maximum(n, 1).astype(jnp.float32)
    large = max_exact + (jnp.log(nf / max_exact) / math.log(MAX_DISTANCE / max_exact)
                         * (NUM_BUCKETS - max_exact)).astype(jnp.int32)
    large = jnp.minimum(large, NUM_BUCKETS - 1)
    return jnp.where(n < max_exact, n, large)


def diff_attention(q, k, v, lam, bias_tab):
    B, S, H, _, D = q.shape
    nb = S // Q_BLOCK
    scale = D ** -0.5
    k_pos = jnp.arange(S)
    qb = q.reshape(B, nb, Q_BLOCK, H, 2, D).transpose(1, 0, 2, 3, 4, 5)

    def block(args):
        i, qi = args
        q_pos = i * Q_BLOCK + jnp.arange(Q_BLOCK)
        dist = q_pos[:, None] - k_pos[None, :]
        mask = dist >= 0
        bias = bias_tab[rel_bucket(dist)].transpose(2, 0, 1).astype(jnp.float32)
        s = jnp.einsum('bqhcd,bkhcd->cbhqk', qi, k).astype(jnp.float32) * scale + bias
        p = jax.nn.softmax(jnp.where(mask, s, NEG_INF), axis=-1)
        w = p[0] - lam * p[1]
        return jnp.einsum('bhqk,bkhe->bqhe', w.astype(v.dtype), v)

    out = lax.map(block, (jnp.arange(nb), qb))
    return out.transpose(1, 0, 2, 3, 4).reshape(B, S, H, v.shape[-1])


def banded_attention(q, k, v, bias_tab, window, sink=None):
    B, S, H, D = q.shape
    G = k.shape[2]
    R = H // G
    nb = S // Q_BLOCK
    n_prev = -(-window // Q_BLOCK)
    kw = (n_prev + 1) * Q_BLOCK

    def band(t):
        tb = t.reshape(B, nb, Q_BLOCK, G, D)
        tb = jnp.pad(tb, ((0, 0), (n_prev, 0), (0, 0), (0, 0), (0, 0)))
        return jnp.concatenate([tb[:, j:j + nb] for j in range(n_prev + 1)], axis=2)

    kb, vb = band(k), band(v)
    qb = q.reshape(B, nb, Q_BLOCK, G, R, D)
    r = jnp.arange(Q_BLOCK)
    c = jnp.arange(kw)
    dist = n_prev * Q_BLOCK + r[:, None] - c[None, :]
    k_pos = (jnp.arange(nb)[:, None] - n_prev) * Q_BLOCK + c[None, :]
    mask = ((dist >= 0) & (dist < window))[None] & (k_pos >= 0)[:, None, :]
    bias = bias_tab[rel_bucket(dist)].reshape(Q_BLOCK, kw, G, R).transpose(2, 3, 0, 1).astype(jnp.float32)
    s = jnp.einsum('bnqgrd,bnkgd->bngrqk', qb, kb).astype(jnp.float32) * D ** -0.5 + bias
    s = jnp.where(mask[:, None, None], s, NEG_INF)
    if sink is None:
        p = jax.nn.softmax(s, axis=-1)
    else:
        sk = sink.astype(jnp.float32).reshape(G, R, 1, 1)
        m = jnp.maximum(jnp.max(s, axis=-1, keepdims=True), sk)
        e = jnp.exp(s - m)
        p = e / (jnp.sum(e, axis=-1, keepdims=True) + jnp.exp(sk - m))
    o = jnp.einsum('bngrqk,bnkgd->bnqgrd', p.astype(v.dtype), vb)
    return o.reshape(B, S, H, D)


def compress_blocks(t, pos, w1, w2):
    B, S, G, D = t.shape
    nc = (S - CMP_BLOCK) // CMP_STRIDE + 1
    idx = jnp.arange(nc)[:, None] * CMP_STRIDE + jnp.arange(CMP_BLOCK)[None, :]
    blocks = t[:, idx] + pos[None, None, :, None, :]
    flat = blocks.transpose(0, 1, 3, 2, 4).reshape(B, nc, G, CMP_BLOCK * D)
    return jax.nn.gelu(flat @ w1) @ w2


def nsa_attention(q, kc, vc, ks, vs, kw, vw, gates, bias_tab):
    B, S, H, D = q.shape
    G = ks.shape[2]
    R = H // G
    nc = kc.shape[1]
    scale = D ** -0.5
    qg = q.reshape(B, S, G, R, D)
    t_pos = jnp.arange(S)

    c_start = jnp.arange(nc) * CMP_STRIDE
    cmask = (c_start + CMP_BLOCK - 1)[None, :] <= t_pos[:, None]
    s = jnp.einsum('bsgrd,bcgd->bgrsc', qg, kc).astype(jnp.float32) * scale
    p_cmp = jnp.where(cmask, jax.nn.softmax(jnp.where(cmask, s, NEG_INF), axis=-1), 0.0)
    o_cmp = jnp.einsum('bgrsc,bcgd->bsgrd', p_cmp.astype(vc.dtype), vc).reshape(B, S, H, D)

    n_blk = S // SEL_BLOCK
    j_start = jnp.arange(n_blk) * SEL_BLOCK
    overlap = ((c_start[:, None] < j_start[None, :] + SEL_BLOCK)
               & (c_start[:, None] + CMP_BLOCK > j_start[None, :])).astype(jnp.float32)
    imp = jnp.einsum('bgrsc,cj->bgsj', p_cmp, overlap)
    cur = t_pos // SEL_BLOCK
    jj = jnp.arange(n_blk)
    valid = j_start[None, :] <= t_pos[:, None]
    forced = (jj[None, :] == 0) | (jj[None, :] == cur[:, None]) | (jj[None, :] == cur[:, None] - 1)
    score = jnp.where(valid, imp + jnp.where(forced, FORCE_BONUS, 0.0), NEG_INF)
    n_sel = min(N_SELECT, n_blk)
    _, sel_idx = lax.top_k(score, n_sel)

    kbk = ks.reshape(B, n_blk, SEL_BLOCK, G, D).transpose(0, 3, 1, 2, 4)
    vbk = vs.reshape(B, n_blk, SEL_BLOCK, G, D).transpose(0, 3, 1, 2, 4)
    nqb = S // SEL_Q_BLOCK
    qs = qg.reshape(B, nqb, SEL_Q_BLOCK, G, R, D).transpose(1, 0, 2, 3, 4, 5)
    idx_b = sel_idx.reshape(B, G, nqb, SEL_Q_BLOCK, n_sel).transpose(2, 0, 1, 3, 4)
    bi = jnp.arange(B)[:, None, None, None]
    gi = jnp.arange(G)[None, :, None, None]
    bias_g = bias_tab.reshape(NUM_BUCKETS, G, R).transpose(1, 0, 2)

    def sel_block(args):
        i, qi, ii = args
        kg = kbk[bi, gi, ii]
        vg = vbk[bi, gi, ii].reshape(B, G, SEL_Q_BLOCK, n_sel * SEL_BLOCK, D)
        q_pos = i * SEL_Q_BLOCK + jnp.arange(SEL_Q_BLOCK)
        k_pos = (ii[..., None] * SEL_BLOCK + jnp.arange(SEL_BLOCK)).reshape(B, G, SEL_Q_BLOCK, n_sel * SEL_BLOCK)
        dist = q_pos[None, None, :, None] - k_pos
        mask = dist >= 0
        bias = bias_g[gi, rel_bucket(dist)].transpose(0, 1, 4, 2, 3).astype(jnp.float32)
        s = jnp.einsum('bqgrd,bgqnkd->bgrqnk', qi, kg).reshape(B, G, R, SEL_Q_BLOCK, n_sel * SEL_BLOCK)
        s = s.astype(jnp.float32) * scale + bias
        p = jax.nn.softmax(jnp.where(mask[:, :, None], s, NEG_INF), axis=-1)
        return jnp.einsum('bgrqk,bgqkd->bqgrd', p.astype(vg.dtype), vg)

    o_sel = lax.map(sel_block, (jnp.arange(nqb), qs, idx_b))
    o_sel = o_sel.transpose(1, 0, 2, 3, 4, 5).reshape(B, S, H, D)

    o_win = banded_attention(q, kw, vw, bias_tab, C_WINDOW)

    g = jax.nn.sigmoid(gates.astype(jnp.float32)).astype(q.dtype)
    out = g[..., 0:1] * o_cmp + g[..., 1:2] * o_sel + g[..., 2:3] * o_win
    return out.reshape(B, S, H * D)


def setup_inputs(seed: int = 0) -> dict:
    key = jax.random.key(seed)
    ks = jax.random.split(key, 17)

    def nrm(k, shape, scale):
        return jax.random.normal(k, shape, jnp.float32) * scale

    return {
        'x': nrm(ks[0], (BATCH, SEQ, D_MODEL), 1.0),
        'w_in': nrm(ks[1], (DEPTH, D_MODEL, IN_COLS), D_MODEL ** -0.5),
        'qk_gain': 1.0 + nrm(ks[2], (DEPTH, 8, HEAD_DIM), 0.1),
        'diff_lambda': nrm(ks[3], (DEPTH, 4, HEAD_DIM), 0.1),
        'diff_subln': 1.0 + nrm(ks[4], (DEPTH, A_VDIM), 0.1),
        'sinks': nrm(ks[5], (DEPTH, B_HEADS), 0.5),
        'cmp_pos': nrm(ks[6], (DEPTH, 2, CMP_BLOCK, HEAD_DIM), 0.5),
        'cmp_w1': nrm(ks[7], (DEPTH, 2, CMP_BLOCK * HEAD_DIM, CMP_HIDDEN), (CMP_BLOCK * HEAD_DIM) ** -0.5),
        'cmp_w2': nrm(ks[8], (DEPTH, 2, CMP_HIDDEN, HEAD_DIM), CMP_HIDDEN ** -0.5),
        'w_branch': nrm(ks[9], (DEPTH, N_BRANCH, MIX_WIDTH, D_MODEL), MIX_WIDTH ** -0.5),
        'w_out': nrm(ks[10], (DEPTH, D_MODEL, D_MODEL), D_MODEL ** -0.5),
        'norm_mix': 1.0 + nrm(ks[11], (DEPTH, D_MODEL), 0.1),
        'norm_mlp': 1.0 + nrm(ks[12], (DEPTH, D_MODEL), 0.1),
        'w_up': nrm(ks[13], (DEPTH, D_MODEL, D_FF), D_MODEL ** -0.5),
        'w_down': nrm(ks[14], (DEPTH, D_FF, D_MODEL), D_FF ** -0.5),
        'rel_bias': nrm(ks[15], (NUM_BUCKETS, TOTAL_HEADS), 0.5),
    }


def reference(x, w_in, qk_gain, diff_lambda, diff_subln, sinks, cmp_pos, cmp_w1, cmp_w2,
              w_branch, w_out, norm_mix, norm_mlp, w_up, w_down, rel_bias):
    B, S, _ = x.shape
    points = []
    acc = 0
    for size in SPLIT_SIZES[:-1]:
        acc += size
        points.append(acc)
    bias_a = rel_bias[:, :A_HEADS]
    bias_b = rel_bias[:, A_HEADS:A_HEADS + B_HEADS]
    bias_c = rel_bias[:, A_HEADS + B_HEADS:]

    for layer in range(DEPTH):
        h = rms_norm(x, norm_mix[layer])
        proj = h @ w_in[layer]
        aq, ak, av, bq, bk, bv, cq, ckv, cg, mg = jnp.split(proj, points, axis=-1)
        gains = qk_gain[layer]

        aq = rms_norm(aq.reshape(B, S, A_HEADS, 2, HEAD_DIM), gains[0])
        ak = rms_norm(ak.reshape(B, S, A_HEADS, 2, HEAD_DIM), gains[1])
        av = av.reshape(B, S, A_HEADS, A_VDIM)
        lmb = diff_lambda[layer].astype(jnp.float32)
        lam_init = 0.8 - 0.6 * math.exp(-0.3 * layer)
        lam = jnp.exp(jnp.sum(lmb[0] * lmb[1])) - jnp.exp(jnp.sum(lmb[2] * lmb[3])) + lam_init
        oa = diff_attention(aq, ak, av, lam, bias_a)
        oa = (rms_norm(oa, diff_subln[layer]) * (1.0 - lam_init)).reshape(B, S, MIX_WIDTH)

        bq = rms_norm(bq.reshape(B, S, B_HEADS, HEAD_DIM), gains[2])
        bk = rms_norm(bk.reshape(B, S, B_KV_HEADS, HEAD_DIM), gains[3])
        bv = bv.reshape(B, S, B_KV_HEADS, HEAD_DIM)
        ob = banded_attention(bq, bk, bv, bias_b, B_WINDOW, sinks[layer]).reshape(B, S, MIX_WIDTH)

        cq = rms_norm(cq.reshape(B, S, C_HEADS, HEAD_DIM), gains[4])
        ckv = ckv.reshape(B, S, 6, C_KV_HEADS, HEAD_DIM)
        kc = rms_norm(compress_blocks(ckv[:, :, 0], cmp_pos[layer, 0], cmp_w1[layer, 0], cmp_w2[layer, 0]), gains[5])
        vc = compress_blocks(ckv[:, :, 1], cmp_pos[layer, 1], cmp_w1[layer, 1], cmp_w2[layer, 1])
        ksel = rms_norm(ckv[:, :, 2], gains[6])
        kwin = rms_norm(ckv[:, :, 4], gains[7])
        oc = nsa_attention(cq, kc, vc, ksel, ckv[:, :, 3], kwin, ckv[:, :, 5],
                           cg.reshape(B, S, C_HEADS, 3), bias_c)

        branches = jnp.stack([oa, ob, oc], axis=0)
        y = jnp.einsum('nbsm,nmd->bsnd', branches, w_branch[layer])
        gate = jax.nn.sigmoid(mg.reshape(B, S, N_BRANCH, D_MODEL).astype(jnp.float32)).astype(y.dtype)
        x = x + jnp.sum(gate * y, axis=2) @ w_out[layer]

        h = rms_norm(x, norm_mlp[layer])
        x = x + jnp.square(jax.nn.relu(h @ w_up[layer])) @ w_down[layer]
    return x
```

```python
import functools
import math

import numpy as np
import jax
import jax.numpy as jnp
from jax import lax
from jax.experimental import pallas as pl
from jax.experimental.pallas import tpu as pltpu

F32 = jnp.float32
BF16 = jnp.bfloat16

D_MODEL = 1024
HEAD_DIM = 64
A_HEADS = 4
A_VDIM = 128
B_HEADS = 8
B_WINDOW = 128
C_HEADS = 8
KV_GROUPS = 2
GROUP = 4
CMP_BLOCK = 32
CMP_STRIDE = 16
CMP_HIDDEN = 256
SEL_BLOCK = 64
N_SELECT = 16
C_WINDOW = 512
MIX_WIDTH = 512
D_FF = 4096
NUM_BUCKETS = 32
MAX_DISTANCE = 128
NEG_INF = -1e30
FORCE_BONUS = 1e4
EPS = 1e-6
QK_SCALE = HEAD_DIM ** -0.5

C_AQ, C_AK, C_AV, C_BQ, C_BKV, C_CQ, C_CKV, C_CG, C_MG = 0, 512, 1024, 1536, 2048, 2304, 2816, 3584, 3712
IN_COLS_PAD = C_MG + 3 * D_MODEL
N_CG = C_HEADS * 3
CG_PAD = 128
NC_PAD = 256
SEL_PAD = 128

VMEM_LIMIT = 48 * 1024 * 1024
NT_DIMS = (((1,), (1,)), ((), ()))


def _rms(x, gain):
    ms = jnp.mean(x * x, axis=-1, keepdims=True)
    return (x * lax.rsqrt(ms + EPS)) * gain


def _params(n_axes):
    return pltpu.CompilerParams(dimension_semantics=("arbitrary",) * n_axes,
                                vmem_limit_bytes=VMEM_LIMIT)


def _inproj_kernel(x_ref, nrm_ref, w_ref, gains_ref,
                   qa_ref, ka_ref, va_ref, qb_ref, kb_ref, vb_ref, qc_ref,
                   tkv_ref, ksel_ref, vsel_ref, kwin_ref, vwin_ref, gc_ref, gm_ref):
    hb = _rms(x_ref[0], nrm_ref[...]).astype(BF16)

    def mm(c0, n):
        return jnp.dot(hb, w_ref[:, c0:c0 + n], preferred_element_type=F32)

    def head(res, j, gain_idx=None, scale=None):
        sl = res[:, HEAD_DIM * j:HEAD_DIM * (j + 1)]
        if gain_idx is not None:
            sl = _rms(sl, gains_ref[gain_idx:gain_idx + 1, :])
        if scale is not None:
            sl = sl * scale
        return sl

    for ch in range(2):
        rq = mm(C_AQ + 256 * ch, 256)
        rk = mm(C_AK + 256 * ch, 256)
        rv = mm(C_AV + 256 * ch, 256)
        for j in range(4):
            hh, cc = divmod(ch * 4 + j, 2)
            qa_ref[0, hh, cc] = head(rq, j, 0, QK_SCALE).astype(BF16)
            ka_ref[0, hh, cc] = head(rk, j, 1).astype(BF16)
        for j in range(2):
            va_ref[0, ch * 2 + j] = rv[:, 128 * j:128 * (j + 1)].astype(BF16)
        rq = mm(C_BQ + 256 * ch, 256)
        rc = mm(C_CQ + 256 * ch, 256)
        for j in range(4):
            qb_ref[0, ch, j] = head(rq, j, 2, QK_SCALE).astype(BF16)
            qc_ref[0, ch, j] = head(rc, j, 4, QK_SCALE).astype(BF16)

    r = mm(C_BKV, 256)
    for g in range(2):
        kb_ref[0, g] = head(r, g, 3).astype(BF16)
        vb_ref[0, g] = head(r, 2 + g).astype(BF16)
    r = mm(C_CKV, 256)
    for j in range(4):
        tkv_ref[0, j // 2, j % 2] = head(r, j)
    r = mm(C_CKV + 256, 256)
    for g in range(2):
        ksel_ref[0, g] = head(r, g, 6).astype(BF16)
        vsel_ref[0, g] = head(r, 2 + g).astype(BF16)
    r = mm(C_CKV + 512, 256)
    for g in range(2):
        kwin_ref[0, g] = head(r, g, 7).astype(BF16)
        vwin_ref[0, g] = head(r, 2 + g).astype(BF16)

    gc_ref[0] = jax.nn.sigmoid(mm(C_CG, CG_PAD))
    for ch in range(3 * D_MODEL // 256):
        gm_ref[0, :, 256 * ch:256 * (ch + 1)] = jax.nn.sigmoid(mm(C_MG + 256 * ch, 256))


def _inproj(x, nrm, w, gains, tm=256):
    B, S, _ = x.shape
    hd = lambda n, dt: jax.ShapeDtypeStruct((B, n, S, HEAD_DIM), dt)
    hd2 = lambda a, b, dt: jax.ShapeDtypeStruct((B, a, b, S, HEAD_DIM), dt)
    out_shape = (
        hd2(A_HEADS, 2, BF16), hd2(A_HEADS, 2, BF16),
        jax.ShapeDtypeStruct((B, A_HEADS, S, A_VDIM), BF16),
        hd2(KV_GROUPS, GROUP, BF16), hd(KV_GROUPS, BF16), hd(KV_GROUPS, BF16),
        hd2(KV_GROUPS, GROUP, BF16),
        hd2(2, KV_GROUPS, F32),
        hd(KV_GROUPS, BF16), hd(KV_GROUPS, BF16), hd(KV_GROUPS, BF16), hd(KV_GROUPS, BF16),
        jax.ShapeDtypeStruct((B, S, CG_PAD), F32),
        jax.ShapeDtypeStruct((B, S, 3 * D_MODEL), F32),
    )
    s4 = lambda n, d: pl.BlockSpec((1, n, tm, d), lambda b, i: (b, 0, i, 0))
    s5 = lambda a, c: pl.BlockSpec((1, a, c, tm, HEAD_DIM), lambda b, i: (b, 0, 0, i, 0))
    row = lambda d: pl.BlockSpec((1, tm, d), lambda b, i: (b, i, 0))
    out_specs = (
        s5(A_HEADS, 2), s5(A_HEADS, 2), s4(A_HEADS, A_VDIM),
        s5(KV_GROUPS, GROUP), s4(KV_GROUPS, HEAD_DIM), s4(KV_GROUPS, HEAD_DIM),
        s5(KV_GROUPS, GROUP), s5(2, KV_GROUPS),
        s4(KV_GROUPS, HEAD_DIM), s4(KV_GROUPS, HEAD_DIM), s4(KV_GROUPS, HEAD_DIM), s4(KV_GROUPS, HEAD_DIM),
        row(CG_PAD), row(3 * D_MODEL),
    )
    return pl.pallas_call(
        _inproj_kernel,
        grid=(B, S // tm),
        in_specs=[
            row(D_MODEL),
            pl.BlockSpec((1, D_MODEL), lambda b, i: (0, 0)),
            pl.BlockSpec((D_MODEL, IN_COLS_PAD), lambda b, i: (0, 0), pipeline_mode=pl.Buffered(1)),
            pl.BlockSpec((8, HEAD_DIM), lambda b, i: (0, 0)),
        ],
        out_specs=out_specs,
        out_shape=out_shape,
        compiler_params=_params(2),
        name="inproj",
    )(x, nrm, w, gains)


def _compress_kernel(t_ref, pos_ref, w1_ref, w2_ref, gain_ref, o_ref):
    t = t_ref[0, 0, 0]
    lo = (t + pos_ref[0, 0:1, :]).astype(BF16)
    hi = (t + pos_ref[0, 1:2, :]).astype(BF16)
    v = jnp.dot(lo, w1_ref[0, 0], preferred_element_type=F32)
    u = jnp.dot(hi, w1_ref[0, 1], preferred_element_type=F32)
    pre = v + pltpu.roll(u, NC_PAD - 1, 0)
    hcur = jax.nn.gelu(pre).astype(BF16)
    out = jnp.dot(hcur, w2_ref[0], preferred_element_type=F32)
    is_key = pl.program_id(1) == 0
    out = jnp.where(is_key, _rms(out, gain_ref[...]), out)
    o_ref[0, 0, 0] = out.astype(BF16)


def _compress(tkv16, pos, w1, w2, gain):
    B = tkv16.shape[0]
    half = CMP_BLOCK // 2 * HEAD_DIM
    return pl.pallas_call(
        _compress_kernel,
        grid=(B, 2, KV_GROUPS),
        in_specs=[
            pl.BlockSpec((1, 1, 1, NC_PAD, half), lambda b, kv, g: (b, kv, g, 0, 0)),
            pl.BlockSpec((1, 2, half), lambda b, kv, g: (kv, 0, 0)),
            pl.BlockSpec((1, 2, half, CMP_HIDDEN), lambda b, kv, g: (kv, 0, 0, 0)),
            pl.BlockSpec((1, CMP_HIDDEN, HEAD_DIM), lambda b, kv, g: (kv, 0, 0)),
            pl.BlockSpec((1, HEAD_DIM), lambda b, kv, g: (0, 0)),
        ],
        out_specs=pl.BlockSpec((1, 1, 1, NC_PAD, HEAD_DIM), lambda b, kv, g: (b, kv, g, 0, 0)),
        out_shape=jax.ShapeDtypeStruct((B, 2, KV_GROUPS, NC_PAD, HEAD_DIM), BF16),
        compiler_params=_params(3),
        name="compress",
    )(tkv16, pos, w1, w2, gain)


def _cmp_attn_kernel(q_ref, kc_ref, vc_ref, ovl_ref, o_ref, selm_ref, *, T, n_cmp):
    R = GROUP
    I = pl.program_id(2)
    q = q_ref[0, 0].reshape(R * T, HEAD_DIM)
    s = lax.dot_general(q, kc_ref[0, 0, 0], NT_DIMS, preferred_element_type=F32).reshape(R, T, NC_PAD)
    t_pos = I * T + lax.broadcasted_iota(jnp.int32, (T, NC_PAD), 0)
    c_idx = lax.broadcasted_iota(jnp.int32, (T, NC_PAD), 1)
    cmask = ((c_idx * CMP_STRIDE + (CMP_BLOCK - 1) <= t_pos) & (c_idx < n_cmp))[None]
    s = jnp.where(cmask, s, NEG_INF)
    m = jnp.max(s, axis=-1, keepdims=True)
    e = jnp.where(cmask, jnp.exp(s - m), 0.0)
    l = jnp.sum(e, axis=-1, keepdims=True)
    p = e / jnp.where(l > 0.0, l, 1.0)
    oc = jnp.dot(p.astype(BF16).reshape(R * T, NC_PAD), vc_ref[0, 0, 0],
                 preferred_element_type=F32).reshape(R, T, HEAD_DIM)
    o_ref[0] = jnp.concatenate([oc[r] for r in range(R)], axis=-1)

    psum = jnp.sum(p, axis=0)
    p_hi = psum.astype(BF16)
    p_lo = (psum - p_hi.astype(F32)).astype(BF16)
    ovl = ovl_ref[...]
    imp = (lax.dot_general(ovl, p_hi, NT_DIMS, preferred_element_type=F32)
           + lax.dot_general(ovl, p_lo, NT_DIMS, preferred_element_type=F32))
    n_blk = SEL_PAD // 2
    imp = imp[:n_blk]
    j_idx = lax.broadcasted_iota(jnp.int32, (n_blk, T), 0)
    tq = I * T + lax.broadcasted_iota(jnp.int32, (n_blk, T), 1)
    cur = tq // SEL_BLOCK
    valid = j_idx * SEL_BLOCK <= tq
    forced = (j_idx == 0) | (j_idx == cur) | (j_idx == cur - 1)
    score = jnp.where(valid, imp + jnp.where(forced, FORCE_BONUS, 0.0), NEG_INF)
    rank = jnp.zeros((n_blk, T), jnp.int32)
    for jp in range(n_blk):
        row = score[jp:jp + 1, :]
        beats = (row > score) | ((row == score) & (j_idx > jp))
        rank = rank + beats.astype(jnp.int32)
    keep = jnp.where(rank < N_SELECT, 0.0, -1.0)
    full = jnp.concatenate([keep, jnp.zeros((SEL_PAD - n_blk, T), F32)], axis=0)
    selm_ref[0, 0] = full.T.astype(BF16)


def _cmp_attn(qc, kvc, ovl, n_cmp, T=256):
    B, G, R, S, _ = qc.shape
    return pl.pallas_call(
        functools.partial(_cmp_attn_kernel, T=T, n_cmp=n_cmp),
        grid=(B, G, S // T),
        in_specs=[
            pl.BlockSpec((1, 1, R, T, HEAD_DIM), lambda b, g, i: (b, g, 0, i, 0)),
            pl.BlockSpec((1, 1, 1, NC_PAD, HEAD_DIM), lambda b, g, i: (b, 0, g, 0, 0)),
            pl.BlockSpec((1, 1, 1, NC_PAD, HEAD_DIM), lambda b, g, i: (b, 1, g, 0, 0)),
            pl.BlockSpec((SEL_PAD, NC_PAD), lambda b, g, i: (0, 0)),
        ],
        out_specs=(
            pl.BlockSpec((1, T, R * HEAD_DIM), lambda b, g, i: (b, i, g)),
            pl.BlockSpec((1, 1, T, SEL_PAD), lambda b, g, i: (b, g, i, 0)),
        ),
        out_shape=(
            jax.ShapeDtypeStruct((B, S, G * R * HEAD_DIM), F32),
            jax.ShapeDtypeStruct((B, G, S, SEL_PAD), BF16),
        ),
        compiler_params=_params(3),
        name="cmp_attn",
    )(qc, kvc, kvc, ovl)


def _attn_kernel(*refs, R, T, Dv, nrel, far, k_per_r, mode, has_sel, lam_init):
    it = iter(refs)
    q_ref, k_ref, v_ref, bt_ref = next(it), next(it), next(it), next(it)
    far_ref = next(it) if far else None
    selm_ref = next(it) if has_sel else None
    exp_ref = next(it) if has_sel else None
    sink_ref = next(it) if mode == "sink" else None
    dl_ref = next(it) if mode == "diff" else None
    sub_ref = next(it) if mode == "diff" else None
    o_ref, m_ref, l_ref, acc_ref = next(it), next(it), next(it), next(it)

    I = pl.program_id(2)
    q = q_ref[0, 0]
    m_ref[...] = jnp.full(m_ref.shape, NEG_INF, F32)
    l_ref[...] = jnp.zeros(l_ref.shape, F32)
    acc_ref[...] = jnp.zeros(acc_ref.shape, F32)

    def step(J, bias):
        off = pl.multiple_of(J * T, T)
        if k_per_r:
            s = jnp.stack([
                lax.dot_general(q[r], k_ref[0, 0, r, pl.ds(off, T), :], NT_DIMS,
                                preferred_element_type=F32) for r in range(R)])
        else:
            s = lax.dot_general(q.reshape(R * T, HEAD_DIM), k_ref[0, 0, pl.ds(off, T), :], NT_DIMS,
                                preferred_element_type=F32).reshape(R, T, T)
        s = s + bias
        if has_sel:
            s = s + jnp.dot(selm_ref[0, 0], exp_ref[J], preferred_element_type=F32)[None]
        m_prev = m_ref[...]
        m_new = jnp.maximum(m_prev, jnp.max(s, axis=-1, keepdims=True))
        alpha = jnp.exp(m_prev - m_new)
        p = jnp.exp(s - m_new)
        l_ref[...] = alpha * l_ref[...] + jnp.sum(p, axis=-1, keepdims=True)
        pv = jnp.dot(p.astype(BF16).reshape(R * T, T), v_ref[0, 0, pl.ds(off, T), :],
                     preferred_element_type=F32)
        acc_ref[...] = alpha * acc_ref[...] + pv.reshape(R, T, Dv)
        m_ref[...] = m_new

    if far:
        def far_body(J, carry):
            step(J, far_ref[0])
            return carry
        lax.fori_loop(0, jnp.maximum(I - (nrel - 1), 0), far_body, 0)
    for rel in reversed(range(nrel)):
        if rel == 0:
            step(I, bt_ref[0, 0])
        else:
            @pl.when(I >= rel)
            def _(rel=rel):
                step(I - rel, bt_ref[0, rel])

    m, l, acc = m_ref[...], l_ref[...], acc_ref[...]
    if mode == "sink":
        sk = sink_ref[0]
        m_f = jnp.maximum(m, sk)
        w = jnp.exp(m - m_f)
        out = acc * w / (l * w + jnp.exp(sk - m_f))
    else:
        out = acc / l
    if mode == "diff":
        dl = dl_ref[...]
        lam = (jnp.exp(jnp.sum(dl[0:1] * dl[1:2], keepdims=True))
               - jnp.exp(jnp.sum(dl[2:3] * dl[3:4], keepdims=True)) + lam_init)
        o = out[0] - lam * out[1]
        o_ref[0] = _rms(o, sub_ref[...]) * (1.0 - lam_init)
    else:
        o_ref[0] = jnp.concatenate([out[r] for r in range(R)], axis=-1)


def _attn(q, k, v, bias_tab, *, T, mode="plain", far_bias=None, selm=None, expand=None,
          sink=None, dl=None, subln=None, lam_init=0.0, name="attn"):
    B, G, R, S, _ = q.shape
    Dv = v.shape[-1]
    nrel = bias_tab.shape[1]
    k_per_r = k.ndim == 5
    far = far_bias is not None
    has_sel = selm is not None
    ins = [q, k, v, bias_tab]
    in_specs = [
        pl.BlockSpec((1, 1, R, T, HEAD_DIM), lambda b, g, i: (b, g, 0, i, 0)),
        (pl.BlockSpec((1, 1, R, S, HEAD_DIM), lambda b, g, i: (b, g, 0, 0, 0)) if k_per_r
         else pl.BlockSpec((1, 1, S, HEAD_DIM), lambda b, g, i: (b, g, 0, 0))),
        pl.BlockSpec((1, 1, S, Dv), lambda b, g, i: (b, g, 0, 0)),
        pl.BlockSpec((1, nrel, R, T, T), lambda b, g, i: (g, 0, 0, 0, 0)),
    ]
    if far:
        ins.append(far_bias)
        in_specs.append(pl.BlockSpec((1, R, 1, T), lambda b, g, i: (g, 0, 0, 0)))
    if has_sel:
        ins += [selm, expand]
        in_specs += [
            pl.BlockSpec((1, 1, T, SEL_PAD), lambda b, g, i: (b, g, i, 0)),
            pl.BlockSpec((S // T, SEL_PAD, T), lambda b, g, i: (0, 0, 0)),
        ]
    if mode == "sink":
        ins.append(sink)
        in_specs.append(pl.BlockSpec((1, R, 1, 1), lambda b, g, i: (g, 0, 0, 0)))
    if mode == "diff":
        ins += [dl, subln]
        in_specs += [pl.BlockSpec((4, HEAD_DIM), lambda b, g, i: (0, 0)),
                     pl.BlockSpec((1, A_VDIM), lambda b, g, i: (0, 0))]
        out_w = Dv
    else:
        out_w = R * Dv
    kern = functools.partial(_attn_kernel, R=R, T=T, Dv=Dv, nrel=nrel, far=far, k_per_r=k_per_r,
                             mode=mode, has_sel=has_sel, lam_init=lam_init)
    return pl.pallas_call(
        kern,
        grid=(B, G, S // T),
        in_specs=in_specs,
        out_specs=pl.BlockSpec((1, T, out_w), lambda b, g, i: (b, i, g)),
        out_shape=jax.ShapeDtypeStruct((B, S, G * out_w), F32),
        scratch_shapes=[pltpu.VMEM((R, T, 1), F32), pltpu.VMEM((R, T, 1), F32),
                        pltpu.VMEM((R, T, Dv), F32)],
        compiler_params=_params(3),
        name=name,
    )(*ins)


def _merge_kernel(x_ref, oa_ref, ob_ref, oc_ref, os_ref, ow_ref, gc_ref, gm_ref, ex_ref, wb_ref, wo_ref, o_ref):
    gc = gc_ref[...]
    g1 = gc.astype(BF16)
    r1 = gc - g1.astype(F32)
    g2 = r1.astype(BF16)
    g3 = (r1 - g2.astype(F32)).astype(BF16)
    ex = ex_ref[...]
    gx = (jnp.dot(g1, ex, preferred_element_type=F32) + jnp.dot(g2, ex, preferred_element_type=F32)
          + jnp.dot(g3, ex, preferred_element_type=F32))
    oc = (gx[:, 0:MIX_WIDTH] * oc_ref[...] + gx[:, MIX_WIDTH:2 * MIX_WIDTH] * os_ref[...]
          + gx[:, 2 * MIX_WIDTH:3 * MIX_WIDTH] * ow_ref[...])
    z = None
    for n, br in enumerate((oa_ref[...], ob_ref[...], oc)):
        y = jnp.dot(br.astype(BF16), wb_ref[n], preferred_element_type=F32)
        t = gm_ref[:, n * D_MODEL:(n + 1) * D_MODEL] * y
        z = t if z is None else z + t
    o_ref[...] = x_ref[...] + jnp.dot(z.astype(BF16), wo_ref[...], preferred_element_type=F32)


def _merge(x2, oa, ob, oc, osel, ow, gc, gm, expand_g, wb, wo, tm=256):
    Tn = x2.shape[0]
    row = lambda d: pl.BlockSpec((tm, d), lambda i: (i, 0))
    return pl.pallas_call(
        _merge_kernel,
        grid=(Tn // tm,),
        in_specs=[row(D_MODEL), row(MIX_WIDTH), row(MIX_WIDTH), row(MIX_WIDTH), row(MIX_WIDTH), row(MIX_WIDTH),
                  row(CG_PAD), row(3 * D_MODEL),
                  pl.BlockSpec((CG_PAD, 3 * MIX_WIDTH), lambda i: (0, 0)),
                  pl.BlockSpec((3, MIX_WIDTH, D_MODEL), lambda i: (0, 0, 0)),
                  pl.BlockSpec((D_MODEL, D_MODEL), lambda i: (0, 0))],
        out_specs=row(D_MODEL),
        out_shape=jax.ShapeDtypeStruct((Tn, D_MODEL), F32),
        compiler_params=_params(1),
        name="merge",
    )(x2, oa, ob, oc, osel, ow, gc, gm, expand_g, wb, wo)


def _mlp_kernel(x_ref, nrm_ref, wu_ref, wd_ref, o_ref, h_ref, acc_ref):
    f = pl.program_id(1)

    @pl.when(f == 0)
    def _():
        h_ref[...] = _rms(x_ref[...], nrm_ref[...]).astype(BF16)
        acc_ref[...] = jnp.zeros(acc_ref.shape, F32)

    u = jnp.dot(h_ref[...], wu_ref[...], preferred_element_type=F32)
    u = jnp.square(jnp.maximum(u, 0.0)).astype(BF16)
    acc_ref[...] += jnp.dot(u, wd_ref[...], preferred_element_type=F32)

    @pl.when(f == pl.num_programs(1) - 1)
    def _():
        o_ref[...] = x_ref[...] + acc_ref[...]


def _mlp(x2, nrm, wu, wd, tm=1024, tf=512):
    Tn = x2.shape[0]
    return pl.pallas_call(
        _mlp_kernel,
        grid=(Tn // tm, D_FF // tf),
        in_specs=[pl.BlockSpec((tm, D_MODEL), lambda i, f: (i, 0)),
                  pl.BlockSpec((1, D_MODEL), lambda i, f: (0, 0)),
                  pl.BlockSpec((D_MODEL, tf), lambda i, f: (0, f)),
                  pl.BlockSpec((tf, D_MODEL), lambda i, f: (f, 0))],
        out_specs=pl.BlockSpec((tm, D_MODEL), lambda i, f: (i, 0)),
        out_shape=jax.ShapeDtypeStruct((Tn, D_MODEL), F32),
        scratch_shapes=[pltpu.VMEM((tm, D_MODEL), BF16), pltpu.VMEM((tm, D_MODEL), F32)],
        compiler_params=_params(2),
        name="mlp",
    )(x2, nrm, wu, wd)


def _bucket_np(dist):
    n = np.maximum(dist, 0)
    max_exact = NUM_BUCKETS // 2
    nf = np.maximum(n, 1).astype(np.float32)
    large = max_exact + (np.log(nf / max_exact) / math.log(MAX_DISTANCE / max_exact)
                         * (NUM_BUCKETS - max_exact)).astype(np.int32)
    large = np.minimum(large, NUM_BUCKETS - 1)
    return np.where(n < max_exact, n, large)


def _bias_tiles(tab, T, nrel, window, R):
    r = np.arange(T)[:, None]
    c = np.arange(T)[None, :]
    dist = np.stack([rel * T + r - c for rel in range(nrel)])
    allowed = dist >= 0
    if window is not None:
        allowed &= dist < window
    vals = tab.T[:, _bucket_np(dist)]
    vals = jnp.where(allowed[None], vals, NEG_INF).astype(F32)
    H = tab.shape[1]
    return vals.reshape(H // R, R, nrel, T, T).transpose(0, 2, 1, 3, 4)


def _far_rows(tab, T, R):
    H = tab.shape[1]
    return jnp.broadcast_to(tab[NUM_BUCKETS - 1].reshape(H // R, R, 1, 1), (H // R, R, 1, T)).astype(F32)


def _overlap_t(n_cmp):
    c_start = np.arange(NC_PAD) * CMP_STRIDE
    j_start = np.arange(SEL_PAD) * SEL_BLOCK
    ov = ((c_start[None, :] < j_start[:, None] + SEL_BLOCK) & (c_start[None, :] + CMP_BLOCK > j_start[:, None])
          & (np.arange(NC_PAD)[None, :] < n_cmp) & (np.arange(SEL_PAD)[:, None] < SEL_PAD // 2))
    return jnp.asarray(ov.astype(np.float32), BF16)


def _sel_expand(S, T):
    k_blk = (np.arange(S) // SEL_BLOCK).reshape(S // T, 1, T)
    hit = k_blk == np.arange(SEL_PAD).reshape(1, SEL_PAD, 1)
    return jnp.asarray(np.where(hit, -NEG_INF, 0.0).astype(np.float32), BF16)


def _gate_expand():
    e = np.zeros((CG_PAD, 3 * MIX_WIDTH), np.float32)
    for h in range(C_HEADS):
        for j in range(3):
            e[h * 3 + j, j * MIX_WIDTH + h * HEAD_DIM:j * MIX_WIDTH + (h + 1) * HEAD_DIM] = 1.0
    return jnp.asarray(e, BF16)


def kernel(x, w_in, qk_gain, diff_lambda, diff_subln, sinks, cmp_pos, cmp_w1, cmp_w2,
           w_branch, w_out, norm_mix, norm_mlp, w_up, w_down, rel_bias):
    B, S, _ = x.shape
    depth = w_in.shape[0]
    n_cmp = (S - CMP_BLOCK) // CMP_STRIDE + 1
    assert S % 256 == 0 and S // CMP_STRIDE == NC_PAD and S // SEL_BLOCK == SEL_PAD // 2
    TA = 256
    TW = 128

    w_in_p = jnp.concatenate(
        [w_in[:, :, :C_CG], jnp.pad(w_in[:, :, C_CG:C_CG + N_CG], ((0, 0), (0, 0), (0, CG_PAD - N_CG))),
         w_in[:, :, C_CG + N_CG:]], axis=-1).astype(BF16)
    w1 = cmp_w1.astype(BF16).reshape(depth, 2, 2, CMP_BLOCK // 2 * HEAD_DIM, CMP_HIDDEN)
    w2 = cmp_w2.astype(BF16)
    pos = cmp_pos.reshape(depth, 2, 2, CMP_BLOCK // 2 * HEAD_DIM)
    wb = w_branch.astype(BF16)
    wo = w_out.astype(BF16)
    wu = w_up.astype(BF16)
    wd = w_down.astype(BF16)

    bias_a = jnp.repeat(rel_bias[:, :A_HEADS], 2, axis=1)
    bias_b = rel_bias[:, A_HEADS:A_HEADS + B_HEADS]
    bias_c = rel_bias[:, A_HEADS + B_HEADS:]
    bt_a = _bias_tiles(bias_a, TA, 2, None, 2)
    far_a = _far_rows(bias_a, TA, 2)
    bt_b = _bias_tiles(bias_b, TW, 2, B_WINDOW, GROUP)
    bt_sel = _bias_tiles(bias_c, TA, 2, None, GROUP)
    far_sel = _far_rows(bias_c, TA, GROUP)
    bt_win = _bias_tiles(bias_c, TW, C_WINDOW // TW + 1, C_WINDOW, GROUP)
    ovl = _overlap_t(n_cmp)
    expand_s = _sel_expand(S, TA)
    expand_g = _gate_expand()

    for layer in range(depth):
        lam_init = 0.8 - 0.6 * math.exp(-0.3 * layer)
        (qa, ka, va, qb, kb, vb, qc, tkv, ksel, vsel, kwin, vwin, gc, gm) = _inproj(
            x, norm_mix[layer][None], w_in_p[layer], qk_gain[layer])
        oa = _attn(qa, ka, va, bt_a, T=TA, mode="diff", far_bias=far_a, dl=diff_lambda[layer],
                   subln=diff_subln[layer][None], lam_init=lam_init, name="attn_diff")
        ob = _attn(qb, kb, vb, bt_b, T=TW, mode="sink",
                   sink=sinks[layer].reshape(KV_GROUPS, GROUP, 1, 1), name="attn_swa")
        kvc = _compress(tkv.reshape(B, 2, KV_GROUPS, NC_PAD, CMP_BLOCK // 2 * HEAD_DIM), pos[layer],
                        w1[layer], w2[layer], qk_gain[layer][5:6])
        ocmp, selm = _cmp_attn(qc, kvc, ovl, n_cmp, T=TA)
        osel = _attn(qc, ksel, vsel, bt_sel, T=TA, far_bias=far_sel, selm=selm, expand=expand_s,
                     name="attn_sel")
        owin = _attn(qc, kwin, vwin, bt_win, T=TW, name="attn_win")
        f2 = lambda a: a.reshape(B * S, a.shape[-1])
        x2 = _merge(f2(x), f2(oa), f2(ob), f2(ocmp), f2(osel), f2(owin), f2(gc), f2(gm),
                    expand_g, wb[layer], wo[layer])
        x2 = _mlp(x2, norm_mlp[layer][None], wu[layer], wd[layer])
        x = x2.reshape(B, S, D_MODEL)
    return x
```

```python
import functools
import math

import numpy as np
import jax
import jax.numpy as jnp
from jax import lax
from jax.experimental import pallas as pl
from jax.experimental.pallas import tpu as pltpu

F32 = jnp.float32
BF16 = jnp.bfloat16

D_MODEL = 1024
HEAD_DIM = 64
A_HEADS = 4
A_VDIM = 128
B_HEADS = 8
B_WINDOW = 128
C_HEADS = 8
KV_GROUPS = 2
GROUP = 4
CMP_BLOCK = 32
CMP_STRIDE = 16
CMP_HIDDEN = 256
SEL_BLOCK = 64
N_SELECT = 16
C_WINDOW = 512
MIX_WIDTH = 512
D_FF = 4096
NUM_BUCKETS = 32
MAX_DISTANCE = 128
NEG_INF = -1e30
FORCE_BONUS = 1e4
EPS = 1e-6
QK_SCALE = HEAD_DIM ** -0.5

C_AQ, C_AK, C_AV, C_BQ, C_BKV, C_CQ, C_CKV, C_CG, C_MG = 0, 512, 1024, 1536, 2048, 2304, 2816, 3584, 3712
IN_COLS_PAD = C_MG + 3 * D_MODEL
N_CG = C_HEADS * 3
LANES = 128
CG_PAD = LANES
NC_PAD = 256
N_SEL_BLK = 64
ONES_ROWS = 16

T_ROW = 256
T_SWA = 128

VMEM_LIMIT = 48 * 1024 * 1024


def _rms(x, gain):
    ms = jnp.mean(x * x, axis=-1, keepdims=True)
    return (x * lax.rsqrt(ms + EPS)) * gain


def _params(n_axes):
    return pltpu.CompilerParams(dimension_semantics=("arbitrary",) * n_axes,
                                vmem_limit_bytes=VMEM_LIMIT)


def _inproj_kernel(x_ref, nrm_ref, w_ref, gains_ref,
                   qa_ref, ka_ref, va_ref, qb_ref, kb_ref, vb_ref, qc_ref,
                   tkv_ref, ksel_ref, vsel_ref, kwin_ref, vwin_ref, gc_ref, gm_ref):
    tm = x_ref.shape[1]
    hb = _rms(x_ref[0], nrm_ref[...]).astype(BF16)
    lane = lax.broadcasted_iota(jnp.int32, (tm, LANES), 1)
    first = lane < HEAD_DIM

    def ones_pad(width):
        return jnp.where(lax.broadcasted_iota(jnp.int32, (ONES_ROWS, width), 0) == 0, 1.0, 0.0).astype(BF16)

    def mm(c0, n):
        return jnp.dot(hb, w_ref[:, c0:c0 + n], preferred_element_type=F32)

    def norm_pair(y, gain_idx, scale=None):
        sq = y * y
        ms0 = jnp.sum(jnp.where(first, sq, 0.0), axis=-1, keepdims=True) * (1.0 / HEAD_DIM)
        ms1 = jnp.sum(jnp.where(first, 0.0, sq), axis=-1, keepdims=True) * (1.0 / HEAD_DIM)
        inv = jnp.where(first, lax.rsqrt(ms0 + EPS), lax.rsqrt(ms1 + EPS))
        out = (y * inv) * gains_ref[gain_idx:gain_idx + 1, :]
        return out if scale is None else out * scale

    def pairs(c0):
        r = mm(c0, 256)
        return r[:, :LANES], r[:, LANES:]

    def put_qt(ref, i0, i1, y):
        yt = y.T.astype(BF16)
        ref[0, i0, i1] = yt[:HEAD_DIM]
        ref[0, i0, i1 + 1] = yt[HEAD_DIM:]

    def put_vt(ref, g0, y, n_tiles):
        yt = y.T.astype(BF16)
        tk = tm // n_tiles
        for g in range(2):
            for t in range(n_tiles):
                ref[0, g0 + g, t, 0:HEAD_DIM, :] = yt[HEAD_DIM * g:HEAD_DIM * (g + 1), tk * t:tk * (t + 1)]
                ref[0, g0 + g, t, HEAD_DIM:HEAD_DIM + ONES_ROWS, :] = ones_pad(tk)

    for ch in range(2):
        for half, y in enumerate(pairs(C_AQ + 256 * ch)):
            put_qt(qa_ref, ch * 2 + half, 0, norm_pair(y, 0, QK_SCALE))
        for half, y in enumerate(pairs(C_AK + 256 * ch)):
            y = norm_pair(y, 1).astype(BF16)
            ka_ref[0, ch * 2 + half, 0] = y[:, :HEAD_DIM]
            ka_ref[0, ch * 2 + half, 1] = y[:, HEAD_DIM:]
        for half, y in enumerate(pairs(C_AV + 256 * ch)):
            va_ref[0, ch * 2 + half, 0, 0:A_VDIM, :] = y.T.astype(BF16)
            va_ref[0, ch * 2 + half, 0, A_VDIM:A_VDIM + ONES_ROWS, :] = ones_pad(tm)
        for half, y in enumerate(pairs(C_BQ + 256 * ch)):
            put_qt(qb_ref, ch, 2 * half, norm_pair(y, 2, QK_SCALE))
        for half, y in enumerate(pairs(C_CQ + 256 * ch)):
            put_qt(qc_ref, ch, 2 * half, norm_pair(y, 4, QK_SCALE))

    def put_k(ref, y):
        y = y.astype(BF16)
        ref[0, 0] = y[:, :HEAD_DIM]
        ref[0, 1] = y[:, HEAD_DIM:]

    yk, yv = pairs(C_BKV)
    put_k(kb_ref, norm_pair(yk, 3))
    put_vt(vb_ref, 0, yv, tm // T_SWA)
    y0, y1 = pairs(C_CKV)
    for kv, y in enumerate((y0, y1)):
        tkv_ref[0, kv, 0] = y[:, :HEAD_DIM]
        tkv_ref[0, kv, 1] = y[:, HEAD_DIM:]
    yk, yv = pairs(C_CKV + 256)
    put_k(ksel_ref, norm_pair(yk, 6))
    put_vt(vsel_ref, 0, yv, 1)
    yk, yv = pairs(C_CKV + 512)
    put_k(kwin_ref, norm_pair(yk, 7))
    put_vt(vwin_ref, 0, yv, 1)

    gc_ref[0] = jax.nn.sigmoid(mm(C_CG, CG_PAD))
    for ch in range(3 * D_MODEL // 256):
        gm_ref[0, :, 256 * ch:256 * (ch + 1)] = jax.nn.sigmoid(mm(C_MG + 256 * ch, 256))


def _inproj(x, nrm, w, gains2):
    B, S, _ = x.shape
    tm = T_ROW
    nt = S // tm
    sd = jax.ShapeDtypeStruct
    qt = lambda a, b: sd((B, a, b, HEAD_DIM, S), BF16)
    kk = lambda n: sd((B, n, S, HEAD_DIM), BF16)
    vt = lambda n, dv, tk: sd((B, n, S // tk, dv + ONES_ROWS, tk), BF16)
    out_shape = (
        qt(A_HEADS, 2), sd((B, A_HEADS, 2, S, HEAD_DIM), BF16), vt(A_HEADS, A_VDIM, tm),
        qt(KV_GROUPS, GROUP), kk(KV_GROUPS), vt(KV_GROUPS, HEAD_DIM, T_SWA),
        qt(KV_GROUPS, GROUP),
        sd((B, 2, KV_GROUPS, S, HEAD_DIM), F32),
        kk(KV_GROUPS), vt(KV_GROUPS, HEAD_DIM, tm), kk(KV_GROUPS), vt(KV_GROUPS, HEAD_DIM, tm),
        sd((B, S, CG_PAD), F32),
        sd((B, S, 3 * D_MODEL), F32),
    )
    s_qt = lambda a, b: pl.BlockSpec((1, a, b, HEAD_DIM, tm), lambda b_, i: (b_, 0, 0, 0, i))
    s_k = lambda n: pl.BlockSpec((1, n, tm, HEAD_DIM), lambda b_, i: (b_, 0, i, 0))
    s_k5 = lambda a, c: pl.BlockSpec((1, a, c, tm, HEAD_DIM), lambda b_, i: (b_, 0, 0, i, 0))
    s_vt = lambda n, dv, tk: pl.BlockSpec((1, n, tm // tk, dv + ONES_ROWS, tk), lambda b_, i: (b_, 0, i, 0, 0))
    row = lambda d: pl.BlockSpec((1, tm, d), lambda b_, i: (b_, i, 0))
    out_specs = (
        s_qt(A_HEADS, 2), s_k5(A_HEADS, 2), s_vt(A_HEADS, A_VDIM, tm),
        s_qt(KV_GROUPS, GROUP), s_k(KV_GROUPS), s_vt(KV_GROUPS, HEAD_DIM, T_SWA),
        s_qt(KV_GROUPS, GROUP), s_k5(2, KV_GROUPS),
        s_k(KV_GROUPS), s_vt(KV_GROUPS, HEAD_DIM, tm), s_k(KV_GROUPS), s_vt(KV_GROUPS, HEAD_DIM, tm),
        row(CG_PAD), row(3 * D_MODEL),
    )
    return pl.pallas_call(
        _inproj_kernel,
        grid=(B, nt),
        in_specs=[
            row(D_MODEL),
            pl.BlockSpec((1, D_MODEL), lambda b_, i: (0, 0)),
            pl.BlockSpec((D_MODEL, IN_COLS_PAD), lambda b_, i: (0, 0), pipeline_mode=pl.Buffered(1)),
            pl.BlockSpec((8, LANES), lambda b_, i: (0, 0)),
        ],
        out_specs=out_specs,
        out_shape=out_shape,
        compiler_params=_params(2),
        name="inproj",
    )(x, nrm, w, gains2)


def _compress_kernel(t_ref, pos_ref, w1_ref, w2_ref, gain_ref, o_ref, ot_ref):
    t = t_ref[0, 0, 0]
    lo = (t + pos_ref[0, 0:1, :]).astype(BF16)
    hi = (t + pos_ref[0, 1:2, :]).astype(BF16)
    v = jnp.dot(lo, w1_ref[0, 0], preferred_element_type=F32)
    u = jnp.dot(hi, w1_ref[0, 1], preferred_element_type=F32)
    pre = v + pltpu.roll(u, NC_PAD - 1, 0)
    hcur = jax.nn.gelu(pre).astype(BF16)
    out = jnp.dot(hcur, w2_ref[0], preferred_element_type=F32)
    o64 = out[:, :HEAD_DIM]
    is_key = pl.program_id(1) == 0
    o_ref[0, 0, 0] = jnp.where(is_key, _rms(o64, gain_ref[...]), o64).astype(BF16)
    ot_ref[0, 0, 0] = out.T[:HEAD_DIM].astype(BF16)


def _compress(tkv16, pos, w1, w2p, gain):
    B = tkv16.shape[0]
    half = CMP_BLOCK // 2 * HEAD_DIM
    return pl.pallas_call(
        _compress_kernel,
        grid=(B, 2, KV_GROUPS),
        in_specs=[
            pl.BlockSpec((1, 1, 1, NC_PAD, half), lambda b, kv, g: (b, kv, g, 0, 0)),
            pl.BlockSpec((1, 2, half), lambda b, kv, g: (kv, 0, 0)),
            pl.BlockSpec((1, 2, half, CMP_HIDDEN), lambda b, kv, g: (kv, 0, 0, 0)),
            pl.BlockSpec((1, CMP_HIDDEN, LANES), lambda b, kv, g: (kv, 0, 0)),
            pl.BlockSpec((1, HEAD_DIM), lambda b, kv, g: (0, 0)),
        ],
        out_specs=(pl.BlockSpec((1, 1, 1, NC_PAD, HEAD_DIM), lambda b, kv, g: (b, kv, g, 0, 0)),
                   pl.BlockSpec((1, 1, 1, HEAD_DIM, NC_PAD), lambda b, kv, g: (b, kv, g, 0, 0))),
        out_shape=(jax.ShapeDtypeStruct((B, 2, KV_GROUPS, NC_PAD, HEAD_DIM), BF16),
                   jax.ShapeDtypeStruct((B, 2, KV_GROUPS, HEAD_DIM, NC_PAD), BF16)),
        compiler_params=_params(3),
        name="compress",
    )(tkv16, pos, w1, w2p, gain)


def _cmp_attn_kernel(qt_ref, kc_ref, vct_ref, ovl_ref, o_ref, sel_ref, *, T, n_cmp):
    R = GROUP
    I = pl.program_id(2)
    kc = kc_ref[0, 0, 0]
    vct = vct_ref[0, 0, 0]
    c_idx = lax.broadcasted_iota(jnp.int32, (NC_PAD, T), 0)
    t_pos = I * T + lax.broadcasted_iota(jnp.int32, (NC_PAD, T), 1)
    cmask = (c_idx * CMP_STRIDE + (CMP_BLOCK - 1) <= t_pos) & (c_idx < n_cmp)
    outs = []
    psum = jnp.zeros((NC_PAD, T), F32)
    for r in range(R):
        s = jnp.dot(kc, qt_ref[0, 0, r], preferred_element_type=F32)
        s = jnp.where(cmask, s, NEG_INF)
        m = jnp.max(s, axis=0, keepdims=True)
        e = jnp.where(cmask, jnp.exp(s - m), 0.0)
        l = jnp.sum(e, axis=0, keepdims=True)
        p = e / jnp.where(l > 0.0, l, 1.0)
        outs.append(jnp.dot(vct, p.astype(BF16), preferred_element_type=F32))
        psum = psum + p
    o_ref[0] = jnp.concatenate(outs, axis=0).T

    p_hi = psum.astype(BF16)
    p_lo = (psum - p_hi.astype(F32)).astype(BF16)
    ovl = ovl_ref[...]
    imp = jnp.dot(ovl, p_hi, preferred_element_type=F32) + jnp.dot(ovl, p_lo, preferred_element_type=F32)
    j_idx = lax.broadcasted_iota(jnp.int32, (N_SEL_BLK, T), 0)
    tq = I * T + lax.broadcasted_iota(jnp.int32, (N_SEL_BLK, T), 1)
    cur = tq // SEL_BLOCK
    valid = j_idx * SEL_BLOCK <= tq
    forced = (j_idx == 0) | (j_idx == cur) | (j_idx == cur - 1)
    score = jnp.where(valid, imp + jnp.where(forced, FORCE_BONUS, 0.0), NEG_INF)
    rank = jnp.zeros((N_SEL_BLK, T), jnp.int32)
    for jp in range(N_SEL_BLK):
        row = score[jp:jp + 1, :]
        beats = (row > score) | ((row == score) & (j_idx > jp))
        rank = rank + beats.astype(jnp.int32)
    sel_ref[0, 0] = jnp.where(rank < N_SELECT, 0.0, NEG_INF)


def _cmp_attn(qct, kc, vct, ovl, n_cmp):
    B, G, R, _, S = qct.shape
    T = T_ROW
    return pl.pallas_call(
        functools.partial(_cmp_attn_kernel, T=T, n_cmp=n_cmp),
        grid=(B, G, S // T),
        in_specs=[
            pl.BlockSpec((1, 1, R, HEAD_DIM, T), lambda b, g, i: (b, g, 0, 0, i)),
            pl.BlockSpec((1, 1, 1, NC_PAD, HEAD_DIM), lambda b, g, i: (b, 0, g, 0, 0)),
            pl.BlockSpec((1, 1, 1, HEAD_DIM, NC_PAD), lambda b, g, i: (b, 1, g, 0, 0)),
            pl.BlockSpec((N_SEL_BLK, NC_PAD), lambda b, g, i: (0, 0)),
        ],
        out_specs=(
            pl.BlockSpec((1, T, R * HEAD_DIM), lambda b, g, i: (b, i, g)),
            pl.BlockSpec((1, 1, N_SEL_BLK, T), lambda b, g, i: (b, g, 0, i)),
        ),
        out_shape=(
            jax.ShapeDtypeStruct((B, S, G * R * HEAD_DIM), F32),
            jax.ShapeDtypeStruct((B, G, N_SEL_BLK, S), F32),
        ),
        compiler_params=_params(3),
        name="cmp_attn",
    )(qct, kc, vct, ovl)


def _attn_kernel(*refs, R, T, Dv, nrel, far, k_per_r, shared_bias, mode, has_sel, lam_init):
    it = iter(refs)
    qt_ref, k_ref, vt_ref, bt_ref = next(it), next(it), next(it), next(it)
    far_ref = next(it) if far else None
    sel_ref = next(it) if has_sel else None
    sink_ref = next(it) if mode == "sink" else None
    dl_ref = next(it) if mode == "diff" else None
    sub_ref = next(it) if mode == "diff" else None
    o_ref, m_ref, acc_ref = next(it), next(it), next(it)

    I = pl.program_id(2)
    m_ref[...] = jnp.full(m_ref.shape, NEG_INF, F32)
    acc_ref[...] = jnp.zeros(acc_ref.shape, F32)
    blk_per_tile = T // SEL_BLOCK

    def step(J, bias_of):
        off = pl.multiple_of(J * T, T)
        vt = vt_ref[0, 0, J]
        if not k_per_r:
            kt = k_ref[0, 0, pl.ds(off, T), :]
        if has_sel:
            mask = jnp.concatenate(
                [jnp.broadcast_to(sel_ref[0, 0, pl.ds(J * blk_per_tile + i, 1), :], (SEL_BLOCK, T))
                 for i in range(blk_per_tile)], axis=0)
        scores = []
        for r in range(R):
            if k_per_r:
                kt = k_ref[0, 0, r, pl.ds(off, T), :]
            scores.append(jnp.dot(kt, qt_ref[0, 0, r], preferred_element_type=F32))
        probs, alphas = [], []
        for r in range(R):
            s = scores[r] + bias_of(r)
            if has_sel:
                s = s + mask
            m_prev = m_ref[r]
            m_new = jnp.maximum(m_prev, jnp.max(s, axis=0, keepdims=True))
            alphas.append(jnp.exp(m_prev - m_new))
            probs.append(jnp.exp(s - m_new).astype(BF16))
            m_ref[r] = m_new
        pvs = [jnp.dot(vt, p, preferred_element_type=F32) for p in probs]
        for r in range(R):
            acc_ref[r] = alphas[r] * acc_ref[r] + pvs[r]

    bias_r = (lambda r: 0) if shared_bias else (lambda r: r)
    if far:
        def far_body(J, carry):
            step(J, lambda r: far_ref[0, bias_r(r)])
            return carry
        lax.fori_loop(0, jnp.maximum(I - (nrel - 1), 0), far_body, 0)
    for rel in reversed(range(nrel)):
        if rel == 0:
            step(I, lambda r: bt_ref[0, 0, bias_r(r)])
        else:
            @pl.when(I >= rel)
            def _(rel=rel):
                step(I - rel, lambda r: bt_ref[0, rel, bias_r(r)])

    outs = []
    for r in range(R):
        acc = acc_ref[r]
        num, l = acc[:Dv], acc[Dv:Dv + 1]
        if mode == "sink":
            m = m_ref[r]
            sk = sink_ref[0, r]
            m_f = jnp.maximum(m, sk)
            w = jnp.exp(m - m_f)
            outs.append(num * w / (l * w + jnp.exp(sk - m_f)))
        else:
            outs.append(num / l)
    if mode == "diff":
        dl = dl_ref[...]
        lam = (jnp.exp(jnp.sum(dl[0:1] * dl[1:2], keepdims=True))
               - jnp.exp(jnp.sum(dl[2:3] * dl[3:4], keepdims=True)) + lam_init)
        o = (outs[0] - lam * outs[1]).T
        o_ref[0] = _rms(o, sub_ref[...]) * (1.0 - lam_init)
    else:
        o_ref[0] = jnp.concatenate(outs, axis=0).T


def _attn(qt, k, vt, bias_tab, *, mode="plain", far_bias=None, sel=None,
          sink=None, dl=None, subln=None, lam_init=0.0, name="attn"):
    B, G, R, _, S = qt.shape
    T = vt.shape[-1]
    Dv = vt.shape[-2] - ONES_ROWS
    nrel, bias_heads = bias_tab.shape[1], bias_tab.shape[2]
    k_per_r = k.ndim == 5
    far = far_bias is not None
    has_sel = sel is not None
    ins = [qt, k, vt, bias_tab]
    in_specs = [
        pl.BlockSpec((1, 1, R, HEAD_DIM, T), lambda b, g, i: (b, g, 0, 0, i)),
        (pl.BlockSpec((1, 1, R, S, HEAD_DIM), lambda b, g, i: (b, g, 0, 0, 0)) if k_per_r
         else pl.BlockSpec((1, 1, S, HEAD_DIM), lambda b, g, i: (b, g, 0, 0))),
        pl.BlockSpec((1, 1, S // T, Dv + ONES_ROWS, T), lambda b, g, i: (b, g, 0, 0, 0)),
        pl.BlockSpec((1, nrel, bias_heads, T, T), lambda b, g, i: (g, 0, 0, 0, 0)),
    ]
    if far:
        ins.append(far_bias)
        in_specs.append(pl.BlockSpec((1, bias_heads, 1, T), lambda b, g, i: (g, 0, 0, 0)))
    if has_sel:
        ins.append(sel)
        in_specs.append(pl.BlockSpec((1, 1, N_SEL_BLK, T), lambda b, g, i: (b, g, 0, i)))
    if mode == "sink":
        ins.append(sink)
        in_specs.append(pl.BlockSpec((1, R, 1, 1), lambda b, g, i: (g, 0, 0, 0)))
    if mode == "diff":
        ins += [dl, subln]
        in_specs += [pl.BlockSpec((4, HEAD_DIM), lambda b, g, i: (0, 0)),
                     pl.BlockSpec((1, A_VDIM), lambda b, g, i: (0, 0))]
        out_w = Dv
    else:
        out_w = R * Dv
    kern = functools.partial(_attn_kernel, R=R, T=T, Dv=Dv, nrel=nrel, far=far, k_per_r=k_per_r,
                             shared_bias=bias_heads == 1, mode=mode, has_sel=has_sel, lam_init=lam_init)
    return pl.pallas_call(
        kern,
        grid=(B, G, S // T),
        in_specs=in_specs,
        out_specs=pl.BlockSpec((1, T, out_w), lambda b, g, i: (b, i, g)),
        out_shape=jax.ShapeDtypeStruct((B, S, G * out_w), F32),
        scratch_shapes=[pltpu.VMEM((R, 1, T), F32), pltpu.VMEM((R, Dv + ONES_ROWS, T), F32)],
        compiler_params=_params(3),
        name=name,
    )(*ins)


def _bucket_np(dist):
    n = np.maximum(dist, 0)
    max_exact = NUM_BUCKETS // 2
    nf = np.maximum(n, 1).astype(np.float32)
    large = max_exact + (np.log(nf / max_exact) / math.log(MAX_DISTANCE / max_exact)
                         * (NUM_BUCKETS - max_exact)).astype(np.int32)
    large = np.minimum(large, NUM_BUCKETS - 1)
    return np.where(n < max_exact, n, large)


def _bucket_starts():
    b = _bucket_np(np.arange(4 * MAX_DISTANCE))
    assert (np.diff(b) >= 0).all() and b[-1] == NUM_BUCKETS - 1
    return [int(np.argmax(b >= i)) for i in range(NUM_BUCKETS)]


def _bias_kernel(tab_ref, o_ref, *, T, nrel, window, starts):
    h = pl.program_id(0)
    key = lax.broadcasted_iota(jnp.int32, (T, T), 0)
    qry = lax.broadcasted_iota(jnp.int32, (T, T), 1)
    for rel in range(nrel):
        dist = rel * T + qry - key
        val = jnp.full((T, T), tab_ref[0, h], F32)
        for b in range(1, NUM_BUCKETS):
            val = jnp.where(dist >= starts[b], tab_ref[b, h], val)
        allowed = dist >= 0
        if window is not None:
            allowed = allowed & (dist < window)
        o_ref[0, rel, 0] = jnp.where(allowed, val, NEG_INF)


def _bias_tiles(tab, T, nrel, window, R):
    H = tab.shape[1]
    return pl.pallas_call(
        functools.partial(_bias_kernel, T=T, nrel=nrel, window=window, starts=_bucket_starts()),
        grid=(H,),
        in_specs=[pl.BlockSpec(memory_space=pltpu.SMEM)],
        out_specs=pl.BlockSpec((1, nrel, 1, T, T), lambda h: (h // R, 0, h % R, 0, 0)),
        out_shape=jax.ShapeDtypeStruct((H // R, nrel, R, T, T), F32),
        compiler_params=_params(1),
        name="bias_tiles",
    )(tab)


def _far_rows(tab, T, R):
    H = tab.shape[1]
    return jnp.broadcast_to(tab[NUM_BUCKETS - 1].reshape(H // R, R, 1, 1), (H // R, R, 1, T)).astype(F32)


def _merge_kernel(x_ref, oa_ref, ob_ref, oc_ref, os_ref, ow_ref, gc_ref, gm_ref, ex_ref, wb_ref, wo_ref, o_ref):
    gc = gc_ref[...]
    g1 = gc.astype(BF16)
    r1 = gc - g1.astype(F32)
    g2 = r1.astype(BF16)
    g3 = (r1 - g2.astype(F32)).astype(BF16)
    ex = ex_ref[...]
    gx = (jnp.dot(g1, ex, preferred_element_type=F32) + jnp.dot(g2, ex, preferred_element_type=F32)
          + jnp.dot(g3, ex, preferred_element_type=F32))
    oc = (gx[:, 0:MIX_WIDTH] * oc_ref[...] + gx[:, MIX_WIDTH:2 * MIX_WIDTH] * os_ref[...]
          + gx[:, 2 * MIX_WIDTH:3 * MIX_WIDTH] * ow_ref[...])
    z = None
    for n, br in enumerate((oa_ref[...], ob_ref[...], oc)):
        y = jnp.dot(br.astype(BF16), wb_ref[n], preferred_element_type=F32)
        t = gm_ref[:, n * D_MODEL:(n + 1) * D_MODEL] * y
        z = t if z is None else z + t
    o_ref[...] = x_ref[...] + jnp.dot(z.astype(BF16), wo_ref[...], preferred_element_type=F32)


def _merge(x2, oa, ob, oc, osel, ow, gc, gm, expand_g, wb, wo, tm=256):
    Tn = x2.shape[0]
    row = lambda d: pl.BlockSpec((tm, d), lambda i: (i, 0))
    return pl.pallas_call(
        _merge_kernel,
        grid=(Tn // tm,),
        in_specs=[row(D_MODEL), row(MIX_WIDTH), row(MIX_WIDTH), row(MIX_WIDTH), row(MIX_WIDTH), row(MIX_WIDTH),
                  row(CG_PAD), row(3 * D_MODEL),
                  pl.BlockSpec((CG_PAD, 3 * MIX_WIDTH), lambda i: (0, 0)),
                  pl.BlockSpec((3, MIX_WIDTH, D_MODEL), lambda i: (0, 0, 0)),
                  pl.BlockSpec((D_MODEL, D_MODEL), lambda i: (0, 0))],
        out_specs=row(D_MODEL),
        out_shape=jax.ShapeDtypeStruct((Tn, D_MODEL), F32),
        compiler_params=_params(1),
        name="merge",
    )(x2, oa, ob, oc, osel, ow, gc, gm, expand_g, wb, wo)


def _mlp_kernel(x_ref, nrm_ref, wu_ref, wd_ref, o_ref, h_ref, acc_ref):
    f = pl.program_id(1)

    @pl.when(f == 0)
    def _():
        h_ref[...] = _rms(x_ref[...], nrm_ref[...]).astype(BF16)
        acc_ref[...] = jnp.zeros(acc_ref.shape, F32)

    u = jnp.dot(h_ref[...], wu_ref[...], preferred_element_type=F32)
    u = jnp.square(jnp.maximum(u, 0.0)).astype(BF16)
    acc_ref[...] += jnp.dot(u, wd_ref[...], preferred_element_type=F32)

    @pl.when(f == pl.num_programs(1) - 1)
    def _():
        o_ref[...] = x_ref[...] + acc_ref[...]


def _mlp(x2, nrm, wu, wd, tm=1024, tf=512):
    Tn = x2.shape[0]
    return pl.pallas_call(
        _mlp_kernel,
        grid=(Tn // tm, D_FF // tf),
        in_specs=[pl.BlockSpec((tm, D_MODEL), lambda i, f: (i, 0)),
                  pl.BlockSpec((1, D_MODEL), lambda i, f: (0, 0)),
                  pl.BlockSpec((D_MODEL, tf), lambda i, f: (0, f)),
                  pl.BlockSpec((tf, D_MODEL), lambda i, f: (f, 0))],
        out_specs=pl.BlockSpec((tm, D_MODEL), lambda i, f: (i, 0)),
        out_shape=jax.ShapeDtypeStruct((Tn, D_MODEL), F32),
        scratch_shapes=[pltpu.VMEM((tm, D_MODEL), BF16), pltpu.VMEM((tm, D_MODEL), F32)],
        compiler_params=_params(2),
        name="mlp",
    )(x2, nrm, wu, wd)


def _overlap(n_cmp):
    c_start = np.arange(NC_PAD) * CMP_STRIDE
    j_start = np.arange(N_SEL_BLK) * SEL_BLOCK
    ov = ((c_start[None, :] < j_start[:, None] + SEL_BLOCK) & (c_start[None, :] + CMP_BLOCK > j_start[:, None])
          & (np.arange(NC_PAD)[None, :] < n_cmp))
    return jnp.asarray(ov.astype(np.float32), BF16)


def _gate_expand():
    e = np.zeros((CG_PAD, 3 * MIX_WIDTH), np.float32)
    for h in range(C_HEADS):
        for j in range(3):
            e[h * 3 + j, j * MIX_WIDTH + h * HEAD_DIM:j * MIX_WIDTH + (h + 1) * HEAD_DIM] = 1.0
    return jnp.asarray(e, BF16)


def kernel(x, w_in, qk_gain, diff_lambda, diff_subln, sinks, cmp_pos, cmp_w1, cmp_w2,
           w_branch, w_out, norm_mix, norm_mlp, w_up, w_down, rel_bias):
    B, S, _ = x.shape
    depth = w_in.shape[0]
    n_cmp = (S - CMP_BLOCK) // CMP_STRIDE + 1
    assert S % T_ROW == 0 and S // CMP_STRIDE == NC_PAD and S // SEL_BLOCK == N_SEL_BLK
    half = CMP_BLOCK // 2 * HEAD_DIM

    w_in_p = jnp.concatenate(
        [w_in[:, :, :C_CG], jnp.pad(w_in[:, :, C_CG:C_CG + N_CG], ((0, 0), (0, 0), (0, CG_PAD - N_CG))),
         w_in[:, :, C_CG + N_CG:]], axis=-1).astype(BF16)
    gains2 = jnp.concatenate([qk_gain, qk_gain], axis=-1)
    w1 = cmp_w1.astype(BF16).reshape(depth, 2, 2, half, CMP_HIDDEN)
    w2p = jnp.pad(cmp_w2, ((0, 0), (0, 0), (0, 0), (0, LANES - HEAD_DIM))).astype(BF16)
    pos = cmp_pos.reshape(depth, 2, 2, half)
    wb = w_branch.astype(BF16)
    wo = w_out.astype(BF16)
    wu = w_up.astype(BF16)
    wd = w_down.astype(BF16)

    bias_a = rel_bias[:, :A_HEADS]
    bias_b = rel_bias[:, A_HEADS:A_HEADS + B_HEADS]
    bias_c = rel_bias[:, A_HEADS + B_HEADS:]
    bt_a = _bias_tiles(bias_a, T_ROW, 2, None, 1)
    far_a = _far_rows(bias_a, T_ROW, 1)
    bt_b = _bias_tiles(bias_b, T_SWA, B_WINDOW // T_SWA + 1, B_WINDOW, GROUP)
    bt_sel = _bias_tiles(bias_c, T_ROW, 2, None, GROUP)
    far_sel = _far_rows(bias_c, T_ROW, GROUP)
    bt_win = _bias_tiles(bias_c, T_ROW, C_WINDOW // T_ROW + 1, C_WINDOW, GROUP)
    ovl = _overlap(n_cmp)
    expand_g = _gate_expand()

    for layer in range(depth):
        lam_init = 0.8 - 0.6 * math.exp(-0.3 * layer)
        (qat, ka, vat, qbt, kb, vbt, qct, tkv, ksel, vselt, kwin, vwint, gc, gm) = _inproj(
            x, norm_mix[layer][None], w_in_p[layer], gains2[layer])
        oa = _attn(qat, ka, vat, bt_a, mode="diff", far_bias=far_a, dl=diff_lambda[layer],
                   subln=diff_subln[layer][None], lam_init=lam_init, name="attn_diff")
        ob = _attn(qbt, kb, vbt, bt_b, mode="sink",
                   sink=sinks[layer].reshape(KV_GROUPS, GROUP, 1, 1), name="attn_swa")
        kc, vct = _compress(tkv.reshape(B, 2, KV_GROUPS, NC_PAD, half), pos[layer],
                            w1[layer], w2p[layer], qk_gain[layer][5:6])
        ocmp, sel = _cmp_attn(qct, kc, vct, ovl, n_cmp)
        osel = _attn(qct, ksel, vselt, bt_sel, far_bias=far_sel, sel=sel, name="attn_sel")
        owin = _attn(qct, kwin, vwint, bt_win, name="attn_win")
        f2 = lambda a: a.reshape(B * S, a.shape[-1])
        x2 = _merge(f2(x), f2(oa), f2(ob), f2(ocmp), f2(osel), f2(owin), f2(gc), f2(gm),
                    expand_g, wb[layer], wo[layer])
        x2 = _mlp(x2, norm_mlp[layer][None], wu[layer], wd[layer])
        x = x2.reshape(B, S, D_MODEL)
    return x
```

```python
import functools
import math

import numpy as np
import jax
import jax.numpy as jnp
from jax import lax
from jax.experimental import pallas as pl
from jax.experimental.pallas import tpu as pltpu

F32 = jnp.float32
BF16 = jnp.bfloat16

D_MODEL = 1024
HEAD_DIM = 64
A_HEADS = 4
A_VDIM = 128
B_HEADS = 8
B_WINDOW = 128
C_HEADS = 8
KV_GROUPS = 2
GROUP = 4
CMP_BLOCK = 32
CMP_STRIDE = 16
CMP_HIDDEN = 256
SEL_BLOCK = 64
N_SELECT = 16
C_WINDOW = 512
MIX_WIDTH = 512
D_FF = 4096
NUM_BUCKETS = 32
MAX_DISTANCE = 128
NEG_INF = -1e30
FORCE_BONUS = 1e4
EPS = 1e-6
QK_SCALE = HEAD_DIM ** -0.5

C_AQ, C_AK, C_AV, C_BQ, C_BKV, C_CQ, C_CKV, C_CG, C_MG = 0, 512, 1024, 1536, 2048, 2304, 2816, 3584, 3712
IN_COLS_PAD = C_MG + 3 * D_MODEL
N_CG = C_HEADS * 3
LANES = 128
CG_PAD = LANES
NC_PAD = 256
N_SEL_BLK = 64
ONES_ROWS = 16

T_ROW = 256
T_SWA = 128

VMEM_LIMIT = 48 * 1024 * 1024


def _rms(x, gain):
    ms = jnp.mean(x * x, axis=-1, keepdims=True)
    return (x * lax.rsqrt(ms + EPS)) * gain


def _params(n_axes):
    return pltpu.CompilerParams(dimension_semantics=("arbitrary",) * n_axes,
                                vmem_limit_bytes=VMEM_LIMIT)


def _inproj_kernel(x_ref, nrm_ref, w_ref, gains_ref,
                   qa_ref, ka_ref, va_ref, qb_ref, kb_ref, vb_ref, qc_ref,
                   tkv_ref, ksel_ref, vsel_ref, kwin_ref, vwin_ref, gc_ref, gm_ref):
    tm = x_ref.shape[1]
    hb = _rms(x_ref[0], nrm_ref[...]).astype(BF16)
    lane = lax.broadcasted_iota(jnp.int32, (tm, LANES), 1)
    first = lane < HEAD_DIM

    def ones_pad(width):
        return jnp.where(lax.broadcasted_iota(jnp.int32, (ONES_ROWS, width), 0) == 0, 1.0, 0.0).astype(BF16)

    def mm(c0, n):
        return jnp.dot(hb, w_ref[:, c0:c0 + n], preferred_element_type=F32)

    def norm_pair(y, gain_idx, scale=None):
        sq = y * y
        ms0 = jnp.sum(jnp.where(first, sq, 0.0), axis=-1, keepdims=True) * (1.0 / HEAD_DIM)
        ms1 = jnp.sum(jnp.where(first, 0.0, sq), axis=-1, keepdims=True) * (1.0 / HEAD_DIM)
        inv = jnp.where(first, lax.rsqrt(ms0 + EPS), lax.rsqrt(ms1 + EPS))
        out = (y * inv) * gains_ref[gain_idx:gain_idx + 1, :]
        return out if scale is None else out * scale

    def pairs(c0):
        r = mm(c0, 256)
        return r[:, :LANES], r[:, LANES:]

    def put_qt(ref, i0, i1, y):
        yt = y.T.astype(BF16)
        ref[0, i0, i1] = yt[:HEAD_DIM]
        ref[0, i0, i1 + 1] = yt[HEAD_DIM:]

    def put_vt(ref, g0, y, n_tiles):
        yt = y.T.astype(BF16)
        tk = tm // n_tiles
        for g in range(2):
            for t in range(n_tiles):
                ref[0, g0 + g, t, 0:HEAD_DIM, :] = yt[HEAD_DIM * g:HEAD_DIM * (g + 1), tk * t:tk * (t + 1)]
                ref[0, g0 + g, t, HEAD_DIM:HEAD_DIM + ONES_ROWS, :] = ones_pad(tk)

    for ch in range(2):
        for half, y in enumerate(pairs(C_AQ + 256 * ch)):
            put_qt(qa_ref, ch * 2 + half, 0, norm_pair(y, 0, QK_SCALE))
        for half, y in enumerate(pairs(C_AK + 256 * ch)):
            y = norm_pair(y, 1).astype(BF16)
            ka_ref[0, ch * 2 + half, 0] = y[:, :HEAD_DIM]
            ka_ref[0, ch * 2 + half, 1] = y[:, HEAD_DIM:]
        for half, y in enumerate(pairs(C_AV + 256 * ch)):
            va_ref[0, ch * 2 + half, 0, 0:A_VDIM, :] = y.T.astype(BF16)
            va_ref[0, ch * 2 + half, 0, A_VDIM:A_VDIM + ONES_ROWS, :] = ones_pad(tm)
        for half, y in enumerate(pairs(C_BQ + 256 * ch)):
            put_qt(qb_ref, ch, 2 * half, norm_pair(y, 2, QK_SCALE))
        for half, y in enumerate(pairs(C_CQ + 256 * ch)):
            put_qt(qc_ref, ch, 2 * half, norm_pair(y, 4, QK_SCALE))

    def put_k(ref, y):
        y = y.astype(BF16)
        ref[0, 0] = y[:, :HEAD_DIM]
        ref[0, 1] = y[:, HEAD_DIM:]

    yk, yv = pairs(C_BKV)
    put_k(kb_ref, norm_pair(yk, 3))
    put_vt(vb_ref, 0, yv, tm // T_SWA)
    y0, y1 = pairs(C_CKV)
    for kv, y in enumerate((y0, y1)):
        tkv_ref[0, kv, 0] = y[:, :HEAD_DIM]
        tkv_ref[0, kv, 1] = y[:, HEAD_DIM:]
    yk, yv = pairs(C_CKV + 256)
    put_k(ksel_ref, norm_pair(yk, 6))
    put_vt(vsel_ref, 0, yv, 1)
    yk, yv = pairs(C_CKV + 512)
    put_k(kwin_ref, norm_pair(yk, 7))
    put_vt(vwin_ref, 0, yv, 1)

    gc_ref[0] = jax.nn.sigmoid(mm(C_CG, CG_PAD))
    for ch in range(3 * D_MODEL // 256):
        gm_ref[0, :, 256 * ch:256 * (ch + 1)] = jax.nn.sigmoid(mm(C_MG + 256 * ch, 256))


def _inproj(x, nrm, w, gains2):
    B, S, _ = x.shape
    tm = T_ROW
    nt = S // tm
    sd = jax.ShapeDtypeStruct
    qt = lambda a, b: sd((B, a, b, HEAD_DIM, S), BF16)
    kk = lambda n: sd((B, n, S, HEAD_DIM), BF16)
    vt = lambda n, dv, tk: sd((B, n, S // tk, dv + ONES_ROWS, tk), BF16)
    out_shape = (
        qt(A_HEADS, 2), sd((B, A_HEADS, 2, S, HEAD_DIM), BF16), vt(A_HEADS, A_VDIM, tm),
        qt(KV_GROUPS, GROUP), kk(KV_GROUPS), vt(KV_GROUPS, HEAD_DIM, T_SWA),
        qt(KV_GROUPS, GROUP),
        sd((B, 2, KV_GROUPS, S, HEAD_DIM), F32),
        kk(KV_GROUPS), vt(KV_GROUPS, HEAD_DIM, tm), kk(KV_GROUPS), vt(KV_GROUPS, HEAD_DIM, tm),
        sd((B, S, CG_PAD), F32),
        sd((B, S, 3 * D_MODEL), F32),
    )
    s_qt = lambda a, b: pl.BlockSpec((1, a, b, HEAD_DIM, tm), lambda b_, i: (b_, 0, 0, 0, i))
    s_k = lambda n: pl.BlockSpec((1, n, tm, HEAD_DIM), lambda b_, i: (b_, 0, i, 0))
    s_k5 = lambda a, c: pl.BlockSpec((1, a, c, tm, HEAD_DIM), lambda b_, i: (b_, 0, 0, i, 0))
    s_vt = lambda n, dv, tk: pl.BlockSpec((1, n, tm // tk, dv + ONES_ROWS, tk), lambda b_, i: (b_, 0, i, 0, 0))
    row = lambda d: pl.BlockSpec((1, tm, d), lambda b_, i: (b_, i, 0))
    out_specs = (
        s_qt(A_HEADS, 2), s_k5(A_HEADS, 2), s_vt(A_HEADS, A_VDIM, tm),
        s_qt(KV_GROUPS, GROUP), s_k(KV_GROUPS), s_vt(KV_GROUPS, HEAD_DIM, T_SWA),
        s_qt(KV_GROUPS, GROUP), s_k5(2, KV_GROUPS),
        s_k(KV_GROUPS), s_vt(KV_GROUPS, HEAD_DIM, tm), s_k(KV_GROUPS), s_vt(KV_GROUPS, HEAD_DIM, tm),
        row(CG_PAD), row(3 * D_MODEL),
    )
    return pl.pallas_call(
        _inproj_kernel,
        grid=(B, nt),
        in_specs=[
            row(D_MODEL),
            pl.BlockSpec((1, D_MODEL), lambda b_, i: (0, 0)),
            pl.BlockSpec((D_MODEL, IN_COLS_PAD), lambda b_, i: (0, 0), pipeline_mode=pl.Buffered(1)),
            pl.BlockSpec((8, LANES), lambda b_, i: (0, 0)),
        ],
        out_specs=out_specs,
        out_shape=out_shape,
        compiler_params=_params(2),
        name="inproj",
    )(x, nrm, w, gains2)


def _compress_kernel(t_ref, pos_ref, w1_ref, w2_ref, gain_ref, o_ref, ot_ref):
    t = t_ref[0, 0, 0]
    lo = (t + pos_ref[0, 0:1, :]).astype(BF16)
    hi = (t + pos_ref[0, 1:2, :]).astype(BF16)
    v = jnp.dot(lo, w1_ref[0, 0], preferred_element_type=F32)
    u = jnp.dot(hi, w1_ref[0, 1], preferred_element_type=F32)
    pre = v + pltpu.roll(u, NC_PAD - 1, 0)
    hcur = jax.nn.gelu(pre).astype(BF16)
    out = jnp.dot(hcur, w2_ref[0], preferred_element_type=F32)
    o64 = out[:, :HEAD_DIM]
    is_key = pl.program_id(1) == 0
    o_ref[0, 0, 0] = jnp.where(is_key, _rms(o64, gain_ref[...]), o64).astype(BF16)
    ot_ref[0, 0, 0] = out.T[:HEAD_DIM].astype(BF16)


def _compress(tkv16, pos, w1, w2p, gain):
    B = tkv16.shape[0]
    half = CMP_BLOCK // 2 * HEAD_DIM
    return pl.pallas_call(
        _compress_kernel,
        grid=(B, 2, KV_GROUPS),
        in_specs=[
            pl.BlockSpec((1, 1, 1, NC_PAD, half), lambda b, kv, g: (b, kv, g, 0, 0)),
            pl.BlockSpec((1, 2, half), lambda b, kv, g: (kv, 0, 0)),
            pl.BlockSpec((1, 2, half, CMP_HIDDEN), lambda b, kv, g: (kv, 0, 0, 0)),
            pl.BlockSpec((1, CMP_HIDDEN, LANES), lambda b, kv, g: (kv, 0, 0)),
            pl.BlockSpec((1, HEAD_DIM), lambda b, kv, g: (0, 0)),
        ],
        out_specs=(pl.BlockSpec((1, 1, 1, NC_PAD, HEAD_DIM), lambda b, kv, g: (b, kv, g, 0, 0)),
                   pl.BlockSpec((1, 1, 1, HEAD_DIM, NC_PAD), lambda b, kv, g: (b, kv, g, 0, 0))),
        out_shape=(jax.ShapeDtypeStruct((B, 2, KV_GROUPS, NC_PAD, HEAD_DIM), BF16),
                   jax.ShapeDtypeStruct((B, 2, KV_GROUPS, HEAD_DIM, NC_PAD), BF16)),
        compiler_params=_params(3),
        name="compress",
    )(tkv16, pos, w1, w2p, gain)


def _cmp_attn_kernel(qt_ref, kc_ref, vct_ref, ovl_ref, o_ref, sel_ref, *, T, n_cmp):
    R = GROUP
    I = pl.program_id(2)
    kc = kc_ref[0, 0, 0]
    vct = vct_ref[0, 0, 0]
    c_idx = lax.broadcasted_iota(jnp.int32, (NC_PAD, T), 0)
    t_pos = I * T + lax.broadcasted_iota(jnp.int32, (NC_PAD, T), 1)
    cmask = (c_idx * CMP_STRIDE + (CMP_BLOCK - 1) <= t_pos) & (c_idx < n_cmp)
    outs = []
    psum = jnp.zeros((NC_PAD, T), F32)
    for r in range(R):
        s = jnp.dot(kc, qt_ref[0, 0, r], preferred_element_type=F32)
        s = jnp.where(cmask, s, NEG_INF)
        m = jnp.max(s, axis=0, keepdims=True)
        e = jnp.where(cmask, jnp.exp(s - m), 0.0)
        l = jnp.sum(e, axis=0, keepdims=True)
        p = e / jnp.where(l > 0.0, l, 1.0)
        outs.append(jnp.dot(vct, p.astype(BF16), preferred_element_type=F32))
        psum = psum + p
    o_ref[0] = jnp.concatenate(outs, axis=0).T

    p_hi = psum.astype(BF16)
    p_lo = (psum - p_hi.astype(F32)).astype(BF16)
    ovl = ovl_ref[...]
    imp = jnp.dot(ovl, p_hi, preferred_element_type=F32) + jnp.dot(ovl, p_lo, preferred_element_type=F32)
    j_idx = lax.broadcasted_iota(jnp.int32, (N_SEL_BLK, T), 0)
    tq = I * T + lax.broadcasted_iota(jnp.int32, (N_SEL_BLK, T), 1)
    cur = tq // SEL_BLOCK
    valid = j_idx * SEL_BLOCK <= tq
    forced = (j_idx == 0) | (j_idx == cur) | (j_idx == cur - 1)
    score = jnp.where(valid, imp + jnp.where(forced, FORCE_BONUS, 0.0), NEG_INF)
    rank = jnp.zeros((N_SEL_BLK, T), jnp.int32)
    for jp in range(N_SEL_BLK):
        row = score[jp:jp + 1, :]
        beats = (row > score) | ((row == score) & (j_idx > jp))
        rank = rank + beats.astype(jnp.int32)
    sel_ref[0, 0] = jnp.where(rank < N_SELECT, 0.0, NEG_INF)


def _cmp_attn(qct, kc, vct, ovl, n_cmp):
    B, G, R, _, S = qct.shape
    T = T_ROW
    return pl.pallas_call(
        functools.partial(_cmp_attn_kernel, T=T, n_cmp=n_cmp),
        grid=(B, G, S // T),
        in_specs=[
            pl.BlockSpec((1, 1, R, HEAD_DIM, T), lambda b, g, i: (b, g, 0, 0, i)),
            pl.BlockSpec((1, 1, 1, NC_PAD, HEAD_DIM), lambda b, g, i: (b, 0, g, 0, 0)),
            pl.BlockSpec((1, 1, 1, HEAD_DIM, NC_PAD), lambda b, g, i: (b, 1, g, 0, 0)),
            pl.BlockSpec((N_SEL_BLK, NC_PAD), lambda b, g, i: (0, 0)),
        ],
        out_specs=(
            pl.BlockSpec((1, T, R * HEAD_DIM), lambda b, g, i: (b, i, g)),
            pl.BlockSpec((1, 1, N_SEL_BLK, T), lambda b, g, i: (b, g, 0, i)),
        ),
        out_shape=(
            jax.ShapeDtypeStruct((B, S, G * R * HEAD_DIM), F32),
            jax.ShapeDtypeStruct((B, G, N_SEL_BLK, S), F32),
        ),
        compiler_params=_params(3),
        name="cmp_attn",
    )(qct, kc, vct, ovl)


def _attn_kernel(*refs, R, T, Dv, nrel, far, k_per_r, shared_bias, mode, has_sel, lam_init):
    it = iter(refs)
    qt_ref, k_ref, vt_ref, bt_ref = next(it), next(it), next(it), next(it)
    sel_ref = next(it) if has_sel else None
    sink_ref = next(it) if mode == "sink" else None
    dl_ref = next(it) if mode == "diff" else None
    sub_ref = next(it) if mode == "diff" else None
    o_ref, m_ref, acc_ref, s_ref, p_ref, al_ref = (next(it) for _ in range(6))

    I = pl.program_id(2)
    m_ref[...] = jnp.full(m_ref.shape, NEG_INF, F32)
    acc_ref[...] = jnp.zeros(acc_ref.shape, F32)
    blk_per_tile = T // SEL_BLOCK
    bias_r = (lambda r: 0) if shared_bias else (lambda r: r)
    J0 = 0 if far else jnp.maximum(I - (nrel - 1), 0)

    def rel_of(J):
        return jnp.minimum(I - J, nrel - 1) if far else I - J

    def score_stage(J, slot):
        off = pl.multiple_of(J * T, T)
        for r in range(R):
            kt = k_ref[0, 0, r, pl.ds(off, T), :] if k_per_r else k_ref[0, 0, pl.ds(off, T), :]
            s_ref[slot, r] = jnp.dot(kt, qt_ref[0, 0, r], preferred_element_type=F32)

    def softmax_stage(J, rel, slot):
        if has_sel:
            mask = jnp.concatenate(
                [jnp.broadcast_to(sel_ref[0, 0, pl.ds(J * blk_per_tile + i, 1), :], (SEL_BLOCK, T))
                 for i in range(blk_per_tile)], axis=0)
        for r in range(R):
            s = s_ref[slot, r] + bt_ref[0, rel, bias_r(r)]
            if has_sel:
                s = s + mask
            m_prev = m_ref[r]
            m_new = jnp.maximum(m_prev, jnp.max(s, axis=0, keepdims=True))
            al_ref[slot, r] = jnp.exp(m_prev - m_new)
            p_ref[slot, r] = jnp.exp(s - m_new).astype(BF16)
            m_ref[r] = m_new

    def value_stage(J, slot):
        vt = vt_ref[0, 0, J]
        return [jnp.dot(vt, p_ref[slot, r], preferred_element_type=F32) for r in range(R)]

    def accumulate(slot, pvs):
        for r in range(R):
            acc_ref[r] = al_ref[slot, r] * acc_ref[r] + pvs[r]

    n_off = I - J0

    @pl.when(n_off % 2 == 1)
    def _():
        score_stage(J0, 0)
        softmax_stage(J0, rel_of(J0), 0)
        accumulate(0, value_stage(J0, 0))

    J1 = J0 + n_off % 2

    def half_step(J, slot):
        other = 1 - slot
        pvs = value_stage(jnp.maximum(J - 1, 0), other)
        score_stage(J + 1, other)
        softmax_stage(J, rel_of(J), slot)
        accumulate(other, pvs)

    def pair(u, carry):
        J = J1 + 2 * u
        half_step(J, 0)
        half_step(J + 1, 1)
        return carry

    score_stage(J1, 0)
    p_ref[1] = jnp.zeros(p_ref.shape[1:], BF16)
    al_ref[1] = jnp.ones(al_ref.shape[1:], F32)
    lax.fori_loop(0, n_off // 2, pair, 0)
    pvs = value_stage(jnp.maximum(I - 1, 0), 1)
    softmax_stage(I, 0, 0)
    accumulate(1, pvs)
    accumulate(0, value_stage(I, 0))

    outs = []
    for r in range(R):
        acc = acc_ref[r]
        num, l = acc[:Dv], acc[Dv:Dv + 1]
        if mode == "sink":
            m = m_ref[r]
            sk = sink_ref[0, r]
            m_f = jnp.maximum(m, sk)
            w = jnp.exp(m - m_f)
            outs.append(num * w / (l * w + jnp.exp(sk - m_f)))
        else:
            outs.append(num / l)
    if mode == "diff":
        dl = dl_ref[...]
        lam = (jnp.exp(jnp.sum(dl[0:1] * dl[1:2], keepdims=True))
               - jnp.exp(jnp.sum(dl[2:3] * dl[3:4], keepdims=True)) + lam_init)
        o = (outs[0] - lam * outs[1]).T
        o_ref[0] = _rms(o, sub_ref[...]) * (1.0 - lam_init)
    else:
        o_ref[0] = jnp.concatenate(outs, axis=0).T


def _attn(qt, k, vt, bias_tab, *, mode="plain", far=False, sel=None,
          sink=None, dl=None, subln=None, lam_init=0.0, name="attn"):
    B, G, R, _, S = qt.shape
    T = vt.shape[-1]
    Dv = vt.shape[-2] - ONES_ROWS
    nrel, bias_heads = bias_tab.shape[1], bias_tab.shape[2]
    k_per_r = k.ndim == 5
    has_sel = sel is not None
    ins = [qt, k, vt, bias_tab]
    in_specs = [
        pl.BlockSpec((1, 1, R, HEAD_DIM, T), lambda b, g, i: (b, g, 0, 0, i)),
        (pl.BlockSpec((1, 1, R, S, HEAD_DIM), lambda b, g, i: (b, g, 0, 0, 0)) if k_per_r
         else pl.BlockSpec((1, 1, S, HEAD_DIM), lambda b, g, i: (b, g, 0, 0))),
        pl.BlockSpec((1, 1, S // T, Dv + ONES_ROWS, T), lambda b, g, i: (b, g, 0, 0, 0)),
        pl.BlockSpec((1, nrel, bias_heads, T, T), lambda b, g, i: (g, 0, 0, 0, 0)),
    ]
    if has_sel:
        ins.append(sel)
        in_specs.append(pl.BlockSpec((1, 1, N_SEL_BLK, T), lambda b, g, i: (b, g, 0, i)))
    if mode == "sink":
        ins.append(sink)
        in_specs.append(pl.BlockSpec((1, R, 1, 1), lambda b, g, i: (g, 0, 0, 0)))
    if mode == "diff":
        ins += [dl, subln]
        in_specs += [pl.BlockSpec((4, HEAD_DIM), lambda b, g, i: (0, 0)),
                     pl.BlockSpec((1, A_VDIM), lambda b, g, i: (0, 0))]
        out_w = Dv
    else:
        out_w = R * Dv
    kern = functools.partial(_attn_kernel, R=R, T=T, Dv=Dv, nrel=nrel, far=far, k_per_r=k_per_r,
                             shared_bias=bias_heads == 1, mode=mode, has_sel=has_sel, lam_init=lam_init)
    return pl.pallas_call(
        kern,
        grid=(B, G, S // T),
        in_specs=in_specs,
        out_specs=pl.BlockSpec((1, T, out_w), lambda b, g, i: (b, i, g)),
        out_shape=jax.ShapeDtypeStruct((B, S, G * out_w), F32),
        scratch_shapes=[pltpu.VMEM((R, 1, T), F32), pltpu.VMEM((R, Dv + ONES_ROWS, T), F32),
                        pltpu.VMEM((2, R, T, T), F32), pltpu.VMEM((2, R, T, T), BF16),
                        pltpu.VMEM((2, R, 1, T), F32)],
        compiler_params=_params(3),
        name=name,
    )(*ins)


def _bucket_np(dist):
    n = np.maximum(dist, 0)
    max_exact = NUM_BUCKETS // 2
    nf = np.maximum(n, 1).astype(np.float32)
    large = max_exact + (np.log(nf / max_exact) / math.log(MAX_DISTANCE / max_exact)
                         * (NUM_BUCKETS - max_exact)).astype(np.int32)
    large = np.minimum(large, NUM_BUCKETS - 1)
    return np.where(n < max_exact, n, large)


def _bucket_starts():
    b = _bucket_np(np.arange(4 * MAX_DISTANCE))
    assert (np.diff(b) >= 0).all() and b[-1] == NUM_BUCKETS - 1
    return [int(np.argmax(b >= i)) for i in range(NUM_BUCKETS)]


def _bias_kernel(tab_ref, o_ref, *, T, nrel, window, starts):
    h = pl.program_id(0)
    key = lax.broadcasted_iota(jnp.int32, (T, T), 0)
    qry = lax.broadcasted_iota(jnp.int32, (T, T), 1)
    for rel in range(nrel):
        dist = rel * T + qry - key
        val = jnp.full((T, T), tab_ref[0, h], F32)
        for b in range(1, NUM_BUCKETS):
            val = jnp.where(dist >= starts[b], tab_ref[b, h], val)
        allowed = dist >= 0
        if window is not None:
            allowed = allowed & (dist < window)
        o_ref[0, rel, 0] = jnp.where(allowed, val, NEG_INF)


def _bias_tiles(tab, T, nrel, window, R):
    H = tab.shape[1]
    return pl.pallas_call(
        functools.partial(_bias_kernel, T=T, nrel=nrel, window=window, starts=_bucket_starts()),
        grid=(H,),
        in_specs=[pl.BlockSpec(memory_space=pltpu.SMEM)],
        out_specs=pl.BlockSpec((1, nrel, 1, T, T), lambda h: (h // R, 0, h % R, 0, 0)),
        out_shape=jax.ShapeDtypeStruct((H // R, nrel, R, T, T), F32),
        compiler_params=_params(1),
        name="bias_tiles",
    )(tab)


def _merge_kernel(x_ref, oa_ref, ob_ref, oc_ref, os_ref, ow_ref, gc_ref, gm_ref, ex_ref, wb_ref, wo_ref, o_ref):
    gc = gc_ref[...]
    g1 = gc.astype(BF16)
    r1 = gc - g1.astype(F32)
    g2 = r1.astype(BF16)
    g3 = (r1 - g2.astype(F32)).astype(BF16)
    ex = ex_ref[...]
    gx = (jnp.dot(g1, ex, preferred_element_type=F32) + jnp.dot(g2, ex, preferred_element_type=F32)
          + jnp.dot(g3, ex, preferred_element_type=F32))
    oc = (gx[:, 0:MIX_WIDTH] * oc_ref[...] + gx[:, MIX_WIDTH:2 * MIX_WIDTH] * os_ref[...]
          + gx[:, 2 * MIX_WIDTH:3 * MIX_WIDTH] * ow_ref[...])
    z = None
    for n, br in enumerate((oa_ref[...], ob_ref[...], oc)):
        y = jnp.dot(br.astype(BF16), wb_ref[n], preferred_element_type=F32)
        t = gm_ref[:, n * D_MODEL:(n + 1) * D_MODEL] * y
        z = t if z is None else z + t
    o_ref[...] = x_ref[...] + jnp.dot(z.astype(BF16), wo_ref[...], preferred_element_type=F32)


def _merge(x2, oa, ob, oc, osel, ow, gc, gm, expand_g, wb, wo, tm=256):
    Tn = x2.shape[0]
    row = lambda d: pl.BlockSpec((tm, d), lambda i: (i, 0))
    return pl.pallas_call(
        _merge_kernel,
        grid=(Tn // tm,),
        in_specs=[row(D_MODEL), row(MIX_WIDTH), row(MIX_WIDTH), row(MIX_WIDTH), row(MIX_WIDTH), row(MIX_WIDTH),
                  row(CG_PAD), row(3 * D_MODEL),
                  pl.BlockSpec((CG_PAD, 3 * MIX_WIDTH), lambda i: (0, 0)),
                  pl.BlockSpec((3, MIX_WIDTH, D_MODEL), lambda i: (0, 0, 0)),
                  pl.BlockSpec((D_MODEL, D_MODEL), lambda i: (0, 0))],
        out_specs=row(D_MODEL),
        out_shape=jax.ShapeDtypeStruct((Tn, D_MODEL), F32),
        compiler_params=_params(1),
        name="merge",
    )(x2, oa, ob, oc, osel, ow, gc, gm, expand_g, wb, wo)


def _mlp_kernel(x_ref, nrm_ref, wu_ref, wd_ref, o_ref, h_ref, acc_ref):
    f = pl.program_id(1)

    @pl.when(f == 0)
    def _():
        h_ref[...] = _rms(x_ref[...], nrm_ref[...]).astype(BF16)
        acc_ref[...] = jnp.zeros(acc_ref.shape, F32)

    u = jnp.dot(h_ref[...], wu_ref[...], preferred_element_type=F32)
    u = jnp.square(jnp.maximum(u, 0.0)).astype(BF16)
    acc_ref[...] += jnp.dot(u, wd_ref[...], preferred_element_type=F32)

    @pl.when(f == pl.num_programs(1) - 1)
    def _():
        o_ref[...] = x_ref[...] + acc_ref[...]


def _mlp(x2, nrm, wu, wd, tm=1024, tf=512):
    Tn = x2.shape[0]
    return pl.pallas_call(
        _mlp_kernel,
        grid=(Tn // tm, D_FF // tf),
        in_specs=[pl.BlockSpec((tm, D_MODEL), lambda i, f: (i, 0)),
                  pl.BlockSpec((1, D_MODEL), lambda i, f: (0, 0)),
                  pl.BlockSpec((D_MODEL, tf), lambda i, f: (0, f)),
                  pl.BlockSpec((tf, D_MODEL), lambda i, f: (f, 0))],
        out_specs=pl.BlockSpec((tm, D_MODEL), lambda i, f: (i, 0)),
        out_shape=jax.ShapeDtypeStruct((Tn, D_MODEL), F32),
        scratch_shapes=[pltpu.VMEM((tm, D_MODEL), BF16), pltpu.VMEM((tm, D_MODEL), F32)],
        compiler_params=_params(2),
        name="mlp",
    )(x2, nrm, wu, wd)


def _overlap(n_cmp):
    c_start = np.arange(NC_PAD) * CMP_STRIDE
    j_start = np.arange(N_SEL_BLK) * SEL_BLOCK
    ov = ((c_start[None, :] < j_start[:, None] + SEL_BLOCK) & (c_start[None, :] + CMP_BLOCK > j_start[:, None])
          & (np.arange(NC_PAD)[None, :] < n_cmp))
    return jnp.asarray(ov.astype(np.float32), BF16)


def _gate_expand():
    e = np.zeros((CG_PAD, 3 * MIX_WIDTH), np.float32)
    for h in range(C_HEADS):
        for j in range(3):
            e[h * 3 + j, j * MIX_WIDTH + h * HEAD_DIM:j * MIX_WIDTH + (h + 1) * HEAD_DIM] = 1.0
    return jnp.asarray(e, BF16)


def kernel(x, w_in, qk_gain, diff_lambda, diff_subln, sinks, cmp_pos, cmp_w1, cmp_w2,
           w_branch, w_out, norm_mix, norm_mlp, w_up, w_down, rel_bias):
    B, S, _ = x.shape
    depth = w_in.shape[0]
    n_cmp = (S - CMP_BLOCK) // CMP_STRIDE + 1
    assert S % T_ROW == 0 and S // CMP_STRIDE == NC_PAD and S // SEL_BLOCK == N_SEL_BLK
    half = CMP_BLOCK // 2 * HEAD_DIM

    w_in_p = jnp.concatenate(
        [w_in[:, :, :C_CG], jnp.pad(w_in[:, :, C_CG:C_CG + N_CG], ((0, 0), (0, 0), (0, CG_PAD - N_CG))),
         w_in[:, :, C_CG + N_CG:]], axis=-1).astype(BF16)
    gains2 = jnp.concatenate([qk_gain, qk_gain], axis=-1)
    w1 = cmp_w1.astype(BF16).reshape(depth, 2, 2, half, CMP_HIDDEN)
    w2p = jnp.pad(cmp_w2, ((0, 0), (0, 0), (0, 0), (0, LANES - HEAD_DIM))).astype(BF16)
    pos = cmp_pos.reshape(depth, 2, 2, half)
    wb = w_branch.astype(BF16)
    wo = w_out.astype(BF16)
    wu = w_up.astype(BF16)
    wd = w_down.astype(BF16)

    bias_a = rel_bias[:, :A_HEADS]
    bias_b = rel_bias[:, A_HEADS:A_HEADS + B_HEADS]
    bias_c = rel_bias[:, A_HEADS + B_HEADS:]
    bt_a = _bias_tiles(bias_a, T_ROW, 3, None, 1)
    bt_b = _bias_tiles(bias_b, T_SWA, B_WINDOW // T_SWA + 1, B_WINDOW, GROUP)
    bt_sel = _bias_tiles(bias_c, T_ROW, 3, None, GROUP)
    bt_win = _bias_tiles(bias_c, T_ROW, C_WINDOW // T_ROW + 1, C_WINDOW, GROUP)
    ovl = _overlap(n_cmp)
    expand_g = _gate_expand()

    for layer in range(depth):
        lam_init = 0.8 - 0.6 * math.exp(-0.3 * layer)
        (qat, ka, vat, qbt, kb, vbt, qct, tkv, ksel, vselt, kwin, vwint, gc, gm) = _inproj(
            x, norm_mix[layer][None], w_in_p[layer], gains2[layer])
        oa = _attn(qat, ka, vat, bt_a, mode="diff", far=True, dl=diff_lambda[layer],
                   subln=diff_subln[layer][None], lam_init=lam_init, name="attn_diff")
        ob = _attn(qbt, kb, vbt, bt_b, mode="sink",
                   sink=sinks[layer].reshape(KV_GROUPS, GROUP, 1, 1), name="attn_swa")
        kc, vct = _compress(tkv.reshape(B, 2, KV_GROUPS, NC_PAD, half), pos[layer],
                            w1[layer], w2p[layer], qk_gain[layer][5:6])
        ocmp, sel = _cmp_attn(qct, kc, vct, ovl, n_cmp)
        osel = _attn(qct, ksel, vselt, bt_sel, far=True, sel=sel, name="attn_sel")
        owin = _attn(qct, kwin, vwint, bt_win, name="attn_win")
        f2 = lambda a: a.reshape(B * S, a.shape[-1])
        x2 = _merge(f2(x), f2(oa), f2(ob), f2(ocmp), f2(osel), f2(owin), f2(gc), f2(gm),
                    expand_g, wb[layer], wo[layer])
        x2 = _mlp(x2, norm_mlp[layer][None], wu[layer], wd[layer])
        x = x2.reshape(B, S, D_MODEL)
    return x
```

```python
import functools
import math

import numpy as np
import jax
import jax.numpy as jnp
from jax import lax
from jax.experimental import pallas as pl
from jax.experimental.pallas import tpu as pltpu

F32 = jnp.float32
BF16 = jnp.bfloat16

D_MODEL = 1024
HEAD_DIM = 64
A_HEADS = 4
A_VDIM = 128
B_HEADS = 8
B_WINDOW = 128
C_HEADS = 8
KV_GROUPS = 2
GROUP = 4
CMP_BLOCK = 32
CMP_STRIDE = 16
CMP_HIDDEN = 256
SEL_BLOCK = 64
N_SELECT = 16
C_WINDOW = 512
MIX_WIDTH = 512
D_FF = 4096
NUM_BUCKETS = 32
MAX_DISTANCE = 128
NEG_INF = -1e30
FORCE_BONUS = 1e4
EPS = 1e-6
LOG2E = 1.4426950408889634
QK_SCALE = HEAD_DIM ** -0.5 * LOG2E

C_AQ, C_AK, C_AV, C_BQ, C_BKV, C_CQ, C_CKV, C_CG, C_MG = 0, 512, 1024, 1536, 2048, 2304, 2816, 3584, 3712
IN_COLS_PAD = C_MG + 3 * D_MODEL
N_CG = C_HEADS * 3
LANES = 128
CG_PAD = LANES
NC_PAD = 256
N_SEL_BLK = 64
ONES_ROWS = 16

T_ROW = 256
T_SWA = 128

VMEM_LIMIT = 48 * 1024 * 1024


def _rms(x, gain):
    ms = jnp.mean(x * x, axis=-1, keepdims=True)
    return (x * lax.rsqrt(ms + EPS)) * gain


def _params(n_axes):
    return pltpu.CompilerParams(dimension_semantics=("arbitrary",) * n_axes,
                                vmem_limit_bytes=VMEM_LIMIT)


def _inproj_kernel(x_ref, nrm_ref, w_ref, gains_ref,
                   qa_ref, ka_ref, va_ref, qb_ref, kb_ref, vb_ref, qc_ref,
                   tkv_ref, ksel_ref, vsel_ref, kwin_ref, vwin_ref, gc_ref, gm_ref):
    tm = x_ref.shape[1]
    hb = _rms(x_ref[0], nrm_ref[...]).astype(BF16)
    lane = lax.broadcasted_iota(jnp.int32, (tm, LANES), 1)
    first = lane < HEAD_DIM

    def ones_pad(width):
        return jnp.where(lax.broadcasted_iota(jnp.int32, (ONES_ROWS, width), 0) == 0, 1.0, 0.0).astype(BF16)

    def mm(c0, n):
        return jnp.dot(hb, w_ref[:, c0:c0 + n], preferred_element_type=F32)

    def norm_pair(y, gain_idx, scale=None):
        sq = y * y
        ms0 = jnp.sum(jnp.where(first, sq, 0.0), axis=-1, keepdims=True) * (1.0 / HEAD_DIM)
        ms1 = jnp.sum(jnp.where(first, 0.0, sq), axis=-1, keepdims=True) * (1.0 / HEAD_DIM)
        inv = jnp.where(first, lax.rsqrt(ms0 + EPS), lax.rsqrt(ms1 + EPS))
        out = (y * inv) * gains_ref[gain_idx:gain_idx + 1, :]
        return out if scale is None else out * scale

    def pairs(c0):
        r = mm(c0, 256)
        return r[:, :LANES], r[:, LANES:]

    def put_qt(ref, i0, i1, y):
        yt = y.T.astype(BF16)
        ref[0, i0, i1] = yt[:HEAD_DIM]
        ref[0, i0, i1 + 1] = yt[HEAD_DIM:]

    def put_vt(ref, g0, y, n_tiles):
        yt = y.T.astype(BF16)
        tk = tm // n_tiles
        for g in range(2):
            for t in range(n_tiles):
                ref[0, g0 + g, t, 0:HEAD_DIM, :] = yt[HEAD_DIM * g:HEAD_DIM * (g + 1), tk * t:tk * (t + 1)]
                ref[0, g0 + g, t, HEAD_DIM:HEAD_DIM + ONES_ROWS, :] = ones_pad(tk)

    for ch in range(2):
        for half, y in enumerate(pairs(C_AQ + 256 * ch)):
            put_qt(qa_ref, ch * 2 + half, 0, norm_pair(y, 0, QK_SCALE))
        for half, y in enumerate(pairs(C_AK + 256 * ch)):
            y = norm_pair(y, 1).astype(BF16)
            ka_ref[0, ch * 2 + half, 0] = y[:, :HEAD_DIM]
            ka_ref[0, ch * 2 + half, 1] = y[:, HEAD_DIM:]
        for half, y in enumerate(pairs(C_AV + 256 * ch)):
            va_ref[0, ch * 2 + half, 0, 0:A_VDIM, :] = y.T.astype(BF16)
            va_ref[0, ch * 2 + half, 0, A_VDIM:A_VDIM + ONES_ROWS, :] = ones_pad(tm)
        for half, y in enumerate(pairs(C_BQ + 256 * ch)):
            put_qt(qb_ref, ch, 2 * half, norm_pair(y, 2, QK_SCALE))
        for half, y in enumerate(pairs(C_CQ + 256 * ch)):
            put_qt(qc_ref, ch, 2 * half, norm_pair(y, 4, QK_SCALE))

    def put_k(ref, y):
        y = y.astype(BF16)
        ref[0, 0] = y[:, :HEAD_DIM]
        ref[0, 1] = y[:, HEAD_DIM:]

    yk, yv = pairs(C_BKV)
    put_k(kb_ref, norm_pair(yk, 3))
    put_vt(vb_ref, 0, yv, tm // T_SWA)
    y0, y1 = pairs(C_CKV)
    for kv, y in enumerate((y0, y1)):
        tkv_ref[0, kv, 0] = y[:, :HEAD_DIM]
        tkv_ref[0, kv, 1] = y[:, HEAD_DIM:]
    yk, yv = pairs(C_CKV + 256)
    put_k(ksel_ref, norm_pair(yk, 6))
    put_vt(vsel_ref, 0, yv, 1)
    yk, yv = pairs(C_CKV + 512)
    put_k(kwin_ref, norm_pair(yk, 7))
    put_vt(vwin_ref, 0, yv, 1)

    gc_ref[0] = jax.nn.sigmoid(mm(C_CG, CG_PAD))
    for ch in range(3 * D_MODEL // 256):
        gm_ref[0, :, 256 * ch:256 * (ch + 1)] = jax.nn.sigmoid(mm(C_MG + 256 * ch, 256))


def _inproj(x, nrm, w, gains2):
    B, S, _ = x.shape
    tm = T_ROW
    nt = S // tm
    sd = jax.ShapeDtypeStruct
    qt = lambda a, b: sd((B, a, b, HEAD_DIM, S), BF16)
    kk = lambda n: sd((B, n, S, HEAD_DIM), BF16)
    vt = lambda n, dv, tk: sd((B, n, S // tk, dv + ONES_ROWS, tk), BF16)
    out_shape = (
        qt(A_HEADS, 2), sd((B, A_HEADS, 2, S, HEAD_DIM), BF16), vt(A_HEADS, A_VDIM, tm),
        qt(KV_GROUPS, GROUP), kk(KV_GROUPS), vt(KV_GROUPS, HEAD_DIM, T_SWA),
        qt(KV_GROUPS, GROUP),
        sd((B, 2, KV_GROUPS, S, HEAD_DIM), F32),
        kk(KV_GROUPS), vt(KV_GROUPS, HEAD_DIM, tm), kk(KV_GROUPS), vt(KV_GROUPS, HEAD_DIM, tm),
        sd((B, S, CG_PAD), F32),
        sd((B, S, 3 * D_MODEL), F32),
    )
    s_qt = lambda a, b: pl.BlockSpec((1, a, b, HEAD_DIM, tm), lambda b_, i: (b_, 0, 0, 0, i))
    s_k = lambda n: pl.BlockSpec((1, n, tm, HEAD_DIM), lambda b_, i: (b_, 0, i, 0))
    s_k5 = lambda a, c: pl.BlockSpec((1, a, c, tm, HEAD_DIM), lambda b_, i: (b_, 0, 0, i, 0))
    s_vt = lambda n, dv, tk: pl.BlockSpec((1, n, tm // tk, dv + ONES_ROWS, tk), lambda b_, i: (b_, 0, i, 0, 0))
    row = lambda d: pl.BlockSpec((1, tm, d), lambda b_, i: (b_, i, 0))
    out_specs = (
        s_qt(A_HEADS, 2), s_k5(A_HEADS, 2), s_vt(A_HEADS, A_VDIM, tm),
        s_qt(KV_GROUPS, GROUP), s_k(KV_GROUPS), s_vt(KV_GROUPS, HEAD_DIM, T_SWA),
        s_qt(KV_GROUPS, GROUP), s_k5(2, KV_GROUPS),
        s_k(KV_GROUPS), s_vt(KV_GROUPS, HEAD_DIM, tm), s_k(KV_GROUPS), s_vt(KV_GROUPS, HEAD_DIM, tm),
        row(CG_PAD), row(3 * D_MODEL),
    )
    return pl.pallas_call(
        _inproj_kernel,
        grid=(B, nt),
        in_specs=[
            row(D_MODEL),
            pl.BlockSpec((1, D_MODEL), lambda b_, i: (0, 0)),
            pl.BlockSpec((D_MODEL, IN_COLS_PAD), lambda b_, i: (0, 0), pipeline_mode=pl.Buffered(1)),
            pl.BlockSpec((8, LANES), lambda b_, i: (0, 0)),
        ],
        out_specs=out_specs,
        out_shape=out_shape,
        compiler_params=_params(2),
        name="inproj",
    )(x, nrm, w, gains2)


def _compress_kernel(t_ref, pos_ref, w1_ref, w2_ref, gain_ref, o_ref, ot_ref):
    t = t_ref[0, 0, 0]
    lo = (t + pos_ref[0, 0:1, :]).astype(BF16)
    hi = (t + pos_ref[0, 1:2, :]).astype(BF16)
    v = jnp.dot(lo, w1_ref[0, 0], preferred_element_type=F32)
    u = jnp.dot(hi, w1_ref[0, 1], preferred_element_type=F32)
    pre = v + pltpu.roll(u, NC_PAD - 1, 0)
    hcur = jax.nn.gelu(pre).astype(BF16)
    out = jnp.dot(hcur, w2_ref[0], preferred_element_type=F32)
    o64 = out[:, :HEAD_DIM]
    is_key = pl.program_id(1) == 0
    o_ref[0, 0, 0] = jnp.where(is_key, _rms(o64, gain_ref[...]), o64).astype(BF16)
    ot_ref[0, 0, 0] = out.T[:HEAD_DIM].astype(BF16)


def _compress(tkv16, pos, w1, w2p, gain):
    B = tkv16.shape[0]
    half = CMP_BLOCK // 2 * HEAD_DIM
    return pl.pallas_call(
        _compress_kernel,
        grid=(B, 2, KV_GROUPS),
        in_specs=[
            pl.BlockSpec((1, 1, 1, NC_PAD, half), lambda b, kv, g: (b, kv, g, 0, 0)),
            pl.BlockSpec((1, 2, half), lambda b, kv, g: (kv, 0, 0)),
            pl.BlockSpec((1, 2, half, CMP_HIDDEN), lambda b, kv, g: (kv, 0, 0, 0)),
            pl.BlockSpec((1, CMP_HIDDEN, LANES), lambda b, kv, g: (kv, 0, 0)),
            pl.BlockSpec((1, HEAD_DIM), lambda b, kv, g: (0, 0)),
        ],
        out_specs=(pl.BlockSpec((1, 1, 1, NC_PAD, HEAD_DIM), lambda b, kv, g: (b, kv, g, 0, 0)),
                   pl.BlockSpec((1, 1, 1, HEAD_DIM, NC_PAD), lambda b, kv, g: (b, kv, g, 0, 0))),
        out_shape=(jax.ShapeDtypeStruct((B, 2, KV_GROUPS, NC_PAD, HEAD_DIM), BF16),
                   jax.ShapeDtypeStruct((B, 2, KV_GROUPS, HEAD_DIM, NC_PAD), BF16)),
        compiler_params=_params(3),
        name="compress",
    )(tkv16, pos, w1, w2p, gain)


def _cmp_attn_kernel(qt_ref, kc_ref, vct_ref, ovl_ref, o_ref, sel_ref, score_ref, rank_ref, *, T, n_cmp):
    R = GROUP
    I = pl.program_id(2)
    kc = kc_ref[0, 0, 0]
    vct = vct_ref[0, 0, 0]
    c_idx = lax.broadcasted_iota(jnp.int32, (NC_PAD, T), 0)
    t_pos = I * T + lax.broadcasted_iota(jnp.int32, (NC_PAD, T), 1)
    cmask = (c_idx * CMP_STRIDE + (CMP_BLOCK - 1) <= t_pos) & (c_idx < n_cmp)
    scores = [jnp.dot(kc, qt_ref[0, 0, r], preferred_element_type=F32) for r in range(R)]
    probs = []
    for r in range(R):
        s = jnp.where(cmask, scores[r], NEG_INF)
        m = jnp.max(s, axis=0, keepdims=True)
        e = jnp.where(cmask, jnp.exp2(s - m), 0.0)
        l = jnp.sum(e, axis=0, keepdims=True)
        probs.append(e * (1.0 / jnp.where(l > 0.0, l, 1.0)))
    outs = [jnp.dot(vct, p.astype(BF16), preferred_element_type=F32) for p in probs]
    o_ref[0] = jnp.concatenate(outs, axis=0).T
    psum = (probs[0] + probs[1]) + (probs[2] + probs[3])

    p_hi = psum.astype(BF16)
    p_lo = (psum - p_hi.astype(F32)).astype(BF16)
    ovl = ovl_ref[...]
    imp = jnp.dot(ovl, p_hi, preferred_element_type=F32) + jnp.dot(ovl, p_lo, preferred_element_type=F32)
    j_idx = lax.broadcasted_iota(jnp.int32, (N_SEL_BLK, T), 0)
    tq = I * T + lax.broadcasted_iota(jnp.int32, (N_SEL_BLK, T), 1)
    cur = tq // SEL_BLOCK
    valid = j_idx * SEL_BLOCK <= tq
    forced = (j_idx == 0) | (j_idx == cur) | (j_idx == cur - 1)
    score = jnp.where(valid, imp + jnp.where(forced, FORCE_BONUS, 0.0), NEG_INF)
    score_ref[...] = score
    rank_ref[...] = jnp.zeros(rank_ref.shape, F32)
    n_live = (I + 1) * (T // SEL_BLOCK)
    sub = 8
    j_loc = lax.broadcasted_iota(jnp.int32, (sub, T), 0)
    for c in range(N_SEL_BLK // sub):
        @pl.when(c * sub < n_live)
        def _(c=c):
            n_grp = N_SEL_BLK // sub
            grps = [score_ref[g * sub:(g + 1) * sub, :] for g in range(n_grp)]
            cnts = [rank_ref[g * sub:(g + 1) * sub, :] for g in range(n_grp)]
            for jp in range(c * sub, (c + 1) * sub):
                row = jnp.broadcast_to(score_ref[jp:jp + 1, :], (sub, T))
                for g in range(n_grp):
                    if g * sub > jp:
                        beats = row >= grps[g]
                    elif (g + 1) * sub - 1 < jp:
                        beats = row > grps[g]
                    else:
                        beats = (row > grps[g]) | ((row == grps[g]) & (j_loc + g * sub > jp))
                    cnts[g] = cnts[g] + jnp.where(beats, 1.0, 0.0)
            for g in range(n_grp):
                rank_ref[g * sub:(g + 1) * sub, :] = cnts[g]
    sel_ref[0, 0] = jnp.where(rank_ref[...] < N_SELECT, 0.0, NEG_INF)


def _cmp_attn(qct, kc, vct, ovl, n_cmp):
    B, G, R, _, S = qct.shape
    T = T_ROW
    return pl.pallas_call(
        functools.partial(_cmp_attn_kernel, T=T, n_cmp=n_cmp),
        grid=(B, G, S // T),
        in_specs=[
            pl.BlockSpec((1, 1, R, HEAD_DIM, T), lambda b, g, i: (b, g, 0, 0, i)),
            pl.BlockSpec((1, 1, 1, NC_PAD, HEAD_DIM), lambda b, g, i: (b, 0, g, 0, 0)),
            pl.BlockSpec((1, 1, 1, HEAD_DIM, NC_PAD), lambda b, g, i: (b, 1, g, 0, 0)),
            pl.BlockSpec((N_SEL_BLK, NC_PAD), lambda b, g, i: (0, 0)),
        ],
        out_specs=(
            pl.BlockSpec((1, T, R * HEAD_DIM), lambda b, g, i: (b, i, g)),
            pl.BlockSpec((1, 1, N_SEL_BLK, T), lambda b, g, i: (b, g, 0, i)),
        ),
        out_shape=(
            jax.ShapeDtypeStruct((B, S, G * R * HEAD_DIM), F32),
            jax.ShapeDtypeStruct((B, G, N_SEL_BLK, S), F32),
        ),
        scratch_shapes=[pltpu.VMEM((N_SEL_BLK, T), F32), pltpu.VMEM((N_SEL_BLK, T), F32)],
        compiler_params=_params(3),
        name="cmp_attn",
    )(qct, kc, vct, ovl)


def _attn_kernel(*refs, R, T, Dv, nrel, far, k_per_r, shared_bias, mode, has_sel, lam_init):
    it = iter(refs)
    qt_ref, k_ref, vt_ref, bt_ref = next(it), next(it), next(it), next(it)
    sel_ref = next(it) if has_sel else None
    sink_ref = next(it) if mode == "sink" else None
    dl_ref = next(it) if mode == "diff" else None
    sub_ref = next(it) if mode == "diff" else None
    o_ref, m_ref, acc_ref, s_ref, p_ref, al_ref = (next(it) for _ in range(6))

    I = pl.program_id(2)
    m_ref[...] = jnp.full(m_ref.shape, NEG_INF, F32)
    acc_ref[...] = jnp.zeros(acc_ref.shape, F32)
    blk_per_tile = T // SEL_BLOCK
    bias_r = (lambda r: 0) if shared_bias else (lambda r: r)
    J0 = 0 if far else jnp.maximum(I - (nrel - 1), 0)

    def rel_of(J):
        return jnp.minimum(I - J, nrel - 1) if far else I - J

    def score_stage(J, slot):
        off = pl.multiple_of(J * T, T)
        for r in range(R):
            kt = k_ref[0, 0, r, pl.ds(off, T), :] if k_per_r else k_ref[0, 0, pl.ds(off, T), :]
            s_ref[slot, r] = jnp.dot(kt, qt_ref[0, 0, r], preferred_element_type=F32)

    def softmax_stage(J, rel, slot):
        if has_sel:
            mask = jnp.concatenate(
                [jnp.broadcast_to(sel_ref[0, 0, pl.ds(J * blk_per_tile + i, 1), :], (SEL_BLOCK, T))
                 for i in range(blk_per_tile)], axis=0)
        for r in range(R):
            s = s_ref[slot, r] + bt_ref[0, rel, bias_r(r)]
            if has_sel:
                s = s + mask
            m_prev = m_ref[r]
            m_new = jnp.maximum(m_prev, jnp.max(s, axis=0, keepdims=True))
            al_ref[slot, r] = jnp.exp2(m_prev - m_new)
            p_ref[slot, r] = jnp.exp2(s - m_new).astype(BF16)
            m_ref[r] = m_new

    def value_stage(J, slot):
        vt = vt_ref[0, 0, J]
        return [jnp.dot(vt, p_ref[slot, r], preferred_element_type=F32) for r in range(R)]

    def accumulate(slot, pvs):
        for r in range(R):
            acc_ref[r] = al_ref[slot, r] * acc_ref[r] + pvs[r]

    n_off = I - J0

    @pl.when(n_off % 2 == 1)
    def _():
        score_stage(J0, 0)
        softmax_stage(J0, rel_of(J0), 0)
        accumulate(0, value_stage(J0, 0))

    J1 = J0 + n_off % 2

    def half_step(J, slot):
        other = 1 - slot
        pvs = value_stage(jnp.maximum(J - 1, 0), other)
        score_stage(J + 1, other)
        softmax_stage(J, rel_of(J), slot)
        accumulate(other, pvs)

    def pair(u, carry):
        J = J1 + 2 * u
        half_step(J, 0)
        half_step(J + 1, 1)
        return carry

    score_stage(J1, 0)
    p_ref[1] = jnp.zeros(p_ref.shape[1:], BF16)
    al_ref[1] = jnp.ones(al_ref.shape[1:], F32)
    lax.fori_loop(0, n_off // 2, pair, 0)
    pvs = value_stage(jnp.maximum(I - 1, 0), 1)
    softmax_stage(I, 0, 0)
    accumulate(1, pvs)
    accumulate(0, value_stage(I, 0))

    outs = []
    for r in range(R):
        acc = acc_ref[r]
        num, l = acc[:Dv], acc[Dv:Dv + 1]
        if mode == "sink":
            m = m_ref[r]
            sk = sink_ref[0, r] * LOG2E
            m_f = jnp.maximum(m, sk)
            w = jnp.exp2(m - m_f)
            outs.append(num * w / (l * w + jnp.exp2(sk - m_f)))
        else:
            outs.append(num / l)
    if mode == "diff":
        dl = dl_ref[...]
        lam = (jnp.exp(jnp.sum(dl[0:1] * dl[1:2], keepdims=True))
               - jnp.exp(jnp.sum(dl[2:3] * dl[3:4], keepdims=True)) + lam_init)
        o = (outs[0] - lam * outs[1]).T
        o_ref[0] = _rms(o, sub_ref[...]) * (1.0 - lam_init)
    else:
        o_ref[0] = jnp.concatenate(outs, axis=0).T


def _attn(qt, k, vt, bias_tab, *, mode="plain", far=False, sel=None,
          sink=None, dl=None, subln=None, lam_init=0.0, name="attn"):
    B, G, R, _, S = qt.shape
    T = vt.shape[-1]
    Dv = vt.shape[-2] - ONES_ROWS
    nrel, bias_heads = bias_tab.shape[1], bias_tab.shape[2]
    k_per_r = k.ndim == 5
    has_sel = sel is not None
    ins = [qt, k, vt, bias_tab]
    in_specs = [
        pl.BlockSpec((1, 1, R, HEAD_DIM, T), lambda b, g, i: (b, g, 0, 0, i)),
        (pl.BlockSpec((1, 1, R, S, HEAD_DIM), lambda b, g, i: (b, g, 0, 0, 0)) if k_per_r
         else pl.BlockSpec((1, 1, S, HEAD_DIM), lambda b, g, i: (b, g, 0, 0))),
        pl.BlockSpec((1, 1, S // T, Dv + ONES_ROWS, T), lambda b, g, i: (b, g, 0, 0, 0)),
        pl.BlockSpec((1, nrel, bias_heads, T, T), lambda b, g, i: (g, 0, 0, 0, 0)),
    ]
    if has_sel:
        ins.append(sel)
        in_specs.append(pl.BlockSpec((1, 1, N_SEL_BLK, T), lambda b, g, i: (b, g, 0, i)))
    if mode == "sink":
        ins.append(sink)
        in_specs.append(pl.BlockSpec((1, R, 1, 1), lambda b, g, i: (g, 0, 0, 0)))
    if mode == "diff":
        ins += [dl, subln]
        in_specs += [pl.BlockSpec((4, HEAD_DIM), lambda b, g, i: (0, 0)),
                     pl.BlockSpec((1, A_VDIM), lambda b, g, i: (0, 0))]
        out_w = Dv
    else:
        out_w = R * Dv
    kern = functools.partial(_attn_kernel, R=R, T=T, Dv=Dv, nrel=nrel, far=far, k_per_r=k_per_r,
                             shared_bias=bias_heads == 1, mode=mode, has_sel=has_sel, lam_init=lam_init)
    return pl.pallas_call(
        kern,
        grid=(B, G, S // T),
        in_specs=in_specs,
        out_specs=pl.BlockSpec((1, T, out_w), lambda b, g, i: (b, i, g)),
        out_shape=jax.ShapeDtypeStruct((B, S, G * out_w), F32),
        scratch_shapes=[pltpu.VMEM((R, 1, T), F32), pltpu.VMEM((R, Dv + ONES_ROWS, T), F32),
                        pltpu.VMEM((2, R, T, T), F32), pltpu.VMEM((2, R, T, T), BF16),
                        pltpu.VMEM((2, R, 1, T), F32)],
        compiler_params=_params(3),
        name=name,
    )(*ins)


def _window_kernel(*refs, R, T, NK, TV, Dv, has_sink):
    it = iter(refs)
    qt_ref, k_ref, vt_ref, bt_ref = (next(it) for _ in range(4))
    sink_ref = next(it) if has_sink else None
    o_ref = next(it)
    I = pl.program_id(2)
    entry = jnp.minimum(I, (NK - T) // T)
    first_tile = I - entry
    kt = k_ref[0, 0, pl.ds(pl.multiple_of(first_tile * T, T), NK), :]
    scores = [jnp.dot(kt, qt_ref[0, 0, r], preferred_element_type=F32) for r in range(R)]
    probs, maxes = [], []
    for r in range(R):
        s = scores[r] + bt_ref[0, entry, r]
        m = jnp.max(s, axis=0, keepdims=True)
        probs.append(jnp.exp2(s - m).astype(BF16))
        maxes.append(m)
    outs = []
    for r in range(R):
        acc = None
        for c in range(NK // TV):
            part = jnp.dot(vt_ref[0, 0, first_tile * (T // TV) + c], probs[r][c * TV:(c + 1) * TV],
                           preferred_element_type=F32)
            acc = part if acc is None else acc + part
        num, l = acc[:Dv], acc[Dv:Dv + 1]
        if has_sink:
            sk = sink_ref[0, r] * LOG2E
            m_f = jnp.maximum(maxes[r], sk)
            w = jnp.exp2(maxes[r] - m_f)
            outs.append(num * w / (l * w + jnp.exp2(sk - m_f)))
        else:
            outs.append(num / l)
    o_ref[0] = jnp.concatenate(outs, axis=0).T


def _window_attn(qt, k, vt, bias_tab, *, T, sink=None, name="attn_window"):
    B, G, R, _, S = qt.shape
    TV = vt.shape[-1]
    Dv = vt.shape[-2] - ONES_ROWS
    entries, NK = bias_tab.shape[1], bias_tab.shape[3]
    has_sink = sink is not None
    ins = [qt, k, vt, bias_tab]
    in_specs = [
        pl.BlockSpec((1, 1, R, HEAD_DIM, T), lambda b, g, i: (b, g, 0, 0, i)),
        pl.BlockSpec((1, 1, S, HEAD_DIM), lambda b, g, i: (b, g, 0, 0)),
        pl.BlockSpec((1, 1, S // TV, Dv + ONES_ROWS, TV), lambda b, g, i: (b, g, 0, 0, 0)),
        pl.BlockSpec((1, entries, R, NK, T), lambda b, g, i: (g, 0, 0, 0, 0)),
    ]
    if has_sink:
        ins.append(sink)
        in_specs.append(pl.BlockSpec((1, R, 1, 1), lambda b, g, i: (g, 0, 0, 0)))
    kern = functools.partial(_window_kernel, R=R, T=T, NK=NK, TV=TV, Dv=Dv, has_sink=has_sink)
    return pl.pallas_call(
        kern,
        grid=(B, G, S // T),
        in_specs=in_specs,
        out_specs=pl.BlockSpec((1, T, R * Dv), lambda b, g, i: (b, i, g)),
        out_shape=jax.ShapeDtypeStruct((B, S, G * R * Dv), F32),
        compiler_params=_params(3),
        name=name,
    )(*ins)


def _bucket_np(dist):
    n = np.maximum(dist, 0)
    max_exact = NUM_BUCKETS // 2
    nf = np.maximum(n, 1).astype(np.float32)
    large = max_exact + (np.log(nf / max_exact) / math.log(MAX_DISTANCE / max_exact)
                         * (NUM_BUCKETS - max_exact)).astype(np.int32)
    large = np.minimum(large, NUM_BUCKETS - 1)
    return np.where(n < max_exact, n, large)


def _bucket_starts():
    b = _bucket_np(np.arange(4 * MAX_DISTANCE))
    assert (np.diff(b) >= 0).all() and b[-1] == NUM_BUCKETS - 1
    return [int(np.argmax(b >= i)) for i in range(NUM_BUCKETS)]


def _bias_kernel(tab_ref, o_ref, *, T, rows, nrel, window, starts):
    h = pl.program_id(0)
    key = lax.broadcasted_iota(jnp.int32, (rows, T), 0)
    qry = lax.broadcasted_iota(jnp.int32, (rows, T), 1)
    for rel in range(nrel):
        dist = rel * T + qry - key
        val = jnp.full((rows, T), tab_ref[0, h] * LOG2E, F32)
        for b in range(1, NUM_BUCKETS):
            val = jnp.where(dist >= starts[b], tab_ref[b, h] * LOG2E, val)
        allowed = dist >= 0
        if window is not None:
            allowed = allowed & (dist < window)
        o_ref[0, rel, 0] = jnp.where(allowed, val, NEG_INF)


def _bias_tiles(tab, T, nrel, window, R, rows=None):
    H = tab.shape[1]
    rows = T if rows is None else rows
    return pl.pallas_call(
        functools.partial(_bias_kernel, T=T, rows=rows, nrel=nrel, window=window, starts=_bucket_starts()),
        grid=(H,),
        in_specs=[pl.BlockSpec(memory_space=pltpu.SMEM)],
        out_specs=pl.BlockSpec((1, nrel, 1, rows, T), lambda h: (h // R, 0, h % R, 0, 0)),
        out_shape=jax.ShapeDtypeStruct((H // R, nrel, R, rows, T), F32),
        compiler_params=_params(1),
        name="bias_tiles",
    )(tab)


def _merge_kernel(x_ref, oa_ref, ob_ref, oc_ref, os_ref, ow_ref, gc_ref, gm_ref, ex_ref, wb_ref, wo_ref, o_ref):
    gc = gc_ref[...]
    g1 = gc.astype(BF16)
    r1 = gc - g1.astype(F32)
    g2 = r1.astype(BF16)
    g3 = (r1 - g2.astype(F32)).astype(BF16)
    ex = ex_ref[...]
    gx = (jnp.dot(g1, ex, preferred_element_type=F32) + jnp.dot(g2, ex, preferred_element_type=F32)
          + jnp.dot(g3, ex, preferred_element_type=F32))
    oc = (gx[:, 0:MIX_WIDTH] * oc_ref[...] + gx[:, MIX_WIDTH:2 * MIX_WIDTH] * os_ref[...]
          + gx[:, 2 * MIX_WIDTH:3 * MIX_WIDTH] * ow_ref[...])
    z = None
    for n, br in enumerate((oa_ref[...], ob_ref[...], oc)):
        y = jnp.dot(br.astype(BF16), wb_ref[n], preferred_element_type=F32)
        t = gm_ref[:, n * D_MODEL:(n + 1) * D_MODEL] * y
        z = t if z is None else z + t
    o_ref[...] = x_ref[...] + jnp.dot(z.astype(BF16), wo_ref[...], preferred_element_type=F32)


def _merge(x2, oa, ob, oc, osel, ow, gc, gm, expand_g, wb, wo, tm=256):
    Tn = x2.shape[0]
    row = lambda d: pl.BlockSpec((tm, d), lambda i: (i, 0))
    return pl.pallas_call(
        _merge_kernel,
        grid=(Tn // tm,),
        in_specs=[row(D_MODEL), row(MIX_WIDTH), row(MIX_WIDTH), row(MIX_WIDTH), row(MIX_WIDTH), row(MIX_WIDTH),
                  row(CG_PAD), row(3 * D_MODEL),
                  pl.BlockSpec((CG_PAD, 3 * MIX_WIDTH), lambda i: (0, 0)),
                  pl.BlockSpec((3, MIX_WIDTH, D_MODEL), lambda i: (0, 0, 0)),
                  pl.BlockSpec((D_MODEL, D_MODEL), lambda i: (0, 0))],
        out_specs=row(D_MODEL),
        out_shape=jax.ShapeDtypeStruct((Tn, D_MODEL), F32),
        compiler_params=_params(1),
        name="merge",
    )(x2, oa, ob, oc, osel, ow, gc, gm, expand_g, wb, wo)


def _mlp_kernel(x_ref, nrm_ref, wu_ref, wd_ref, o_ref, h_ref, acc_ref):
    f = pl.program_id(1)

    @pl.when(f == 0)
    def _():
        h_ref[...] = _rms(x_ref[...], nrm_ref[...]).astype(BF16)
        acc_ref[...] = jnp.zeros(acc_ref.shape, F32)

    u = jnp.dot(h_ref[...], wu_ref[...], preferred_element_type=F32)
    u = jnp.square(jnp.maximum(u, 0.0)).astype(BF16)
    acc_ref[...] += jnp.dot(u, wd_ref[...], preferred_element_type=F32)

    @pl.when(f == pl.num_programs(1) - 1)
    def _():
        o_ref[...] = x_ref[...] + acc_ref[...]


def _mlp(x2, nrm, wu, wd, tm=1024, tf=512):
    Tn = x2.shape[0]
    return pl.pallas_call(
        _mlp_kernel,
        grid=(Tn // tm, D_FF // tf),
        in_specs=[pl.BlockSpec((tm, D_MODEL), lambda i, f: (i, 0)),
                  pl.BlockSpec((1, D_MODEL), lambda i, f: (0, 0)),
                  pl.BlockSpec((D_MODEL, tf), lambda i, f: (0, f)),
                  pl.BlockSpec((tf, D_MODEL), lambda i, f: (f, 0))],
        out_specs=pl.BlockSpec((tm, D_MODEL), lambda i, f: (i, 0)),
        out_shape=jax.ShapeDtypeStruct((Tn, D_MODEL), F32),
        scratch_shapes=[pltpu.VMEM((tm, D_MODEL), BF16), pltpu.VMEM((tm, D_MODEL), F32)],
        compiler_params=_params(2),
        name="mlp",
    )(x2, nrm, wu, wd)


def _overlap(n_cmp):
    c_start = np.arange(NC_PAD) * CMP_STRIDE
    j_start = np.arange(N_SEL_BLK) * SEL_BLOCK
    ov = ((c_start[None, :] < j_start[:, None] + SEL_BLOCK) & (c_start[None, :] + CMP_BLOCK > j_start[:, None])
          & (np.arange(NC_PAD)[None, :] < n_cmp))
    return jnp.asarray(ov.astype(np.float32), BF16)


def _gate_expand():
    e = np.zeros((CG_PAD, 3 * MIX_WIDTH), np.float32)
    for h in range(C_HEADS):
        for j in range(3):
            e[h * 3 + j, j * MIX_WIDTH + h * HEAD_DIM:j * MIX_WIDTH + (h + 1) * HEAD_DIM] = 1.0
    return jnp.asarray(e, BF16)


def kernel(x, w_in, qk_gain, diff_lambda, diff_subln, sinks, cmp_pos, cmp_w1, cmp_w2,
           w_branch, w_out, norm_mix, norm_mlp, w_up, w_down, rel_bias):
    B, S, _ = x.shape
    depth = w_in.shape[0]
    n_cmp = (S - CMP_BLOCK) // CMP_STRIDE + 1
    assert S % T_ROW == 0 and S // CMP_STRIDE == NC_PAD and S // SEL_BLOCK == N_SEL_BLK
    half = CMP_BLOCK // 2 * HEAD_DIM

    w_in_p = jnp.concatenate(
        [w_in[:, :, :C_CG], jnp.pad(w_in[:, :, C_CG:C_CG + N_CG], ((0, 0), (0, 0), (0, CG_PAD - N_CG))),
         w_in[:, :, C_CG + N_CG:]], axis=-1).astype(BF16)
    gains2 = jnp.concatenate([qk_gain, qk_gain], axis=-1)
    w1 = cmp_w1.astype(BF16).reshape(depth, 2, 2, half, CMP_HIDDEN)
    w2p = jnp.pad(cmp_w2, ((0, 0), (0, 0), (0, 0), (0, LANES - HEAD_DIM))).astype(BF16)
    pos = cmp_pos.reshape(depth, 2, 2, half)
    wb = w_branch.astype(BF16)
    wo = w_out.astype(BF16)
    wu = w_up.astype(BF16)
    wd = w_down.astype(BF16)

    bias_a = rel_bias[:, :A_HEADS]
    bias_b = rel_bias[:, A_HEADS:A_HEADS + B_HEADS]
    bias_c = rel_bias[:, A_HEADS + B_HEADS:]
    bt_a = _bias_tiles(bias_a, T_ROW, 3, None, 1)
    bt_b = _bias_tiles(bias_b, T_SWA, B_WINDOW // T_SWA + 1, B_WINDOW, GROUP, rows=B_WINDOW + T_SWA)
    bt_sel = _bias_tiles(bias_c, T_ROW, 3, None, GROUP)
    bt_win = _bias_tiles(bias_c, T_ROW, C_WINDOW // T_ROW + 1, C_WINDOW, GROUP, rows=C_WINDOW + T_ROW)
    ovl = _overlap(n_cmp)
    expand_g = _gate_expand()

    for layer in range(depth):
        lam_init = 0.8 - 0.6 * math.exp(-0.3 * layer)
        (qat, ka, vat, qbt, kb, vbt, qct, tkv, ksel, vselt, kwin, vwint, gc, gm) = _inproj(
            x, norm_mix[layer][None], w_in_p[layer], gains2[layer])
        oa = _attn(qat, ka, vat, bt_a, mode="diff", far=True, dl=diff_lambda[layer],
                   subln=diff_subln[layer][None], lam_init=lam_init, name="attn_diff")
        ob = _window_attn(qbt, kb, vbt, bt_b, T=T_SWA,
                          sink=sinks[layer].reshape(KV_GROUPS, GROUP, 1, 1), name="attn_swa")
        kc, vct = _compress(tkv.reshape(B, 2, KV_GROUPS, NC_PAD, half), pos[layer],
                            w1[layer], w2p[layer], qk_gain[layer][5:6])
        ocmp, sel = _cmp_attn(qct, kc, vct, ovl, n_cmp)
        osel = _attn(qct, ksel, vselt, bt_sel, far=True, sel=sel, name="attn_sel")
        owin = _window_attn(qct, kwin, vwint, bt_win, T=T_ROW, name="attn_win")
        f2 = lambda a: a.reshape(B * S, a.shape[-1])
        x2 = _merge(f2(x), f2(oa), f2(ob), f2(ocmp), f2(osel), f2(owin), f2(gc), f2(gm),
                    expand_g, wb[layer], wo[layer])
        x2 = _mlp(x2, norm_mlp[layer][None], wu[layer], wd[layer])
        x = x2.reshape(B, S, D_MODEL)
    return x
```

```python
import functools
import math

import numpy as np
import jax
import jax.numpy as jnp
from jax import lax
from jax.experimental import pallas as pl
from jax.experimental.pallas import tpu as pltpu

F32 = jnp.float32
BF16 = jnp.bfloat16

D_MODEL = 1024
HEAD_DIM = 64
A_HEADS = 4
A_VDIM = 128
B_HEADS = 8
B_WINDOW = 128
C_HEADS = 8
KV_GROUPS = 2
GROUP = 4
CMP_BLOCK = 32
CMP_STRIDE = 16
CMP_HIDDEN = 256
SEL_BLOCK = 64
N_SELECT = 16
C_WINDOW = 512
MIX_WIDTH = 512
D_FF = 4096
NUM_BUCKETS = 32
MAX_DISTANCE = 128
NEG_INF = -1e30
FORCE_BONUS = 1e4
EPS = 1e-6
LOG2E = 1.4426950408889634
QK_SCALE = HEAD_DIM ** -0.5 * LOG2E

C_AQ, C_AK, C_AV, C_BQ, C_BKV, C_CQ, C_CKV, C_CG, C_MG = 0, 512, 1024, 1536, 2048, 2304, 2816, 3584, 3712
IN_COLS_PAD = C_MG + 3 * D_MODEL
N_CG = C_HEADS * 3
LANES = 128
CG_PAD = LANES
NC_PAD = 256
N_SEL_BLK = 64
ONES_ROWS = 16

T_ROW = 256
T_SWA = 128

VMEM_LIMIT = 48 * 1024 * 1024


def _rms(x, gain):
    ms = jnp.mean(x * x, axis=-1, keepdims=True)
    return (x * lax.rsqrt(ms + EPS)) * gain


def _params(n_axes):
    return pltpu.CompilerParams(dimension_semantics=("arbitrary",) * n_axes,
                                vmem_limit_bytes=VMEM_LIMIT)


def _inproj_kernel(x_ref, nrm_ref, w_ref, gains_ref,
                   qa_ref, ka_ref, va_ref, qb_ref, kb_ref, vb_ref, qc_ref,
                   tkv_ref, ksel_ref, vsel_ref, kwin_ref, vwin_ref, gc_ref, gm_ref):
    tm = x_ref.shape[1]
    hb = _rms(x_ref[0], nrm_ref[...]).astype(BF16)
    lane = lax.broadcasted_iota(jnp.int32, (tm, LANES), 1)
    first = lane < HEAD_DIM

    def ones_pad(width):
        return jnp.where(lax.broadcasted_iota(jnp.int32, (ONES_ROWS, width), 0) == 0, 1.0, 0.0).astype(BF16)

    def mm(c0, n):
        return jnp.dot(hb, w_ref[:, c0:c0 + n], preferred_element_type=F32)

    def norm_pair(y, gain_idx, scale=None):
        sq = y * y
        ms0 = jnp.sum(jnp.where(first, sq, 0.0), axis=-1, keepdims=True) * (1.0 / HEAD_DIM)
        ms1 = jnp.sum(jnp.where(first, 0.0, sq), axis=-1, keepdims=True) * (1.0 / HEAD_DIM)
        inv = jnp.where(first, lax.rsqrt(ms0 + EPS), lax.rsqrt(ms1 + EPS))
        out = (y * inv) * gains_ref[gain_idx:gain_idx + 1, :]
        return out if scale is None else out * scale

    def pairs(c0):
        r = mm(c0, 256)
        return r[:, :LANES], r[:, LANES:]

    def put_qt(ref, i0, i1, y):
        yt = y.T.astype(BF16)
        n_tiles = ref.shape[3]
        tq = tm // n_tiles
        for t in range(n_tiles):
            ref[0, i0, i1, t] = yt[:HEAD_DIM, tq * t:tq * (t + 1)]
            ref[0, i0, i1 + 1, t] = yt[HEAD_DIM:, tq * t:tq * (t + 1)]

    def put_vt(ref, g0, y, n_tiles):
        yt = y.T.astype(BF16)
        tk = tm // n_tiles
        for g in range(2):
            for t in range(n_tiles):
                ref[0, g0 + g, t, 0:HEAD_DIM, :] = yt[HEAD_DIM * g:HEAD_DIM * (g + 1), tk * t:tk * (t + 1)]
                ref[0, g0 + g, t, HEAD_DIM:HEAD_DIM + ONES_ROWS, :] = ones_pad(tk)

    for ch in range(2):
        for half, y in enumerate(pairs(C_AQ + 256 * ch)):
            put_qt(qa_ref, ch * 2 + half, 0, norm_pair(y, 0, QK_SCALE))
        for half, y in enumerate(pairs(C_AK + 256 * ch)):
            y = norm_pair(y, 1).astype(BF16)
            ka_ref[0, ch * 2 + half, 0] = y[:, :HEAD_DIM]
            ka_ref[0, ch * 2 + half, 1] = y[:, HEAD_DIM:]
        for half, y in enumerate(pairs(C_AV + 256 * ch)):
            va_ref[0, ch * 2 + half, 0, 0:A_VDIM, :] = y.T.astype(BF16)
            va_ref[0, ch * 2 + half, 0, A_VDIM:A_VDIM + ONES_ROWS, :] = ones_pad(tm)
        for half, y in enumerate(pairs(C_BQ + 256 * ch)):
            put_qt(qb_ref, ch, 2 * half, norm_pair(y, 2, QK_SCALE))
        for half, y in enumerate(pairs(C_CQ + 256 * ch)):
            put_qt(qc_ref, ch, 2 * half, norm_pair(y, 4, QK_SCALE))

    def put_k(ref, y):
        y = y.astype(BF16)
        ref[0, 0] = y[:, :HEAD_DIM]
        ref[0, 1] = y[:, HEAD_DIM:]

    yk, yv = pairs(C_BKV)
    put_k(kb_ref, norm_pair(yk, 3))
    put_vt(vb_ref, 0, yv, tm // T_SWA)
    y0, y1 = pairs(C_CKV)
    for kv, y in enumerate((y0, y1)):
        tkv_ref[0, kv, 0] = y[:, :HEAD_DIM]
        tkv_ref[0, kv, 1] = y[:, HEAD_DIM:]
    yk, yv = pairs(C_CKV + 256)
    put_k(ksel_ref, norm_pair(yk, 6))
    put_vt(vsel_ref, 0, yv, 1)
    yk, yv = pairs(C_CKV + 512)
    put_k(kwin_ref, norm_pair(yk, 7))
    put_vt(vwin_ref, 0, yv, 1)

    gc_ref[0] = jax.nn.sigmoid(mm(C_CG, CG_PAD))
    for ch in range(3 * D_MODEL // 256):
        gm_ref[0, :, 256 * ch:256 * (ch + 1)] = jax.nn.sigmoid(mm(C_MG + 256 * ch, 256))


def _inproj(x, nrm, w, gains2):
    B, S, _ = x.shape
    tm = T_ROW
    nt = S // tm
    sd = jax.ShapeDtypeStruct
    qt = lambda a, b, tq: sd((B, a, b, S // tq, HEAD_DIM, tq), BF16)
    kk = lambda n: sd((B, n, S, HEAD_DIM), BF16)
    vt = lambda n, dv, tk: sd((B, n, S // tk, dv + ONES_ROWS, tk), BF16)
    out_shape = (
        qt(A_HEADS, 2, tm), sd((B, A_HEADS, 2, S, HEAD_DIM), BF16), vt(A_HEADS, A_VDIM, tm),
        qt(KV_GROUPS, GROUP, T_SWA), kk(KV_GROUPS), vt(KV_GROUPS, HEAD_DIM, T_SWA),
        qt(KV_GROUPS, GROUP, tm),
        sd((B, 2, KV_GROUPS, S, HEAD_DIM), F32),
        kk(KV_GROUPS), vt(KV_GROUPS, HEAD_DIM, tm), kk(KV_GROUPS), vt(KV_GROUPS, HEAD_DIM, tm),
        sd((B, S, CG_PAD), F32),
        sd((B, S, 3 * D_MODEL), F32),
    )
    s_qt = lambda a, b, tq: pl.BlockSpec((1, a, b, tm // tq, HEAD_DIM, tq), lambda b_, i: (b_, 0, 0, i, 0, 0))
    s_k = lambda n: pl.BlockSpec((1, n, tm, HEAD_DIM), lambda b_, i: (b_, 0, i, 0))
    s_k5 = lambda a, c: pl.BlockSpec((1, a, c, tm, HEAD_DIM), lambda b_, i: (b_, 0, 0, i, 0))
    s_vt = lambda n, dv, tk: pl.BlockSpec((1, n, tm // tk, dv + ONES_ROWS, tk), lambda b_, i: (b_, 0, i, 0, 0))
    row = lambda d: pl.BlockSpec((1, tm, d), lambda b_, i: (b_, i, 0))
    out_specs = (
        s_qt(A_HEADS, 2, tm), s_k5(A_HEADS, 2), s_vt(A_HEADS, A_VDIM, tm),
        s_qt(KV_GROUPS, GROUP, T_SWA), s_k(KV_GROUPS), s_vt(KV_GROUPS, HEAD_DIM, T_SWA),
        s_qt(KV_GROUPS, GROUP, tm), s_k5(2, KV_GROUPS),
        s_k(KV_GROUPS), s_vt(KV_GROUPS, HEAD_DIM, tm), s_k(KV_GROUPS), s_vt(KV_GROUPS, HEAD_DIM, tm),
        row(CG_PAD), row(3 * D_MODEL),
    )
    return pl.pallas_call(
        _inproj_kernel,
        grid=(B, nt),
        in_specs=[
            row(D_MODEL),
            pl.BlockSpec((1, D_MODEL), lambda b_, i: (0, 0)),
            pl.BlockSpec((D_MODEL, IN_COLS_PAD), lambda b_, i: (0, 0), pipeline_mode=pl.Buffered(1)),
            pl.BlockSpec((8, LANES), lambda b_, i: (0, 0)),
        ],
        out_specs=out_specs,
        out_shape=out_shape,
        compiler_params=_params(2),
        name="inproj",
    )(x, nrm, w, gains2)


def _compress_kernel(t_ref, pos_ref, w1_ref, w2_ref, gain_ref, o_ref, ot_ref):
    t = t_ref[0, 0, 0]
    lo = (t + pos_ref[0, 0:1, :]).astype(BF16)
    hi = (t + pos_ref[0, 1:2, :]).astype(BF16)
    v = jnp.dot(lo, w1_ref[0, 0], preferred_element_type=F32)
    u = jnp.dot(hi, w1_ref[0, 1], preferred_element_type=F32)
    pre = v + pltpu.roll(u, NC_PAD - 1, 0)
    hcur = jax.nn.gelu(pre).astype(BF16)
    out = jnp.dot(hcur, w2_ref[0], preferred_element_type=F32)
    o64 = out[:, :HEAD_DIM]
    is_key = pl.program_id(1) == 0
    o_ref[0, 0, 0] = jnp.where(is_key, _rms(o64, gain_ref[...]), o64).astype(BF16)
    ot_ref[0, 0, 0] = out.T[:HEAD_DIM].astype(BF16)


def _compress(tkv16, pos, w1, w2p, gain):
    B = tkv16.shape[0]
    half = CMP_BLOCK // 2 * HEAD_DIM
    return pl.pallas_call(
        _compress_kernel,
        grid=(B, 2, KV_GROUPS),
        in_specs=[
            pl.BlockSpec((1, 1, 1, NC_PAD, half), lambda b, kv, g: (b, kv, g, 0, 0)),
            pl.BlockSpec((1, 2, half), lambda b, kv, g: (kv, 0, 0)),
            pl.BlockSpec((1, 2, half, CMP_HIDDEN), lambda b, kv, g: (kv, 0, 0, 0)),
            pl.BlockSpec((1, CMP_HIDDEN, LANES), lambda b, kv, g: (kv, 0, 0)),
            pl.BlockSpec((1, HEAD_DIM), lambda b, kv, g: (0, 0)),
        ],
        out_specs=(pl.BlockSpec((1, 1, 1, NC_PAD, HEAD_DIM), lambda b, kv, g: (b, kv, g, 0, 0)),
                   pl.BlockSpec((1, 1, 1, HEAD_DIM, NC_PAD), lambda b, kv, g: (b, kv, g, 0, 0))),
        out_shape=(jax.ShapeDtypeStruct((B, 2, KV_GROUPS, NC_PAD, HEAD_DIM), BF16),
                   jax.ShapeDtypeStruct((B, 2, KV_GROUPS, HEAD_DIM, NC_PAD), BF16)),
        compiler_params=_params(3),
        name="compress",
    )(tkv16, pos, w1, w2p, gain)


def _cmp_attn_kernel(qt_ref, kc_ref, vct_ref, ovl_ref, o_ref, sel_ref, score_ref, rank_ref, *, T, n_cmp):
    R = GROUP
    I = pl.program_id(2)
    kc = kc_ref[0, 0, 0]
    vct = vct_ref[0, 0, 0]
    c_idx = lax.broadcasted_iota(jnp.int32, (NC_PAD, T), 0)
    t_pos = I * T + lax.broadcasted_iota(jnp.int32, (NC_PAD, T), 1)
    cmask = (c_idx * CMP_STRIDE + (CMP_BLOCK - 1) <= t_pos) & (c_idx < n_cmp)
    scores = [jnp.dot(kc, qt_ref[0, 0, r, 0], preferred_element_type=F32) for r in range(R)]
    probs = []
    for r in range(R):
        s = jnp.where(cmask, scores[r], NEG_INF)
        m = jnp.max(s, axis=0, keepdims=True)
        e = jnp.where(cmask, jnp.exp2(s - m), 0.0)
        l = jnp.sum(e, axis=0, keepdims=True)
        probs.append(e * (1.0 / jnp.where(l > 0.0, l, 1.0)))
    outs = [jnp.dot(vct, p.astype(BF16), preferred_element_type=F32) for p in probs]
    o_ref[0] = jnp.concatenate(outs, axis=0).T
    psum = (probs[0] + probs[1]) + (probs[2] + probs[3])

    p_hi = psum.astype(BF16)
    p_lo = (psum - p_hi.astype(F32)).astype(BF16)
    ovl = ovl_ref[...]
    imp = jnp.dot(ovl, p_hi, preferred_element_type=F32) + jnp.dot(ovl, p_lo, preferred_element_type=F32)
    j_idx = lax.broadcasted_iota(jnp.int32, (N_SEL_BLK, T), 0)
    tq = I * T + lax.broadcasted_iota(jnp.int32, (N_SEL_BLK, T), 1)
    cur = tq // SEL_BLOCK
    valid = j_idx * SEL_BLOCK <= tq
    forced = (j_idx == 0) | (j_idx == cur) | (j_idx == cur - 1)
    score = jnp.where(valid, imp + jnp.where(forced, FORCE_BONUS, 0.0), NEG_INF)
    score_ref[...] = score
    rank_ref[...] = jnp.zeros(rank_ref.shape, F32)
    n_live = (I + 1) * (T // SEL_BLOCK)
    sub = 8
    j_loc = lax.broadcasted_iota(jnp.int32, (sub, T), 0)
    for c in range(N_SEL_BLK // sub):
        @pl.when(c * sub < n_live)
        def _(c=c):
            n_grp = N_SEL_BLK // sub
            grps = [score_ref[g * sub:(g + 1) * sub, :] for g in range(n_grp)]
            cnts = [rank_ref[g * sub:(g + 1) * sub, :] for g in range(n_grp)]
            for jp in range(c * sub, (c + 1) * sub):
                row = jnp.broadcast_to(score_ref[jp:jp + 1, :], (sub, T))
                for g in range(n_grp):
                    if g * sub > jp:
                        beats = row >= grps[g]
                    elif (g + 1) * sub - 1 < jp:
                        beats = row > grps[g]
                    else:
                        beats = (row > grps[g]) | ((row == grps[g]) & (j_loc + g * sub > jp))
                    cnts[g] = cnts[g] + jnp.where(beats, 1.0, 0.0)
            for g in range(n_grp):
                rank_ref[g * sub:(g + 1) * sub, :] = cnts[g]
    sel_ref[0, 0, 0] = jnp.where(rank_ref[...] < N_SELECT, 0.0, NEG_INF)


def _cmp_attn(qct, kc, vct, ovl, n_cmp):
    B, G, R, NQ, _, T = qct.shape
    S = NQ * T
    return pl.pallas_call(
        functools.partial(_cmp_attn_kernel, T=T, n_cmp=n_cmp),
        grid=(B, G, S // T),
        in_specs=[
            pl.BlockSpec((1, 1, R, 1, HEAD_DIM, T), lambda b, g, i: (b, g, 0, i, 0, 0)),
            pl.BlockSpec((1, 1, 1, NC_PAD, HEAD_DIM), lambda b, g, i: (b, 0, g, 0, 0)),
            pl.BlockSpec((1, 1, 1, HEAD_DIM, NC_PAD), lambda b, g, i: (b, 1, g, 0, 0)),
            pl.BlockSpec((N_SEL_BLK, NC_PAD), lambda b, g, i: (0, 0)),
        ],
        out_specs=(
            pl.BlockSpec((1, T, R * HEAD_DIM), lambda b, g, i: (b, i, g)),
            pl.BlockSpec((1, 1, 1, N_SEL_BLK, T), lambda b, g, i: (b, g, i, 0, 0)),
        ),
        out_shape=(
            jax.ShapeDtypeStruct((B, S, G * R * HEAD_DIM), F32),
            jax.ShapeDtypeStruct((B, G, NQ, N_SEL_BLK, T), F32),
        ),
        scratch_shapes=[pltpu.VMEM((N_SEL_BLK, T), F32), pltpu.VMEM((N_SEL_BLK, T), F32)],
        compiler_params=_params(3),
        name="cmp_attn",
    )(qct, kc, vct, ovl)


def _flash_kernel(*refs, R, T, NQ, Dv, nrel, k_per_r, shared_bias, diff, has_sel, lam_init):
    it = iter(refs)
    qt_ref, k_ref, vt_ref, bt_ref = (next(it) for _ in range(4))
    sel_ref = next(it) if has_sel else None
    dl_ref = next(it) if diff else None
    sub_ref = next(it) if diff else None
    o_ref, m_ref, acc_ref, s_ref, p_ref, al_ref = (next(it) for _ in range(6))

    m_ref[...] = jnp.full(m_ref.shape, NEG_INF, F32)
    acc_ref[...] = jnp.zeros(acc_ref.shape, F32)
    blk_per_tile = T // SEL_BLOCK
    bias_r = (lambda r: 0) if shared_bias else (lambda r: r)
    n_steps = NQ * (NQ + 1) // 2
    assert n_steps % 2 == 0

    def score_stage(I, J, slot):
        off = pl.multiple_of(J * T, T)
        for r in range(R):
            kt = k_ref[0, 0, r, pl.ds(off, T), :] if k_per_r else k_ref[0, 0, pl.ds(off, T), :]
            s_ref[slot, r] = jnp.dot(kt, qt_ref[0, 0, r, I], preferred_element_type=F32)

    def softmax_stage(I, J, slot):
        rel = jnp.minimum(I - J, nrel - 1)
        if has_sel:
            mask = jnp.concatenate(
                [jnp.broadcast_to(sel_ref[0, 0, I, pl.ds(J * blk_per_tile + i, 1), :], (SEL_BLOCK, T))
                 for i in range(blk_per_tile)], axis=0)
        for r in range(R):
            s = s_ref[slot, r] + bt_ref[0, rel, bias_r(r)]
            if has_sel:
                s = s + mask
            m_prev = m_ref[I, r]
            m_new = jnp.maximum(m_prev, jnp.max(s, axis=0, keepdims=True))
            al_ref[slot, r] = jnp.exp2(m_prev - m_new)
            p_ref[slot, r] = jnp.exp2(s - m_new).astype(BF16)
            m_ref[I, r] = m_new

    def value_stage(J, slot):
        vt = vt_ref[0, 0, J]
        return [jnp.dot(vt, p_ref[slot, r], preferred_element_type=F32) for r in range(R)]

    def accumulate(I, slot, pvs):
        for r in range(R):
            acc_ref[I, r] = al_ref[slot, r] * acc_ref[I, r] + pvs[r]

    def advance(I, J):
        last = J == I
        return jnp.where(last, I + 1, I), jnp.where(last, 0, J + 1)

    def half_step(cur, prev, slot):
        other = 1 - slot
        nxt = advance(*cur)
        pvs = value_stage(prev[1], other)
        score_stage(jnp.minimum(nxt[0], NQ - 1), nxt[1], other)
        softmax_stage(cur[0], cur[1], slot)
        accumulate(prev[0], other, pvs)
        return nxt

    def pair(u, carry):
        cur, prev = carry[:2], carry[2:]
        mid = half_step(cur, prev, 0)
        nxt = half_step(mid, cur, 1)
        return (*nxt, *mid)

    zero = jnp.int32(0)
    score_stage(zero, zero, 0)
    p_ref[1] = jnp.zeros(p_ref.shape[1:], BF16)
    al_ref[1] = jnp.ones(al_ref.shape[1:], F32)
    lax.fori_loop(0, n_steps // 2, pair, (zero, zero, zero, zero))
    accumulate(NQ - 1, 1, value_stage(NQ - 1, 1))

    if diff:
        dl = dl_ref[...]
        lam = (jnp.exp(jnp.sum(dl[0:1] * dl[1:2], keepdims=True))
               - jnp.exp(jnp.sum(dl[2:3] * dl[3:4], keepdims=True)) + lam_init)

    def finish(I, carry):
        outs = []
        for r in range(R):
            acc = acc_ref[I, r]
            outs.append(acc[:Dv] / acc[Dv:Dv + 1])
        rows = pl.ds(pl.multiple_of(I * T, T), T)
        if diff:
            o = (outs[0] - lam * outs[1]).T
            o_ref[0, rows, :] = _rms(o, sub_ref[...]) * (1.0 - lam_init)
        else:
            o_ref[0, rows, :] = jnp.concatenate(outs, axis=0).T
        return carry

    lax.fori_loop(0, NQ, finish, 0)


def _flash_attn(qt, k, vt, bias_tab, *, sel=None, dl=None, subln=None, lam_init=0.0, name="attn"):
    B, G, R, NQ, _, T = qt.shape
    S = NQ * T
    Dv = vt.shape[-2] - ONES_ROWS
    nrel, bias_heads = bias_tab.shape[1], bias_tab.shape[2]
    k_per_r = k.ndim == 5
    has_sel = sel is not None
    diff = dl is not None
    ins = [qt, k, vt, bias_tab]
    in_specs = [
        pl.BlockSpec((1, 1, R, NQ, HEAD_DIM, T), lambda b, g: (b, g, 0, 0, 0, 0)),
        (pl.BlockSpec((1, 1, R, S, HEAD_DIM), lambda b, g: (b, g, 0, 0, 0)) if k_per_r
         else pl.BlockSpec((1, 1, S, HEAD_DIM), lambda b, g: (b, g, 0, 0))),
        pl.BlockSpec((1, 1, NQ, Dv + ONES_ROWS, T), lambda b, g: (b, g, 0, 0, 0)),
        pl.BlockSpec((1, nrel, bias_heads, T, T), lambda b, g: (g, 0, 0, 0, 0)),
    ]
    if has_sel:
        ins.append(sel)
        in_specs.append(pl.BlockSpec((1, 1, NQ, N_SEL_BLK, T), lambda b, g: (b, g, 0, 0, 0)))
    if diff:
        ins += [dl, subln]
        in_specs += [pl.BlockSpec((4, HEAD_DIM), lambda b, g: (0, 0)),
                     pl.BlockSpec((1, A_VDIM), lambda b, g: (0, 0))]
    out_w = Dv if diff else R * Dv
    kern = functools.partial(_flash_kernel, R=R, T=T, NQ=NQ, Dv=Dv, nrel=nrel, k_per_r=k_per_r,
                             shared_bias=bias_heads == 1, diff=diff, has_sel=has_sel, lam_init=lam_init)
    return pl.pallas_call(
        kern,
        grid=(B, G),
        in_specs=in_specs,
        out_specs=pl.BlockSpec((1, S, out_w), lambda b, g: (b, 0, g)),
        out_shape=jax.ShapeDtypeStruct((B, S, G * out_w), F32),
        scratch_shapes=[pltpu.VMEM((NQ, R, 1, T), F32), pltpu.VMEM((NQ, R, Dv + ONES_ROWS, T), F32),
                        pltpu.VMEM((2, R, T, T), F32), pltpu.VMEM((2, R, T, T), BF16),
                        pltpu.VMEM((2, R, 1, T), F32)],
        compiler_params=_params(2),
        name=name,
    )(*ins)


def _window_kernel(*refs, R, T, NK, TV, Dv, has_sink):
    it = iter(refs)
    qt_ref, k_ref, vt_ref, bt_ref = (next(it) for _ in range(4))
    sink_ref = next(it) if has_sink else None
    o_ref = next(it)
    I = pl.program_id(2)
    entry = jnp.minimum(I, (NK - T) // T)
    first_tile = I - entry
    kt = k_ref[0, 0, pl.ds(pl.multiple_of(first_tile * T, T), NK), :]
    scores = [jnp.dot(kt, qt_ref[0, 0, r, 0], preferred_element_type=F32) for r in range(R)]
    probs, maxes = [], []
    for r in range(R):
        s = scores[r] + bt_ref[0, entry, r]
        m = jnp.max(s, axis=0, keepdims=True)
        probs.append(jnp.exp2(s - m).astype(BF16))
        maxes.append(m)
    outs = []
    for r in range(R):
        acc = None
        for c in range(NK // TV):
            part = jnp.dot(vt_ref[0, 0, first_tile * (T // TV) + c], probs[r][c * TV:(c + 1) * TV],
                           preferred_element_type=F32)
            acc = part if acc is None else acc + part
        num, l = acc[:Dv], acc[Dv:Dv + 1]
        if has_sink:
            sk = sink_ref[0, r] * LOG2E
            m_f = jnp.maximum(maxes[r], sk)
            w = jnp.exp2(maxes[r] - m_f)
            outs.append(num * w / (l * w + jnp.exp2(sk - m_f)))
        else:
            outs.append(num / l)
    o_ref[0] = jnp.concatenate(outs, axis=0).T


def _window_attn(qt, k, vt, bias_tab, *, sink=None, name="attn_window"):
    B, G, R, NQ, _, T = qt.shape
    S = NQ * T
    TV = vt.shape[-1]
    Dv = vt.shape[-2] - ONES_ROWS
    entries, NK = bias_tab.shape[1], bias_tab.shape[3]
    has_sink = sink is not None
    ins = [qt, k, vt, bias_tab]
    in_specs = [
        pl.BlockSpec((1, 1, R, 1, HEAD_DIM, T), lambda b, g, i: (b, g, 0, i, 0, 0)),
        pl.BlockSpec((1, 1, S, HEAD_DIM), lambda b, g, i: (b, g, 0, 0)),
        pl.BlockSpec((1, 1, S // TV, Dv + ONES_ROWS, TV), lambda b, g, i: (b, g, 0, 0, 0)),
        pl.BlockSpec((1, entries, R, NK, T), lambda b, g, i: (g, 0, 0, 0, 0)),
    ]
    if has_sink:
        ins.append(sink)
        in_specs.append(pl.BlockSpec((1, R, 1, 1), lambda b, g, i: (g, 0, 0, 0)))
    kern = functools.partial(_window_kernel, R=R, T=T, NK=NK, TV=TV, Dv=Dv, has_sink=has_sink)
    return pl.pallas_call(
        kern,
        grid=(B, G, S // T),
        in_specs=in_specs,
        out_specs=pl.BlockSpec((1, T, R * Dv), lambda b, g, i: (b, i, g)),
        out_shape=jax.ShapeDtypeStruct((B, S, G * R * Dv), F32),
        compiler_params=_params(3),
        name=name,
    )(*ins)


def _bucket_np(dist):
    n = np.maximum(dist, 0)
    max_exact = NUM_BUCKETS // 2
    nf = np.maximum(n, 1).astype(np.float32)
    large = max_exact + (np.log(nf / max_exact) / math.log(MAX_DISTANCE / max_exact)
                         * (NUM_BUCKETS - max_exact)).astype(np.int32)
    large = np.minimum(large, NUM_BUCKETS - 1)
    return np.where(n < max_exact, n, large)


def _bucket_starts():
    b = _bucket_np(np.arange(4 * MAX_DISTANCE))
    assert (np.diff(b) >= 0).all() and b[-1] == NUM_BUCKETS - 1
    return [int(np.argmax(b >= i)) for i in range(NUM_BUCKETS)]


def _bias_kernel(tab_ref, o_ref, *, T, rows, nrel, window, starts):
    h = pl.program_id(0)
    key = lax.broadcasted_iota(jnp.int32, (rows, T), 0)
    qry = lax.broadcasted_iota(jnp.int32, (rows, T), 1)
    for rel in range(nrel):
        dist = rel * T + qry - key
        val = jnp.full((rows, T), tab_ref[0, h] * LOG2E, F32)
        for b in range(1, NUM_BUCKETS):
            val = jnp.where(dist >= starts[b], tab_ref[b, h] * LOG2E, val)
        allowed = dist >= 0
        if window is not None:
            allowed = allowed & (dist < window)
        o_ref[0, rel, 0] = jnp.where(allowed, val, NEG_INF)


def _bias_tiles(tab, T, nrel, window, R, rows=None):
    H = tab.shape[1]
    rows = T if rows is None else rows
    return pl.pallas_call(
        functools.partial(_bias_kernel, T=T, rows=rows, nrel=nrel, window=window, starts=_bucket_starts()),
        grid=(H,),
        in_specs=[pl.BlockSpec(memory_space=pltpu.SMEM)],
        out_specs=pl.BlockSpec((1, nrel, 1, rows, T), lambda h: (h // R, 0, h % R, 0, 0)),
        out_shape=jax.ShapeDtypeStruct((H // R, nrel, R, rows, T), F32),
        compiler_params=_params(1),
        name="bias_tiles",
    )(tab)


def _merge_kernel(x_ref, oa_ref, ob_ref, oc_ref, os_ref, ow_ref, gc_ref, gm_ref, ex_ref, wb_ref, wo_ref, o_ref):
    gc = gc_ref[...]
    g1 = gc.astype(BF16)
    r1 = gc - g1.astype(F32)
    g2 = r1.astype(BF16)
    g3 = (r1 - g2.astype(F32)).astype(BF16)
    ex = ex_ref[...]
    gx = (jnp.dot(g1, ex, preferred_element_type=F32) + jnp.dot(g2, ex, preferred_element_type=F32)
          + jnp.dot(g3, ex, preferred_element_type=F32))
    oc = (gx[:, 0:MIX_WIDTH] * oc_ref[...] + gx[:, MIX_WIDTH:2 * MIX_WIDTH] * os_ref[...]
          + gx[:, 2 * MIX_WIDTH:3 * MIX_WIDTH] * ow_ref[...])
    z = None
    for n, br in enumerate((oa_ref[...], ob_ref[...], oc)):
        y = jnp.dot(br.astype(BF16), wb_ref[n], preferred_element_type=F32)
        t = gm_ref[:, n * D_MODEL:(n + 1) * D_MODEL] * y
        z = t if z is None else z + t
    o_ref[...] = x_ref[...] + jnp.dot(z.astype(BF16), wo_ref[...], preferred_element_type=F32)


def _merge(x2, oa, ob, oc, osel, ow, gc, gm, expand_g, wb, wo, tm=256):
    Tn = x2.shape[0]
    row = lambda d: pl.BlockSpec((tm, d), lambda i: (i, 0))
    return pl.pallas_call(
        _merge_kernel,
        grid=(Tn // tm,),
        in_specs=[row(D_MODEL), row(MIX_WIDTH), row(MIX_WIDTH), row(MIX_WIDTH), row(MIX_WIDTH), row(MIX_WIDTH),
                  row(CG_PAD), row(3 * D_MODEL),
                  pl.BlockSpec((CG_PAD, 3 * MIX_WIDTH), lambda i: (0, 0)),
                  pl.BlockSpec((3, MIX_WIDTH, D_MODEL), lambda i: (0, 0, 0)),
                  pl.BlockSpec((D_MODEL, D_MODEL), lambda i: (0, 0))],
        out_specs=row(D_MODEL),
        out_shape=jax.ShapeDtypeStruct((Tn, D_MODEL), F32),
        compiler_params=_params(1),
        name="merge",
    )(x2, oa, ob, oc, osel, ow, gc, gm, expand_g, wb, wo)


def _mlp_kernel(x_ref, nrm_ref, wu_ref, wd_ref, o_ref, h_ref, acc_ref):
    f = pl.program_id(1)

    @pl.when(f == 0)
    def _():
        h_ref[...] = _rms(x_ref[...], nrm_ref[...]).astype(BF16)
        acc_ref[...] = jnp.zeros(acc_ref.shape, F32)

    u = jnp.dot(h_ref[...], wu_ref[...], preferred_element_type=F32)
    u = jnp.square(jnp.maximum(u, 0.0)).astype(BF16)
    acc_ref[...] += jnp.dot(u, wd_ref[...], preferred_element_type=F32)

    @pl.when(f == pl.num_programs(1) - 1)
    def _():
        o_ref[...] = x_ref[...] + acc_ref[...]


def _mlp(x2, nrm, wu, wd, tm=1024, tf=512):
    Tn = x2.shape[0]
    return pl.pallas_call(
        _mlp_kernel,
        grid=(Tn // tm, D_FF // tf),
        in_specs=[pl.BlockSpec((tm, D_MODEL), lambda i, f: (i, 0)),
                  pl.BlockSpec((1, D_MODEL), lambda i, f: (0, 0)),
                  pl.BlockSpec((D_MODEL, tf), lambda i, f: (0, f)),
                  pl.BlockSpec((tf, D_MODEL), lambda i, f: (f, 0))],
        out_specs=pl.BlockSpec((tm, D_MODEL), lambda i, f: (i, 0)),
        out_shape=jax.ShapeDtypeStruct((Tn, D_MODEL), F32),
        scratch_shapes=[pltpu.VMEM((tm, D_MODEL), BF16), pltpu.VMEM((tm, D_MODEL), F32)],
        compiler_params=_params(2),
        name="mlp",
    )(x2, nrm, wu, wd)


def _overlap(n_cmp):
    c_start = np.arange(NC_PAD) * CMP_STRIDE
    j_start = np.arange(N_SEL_BLK) * SEL_BLOCK
    ov = ((c_start[None, :] < j_start[:, None] + SEL_BLOCK) & (c_start[None, :] + CMP_BLOCK > j_start[:, None])
          & (np.arange(NC_PAD)[None, :] < n_cmp))
    return jnp.asarray(ov.astype(np.float32), BF16)


def _gate_expand():
    e = np.zeros((CG_PAD, 3 * MIX_WIDTH), np.float32)
    for h in range(C_HEADS):
        for j in range(3):
            e[h * 3 + j, j * MIX_WIDTH + h * HEAD_DIM:j * MIX_WIDTH + (h + 1) * HEAD_DIM] = 1.0
    return jnp.asarray(e, BF16)


def kernel(x, w_in, qk_gain, diff_lambda, diff_subln, sinks, cmp_pos, cmp_w1, cmp_w2,
           w_branch, w_out, norm_mix, norm_mlp, w_up, w_down, rel_bias):
    B, S, _ = x.shape
    depth = w_in.shape[0]
    n_cmp = (S - CMP_BLOCK) // CMP_STRIDE + 1
    assert S % T_ROW == 0 and S // CMP_STRIDE == NC_PAD and S // SEL_BLOCK == N_SEL_BLK
    half = CMP_BLOCK // 2 * HEAD_DIM

    w_in_p = jnp.concatenate(
        [w_in[:, :, :C_CG], jnp.pad(w_in[:, :, C_CG:C_CG + N_CG], ((0, 0), (0, 0), (0, CG_PAD - N_CG))),
         w_in[:, :, C_CG + N_CG:]], axis=-1).astype(BF16)
    gains2 = jnp.concatenate([qk_gain, qk_gain], axis=-1)
    w1 = cmp_w1.astype(BF16).reshape(depth, 2, 2, half, CMP_HIDDEN)
    w2p = jnp.pad(cmp_w2, ((0, 0), (0, 0), (0, 0), (0, LANES - HEAD_DIM))).astype(BF16)
    pos = cmp_pos.reshape(depth, 2, 2, half)
    wb = w_branch.astype(BF16)
    wo = w_out.astype(BF16)
    wu = w_up.astype(BF16)
    wd = w_down.astype(BF16)

    bias_a = rel_bias[:, :A_HEADS]
    bias_b = rel_bias[:, A_HEADS:A_HEADS + B_HEADS]
    bias_c = rel_bias[:, A_HEADS + B_HEADS:]
    bt_a = _bias_tiles(bias_a, T_ROW, 3, None, 1)
    bt_b = _bias_tiles(bias_b, T_SWA, B_WINDOW // T_SWA + 1, B_WINDOW, GROUP, rows=B_WINDOW + T_SWA)
    bt_sel = _bias_tiles(bias_c, T_ROW, 3, None, GROUP)
    bt_win = _bias_tiles(bias_c, T_ROW, C_WINDOW // T_ROW + 1, C_WINDOW, GROUP, rows=C_WINDOW + T_ROW)
    ovl = _overlap(n_cmp)
    expand_g = _gate_expand()

    for layer in range(depth):
        lam_init = 0.8 - 0.6 * math.exp(-0.3 * layer)
        (qat, ka, vat, qbt, kb, vbt, qct, tkv, ksel, vselt, kwin, vwint, gc, gm) = _inproj(
            x, norm_mix[layer][None], w_in_p[layer], gains2[layer])
        oa = _flash_attn(qat, ka, vat, bt_a, dl=diff_lambda[layer],
                   subln=diff_subln[layer][None], lam_init=lam_init, name="attn_diff")
        ob = _window_attn(qbt, kb, vbt, bt_b,
                          sink=sinks[layer].reshape(KV_GROUPS, GROUP, 1, 1), name="attn_swa")
        kc, vct = _compress(tkv.reshape(B, 2, KV_GROUPS, NC_PAD, half), pos[layer],
                            w1[layer], w2p[layer], qk_gain[layer][5:6])
        ocmp, sel = _cmp_attn(qct, kc, vct, ovl, n_cmp)
        osel = _flash_attn(qct, ksel, vselt, bt_sel, sel=sel, name="attn_sel")
        owin = _window_attn(qct, kwin, vwint, bt_win, name="attn_win")
        f2 = lambda a: a.reshape(B * S, a.shape[-1])
        x2 = _merge(f2(x), f2(oa), f2(ob), f2(ocmp), f2(osel), f2(owin), f2(gc), f2(gm),
                    expand_g, wb[layer], wo[layer])
        x2 = _mlp(x2, norm_mlp[layer][None], wu[layer], wd[layer])
        x = x2.reshape(B, S, D_MODEL)
    return x
```

```python
import functools
import math

import numpy as np
import jax
import jax.numpy as jnp
from jax import lax
from jax.experimental import pallas as pl
from jax.experimental.pallas import tpu as pltpu

F32 = jnp.float32
BF16 = jnp.bfloat16

D_MODEL = 1024
HEAD_DIM = 64
A_HEADS = 4
A_VDIM = 128
B_HEADS = 8
B_WINDOW = 128
C_HEADS = 8
KV_GROUPS = 2
GROUP = 4
CMP_BLOCK = 32
CMP_STRIDE = 16
CMP_HIDDEN = 256
SEL_BLOCK = 64
N_SELECT = 16
C_WINDOW = 512
MIX_WIDTH = 512
D_FF = 4096
NUM_BUCKETS = 32
MAX_DISTANCE = 128
NEG_INF = -1e30
FORCE_BONUS = 1e4
EPS = 1e-6
LOG2E = 1.4426950408889634
QK_SCALE = HEAD_DIM ** -0.5 * LOG2E

C_AQ, C_AK, C_AV, C_BQ, C_BKV, C_CQ, C_CKV, C_CG, C_MG = 0, 512, 1024, 1536, 2048, 2304, 2816, 3584, 3712
IN_COLS_PAD = C_MG + 3 * D_MODEL
N_CG = C_HEADS * 3
LANES = 128
CG_PAD = LANES
NC_PAD = 256
N_SEL_BLK = 64
ONES_ROWS = 16

T_ROW = 256
T_SWA = 256
TV_SWA = 128

VMEM_LIMIT = 48 * 1024 * 1024


def _rms(x, gain):
    ms = jnp.mean(x * x, axis=-1, keepdims=True)
    return (x * lax.rsqrt(ms + EPS)) * gain


def _params(n_axes):
    return pltpu.CompilerParams(dimension_semantics=("arbitrary",) * n_axes,
                                vmem_limit_bytes=VMEM_LIMIT)


def _inproj_kernel(x_ref, nrm_ref, w_ref, gains_ref,
                   qa_ref, ka_ref, va_ref, qb_ref, kb_ref, vb_ref, qc_ref,
                   tkv_ref, ksel_ref, vsel_ref, kwin_ref, vwin_ref, gc_ref, gm_ref):
    tm = x_ref.shape[1]
    hb = _rms(x_ref[0], nrm_ref[...]).astype(BF16)
    lane = lax.broadcasted_iota(jnp.int32, (tm, LANES), 1)
    first = lane < HEAD_DIM

    def ones_pad(width):
        return jnp.where(lax.broadcasted_iota(jnp.int32, (ONES_ROWS, width), 0) == 0, 1.0, 0.0).astype(BF16)

    def mm(c0, n):
        return jnp.dot(hb, w_ref[:, c0:c0 + n], preferred_element_type=F32)

    def norm_pair(y, gain_idx, scale=None):
        sq = y * y
        ms0 = jnp.sum(jnp.where(first, sq, 0.0), axis=-1, keepdims=True) * (1.0 / HEAD_DIM)
        ms1 = jnp.sum(jnp.where(first, 0.0, sq), axis=-1, keepdims=True) * (1.0 / HEAD_DIM)
        inv = jnp.where(first, lax.rsqrt(ms0 + EPS), lax.rsqrt(ms1 + EPS))
        out = (y * inv) * gains_ref[gain_idx:gain_idx + 1, :]
        return out if scale is None else out * scale

    def pairs(c0):
        r = mm(c0, 256)
        return r[:, :LANES], r[:, LANES:]

    def put_qt(ref, i0, i1, y):
        yt = y.T.astype(BF16)
        n_tiles = ref.shape[3]
        tq = tm // n_tiles
        for t in range(n_tiles):
            ref[0, i0, i1, t] = yt[:HEAD_DIM, tq * t:tq * (t + 1)]
            ref[0, i0, i1 + 1, t] = yt[HEAD_DIM:, tq * t:tq * (t + 1)]

    def put_vt(ref, g0, y, n_tiles):
        yt = y.T.astype(BF16)
        tk = tm // n_tiles
        for g in range(2):
            for t in range(n_tiles):
                ref[0, g0 + g, t, 0:HEAD_DIM, :] = yt[HEAD_DIM * g:HEAD_DIM * (g + 1), tk * t:tk * (t + 1)]
                ref[0, g0 + g, t, HEAD_DIM:HEAD_DIM + ONES_ROWS, :] = ones_pad(tk)

    for ch in range(2):
        for half, y in enumerate(pairs(C_AQ + 256 * ch)):
            put_qt(qa_ref, ch * 2 + half, 0, norm_pair(y, 0, QK_SCALE))
        for half, y in enumerate(pairs(C_AK + 256 * ch)):
            y = norm_pair(y, 1).astype(BF16)
            ka_ref[0, ch * 2 + half, 0] = y[:, :HEAD_DIM]
            ka_ref[0, ch * 2 + half, 1] = y[:, HEAD_DIM:]
        for half, y in enumerate(pairs(C_AV + 256 * ch)):
            va_ref[0, ch * 2 + half, 0, 0:A_VDIM, :] = y.T.astype(BF16)
            va_ref[0, ch * 2 + half, 0, A_VDIM:A_VDIM + ONES_ROWS, :] = ones_pad(tm)
        for half, y in enumerate(pairs(C_BQ + 256 * ch)):
            put_qt(qb_ref, ch, 2 * half, norm_pair(y, 2, QK_SCALE))
        for half, y in enumerate(pairs(C_CQ + 256 * ch)):
            put_qt(qc_ref, ch, 2 * half, norm_pair(y, 4, QK_SCALE))

    def put_k(ref, y):
        y = y.astype(BF16)
        ref[0, 0] = y[:, :HEAD_DIM]
        ref[0, 1] = y[:, HEAD_DIM:]

    yk, yv = pairs(C_BKV)
    put_k(kb_ref, norm_pair(yk, 3))
    put_vt(vb_ref, 0, yv, tm // TV_SWA)
    y0, y1 = pairs(C_CKV)
    for kv, y in enumerate((y0, y1)):
        tkv_ref[0, kv, 0] = y[:, :HEAD_DIM]
        tkv_ref[0, kv, 1] = y[:, HEAD_DIM:]
    yk, yv = pairs(C_CKV + 256)
    put_k(ksel_ref, norm_pair(yk, 6))
    put_vt(vsel_ref, 0, yv, 1)
    yk, yv = pairs(C_CKV + 512)
    put_k(kwin_ref, norm_pair(yk, 7))
    put_vt(vwin_ref, 0, yv, 1)

    gc_ref[0] = jax.nn.sigmoid(mm(C_CG, CG_PAD))
    for ch in range(3 * D_MODEL // 256):
        gm_ref[0, :, 256 * ch:256 * (ch + 1)] = jax.nn.sigmoid(mm(C_MG + 256 * ch, 256))


def _inproj(x, nrm, w, gains2):
    B, S, _ = x.shape
    tm = T_ROW
    nt = S // tm
    sd = jax.ShapeDtypeStruct
    qt = lambda a, b, tq: sd((B, a, b, S // tq, HEAD_DIM, tq), BF16)
    kk = lambda n: sd((B, n, S, HEAD_DIM), BF16)
    vt = lambda n, dv, tk: sd((B, n, S // tk, dv + ONES_ROWS, tk), BF16)
    out_shape = (
        qt(A_HEADS, 2, tm), sd((B, A_HEADS, 2, S, HEAD_DIM), BF16), vt(A_HEADS, A_VDIM, tm),
        qt(KV_GROUPS, GROUP, T_SWA), kk(KV_GROUPS), vt(KV_GROUPS, HEAD_DIM, TV_SWA),
        qt(KV_GROUPS, GROUP, tm),
        sd((B, 2, KV_GROUPS, S, HEAD_DIM), F32),
        kk(KV_GROUPS), vt(KV_GROUPS, HEAD_DIM, tm), kk(KV_GROUPS), vt(KV_GROUPS, HEAD_DIM, tm),
        sd((B, S, CG_PAD), F32),
        sd((B, S, 3 * D_MODEL), F32),
    )
    s_qt = lambda a, b, tq: pl.BlockSpec((1, a, b, tm // tq, HEAD_DIM, tq), lambda b_, i: (b_, 0, 0, i, 0, 0))
    s_k = lambda n: pl.BlockSpec((1, n, tm, HEAD_DIM), lambda b_, i: (b_, 0, i, 0))
    s_k5 = lambda a, c: pl.BlockSpec((1, a, c, tm, HEAD_DIM), lambda b_, i: (b_, 0, 0, i, 0))
    s_vt = lambda n, dv, tk: pl.BlockSpec((1, n, tm // tk, dv + ONES_ROWS, tk), lambda b_, i: (b_, 0, i, 0, 0))
    row = lambda d: pl.BlockSpec((1, tm, d), lambda b_, i: (b_, i, 0))
    out_specs = (
        s_qt(A_HEADS, 2, tm), s_k5(A_HEADS, 2), s_vt(A_HEADS, A_VDIM, tm),
        s_qt(KV_GROUPS, GROUP, T_SWA), s_k(KV_GROUPS), s_vt(KV_GROUPS, HEAD_DIM, TV_SWA),
        s_qt(KV_GROUPS, GROUP, tm), s_k5(2, KV_GROUPS),
        s_k(KV_GROUPS), s_vt(KV_GROUPS, HEAD_DIM, tm), s_k(KV_GROUPS), s_vt(KV_GROUPS, HEAD_DIM, tm),
        row(CG_PAD), row(3 * D_MODEL),
    )
    return pl.pallas_call(
        _inproj_kernel,
        grid=(B, nt),
        in_specs=[
            row(D_MODEL),
            pl.BlockSpec((1, D_MODEL), lambda b_, i: (0, 0)),
            pl.BlockSpec((D_MODEL, IN_COLS_PAD), lambda b_, i: (0, 0), pipeline_mode=pl.Buffered(1)),
            pl.BlockSpec((8, LANES), lambda b_, i: (0, 0)),
        ],
        out_specs=out_specs,
        out_shape=out_shape,
        compiler_params=_params(2),
        name="inproj",
    )(x, nrm, w, gains2)


def _compress_kernel(t_ref, pos_ref, w1_ref, w2_ref, gain_ref, o_ref, ot_ref):
    t = t_ref[0, 0, 0]
    lo = (t + pos_ref[0, 0:1, :]).astype(BF16)
    hi = (t + pos_ref[0, 1:2, :]).astype(BF16)
    v = jnp.dot(lo, w1_ref[0, 0], preferred_element_type=F32)
    u = jnp.dot(hi, w1_ref[0, 1], preferred_element_type=F32)
    pre = v + pltpu.roll(u, NC_PAD - 1, 0)
    hcur = jax.nn.gelu(pre).astype(BF16)
    out = jnp.dot(hcur, w2_ref[0], preferred_element_type=F32)
    o64 = out[:, :HEAD_DIM]
    is_key = pl.program_id(1) == 0
    o_ref[0, 0, 0] = jnp.where(is_key, _rms(o64, gain_ref[...]), o64).astype(BF16)
    ot_ref[0, 0, 0] = out.T[:HEAD_DIM].astype(BF16)


def _compress(tkv16, pos, w1, w2p, gain):
    B = tkv16.shape[0]
    half = CMP_BLOCK // 2 * HEAD_DIM
    return pl.pallas_call(
        _compress_kernel,
        grid=(B, 2, KV_GROUPS),
        in_specs=[
            pl.BlockSpec((1, 1, 1, NC_PAD, half), lambda b, kv, g: (b, kv, g, 0, 0)),
            pl.BlockSpec((1, 2, half), lambda b, kv, g: (kv, 0, 0)),
            pl.BlockSpec((1, 2, half, CMP_HIDDEN), lambda b, kv, g: (kv, 0, 0, 0)),
            pl.BlockSpec((1, CMP_HIDDEN, LANES), lambda b, kv, g: (kv, 0, 0)),
            pl.BlockSpec((1, HEAD_DIM), lambda b, kv, g: (0, 0)),
        ],
        out_specs=(pl.BlockSpec((1, 1, 1, NC_PAD, HEAD_DIM), lambda b, kv, g: (b, kv, g, 0, 0)),
                   pl.BlockSpec((1, 1, 1, HEAD_DIM, NC_PAD), lambda b, kv, g: (b, kv, g, 0, 0))),
        out_shape=(jax.ShapeDtypeStruct((B, 2, KV_GROUPS, NC_PAD, HEAD_DIM), BF16),
                   jax.ShapeDtypeStruct((B, 2, KV_GROUPS, HEAD_DIM, NC_PAD), BF16)),
        compiler_params=_params(3),
        name="compress",
    )(tkv16, pos, w1, w2p, gain)


def _cmp_attn_kernel(qt_ref, kc_ref, vct_ref, ovl_ref, o_ref, sel_ref, score_ref, rank_ref, *, T, n_cmp):
    R = GROUP
    I = pl.program_id(2)
    kc = kc_ref[0, 0, 0]
    vct = vct_ref[0, 0, 0]
    c_idx = lax.broadcasted_iota(jnp.int32, (NC_PAD, T), 0)
    t_pos = I * T + lax.broadcasted_iota(jnp.int32, (NC_PAD, T), 1)
    cmask = (c_idx * CMP_STRIDE + (CMP_BLOCK - 1) <= t_pos) & (c_idx < n_cmp)
    scores = [jnp.dot(kc, qt_ref[0, 0, r, 0], preferred_element_type=F32) for r in range(R)]
    probs = []
    for r in range(R):
        s = jnp.where(cmask, scores[r], NEG_INF)
        m = jnp.max(s, axis=0, keepdims=True)
        e = jnp.where(cmask, jnp.exp2(s - m), 0.0)
        l = jnp.sum(e, axis=0, keepdims=True)
        probs.append(e * (1.0 / jnp.where(l > 0.0, l, 1.0)))
    outs = [jnp.dot(vct, p.astype(BF16), preferred_element_type=F32) for p in probs]
    o_ref[0] = jnp.concatenate(outs, axis=0).T
    psum = (probs[0] + probs[1]) + (probs[2] + probs[3])

    p_hi = psum.astype(BF16)
    p_lo = (psum - p_hi.astype(F32)).astype(BF16)
    ovl = ovl_ref[...]
    imp = jnp.dot(ovl, p_hi, preferred_element_type=F32) + jnp.dot(ovl, p_lo, preferred_element_type=F32)
    j_idx = lax.broadcasted_iota(jnp.int32, (N_SEL_BLK, T), 0)
    tq = I * T + lax.broadcasted_iota(jnp.int32, (N_SEL_BLK, T), 1)
    cur = tq // SEL_BLOCK
    valid = j_idx * SEL_BLOCK <= tq
    forced = (j_idx == 0) | (j_idx == cur) | (j_idx == cur - 1)
    score = jnp.where(valid, imp + jnp.where(forced, FORCE_BONUS, 0.0), NEG_INF)
    score_ref[...] = score
    rank_ref[...] = jnp.zeros(rank_ref.shape, F32)
    n_live = (I + 1) * (T // SEL_BLOCK)
    sub = 8
    j_loc = lax.broadcasted_iota(jnp.int32, (sub, T), 0)
    for c in range(N_SEL_BLK // sub):
        @pl.when(c * sub < n_live)
        def _(c=c):
            n_grp = N_SEL_BLK // sub
            grps = [score_ref[g * sub:(g + 1) * sub, :] for g in range(n_grp)]
            cnts = [rank_ref[g * sub:(g + 1) * sub, :] for g in range(n_grp)]
            for jp in range(c * sub, (c + 1) * sub):
                row = jnp.broadcast_to(score_ref[jp:jp + 1, :], (sub, T))
                for g in range(n_grp):
                    if g * sub > jp:
                        beats = row >= grps[g]
                    elif (g + 1) * sub - 1 < jp:
                        beats = row > grps[g]
                    else:
                        beats = (row > grps[g]) | ((row == grps[g]) & (j_loc + g * sub > jp))
                    cnts[g] = cnts[g] + jnp.where(beats, 1.0, 0.0)
            for g in range(n_grp):
                rank_ref[g * sub:(g + 1) * sub, :] = cnts[g]
    sel_ref[0, 0, 0] = jnp.where(rank_ref[...] < N_SELECT, 0.0, NEG_INF)


def _cmp_attn(qct, kc, vct, ovl, n_cmp):
    B, G, R, NQ, _, T = qct.shape
    S = NQ * T
    return pl.pallas_call(
        functools.partial(_cmp_attn_kernel, T=T, n_cmp=n_cmp),
        grid=(B, G, S // T),
        in_specs=[
            pl.BlockSpec((1, 1, R, 1, HEAD_DIM, T), lambda b, g, i: (b, g, 0, i, 0, 0)),
            pl.BlockSpec((1, 1, 1, NC_PAD, HEAD_DIM), lambda b, g, i: (b, 0, g, 0, 0)),
            pl.BlockSpec((1, 1, 1, HEAD_DIM, NC_PAD), lambda b, g, i: (b, 1, g, 0, 0)),
            pl.BlockSpec((N_SEL_BLK, NC_PAD), lambda b, g, i: (0, 0)),
        ],
        out_specs=(
            pl.BlockSpec((1, T, R * HEAD_DIM), lambda b, g, i: (b, i, g)),
            pl.BlockSpec((1, 1, 1, N_SEL_BLK, T), lambda b, g, i: (b, g, i, 0, 0)),
        ),
        out_shape=(
            jax.ShapeDtypeStruct((B, S, G * R * HEAD_DIM), F32),
            jax.ShapeDtypeStruct((B, G, NQ, N_SEL_BLK, T), F32),
        ),
        scratch_shapes=[pltpu.VMEM((N_SEL_BLK, T), F32), pltpu.VMEM((N_SEL_BLK, T), F32)],
        compiler_params=_params(3),
        name="cmp_attn",
    )(qct, kc, vct, ovl)


def _flash_kernel(*refs, R, T, NQ, Dv, nrel, k_per_r, shared_bias, diff, has_sel, lam_init):
    it = iter(refs)
    qt_ref, k_ref, vt_ref, bt_ref = (next(it) for _ in range(4))
    sel_ref = next(it) if has_sel else None
    dl_ref = next(it) if diff else None
    sub_ref = next(it) if diff else None
    o_ref, m_ref, acc_ref, s_ref, p_ref, al_ref = (next(it) for _ in range(6))

    m_ref[...] = jnp.full(m_ref.shape, NEG_INF, F32)
    acc_ref[...] = jnp.zeros(acc_ref.shape, F32)
    blk_per_tile = T // SEL_BLOCK
    bias_r = (lambda r: 0) if shared_bias else (lambda r: r)
    n_steps = NQ * (NQ + 1) // 2
    assert n_steps % 2 == 0

    def score_stage(I, J, slot):
        off = pl.multiple_of(J * T, T)
        for r in range(R):
            kt = k_ref[0, 0, r, pl.ds(off, T), :] if k_per_r else k_ref[0, 0, pl.ds(off, T), :]
            s_ref[slot, r] = jnp.dot(kt, qt_ref[0, 0, r, I], preferred_element_type=F32)

    def softmax_stage(I, J, slot):
        rel = jnp.minimum(I - J, nrel - 1)
        if has_sel:
            mask = jnp.concatenate(
                [jnp.broadcast_to(sel_ref[0, 0, I, pl.ds(J * blk_per_tile + i, 1), :], (SEL_BLOCK, T))
                 for i in range(blk_per_tile)], axis=0)
        for r in range(R):
            s = s_ref[slot, r] + bt_ref[0, rel, bias_r(r)]
            if has_sel:
                s = s + mask
            m_prev = m_ref[I, r]
            m_new = jnp.maximum(m_prev, jnp.max(s, axis=0, keepdims=True))
            al_ref[slot, r] = jnp.exp2(m_prev - m_new)
            p_ref[slot, r] = jnp.exp2(s - m_new).astype(BF16)
            m_ref[I, r] = m_new

    def value_stage(J, slot):
        vt = vt_ref[0, 0, J]
        return [jnp.dot(vt, p_ref[slot, r], preferred_element_type=F32) for r in range(R)]

    def accumulate(I, slot, pvs):
        for r in range(R):
            acc_ref[I, r] = al_ref[slot, r] * acc_ref[I, r] + pvs[r]

    def advance(I, J):
        last = J == I
        return jnp.where(last, I + 1, I), jnp.where(last, 0, J + 1)

    def half_step(cur, prev, slot):
        other = 1 - slot
        nxt = advance(*cur)
        pvs = value_stage(prev[1], other)
        score_stage(jnp.minimum(nxt[0], NQ - 1), nxt[1], other)
        softmax_stage(cur[0], cur[1], slot)
        accumulate(prev[0], other, pvs)
        return nxt

    def pair(u, carry):
        cur, prev = carry[:2], carry[2:]
        mid = half_step(cur, prev, 0)
        nxt = half_step(mid, cur, 1)
        return (*nxt, *mid)

    zero = jnp.int32(0)
    score_stage(zero, zero, 0)
    p_ref[1] = jnp.zeros(p_ref.shape[1:], BF16)
    al_ref[1] = jnp.ones(al_ref.shape[1:], F32)
    lax.fori_loop(0, n_steps // 2, pair, (zero, zero, zero, zero))
    accumulate(NQ - 1, 1, value_stage(NQ - 1, 1))

    if diff:
        dl = dl_ref[...]
        lam = (jnp.exp(jnp.sum(dl[0:1] * dl[1:2], keepdims=True))
               - jnp.exp(jnp.sum(dl[2:3] * dl[3:4], keepdims=True)) + lam_init)

    def finish(I, carry):
        outs = []
        for r in range(R):
            acc = acc_ref[I, r]
            outs.append(acc[:Dv] / acc[Dv:Dv + 1])
        rows = pl.ds(pl.multiple_of(I * T, T), T)
        if diff:
            o = (outs[0] - lam * outs[1]).T
            o_ref[0, rows, :] = _rms(o, sub_ref[...]) * (1.0 - lam_init)
        else:
            o_ref[0, rows, :] = jnp.concatenate(outs, axis=0).T
        return carry

    lax.fori_loop(0, NQ, finish, 0)


def _flash_attn(qt, k, vt, bias_tab, *, sel=None, dl=None, subln=None, lam_init=0.0, name="attn"):
    B, G, R, NQ, _, T = qt.shape
    S = NQ * T
    Dv = vt.shape[-2] - ONES_ROWS
    nrel, bias_heads = bias_tab.shape[1], bias_tab.shape[2]
    k_per_r = k.ndim == 5
    has_sel = sel is not None
    diff = dl is not None
    ins = [qt, k, vt, bias_tab]
    in_specs = [
        pl.BlockSpec((1, 1, R, NQ, HEAD_DIM, T), lambda b, g: (b, g, 0, 0, 0, 0)),
        (pl.BlockSpec((1, 1, R, S, HEAD_DIM), lambda b, g: (b, g, 0, 0, 0)) if k_per_r
         else pl.BlockSpec((1, 1, S, HEAD_DIM), lambda b, g: (b, g, 0, 0))),
        pl.BlockSpec((1, 1, NQ, Dv + ONES_ROWS, T), lambda b, g: (b, g, 0, 0, 0)),
        pl.BlockSpec((1, nrel, bias_heads, T, T), lambda b, g: (g, 0, 0, 0, 0)),
    ]
    if has_sel:
        ins.append(sel)
        in_specs.append(pl.BlockSpec((1, 1, NQ, N_SEL_BLK, T), lambda b, g: (b, g, 0, 0, 0)))
    if diff:
        ins += [dl, subln]
        in_specs += [pl.BlockSpec((4, HEAD_DIM), lambda b, g: (0, 0)),
                     pl.BlockSpec((1, A_VDIM), lambda b, g: (0, 0))]
    out_w = Dv if diff else R * Dv
    kern = functools.partial(_flash_kernel, R=R, T=T, NQ=NQ, Dv=Dv, nrel=nrel, k_per_r=k_per_r,
                             shared_bias=bias_heads == 1, diff=diff, has_sel=has_sel, lam_init=lam_init)
    return pl.pallas_call(
        kern,
        grid=(B, G),
        in_specs=in_specs,
        out_specs=pl.BlockSpec((1, S, out_w), lambda b, g: (b, 0, g)),
        out_shape=jax.ShapeDtypeStruct((B, S, G * out_w), F32),
        scratch_shapes=[pltpu.VMEM((NQ, R, 1, T), F32), pltpu.VMEM((NQ, R, Dv + ONES_ROWS, T), F32),
                        pltpu.VMEM((2, R, T, T), F32), pltpu.VMEM((2, R, T, T), BF16),
                        pltpu.VMEM((2, R, 1, T), F32)],
        compiler_params=_params(2),
        name=name,
    )(*ins)


def _window_kernel(*refs, R, T, NK, TV, Dv, has_sink):
    it = iter(refs)
    qt_ref, k_ref, vt_ref, bt_ref = (next(it) for _ in range(4))
    sink_ref = next(it) if has_sink else None
    o_ref = next(it)
    I = pl.program_id(2)
    entry = jnp.minimum(I, -(-(NK - T) // T))
    first_key = pl.multiple_of(I * T - jnp.minimum(entry * T, NK - T), TV)
    kt = k_ref[0, 0, pl.ds(first_key, NK), :]
    scores = [jnp.dot(kt, qt_ref[0, 0, r, 0], preferred_element_type=F32) for r in range(R)]
    probs, maxes = [], []
    for r in range(R):
        s = scores[r] + bt_ref[0, entry, r]
        m = jnp.max(s, axis=0, keepdims=True)
        probs.append(jnp.exp2(s - m).astype(BF16))
        maxes.append(m)
    outs = []
    for r in range(R):
        acc = None
        for c in range(NK // TV):
            part = jnp.dot(vt_ref[0, 0, first_key // TV + c], probs[r][c * TV:(c + 1) * TV],
                           preferred_element_type=F32)
            acc = part if acc is None else acc + part
        num, l = acc[:Dv], acc[Dv:Dv + 1]
        if has_sink:
            sk = sink_ref[0, r] * LOG2E
            m_f = jnp.maximum(maxes[r], sk)
            w = jnp.exp2(maxes[r] - m_f)
            outs.append(num * w / (l * w + jnp.exp2(sk - m_f)))
        else:
            outs.append(num / l)
    o_ref[0] = jnp.concatenate(outs, axis=0).T


def _window_attn(qt, k, vt, bias_tab, *, sink=None, name="attn_window"):
    B, G, R, NQ, _, T = qt.shape
    S = NQ * T
    TV = vt.shape[-1]
    Dv = vt.shape[-2] - ONES_ROWS
    entries, NK = bias_tab.shape[1], bias_tab.shape[3]
    has_sink = sink is not None
    ins = [qt, k, vt, bias_tab]
    in_specs = [
        pl.BlockSpec((1, 1, R, 1, HEAD_DIM, T), lambda b, g, i: (b, g, 0, i, 0, 0)),
        pl.BlockSpec((1, 1, S, HEAD_DIM), lambda b, g, i: (b, g, 0, 0)),
        pl.BlockSpec((1, 1, S // TV, Dv + ONES_ROWS, TV), lambda b, g, i: (b, g, 0, 0, 0)),
        pl.BlockSpec((1, entries, R, NK, T), lambda b, g, i: (g, 0, 0, 0, 0)),
    ]
    if has_sink:
        ins.append(sink)
        in_specs.append(pl.BlockSpec((1, R, 1, 1), lambda b, g, i: (g, 0, 0, 0)))
    kern = functools.partial(_window_kernel, R=R, T=T, NK=NK, TV=TV, Dv=Dv, has_sink=has_sink)
    return pl.pallas_call(
        kern,
        grid=(B, G, S // T),
        in_specs=in_specs,
        out_specs=pl.BlockSpec((1, T, R * Dv), lambda b, g, i: (b, i, g)),
        out_shape=jax.ShapeDtypeStruct((B, S, G * R * Dv), F32),
        compiler_params=_params(3),
        name=name,
    )(*ins)


def _bucket_np(dist):
    n = np.maximum(dist, 0)
    max_exact = NUM_BUCKETS // 2
    nf = np.maximum(n, 1).astype(np.float32)
    large = max_exact + (np.log(nf / max_exact) / math.log(MAX_DISTANCE / max_exact)
                         * (NUM_BUCKETS - max_exact)).astype(np.int32)
    large = np.minimum(large, NUM_BUCKETS - 1)
    return np.where(n < max_exact, n, large)


def _bucket_starts():
    b = _bucket_np(np.arange(4 * MAX_DISTANCE))
    assert (np.diff(b) >= 0).all() and b[-1] == NUM_BUCKETS - 1
    return [int(np.argmax(b >= i)) for i in range(NUM_BUCKETS)]


def _bias_kernel(tab_ref, o_ref, *, T, rows, nrel, window, back_max, starts):
    h = pl.program_id(0)
    key = lax.broadcasted_iota(jnp.int32, (rows, T), 0)
    qry = lax.broadcasted_iota(jnp.int32, (rows, T), 1)
    for rel in range(nrel):
        dist = min(rel * T, back_max) + qry - key
        val = jnp.full((rows, T), tab_ref[0, h] * LOG2E, F32)
        for b in range(1, NUM_BUCKETS):
            val = jnp.where(dist >= starts[b], tab_ref[b, h] * LOG2E, val)
        allowed = dist >= 0
        if window is not None:
            allowed = allowed & (dist < window)
        o_ref[0, rel, 0] = jnp.where(allowed, val, NEG_INF)


def _bias_tiles(tab, T, nrel, window, R, rows=None):
    H = tab.shape[1]
    back_max = nrel * T if rows is None else rows - T
    rows = T if rows is None else rows
    return pl.pallas_call(
        functools.partial(_bias_kernel, T=T, rows=rows, nrel=nrel, window=window, back_max=back_max,
                          starts=_bucket_starts()),
        grid=(H,),
        in_specs=[pl.BlockSpec(memory_space=pltpu.SMEM)],
        out_specs=pl.BlockSpec((1, nrel, 1, rows, T), lambda h: (h // R, 0, h % R, 0, 0)),
        out_shape=jax.ShapeDtypeStruct((H // R, nrel, R, rows, T), F32),
        compiler_params=_params(1),
        name="bias_tiles",
    )(tab)


def _merge_kernel(x_ref, oa_ref, ob_ref, oc_ref, os_ref, ow_ref, gc_ref, gm_ref, ex_ref, wb_ref, wo_ref, o_ref):
    gc = gc_ref[...]
    g1 = gc.astype(BF16)
    r1 = gc - g1.astype(F32)
    g2 = r1.astype(BF16)
    g3 = (r1 - g2.astype(F32)).astype(BF16)
    ex = ex_ref[...]
    gx = (jnp.dot(g1, ex, preferred_element_type=F32) + jnp.dot(g2, ex, preferred_element_type=F32)
          + jnp.dot(g3, ex, preferred_element_type=F32))
    oc = (gx[:, 0:MIX_WIDTH] * oc_ref[...] + gx[:, MIX_WIDTH:2 * MIX_WIDTH] * os_ref[...]
          + gx[:, 2 * MIX_WIDTH:3 * MIX_WIDTH] * ow_ref[...])
    z = None
    for n, br in enumerate((oa_ref[...], ob_ref[...], oc)):
        y = jnp.dot(br.astype(BF16), wb_ref[n], preferred_element_type=F32)
        t = gm_ref[:, n * D_MODEL:(n + 1) * D_MODEL] * y
        z = t if z is None else z + t
    o_ref[...] = x_ref[...] + jnp.dot(z.astype(BF16), wo_ref[...], preferred_element_type=F32)


def _merge(x2, oa, ob, oc, osel, ow, gc, gm, expand_g, wb, wo, tm=512):
    Tn = x2.shape[0]
    row = lambda d: pl.BlockSpec((tm, d), lambda i: (i, 0))
    return pl.pallas_call(
        _merge_kernel,
        grid=(Tn // tm,),
        in_specs=[row(D_MODEL), row(MIX_WIDTH), row(MIX_WIDTH), row(MIX_WIDTH), row(MIX_WIDTH), row(MIX_WIDTH),
                  row(CG_PAD), row(3 * D_MODEL),
                  pl.BlockSpec((CG_PAD, 3 * MIX_WIDTH), lambda i: (0, 0)),
                  pl.BlockSpec((3, MIX_WIDTH, D_MODEL), lambda i: (0, 0, 0)),
                  pl.BlockSpec((D_MODEL, D_MODEL), lambda i: (0, 0))],
        out_specs=row(D_MODEL),
        out_shape=jax.ShapeDtypeStruct((Tn, D_MODEL), F32),
        compiler_params=_params(1),
        name="merge",
    )(x2, oa, ob, oc, osel, ow, gc, gm, expand_g, wb, wo)


def _mlp_kernel(x_ref, nrm_ref, wu_ref, wd_ref, o_ref, h_ref, acc_ref):
    f = pl.program_id(1)

    @pl.when(f == 0)
    def _():
        h_ref[...] = _rms(x_ref[...], nrm_ref[...]).astype(BF16)
        acc_ref[...] = jnp.zeros(acc_ref.shape, F32)

    u = jnp.dot(h_ref[...], wu_ref[...], preferred_element_type=F32)
    u = jnp.square(jnp.maximum(u, 0.0)).astype(BF16)
    acc_ref[...] += jnp.dot(u, wd_ref[...], preferred_element_type=F32)

    @pl.when(f == pl.num_programs(1) - 1)
    def _():
        o_ref[...] = x_ref[...] + acc_ref[...]


def _mlp(x2, nrm, wu, wd, tm=1024, tf=1024):
    Tn = x2.shape[0]
    return pl.pallas_call(
        _mlp_kernel,
        grid=(Tn // tm, D_FF // tf),
        in_specs=[pl.BlockSpec((tm, D_MODEL), lambda i, f: (i, 0)),
                  pl.BlockSpec((1, D_MODEL), lambda i, f: (0, 0)),
                  pl.BlockSpec((D_MODEL, tf), lambda i, f: (0, f)),
                  pl.BlockSpec((tf, D_MODEL), lambda i, f: (f, 0))],
        out_specs=pl.BlockSpec((tm, D_MODEL), lambda i, f: (i, 0)),
        out_shape=jax.ShapeDtypeStruct((Tn, D_MODEL), F32),
        scratch_shapes=[pltpu.VMEM((tm, D_MODEL), BF16), pltpu.VMEM((tm, D_MODEL), F32)],
        compiler_params=_params(2),
        name="mlp",
    )(x2, nrm, wu, wd)


def _overlap(n_cmp):
    c_start = np.arange(NC_PAD) * CMP_STRIDE
    j_start = np.arange(N_SEL_BLK) * SEL_BLOCK
    ov = ((c_start[None, :] < j_start[:, None] + SEL_BLOCK) & (c_start[None, :] + CMP_BLOCK > j_start[:, None])
          & (np.arange(NC_PAD)[None, :] < n_cmp))
    return jnp.asarray(ov.astype(np.float32), BF16)


def _gate_expand():
    e = np.zeros((CG_PAD, 3 * MIX_WIDTH), np.float32)
    for h in range(C_HEADS):
        for j in range(3):
            e[h * 3 + j, j * MIX_WIDTH + h * HEAD_DIM:j * MIX_WIDTH + (h + 1) * HEAD_DIM] = 1.0
    return jnp.asarray(e, BF16)


def kernel(x, w_in, qk_gain, diff_lambda, diff_subln, sinks, cmp_pos, cmp_w1, cmp_w2,
           w_branch, w_out, norm_mix, norm_mlp, w_up, w_down, rel_bias):
    B, S, _ = x.shape
    depth = w_in.shape[0]
    n_cmp = (S - CMP_BLOCK) // CMP_STRIDE + 1
    assert S % T_ROW == 0 and S // CMP_STRIDE == NC_PAD and S // SEL_BLOCK == N_SEL_BLK
    half = CMP_BLOCK // 2 * HEAD_DIM

    w_in_p = jnp.concatenate(
        [w_in[:, :, :C_CG], jnp.pad(w_in[:, :, C_CG:C_CG + N_CG], ((0, 0), (0, 0), (0, CG_PAD - N_CG))),
         w_in[:, :, C_CG + N_CG:]], axis=-1).astype(BF16)
    gains2 = jnp.concatenate([qk_gain, qk_gain], axis=-1)
    w1 = cmp_w1.astype(BF16).reshape(depth, 2, 2, half, CMP_HIDDEN)
    w2p = jnp.pad(cmp_w2, ((0, 0), (0, 0), (0, 0), (0, LANES - HEAD_DIM))).astype(BF16)
    pos = cmp_pos.reshape(depth, 2, 2, half)
    wb = w_branch.astype(BF16)
    wo = w_out.astype(BF16)
    wu = w_up.astype(BF16)
    wd = w_down.astype(BF16)

    bias_a = rel_bias[:, :A_HEADS]
    bias_b = rel_bias[:, A_HEADS:A_HEADS + B_HEADS]
    bias_c = rel_bias[:, A_HEADS + B_HEADS:]
    bt_a = _bias_tiles(bias_a, T_ROW, 3, None, 1)
    bt_b = _bias_tiles(bias_b, T_SWA, -(-B_WINDOW // T_SWA) + 1, B_WINDOW, GROUP, rows=B_WINDOW + T_SWA)
    bt_sel = _bias_tiles(bias_c, T_ROW, 3, None, GROUP)
    bt_win = _bias_tiles(bias_c, T_ROW, C_WINDOW // T_ROW + 1, C_WINDOW, GROUP, rows=C_WINDOW + T_ROW)
    ovl = _overlap(n_cmp)
    expand_g = _gate_expand()

    for layer in range(depth):
        lam_init = 0.8 - 0.6 * math.exp(-0.3 * layer)
        (qat, ka, vat, qbt, kb, vbt, qct, tkv, ksel, vselt, kwin, vwint, gc, gm) = _inproj(
            x, norm_mix[layer][None], w_in_p[layer], gains2[layer])
        oa = _flash_attn(qat, ka, vat, bt_a, dl=diff_lambda[layer],
                   subln=diff_subln[layer][None], lam_init=lam_init, name="attn_diff")
        ob = _window_attn(qbt, kb, vbt, bt_b,
                          sink=sinks[layer].reshape(KV_GROUPS, GROUP, 1, 1), name="attn_swa")
        kc, vct = _compress(tkv.reshape(B, 2, KV_GROUPS, NC_PAD, half), pos[layer],
                            w1[layer], w2p[layer], qk_gain[layer][5:6])
        ocmp, sel = _cmp_attn(qct, kc, vct, ovl, n_cmp)
        osel = _flash_attn(qct, ksel, vselt, bt_sel, sel=sel, name="attn_sel")
        owin = _window_attn(qct, kwin, vwint, bt_win, name="attn_win")
        f2 = lambda a: a.reshape(B * S, a.shape[-1])
        x2 = _merge(f2(x), f2(oa), f2(ob), f2(ocmp), f2(osel), f2(owin), f2(gc), f2(gm),
                    expand_g, wb[layer], wo[layer])
        x2 = _mlp(x2, norm_mlp[layer][None], wu[layer], wd[layer])
        x = x2.reshape(B, S, D_MODEL)
    return x
```

```python
import functools
import math

import numpy as np
import jax
import jax.numpy as jnp
from jax import lax
from jax.experimental import pallas as pl
from jax.experimental.pallas import tpu as pltpu

F32 = jnp.float32
BF16 = jnp.bfloat16

D_MODEL = 1024
HEAD_DIM = 64
A_HEADS = 4
A_VDIM = 128
B_HEADS = 8
B_WINDOW = 128
C_HEADS = 8
KV_GROUPS = 2
GROUP = 4
CMP_BLOCK = 32
CMP_STRIDE = 16
CMP_HIDDEN = 256
SEL_BLOCK = 64
N_SELECT = 16
C_WINDOW = 512
MIX_WIDTH = 512
D_FF = 4096
NUM_BUCKETS = 32
MAX_DISTANCE = 128
NEG_INF = -1e30
FORCE_BONUS = 1e4
EPS = 1e-6
LOG2E = 1.4426950408889634
QK_SCALE = HEAD_DIM ** -0.5 * LOG2E

C_AQ, C_AK, C_AV, C_BQ, C_BKV, C_CQ, C_CKV, C_CG, C_MG = 0, 512, 1024, 1536, 2048, 2304, 2816, 3584, 3712
IN_COLS_PAD = C_MG + 3 * D_MODEL
N_CG = C_HEADS * 3
LANES = 128
CG_PAD = LANES
NC_PAD = 256
N_SEL_BLK = 64
ONES_ROWS = 16
MASK_ROWS = 16

SCORE_SLOTS = 4

T_ROW = 256
T_SWA = 256
TV_SWA = 128

VMEM_LIMIT = 48 * 1024 * 1024


def _rms(x, gain):
    ms = jnp.mean(x * x, axis=-1, keepdims=True)
    return (x * lax.rsqrt(ms + EPS)) * gain


def _params(n_axes):
    return pltpu.CompilerParams(dimension_semantics=("arbitrary",) * n_axes,
                                vmem_limit_bytes=VMEM_LIMIT)


def _inproj_kernel(x_ref, nrm_ref, w_ref, gains_ref,
                   qa_ref, ka_ref, va_ref, qb_ref, kb_ref, vb_ref, qc_ref,
                   tkv_ref, ksel_ref, vsel_ref, kwin_ref, vwin_ref, gc_ref, gm_ref):
    tm = x_ref.shape[1]
    hb = _rms(x_ref[0], nrm_ref[...]).astype(BF16)
    lane = lax.broadcasted_iota(jnp.int32, (tm, LANES), 1)
    first = lane < HEAD_DIM

    def ones_pad(width):
        return jnp.where(lax.broadcasted_iota(jnp.int32, (ONES_ROWS, width), 0) == 0, 1.0, 0.0).astype(BF16)

    def mm(c0, n):
        return jnp.dot(hb, w_ref[:, c0:c0 + n], preferred_element_type=F32)

    def norm_pair(y, gain_idx, scale=None):
        sq = y * y
        ms0 = jnp.sum(jnp.where(first, sq, 0.0), axis=-1, keepdims=True) * (1.0 / HEAD_DIM)
        ms1 = jnp.sum(jnp.where(first, 0.0, sq), axis=-1, keepdims=True) * (1.0 / HEAD_DIM)
        inv = jnp.where(first, lax.rsqrt(ms0 + EPS), lax.rsqrt(ms1 + EPS))
        out = (y * inv) * gains_ref[gain_idx:gain_idx + 1, :]
        return out if scale is None else out * scale

    def pairs(c0):
        r = mm(c0, 256)
        return r[:, :LANES], r[:, LANES:]

    def put_qt(ref, i0, i1, y):
        yt = y.T.astype(BF16)
        n_tiles = ref.shape[3]
        tq = tm // n_tiles
        for t in range(n_tiles):
            ref[0, i0, i1, t] = yt[:HEAD_DIM, tq * t:tq * (t + 1)]
            ref[0, i0, i1 + 1, t] = yt[HEAD_DIM:, tq * t:tq * (t + 1)]

    def put_vt(ref, g0, y, n_tiles):
        yt = y.T.astype(BF16)
        tk = tm // n_tiles
        for g in range(2):
            for t in range(n_tiles):
                ref[0, g0 + g, t, 0:HEAD_DIM, :] = yt[HEAD_DIM * g:HEAD_DIM * (g + 1), tk * t:tk * (t + 1)]
                ref[0, g0 + g, t, HEAD_DIM:HEAD_DIM + ONES_ROWS, :] = ones_pad(tk)

    for ch in range(2):
        for half, y in enumerate(pairs(C_AQ + 256 * ch)):
            put_qt(qa_ref, ch * 2 + half, 0, norm_pair(y, 0, QK_SCALE))
        for half, y in enumerate(pairs(C_AK + 256 * ch)):
            y = norm_pair(y, 1).astype(BF16)
            ka_ref[0, ch * 2 + half, 0] = y[:, :HEAD_DIM]
            ka_ref[0, ch * 2 + half, 1] = y[:, HEAD_DIM:]
        for half, y in enumerate(pairs(C_AV + 256 * ch)):
            va_ref[0, ch * 2 + half, 0, 0:A_VDIM, :] = y.T.astype(BF16)
            va_ref[0, ch * 2 + half, 0, A_VDIM:A_VDIM + ONES_ROWS, :] = ones_pad(tm)
        for half, y in enumerate(pairs(C_BQ + 256 * ch)):
            put_qt(qb_ref, ch, 2 * half, norm_pair(y, 2, QK_SCALE))
        for half, y in enumerate(pairs(C_CQ + 256 * ch)):
            put_qt(qc_ref, ch, 2 * half, norm_pair(y, 4, QK_SCALE))

    def put_k(ref, y):
        y = y.astype(BF16)
        ref[0, 0] = y[:, :HEAD_DIM]
        ref[0, 1] = y[:, HEAD_DIM:]

    yk, yv = pairs(C_BKV)
    put_k(kb_ref, norm_pair(yk, 3))
    put_vt(vb_ref, 0, yv, tm // TV_SWA)
    y0, y1 = pairs(C_CKV)
    for kv, y in enumerate((y0, y1)):
        tkv_ref[0, kv, 0] = y[:, :HEAD_DIM]
        tkv_ref[0, kv, 1] = y[:, HEAD_DIM:]
    yk, yv = pairs(C_CKV + 256)
    yk = norm_pair(yk, 6)
    row = lax.broadcasted_iota(jnp.int32, (tm, LANES), 0)
    hot = jnp.where(lane == HEAD_DIM + row // SEL_BLOCK, 1.0, 0.0)
    ksel_ref[0, 0] = jnp.where(first, yk, hot).astype(BF16)
    ksel_ref[0, 1] = jnp.where(first, pltpu.roll(yk, HEAD_DIM, 1), hot).astype(BF16)
    put_vt(vsel_ref, 0, yv, 1)
    yk, yv = pairs(C_CKV + 512)
    put_k(kwin_ref, norm_pair(yk, 7))
    put_vt(vwin_ref, 0, yv, 1)

    gc_ref[0] = jax.nn.sigmoid(mm(C_CG, CG_PAD))
    for ch in range(3 * D_MODEL // 256):
        gm_ref[0, :, 256 * ch:256 * (ch + 1)] = jax.nn.sigmoid(mm(C_MG + 256 * ch, 256))


def _inproj(x, nrm, w, gains2):
    B, S, _ = x.shape
    tm = T_ROW
    nt = S // tm
    sd = jax.ShapeDtypeStruct
    qt = lambda a, b, tq: sd((B, a, b, S // tq, HEAD_DIM, tq), BF16)
    kk = lambda n: sd((B, n, S, HEAD_DIM), BF16)
    vt = lambda n, dv, tk: sd((B, n, S // tk, dv + ONES_ROWS, tk), BF16)
    out_shape = (
        qt(A_HEADS, 2, tm), sd((B, A_HEADS, 2, S, HEAD_DIM), BF16), vt(A_HEADS, A_VDIM, tm),
        qt(KV_GROUPS, GROUP, T_SWA), kk(KV_GROUPS), vt(KV_GROUPS, HEAD_DIM, TV_SWA),
        qt(KV_GROUPS, GROUP, tm),
        sd((B, 2, KV_GROUPS, S, HEAD_DIM), F32),
        sd((B, KV_GROUPS, S, LANES), BF16), vt(KV_GROUPS, HEAD_DIM, tm), kk(KV_GROUPS), vt(KV_GROUPS, HEAD_DIM, tm),
        sd((B, S, CG_PAD), F32),
        sd((B, S, 3 * D_MODEL), F32),
    )
    s_qt = lambda a, b, tq: pl.BlockSpec((1, a, b, tm // tq, HEAD_DIM, tq), lambda b_, i: (b_, 0, 0, i, 0, 0))
    s_k = lambda n: pl.BlockSpec((1, n, tm, HEAD_DIM), lambda b_, i: (b_, 0, i, 0))
    s_k5 = lambda a, c: pl.BlockSpec((1, a, c, tm, HEAD_DIM), lambda b_, i: (b_, 0, 0, i, 0))
    s_vt = lambda n, dv, tk: pl.BlockSpec((1, n, tm // tk, dv + ONES_ROWS, tk), lambda b_, i: (b_, 0, i, 0, 0))
    row = lambda d: pl.BlockSpec((1, tm, d), lambda b_, i: (b_, i, 0))
    out_specs = (
        s_qt(A_HEADS, 2, tm), s_k5(A_HEADS, 2), s_vt(A_HEADS, A_VDIM, tm),
        s_qt(KV_GROUPS, GROUP, T_SWA), s_k(KV_GROUPS), s_vt(KV_GROUPS, HEAD_DIM, TV_SWA),
        s_qt(KV_GROUPS, GROUP, tm), s_k5(2, KV_GROUPS),
        pl.BlockSpec((1, KV_GROUPS, tm, LANES), lambda b_, i: (b_, 0, i, 0)), s_vt(KV_GROUPS, HEAD_DIM, tm),
        s_k(KV_GROUPS), s_vt(KV_GROUPS, HEAD_DIM, tm),
        row(CG_PAD), row(3 * D_MODEL),
    )
    return pl.pallas_call(
        _inproj_kernel,
        grid=(B, nt),
        in_specs=[
            row(D_MODEL),
            pl.BlockSpec((1, D_MODEL), lambda b_, i: (0, 0)),
            pl.BlockSpec((D_MODEL, IN_COLS_PAD), lambda b_, i: (0, 0), pipeline_mode=pl.Buffered(1)),
            pl.BlockSpec((8, LANES), lambda b_, i: (0, 0)),
        ],
        out_specs=out_specs,
        out_shape=out_shape,
        compiler_params=_params(2),
        name="inproj",
    )(x, nrm, w, gains2)


def _compress_kernel(t_ref, pos_ref, w1_ref, w2_ref, gain_ref, o_ref, ot_ref):
    t = t_ref[0, 0, 0]
    lo = (t + pos_ref[0, 0:1, :]).astype(BF16)
    hi = (t + pos_ref[0, 1:2, :]).astype(BF16)
    v = jnp.dot(lo, w1_ref[0, 0], preferred_element_type=F32)
    u = jnp.dot(hi, w1_ref[0, 1], preferred_element_type=F32)
    pre = v + pltpu.roll(u, NC_PAD - 1, 0)
    hcur = jax.nn.gelu(pre).astype(BF16)
    out = jnp.dot(hcur, w2_ref[0], preferred_element_type=F32)
    o64 = out[:, :HEAD_DIM]
    is_key = pl.program_id(1) == 0
    o_ref[0, 0, 0] = jnp.where(is_key, _rms(o64, gain_ref[...]), o64).astype(BF16)
    ot_ref[0, 0, 0] = out.T[:HEAD_DIM].astype(BF16)


def _compress(tkv16, pos, w1, w2p, gain):
    B = tkv16.shape[0]
    half = CMP_BLOCK // 2 * HEAD_DIM
    return pl.pallas_call(
        _compress_kernel,
        grid=(B, 2, KV_GROUPS),
        in_specs=[
            pl.BlockSpec((1, 1, 1, NC_PAD, half), lambda b, kv, g: (b, kv, g, 0, 0)),
            pl.BlockSpec((1, 2, half), lambda b, kv, g: (kv, 0, 0)),
            pl.BlockSpec((1, 2, half, CMP_HIDDEN), lambda b, kv, g: (kv, 0, 0, 0)),
            pl.BlockSpec((1, CMP_HIDDEN, LANES), lambda b, kv, g: (kv, 0, 0)),
            pl.BlockSpec((1, HEAD_DIM), lambda b, kv, g: (0, 0)),
        ],
        out_specs=(pl.BlockSpec((1, 1, 1, NC_PAD, HEAD_DIM), lambda b, kv, g: (b, kv, g, 0, 0)),
                   pl.BlockSpec((1, 1, 1, HEAD_DIM, NC_PAD), lambda b, kv, g: (b, kv, g, 0, 0))),
        out_shape=(jax.ShapeDtypeStruct((B, 2, KV_GROUPS, NC_PAD, HEAD_DIM), BF16),
                   jax.ShapeDtypeStruct((B, 2, KV_GROUPS, HEAD_DIM, NC_PAD), BF16)),
        compiler_params=_params(3),
        name="compress",
    )(tkv16, pos, w1, w2p, gain)


def _cmp_attn_kernel(qt_ref, kc_ref, vct_ref, ovl_ref, o_ref, sel_ref, score_ref, rank_ref, *, T, n_cmp):
    R = GROUP
    I = pl.program_id(2)
    kc = kc_ref[0, 0, 0]
    vct = vct_ref[0, 0, 0]
    c_idx = lax.broadcasted_iota(jnp.int32, (NC_PAD, T), 0)
    t_pos = I * T + lax.broadcasted_iota(jnp.int32, (NC_PAD, T), 1)
    cmask = (c_idx * CMP_STRIDE + (CMP_BLOCK - 1) <= t_pos) & (c_idx < n_cmp)
    scores = [jnp.dot(kc, qt_ref[0, 0, r, 0], preferred_element_type=F32) for r in range(R)]
    probs = []
    for r in range(R):
        s = jnp.where(cmask, scores[r], NEG_INF)
        m = jnp.max(s, axis=0, keepdims=True)
        e = jnp.where(cmask, jnp.exp2(s - m), 0.0)
        l = jnp.sum(e, axis=0, keepdims=True)
        probs.append(e * (1.0 / jnp.where(l > 0.0, l, 1.0)))
    outs = [jnp.dot(vct, p.astype(BF16), preferred_element_type=F32) for p in probs]
    o_ref[0] = jnp.concatenate(outs, axis=0).T
    psum = (probs[0] + probs[1]) + (probs[2] + probs[3])

    p_hi = psum.astype(BF16)
    p_lo = (psum - p_hi.astype(F32)).astype(BF16)
    ovl = ovl_ref[...]
    imp = jnp.dot(ovl, p_hi, preferred_element_type=F32) + jnp.dot(ovl, p_lo, preferred_element_type=F32)
    j_idx = lax.broadcasted_iota(jnp.int32, (N_SEL_BLK, T), 0)
    tq = I * T + lax.broadcasted_iota(jnp.int32, (N_SEL_BLK, T), 1)
    cur = tq // SEL_BLOCK
    valid = j_idx * SEL_BLOCK <= tq
    forced = (j_idx == 0) | (j_idx == cur) | (j_idx == cur - 1)
    score = jnp.where(valid, imp + jnp.where(forced, FORCE_BONUS, 0.0), NEG_INF)
    score_ref[...] = score
    rank_ref[...] = jnp.zeros(rank_ref.shape, F32)
    n_live = (I + 1) * (T // SEL_BLOCK)
    sub = 8
    j_loc = lax.broadcasted_iota(jnp.int32, (sub, T), 0)
    for c in range(N_SEL_BLK // sub):
        @pl.when(c * sub < n_live)
        def _(c=c):
            n_grp = N_SEL_BLK // sub
            grps = [score_ref[g * sub:(g + 1) * sub, :] for g in range(n_grp)]
            cnts = [rank_ref[g * sub:(g + 1) * sub, :] for g in range(n_grp)]
            for jp in range(c * sub, (c + 1) * sub):
                row = jnp.broadcast_to(score_ref[jp:jp + 1, :], (sub, T))
                for g in range(n_grp):
                    if g * sub > jp:
                        beats = row >= grps[g]
                    elif (g + 1) * sub - 1 < jp:
                        beats = row > grps[g]
                    else:
                        beats = (row > grps[g]) | ((row == grps[g]) & (j_loc + g * sub > jp))
                    cnts[g] = cnts[g] + jnp.where(beats, 1.0, 0.0)
            for g in range(n_grp):
                rank_ref[g * sub:(g + 1) * sub, :] = cnts[g]
    blk = T // SEL_BLOCK
    for j in range(N_SEL_BLK // blk):
        keep = rank_ref[j * blk:(j + 1) * blk, :] < N_SELECT
        slab = jnp.concatenate([jnp.where(keep, 0.0, NEG_INF), jnp.zeros((MASK_ROWS - blk, T), F32)], axis=0)
        sel_ref[0, 0, 0, j] = slab.astype(BF16)


def _cmp_attn(qct, kc, vct, ovl, n_cmp):
    B, G, R, NQ, _, T = qct.shape
    S = NQ * T
    return pl.pallas_call(
        functools.partial(_cmp_attn_kernel, T=T, n_cmp=n_cmp),
        grid=(B, G, S // T),
        in_specs=[
            pl.BlockSpec((1, 1, R, 1, HEAD_DIM, T), lambda b, g, i: (b, g, 0, i, 0, 0)),
            pl.BlockSpec((1, 1, 1, NC_PAD, HEAD_DIM), lambda b, g, i: (b, 0, g, 0, 0)),
            pl.BlockSpec((1, 1, 1, HEAD_DIM, NC_PAD), lambda b, g, i: (b, 1, g, 0, 0)),
            pl.BlockSpec((N_SEL_BLK, NC_PAD), lambda b, g, i: (0, 0)),
        ],
        out_specs=(
            pl.BlockSpec((1, T, R * HEAD_DIM), lambda b, g, i: (b, i, g)),
            pl.BlockSpec((1, 1, 1, NQ, MASK_ROWS, T), lambda b, g, i: (b, g, i, 0, 0, 0)),
        ),
        out_shape=(
            jax.ShapeDtypeStruct((B, S, G * R * HEAD_DIM), F32),
            jax.ShapeDtypeStruct((B, G, NQ, NQ, MASK_ROWS, T), BF16),
        ),
        scratch_shapes=[pltpu.VMEM((N_SEL_BLK, T), F32), pltpu.VMEM((N_SEL_BLK, T), F32)],
        compiler_params=_params(3),
        name="cmp_attn",
    )(qct, kc, vct, ovl)


def _flash_kernel(*refs, R, T, NQ, Dv, nrel, k_per_r, shared_bias, diff, has_sel, lam_init):
    it = iter(refs)
    qt_ref, k_ref, vt_ref, bt_ref = (next(it) for _ in range(4))
    sel_ref = next(it) if has_sel else None
    dl_ref = next(it) if diff else None
    sub_ref = next(it) if diff else None
    o_ref, m_ref, acc_ref, s_ref, p_ref, al_ref, tmax_ref = (next(it) for _ in range(7))

    m_ref[...] = jnp.full(m_ref.shape, NEG_INF, F32)
    acc_ref[...] = jnp.zeros(acc_ref.shape, F32)
    bias_r = (lambda r: 0) if shared_bias else (lambda r: r)
    if has_sel:
        pad_rows = jnp.zeros((k_ref.shape[-1] - HEAD_DIM - MASK_ROWS, T), BF16)
    n_steps = NQ * (NQ + 1) // 2
    N_S = SCORE_SLOTS
    assert N_S % 2 == 0 and n_steps % N_S == 0

    def score_stage(I, J, slot):
        off = pl.multiple_of(J * T, T)
        rel = jnp.minimum(I - J, nrel - 1)
        for r in range(R):
            kt = k_ref[0, 0, r, pl.ds(off, T), :] if k_per_r else k_ref[0, 0, pl.ds(off, T), :]
            w = qt_ref[0, 0, r, I]
            if has_sel:
                w = jnp.concatenate([w, sel_ref[0, 0, I, J], pad_rows], axis=0)
            s = jnp.dot(kt, w, preferred_element_type=F32) + bt_ref[0, rel, bias_r(r)]
            s_ref[slot, r] = s
            tmax_ref[slot, r] = jnp.max(s, axis=0, keepdims=True)

    def softmax_stage(I, s_slot, slot):
        for r in range(R):
            m_prev = m_ref[I, r]
            m_new = jnp.maximum(m_prev, tmax_ref[s_slot, r])
            al_ref[slot, r] = jnp.exp2(m_prev - m_new)
            p_ref[slot, r] = jnp.exp2(s_ref[s_slot, r] - m_new).astype(BF16)
            m_ref[I, r] = m_new

    def value_stage(J, slot):
        vt = vt_ref[0, 0, J]
        return [jnp.dot(vt, p_ref[slot, r], preferred_element_type=F32) for r in range(R)]

    def accumulate(I, slot, pvs):
        for r in range(R):
            acc_ref[I, r] = al_ref[slot, r] * acc_ref[I, r] + pvs[r]

    def advance(I, J):
        last = J == I
        return jnp.where(last, I + 1, I), jnp.where(last, 0, J + 1)

    def step(cur, prev, t):
        other = 1 - t % 2
        nxt = advance(*cur)
        pvs = value_stage(prev[1], other)
        score_stage(jnp.minimum(nxt[0], NQ - 1), nxt[1], (t + 1) % N_S)
        softmax_stage(cur[0], t % N_S, t % 2)
        accumulate(prev[0], other, pvs)
        return nxt

    def trip(u, carry):
        cur, prev = carry[:2], carry[2:]
        for t in range(N_S):
            cur, prev = step(cur, prev, t), cur
        return (*cur, *prev)

    zero = jnp.int32(0)
    score_stage(zero, zero, 0)
    p_ref[1] = jnp.zeros(p_ref.shape[1:], BF16)
    al_ref[1] = jnp.ones(al_ref.shape[1:], F32)
    lax.fori_loop(0, n_steps // N_S, trip, (zero, zero, zero, zero))
    accumulate(NQ - 1, 1, value_stage(NQ - 1, 1))

    if diff:
        dl = dl_ref[...]
        lam = (jnp.exp(jnp.sum(dl[0:1] * dl[1:2], keepdims=True))
               - jnp.exp(jnp.sum(dl[2:3] * dl[3:4], keepdims=True)) + lam_init)

    def finish(I, carry):
        outs = []
        for r in range(R):
            acc = acc_ref[I, r]
            outs.append(acc[:Dv] / acc[Dv:Dv + 1])
        rows = pl.ds(pl.multiple_of(I * T, T), T)
        if diff:
            o = (outs[0] - lam * outs[1]).T
            o_ref[0, rows, :] = _rms(o, sub_ref[...]) * (1.0 - lam_init)
        else:
            o_ref[0, rows, :] = jnp.concatenate(outs, axis=0).T
        return carry

    lax.fori_loop(0, NQ, finish, 0)


def _flash_attn(qt, k, vt, bias_tab, *, sel=None, dl=None, subln=None, lam_init=0.0, name="attn"):
    B, G, R, NQ, _, T = qt.shape
    S = NQ * T
    Dv = vt.shape[-2] - ONES_ROWS
    nrel, bias_heads = bias_tab.shape[1], bias_tab.shape[2]
    k_per_r = k.ndim == 5
    has_sel = sel is not None
    diff = dl is not None
    ins = [qt, k, vt, bias_tab]
    in_specs = [
        pl.BlockSpec((1, 1, R, NQ, HEAD_DIM, T), lambda b, g: (b, g, 0, 0, 0, 0)),
        (pl.BlockSpec((1, 1, R, S, HEAD_DIM), lambda b, g: (b, g, 0, 0, 0)) if k_per_r
         else pl.BlockSpec((1, 1, S, k.shape[-1]), lambda b, g: (b, g, 0, 0))),
        pl.BlockSpec((1, 1, NQ, Dv + ONES_ROWS, T), lambda b, g: (b, g, 0, 0, 0)),
        pl.BlockSpec((1, nrel, bias_heads, T, T), lambda b, g: (g, 0, 0, 0, 0)),
    ]
    if has_sel:
        ins.append(sel)
        in_specs.append(pl.BlockSpec((1, 1, NQ, NQ, MASK_ROWS, T), lambda b, g: (b, g, 0, 0, 0, 0)))
    if diff:
        ins += [dl, subln]
        in_specs += [pl.BlockSpec((4, HEAD_DIM), lambda b, g: (0, 0)),
                     pl.BlockSpec((1, A_VDIM), lambda b, g: (0, 0))]
    out_w = Dv if diff else R * Dv
    kern = functools.partial(_flash_kernel, R=R, T=T, NQ=NQ, Dv=Dv, nrel=nrel, k_per_r=k_per_r,
                             shared_bias=bias_heads == 1, diff=diff, has_sel=has_sel, lam_init=lam_init)
    return pl.pallas_call(
        kern,
        grid=(B, G),
        in_specs=in_specs,
        out_specs=pl.BlockSpec((1, S, out_w), lambda b, g: (b, 0, g)),
        out_shape=jax.ShapeDtypeStruct((B, S, G * out_w), F32),
        scratch_shapes=[pltpu.VMEM((NQ, R, 1, T), F32), pltpu.VMEM((NQ, R, Dv + ONES_ROWS, T), F32),
                        pltpu.VMEM((SCORE_SLOTS, R, T, T), F32), pltpu.VMEM((2, R, T, T), BF16),
                        pltpu.VMEM((2, R, 1, T), F32), pltpu.VMEM((SCORE_SLOTS, R, 1, T), F32)],
        compiler_params=_params(2),
        name=name,
    )(*ins)


def _window_kernel(*refs, R, T, NK, TV, Dv, has_sink):
    it = iter(refs)
    qt_ref, k_ref, vt_ref, bt_ref = (next(it) for _ in range(4))
    sink_ref = next(it) if has_sink else None
    o_ref = next(it)
    I = pl.program_id(2)
    entry = jnp.minimum(I, -(-(NK - T) // T))
    first_key = pl.multiple_of(I * T - jnp.minimum(entry * T, NK - T), TV)
    kt = k_ref[0, 0, pl.ds(first_key, NK), :]
    scores = [jnp.dot(kt, qt_ref[0, 0, r, 0], preferred_element_type=F32) for r in range(R)]
    probs, maxes = [], []
    for r in range(R):
        s = scores[r] + bt_ref[0, entry, r]
        m = jnp.max(s, axis=0, keepdims=True)
        probs.append(jnp.exp2(s - m).astype(BF16))
        maxes.append(m)
    outs = []
    for r in range(R):
        acc = None
        for c in range(NK // TV):
            part = jnp.dot(vt_ref[0, 0, first_key // TV + c], probs[r][c * TV:(c + 1) * TV],
                           preferred_element_type=F32)
            acc = part if acc is None else acc + part
        num, l = acc[:Dv], acc[Dv:Dv + 1]
        if has_sink:
            sk = sink_ref[0, r] * LOG2E
            m_f = jnp.maximum(maxes[r], sk)
            w = jnp.exp2(maxes[r] - m_f)
            outs.append(num * w / (l * w + jnp.exp2(sk - m_f)))
        else:
            outs.append(num / l)
    o_ref[0] = jnp.concatenate(outs, axis=0).T


def _window_attn(qt, k, vt, bias_tab, *, sink=None, name="attn_window"):
    B, G, R, NQ, _, T = qt.shape
    S = NQ * T
    TV = vt.shape[-1]
    Dv = vt.shape[-2] - ONES_ROWS
    entries, NK = bias_tab.shape[1], bias_tab.shape[3]
    has_sink = sink is not None
    ins = [qt, k, vt, bias_tab]
    in_specs = [
        pl.BlockSpec((1, 1, R, 1, HEAD_DIM, T), lambda b, g, i: (b, g, 0, i, 0, 0)),
        pl.BlockSpec((1, 1, S, HEAD_DIM), lambda b, g, i: (b, g, 0, 0)),
        pl.BlockSpec((1, 1, S // TV, Dv + ONES_ROWS, TV), lambda b, g, i: (b, g, 0, 0, 0)),
        pl.BlockSpec((1, entries, R, NK, T), lambda b, g, i: (g, 0, 0, 0, 0)),
    ]
    if has_sink:
        ins.append(sink)
        in_specs.append(pl.BlockSpec((1, R, 1, 1), lambda b, g, i: (g, 0, 0, 0)))
    kern = functools.partial(_window_kernel, R=R, T=T, NK=NK, TV=TV, Dv=Dv, has_sink=has_sink)
    return pl.pallas_call(
        kern,
        grid=(B, G, S // T),
        in_specs=in_specs,
        out_specs=pl.BlockSpec((1, T, R * Dv), lambda b, g, i: (b, i, g)),
        out_shape=jax.ShapeDtypeStruct((B, S, G * R * Dv), F32),
        compiler_params=_params(3),
        name=name,
    )(*ins)


def _bucket_np(dist):
    n = np.maximum(dist, 0)
    max_exact = NUM_BUCKETS // 2
    nf = np.maximum(n, 1).astype(np.float32)
    large = max_exact + (np.log(nf / max_exact) / math.log(MAX_DISTANCE / max_exact)
                         * (NUM_BUCKETS - max_exact)).astype(np.int32)
    large = np.minimum(large, NUM_BUCKETS - 1)
    return np.where(n < max_exact, n, large)


def _bucket_starts():
    b = _bucket_np(np.arange(4 * MAX_DISTANCE))
    assert (np.diff(b) >= 0).all() and b[-1] == NUM_BUCKETS - 1
    return [int(np.argmax(b >= i)) for i in range(NUM_BUCKETS)]


def _bias_kernel(tab_ref, o_ref, *, T, rows, nrel, window, back_max, starts):
    h = pl.program_id(0)
    key = lax.broadcasted_iota(jnp.int32, (rows, T), 0)
    qry = lax.broadcasted_iota(jnp.int32, (rows, T), 1)
    for rel in range(nrel):
        dist = min(rel * T, back_max) + qry - key
        val = jnp.full((rows, T), tab_ref[0, h] * LOG2E, F32)
        for b in range(1, NUM_BUCKETS):
            val = jnp.where(dist >= starts[b], tab_ref[b, h] * LOG2E, val)
        allowed = dist >= 0
        if window is not None:
            allowed = allowed & (dist < window)
        o_ref[0, rel, 0] = jnp.where(allowed, val, NEG_INF)


def _bias_tiles(tab, T, nrel, window, R, rows=None):
    H = tab.shape[1]
    back_max = nrel * T if rows is None else rows - T
    rows = T if rows is None else rows
    return pl.pallas_call(
        functools.partial(_bias_kernel, T=T, rows=rows, nrel=nrel, window=window, back_max=back_max,
                          starts=_bucket_starts()),
        grid=(H,),
        in_specs=[pl.BlockSpec(memory_space=pltpu.SMEM)],
        out_specs=pl.BlockSpec((1, nrel, 1, rows, T), lambda h: (h // R, 0, h % R, 0, 0)),
        out_shape=jax.ShapeDtypeStruct((H // R, nrel, R, rows, T), F32),
        compiler_params=_params(1),
        name="bias_tiles",
    )(tab)


def _merge_kernel(x_ref, oa_ref, ob_ref, oc_ref, os_ref, ow_ref, gc_ref, gm_ref, ex_ref, wb_ref, wo_ref, o_ref):
    gc = gc_ref[...]
    g1 = gc.astype(BF16)
    r1 = gc - g1.astype(F32)
    g2 = r1.astype(BF16)
    g3 = (r1 - g2.astype(F32)).astype(BF16)
    ex = ex_ref[...]
    gx = (jnp.dot(g1, ex, preferred_element_type=F32) + jnp.dot(g2, ex, preferred_element_type=F32)
          + jnp.dot(g3, ex, preferred_element_type=F32))
    oc = (gx[:, 0:MIX_WIDTH] * oc_ref[...] + gx[:, MIX_WIDTH:2 * MIX_WIDTH] * os_ref[...]
          + gx[:, 2 * MIX_WIDTH:3 * MIX_WIDTH] * ow_ref[...])
    z = None
    for n, br in enumerate((oa_ref[...], ob_ref[...], oc)):
        y = jnp.dot(br.astype(BF16), wb_ref[n], preferred_element_type=F32)
        t = gm_ref[:, n * D_MODEL:(n + 1) * D_MODEL] * y
        z = t if z is None else z + t
    o_ref[...] = x_ref[...] + jnp.dot(z.astype(BF16), wo_ref[...], preferred_element_type=F32)


def _merge(x2, oa, ob, oc, osel, ow, gc, gm, expand_g, wb, wo, tm=512):
    Tn = x2.shape[0]
    row = lambda d: pl.BlockSpec((tm, d), lambda i: (i, 0))
    return pl.pallas_call(
        _merge_kernel,
        grid=(Tn // tm,),
        in_specs=[row(D_MODEL), row(MIX_WIDTH), row(MIX_WIDTH), row(MIX_WIDTH), row(MIX_WIDTH), row(MIX_WIDTH),
                  row(CG_PAD), row(3 * D_MODEL),
                  pl.BlockSpec((CG_PAD, 3 * MIX_WIDTH), lambda i: (0, 0)),
                  pl.BlockSpec((3, MIX_WIDTH, D_MODEL), lambda i: (0, 0, 0)),
                  pl.BlockSpec((D_MODEL, D_MODEL), lambda i: (0, 0))],
        out_specs=row(D_MODEL),
        out_shape=jax.ShapeDtypeStruct((Tn, D_MODEL), F32),
        compiler_params=_params(1),
        name="merge",
    )(x2, oa, ob, oc, osel, ow, gc, gm, expand_g, wb, wo)


def _mlp_kernel(x_ref, nrm_ref, wu_ref, wd_ref, o_ref, h_ref, acc_ref):
    f = pl.program_id(1)

    @pl.when(f == 0)
    def _():
        h_ref[...] = _rms(x_ref[...], nrm_ref[...]).astype(BF16)
        acc_ref[...] = jnp.zeros(acc_ref.shape, F32)

    u = jnp.dot(h_ref[...], wu_ref[...], preferred_element_type=F32)
    u = jnp.square(jnp.maximum(u, 0.0)).astype(BF16)
    acc_ref[...] += jnp.dot(u, wd_ref[...], preferred_element_type=F32)

    @pl.when(f == pl.num_programs(1) - 1)
    def _():
        o_ref[...] = x_ref[...] + acc_ref[...]


def _mlp(x2, nrm, wu, wd, tm=1024, tf=1024):
    Tn = x2.shape[0]
    return pl.pallas_call(
        _mlp_kernel,
        grid=(Tn // tm, D_FF // tf),
        in_specs=[pl.BlockSpec((tm, D_MODEL), lambda i, f: (i, 0)),
                  pl.BlockSpec((1, D_MODEL), lambda i, f: (0, 0)),
                  pl.BlockSpec((D_MODEL, tf), lambda i, f: (0, f)),
                  pl.BlockSpec((tf, D_MODEL), lambda i, f: (f, 0))],
        out_specs=pl.BlockSpec((tm, D_MODEL), lambda i, f: (i, 0)),
        out_shape=jax.ShapeDtypeStruct((Tn, D_MODEL), F32),
        scratch_shapes=[pltpu.VMEM((tm, D_MODEL), BF16), pltpu.VMEM((tm, D_MODEL), F32)],
        compiler_params=_params(2),
        name="mlp",
    )(x2, nrm, wu, wd)


def _overlap(n_cmp):
    c_start = np.arange(NC_PAD) * CMP_STRIDE
    j_start = np.arange(N_SEL_BLK) * SEL_BLOCK
    ov = ((c_start[None, :] < j_start[:, None] + SEL_BLOCK) & (c_start[None, :] + CMP_BLOCK > j_start[:, None])
          & (np.arange(NC_PAD)[None, :] < n_cmp))
    return jnp.asarray(ov.astype(np.float32), BF16)


def _gate_expand():
    e = np.zeros((CG_PAD, 3 * MIX_WIDTH), np.float32)
    for h in range(C_HEADS):
        for j in range(3):
            e[h * 3 + j, j * MIX_WIDTH + h * HEAD_DIM:j * MIX_WIDTH + (h + 1) * HEAD_DIM] = 1.0
    return jnp.asarray(e, BF16)


def kernel(x, w_in, qk_gain, diff_lambda, diff_subln, sinks, cmp_pos, cmp_w1, cmp_w2,
           w_branch, w_out, norm_mix, norm_mlp, w_up, w_down, rel_bias):
    B, S, _ = x.shape
    depth = w_in.shape[0]
    n_cmp = (S - CMP_BLOCK) // CMP_STRIDE + 1
    assert S % T_ROW == 0 and S // CMP_STRIDE == NC_PAD and S // SEL_BLOCK == N_SEL_BLK
    half = CMP_BLOCK // 2 * HEAD_DIM

    w_in_p = jnp.concatenate(
        [w_in[:, :, :C_CG], jnp.pad(w_in[:, :, C_CG:C_CG + N_CG], ((0, 0), (0, 0), (0, CG_PAD - N_CG))),
         w_in[:, :, C_CG + N_CG:]], axis=-1).astype(BF16)
    gains2 = jnp.concatenate([qk_gain, qk_gain], axis=-1)
    w1 = cmp_w1.astype(BF16).reshape(depth, 2, 2, half, CMP_HIDDEN)
    w2p = jnp.pad(cmp_w2, ((0, 0), (0, 0), (0, 0), (0, LANES - HEAD_DIM))).astype(BF16)
    pos = cmp_pos.reshape(depth, 2, 2, half)
    wb = w_branch.astype(BF16)
    wo = w_out.astype(BF16)
    wu = w_up.astype(BF16)
    wd = w_down.astype(BF16)

    bias_a = rel_bias[:, :A_HEADS]
    bias_b = rel_bias[:, A_HEADS:A_HEADS + B_HEADS]
    bias_c = rel_bias[:, A_HEADS + B_HEADS:]
    bt_a = _bias_tiles(bias_a, T_ROW, 3, None, 1)
    bt_b = _bias_tiles(bias_b, T_SWA, -(-B_WINDOW // T_SWA) + 1, B_WINDOW, GROUP, rows=B_WINDOW + T_SWA)
    bt_sel = _bias_tiles(bias_c, T_ROW, 3, None, GROUP)
    bt_win = _bias_tiles(bias_c, T_ROW, C_WINDOW // T_ROW + 1, C_WINDOW, GROUP, rows=C_WINDOW + T_ROW)
    ovl = _overlap(n_cmp)
    expand_g = _gate_expand()

    for layer in range(depth):
        lam_init = 0.8 - 0.6 * math.exp(-0.3 * layer)
        (qat, ka, vat, qbt, kb, vbt, qct, tkv, ksel, vselt, kwin, vwint, gc, gm) = _inproj(
            x, norm_mix[layer][None], w_in_p[layer], gains2[layer])
        oa = _flash_attn(qat, ka, vat, bt_a, dl=diff_lambda[layer],
                   subln=diff_subln[layer][None], lam_init=lam_init, name="attn_diff")
        ob = _window_attn(qbt, kb, vbt, bt_b,
                          sink=sinks[layer].reshape(KV_GROUPS, GROUP, 1, 1), name="attn_swa")
        kc, vct = _compress(tkv.reshape(B, 2, KV_GROUPS, NC_PAD, half), pos[layer],
                            w1[layer], w2p[layer], qk_gain[layer][5:6])
        ocmp, sel = _cmp_attn(qct, kc, vct, ovl, n_cmp)
        osel = _flash_attn(qct, ksel, vselt, bt_sel, sel=sel, name="attn_sel")
        owin = _window_attn(qct, kwin, vwint, bt_win, name="attn_win")
        f2 = lambda a: a.reshape(B * S, a.shape[-1])
        x2 = _merge(f2(x), f2(oa), f2(ob), f2(ocmp), f2(osel), f2(owin), f2(gc), f2(gm),
                    expand_g, wb[layer], wo[layer])
        x2 = _mlp(x2, norm_mlp[layer][None], wu[layer], wd[layer])
        x = x2.reshape(B, S, D_MODEL)
    return x
```

```python
import functools
import math

import numpy as np
import jax
import jax.numpy as jnp
from jax import lax
from jax.experimental import pallas as pl
from jax.experimental.pallas import tpu as pltpu

F32 = jnp.float32
BF16 = jnp.bfloat16

D_MODEL = 1024
HEAD_DIM = 64
A_HEADS = 4
A_VDIM = 128
B_HEADS = 8
B_WINDOW = 128
C_HEADS = 8
KV_GROUPS = 2
GROUP = 4
CMP_BLOCK = 32
CMP_STRIDE = 16
CMP_HIDDEN = 256
SEL_BLOCK = 64
N_SELECT = 16
C_WINDOW = 512
MIX_WIDTH = 512
D_FF = 4096
NUM_BUCKETS = 32
MAX_DISTANCE = 128
NEG_INF = -1e30
FORCE_BONUS = 1e4
EPS = 1e-6
LOG2E = 1.4426950408889634
QK_SCALE = HEAD_DIM ** -0.5 * LOG2E

C_AQ, C_AK, C_AV, C_BQ, C_BKV, C_CQ, C_CKV, C_CG = 0, 512, 1024, 1536, 2048, 2304, 2816, 3584
N_CG = C_HEADS * 3
LANES = 128
CG_PAD = LANES
NC_PAD = 256
N_SEL_BLK = 64
ONES_ROWS = 16
MASK_ROWS = 16

SCORE_SLOTS = 4

T_IN = 512
T_ROW = 256
T_SWA = 256
TV_SWA = 128

VMEM_LIMIT = 48 * 1024 * 1024


def _rms(x, gain):
    ms = jnp.mean(x * x, axis=-1, keepdims=True)
    return (x * lax.rsqrt(ms + EPS)) * gain


def _params(n_axes):
    return pltpu.CompilerParams(dimension_semantics=("arbitrary",) * n_axes,
                                vmem_limit_bytes=VMEM_LIMIT)


def _inproj_kernel(x_ref, nrm_ref, w_ref, wcg_ref, gains_ref,
                   qa_ref, ka_ref, va_ref, qb_ref, kb_ref, vb_ref, qc_ref,
                   tkv_ref, ksel_ref, vsel_ref, kwin_ref, vwin_ref, gc_ref):
    tm = x_ref.shape[1]
    hb = _rms(x_ref[0], nrm_ref[...]).astype(BF16)
    lane = lax.broadcasted_iota(jnp.int32, (tm, LANES), 1)
    first = lane < HEAD_DIM

    def ones_pad(width):
        return jnp.where(lax.broadcasted_iota(jnp.int32, (ONES_ROWS, width), 0) == 0, 1.0, 0.0).astype(BF16)

    def mm(c0, n):
        return jnp.dot(hb, w_ref[:, c0:c0 + n], preferred_element_type=F32)

    def norm_pair(y, gain_idx, scale=None):
        sq = y * y
        ms0 = jnp.sum(jnp.where(first, sq, 0.0), axis=-1, keepdims=True) * (1.0 / HEAD_DIM)
        ms1 = jnp.sum(jnp.where(first, 0.0, sq), axis=-1, keepdims=True) * (1.0 / HEAD_DIM)
        inv = jnp.where(first, lax.rsqrt(ms0 + EPS), lax.rsqrt(ms1 + EPS))
        out = (y * inv) * gains_ref[gain_idx:gain_idx + 1, :]
        return out if scale is None else out * scale

    def pairs(c0):
        r = mm(c0, 256)
        return r[:, :LANES], r[:, LANES:]

    def put_qt(ref, i0, i1, y):
        yt = y.T.astype(BF16)
        n_tiles = ref.shape[3]
        tq = tm // n_tiles
        for t in range(n_tiles):
            ref[0, i0, i1, t] = yt[:HEAD_DIM, tq * t:tq * (t + 1)]
            ref[0, i0, i1 + 1, t] = yt[HEAD_DIM:, tq * t:tq * (t + 1)]

    def put_vt(ref, g0, y, n_tiles):
        yt = y.T.astype(BF16)
        tk = tm // n_tiles
        for g in range(2):
            for t in range(n_tiles):
                ref[0, g0 + g, t, 0:HEAD_DIM, :] = yt[HEAD_DIM * g:HEAD_DIM * (g + 1), tk * t:tk * (t + 1)]
                ref[0, g0 + g, t, HEAD_DIM:HEAD_DIM + ONES_ROWS, :] = ones_pad(tk)

    for ch in range(2):
        for half, y in enumerate(pairs(C_AQ + 256 * ch)):
            put_qt(qa_ref, ch * 2 + half, 0, norm_pair(y, 0, QK_SCALE))
        for half, y in enumerate(pairs(C_AK + 256 * ch)):
            y = norm_pair(y, 1).astype(BF16)
            ka_ref[0, ch * 2 + half, 0] = y[:, :HEAD_DIM]
            ka_ref[0, ch * 2 + half, 1] = y[:, HEAD_DIM:]
        for half, y in enumerate(pairs(C_AV + 256 * ch)):
            yt = y.T.astype(BF16)
            for t in range(tm // T_ROW):
                va_ref[0, ch * 2 + half, t, 0:A_VDIM, :] = yt[:, T_ROW * t:T_ROW * (t + 1)]
                va_ref[0, ch * 2 + half, t, A_VDIM:A_VDIM + ONES_ROWS, :] = ones_pad(T_ROW)
        for half, y in enumerate(pairs(C_BQ + 256 * ch)):
            put_qt(qb_ref, ch, 2 * half, norm_pair(y, 2, QK_SCALE))
        for half, y in enumerate(pairs(C_CQ + 256 * ch)):
            put_qt(qc_ref, ch, 2 * half, norm_pair(y, 4, QK_SCALE))

    def put_k(ref, y):
        y = y.astype(BF16)
        ref[0, 0] = y[:, :HEAD_DIM]
        ref[0, 1] = y[:, HEAD_DIM:]

    yk, yv = pairs(C_BKV)
    put_k(kb_ref, norm_pair(yk, 3))
    put_vt(vb_ref, 0, yv, tm // TV_SWA)
    y0, y1 = pairs(C_CKV)
    for kv, y in enumerate((y0, y1)):
        tkv_ref[0, kv, 0] = y[:, :HEAD_DIM]
        tkv_ref[0, kv, 1] = y[:, HEAD_DIM:]
    yk, yv = pairs(C_CKV + 256)
    yk = norm_pair(yk, 6)
    row = lax.broadcasted_iota(jnp.int32, (tm, LANES), 0)
    hot = jnp.where(lane == HEAD_DIM + (row // SEL_BLOCK) % (T_ROW // SEL_BLOCK), 1.0, 0.0)
    ksel_ref[0, 0] = jnp.where(first, yk, hot).astype(BF16)
    ksel_ref[0, 1] = jnp.where(first, pltpu.roll(yk, HEAD_DIM, 1), hot).astype(BF16)
    put_vt(vsel_ref, 0, yv, tm // T_ROW)
    yk, yv = pairs(C_CKV + 512)
    put_k(kwin_ref, norm_pair(yk, 7))
    put_vt(vwin_ref, 0, yv, tm // T_ROW)

    gc_ref[0] = jax.nn.sigmoid(jnp.dot(hb, wcg_ref[...], preferred_element_type=F32))


def _inproj(x, nrm, w, wcg, gains2):
    B, S, _ = x.shape
    tm = T_IN
    nt = S // tm
    sd = jax.ShapeDtypeStruct
    qt = lambda a, b, tq: sd((B, a, b, S // tq, HEAD_DIM, tq), BF16)
    kk = lambda n: sd((B, n, S, HEAD_DIM), BF16)
    vt = lambda n, dv, tk: sd((B, n, S // tk, dv + ONES_ROWS, tk), BF16)
    out_shape = (
        qt(A_HEADS, 2, T_ROW), sd((B, A_HEADS, 2, S, HEAD_DIM), BF16), vt(A_HEADS, A_VDIM, T_ROW),
        qt(KV_GROUPS, GROUP, T_SWA), kk(KV_GROUPS), vt(KV_GROUPS, HEAD_DIM, TV_SWA),
        qt(KV_GROUPS, GROUP, T_ROW),
        sd((B, 2, KV_GROUPS, S, HEAD_DIM), F32),
        sd((B, KV_GROUPS, S, LANES), BF16), vt(KV_GROUPS, HEAD_DIM, T_ROW),
        kk(KV_GROUPS), vt(KV_GROUPS, HEAD_DIM, T_ROW),
        sd((B, S, CG_PAD), F32),
    )
    s_qt = lambda a, b, tq: pl.BlockSpec((1, a, b, tm // tq, HEAD_DIM, tq), lambda b_, i: (b_, 0, 0, i, 0, 0))
    s_k = lambda n: pl.BlockSpec((1, n, tm, HEAD_DIM), lambda b_, i: (b_, 0, i, 0))
    s_k5 = lambda a, c: pl.BlockSpec((1, a, c, tm, HEAD_DIM), lambda b_, i: (b_, 0, 0, i, 0))
    s_vt = lambda n, dv, tk: pl.BlockSpec((1, n, tm // tk, dv + ONES_ROWS, tk), lambda b_, i: (b_, 0, i, 0, 0))
    row = lambda d: pl.BlockSpec((1, tm, d), lambda b_, i: (b_, i, 0))
    out_specs = (
        s_qt(A_HEADS, 2, T_ROW), s_k5(A_HEADS, 2), s_vt(A_HEADS, A_VDIM, T_ROW),
        s_qt(KV_GROUPS, GROUP, T_SWA), s_k(KV_GROUPS), s_vt(KV_GROUPS, HEAD_DIM, TV_SWA),
        s_qt(KV_GROUPS, GROUP, T_ROW), s_k5(2, KV_GROUPS),
        pl.BlockSpec((1, KV_GROUPS, tm, LANES), lambda b_, i: (b_, 0, i, 0)), s_vt(KV_GROUPS, HEAD_DIM, T_ROW),
        s_k(KV_GROUPS), s_vt(KV_GROUPS, HEAD_DIM, T_ROW),
        row(CG_PAD),
    )
    return pl.pallas_call(
        _inproj_kernel,
        grid=(B, nt),
        in_specs=[
            row(D_MODEL),
            pl.BlockSpec((1, D_MODEL), lambda b_, i: (0, 0)),
            pl.BlockSpec((D_MODEL, C_CG), lambda b_, i: (0, 0), pipeline_mode=pl.Buffered(1)),
            pl.BlockSpec((D_MODEL, CG_PAD), lambda b_, i: (0, 0)),
            pl.BlockSpec((8, LANES), lambda b_, i: (0, 0)),
        ],
        out_specs=out_specs,
        out_shape=out_shape,
        compiler_params=_params(2),
        name="inproj",
    )(x, nrm, w, wcg, gains2)


def _compress_kernel(t_ref, pos_ref, w1_ref, w2_ref, gain_ref, o_ref, ot_ref):
    t = t_ref[0, 0, 0]
    lo = (t + pos_ref[0, 0:1, :]).astype(BF16)
    hi = (t + pos_ref[0, 1:2, :]).astype(BF16)
    v = jnp.dot(lo, w1_ref[0, 0], preferred_element_type=F32)
    u = jnp.dot(hi, w1_ref[0, 1], preferred_element_type=F32)
    pre = v + pltpu.roll(u, NC_PAD - 1, 0)
    hcur = jax.nn.gelu(pre).astype(BF16)
    out = jnp.dot(hcur, w2_ref[0], preferred_element_type=F32)
    o64 = out[:, :HEAD_DIM]
    is_key = pl.program_id(1) == 0
    o_ref[0, 0, 0] = jnp.where(is_key, _rms(o64, gain_ref[...]), o64).astype(BF16)
    ot_ref[0, 0, 0] = out.T[:HEAD_DIM].astype(BF16)


def _compress(tkv16, pos, w1, w2p, gain):
    B = tkv16.shape[0]
    half = CMP_BLOCK // 2 * HEAD_DIM
    return pl.pallas_call(
        _compress_kernel,
        grid=(B, 2, KV_GROUPS),
        in_specs=[
            pl.BlockSpec((1, 1, 1, NC_PAD, half), lambda b, kv, g: (b, kv, g, 0, 0)),
            pl.BlockSpec((1, 2, half), lambda b, kv, g: (kv, 0, 0)),
            pl.BlockSpec((1, 2, half, CMP_HIDDEN), lambda b, kv, g: (kv, 0, 0, 0)),
            pl.BlockSpec((1, CMP_HIDDEN, LANES), lambda b, kv, g: (kv, 0, 0)),
            pl.BlockSpec((1, HEAD_DIM), lambda b, kv, g: (0, 0)),
        ],
        out_specs=(pl.BlockSpec((1, 1, 1, NC_PAD, HEAD_DIM), lambda b, kv, g: (b, kv, g, 0, 0)),
                   pl.BlockSpec((1, 1, 1, HEAD_DIM, NC_PAD), lambda b, kv, g: (b, kv, g, 0, 0))),
        out_shape=(jax.ShapeDtypeStruct((B, 2, KV_GROUPS, NC_PAD, HEAD_DIM), BF16),
                   jax.ShapeDtypeStruct((B, 2, KV_GROUPS, HEAD_DIM, NC_PAD), BF16)),
        compiler_params=_params(3),
        name="compress",
    )(tkv16, pos, w1, w2p, gain)


def _cmp_attn_kernel(qt_ref, kc_ref, vct_ref, ovl_ref, o_ref, sel_ref, score_ref, rank_ref, *, T, n_cmp):
    R = GROUP
    I = pl.program_id(2)
    kc = kc_ref[0, 0, 0]
    vct = vct_ref[0, 0, 0]
    c_idx = lax.broadcasted_iota(jnp.int32, (NC_PAD, T), 0)
    t_pos = I * T + lax.broadcasted_iota(jnp.int32, (NC_PAD, T), 1)
    cmask = (c_idx * CMP_STRIDE + (CMP_BLOCK - 1) <= t_pos) & (c_idx < n_cmp)
    scores = [jnp.dot(kc, qt_ref[0, 0, r, 0], preferred_element_type=F32) for r in range(R)]
    probs = []
    for r in range(R):
        s = jnp.where(cmask, scores[r], NEG_INF)
        m = jnp.max(s, axis=0, keepdims=True)
        e = jnp.where(cmask, jnp.exp2(s - m), 0.0)
        l = jnp.sum(e, axis=0, keepdims=True)
        probs.append(e * (1.0 / jnp.where(l > 0.0, l, 1.0)))
    outs = [jnp.dot(vct, p.astype(BF16), preferred_element_type=F32) for p in probs]
    o_ref[0] = jnp.concatenate(outs, axis=0).T
    psum = (probs[0] + probs[1]) + (probs[2] + probs[3])

    p_hi = psum.astype(BF16)
    p_lo = (psum - p_hi.astype(F32)).astype(BF16)
    ovl = ovl_ref[...]
    imp = jnp.dot(ovl, p_hi, preferred_element_type=F32) + jnp.dot(ovl, p_lo, preferred_element_type=F32)
    j_idx = lax.broadcasted_iota(jnp.int32, (N_SEL_BLK, T), 0)
    tq = I * T + lax.broadcasted_iota(jnp.int32, (N_SEL_BLK, T), 1)
    cur = tq // SEL_BLOCK
    valid = j_idx * SEL_BLOCK <= tq
    forced = (j_idx == 0) | (j_idx == cur) | (j_idx == cur - 1)
    score = jnp.where(valid, imp + jnp.where(forced, FORCE_BONUS, 0.0), NEG_INF)
    score_ref[...] = score
    rank_ref[...] = jnp.zeros(rank_ref.shape, F32)
    n_live = (I + 1) * (T // SEL_BLOCK)
    sub = 8
    j_loc = lax.broadcasted_iota(jnp.int32, (sub, T), 0)
    for c in range(N_SEL_BLK // sub):
        @pl.when(c * sub < n_live)
        def _(c=c):
            n_grp = N_SEL_BLK // sub
            grps = [score_ref[g * sub:(g + 1) * sub, :] for g in range(n_grp)]
            cnts = [rank_ref[g * sub:(g + 1) * sub, :] for g in range(n_grp)]
            for jp in range(c * sub, (c + 1) * sub):
                row = jnp.broadcast_to(score_ref[jp:jp + 1, :], (sub, T))
                for g in range(n_grp):
                    if g * sub > jp:
                        beats = row >= grps[g]
                    elif (g + 1) * sub - 1 < jp:
                        beats = row > grps[g]
                    else:
                        beats = (row > grps[g]) | ((row == grps[g]) & (j_loc + g * sub > jp))
                    cnts[g] = cnts[g] + jnp.where(beats, 1.0, 0.0)
            for g in range(n_grp):
                rank_ref[g * sub:(g + 1) * sub, :] = cnts[g]
    blk = T // SEL_BLOCK
    for j in range(N_SEL_BLK // blk):
        keep = rank_ref[j * blk:(j + 1) * blk, :] < N_SELECT
        slab = jnp.concatenate([jnp.where(keep, 0.0, NEG_INF), jnp.zeros((MASK_ROWS - blk, T), F32)], axis=0)
        sel_ref[0, 0, 0, j] = slab.astype(BF16)


def _cmp_attn(qct, kc, vct, ovl, n_cmp):
    B, G, R, NQ, _, T = qct.shape
    S = NQ * T
    return pl.pallas_call(
        functools.partial(_cmp_attn_kernel, T=T, n_cmp=n_cmp),
        grid=(B, G, S // T),
        in_specs=[
            pl.BlockSpec((1, 1, R, 1, HEAD_DIM, T), lambda b, g, i: (b, g, 0, i, 0, 0)),
            pl.BlockSpec((1, 1, 1, NC_PAD, HEAD_DIM), lambda b, g, i: (b, 0, g, 0, 0)),
            pl.BlockSpec((1, 1, 1, HEAD_DIM, NC_PAD), lambda b, g, i: (b, 1, g, 0, 0)),
            pl.BlockSpec((N_SEL_BLK, NC_PAD), lambda b, g, i: (0, 0)),
        ],
        out_specs=(
            pl.BlockSpec((1, T, R * HEAD_DIM), lambda b, g, i: (b, i, g)),
            pl.BlockSpec((1, 1, 1, NQ, MASK_ROWS, T), lambda b, g, i: (b, g, i, 0, 0, 0)),
        ),
        out_shape=(
            jax.ShapeDtypeStruct((B, S, G * R * HEAD_DIM), F32),
            jax.ShapeDtypeStruct((B, G, NQ, NQ, MASK_ROWS, T), BF16),
        ),
        scratch_shapes=[pltpu.VMEM((N_SEL_BLK, T), F32), pltpu.VMEM((N_SEL_BLK, T), F32)],
        compiler_params=_params(3),
        name="cmp_attn",
    )(qct, kc, vct, ovl)


def _flash_kernel(*refs, R, T, NQ, Dv, nrel, k_per_r, shared_bias, diff, has_sel, lam_init):
    it = iter(refs)
    qt_ref, k_ref, vt_ref, bt_ref = (next(it) for _ in range(4))
    sel_ref = next(it) if has_sel else None
    dl_ref = next(it) if diff else None
    sub_ref = next(it) if diff else None
    o_ref, m_ref, acc_ref, s_ref, p_ref, al_ref, tmax_ref = (next(it) for _ in range(7))

    m_ref[...] = jnp.full(m_ref.shape, NEG_INF, F32)
    acc_ref[...] = jnp.zeros(acc_ref.shape, F32)
    bias_r = (lambda r: 0) if shared_bias else (lambda r: r)
    if has_sel:
        pad_rows = jnp.zeros((k_ref.shape[-1] - HEAD_DIM - MASK_ROWS, T), BF16)
    n_steps = NQ * (NQ + 1) // 2
    N_S = SCORE_SLOTS
    assert N_S % 2 == 0 and n_steps % N_S == 0

    def score_stage(I, J, slot):
        off = pl.multiple_of(J * T, T)
        rel = jnp.minimum(I - J, nrel - 1)
        for r in range(R):
            kt = k_ref[0, 0, r, pl.ds(off, T), :] if k_per_r else k_ref[0, 0, pl.ds(off, T), :]
            w = qt_ref[0, 0, r, I]
            if has_sel:
                w = jnp.concatenate([w, sel_ref[0, 0, I, J], pad_rows], axis=0)
            s = jnp.dot(kt, w, preferred_element_type=F32) + bt_ref[0, rel, bias_r(r)]
            s_ref[slot, r] = s
            tmax_ref[slot, r] = jnp.max(s, axis=0, keepdims=True)

    def softmax_stage(I, s_slot, slot):
        for r in range(R):
            m_prev = m_ref[I, r]
            m_new = jnp.maximum(m_prev, tmax_ref[s_slot, r])
            al_ref[slot, r] = jnp.exp2(m_prev - m_new)
            p_ref[slot, r] = jnp.exp2(s_ref[s_slot, r] - m_new).astype(BF16)
            m_ref[I, r] = m_new

    def value_stage(J, slot):
        vt = vt_ref[0, 0, J]
        return [jnp.dot(vt, p_ref[slot, r], preferred_element_type=F32) for r in range(R)]

    def accumulate(I, slot, pvs):
        for r in range(R):
            acc_ref[I, r] = al_ref[slot, r] * acc_ref[I, r] + pvs[r]

    def advance(I, J):
        last = J == I
        return jnp.where(last, I + 1, I), jnp.where(last, 0, J + 1)

    def step(cur, prev, t):
        other = 1 - t % 2
        nxt = advance(*cur)
        pvs = value_stage(prev[1], other)
        score_stage(jnp.minimum(nxt[0], NQ - 1), nxt[1], (t + 1) % N_S)
        softmax_stage(cur[0], t % N_S, t % 2)
        accumulate(prev[0], other, pvs)
        return nxt

    def trip(u, carry):
        cur, prev = carry[:2], carry[2:]
        for t in range(N_S):
            cur, prev = step(cur, prev, t), cur
        return (*cur, *prev)

    zero = jnp.int32(0)
    score_stage(zero, zero, 0)
    p_ref[1] = jnp.zeros(p_ref.shape[1:], BF16)
    al_ref[1] = jnp.ones(al_ref.shape[1:], F32)
    lax.fori_loop(0, n_steps // N_S, trip, (zero, zero, zero, zero))
    accumulate(NQ - 1, 1, value_stage(NQ - 1, 1))

    if diff:
        dl = dl_ref[...]
        lam = (jnp.exp(jnp.sum(dl[0:1] * dl[1:2], keepdims=True))
               - jnp.exp(jnp.sum(dl[2:3] * dl[3:4], keepdims=True)) + lam_init)

    def finish(I, carry):
        outs = []
        for r in range(R):
            acc = acc_ref[I, r]
            outs.append(acc[:Dv] / acc[Dv:Dv + 1])
        rows = pl.ds(pl.multiple_of(I * T, T), T)
        if diff:
            o = (outs[0] - lam * outs[1]).T
            o_ref[0, rows, :] = _rms(o, sub_ref[...]) * (1.0 - lam_init)
        else:
            o_ref[0, rows, :] = jnp.concatenate(outs, axis=0).T
        return carry

    lax.fori_loop(0, NQ, finish, 0)


def _flash_attn(qt, k, vt, bias_tab, *, sel=None, dl=None, subln=None, lam_init=0.0, name="attn"):
    B, G, R, NQ, _, T = qt.shape
    S = NQ * T
    Dv = vt.shape[-2] - ONES_ROWS
    nrel, bias_heads = bias_tab.shape[1], bias_tab.shape[2]
    k_per_r = k.ndim == 5
    has_sel = sel is not None
    diff = dl is not None
    ins = [qt, k, vt, bias_tab]
    in_specs = [
        pl.BlockSpec((1, 1, R, NQ, HEAD_DIM, T), lambda b, g: (b, g, 0, 0, 0, 0)),
        (pl.BlockSpec((1, 1, R, S, HEAD_DIM), lambda b, g: (b, g, 0, 0, 0)) if k_per_r
         else pl.BlockSpec((1, 1, S, k.shape[-1]), lambda b, g: (b, g, 0, 0))),
        pl.BlockSpec((1, 1, NQ, Dv + ONES_ROWS, T), lambda b, g: (b, g, 0, 0, 0)),
        pl.BlockSpec((1, nrel, bias_heads, T, T), lambda b, g: (g, 0, 0, 0, 0)),
    ]
    if has_sel:
        ins.append(sel)
        in_specs.append(pl.BlockSpec((1, 1, NQ, NQ, MASK_ROWS, T), lambda b, g: (b, g, 0, 0, 0, 0)))
    if diff:
        ins += [dl, subln]
        in_specs += [pl.BlockSpec((4, HEAD_DIM), lambda b, g: (0, 0)),
                     pl.BlockSpec((1, A_VDIM), lambda b, g: (0, 0))]
    out_w = Dv if diff else R * Dv
    kern = functools.partial(_flash_kernel, R=R, T=T, NQ=NQ, Dv=Dv, nrel=nrel, k_per_r=k_per_r,
                             shared_bias=bias_heads == 1, diff=diff, has_sel=has_sel, lam_init=lam_init)
    return pl.pallas_call(
        kern,
        grid=(B, G),
        in_specs=in_specs,
        out_specs=pl.BlockSpec((1, S, out_w), lambda b, g: (b, 0, g)),
        out_shape=jax.ShapeDtypeStruct((B, S, G * out_w), F32),
        scratch_shapes=[pltpu.VMEM((NQ, R, 1, T), F32), pltpu.VMEM((NQ, R, Dv + ONES_ROWS, T), F32),
                        pltpu.VMEM((SCORE_SLOTS, R, T, T), F32), pltpu.VMEM((2, R, T, T), BF16),
                        pltpu.VMEM((2, R, 1, T), F32), pltpu.VMEM((SCORE_SLOTS, R, 1, T), F32)],
        compiler_params=_params(2),
        name=name,
    )(*ins)


def _window_kernel(*refs, R, T, NK, TV, Dv, has_sink):
    it = iter(refs)
    qt_ref, k_ref, vt_ref, bt_ref = (next(it) for _ in range(4))
    sink_ref = next(it) if has_sink else None
    o_ref = next(it)
    I = pl.program_id(2)
    entry = jnp.minimum(I, -(-(NK - T) // T))
    first_key = pl.multiple_of(I * T - jnp.minimum(entry * T, NK - T), TV)
    kt = k_ref[0, 0, pl.ds(first_key, NK), :]
    scores = [jnp.dot(kt, qt_ref[0, 0, r, 0], preferred_element_type=F32) for r in range(R)]
    probs, maxes = [], []
    for r in range(R):
        s = scores[r] + bt_ref[0, entry, r]
        m = jnp.max(s, axis=0, keepdims=True)
        probs.append(jnp.exp2(s - m).astype(BF16))
        maxes.append(m)
    outs = []
    for r in range(R):
        acc = None
        for c in range(NK // TV):
            part = jnp.dot(vt_ref[0, 0, first_key // TV + c], probs[r][c * TV:(c + 1) * TV],
                           preferred_element_type=F32)
            acc = part if acc is None else acc + part
        num, l = acc[:Dv], acc[Dv:Dv + 1]
        if has_sink:
            sk = sink_ref[0, r] * LOG2E
            m_f = jnp.maximum(maxes[r], sk)
            w = jnp.exp2(maxes[r] - m_f)
            outs.append(num * w / (l * w + jnp.exp2(sk - m_f)))
        else:
            outs.append(num / l)
    o_ref[0] = jnp.concatenate(outs, axis=0).T


def _window_attn(qt, k, vt, bias_tab, *, sink=None, name="attn_window"):
    B, G, R, NQ, _, T = qt.shape
    S = NQ * T
    TV = vt.shape[-1]
    Dv = vt.shape[-2] - ONES_ROWS
    entries, NK = bias_tab.shape[1], bias_tab.shape[3]
    has_sink = sink is not None
    ins = [qt, k, vt, bias_tab]
    in_specs = [
        pl.BlockSpec((1, 1, R, 1, HEAD_DIM, T), lambda b, g, i: (b, g, 0, i, 0, 0)),
        pl.BlockSpec((1, 1, S, HEAD_DIM), lambda b, g, i: (b, g, 0, 0)),
        pl.BlockSpec((1, 1, S // TV, Dv + ONES_ROWS, TV), lambda b, g, i: (b, g, 0, 0, 0)),
        pl.BlockSpec((1, entries, R, NK, T), lambda b, g, i: (g, 0, 0, 0, 0)),
    ]
    if has_sink:
        ins.append(sink)
        in_specs.append(pl.BlockSpec((1, R, 1, 1), lambda b, g, i: (g, 0, 0, 0)))
    kern = functools.partial(_window_kernel, R=R, T=T, NK=NK, TV=TV, Dv=Dv, has_sink=has_sink)
    return pl.pallas_call(
        kern,
        grid=(B, G, S // T),
        in_specs=in_specs,
        out_specs=pl.BlockSpec((1, T, R * Dv), lambda b, g, i: (b, i, g)),
        out_shape=jax.ShapeDtypeStruct((B, S, G * R * Dv), F32),
        compiler_params=_params(3),
        name=name,
    )(*ins)


def _bucket_np(dist):
    n = np.maximum(dist, 0)
    max_exact = NUM_BUCKETS // 2
    nf = np.maximum(n, 1).astype(np.float32)
    large = max_exact + (np.log(nf / max_exact) / math.log(MAX_DISTANCE / max_exact)
                         * (NUM_BUCKETS - max_exact)).astype(np.int32)
    large = np.minimum(large, NUM_BUCKETS - 1)
    return np.where(n < max_exact, n, large)


def _bucket_starts():
    b = _bucket_np(np.arange(4 * MAX_DISTANCE))
    assert (np.diff(b) >= 0).all() and b[-1] == NUM_BUCKETS - 1
    return [int(np.argmax(b >= i)) for i in range(NUM_BUCKETS)]


def _bias_kernel(tab_ref, o_ref, *, T, rows, nrel, window, back_max, starts):
    h = pl.program_id(0)
    key = lax.broadcasted_iota(jnp.int32, (rows, T), 0)
    qry = lax.broadcasted_iota(jnp.int32, (rows, T), 1)
    for rel in range(nrel):
        dist = min(rel * T, back_max) + qry - key
        val = jnp.full((rows, T), tab_ref[0, h] * LOG2E, F32)
        for b in range(1, NUM_BUCKETS):
            val = jnp.where(dist >= starts[b], tab_ref[b, h] * LOG2E, val)
        allowed = dist >= 0
        if window is not None:
            allowed = allowed & (dist < window)
        o_ref[0, rel, 0] = jnp.where(allowed, val, NEG_INF)


def _bias_tiles(tab, T, nrel, window, R, rows=None):
    H = tab.shape[1]
    back_max = nrel * T if rows is None else rows - T
    rows = T if rows is None else rows
    return pl.pallas_call(
        functools.partial(_bias_kernel, T=T, rows=rows, nrel=nrel, window=window, back_max=back_max,
                          starts=_bucket_starts()),
        grid=(H,),
        in_specs=[pl.BlockSpec(memory_space=pltpu.SMEM)],
        out_specs=pl.BlockSpec((1, nrel, 1, rows, T), lambda h: (h // R, 0, h % R, 0, 0)),
        out_shape=jax.ShapeDtypeStruct((H // R, nrel, R, rows, T), F32),
        compiler_params=_params(1),
        name="bias_tiles",
    )(tab)


def _merge_kernel(x_ref, nrm_ref, oa_ref, ob_ref, oc_ref, os_ref, ow_ref, gc_ref, ex_ref, wmg_ref, wb_ref, wo_ref,
                  o_ref):
    x = x_ref[...]
    hb = _rms(x, nrm_ref[...]).astype(BF16)
    gc = gc_ref[...]
    g1 = gc.astype(BF16)
    r1 = gc - g1.astype(F32)
    g2 = r1.astype(BF16)
    g3 = (r1 - g2.astype(F32)).astype(BF16)
    ex = ex_ref[...]
    gx = (jnp.dot(g1, ex, preferred_element_type=F32) + jnp.dot(g2, ex, preferred_element_type=F32)
          + jnp.dot(g3, ex, preferred_element_type=F32))
    oc = (gx[:, 0:MIX_WIDTH] * oc_ref[...] + gx[:, MIX_WIDTH:2 * MIX_WIDTH] * os_ref[...]
          + gx[:, 2 * MIX_WIDTH:3 * MIX_WIDTH] * ow_ref[...])
    z = None
    for n, br in enumerate((oa_ref[...], ob_ref[...], oc)):
        y = jnp.dot(br.astype(BF16), wb_ref[n], preferred_element_type=F32)
        gate = jax.nn.sigmoid(jnp.dot(hb, wmg_ref[:, n * D_MODEL:(n + 1) * D_MODEL], preferred_element_type=F32))
        t = gate * y
        z = t if z is None else z + t
    o_ref[...] = x + jnp.dot(z.astype(BF16), wo_ref[...], preferred_element_type=F32)


def _merge(x2, nrm, oa, ob, oc, osel, ow, gc, expand_g, wmg, wb, wo, tm=512):
    Tn = x2.shape[0]
    row = lambda d: pl.BlockSpec((tm, d), lambda i: (i, 0))
    const = lambda shape: pl.BlockSpec(shape, lambda i: (0,) * len(shape), pipeline_mode=pl.Buffered(1))
    return pl.pallas_call(
        _merge_kernel,
        grid=(Tn // tm,),
        in_specs=[row(D_MODEL), const((1, D_MODEL)),
                  row(MIX_WIDTH), row(MIX_WIDTH), row(MIX_WIDTH), row(MIX_WIDTH), row(MIX_WIDTH),
                  row(CG_PAD),
                  const((CG_PAD, 3 * MIX_WIDTH)),
                  const((D_MODEL, 3 * D_MODEL)),
                  const((3, MIX_WIDTH, D_MODEL)),
                  const((D_MODEL, D_MODEL))],
        out_specs=row(D_MODEL),
        out_shape=jax.ShapeDtypeStruct((Tn, D_MODEL), F32),
        compiler_params=_params(1),
        name="merge",
    )(x2, nrm, oa, ob, oc, osel, ow, gc, expand_g, wmg, wb, wo)


def _mlp_kernel(x_ref, nrm_ref, wu_ref, wd_ref, o_ref, h_ref, acc_ref):
    f = pl.program_id(1)

    @pl.when(f == 0)
    def _():
        h_ref[...] = _rms(x_ref[...], nrm_ref[...]).astype(BF16)
        acc_ref[...] = jnp.zeros(acc_ref.shape, F32)

    u = jnp.dot(h_ref[...], wu_ref[...], preferred_element_type=F32)
    u = jnp.square(jnp.maximum(u, 0.0)).astype(BF16)
    acc_ref[...] += jnp.dot(u, wd_ref[...], preferred_element_type=F32)

    @pl.when(f == pl.num_programs(1) - 1)
    def _():
        o_ref[...] = x_ref[...] + acc_ref[...]


def _mlp(x2, nrm, wu, wd, tm=1024, tf=1024):
    Tn = x2.shape[0]
    return pl.pallas_call(
        _mlp_kernel,
        grid=(Tn // tm, D_FF // tf),
        in_specs=[pl.BlockSpec((tm, D_MODEL), lambda i, f: (i, 0)),
                  pl.BlockSpec((1, D_MODEL), lambda i, f: (0, 0)),
                  pl.BlockSpec((D_MODEL, tf), lambda i, f: (0, f)),
                  pl.BlockSpec((tf, D_MODEL), lambda i, f: (f, 0))],
        out_specs=pl.BlockSpec((tm, D_MODEL), lambda i, f: (i, 0)),
        out_shape=jax.ShapeDtypeStruct((Tn, D_MODEL), F32),
        scratch_shapes=[pltpu.VMEM((tm, D_MODEL), BF16), pltpu.VMEM((tm, D_MODEL), F32)],
        compiler_params=_params(2),
        name="mlp",
    )(x2, nrm, wu, wd)


def _overlap(n_cmp):
    c_start = np.arange(NC_PAD) * CMP_STRIDE
    j_start = np.arange(N_SEL_BLK) * SEL_BLOCK
    ov = ((c_start[None, :] < j_start[:, None] + SEL_BLOCK) & (c_start[None, :] + CMP_BLOCK > j_start[:, None])
          & (np.arange(NC_PAD)[None, :] < n_cmp))
    return jnp.asarray(ov.astype(np.float32), BF16)


def _gate_expand():
    e = np.zeros((CG_PAD, 3 * MIX_WIDTH), np.float32)
    for h in range(C_HEADS):
        for j in range(3):
            e[h * 3 + j, j * MIX_WIDTH + h * HEAD_DIM:j * MIX_WIDTH + (h + 1) * HEAD_DIM] = 1.0
    return jnp.asarray(e, BF16)


def kernel(x, w_in, qk_gain, diff_lambda, diff_subln, sinks, cmp_pos, cmp_w1, cmp_w2,
           w_branch, w_out, norm_mix, norm_mlp, w_up, w_down, rel_bias):
    B, S, _ = x.shape
    depth = w_in.shape[0]
    n_cmp = (S - CMP_BLOCK) // CMP_STRIDE + 1
    assert S % T_ROW == 0 and S // CMP_STRIDE == NC_PAD and S // SEL_BLOCK == N_SEL_BLK
    half = CMP_BLOCK // 2 * HEAD_DIM

    w_heads = w_in[:, :, :C_CG].astype(BF16)
    w_cg = jnp.pad(w_in[:, :, C_CG:C_CG + N_CG], ((0, 0), (0, 0), (0, CG_PAD - N_CG))).astype(BF16)
    w_mg = w_in[:, :, C_CG + N_CG:].astype(BF16)
    gains2 = jnp.concatenate([qk_gain, qk_gain], axis=-1)
    w1 = cmp_w1.astype(BF16).reshape(depth, 2, 2, half, CMP_HIDDEN)
    w2p = jnp.pad(cmp_w2, ((0, 0), (0, 0), (0, 0), (0, LANES - HEAD_DIM))).astype(BF16)
    pos = cmp_pos.reshape(depth, 2, 2, half)
    wb = w_branch.astype(BF16)
    wo = w_out.astype(BF16)
    wu = w_up.astype(BF16)
    wd = w_down.astype(BF16)

    bias_a = rel_bias[:, :A_HEADS]
    bias_b = rel_bias[:, A_HEADS:A_HEADS + B_HEADS]
    bias_c = rel_bias[:, A_HEADS + B_HEADS:]
    bt_a = _bias_tiles(bias_a, T_ROW, 3, None, 1)
    bt_b = _bias_tiles(bias_b, T_SWA, -(-B_WINDOW // T_SWA) + 1, B_WINDOW, GROUP, rows=B_WINDOW + T_SWA)
    bt_sel = _bias_tiles(bias_c, T_ROW, 3, None, GROUP)
    bt_win = _bias_tiles(bias_c, T_ROW, C_WINDOW // T_ROW + 1, C_WINDOW, GROUP, rows=C_WINDOW + T_ROW)
    ovl = _overlap(n_cmp)
    expand_g = _gate_expand()

    for layer in range(depth):
        lam_init = 0.8 - 0.6 * math.exp(-0.3 * layer)
        (qat, ka, vat, qbt, kb, vbt, qct, tkv, ksel, vselt, kwin, vwint, gc) = _inproj(
            x, norm_mix[layer][None], w_heads[layer], w_cg[layer], gains2[layer])
        oa = _flash_attn(qat, ka, vat, bt_a, dl=diff_lambda[layer],
                   subln=diff_subln[layer][None], lam_init=lam_init, name="attn_diff")
        ob = _window_attn(qbt, kb, vbt, bt_b,
                          sink=sinks[layer].reshape(KV_GROUPS, GROUP, 1, 1), name="attn_swa")
        kc, vct = _compress(tkv.reshape(B, 2, KV_GROUPS, NC_PAD, half), pos[layer],
                            w1[layer], w2p[layer], qk_gain[layer][5:6])
        ocmp, sel = _cmp_attn(qct, kc, vct, ovl, n_cmp)
        osel = _flash_attn(qct, ksel, vselt, bt_sel, sel=sel, name="attn_sel")
        owin = _window_attn(qct, kwin, vwint, bt_win, name="attn_win")
        f2 = lambda a: a.reshape(B * S, a.shape[-1])
        x2 = _merge(f2(x), norm_mix[layer][None], f2(oa), f2(ob), f2(ocmp), f2(osel), f2(owin), f2(gc),
                    expand_g, w_mg[layer], wb[layer], wo[layer])
        x2 = _mlp(x2, norm_mlp[layer][None], wu[layer], wd[layer])
        x = x2.reshape(B, S, D_MODEL)
    return x
```

```python
import functools
import math

import numpy as np
import jax
import jax.numpy as jnp
from jax import lax
from jax.experimental import pallas as pl
from jax.experimental.pallas import tpu as pltpu

F32 = jnp.float32
BF16 = jnp.bfloat16

D_MODEL = 1024
HEAD_DIM = 64
A_HEADS = 4
A_VDIM = 128
B_HEADS = 8
B_WINDOW = 128
C_HEADS = 8
KV_GROUPS = 2
GROUP = 4
CMP_BLOCK = 32
CMP_STRIDE = 16
CMP_HIDDEN = 256
SEL_BLOCK = 64
N_SELECT = 16
C_WINDOW = 512
MIX_WIDTH = 512
D_FF = 4096
NUM_BUCKETS = 32
MAX_DISTANCE = 128
NEG_INF = -1e30
FORCE_BONUS = 1e4
EPS = 1e-6
LOG2E = 1.4426950408889634
QK_SCALE = HEAD_DIM ** -0.5 * LOG2E

C_AQ, C_AK, C_AV, C_BQ, C_BKV, C_CQ, C_CKV, C_CG = 0, 512, 1024, 1536, 2048, 2304, 2816, 3584
N_CG = C_HEADS * 3
LANES = 128
CG_PAD = LANES
NC_PAD = 256
N_SEL_BLK = 64
ONES_ROWS = 16
MASK_ROWS = 16

SCORE_SLOTS = 4

T_IN = 512
T_ROW = 256
T_SWA = 256
TV_SWA = 128

VMEM_LIMIT = 48 * 1024 * 1024


def _rms(x, gain):
    ms = jnp.mean(x * x, axis=-1, keepdims=True)
    return (x * lax.rsqrt(ms + EPS)) * gain


def _params(n_axes):
    return pltpu.CompilerParams(dimension_semantics=("arbitrary",) * n_axes,
                                vmem_limit_bytes=VMEM_LIMIT)


def _inproj_kernel(x_ref, nrm_ref, w_ref, wcg_ref, gains_ref,
                   qa_ref, ka_ref, va_ref, qb_ref, kb_ref, vb_ref, qc_ref,
                   tkv_ref, ksel_ref, vsel_ref, kwin_ref, vwin_ref, gc_ref):
    tm = x_ref.shape[1]
    hb = _rms(x_ref[0], nrm_ref[...]).astype(BF16)
    lane = lax.broadcasted_iota(jnp.int32, (tm, LANES), 1)
    first = lane < HEAD_DIM

    def ones_pad(width):
        return jnp.where(lax.broadcasted_iota(jnp.int32, (ONES_ROWS, width), 0) == 0, 1.0, 0.0).astype(BF16)

    def mm(c0, n):
        return jnp.dot(hb, w_ref[:, c0:c0 + n], preferred_element_type=F32)

    def norm_pair(y, gain_idx, scale=None):
        sq = y * y
        ms0 = jnp.sum(jnp.where(first, sq, 0.0), axis=-1, keepdims=True) * (1.0 / HEAD_DIM)
        ms1 = jnp.sum(jnp.where(first, 0.0, sq), axis=-1, keepdims=True) * (1.0 / HEAD_DIM)
        inv = jnp.where(first, lax.rsqrt(ms0 + EPS), lax.rsqrt(ms1 + EPS))
        out = (y * inv) * gains_ref[gain_idx:gain_idx + 1, :]
        return out if scale is None else out * scale

    def pairs(c0):
        r = mm(c0, 256)
        return r[:, :LANES], r[:, LANES:]

    def put_qt(ref, i0, i1, y):
        yt = y.T.astype(BF16)
        n_tiles = ref.shape[3]
        tq = tm // n_tiles
        for t in range(n_tiles):
            ref[0, i0, i1, t] = yt[:HEAD_DIM, tq * t:tq * (t + 1)]
            ref[0, i0, i1 + 1, t] = yt[HEAD_DIM:, tq * t:tq * (t + 1)]

    def put_vt(ref, g0, y, n_tiles):
        yt = y.T.astype(BF16)
        tk = tm // n_tiles
        for g in range(2):
            for t in range(n_tiles):
                ref[0, g0 + g, t, 0:HEAD_DIM, :] = yt[HEAD_DIM * g:HEAD_DIM * (g + 1), tk * t:tk * (t + 1)]
                ref[0, g0 + g, t, HEAD_DIM:HEAD_DIM + ONES_ROWS, :] = ones_pad(tk)

    for ch in range(2):
        for half, y in enumerate(pairs(C_AQ + 256 * ch)):
            put_qt(qa_ref, ch * 2 + half, 0, norm_pair(y, 0, QK_SCALE))
        for half, y in enumerate(pairs(C_AK + 256 * ch)):
            y = norm_pair(y, 1).astype(BF16)
            ka_ref[0, ch * 2 + half, 0] = y[:, :HEAD_DIM]
            ka_ref[0, ch * 2 + half, 1] = y[:, HEAD_DIM:]
        for half, y in enumerate(pairs(C_AV + 256 * ch)):
            yt = y.T.astype(BF16)
            for t in range(tm // T_ROW):
                va_ref[0, ch * 2 + half, t, 0:A_VDIM, :] = yt[:, T_ROW * t:T_ROW * (t + 1)]
                va_ref[0, ch * 2 + half, t, A_VDIM:A_VDIM + ONES_ROWS, :] = ones_pad(T_ROW)
        for half, y in enumerate(pairs(C_BQ + 256 * ch)):
            put_qt(qb_ref, ch, 2 * half, norm_pair(y, 2, QK_SCALE))
        for half, y in enumerate(pairs(C_CQ + 256 * ch)):
            put_qt(qc_ref, ch, 2 * half, norm_pair(y, 4, QK_SCALE))

    def put_k(ref, y):
        y = y.astype(BF16)
        ref[0, 0] = y[:, :HEAD_DIM]
        ref[0, 1] = y[:, HEAD_DIM:]

    yk, yv = pairs(C_BKV)
    put_k(kb_ref, norm_pair(yk, 3))
    put_vt(vb_ref, 0, yv, tm // TV_SWA)
    y0, y1 = pairs(C_CKV)
    for kv, y in enumerate((y0, y1)):
        tkv_ref[0, kv, 0] = y[:, :HEAD_DIM]
        tkv_ref[0, kv, 1] = y[:, HEAD_DIM:]
    yk, yv = pairs(C_CKV + 256)
    yk = norm_pair(yk, 6)
    row = lax.broadcasted_iota(jnp.int32, (tm, LANES), 0)
    hot = jnp.where(lane == HEAD_DIM + (row // SEL_BLOCK) % (T_ROW // SEL_BLOCK), 1.0, 0.0)
    ksel_ref[0, 0] = jnp.where(first, yk, hot).astype(BF16)
    ksel_ref[0, 1] = jnp.where(first, pltpu.roll(yk, HEAD_DIM, 1), hot).astype(BF16)
    put_vt(vsel_ref, 0, yv, tm // T_ROW)
    yk, yv = pairs(C_CKV + 512)
    put_k(kwin_ref, norm_pair(yk, 7))
    put_vt(vwin_ref, 0, yv, tm // T_ROW)

    gc_ref[0] = jax.nn.sigmoid(jnp.dot(hb, wcg_ref[...], preferred_element_type=F32))


def _inproj(x, nrm, w, wcg, gains2):
    B, S, _ = x.shape
    tm = T_IN
    nt = S // tm
    sd = jax.ShapeDtypeStruct
    qt = lambda a, b, tq: sd((B, a, b, S // tq, HEAD_DIM, tq), BF16)
    kk = lambda n: sd((B, n, S, HEAD_DIM), BF16)
    vt = lambda n, dv, tk: sd((B, n, S // tk, dv + ONES_ROWS, tk), BF16)
    out_shape = (
        qt(A_HEADS, 2, T_ROW), sd((B, A_HEADS, 2, S, HEAD_DIM), BF16), vt(A_HEADS, A_VDIM, T_ROW),
        qt(KV_GROUPS, GROUP, T_SWA), kk(KV_GROUPS), vt(KV_GROUPS, HEAD_DIM, TV_SWA),
        qt(KV_GROUPS, GROUP, T_ROW),
        sd((B, 2, KV_GROUPS, S, HEAD_DIM), F32),
        sd((B, KV_GROUPS, S, LANES), BF16), vt(KV_GROUPS, HEAD_DIM, T_ROW),
        kk(KV_GROUPS), vt(KV_GROUPS, HEAD_DIM, T_ROW),
        sd((B, S, CG_PAD), F32),
    )
    s_qt = lambda a, b, tq: pl.BlockSpec((1, a, b, tm // tq, HEAD_DIM, tq), lambda b_, i: (b_, 0, 0, i, 0, 0))
    s_k = lambda n: pl.BlockSpec((1, n, tm, HEAD_DIM), lambda b_, i: (b_, 0, i, 0))
    s_k5 = lambda a, c: pl.BlockSpec((1, a, c, tm, HEAD_DIM), lambda b_, i: (b_, 0, 0, i, 0))
    s_vt = lambda n, dv, tk: pl.BlockSpec((1, n, tm // tk, dv + ONES_ROWS, tk), lambda b_, i: (b_, 0, i, 0, 0))
    row = lambda d: pl.BlockSpec((1, tm, d), lambda b_, i: (b_, i, 0))
    out_specs = (
        s_qt(A_HEADS, 2, T_ROW), s_k5(A_HEADS, 2), s_vt(A_HEADS, A_VDIM, T_ROW),
        s_qt(KV_GROUPS, GROUP, T_SWA), s_k(KV_GROUPS), s_vt(KV_GROUPS, HEAD_DIM, TV_SWA),
        s_qt(KV_GROUPS, GROUP, T_ROW), s_k5(2, KV_GROUPS),
        pl.BlockSpec((1, KV_GROUPS, tm, LANES), lambda b_, i: (b_, 0, i, 0)), s_vt(KV_GROUPS, HEAD_DIM, T_ROW),
        s_k(KV_GROUPS), s_vt(KV_GROUPS, HEAD_DIM, T_ROW),
        row(CG_PAD),
    )
    return pl.pallas_call(
        _inproj_kernel,
        grid=(B, nt),
        in_specs=[
            row(D_MODEL),
            pl.BlockSpec((1, D_MODEL), lambda b_, i: (0, 0)),
            pl.BlockSpec((D_MODEL, C_CG), lambda b_, i: (0, 0), pipeline_mode=pl.Buffered(1)),
            pl.BlockSpec((D_MODEL, CG_PAD), lambda b_, i: (0, 0)),
            pl.BlockSpec((8, LANES), lambda b_, i: (0, 0)),
        ],
        out_specs=out_specs,
        out_shape=out_shape,
        compiler_params=_params(2),
        name="inproj",
    )(x, nrm, w, wcg, gains2)


def _compress_kernel(t_ref, pos_ref, w1_ref, w2_ref, gain_ref, o_ref, ot_ref):
    t = t_ref[0, 0, 0]
    lo = (t + pos_ref[0, 0:1, :]).astype(BF16)
    hi = (t + pos_ref[0, 1:2, :]).astype(BF16)
    v = jnp.dot(lo, w1_ref[0, 0], preferred_element_type=F32)
    u = jnp.dot(hi, w1_ref[0, 1], preferred_element_type=F32)
    pre = v + pltpu.roll(u, NC_PAD - 1, 0)
    hcur = jax.nn.gelu(pre).astype(BF16)
    out = jnp.dot(hcur, w2_ref[0], preferred_element_type=F32)
    o64 = out[:, :HEAD_DIM]
    is_key = pl.program_id(1) == 0
    o_ref[0, 0, 0] = jnp.where(is_key, _rms(o64, gain_ref[...]), o64).astype(BF16)
    ot_ref[0, 0, 0] = out.T[:HEAD_DIM].astype(BF16)


def _compress(tkv16, pos, w1, w2p, gain):
    B = tkv16.shape[0]
    half = CMP_BLOCK // 2 * HEAD_DIM
    return pl.pallas_call(
        _compress_kernel,
        grid=(B, 2, KV_GROUPS),
        in_specs=[
            pl.BlockSpec((1, 1, 1, NC_PAD, half), lambda b, kv, g: (b, kv, g, 0, 0)),
            pl.BlockSpec((1, 2, half), lambda b, kv, g: (kv, 0, 0)),
            pl.BlockSpec((1, 2, half, CMP_HIDDEN), lambda b, kv, g: (kv, 0, 0, 0)),
            pl.BlockSpec((1, CMP_HIDDEN, LANES), lambda b, kv, g: (kv, 0, 0)),
            pl.BlockSpec((1, HEAD_DIM), lambda b, kv, g: (0, 0)),
        ],
        out_specs=(pl.BlockSpec((1, 1, 1, NC_PAD, HEAD_DIM), lambda b, kv, g: (b, kv, g, 0, 0)),
                   pl.BlockSpec((1, 1, 1, HEAD_DIM, NC_PAD), lambda b, kv, g: (b, kv, g, 0, 0))),
        out_shape=(jax.ShapeDtypeStruct((B, 2, KV_GROUPS, NC_PAD, HEAD_DIM), BF16),
                   jax.ShapeDtypeStruct((B, 2, KV_GROUPS, HEAD_DIM, NC_PAD), BF16)),
        compiler_params=_params(3),
        name="compress",
    )(tkv16, pos, w1, w2p, gain)


def _cmp_attn_kernel(qt_ref, kc_ref, vct_ref, ovl_ref, o_ref, sel_ref, score_ref, rank_ref, *, T, n_cmp):
    R = GROUP
    I = pl.program_id(2)
    kc = kc_ref[0, 0, 0]
    vct = vct_ref[0, 0, 0]
    c_idx = lax.broadcasted_iota(jnp.int32, (NC_PAD, T), 0)
    t_pos = I * T + lax.broadcasted_iota(jnp.int32, (NC_PAD, T), 1)
    cmask = (c_idx * CMP_STRIDE + (CMP_BLOCK - 1) <= t_pos) & (c_idx < n_cmp)
    scores = [jnp.dot(kc, qt_ref[0, 0, r, 0], preferred_element_type=F32) for r in range(R)]
    probs = []
    for r in range(R):
        s = jnp.where(cmask, scores[r], NEG_INF)
        m = jnp.max(s, axis=0, keepdims=True)
        e = jnp.where(cmask, jnp.exp2(s - m), 0.0)
        l = jnp.sum(e, axis=0, keepdims=True)
        probs.append(e * (1.0 / jnp.where(l > 0.0, l, 1.0)))
    outs = [jnp.dot(vct, p.astype(BF16), preferred_element_type=F32) for p in probs]
    o_ref[0] = jnp.concatenate(outs, axis=0).T
    psum = (probs[0] + probs[1]) + (probs[2] + probs[3])

    p_hi = psum.astype(BF16)
    p_lo = (psum - p_hi.astype(F32)).astype(BF16)
    ovl = ovl_ref[...]
    imp = jnp.dot(ovl, p_hi, preferred_element_type=F32) + jnp.dot(ovl, p_lo, preferred_element_type=F32)
    j_idx = lax.broadcasted_iota(jnp.int32, (N_SEL_BLK, T), 0)
    tq = I * T + lax.broadcasted_iota(jnp.int32, (N_SEL_BLK, T), 1)
    cur = tq // SEL_BLOCK
    valid = j_idx * SEL_BLOCK <= tq
    forced = (j_idx == 0) | (j_idx == cur) | (j_idx == cur - 1)
    score = jnp.where(valid, imp + jnp.where(forced, FORCE_BONUS, 0.0), NEG_INF)
    score_ref[...] = score
    rank_ref[...] = jnp.zeros(rank_ref.shape, F32)
    n_live = (I + 1) * (T // SEL_BLOCK)
    sub = 8
    j_loc = lax.broadcasted_iota(jnp.int32, (sub, T), 0)
    for c in range(N_SEL_BLK // sub):
        @pl.when(c * sub < n_live)
        def _(c=c):
            n_grp = N_SEL_BLK // sub
            grps = [score_ref[g * sub:(g + 1) * sub, :] for g in range(n_grp)]
            cnts = [rank_ref[g * sub:(g + 1) * sub, :] for g in range(n_grp)]
            for jp in range(c * sub, (c + 1) * sub):
                row = jnp.broadcast_to(score_ref[jp:jp + 1, :], (sub, T))
                for g in range(n_grp):
                    if g * sub > jp:
                        beats = row >= grps[g]
                    elif (g + 1) * sub - 1 < jp:
                        beats = row > grps[g]
                    else:
                        beats = (row > grps[g]) | ((row == grps[g]) & (j_loc + g * sub > jp))
                    cnts[g] = cnts[g] + jnp.where(beats, 1.0, 0.0)
            for g in range(n_grp):
                rank_ref[g * sub:(g + 1) * sub, :] = cnts[g]
    blk = T // SEL_BLOCK
    for j in range(N_SEL_BLK // blk):
        keep = rank_ref[j * blk:(j + 1) * blk, :] < N_SELECT
        slab = jnp.concatenate([jnp.where(keep, 0.0, NEG_INF), jnp.zeros((MASK_ROWS - blk, T), F32)], axis=0)
        sel_ref[0, 0, 0, j] = slab.astype(BF16)


def _cmp_attn(qct, kc, vct, ovl, n_cmp):
    B, G, R, NQ, _, T = qct.shape
    S = NQ * T
    return pl.pallas_call(
        functools.partial(_cmp_attn_kernel, T=T, n_cmp=n_cmp),
        grid=(B, G, S // T),
        in_specs=[
            pl.BlockSpec((1, 1, R, 1, HEAD_DIM, T), lambda b, g, i: (b, g, 0, i, 0, 0)),
            pl.BlockSpec((1, 1, 1, NC_PAD, HEAD_DIM), lambda b, g, i: (b, 0, g, 0, 0)),
            pl.BlockSpec((1, 1, 1, HEAD_DIM, NC_PAD), lambda b, g, i: (b, 1, g, 0, 0)),
            pl.BlockSpec((N_SEL_BLK, NC_PAD), lambda b, g, i: (0, 0)),
        ],
        out_specs=(
            pl.BlockSpec((1, T, R * HEAD_DIM), lambda b, g, i: (b, i, g)),
            pl.BlockSpec((1, 1, 1, NQ, MASK_ROWS, T), lambda b, g, i: (b, g, i, 0, 0, 0)),
        ),
        out_shape=(
            jax.ShapeDtypeStruct((B, S, G * R * HEAD_DIM), F32),
            jax.ShapeDtypeStruct((B, G, NQ, NQ, MASK_ROWS, T), BF16),
        ),
        scratch_shapes=[pltpu.VMEM((N_SEL_BLK, T), F32), pltpu.VMEM((N_SEL_BLK, T), F32)],
        compiler_params=_params(3),
        name="cmp_attn",
    )(qct, kc, vct, ovl)


def _flash_kernel(*refs, R, T, NQ, Dv, nrel, k_per_r, shared_bias, diff, has_sel, lam_init):
    it = iter(refs)
    qt_ref, k_ref, vt_ref, bt_ref = (next(it) for _ in range(4))
    sel_ref = next(it) if has_sel else None
    dl_ref = next(it) if diff else None
    sub_ref = next(it) if diff else None
    o_ref, m_ref, acc_ref, s_ref, p_ref, al_ref, tmax_ref = (next(it) for _ in range(7))

    m_ref[...] = jnp.full(m_ref.shape, NEG_INF, F32)
    acc_ref[...] = jnp.zeros(acc_ref.shape, F32)
    bias_r = (lambda r: 0) if shared_bias else (lambda r: r)
    if has_sel:
        pad_rows = jnp.zeros((k_ref.shape[-1] - HEAD_DIM - MASK_ROWS, T), BF16)
    n_steps = NQ * (NQ + 1) // 2
    N_S = SCORE_SLOTS
    assert N_S % 2 == 0 and n_steps % N_S == 0

    def score_stage(I, J, slot):
        off = pl.multiple_of(J * T, T)
        rel = jnp.minimum(I - J, nrel - 1)
        for r in range(R):
            kt = k_ref[0, 0, r, pl.ds(off, T), :] if k_per_r else k_ref[0, 0, pl.ds(off, T), :]
            w = qt_ref[0, 0, r, I]
            if has_sel:
                w = jnp.concatenate([w, sel_ref[0, 0, I, J], pad_rows], axis=0)
            s = jnp.dot(kt, w, preferred_element_type=F32) + bt_ref[0, rel, bias_r(r)]
            s_ref[slot, r] = s
            tmax_ref[slot, r] = jnp.max(s, axis=0, keepdims=True)

    def softmax_stage(I, s_slot, slot):
        for r in range(R):
            m_prev = m_ref[I, r]
            m_new = jnp.maximum(m_prev, tmax_ref[s_slot, r])
            al_ref[slot, r] = jnp.exp2(m_prev - m_new)
            p_ref[slot, r] = jnp.exp2(s_ref[s_slot, r] - m_new).astype(BF16)
            m_ref[I, r] = m_new

    def value_stage(J, slot):
        vt = vt_ref[0, 0, J]
        return [jnp.dot(vt, p_ref[slot, r], preferred_element_type=F32) for r in range(R)]

    def accumulate(I, slot, pvs):
        for r in range(R):
            acc_ref[I, r] = al_ref[slot, r] * acc_ref[I, r] + pvs[r]

    def advance(I, J):
        last = J == I
        return jnp.where(last, I + 1, I), jnp.where(last, 0, J + 1)

    def step(cur, prev, t):
        other = 1 - t % 2
        nxt = advance(*cur)
        pvs = value_stage(prev[1], other)
        score_stage(jnp.minimum(nxt[0], NQ - 1), nxt[1], (t + 1) % N_S)
        softmax_stage(cur[0], t % N_S, t % 2)
        accumulate(prev[0], other, pvs)
        return nxt

    def trip(u, carry):
        cur, prev = carry[:2], carry[2:]
        for t in range(N_S):
            cur, prev = step(cur, prev, t), cur
        return (*cur, *prev)

    zero = jnp.int32(0)
    score_stage(zero, zero, 0)
    p_ref[1] = jnp.zeros(p_ref.shape[1:], BF16)
    al_ref[1] = jnp.ones(al_ref.shape[1:], F32)
    lax.fori_loop(0, n_steps // N_S, trip, (zero, zero, zero, zero))
    accumulate(NQ - 1, 1, value_stage(NQ - 1, 1))

    if diff:
        dl = dl_ref[...]
        lam = (jnp.exp(jnp.sum(dl[0:1] * dl[1:2], keepdims=True))
               - jnp.exp(jnp.sum(dl[2:3] * dl[3:4], keepdims=True)) + lam_init)

    def finish(I, carry):
        outs = []
        for r in range(R):
            acc = acc_ref[I, r]
            outs.append(acc[:Dv] / acc[Dv:Dv + 1])
        rows = pl.ds(pl.multiple_of(I * T, T), T)
        if diff:
            o = (outs[0] - lam * outs[1]).T
            o_ref[0, rows, :] = _rms(o, sub_ref[...]) * (1.0 - lam_init)
        else:
            o_ref[0, rows, :] = jnp.concatenate(outs, axis=0).T
        return carry

    lax.fori_loop(0, NQ, finish, 0)


def _flash_attn(qt, k, vt, bias_tab, *, sel=None, dl=None, subln=None, lam_init=0.0, name="attn"):
    B, G, R, NQ, _, T = qt.shape
    S = NQ * T
    Dv = vt.shape[-2] - ONES_ROWS
    nrel, bias_heads = bias_tab.shape[1], bias_tab.shape[2]
    k_per_r = k.ndim == 5
    has_sel = sel is not None
    diff = dl is not None
    ins = [qt, k, vt, bias_tab]
    in_specs = [
        pl.BlockSpec((1, 1, R, NQ, HEAD_DIM, T), lambda b, g: (b, g, 0, 0, 0, 0)),
        (pl.BlockSpec((1, 1, R, S, HEAD_DIM), lambda b, g: (b, g, 0, 0, 0)) if k_per_r
         else pl.BlockSpec((1, 1, S, k.shape[-1]), lambda b, g: (b, g, 0, 0))),
        pl.BlockSpec((1, 1, NQ, Dv + ONES_ROWS, T), lambda b, g: (b, g, 0, 0, 0)),
        pl.BlockSpec((1, nrel, bias_heads, T, T), lambda b, g: (g, 0, 0, 0, 0)),
    ]
    if has_sel:
        ins.append(sel)
        in_specs.append(pl.BlockSpec((1, 1, NQ, NQ, MASK_ROWS, T), lambda b, g: (b, g, 0, 0, 0, 0)))
    if diff:
        ins += [dl, subln]
        in_specs += [pl.BlockSpec((4, HEAD_DIM), lambda b, g: (0, 0)),
                     pl.BlockSpec((1, A_VDIM), lambda b, g: (0, 0))]
    out_w = Dv if diff else R * Dv
    kern = functools.partial(_flash_kernel, R=R, T=T, NQ=NQ, Dv=Dv, nrel=nrel, k_per_r=k_per_r,
                             shared_bias=bias_heads == 1, diff=diff, has_sel=has_sel, lam_init=lam_init)
    return pl.pallas_call(
        kern,
        grid=(B, G),
        in_specs=in_specs,
        out_specs=pl.BlockSpec((1, S, out_w), lambda b, g: (b, 0, g)),
        out_shape=jax.ShapeDtypeStruct((B, S, G * out_w), F32),
        scratch_shapes=[pltpu.VMEM((NQ, R, 1, T), F32), pltpu.VMEM((NQ, R, Dv + ONES_ROWS, T), F32),
                        pltpu.VMEM((SCORE_SLOTS, R, T, T), F32), pltpu.VMEM((2, R, T, T), BF16),
                        pltpu.VMEM((2, R, 1, T), F32), pltpu.VMEM((SCORE_SLOTS, R, 1, T), F32)],
        compiler_params=_params(2),
        name=name,
    )(*ins)


def _window_kernel(*refs, R, T, NK, TV, Dv, n_sub, has_sink):
    it = iter(refs)
    qt_ref, k_ref, vt_ref, bt_ref = (next(it) for _ in range(4))
    sink_ref = next(it) if has_sink else None
    o_ref = next(it)
    heads = [(t, r) for t in range(n_sub) for r in range(R)]
    tile = [pl.program_id(2) * n_sub + t for t in range(n_sub)]
    entry = [jnp.minimum(I, -(-(NK - T) // T)) for I in tile]
    first_key = [pl.multiple_of(I * T - jnp.minimum(e * T, NK - T), TV) for I, e in zip(tile, entry)]
    keys = [k_ref[0, 0, pl.ds(fk, NK), :] for fk in first_key]
    scores = {(t, r): jnp.dot(keys[t], qt_ref[0, 0, r, t], preferred_element_type=F32) for t, r in heads}
    probs, maxes = {}, {}
    for t, r in heads:
        s = scores[t, r] + bt_ref[0, entry[t], r]
        maxes[t, r] = jnp.max(s, axis=0, keepdims=True)
        probs[t, r] = jnp.exp2(s - maxes[t, r]).astype(BF16)
    accs = {}
    for t, r in heads:
        acc = None
        for c in range(NK // TV):
            part = jnp.dot(vt_ref[0, 0, first_key[t] // TV + c], probs[t, r][c * TV:(c + 1) * TV],
                           preferred_element_type=F32)
            acc = part if acc is None else acc + part
        accs[t, r] = acc
    for t in range(n_sub):
        outs = []
        for r in range(R):
            num, l = accs[t, r][:Dv], accs[t, r][Dv:Dv + 1]
            if has_sink:
                sk = sink_ref[0, r] * LOG2E
                m_f = jnp.maximum(maxes[t, r], sk)
                w = jnp.exp2(maxes[t, r] - m_f)
                outs.append(num * w / (l * w + jnp.exp2(sk - m_f)))
            else:
                outs.append(num / l)
        o_ref[0, t * T:(t + 1) * T, :] = jnp.concatenate(outs, axis=0).T


def _window_attn(qt, k, vt, bias_tab, *, n_sub=2, sink=None, name="attn_window"):
    B, G, R, NQ, _, T = qt.shape
    S = NQ * T
    TV = vt.shape[-1]
    Dv = vt.shape[-2] - ONES_ROWS
    entries, NK = bias_tab.shape[1], bias_tab.shape[3]
    has_sink = sink is not None
    ins = [qt, k, vt, bias_tab]
    in_specs = [
        pl.BlockSpec((1, 1, R, n_sub, HEAD_DIM, T), lambda b, g, i: (b, g, 0, i, 0, 0)),
        pl.BlockSpec((1, 1, S, HEAD_DIM), lambda b, g, i: (b, g, 0, 0)),
        pl.BlockSpec((1, 1, S // TV, Dv + ONES_ROWS, TV), lambda b, g, i: (b, g, 0, 0, 0)),
        pl.BlockSpec((1, entries, R, NK, T), lambda b, g, i: (g, 0, 0, 0, 0)),
    ]
    if has_sink:
        ins.append(sink)
        in_specs.append(pl.BlockSpec((1, R, 1, 1), lambda b, g, i: (g, 0, 0, 0)))
    kern = functools.partial(_window_kernel, R=R, T=T, NK=NK, TV=TV, Dv=Dv, n_sub=n_sub, has_sink=has_sink)
    return pl.pallas_call(
        kern,
        grid=(B, G, NQ // n_sub),
        in_specs=in_specs,
        out_specs=pl.BlockSpec((1, n_sub * T, R * Dv), lambda b, g, i: (b, i, g)),
        out_shape=jax.ShapeDtypeStruct((B, S, G * R * Dv), F32),
        compiler_params=_params(3),
        name=name,
    )(*ins)


def _bucket_np(dist):
    n = np.maximum(dist, 0)
    max_exact = NUM_BUCKETS // 2
    nf = np.maximum(n, 1).astype(np.float32)
    large = max_exact + (np.log(nf / max_exact) / math.log(MAX_DISTANCE / max_exact)
                         * (NUM_BUCKETS - max_exact)).astype(np.int32)
    large = np.minimum(large, NUM_BUCKETS - 1)
    return np.where(n < max_exact, n, large)


def _bucket_starts():
    b = _bucket_np(np.arange(4 * MAX_DISTANCE))
    assert (np.diff(b) >= 0).all() and b[-1] == NUM_BUCKETS - 1
    return [int(np.argmax(b >= i)) for i in range(NUM_BUCKETS)]


def _bias_kernel(tab_ref, o_ref, *, T, rows, nrel, window, back_max, starts):
    h = pl.program_id(0)
    key = lax.broadcasted_iota(jnp.int32, (rows, T), 0)
    qry = lax.broadcasted_iota(jnp.int32, (rows, T), 1)
    for rel in range(nrel):
        dist = min(rel * T, back_max) + qry - key
        val = jnp.full((rows, T), tab_ref[0, h] * LOG2E, F32)
        for b in range(1, NUM_BUCKETS):
            val = jnp.where(dist >= starts[b], tab_ref[b, h] * LOG2E, val)
        allowed = dist >= 0
        if window is not None:
            allowed = allowed & (dist < window)
        o_ref[0, rel, 0] = jnp.where(allowed, val, NEG_INF)


def _bias_tiles(tab, T, nrel, window, R, rows=None):
    H = tab.shape[1]
    back_max = nrel * T if rows is None else rows - T
    rows = T if rows is None else rows
    return pl.pallas_call(
        functools.partial(_bias_kernel, T=T, rows=rows, nrel=nrel, window=window, back_max=back_max,
                          starts=_bucket_starts()),
        grid=(H,),
        in_specs=[pl.BlockSpec(memory_space=pltpu.SMEM)],
        out_specs=pl.BlockSpec((1, nrel, 1, rows, T), lambda h: (h // R, 0, h % R, 0, 0)),
        out_shape=jax.ShapeDtypeStruct((H // R, nrel, R, rows, T), F32),
        compiler_params=_params(1),
        name="bias_tiles",
    )(tab)


def _merge_kernel(x_ref, nrm_ref, oa_ref, ob_ref, oc_ref, os_ref, ow_ref, gc_ref, ex_ref, wmg_ref, wb_ref, wo_ref,
                  o_ref):
    x = x_ref[...]
    hb = _rms(x, nrm_ref[...]).astype(BF16)
    gc = gc_ref[...]
    g1 = gc.astype(BF16)
    r1 = gc - g1.astype(F32)
    g2 = r1.astype(BF16)
    g3 = (r1 - g2.astype(F32)).astype(BF16)
    ex = ex_ref[...]
    gx = (jnp.dot(g1, ex, preferred_element_type=F32) + jnp.dot(g2, ex, preferred_element_type=F32)
          + jnp.dot(g3, ex, preferred_element_type=F32))
    oc = (gx[:, 0:MIX_WIDTH] * oc_ref[...] + gx[:, MIX_WIDTH:2 * MIX_WIDTH] * os_ref[...]
          + gx[:, 2 * MIX_WIDTH:3 * MIX_WIDTH] * ow_ref[...])
    z = None
    for n, br in enumerate((oa_ref[...], ob_ref[...], oc)):
        y = jnp.dot(br.astype(BF16), wb_ref[n], preferred_element_type=F32)
        gate = jax.nn.sigmoid(jnp.dot(hb, wmg_ref[:, n * D_MODEL:(n + 1) * D_MODEL], preferred_element_type=F32))
        t = gate * y
        z = t if z is None else z + t
    o_ref[...] = x + jnp.dot(z.astype(BF16), wo_ref[...], preferred_element_type=F32)


def _merge(x2, nrm, oa, ob, oc, osel, ow, gc, expand_g, wmg, wb, wo, tm=512):
    Tn = x2.shape[0]
    row = lambda d: pl.BlockSpec((tm, d), lambda i: (i, 0))
    const = lambda shape: pl.BlockSpec(shape, lambda i: (0,) * len(shape), pipeline_mode=pl.Buffered(1))
    return pl.pallas_call(
        _merge_kernel,
        grid=(Tn // tm,),
        in_specs=[row(D_MODEL), const((1, D_MODEL)),
                  row(MIX_WIDTH), row(MIX_WIDTH), row(MIX_WIDTH), row(MIX_WIDTH), row(MIX_WIDTH),
                  row(CG_PAD),
                  const((CG_PAD, 3 * MIX_WIDTH)),
                  const((D_MODEL, 3 * D_MODEL)),
                  const((3, MIX_WIDTH, D_MODEL)),
                  const((D_MODEL, D_MODEL))],
        out_specs=row(D_MODEL),
        out_shape=jax.ShapeDtypeStruct((Tn, D_MODEL), F32),
        compiler_params=_params(1),
        name="merge",
    )(x2, nrm, oa, ob, oc, osel, ow, gc, expand_g, wmg, wb, wo)


def _mlp_kernel(x_ref, nrm_ref, wu_ref, wd_ref, o_ref, h_ref, acc_ref):
    f = pl.program_id(1)

    @pl.when(f == 0)
    def _():
        h_ref[...] = _rms(x_ref[...], nrm_ref[...]).astype(BF16)
        acc_ref[...] = jnp.zeros(acc_ref.shape, F32)

    u = jnp.dot(h_ref[...], wu_ref[...], preferred_element_type=F32)
    u = jnp.square(jnp.maximum(u, 0.0)).astype(BF16)
    acc_ref[...] += jnp.dot(u, wd_ref[...], preferred_element_type=F32)

    @pl.when(f == pl.num_programs(1) - 1)
    def _():
        o_ref[...] = x_ref[...] + acc_ref[...]


def _mlp(x2, nrm, wu, wd, tm=1024, tf=1024):
    Tn = x2.shape[0]
    return pl.pallas_call(
        _mlp_kernel,
        grid=(Tn // tm, D_FF // tf),
        in_specs=[pl.BlockSpec((tm, D_MODEL), lambda i, f: (i, 0)),
                  pl.BlockSpec((1, D_MODEL), lambda i, f: (0, 0)),
                  pl.BlockSpec((D_MODEL, tf), lambda i, f: (0, f)),
                  pl.BlockSpec((tf, D_MODEL), lambda i, f: (f, 0))],
        out_specs=pl.BlockSpec((tm, D_MODEL), lambda i, f: (i, 0)),
        out_shape=jax.ShapeDtypeStruct((Tn, D_MODEL), F32),
        scratch_shapes=[pltpu.VMEM((tm, D_MODEL), BF16), pltpu.VMEM((tm, D_MODEL), F32)],
        compiler_params=_params(2),
        name="mlp",
    )(x2, nrm, wu, wd)


def _overlap(n_cmp):
    c_start = np.arange(NC_PAD) * CMP_STRIDE
    j_start = np.arange(N_SEL_BLK) * SEL_BLOCK
    ov = ((c_start[None, :] < j_start[:, None] + SEL_BLOCK) & (c_start[None, :] + CMP_BLOCK > j_start[:, None])
          & (np.arange(NC_PAD)[None, :] < n_cmp))
    return jnp.asarray(ov.astype(np.float32), BF16)


def _gate_expand():
    e = np.zeros((CG_PAD, 3 * MIX_WIDTH), np.float32)
    for h in range(C_HEADS):
        for j in range(3):
            e[h * 3 + j, j * MIX_WIDTH + h * HEAD_DIM:j * MIX_WIDTH + (h + 1) * HEAD_DIM] = 1.0
    return jnp.asarray(e, BF16)


def kernel(x, w_in, qk_gain, diff_lambda, diff_subln, sinks, cmp_pos, cmp_w1, cmp_w2,
           w_branch, w_out, norm_mix, norm_mlp, w_up, w_down, rel_bias):
    B, S, _ = x.shape
    depth = w_in.shape[0]
    n_cmp = (S - CMP_BLOCK) // CMP_STRIDE + 1
    assert S % T_ROW == 0 and S // CMP_STRIDE == NC_PAD and S // SEL_BLOCK == N_SEL_BLK
    half = CMP_BLOCK // 2 * HEAD_DIM

    w_heads = w_in[:, :, :C_CG].astype(BF16)
    w_cg = jnp.pad(w_in[:, :, C_CG:C_CG + N_CG], ((0, 0), (0, 0), (0, CG_PAD - N_CG))).astype(BF16)
    w_mg = w_in[:, :, C_CG + N_CG:].astype(BF16)
    gains2 = jnp.concatenate([qk_gain, qk_gain], axis=-1)
    w1 = cmp_w1.astype(BF16).reshape(depth, 2, 2, half, CMP_HIDDEN)
    w2p = jnp.pad(cmp_w2, ((0, 0), (0, 0), (0, 0), (0, LANES - HEAD_DIM))).astype(BF16)
    pos = cmp_pos.reshape(depth, 2, 2, half)
    wb = w_branch.astype(BF16)
    wo = w_out.astype(BF16)
    wu = w_up.astype(BF16)
    wd = w_down.astype(BF16)

    bias_a = rel_bias[:, :A_HEADS]
    bias_b = rel_bias[:, A_HEADS:A_HEADS + B_HEADS]
    bias_c = rel_bias[:, A_HEADS + B_HEADS:]
    bt_a = _bias_tiles(bias_a, T_ROW, 3, None, 1)
    bt_b = _bias_tiles(bias_b, T_SWA, -(-B_WINDOW // T_SWA) + 1, B_WINDOW, GROUP, rows=B_WINDOW + T_SWA)
    bt_sel = _bias_tiles(bias_c, T_ROW, 3, None, GROUP)
    bt_win = _bias_tiles(bias_c, T_ROW, C_WINDOW // T_ROW + 1, C_WINDOW, GROUP, rows=C_WINDOW + T_ROW)
    ovl = _overlap(n_cmp)
    expand_g = _gate_expand()

    for layer in range(depth):
        lam_init = 0.8 - 0.6 * math.exp(-0.3 * layer)
        (qat, ka, vat, qbt, kb, vbt, qct, tkv, ksel, vselt, kwin, vwint, gc) = _inproj(
            x, norm_mix[layer][None], w_heads[layer], w_cg[layer], gains2[layer])
        oa = _flash_attn(qat, ka, vat, bt_a, dl=diff_lambda[layer],
                   subln=diff_subln[layer][None], lam_init=lam_init, name="attn_diff")
        ob = _window_attn(qbt, kb, vbt, bt_b, n_sub=4,
                          sink=sinks[layer].reshape(KV_GROUPS, GROUP, 1, 1), name="attn_swa")
        kc, vct = _compress(tkv.reshape(B, 2, KV_GROUPS, NC_PAD, half), pos[layer],
                            w1[layer], w2p[layer], qk_gain[layer][5:6])
        ocmp, sel = _cmp_attn(qct, kc, vct, ovl, n_cmp)
        osel = _flash_attn(qct, ksel, vselt, bt_sel, sel=sel, name="attn_sel")
        owin = _window_attn(qct, kwin, vwint, bt_win, n_sub=2, name="attn_win")
        f2 = lambda a: a.reshape(B * S, a.shape[-1])
        x2 = _merge(f2(x), norm_mix[layer][None], f2(oa), f2(ob), f2(ocmp), f2(osel), f2(owin), f2(gc),
                    expand_g, w_mg[layer], wb[layer], wo[layer])
        x2 = _mlp(x2, norm_mlp[layer][None], wu[layer], wd[layer])
        x = x2.reshape(B, S, D_MODEL)
    return x
```

```python
import functools
import math

import numpy as np
import jax
import jax.numpy as jnp
from jax import lax
from jax.experimental import pallas as pl
from jax.experimental.pallas import tpu as pltpu

F32 = jnp.float32
BF16 = jnp.bfloat16

D_MODEL = 1024
HEAD_DIM = 64
A_HEADS = 4
A_VDIM = 128
B_HEADS = 8
B_WINDOW = 128
C_HEADS = 8
KV_GROUPS = 2
GROUP = 4
CMP_BLOCK = 32
CMP_STRIDE = 16
CMP_HIDDEN = 256
SEL_BLOCK = 64
N_SELECT = 16
C_WINDOW = 512
MIX_WIDTH = 512
D_FF = 4096
NUM_BUCKETS = 32
MAX_DISTANCE = 128
NEG_INF = -1e30
FORCE_BONUS = 1e4
EPS = 1e-6
LOG2E = 1.4426950408889634
QK_SCALE = HEAD_DIM ** -0.5 * LOG2E

C_AQ, C_AK, C_AV, C_BQ, C_BKV, C_CQ, C_CKV, C_CG = 0, 512, 1024, 1536, 2048, 2304, 2816, 3584
N_CG = C_HEADS * 3
LANES = 128
CG_PAD = LANES
NC_PAD = 256
N_SEL_BLK = 64
ONES_ROWS = 16
MASK_ROWS = 16

SCORE_SLOTS = 4

T_IN = 512
T_ROW = 256
T_SWA = 256
TV_SWA = 128

VMEM_LIMIT = 48 * 1024 * 1024


def _rms(x, gain):
    ms = jnp.mean(x * x, axis=-1, keepdims=True)
    return (x * lax.rsqrt(ms + EPS)) * gain


def _params(n_axes):
    return pltpu.CompilerParams(dimension_semantics=("arbitrary",) * n_axes,
                                vmem_limit_bytes=VMEM_LIMIT)


def _inproj_kernel(x_ref, nrm_ref, w_ref, wcg_ref, gains_ref,
                   qa_ref, ka_ref, va_ref, qb_ref, kb_ref, vb_ref, qc_ref,
                   tkv_ref, ksel_ref, vsel_ref, kwin_ref, vwin_ref, gc_ref):
    tm = x_ref.shape[1]
    hb = _rms(x_ref[0], nrm_ref[...]).astype(BF16)
    lane = lax.broadcasted_iota(jnp.int32, (tm, LANES), 1)
    first = lane < HEAD_DIM

    def ones_pad(width):
        return jnp.where(lax.broadcasted_iota(jnp.int32, (ONES_ROWS, width), 0) == 0, 1.0, 0.0).astype(BF16)

    def mm(c0, n):
        return jnp.dot(hb, w_ref[:, c0:c0 + n], preferred_element_type=F32)

    def norm_pair(y, gain_idx, scale=None):
        sq = y * y
        ms0 = jnp.sum(jnp.where(first, sq, 0.0), axis=-1, keepdims=True) * (1.0 / HEAD_DIM)
        ms1 = jnp.sum(jnp.where(first, 0.0, sq), axis=-1, keepdims=True) * (1.0 / HEAD_DIM)
        inv = jnp.where(first, lax.rsqrt(ms0 + EPS), lax.rsqrt(ms1 + EPS))
        out = (y * inv) * gains_ref[gain_idx:gain_idx + 1, :]
        return out if scale is None else out * scale

    def pairs(c0):
        r = mm(c0, 256)
        return r[:, :LANES], r[:, LANES:]

    def put_qt(ref, i0, i1, y):
        yt = y.T.astype(BF16)
        n_tiles = ref.shape[3]
        tq = tm // n_tiles
        for t in range(n_tiles):
            ref[0, i0, i1, t] = yt[:HEAD_DIM, tq * t:tq * (t + 1)]
            ref[0, i0, i1 + 1, t] = yt[HEAD_DIM:, tq * t:tq * (t + 1)]

    def put_vt(ref, g0, y, n_tiles):
        yt = y.T.astype(BF16)
        tk = tm // n_tiles
        for g in range(2):
            for t in range(n_tiles):
                ref[0, g0 + g, t, 0:HEAD_DIM, :] = yt[HEAD_DIM * g:HEAD_DIM * (g + 1), tk * t:tk * (t + 1)]
                ref[0, g0 + g, t, HEAD_DIM:HEAD_DIM + ONES_ROWS, :] = ones_pad(tk)

    for ch in range(2):
        for half, y in enumerate(pairs(C_AQ + 256 * ch)):
            put_qt(qa_ref, ch * 2 + half, 0, norm_pair(y, 0, QK_SCALE))
        for half, y in enumerate(pairs(C_AK + 256 * ch)):
            y = norm_pair(y, 1).astype(BF16)
            ka_ref[0, ch * 2 + half, 0] = y[:, :HEAD_DIM]
            ka_ref[0, ch * 2 + half, 1] = y[:, HEAD_DIM:]
        for half, y in enumerate(pairs(C_AV + 256 * ch)):
            yt = y.T.astype(BF16)
            for t in range(tm // T_ROW):
                va_ref[0, ch * 2 + half, t, 0:A_VDIM, :] = yt[:, T_ROW * t:T_ROW * (t + 1)]
                va_ref[0, ch * 2 + half, t, A_VDIM:A_VDIM + ONES_ROWS, :] = ones_pad(T_ROW)
        for half, y in enumerate(pairs(C_BQ + 256 * ch)):
            put_qt(qb_ref, ch, 2 * half, norm_pair(y, 2, QK_SCALE))
        for half, y in enumerate(pairs(C_CQ + 256 * ch)):
            put_qt(qc_ref, ch, 2 * half, norm_pair(y, 4, QK_SCALE))

    def put_k(ref, y):
        y = y.astype(BF16)
        ref[0, 0] = y[:, :HEAD_DIM]
        ref[0, 1] = y[:, HEAD_DIM:]

    yk, yv = pairs(C_BKV)
    put_k(kb_ref, norm_pair(yk, 3))
    put_vt(vb_ref, 0, yv, tm // TV_SWA)
    y0, y1 = pairs(C_CKV)
    for kv, y in enumerate((y0, y1)):
        tkv_ref[0, kv, 0] = y[:, :HEAD_DIM]
        tkv_ref[0, kv, 1] = y[:, HEAD_DIM:]
    yk, yv = pairs(C_CKV + 256)
    yk = norm_pair(yk, 6)
    row = lax.broadcasted_iota(jnp.int32, (tm, LANES), 0)
    hot = jnp.where(lane == HEAD_DIM + (row // SEL_BLOCK) % (T_ROW // SEL_BLOCK), 1.0, 0.0)
    ksel_ref[0, 0] = jnp.where(first, yk, hot).astype(BF16)
    ksel_ref[0, 1] = jnp.where(first, pltpu.roll(yk, HEAD_DIM, 1), hot).astype(BF16)
    put_vt(vsel_ref, 0, yv, tm // T_ROW)
    yk, yv = pairs(C_CKV + 512)
    put_k(kwin_ref, norm_pair(yk, 7))
    put_vt(vwin_ref, 0, yv, tm // T_ROW)

    gc_ref[0] = jax.nn.sigmoid(jnp.dot(hb, wcg_ref[...], preferred_element_type=F32))


def _inproj(x, nrm, w, wcg, gains2):
    B, S, _ = x.shape
    tm = T_IN
    nt = S // tm
    sd = jax.ShapeDtypeStruct
    qt = lambda a, b, tq: sd((B, a, b, S // tq, HEAD_DIM, tq), BF16)
    kk = lambda n: sd((B, n, S, HEAD_DIM), BF16)
    vt = lambda n, dv, tk: sd((B, n, S // tk, dv + ONES_ROWS, tk), BF16)
    out_shape = (
        qt(A_HEADS, 2, T_ROW), sd((B, A_HEADS, 2, S, HEAD_DIM), BF16), vt(A_HEADS, A_VDIM, T_ROW),
        qt(KV_GROUPS, GROUP, T_SWA), kk(KV_GROUPS), vt(KV_GROUPS, HEAD_DIM, TV_SWA),
        qt(KV_GROUPS, GROUP, T_ROW),
        sd((B, 2, KV_GROUPS, S, HEAD_DIM), F32),
        sd((B, KV_GROUPS, S, LANES), BF16), vt(KV_GROUPS, HEAD_DIM, T_ROW),
        kk(KV_GROUPS), vt(KV_GROUPS, HEAD_DIM, T_ROW),
        sd((B, S, CG_PAD), F32),
    )
    s_qt = lambda a, b, tq: pl.BlockSpec((1, a, b, tm // tq, HEAD_DIM, tq), lambda b_, i: (b_, 0, 0, i, 0, 0))
    s_k = lambda n: pl.BlockSpec((1, n, tm, HEAD_DIM), lambda b_, i: (b_, 0, i, 0))
    s_k5 = lambda a, c: pl.BlockSpec((1, a, c, tm, HEAD_DIM), lambda b_, i: (b_, 0, 0, i, 0))
    s_vt = lambda n, dv, tk: pl.BlockSpec((1, n, tm // tk, dv + ONES_ROWS, tk), lambda b_, i: (b_, 0, i, 0, 0))
    row = lambda d: pl.BlockSpec((1, tm, d), lambda b_, i: (b_, i, 0))
    out_specs = (
        s_qt(A_HEADS, 2, T_ROW), s_k5(A_HEADS, 2), s_vt(A_HEADS, A_VDIM, T_ROW),
        s_qt(KV_GROUPS, GROUP, T_SWA), s_k(KV_GROUPS), s_vt(KV_GROUPS, HEAD_DIM, TV_SWA),
        s_qt(KV_GROUPS, GROUP, T_ROW), s_k5(2, KV_GROUPS),
        pl.BlockSpec((1, KV_GROUPS, tm, LANES), lambda b_, i: (b_, 0, i, 0)), s_vt(KV_GROUPS, HEAD_DIM, T_ROW),
        s_k(KV_GROUPS), s_vt(KV_GROUPS, HEAD_DIM, T_ROW),
        row(CG_PAD),
    )
    return pl.pallas_call(
        _inproj_kernel,
        grid=(B, nt),
        in_specs=[
            row(D_MODEL),
            pl.BlockSpec((1, D_MODEL), lambda b_, i: (0, 0)),
            pl.BlockSpec((D_MODEL, C_CG), lambda b_, i: (0, 0), pipeline_mode=pl.Buffered(1)),
            pl.BlockSpec((D_MODEL, CG_PAD), lambda b_, i: (0, 0)),
            pl.BlockSpec((8, LANES), lambda b_, i: (0, 0)),
        ],
        out_specs=out_specs,
        out_shape=out_shape,
        compiler_params=_params(2),
        name="inproj",
    )(x, nrm, w, wcg, gains2)


def _compress_kernel(t_ref, pos_ref, w1_ref, w2_ref, gain_ref, o_ref, ot_ref):
    t = t_ref[0, 0, 0]
    lo = (t + pos_ref[0, 0:1, :]).astype(BF16)
    hi = (t + pos_ref[0, 1:2, :]).astype(BF16)
    v = jnp.dot(lo, w1_ref[0, 0], preferred_element_type=F32)
    u = jnp.dot(hi, w1_ref[0, 1], preferred_element_type=F32)
    pre = v + pltpu.roll(u, NC_PAD - 1, 0)
    hcur = jax.nn.gelu(pre).astype(BF16)
    out = jnp.dot(hcur, w2_ref[0], preferred_element_type=F32)
    o64 = out[:, :HEAD_DIM]
    is_key = pl.program_id(1) == 0
    o_ref[0, 0, 0] = jnp.where(is_key, _rms(o64, gain_ref[...]), o64).astype(BF16)
    ot_ref[0, 0, 0] = out.T[:HEAD_DIM].astype(BF16)


def _compress(tkv16, pos, w1, w2p, gain):
    B = tkv16.shape[0]
    half = CMP_BLOCK // 2 * HEAD_DIM
    return pl.pallas_call(
        _compress_kernel,
        grid=(B, 2, KV_GROUPS),
        in_specs=[
            pl.BlockSpec((1, 1, 1, NC_PAD, half), lambda b, kv, g: (b, kv, g, 0, 0)),
            pl.BlockSpec((1, 2, half), lambda b, kv, g: (kv, 0, 0)),
            pl.BlockSpec((1, 2, half, CMP_HIDDEN), lambda b, kv, g: (kv, 0, 0, 0)),
            pl.BlockSpec((1, CMP_HIDDEN, LANES), lambda b, kv, g: (kv, 0, 0)),
            pl.BlockSpec((1, HEAD_DIM), lambda b, kv, g: (0, 0)),
        ],
        out_specs=(pl.BlockSpec((1, 1, 1, NC_PAD, HEAD_DIM), lambda b, kv, g: (b, kv, g, 0, 0)),
                   pl.BlockSpec((1, 1, 1, HEAD_DIM, NC_PAD), lambda b, kv, g: (b, kv, g, 0, 0))),
        out_shape=(jax.ShapeDtypeStruct((B, 2, KV_GROUPS, NC_PAD, HEAD_DIM), BF16),
                   jax.ShapeDtypeStruct((B, 2, KV_GROUPS, HEAD_DIM, NC_PAD), BF16)),
        compiler_params=_params(3),
        name="compress",
    )(tkv16, pos, w1, w2p, gain)


def _cmp_attn_kernel(qt_ref, kc_ref, vct_ref, ovl_ref, o_ref, sel_ref, score_ref, rank_ref, *, T, n_cmp):
    R = GROUP
    I = pl.program_id(2)
    j_idx = lax.broadcasted_iota(jnp.int32, (N_SEL_BLK, T), 0)
    tq = I * T + lax.broadcasted_iota(jnp.int32, (N_SEL_BLK, T), 1)
    cur = tq // SEL_BLOCK
    valid = j_idx * SEL_BLOCK <= tq
    forced = (j_idx == 0) | (j_idx == cur) | (j_idx == cur - 1)

    def attend(rows):
        kc = kc_ref[0, 0, 0, 0:rows, :]
        vct = vct_ref[0, 0, 0, :, 0:rows]
        c_idx = lax.broadcasted_iota(jnp.int32, (rows, T), 0)
        t_pos = I * T + lax.broadcasted_iota(jnp.int32, (rows, T), 1)
        cmask = (c_idx * CMP_STRIDE + (CMP_BLOCK - 1) <= t_pos) & (c_idx < n_cmp)
        scores = [jnp.dot(kc, qt_ref[0, 0, r, 0], preferred_element_type=F32) for r in range(R)]
        probs = []
        for r in range(R):
            s = jnp.where(cmask, scores[r], NEG_INF)
            m = jnp.max(s, axis=0, keepdims=True)
            e = jnp.where(cmask, jnp.exp2(s - m), 0.0)
            l = jnp.sum(e, axis=0, keepdims=True)
            probs.append(e * (1.0 / jnp.where(l > 0.0, l, 1.0)))
        outs = [jnp.dot(vct, p.astype(BF16), preferred_element_type=F32) for p in probs]
        o_ref[0] = jnp.concatenate(outs, axis=0).T
        psum = (probs[0] + probs[1]) + (probs[2] + probs[3])
        p_hi = psum.astype(BF16)
        p_lo = (psum - p_hi.astype(F32)).astype(BF16)
        ovl = ovl_ref[:, 0:rows]
        imp = jnp.dot(ovl, p_hi, preferred_element_type=F32) + jnp.dot(ovl, p_lo, preferred_element_type=F32)
        score_ref[...] = jnp.where(valid, imp + jnp.where(forced, FORCE_BONUS, 0.0), NEG_INF)

    chunk = NC_PAD // 4
    vis_per_tile = T // CMP_STRIDE
    for v in range(4):
        lo = -(-(v * chunk) // vis_per_tile)
        hi = -(-((v + 1) * chunk) // vis_per_tile)

        @pl.when((I >= lo) & (I < hi))
        def _(v=v):
            attend((v + 1) * chunk)

    rank_ref[...] = jnp.zeros(rank_ref.shape, F32)
    n_live = (I + 1) * (T // SEL_BLOCK)
    sub = 8
    j_loc = lax.broadcasted_iota(jnp.int32, (sub, T), 0)
    for c in range(N_SEL_BLK // sub):
        @pl.when(c * sub < n_live)
        def _(c=c):
            n_grp = N_SEL_BLK // sub
            grps = [score_ref[g * sub:(g + 1) * sub, :] for g in range(n_grp)]
            cnts = [rank_ref[g * sub:(g + 1) * sub, :] for g in range(n_grp)]
            for jp in range(c * sub, (c + 1) * sub):
                row = jnp.broadcast_to(score_ref[jp:jp + 1, :], (sub, T))
                for g in range(n_grp):
                    if g * sub > jp:
                        beats = row >= grps[g]
                    elif (g + 1) * sub - 1 < jp:
                        beats = row > grps[g]
                    else:
                        beats = (row > grps[g]) | ((row == grps[g]) & (j_loc + g * sub > jp))
                    cnts[g] = cnts[g] + jnp.where(beats, 1.0, 0.0)
            for g in range(n_grp):
                rank_ref[g * sub:(g + 1) * sub, :] = cnts[g]
    blk = T // SEL_BLOCK
    for j in range(N_SEL_BLK // blk):
        keep = rank_ref[j * blk:(j + 1) * blk, :] < N_SELECT
        slab = jnp.concatenate([jnp.where(keep, 0.0, NEG_INF), jnp.zeros((MASK_ROWS - blk, T), F32)], axis=0)
        sel_ref[0, 0, 0, j] = slab.astype(BF16)


def _cmp_attn(qct, kc, vct, ovl, n_cmp):
    B, G, R, NQ, _, T = qct.shape
    S = NQ * T
    return pl.pallas_call(
        functools.partial(_cmp_attn_kernel, T=T, n_cmp=n_cmp),
        grid=(B, G, S // T),
        in_specs=[
            pl.BlockSpec((1, 1, R, 1, HEAD_DIM, T), lambda b, g, i: (b, g, 0, i, 0, 0)),
            pl.BlockSpec((1, 1, 1, NC_PAD, HEAD_DIM), lambda b, g, i: (b, 0, g, 0, 0)),
            pl.BlockSpec((1, 1, 1, HEAD_DIM, NC_PAD), lambda b, g, i: (b, 1, g, 0, 0)),
            pl.BlockSpec((N_SEL_BLK, NC_PAD), lambda b, g, i: (0, 0)),
        ],
        out_specs=(
            pl.BlockSpec((1, T, R * HEAD_DIM), lambda b, g, i: (b, i, g)),
            pl.BlockSpec((1, 1, 1, NQ, MASK_ROWS, T), lambda b, g, i: (b, g, i, 0, 0, 0)),
        ),
        out_shape=(
            jax.ShapeDtypeStruct((B, S, G * R * HEAD_DIM), F32),
            jax.ShapeDtypeStruct((B, G, NQ, NQ, MASK_ROWS, T), BF16),
        ),
        scratch_shapes=[pltpu.VMEM((N_SEL_BLK, T), F32), pltpu.VMEM((N_SEL_BLK, T), F32)],
        compiler_params=_params(3),
        name="cmp_attn",
    )(qct, kc, vct, ovl)


def _flash_kernel(*refs, R, T, NQ, Dv, nrel, k_per_r, shared_bias, diff, has_sel, lam_init):
    it = iter(refs)
    qt_ref, k_ref, vt_ref, bt_ref = (next(it) for _ in range(4))
    sel_ref = next(it) if has_sel else None
    dl_ref = next(it) if diff else None
    sub_ref = next(it) if diff else None
    o_ref, m_ref, acc_ref, s_ref, p_ref, al_ref, tmax_ref = (next(it) for _ in range(7))

    m_ref[...] = jnp.full(m_ref.shape, NEG_INF, F32)
    acc_ref[...] = jnp.zeros(acc_ref.shape, F32)
    bias_r = (lambda r: 0) if shared_bias else (lambda r: r)
    if has_sel:
        pad_rows = jnp.zeros((k_ref.shape[-1] - HEAD_DIM - MASK_ROWS, T), BF16)
    n_steps = NQ * (NQ + 1) // 2
    N_S = SCORE_SLOTS
    assert N_S % 2 == 0 and n_steps % N_S == 0

    def score_stage(I, J, slot):
        off = pl.multiple_of(J * T, T)
        rel = jnp.minimum(I - J, nrel - 1)
        for r in range(R):
            kt = k_ref[0, 0, r, pl.ds(off, T), :] if k_per_r else k_ref[0, 0, pl.ds(off, T), :]
            w = qt_ref[0, 0, r, I]
            if has_sel:
                w = jnp.concatenate([w, sel_ref[0, 0, I, J], pad_rows], axis=0)
            s = jnp.dot(kt, w, preferred_element_type=F32) + bt_ref[0, rel, bias_r(r)]
            s_ref[slot, r] = s
            tmax_ref[slot, r] = jnp.max(s, axis=0, keepdims=True)

    def softmax_stage(I, s_slot, slot):
        for r in range(R):
            m_prev = m_ref[I, r]
            m_new = jnp.maximum(m_prev, tmax_ref[s_slot, r])
            al_ref[slot, r] = jnp.exp2(m_prev - m_new)
            p_ref[slot, r] = jnp.exp2(s_ref[s_slot, r] - m_new).astype(BF16)
            m_ref[I, r] = m_new

    def value_stage(J, slot):
        vt = vt_ref[0, 0, J]
        return [jnp.dot(vt, p_ref[slot, r], preferred_element_type=F32) for r in range(R)]

    def accumulate(I, slot, pvs):
        for r in range(R):
            acc_ref[I, r] = al_ref[slot, r] * acc_ref[I, r] + pvs[r]

    def advance(I, J):
        last = J == I
        return jnp.where(last, I + 1, I), jnp.where(last, 0, J + 1)

    def step(cur, prev, t):
        other = 1 - t % 2
        nxt = advance(*cur)
        pvs = value_stage(prev[1], other)
        score_stage(jnp.minimum(nxt[0], NQ - 1), nxt[1], (t + 1) % N_S)
        softmax_stage(cur[0], t % N_S, t % 2)
        accumulate(prev[0], other, pvs)
        return nxt

    def trip(u, carry):
        cur, prev = carry[:2], carry[2:]
        for t in range(N_S):
            cur, prev = step(cur, prev, t), cur
        return (*cur, *prev)

    zero = jnp.int32(0)
    score_stage(zero, zero, 0)
    p_ref[1] = jnp.zeros(p_ref.shape[1:], BF16)
    al_ref[1] = jnp.ones(al_ref.shape[1:], F32)
    lax.fori_loop(0, n_steps // N_S, trip, (zero, zero, zero, zero))
    accumulate(NQ - 1, 1, value_stage(NQ - 1, 1))

    if diff:
        dl = dl_ref[...]
        lam = (jnp.exp(jnp.sum(dl[0:1] * dl[1:2], keepdims=True))
               - jnp.exp(jnp.sum(dl[2:3] * dl[3:4], keepdims=True)) + lam_init)

    def finish(I, carry):
        outs = []
        for r in range(R):
            acc = acc_ref[I, r]
            outs.append(acc[:Dv] / acc[Dv:Dv + 1])
        rows = pl.ds(pl.multiple_of(I * T, T), T)
        if diff:
            o = (outs[0] - lam * outs[1]).T
            o_ref[0, rows, :] = _rms(o, sub_ref[...]) * (1.0 - lam_init)
        else:
            o_ref[0, rows, :] = jnp.concatenate(outs, axis=0).T
        return carry

    lax.fori_loop(0, NQ, finish, 0)


def _flash_attn(qt, k, vt, bias_tab, *, sel=None, dl=None, subln=None, lam_init=0.0, name="attn"):
    B, G, R, NQ, _, T = qt.shape
    S = NQ * T
    Dv = vt.shape[-2] - ONES_ROWS
    nrel, bias_heads = bias_tab.shape[1], bias_tab.shape[2]
    k_per_r = k.ndim == 5
    has_sel = sel is not None
    diff = dl is not None
    ins = [qt, k, vt, bias_tab]
    in_specs = [
        pl.BlockSpec((1, 1, R, NQ, HEAD_DIM, T), lambda b, g: (b, g, 0, 0, 0, 0)),
        (pl.BlockSpec((1, 1, R, S, HEAD_DIM), lambda b, g: (b, g, 0, 0, 0)) if k_per_r
         else pl.BlockSpec((1, 1, S, k.shape[-1]), lambda b, g: (b, g, 0, 0))),
        pl.BlockSpec((1, 1, NQ, Dv + ONES_ROWS, T), lambda b, g: (b, g, 0, 0, 0)),
        pl.BlockSpec((1, nrel, bias_heads, T, T), lambda b, g: (g, 0, 0, 0, 0)),
    ]
    if has_sel:
        ins.append(sel)
        in_specs.append(pl.BlockSpec((1, 1, NQ, NQ, MASK_ROWS, T), lambda b, g: (b, g, 0, 0, 0, 0)))
    if diff:
        ins += [dl, subln]
        in_specs += [pl.BlockSpec((4, HEAD_DIM), lambda b, g: (0, 0)),
                     pl.BlockSpec((1, A_VDIM), lambda b, g: (0, 0))]
    out_w = Dv if diff else R * Dv
    kern = functools.partial(_flash_kernel, R=R, T=T, NQ=NQ, Dv=Dv, nrel=nrel, k_per_r=k_per_r,
                             shared_bias=bias_heads == 1, diff=diff, has_sel=has_sel, lam_init=lam_init)
    return pl.pallas_call(
        kern,
        grid=(B, G),
        in_specs=in_specs,
        out_specs=pl.BlockSpec((1, S, out_w), lambda b, g: (b, 0, g)),
        out_shape=jax.ShapeDtypeStruct((B, S, G * out_w), F32),
        scratch_shapes=[pltpu.VMEM((NQ, R, 1, T), F32), pltpu.VMEM((NQ, R, Dv + ONES_ROWS, T), F32),
                        pltpu.VMEM((SCORE_SLOTS, R, T, T), F32), pltpu.VMEM((2, R, T, T), BF16),
                        pltpu.VMEM((2, R, 1, T), F32), pltpu.VMEM((SCORE_SLOTS, R, 1, T), F32)],
        compiler_params=_params(2),
        name=name,
    )(*ins)


def _window_kernel(*refs, R, T, NK, TV, Dv, n_sub, has_sink):
    it = iter(refs)
    qt_ref, k_ref, vt_ref, bt_ref = (next(it) for _ in range(4))
    sink_ref = next(it) if has_sink else None
    o_ref = next(it)
    heads = [(t, r) for t in range(n_sub) for r in range(R)]
    tile = [pl.program_id(2) * n_sub + t for t in range(n_sub)]
    entry = [jnp.minimum(I, -(-(NK - T) // T)) for I in tile]
    first_key = [pl.multiple_of(I * T - jnp.minimum(e * T, NK - T), TV) for I, e in zip(tile, entry)]
    keys = [k_ref[0, 0, pl.ds(fk, NK), :] for fk in first_key]
    scores = {(t, r): jnp.dot(keys[t], qt_ref[0, 0, r, t], preferred_element_type=F32) for t, r in heads}
    probs, maxes = {}, {}
    for t, r in heads:
        s = scores[t, r] + bt_ref[0, entry[t], r]
        maxes[t, r] = jnp.max(s, axis=0, keepdims=True)
        probs[t, r] = jnp.exp2(s - maxes[t, r]).astype(BF16)
    accs = {}
    for t, r in heads:
        acc = None
        for c in range(NK // TV):
            part = jnp.dot(vt_ref[0, 0, first_key[t] // TV + c], probs[t, r][c * TV:(c + 1) * TV],
                           preferred_element_type=F32)
            acc = part if acc is None else acc + part
        accs[t, r] = acc
    for t in range(n_sub):
        outs = []
        for r in range(R):
            num, l = accs[t, r][:Dv], accs[t, r][Dv:Dv + 1]
            if has_sink:
                sk = sink_ref[0, r] * LOG2E
                m_f = jnp.maximum(maxes[t, r], sk)
                w = jnp.exp2(maxes[t, r] - m_f)
                outs.append(num * w / (l * w + jnp.exp2(sk - m_f)))
            else:
                outs.append(num / l)
        o_ref[0, t * T:(t + 1) * T, :] = jnp.concatenate(outs, axis=0).T


def _window_attn(qt, k, vt, bias_tab, *, n_sub=2, sink=None, name="attn_window"):
    B, G, R, NQ, _, T = qt.shape
    S = NQ * T
    TV = vt.shape[-1]
    Dv = vt.shape[-2] - ONES_ROWS
    entries, NK = bias_tab.shape[1], bias_tab.shape[3]
    has_sink = sink is not None
    ins = [qt, k, vt, bias_tab]
    in_specs = [
        pl.BlockSpec((1, 1, R, n_sub, HEAD_DIM, T), lambda b, g, i: (b, g, 0, i, 0, 0)),
        pl.BlockSpec((1, 1, S, HEAD_DIM), lambda b, g, i: (b, g, 0, 0)),
        pl.BlockSpec((1, 1, S // TV, Dv + ONES_ROWS, TV), lambda b, g, i: (b, g, 0, 0, 0)),
        pl.BlockSpec((1, entries, R, NK, T), lambda b, g, i: (g, 0, 0, 0, 0)),
    ]
    if has_sink:
        ins.append(sink)
        in_specs.append(pl.BlockSpec((1, R, 1, 1), lambda b, g, i: (g, 0, 0, 0)))
    kern = functools.partial(_window_kernel, R=R, T=T, NK=NK, TV=TV, Dv=Dv, n_sub=n_sub, has_sink=has_sink)
    return pl.pallas_call(
        kern,
        grid=(B, G, NQ // n_sub),
        in_specs=in_specs,
        out_specs=pl.BlockSpec((1, n_sub * T, R * Dv), lambda b, g, i: (b, i, g)),
        out_shape=jax.ShapeDtypeStruct((B, S, G * R * Dv), F32),
        compiler_params=_params(3),
        name=name,
    )(*ins)


def _bucket_np(dist):
    n = np.maximum(dist, 0)
    max_exact = NUM_BUCKETS // 2
    nf = np.maximum(n, 1).astype(np.float32)
    large = max_exact + (np.log(nf / max_exact) / math.log(MAX_DISTANCE / max_exact)
                         * (NUM_BUCKETS - max_exact)).astype(np.int32)
    large = np.minimum(large, NUM_BUCKETS - 1)
    return np.where(n < max_exact, n, large)


def _bucket_starts():
    b = _bucket_np(np.arange(4 * MAX_DISTANCE))
    assert (np.diff(b) >= 0).all() and b[-1] == NUM_BUCKETS - 1
    return [int(np.argmax(b >= i)) for i in range(NUM_BUCKETS)]


def _bias_kernel(tab_ref, o_ref, *, T, rows, nrel, window, back_max, starts):
    h = pl.program_id(0)
    key = lax.broadcasted_iota(jnp.int32, (rows, T), 0)
    qry = lax.broadcasted_iota(jnp.int32, (rows, T), 1)
    for rel in range(nrel):
        dist = min(rel * T, back_max) + qry - key
        val = jnp.full((rows, T), tab_ref[0, h] * LOG2E, F32)
        for b in range(1, NUM_BUCKETS):
            val = jnp.where(dist >= starts[b], tab_ref[b, h] * LOG2E, val)
        allowed = dist >= 0
        if window is not None:
            allowed = allowed & (dist < window)
        o_ref[0, rel, 0] = jnp.where(allowed, val, NEG_INF)


def _bias_tiles(tab, T, nrel, window, R, rows=None):
    H = tab.shape[1]
    back_max = nrel * T if rows is None else rows - T
    rows = T if rows is None else rows
    return pl.pallas_call(
        functools.partial(_bias_kernel, T=T, rows=rows, nrel=nrel, window=window, back_max=back_max,
                          starts=_bucket_starts()),
        grid=(H,),
        in_specs=[pl.BlockSpec(memory_space=pltpu.SMEM)],
        out_specs=pl.BlockSpec((1, nrel, 1, rows, T), lambda h: (h // R, 0, h % R, 0, 0)),
        out_shape=jax.ShapeDtypeStruct((H // R, nrel, R, rows, T), F32),
        compiler_params=_params(1),
        name="bias_tiles",
    )(tab)


def _merge_kernel(x_ref, nrm_ref, oa_ref, ob_ref, oc_ref, os_ref, ow_ref, gc_ref, ex_ref, wmg_ref, wb_ref, wo_ref,
                  o_ref):
    x = x_ref[...]
    hb = _rms(x, nrm_ref[...]).astype(BF16)
    gc = gc_ref[...]
    g1 = gc.astype(BF16)
    r1 = gc - g1.astype(F32)
    g2 = r1.astype(BF16)
    g3 = (r1 - g2.astype(F32)).astype(BF16)
    ex = ex_ref[...]
    gx = (jnp.dot(g1, ex, preferred_element_type=F32) + jnp.dot(g2, ex, preferred_element_type=F32)
          + jnp.dot(g3, ex, preferred_element_type=F32))
    oc = (gx[:, 0:MIX_WIDTH] * oc_ref[...] + gx[:, MIX_WIDTH:2 * MIX_WIDTH] * os_ref[...]
          + gx[:, 2 * MIX_WIDTH:3 * MIX_WIDTH] * ow_ref[...])
    z = None
    for n, br in enumerate((oa_ref[...], ob_ref[...], oc)):
        y = jnp.dot(br.astype(BF16), wb_ref[n], preferred_element_type=F32)
        gate = jax.nn.sigmoid(jnp.dot(hb, wmg_ref[:, n * D_MODEL:(n + 1) * D_MODEL], preferred_element_type=F32))
        t = gate * y
        z = t if z is None else z + t
    o_ref[...] = x + jnp.dot(z.astype(BF16), wo_ref[...], preferred_element_type=F32)


def _merge(x2, nrm, oa, ob, oc, osel, ow, gc, expand_g, wmg, wb, wo, tm=512):
    Tn = x2.shape[0]
    row = lambda d: pl.BlockSpec((tm, d), lambda i: (i, 0))
    const = lambda shape: pl.BlockSpec(shape, lambda i: (0,) * len(shape), pipeline_mode=pl.Buffered(1))
    return pl.pallas_call(
        _merge_kernel,
        grid=(Tn // tm,),
        in_specs=[row(D_MODEL), const((1, D_MODEL)),
                  row(MIX_WIDTH), row(MIX_WIDTH), row(MIX_WIDTH), row(MIX_WIDTH), row(MIX_WIDTH),
                  row(CG_PAD),
                  const((CG_PAD, 3 * MIX_WIDTH)),
                  const((D_MODEL, 3 * D_MODEL)),
                  const((3, MIX_WIDTH, D_MODEL)),
                  const((D_MODEL, D_MODEL))],
        out_specs=row(D_MODEL),
        out_shape=jax.ShapeDtypeStruct((Tn, D_MODEL), F32),
        compiler_params=_params(1),
        name="merge",
    )(x2, nrm, oa, ob, oc, osel, ow, gc, expand_g, wmg, wb, wo)


def _mlp_kernel(x_ref, nrm_ref, wu_ref, wd_ref, o_ref, h_ref, acc_ref):
    f = pl.program_id(1)

    @pl.when(f == 0)
    def _():
        h_ref[...] = _rms(x_ref[...], nrm_ref[...]).astype(BF16)
        acc_ref[...] = jnp.zeros(acc_ref.shape, F32)

    u = jnp.dot(h_ref[...], wu_ref[...], preferred_element_type=F32)
    u = jnp.square(jnp.maximum(u, 0.0)).astype(BF16)
    acc_ref[...] += jnp.dot(u, wd_ref[...], preferred_element_type=F32)

    @pl.when(f == pl.num_programs(1) - 1)
    def _():
        o_ref[...] = x_ref[...] + acc_ref[...]


def _mlp(x2, nrm, wu, wd, tm=1024, tf=1024):
    Tn = x2.shape[0]
    return pl.pallas_call(
        _mlp_kernel,
        grid=(Tn // tm, D_FF // tf),
        in_specs=[pl.BlockSpec((tm, D_MODEL), lambda i, f: (i, 0)),
                  pl.BlockSpec((1, D_MODEL), lambda i, f: (0, 0)),
                  pl.BlockSpec((D_MODEL, tf), lambda i, f: (0, f)),
                  pl.BlockSpec((tf, D_MODEL), lambda i, f: (f, 0))],
        out_specs=pl.BlockSpec((tm, D_MODEL), lambda i, f: (i, 0)),
        out_shape=jax.ShapeDtypeStruct((Tn, D_MODEL), F32),
        scratch_shapes=[pltpu.VMEM((tm, D_MODEL), BF16), pltpu.VMEM((tm, D_MODEL), F32)],
        compiler_params=_params(2),
        name="mlp",
    )(x2, nrm, wu, wd)


def _overlap(n_cmp):
    c_start = np.arange(NC_PAD) * CMP_STRIDE
    j_start = np.arange(N_SEL_BLK) * SEL_BLOCK
    ov = ((c_start[None, :] < j_start[:, None] + SEL_BLOCK) & (c_start[None, :] + CMP_BLOCK > j_start[:, None])
          & (np.arange(NC_PAD)[None, :] < n_cmp))
    return jnp.asarray(ov.astype(np.float32), BF16)


def _gate_expand():
    e = np.zeros((CG_PAD, 3 * MIX_WIDTH), np.float32)
    for h in range(C_HEADS):
        for j in range(3):
            e[h * 3 + j, j * MIX_WIDTH + h * HEAD_DIM:j * MIX_WIDTH + (h + 1) * HEAD_DIM] = 1.0
    return jnp.asarray(e, BF16)


def kernel(x, w_in, qk_gain, diff_lambda, diff_subln, sinks, cmp_pos, cmp_w1, cmp_w2,
           w_branch, w_out, norm_mix, norm_mlp, w_up, w_down, rel_bias):
    B, S, _ = x.shape
    depth = w_in.shape[0]
    n_cmp = (S - CMP_BLOCK) // CMP_STRIDE + 1
    assert S % T_ROW == 0 and S // CMP_STRIDE == NC_PAD and S // SEL_BLOCK == N_SEL_BLK
    half = CMP_BLOCK // 2 * HEAD_DIM

    w_heads = w_in[:, :, :C_CG].astype(BF16)
    w_cg = jnp.pad(w_in[:, :, C_CG:C_CG + N_CG], ((0, 0), (0, 0), (0, CG_PAD - N_CG))).astype(BF16)
    w_mg = w_in[:, :, C_CG + N_CG:].astype(BF16)
    gains2 = jnp.concatenate([qk_gain, qk_gain], axis=-1)
    w1 = cmp_w1.astype(BF16).reshape(depth, 2, 2, half, CMP_HIDDEN)
    w2p = jnp.pad(cmp_w2, ((0, 0), (0, 0), (0, 0), (0, LANES - HEAD_DIM))).astype(BF16)
    pos = cmp_pos.reshape(depth, 2, 2, half)
    wb = w_branch.astype(BF16)
    wo = w_out.astype(BF16)
    wu = w_up.astype(BF16)
    wd = w_down.astype(BF16)

    bias_a = rel_bias[:, :A_HEADS]
    bias_b = rel_bias[:, A_HEADS:A_HEADS + B_HEADS]
    bias_c = rel_bias[:, A_HEADS + B_HEADS:]
    bt_a = _bias_tiles(bias_a, T_ROW, 3, None, 1)
    bt_b = _bias_tiles(bias_b, T_SWA, -(-B_WINDOW // T_SWA) + 1, B_WINDOW, GROUP, rows=B_WINDOW + T_SWA)
    bt_sel = _bias_tiles(bias_c, T_ROW, 3, None, GROUP)
    bt_win = _bias_tiles(bias_c, T_ROW, C_WINDOW // T_ROW + 1, C_WINDOW, GROUP, rows=C_WINDOW + T_ROW)
    ovl = _overlap(n_cmp)
    expand_g = _gate_expand()

    for layer in range(depth):
        lam_init = 0.8 - 0.6 * math.exp(-0.3 * layer)
        (qat, ka, vat, qbt, kb, vbt, qct, tkv, ksel, vselt, kwin, vwint, gc) = _inproj(
            x, norm_mix[layer][None], w_heads[layer], w_cg[layer], gains2[layer])
        oa = _flash_attn(qat, ka, vat, bt_a, dl=diff_lambda[layer],
                   subln=diff_subln[layer][None], lam_init=lam_init, name="attn_diff")
        ob = _window_attn(qbt, kb, vbt, bt_b, n_sub=4,
                          sink=sinks[layer].reshape(KV_GROUPS, GROUP, 1, 1), name="attn_swa")
        kc, vct = _compress(tkv.reshape(B, 2, KV_GROUPS, NC_PAD, half), pos[layer],
                            w1[layer], w2p[layer], qk_gain[layer][5:6])
        ocmp, sel = _cmp_attn(qct, kc, vct, ovl, n_cmp)
        osel = _flash_attn(qct, ksel, vselt, bt_sel, sel=sel, name="attn_sel")
        owin = _window_attn(qct, kwin, vwint, bt_win, n_sub=4, name="attn_win")
        f2 = lambda a: a.reshape(B * S, a.shape[-1])
        x2 = _merge(f2(x), norm_mix[layer][None], f2(oa), f2(ob), f2(ocmp), f2(osel), f2(owin), f2(gc),
                    expand_g, w_mg[layer], wb[layer], wo[layer])
        x2 = _mlp(x2, norm_mlp[layer][None], wu[layer], wd[layer])
        x = x2.reshape(B, S, D_MODEL)
    return x
```

```python
import functools
import math

import numpy as np
import jax
import jax.numpy as jnp
from jax import lax
from jax.experimental import pallas as pl
from jax.experimental.pallas import tpu as pltpu

F32 = jnp.float32
BF16 = jnp.bfloat16

D_MODEL = 1024
HEAD_DIM = 64
A_HEADS = 4
A_VDIM = 128
B_HEADS = 8
B_WINDOW = 128
C_HEADS = 8
KV_GROUPS = 2
GROUP = 4
CMP_BLOCK = 32
CMP_STRIDE = 16
CMP_HIDDEN = 256
SEL_BLOCK = 64
N_SELECT = 16
C_WINDOW = 512
MIX_WIDTH = 512
D_FF = 4096
NUM_BUCKETS = 32
MAX_DISTANCE = 128
NEG_INF = -1e30
FORCE_BONUS = 1e4
EPS = 1e-6
LOG2E = 1.4426950408889634
QK_SCALE = HEAD_DIM ** -0.5 * LOG2E

C_AQ, C_AK, C_AV, C_BQ, C_BKV, C_CQ, C_CKV, C_CG = 0, 512, 1024, 1536, 2048, 2304, 2816, 3584
N_CG = C_HEADS * 3
LANES = 128
CG_PAD = LANES
NC_PAD = 256
N_SEL_BLK = 64
ONES_ROWS = 16
MASK_ROWS = 16

SCORE_SLOTS = 4

T_IN = 512
T_ROW = 256
T_SWA = 256
TV_SWA = 128

VMEM_LIMIT = 48 * 1024 * 1024


def _rms(x, gain):
    ms = jnp.mean(x * x, axis=-1, keepdims=True)
    return (x * lax.rsqrt(ms + EPS)) * gain


def _params(n_axes):
    return pltpu.CompilerParams(dimension_semantics=("arbitrary",) * n_axes,
                                vmem_limit_bytes=VMEM_LIMIT)


def _inproj_kernel(x_ref, nrm_ref, w_ref, wcg_ref, gains_ref,
                   qa_ref, ka_ref, va_ref, qb_ref, kb_ref, vb_ref, qc_ref,
                   tkv_ref, ksel_ref, vsel_ref, kwin_ref, vwin_ref, gc_ref):
    tm = x_ref.shape[1]
    hb = _rms(x_ref[0], nrm_ref[...]).astype(BF16)
    lane = lax.broadcasted_iota(jnp.int32, (tm, LANES), 1)
    first = lane < HEAD_DIM

    def ones_pad(width):
        return jnp.where(lax.broadcasted_iota(jnp.int32, (ONES_ROWS, width), 0) == 0, 1.0, 0.0).astype(BF16)

    def mm(c0, n):
        return jnp.dot(hb, w_ref[:, c0:c0 + n], preferred_element_type=F32)

    def norm_pair(y, gain_idx, scale=None):
        sq = y * y
        ms0 = jnp.sum(jnp.where(first, sq, 0.0), axis=-1, keepdims=True) * (1.0 / HEAD_DIM)
        ms1 = jnp.sum(jnp.where(first, 0.0, sq), axis=-1, keepdims=True) * (1.0 / HEAD_DIM)
        inv = jnp.where(first, lax.rsqrt(ms0 + EPS), lax.rsqrt(ms1 + EPS))
        out = (y * inv) * gains_ref[gain_idx:gain_idx + 1, :]
        return out if scale is None else out * scale

    def pairs(c0):
        r = mm(c0, 256)
        return r[:, :LANES], r[:, LANES:]

    def put_qt(ref, i0, i1, y):
        yt = y.T.astype(BF16)
        n_tiles = ref.shape[3]
        tq = tm // n_tiles
        for t in range(n_tiles):
            ref[0, i0, i1, t] = yt[:HEAD_DIM, tq * t:tq * (t + 1)]
            ref[0, i0, i1 + 1, t] = yt[HEAD_DIM:, tq * t:tq * (t + 1)]

    def put_vt(ref, g0, y, n_tiles):
        yt = y.T.astype(BF16)
        tk = tm // n_tiles
        for g in range(2):
            for t in range(n_tiles):
                ref[0, g0 + g, t, 0:HEAD_DIM, :] = yt[HEAD_DIM * g:HEAD_DIM * (g + 1), tk * t:tk * (t + 1)]
                ref[0, g0 + g, t, HEAD_DIM:HEAD_DIM + ONES_ROWS, :] = ones_pad(tk)

    for ch in range(2):
        for half, y in enumerate(pairs(C_AQ + 256 * ch)):
            put_qt(qa_ref, ch * 2 + half, 0, norm_pair(y, 0, QK_SCALE))
        for half, y in enumerate(pairs(C_AK + 256 * ch)):
            y = norm_pair(y, 1).astype(BF16)
            ka_ref[0, ch * 2 + half, 0] = y[:, :HEAD_DIM]
            ka_ref[0, ch * 2 + half, 1] = y[:, HEAD_DIM:]
        for half, y in enumerate(pairs(C_AV + 256 * ch)):
            yt = y.T.astype(BF16)
            for t in range(tm // T_ROW):
                va_ref[0, ch * 2 + half, t, 0:A_VDIM, :] = yt[:, T_ROW * t:T_ROW * (t + 1)]
                va_ref[0, ch * 2 + half, t, A_VDIM:A_VDIM + ONES_ROWS, :] = ones_pad(T_ROW)
        for half, y in enumerate(pairs(C_BQ + 256 * ch)):
            put_qt(qb_ref, ch, 2 * half, norm_pair(y, 2, QK_SCALE))
        for half, y in enumerate(pairs(C_CQ + 256 * ch)):
            put_qt(qc_ref, ch, 2 * half, norm_pair(y, 4, QK_SCALE))

    def put_k(ref, y):
        y = y.astype(BF16)
        ref[0, 0] = y[:, :HEAD_DIM]
        ref[0, 1] = y[:, HEAD_DIM:]

    yk, yv = pairs(C_BKV)
    put_k(kb_ref, norm_pair(yk, 3))
    put_vt(vb_ref, 0, yv, tm // TV_SWA)
    y0, y1 = pairs(C_CKV)
    for kv, y in enumerate((y0, y1)):
        tkv_ref[0, kv, 0] = y[:, :HEAD_DIM]
        tkv_ref[0, kv, 1] = y[:, HEAD_DIM:]
    yk, yv = pairs(C_CKV + 256)
    yk = norm_pair(yk, 6)
    row = lax.broadcasted_iota(jnp.int32, (tm, LANES), 0)
    hot = jnp.where(lane == HEAD_DIM + (row // SEL_BLOCK) % (T_ROW // SEL_BLOCK), 1.0, 0.0)
    ksel_ref[0, 0] = jnp.where(first, yk, hot).astype(BF16)
    ksel_ref[0, 1] = jnp.where(first, pltpu.roll(yk, HEAD_DIM, 1), hot).astype(BF16)
    put_vt(vsel_ref, 0, yv, tm // T_ROW)
    yk, yv = pairs(C_CKV + 512)
    put_k(kwin_ref, norm_pair(yk, 7))
    put_vt(vwin_ref, 0, yv, tm // T_ROW)

    gc_ref[0] = jax.nn.sigmoid(jnp.dot(hb, wcg_ref[...], preferred_element_type=F32))


def _inproj(x, nrm, w, wcg, gains2, layer):
    B, S, _ = x.shape
    tm = T_IN
    nt = S // tm
    sd = jax.ShapeDtypeStruct
    qt = lambda a, b, tq: sd((B, a, b, S // tq, HEAD_DIM, tq), BF16)
    kk = lambda n: sd((B, n, S, HEAD_DIM), BF16)
    vt = lambda n, dv, tk: sd((B, n, S // tk, dv + ONES_ROWS, tk), BF16)
    out_shape = (
        qt(A_HEADS, 2, T_ROW), sd((B, A_HEADS, 2, S, HEAD_DIM), BF16), vt(A_HEADS, A_VDIM, T_ROW),
        qt(KV_GROUPS, GROUP, T_SWA), kk(KV_GROUPS), vt(KV_GROUPS, HEAD_DIM, TV_SWA),
        qt(KV_GROUPS, GROUP, T_ROW),
        sd((B, 2, KV_GROUPS, S, HEAD_DIM), F32),
        sd((B, KV_GROUPS, S, LANES), BF16), vt(KV_GROUPS, HEAD_DIM, T_ROW),
        kk(KV_GROUPS), vt(KV_GROUPS, HEAD_DIM, T_ROW),
        sd((B, S, CG_PAD), F32),
    )
    s_qt = lambda a, b, tq: pl.BlockSpec((1, a, b, tm // tq, HEAD_DIM, tq), lambda b_, i: (b_, 0, 0, i, 0, 0))
    s_k = lambda n: pl.BlockSpec((1, n, tm, HEAD_DIM), lambda b_, i: (b_, 0, i, 0))
    s_k5 = lambda a, c: pl.BlockSpec((1, a, c, tm, HEAD_DIM), lambda b_, i: (b_, 0, 0, i, 0))
    s_vt = lambda n, dv, tk: pl.BlockSpec((1, n, tm // tk, dv + ONES_ROWS, tk), lambda b_, i: (b_, 0, i, 0, 0))
    row = lambda d: pl.BlockSpec((1, tm, d), lambda b_, i: (b_, i, 0))
    out_specs = (
        s_qt(A_HEADS, 2, T_ROW), s_k5(A_HEADS, 2), s_vt(A_HEADS, A_VDIM, T_ROW),
        s_qt(KV_GROUPS, GROUP, T_SWA), s_k(KV_GROUPS), s_vt(KV_GROUPS, HEAD_DIM, TV_SWA),
        s_qt(KV_GROUPS, GROUP, T_ROW), s_k5(2, KV_GROUPS),
        pl.BlockSpec((1, KV_GROUPS, tm, LANES), lambda b_, i: (b_, 0, i, 0)), s_vt(KV_GROUPS, HEAD_DIM, T_ROW),
        s_k(KV_GROUPS), s_vt(KV_GROUPS, HEAD_DIM, T_ROW),
        row(CG_PAD),
    )
    return pl.pallas_call(
        _inproj_kernel,
        grid=(B, nt),
        in_specs=[
            row(D_MODEL),
            pl.BlockSpec((1, D_MODEL), lambda b_, i: (0, 0)),
            pl.BlockSpec((None, D_MODEL, C_CG), lambda b_, i: (layer, 0, 0), pipeline_mode=pl.Buffered(1)),
            pl.BlockSpec((None, D_MODEL, CG_PAD), lambda b_, i: (layer, 0, 0)),
            pl.BlockSpec((8, LANES), lambda b_, i: (0, 0)),
        ],
        out_specs=out_specs,
        out_shape=out_shape,
        compiler_params=_params(2),
        name="inproj",
    )(x, nrm, w, wcg, gains2)


def _compress_kernel(t_ref, pos_ref, w1_ref, w2_ref, gain_ref, o_ref, ot_ref):
    t = t_ref[0, 0, 0]
    lo = (t + pos_ref[0, 0:1, :]).astype(BF16)
    hi = (t + pos_ref[0, 1:2, :]).astype(BF16)
    v = jnp.dot(lo, w1_ref[0, 0], preferred_element_type=F32)
    u = jnp.dot(hi, w1_ref[0, 1], preferred_element_type=F32)
    pre = v + pltpu.roll(u, NC_PAD - 1, 0)
    hcur = jax.nn.gelu(pre).astype(BF16)
    out = jnp.dot(hcur, w2_ref[0], preferred_element_type=F32)
    o64 = out[:, :HEAD_DIM]
    is_key = pl.program_id(1) == 0
    o_ref[0, 0, 0] = jnp.where(is_key, _rms(o64, gain_ref[...]), o64).astype(BF16)
    ot_ref[0, 0, 0] = out.T[:HEAD_DIM].astype(BF16)


def _compress(tkv16, pos, w1, w2p, gain):
    B = tkv16.shape[0]
    half = CMP_BLOCK // 2 * HEAD_DIM
    return pl.pallas_call(
        _compress_kernel,
        grid=(B, 2, KV_GROUPS),
        in_specs=[
            pl.BlockSpec((1, 1, 1, NC_PAD, half), lambda b, kv, g: (b, kv, g, 0, 0)),
            pl.BlockSpec((1, 2, half), lambda b, kv, g: (kv, 0, 0)),
            pl.BlockSpec((1, 2, half, CMP_HIDDEN), lambda b, kv, g: (kv, 0, 0, 0)),
            pl.BlockSpec((1, CMP_HIDDEN, LANES), lambda b, kv, g: (kv, 0, 0)),
            pl.BlockSpec((1, HEAD_DIM), lambda b, kv, g: (0, 0)),
        ],
        out_specs=(pl.BlockSpec((1, 1, 1, NC_PAD, HEAD_DIM), lambda b, kv, g: (b, kv, g, 0, 0)),
                   pl.BlockSpec((1, 1, 1, HEAD_DIM, NC_PAD), lambda b, kv, g: (b, kv, g, 0, 0))),
        out_shape=(jax.ShapeDtypeStruct((B, 2, KV_GROUPS, NC_PAD, HEAD_DIM), BF16),
                   jax.ShapeDtypeStruct((B, 2, KV_GROUPS, HEAD_DIM, NC_PAD), BF16)),
        compiler_params=_params(3),
        name="compress",
    )(tkv16, pos, w1, w2p, gain)


def _cmp_attn_kernel(qt_ref, kc_ref, vct_ref, ovl_ref, o_ref, sel_ref, score_ref, rank_ref, *, T, n_cmp):
    R = GROUP
    I = pl.program_id(2)
    j_idx = lax.broadcasted_iota(jnp.int32, (N_SEL_BLK, T), 0)
    tq = I * T + lax.broadcasted_iota(jnp.int32, (N_SEL_BLK, T), 1)
    cur = tq // SEL_BLOCK
    valid = j_idx * SEL_BLOCK <= tq
    forced = (j_idx == 0) | (j_idx == cur) | (j_idx == cur - 1)

    def attend(rows):
        kc = kc_ref[0, 0, 0, 0:rows, :]
        vct = vct_ref[0, 0, 0, :, 0:rows]
        c_idx = lax.broadcasted_iota(jnp.int32, (rows, T), 0)
        t_pos = I * T + lax.broadcasted_iota(jnp.int32, (rows, T), 1)
        cmask = (c_idx * CMP_STRIDE + (CMP_BLOCK - 1) <= t_pos) & (c_idx < n_cmp)
        scores = [jnp.dot(kc, qt_ref[0, 0, r, 0], preferred_element_type=F32) for r in range(R)]
        probs = []
        for r in range(R):
            s = jnp.where(cmask, scores[r], NEG_INF)
            m = jnp.max(s, axis=0, keepdims=True)
            e = jnp.where(cmask, jnp.exp2(s - m), 0.0)
            l = jnp.sum(e, axis=0, keepdims=True)
            probs.append(e * (1.0 / jnp.where(l > 0.0, l, 1.0)))
        outs = [jnp.dot(vct, p.astype(BF16), preferred_element_type=F32) for p in probs]
        o_ref[0] = jnp.concatenate(outs, axis=0).T
        psum = (probs[0] + probs[1]) + (probs[2] + probs[3])
        p_hi = psum.astype(BF16)
        p_lo = (psum - p_hi.astype(F32)).astype(BF16)
        ovl = ovl_ref[:, 0:rows]
        imp = jnp.dot(ovl, p_hi, preferred_element_type=F32) + jnp.dot(ovl, p_lo, preferred_element_type=F32)
        score_ref[...] = jnp.where(valid, imp + jnp.where(forced, FORCE_BONUS, 0.0), NEG_INF)

    chunk = NC_PAD // 4
    vis_per_tile = T // CMP_STRIDE
    for v in range(4):
        lo = -(-(v * chunk) // vis_per_tile)
        hi = -(-((v + 1) * chunk) // vis_per_tile)

        @pl.when((I >= lo) & (I < hi))
        def _(v=v):
            attend((v + 1) * chunk)

    rank_ref[...] = jnp.zeros(rank_ref.shape, F32)
    n_live = (I + 1) * (T // SEL_BLOCK)
    sub = 8
    j_loc = lax.broadcasted_iota(jnp.int32, (sub, T), 0)
    for c in range(N_SEL_BLK // sub):
        @pl.when(c * sub < n_live)
        def _(c=c):
            n_grp = N_SEL_BLK // sub
            grps = [score_ref[g * sub:(g + 1) * sub, :] for g in range(n_grp)]
            cnts = [rank_ref[g * sub:(g + 1) * sub, :] for g in range(n_grp)]
            for jp in range(c * sub, (c + 1) * sub):
                row = jnp.broadcast_to(score_ref[jp:jp + 1, :], (sub, T))
                for g in range(n_grp):
                    if g * sub > jp:
                        beats = row >= grps[g]
                    elif (g + 1) * sub - 1 < jp:
                        beats = row > grps[g]
                    else:
                        beats = (row > grps[g]) | ((row == grps[g]) & (j_loc + g * sub > jp))
                    cnts[g] = cnts[g] + jnp.where(beats, 1.0, 0.0)
            for g in range(n_grp):
                rank_ref[g * sub:(g + 1) * sub, :] = cnts[g]
    blk = T // SEL_BLOCK
    for j in range(N_SEL_BLK // blk):
        keep = rank_ref[j * blk:(j + 1) * blk, :] < N_SELECT
        slab = jnp.concatenate([jnp.where(keep, 0.0, NEG_INF), jnp.zeros((MASK_ROWS - blk, T), F32)], axis=0)
        sel_ref[0, 0, 0, j] = slab.astype(BF16)


def _cmp_attn(qct, kc, vct, ovl, n_cmp):
    B, G, R, NQ, _, T = qct.shape
    S = NQ * T
    return pl.pallas_call(
        functools.partial(_cmp_attn_kernel, T=T, n_cmp=n_cmp),
        grid=(B, G, S // T),
        in_specs=[
            pl.BlockSpec((1, 1, R, 1, HEAD_DIM, T), lambda b, g, i: (b, g, 0, i, 0, 0)),
            pl.BlockSpec((1, 1, 1, NC_PAD, HEAD_DIM), lambda b, g, i: (b, 0, g, 0, 0)),
            pl.BlockSpec((1, 1, 1, HEAD_DIM, NC_PAD), lambda b, g, i: (b, 1, g, 0, 0)),
            pl.BlockSpec((N_SEL_BLK, NC_PAD), lambda b, g, i: (0, 0)),
        ],
        out_specs=(
            pl.BlockSpec((1, T, R * HEAD_DIM), lambda b, g, i: (b, i, g)),
            pl.BlockSpec((1, 1, 1, NQ, MASK_ROWS, T), lambda b, g, i: (b, g, i, 0, 0, 0)),
        ),
        out_shape=(
            jax.ShapeDtypeStruct((B, S, G * R * HEAD_DIM), F32),
            jax.ShapeDtypeStruct((B, G, NQ, NQ, MASK_ROWS, T), BF16),
        ),
        scratch_shapes=[pltpu.VMEM((N_SEL_BLK, T), F32), pltpu.VMEM((N_SEL_BLK, T), F32)],
        compiler_params=_params(3),
        name="cmp_attn",
    )(qct, kc, vct, ovl)


def _flash_kernel(*refs, R, T, NQ, Dv, nrel, k_per_r, shared_bias, diff, has_sel, early_bias, lam_init):
    it = iter(refs)
    qt_ref, k_ref, vt_ref, bt_ref = (next(it) for _ in range(4))
    sel_ref = next(it) if has_sel else None
    dl_ref = next(it) if diff else None
    sub_ref = next(it) if diff else None
    o_ref, m_ref, acc_ref, s_ref, p_ref, al_ref, tmax_ref = (next(it) for _ in range(7))

    m_ref[...] = jnp.full(m_ref.shape, NEG_INF, F32)
    acc_ref[...] = jnp.zeros(acc_ref.shape, F32)
    bias_r = (lambda r: 0) if shared_bias else (lambda r: r)
    if has_sel:
        pad_rows = jnp.zeros((k_ref.shape[-1] - HEAD_DIM - MASK_ROWS, T), BF16)
    n_steps = NQ * (NQ + 1) // 2
    N_S = SCORE_SLOTS
    assert N_S % 2 == 0 and n_steps % N_S == 0

    def score_stage(I, J, slot):
        off = pl.multiple_of(J * T, T)
        rel = jnp.minimum(I - J, nrel - 1)
        for r in range(R):
            kt = k_ref[0, 0, r, pl.ds(off, T), :] if k_per_r else k_ref[0, 0, pl.ds(off, T), :]
            w = qt_ref[0, 0, r, I]
            if has_sel:
                w = jnp.concatenate([w, sel_ref[0, 0, I, J], pad_rows], axis=0)
            s = jnp.dot(kt, w, preferred_element_type=F32)
            if early_bias:
                s = s + bt_ref[0, rel, bias_r(r)]
                tmax_ref[slot, r] = jnp.max(s, axis=0, keepdims=True)
            s_ref[slot, r] = s

    def softmax_stage(I, J, s_slot, slot):
        for r in range(R):
            m_prev = m_ref[I, r]
            if early_bias:
                s, tile_max = s_ref[s_slot, r], tmax_ref[s_slot, r]
            else:
                s = s_ref[s_slot, r] + bt_ref[0, jnp.minimum(I - J, nrel - 1), bias_r(r)]
                tile_max = jnp.max(s, axis=0, keepdims=True)
            m_new = jnp.maximum(m_prev, tile_max)
            al_ref[slot, r] = jnp.exp2(m_prev - m_new)
            p_ref[slot, r] = jnp.exp2(s - m_new).astype(BF16)
            m_ref[I, r] = m_new

    def value_stage(J, slot):
        vt = vt_ref[0, 0, J]
        return [jnp.dot(vt, p_ref[slot, r], preferred_element_type=F32) for r in range(R)]

    def accumulate(I, slot, pvs):
        for r in range(R):
            acc_ref[I, r] = al_ref[slot, r] * acc_ref[I, r] + pvs[r]

    def advance(I, J):
        last = J == I
        return jnp.where(last, I + 1, I), jnp.where(last, 0, J + 1)

    def step(cur, prev, t):
        other = 1 - t % 2
        nxt = advance(*cur)
        pvs = value_stage(prev[1], other)
        score_stage(jnp.minimum(nxt[0], NQ - 1), nxt[1], (t + 1) % N_S)
        softmax_stage(cur[0], cur[1], t % N_S, t % 2)
        accumulate(prev[0], other, pvs)
        return nxt

    def trip(u, carry):
        cur, prev = carry[:2], carry[2:]
        for t in range(N_S):
            cur, prev = step(cur, prev, t), cur
        return (*cur, *prev)

    zero = jnp.int32(0)
    score_stage(zero, zero, 0)
    p_ref[1] = jnp.zeros(p_ref.shape[1:], BF16)
    al_ref[1] = jnp.ones(al_ref.shape[1:], F32)
    lax.fori_loop(0, n_steps // N_S, trip, (zero, zero, zero, zero))
    accumulate(NQ - 1, 1, value_stage(NQ - 1, 1))

    if diff:
        dl = dl_ref[...]
        lam = (jnp.exp(jnp.sum(dl[0:1] * dl[1:2], keepdims=True))
               - jnp.exp(jnp.sum(dl[2:3] * dl[3:4], keepdims=True)) + lam_init)

    def finish(I, carry):
        outs = []
        for r in range(R):
            acc = acc_ref[I, r]
            outs.append(acc[:Dv] / acc[Dv:Dv + 1])
        rows = pl.ds(pl.multiple_of(I * T, T), T)
        if diff:
            o = (outs[0] - lam * outs[1]).T
            o_ref[0, rows, :] = _rms(o, sub_ref[...]) * (1.0 - lam_init)
        else:
            o_ref[0, rows, :] = jnp.concatenate(outs, axis=0).T
        return carry

    lax.fori_loop(0, NQ, finish, 0)


def _flash_attn(qt, k, vt, bias_tab, *, sel=None, dl=None, subln=None, lam_init=0.0, name="attn"):
    B, G, R, NQ, _, T = qt.shape
    S = NQ * T
    Dv = vt.shape[-2] - ONES_ROWS
    nrel, bias_heads = bias_tab.shape[1], bias_tab.shape[2]
    k_per_r = k.ndim == 5
    has_sel = sel is not None
    diff = dl is not None
    ins = [qt, k, vt, bias_tab]
    in_specs = [
        pl.BlockSpec((1, 1, R, NQ, HEAD_DIM, T), lambda b, g: (b, g, 0, 0, 0, 0)),
        (pl.BlockSpec((1, 1, R, S, HEAD_DIM), lambda b, g: (b, g, 0, 0, 0)) if k_per_r
         else pl.BlockSpec((1, 1, S, k.shape[-1]), lambda b, g: (b, g, 0, 0))),
        pl.BlockSpec((1, 1, NQ, Dv + ONES_ROWS, T), lambda b, g: (b, g, 0, 0, 0)),
        pl.BlockSpec((1, nrel, bias_heads, T, T), lambda b, g: (g, 0, 0, 0, 0)),
    ]
    if has_sel:
        ins.append(sel)
        in_specs.append(pl.BlockSpec((1, 1, NQ, NQ, MASK_ROWS, T), lambda b, g: (b, g, 0, 0, 0, 0)))
    if diff:
        ins += [dl, subln]
        in_specs += [pl.BlockSpec((4, HEAD_DIM), lambda b, g: (0, 0)),
                     pl.BlockSpec((1, A_VDIM), lambda b, g: (0, 0))]
    out_w = Dv if diff else R * Dv
    kern = functools.partial(_flash_kernel, R=R, T=T, NQ=NQ, Dv=Dv, nrel=nrel, k_per_r=k_per_r,
                             shared_bias=bias_heads == 1, diff=diff, has_sel=has_sel, early_bias=not diff,
                             lam_init=lam_init)
    return pl.pallas_call(
        kern,
        grid=(B, G),
        in_specs=in_specs,
        out_specs=pl.BlockSpec((1, S, out_w), lambda b, g: (b, 0, g)),
        out_shape=jax.ShapeDtypeStruct((B, S, G * out_w), F32),
        scratch_shapes=[pltpu.VMEM((NQ, R, 1, T), F32), pltpu.VMEM((NQ, R, Dv + ONES_ROWS, T), F32),
                        pltpu.VMEM((SCORE_SLOTS, R, T, T), F32), pltpu.VMEM((2, R, T, T), BF16),
                        pltpu.VMEM((2, R, 1, T), F32), pltpu.VMEM((SCORE_SLOTS, R, 1, T), F32)],
        compiler_params=_params(2),
        name=name,
    )(*ins)


def _window_kernel(*refs, R, T, NK, TV, Dv, n_sub, has_sink):
    it = iter(refs)
    qt_ref, k_ref, vt_ref, bt_ref = (next(it) for _ in range(4))
    sink_ref = next(it) if has_sink else None
    o_ref = next(it)
    heads = [(t, r) for t in range(n_sub) for r in range(R)]
    tile = [pl.program_id(2) * n_sub + t for t in range(n_sub)]
    entry = [jnp.minimum(I, -(-(NK - T) // T)) for I in tile]
    first_key = [pl.multiple_of(I * T - jnp.minimum(e * T, NK - T), TV) for I, e in zip(tile, entry)]
    keys = [k_ref[0, 0, pl.ds(fk, NK), :] for fk in first_key]
    scores = {(t, r): jnp.dot(keys[t], qt_ref[0, 0, r, t], preferred_element_type=F32) for t, r in heads}
    probs, maxes = {}, {}
    for t, r in heads:
        s = scores[t, r] + bt_ref[0, entry[t], r]
        maxes[t, r] = jnp.max(s, axis=0, keepdims=True)
        probs[t, r] = jnp.exp2(s - maxes[t, r]).astype(BF16)
    accs = {}
    for t, r in heads:
        acc = None
        for c in range(NK // TV):
            part = jnp.dot(vt_ref[0, 0, first_key[t] // TV + c], probs[t, r][c * TV:(c + 1) * TV],
                           preferred_element_type=F32)
            acc = part if acc is None else acc + part
        accs[t, r] = acc
    for t in range(n_sub):
        outs = []
        for r in range(R):
            num, l = accs[t, r][:Dv], accs[t, r][Dv:Dv + 1]
            if has_sink:
                sk = sink_ref[0, r] * LOG2E
                m_f = jnp.maximum(maxes[t, r], sk)
                w = jnp.exp2(maxes[t, r] - m_f)
                outs.append(num * w / (l * w + jnp.exp2(sk - m_f)))
            else:
                outs.append(num / l)
        o_ref[0, t * T:(t + 1) * T, :] = jnp.concatenate(outs, axis=0).T


def _window_attn(qt, k, vt, bias_tab, *, n_sub=2, sink=None, name="attn_window"):
    B, G, R, NQ, _, T = qt.shape
    S = NQ * T
    TV = vt.shape[-1]
    Dv = vt.shape[-2] - ONES_ROWS
    entries, NK = bias_tab.shape[1], bias_tab.shape[3]
    has_sink = sink is not None
    ins = [qt, k, vt, bias_tab]
    in_specs = [
        pl.BlockSpec((1, 1, R, n_sub, HEAD_DIM, T), lambda b, g, i: (b, g, 0, i, 0, 0)),
        pl.BlockSpec((1, 1, S, HEAD_DIM), lambda b, g, i: (b, g, 0, 0)),
        pl.BlockSpec((1, 1, S // TV, Dv + ONES_ROWS, TV), lambda b, g, i: (b, g, 0, 0, 0)),
        pl.BlockSpec((1, entries, R, NK, T), lambda b, g, i: (g, 0, 0, 0, 0)),
    ]
    if has_sink:
        ins.append(sink)
        in_specs.append(pl.BlockSpec((1, R, 1, 1), lambda b, g, i: (g, 0, 0, 0)))
    kern = functools.partial(_window_kernel, R=R, T=T, NK=NK, TV=TV, Dv=Dv, n_sub=n_sub, has_sink=has_sink)
    return pl.pallas_call(
        kern,
        grid=(B, G, NQ // n_sub),
        in_specs=in_specs,
        out_specs=pl.BlockSpec((1, n_sub * T, R * Dv), lambda b, g, i: (b, i, g)),
        out_shape=jax.ShapeDtypeStruct((B, S, G * R * Dv), F32),
        compiler_params=_params(3),
        name=name,
    )(*ins)


def _bucket_np(dist):
    n = np.maximum(dist, 0)
    max_exact = NUM_BUCKETS // 2
    nf = np.maximum(n, 1).astype(np.float32)
    large = max_exact + (np.log(nf / max_exact) / math.log(MAX_DISTANCE / max_exact)
                         * (NUM_BUCKETS - max_exact)).astype(np.int32)
    large = np.minimum(large, NUM_BUCKETS - 1)
    return np.where(n < max_exact, n, large)


def _bucket_starts():
    b = _bucket_np(np.arange(4 * MAX_DISTANCE))
    assert (np.diff(b) >= 0).all() and b[-1] == NUM_BUCKETS - 1
    return [int(np.argmax(b >= i)) for i in range(NUM_BUCKETS)]


def _bias_kernel(tab_ref, o_ref, *, T, rows, nrel, window, back_max, starts):
    h = pl.program_id(0)
    key = lax.broadcasted_iota(jnp.int32, (rows, T), 0)
    qry = lax.broadcasted_iota(jnp.int32, (rows, T), 1)
    for rel in range(nrel):
        dist = min(rel * T, back_max) + qry - key
        val = jnp.full((rows, T), tab_ref[0, h] * LOG2E, F32)
        for b in range(1, NUM_BUCKETS):
            val = jnp.where(dist >= starts[b], tab_ref[b, h] * LOG2E, val)
        allowed = dist >= 0
        if window is not None:
            allowed = allowed & (dist < window)
        o_ref[0, rel, 0] = jnp.where(allowed, val, NEG_INF)


def _bias_tiles(tab, T, nrel, window, R, rows=None):
    H = tab.shape[1]
    back_max = nrel * T if rows is None else rows - T
    rows = T if rows is None else rows
    return pl.pallas_call(
        functools.partial(_bias_kernel, T=T, rows=rows, nrel=nrel, window=window, back_max=back_max,
                          starts=_bucket_starts()),
        grid=(H,),
        in_specs=[pl.BlockSpec(memory_space=pltpu.SMEM)],
        out_specs=pl.BlockSpec((1, nrel, 1, rows, T), lambda h: (h // R, 0, h % R, 0, 0)),
        out_shape=jax.ShapeDtypeStruct((H // R, nrel, R, rows, T), F32),
        compiler_params=_params(1),
        name="bias_tiles",
    )(tab)


def _merge_kernel(x_ref, nrm_ref, oa_ref, ob_ref, oc_ref, os_ref, ow_ref, gc_ref, ex_ref, wmg_ref, wb_ref, wo_ref,
                  o_ref):
    x = x_ref[...]
    hb = _rms(x, nrm_ref[...]).astype(BF16)
    gc = gc_ref[...]
    g1 = gc.astype(BF16)
    r1 = gc - g1.astype(F32)
    g2 = r1.astype(BF16)
    g3 = (r1 - g2.astype(F32)).astype(BF16)
    ex = ex_ref[...]
    gx = (jnp.dot(g1, ex, preferred_element_type=F32) + jnp.dot(g2, ex, preferred_element_type=F32)
          + jnp.dot(g3, ex, preferred_element_type=F32))
    oc = (gx[:, 0:MIX_WIDTH] * oc_ref[...] + gx[:, MIX_WIDTH:2 * MIX_WIDTH] * os_ref[...]
          + gx[:, 2 * MIX_WIDTH:3 * MIX_WIDTH] * ow_ref[...])
    z = None
    for n, br in enumerate((oa_ref[...], ob_ref[...], oc)):
        y = jnp.dot(br.astype(BF16), wb_ref[n], preferred_element_type=F32)
        gate = jax.nn.sigmoid(jnp.dot(hb, wmg_ref[:, n * D_MODEL:(n + 1) * D_MODEL], preferred_element_type=F32))
        t = gate * y
        z = t if z is None else z + t
    o_ref[...] = x + jnp.dot(z.astype(BF16), wo_ref[...], preferred_element_type=F32)


def _merge(x2, nrm, oa, ob, oc, osel, ow, gc, expand_g, wmg, wb, wo, layer, tm=512):
    Tn = x2.shape[0]
    row = lambda d: pl.BlockSpec((tm, d), lambda i: (i, 0))
    const = lambda shape: pl.BlockSpec(shape, lambda i: (0,) * len(shape), pipeline_mode=pl.Buffered(1))
    weight = lambda shape: pl.BlockSpec((None,) + shape, lambda i: (layer,) + (0,) * len(shape),
                                        pipeline_mode=pl.Buffered(1))
    return pl.pallas_call(
        _merge_kernel,
        grid=(Tn // tm,),
        in_specs=[row(D_MODEL), const((1, D_MODEL)),
                  row(MIX_WIDTH), row(MIX_WIDTH), row(MIX_WIDTH), row(MIX_WIDTH), row(MIX_WIDTH),
                  row(CG_PAD),
                  const((CG_PAD, 3 * MIX_WIDTH)),
                  weight((D_MODEL, 3 * D_MODEL)),
                  weight((3, MIX_WIDTH, D_MODEL)),
                  weight((D_MODEL, D_MODEL))],
        out_specs=row(D_MODEL),
        out_shape=jax.ShapeDtypeStruct((Tn, D_MODEL), F32),
        compiler_params=_params(1),
        name="merge",
    )(x2, nrm, oa, ob, oc, osel, ow, gc, expand_g, wmg, wb, wo)


def _mlp_kernel(x_ref, nrm_ref, wu_ref, wd_ref, o_ref, h_ref, acc_ref):
    f = pl.program_id(1)

    @pl.when(f == 0)
    def _():
        h_ref[...] = _rms(x_ref[...], nrm_ref[...]).astype(BF16)
        acc_ref[...] = jnp.zeros(acc_ref.shape, F32)

    u = jnp.dot(h_ref[...], wu_ref[...], preferred_element_type=F32)
    u = jnp.square(jnp.maximum(u, 0.0)).astype(BF16)
    acc_ref[...] += jnp.dot(u, wd_ref[...], preferred_element_type=F32)

    @pl.when(f == pl.num_programs(1) - 1)
    def _():
        o_ref[...] = x_ref[...] + acc_ref[...]


def _mlp(x2, nrm, wu, wd, layer, tm=1024, tf=1024):
    Tn = x2.shape[0]
    return pl.pallas_call(
        _mlp_kernel,
        grid=(Tn // tm, D_FF // tf),
        in_specs=[pl.BlockSpec((tm, D_MODEL), lambda i, f: (i, 0)),
                  pl.BlockSpec((1, D_MODEL), lambda i, f: (0, 0)),
                  pl.BlockSpec((None, D_MODEL, tf), lambda i, f: (layer, 0, f)),
                  pl.BlockSpec((None, tf, D_MODEL), lambda i, f: (layer, f, 0))],
        out_specs=pl.BlockSpec((tm, D_MODEL), lambda i, f: (i, 0)),
        out_shape=jax.ShapeDtypeStruct((Tn, D_MODEL), F32),
        scratch_shapes=[pltpu.VMEM((tm, D_MODEL), BF16), pltpu.VMEM((tm, D_MODEL), F32)],
        compiler_params=_params(2),
        name="mlp",
    )(x2, nrm, wu, wd)


def _overlap(n_cmp):
    c_start = np.arange(NC_PAD) * CMP_STRIDE
    j_start = np.arange(N_SEL_BLK) * SEL_BLOCK
    ov = ((c_start[None, :] < j_start[:, None] + SEL_BLOCK) & (c_start[None, :] + CMP_BLOCK > j_start[:, None])
          & (np.arange(NC_PAD)[None, :] < n_cmp))
    return jnp.asarray(ov.astype(np.float32), BF16)


def _gate_expand():
    e = np.zeros((CG_PAD, 3 * MIX_WIDTH), np.float32)
    for h in range(C_HEADS):
        for j in range(3):
            e[h * 3 + j, j * MIX_WIDTH + h * HEAD_DIM:j * MIX_WIDTH + (h + 1) * HEAD_DIM] = 1.0
    return jnp.asarray(e, BF16)


def kernel(x, w_in, qk_gain, diff_lambda, diff_subln, sinks, cmp_pos, cmp_w1, cmp_w2,
           w_branch, w_out, norm_mix, norm_mlp, w_up, w_down, rel_bias):
    B, S, _ = x.shape
    depth = w_in.shape[0]
    n_cmp = (S - CMP_BLOCK) // CMP_STRIDE + 1
    assert S % T_ROW == 0 and S // CMP_STRIDE == NC_PAD and S // SEL_BLOCK == N_SEL_BLK
    half = CMP_BLOCK // 2 * HEAD_DIM

    w_heads = w_in[:, :, :C_CG].astype(BF16)
    w_cg = jnp.pad(w_in[:, :, C_CG:C_CG + N_CG], ((0, 0), (0, 0), (0, CG_PAD - N_CG))).astype(BF16)
    w_mg = w_in[:, :, C_CG + N_CG:].astype(BF16)
    gains2 = jnp.concatenate([qk_gain, qk_gain], axis=-1)
    w1 = cmp_w1.astype(BF16).reshape(depth, 2, 2, half, CMP_HIDDEN)
    w2p = jnp.pad(cmp_w2, ((0, 0), (0, 0), (0, 0), (0, LANES - HEAD_DIM))).astype(BF16)
    pos = cmp_pos.reshape(depth, 2, 2, half)
    wb = w_branch.astype(BF16)
    wo = w_out.astype(BF16)
    wu = w_up.astype(BF16)
    wd = w_down.astype(BF16)

    bias_a = rel_bias[:, :A_HEADS]
    bias_b = rel_bias[:, A_HEADS:A_HEADS + B_HEADS]
    bias_c = rel_bias[:, A_HEADS + B_HEADS:]
    bt_a = _bias_tiles(bias_a, T_ROW, 3, None, 1)
    bt_b = _bias_tiles(bias_b, T_SWA, -(-B_WINDOW // T_SWA) + 1, B_WINDOW, GROUP, rows=B_WINDOW + T_SWA)
    bt_sel = _bias_tiles(bias_c, T_ROW, 3, None, GROUP)
    bt_win = _bias_tiles(bias_c, T_ROW, C_WINDOW // T_ROW + 1, C_WINDOW, GROUP, rows=C_WINDOW + T_ROW)
    ovl = _overlap(n_cmp)
    expand_g = _gate_expand()

    for layer in range(depth):
        lam_init = 0.8 - 0.6 * math.exp(-0.3 * layer)
        (qat, ka, vat, qbt, kb, vbt, qct, tkv, ksel, vselt, kwin, vwint, gc) = _inproj(
            x, norm_mix[layer][None], w_heads, w_cg, gains2[layer], layer)
        oa = _flash_attn(qat, ka, vat, bt_a, dl=diff_lambda[layer],
                   subln=diff_subln[layer][None], lam_init=lam_init, name="attn_diff")
        ob = _window_attn(qbt, kb, vbt, bt_b, n_sub=4,
                          sink=sinks[layer].reshape(KV_GROUPS, GROUP, 1, 1), name="attn_swa")
        kc, vct = _compress(tkv.reshape(B, 2, KV_GROUPS, NC_PAD, half), pos[layer],
                            w1[layer], w2p[layer], qk_gain[layer][5:6])
        ocmp, sel = _cmp_attn(qct, kc, vct, ovl, n_cmp)
        osel = _flash_attn(qct, ksel, vselt, bt_sel, sel=sel, name="attn_sel")
        owin = _window_attn(qct, kwin, vwint, bt_win, n_sub=4, name="attn_win")
        f2 = lambda a: a.reshape(B * S, a.shape[-1])
        x2 = _merge(f2(x), norm_mix[layer][None], f2(oa), f2(ob), f2(ocmp), f2(osel), f2(owin), f2(gc),
                    expand_g, w_mg, wb, wo, layer)
        x2 = _mlp(x2, norm_mlp[layer][None], wu, wd, layer)
        x = x2.reshape(B, S, D_MODEL)
    return x
```

```python
import functools
import math

import numpy as np
import jax
import jax.numpy as jnp
from jax import lax
from jax.experimental import pallas as pl
from jax.experimental.pallas import tpu as pltpu

F32 = jnp.float32
BF16 = jnp.bfloat16

D_MODEL = 1024
HEAD_DIM = 64
A_HEADS = 4
A_VDIM = 128
B_HEADS = 8
B_WINDOW = 128
C_HEADS = 8
KV_GROUPS = 2
GROUP = 4
CMP_BLOCK = 32
CMP_STRIDE = 16
CMP_HIDDEN = 256
SEL_BLOCK = 64
N_SELECT = 16
C_WINDOW = 512
MIX_WIDTH = 512
D_FF = 4096
NUM_BUCKETS = 32
MAX_DISTANCE = 128
NEG_INF = -1e30
FORCE_BONUS = 1e4
EPS = 1e-6
LOG2E = 1.4426950408889634
QK_SCALE = HEAD_DIM ** -0.5 * LOG2E

C_AQ, C_AK, C_AV, C_BQ, C_BKV, C_CQ, C_CKV, C_CG = 0, 512, 1024, 1536, 2048, 2304, 2816, 3584
N_CG = C_HEADS * 3
LANES = 128
CG_PAD = LANES
NC_PAD = 256
N_SEL_BLK = 64
ONES_ROWS = 16
MASK_ROWS = 16

SCORE_SLOTS = 8

T_IN = 512
T_ROW = 256
T_SWA = 256
TV_SWA = 128

VMEM_LIMIT = 48 * 1024 * 1024


def _rms(x, gain):
    ms = jnp.mean(x * x, axis=-1, keepdims=True)
    return (x * lax.rsqrt(ms + EPS)) * gain


def _params(n_axes):
    return pltpu.CompilerParams(dimension_semantics=("arbitrary",) * n_axes,
                                vmem_limit_bytes=VMEM_LIMIT)


def _inproj_kernel(x_ref, nrm_ref, w_ref, wcg_ref, gains_ref,
                   qa_ref, ka_ref, va_ref, qb_ref, kb_ref, vb_ref, qc_ref,
                   tkv_ref, ksel_ref, vsel_ref, kwin_ref, vwin_ref, gc_ref):
    tm = x_ref.shape[1]
    hb = _rms(x_ref[0], nrm_ref[...]).astype(BF16)
    lane = lax.broadcasted_iota(jnp.int32, (tm, LANES), 1)
    first = lane < HEAD_DIM

    def ones_pad(width):
        return jnp.where(lax.broadcasted_iota(jnp.int32, (ONES_ROWS, width), 0) == 0, 1.0, 0.0).astype(BF16)

    def mm(c0, n):
        return jnp.dot(hb, w_ref[:, c0:c0 + n], preferred_element_type=F32)

    def norm_pair(y, gain_idx, scale=None):
        sq = y * y
        ms0 = jnp.sum(jnp.where(first, sq, 0.0), axis=-1, keepdims=True) * (1.0 / HEAD_DIM)
        ms1 = jnp.sum(jnp.where(first, 0.0, sq), axis=-1, keepdims=True) * (1.0 / HEAD_DIM)
        inv = jnp.where(first, lax.rsqrt(ms0 + EPS), lax.rsqrt(ms1 + EPS))
        out = (y * inv) * gains_ref[gain_idx:gain_idx + 1, :]
        return out if scale is None else out * scale

    def pairs(c0):
        r = mm(c0, 256)
        return r[:, :LANES], r[:, LANES:]

    def put_qt(ref, i0, i1, y):
        yt = y.T.astype(BF16)
        n_tiles = ref.shape[3]
        tq = tm // n_tiles
        for t in range(n_tiles):
            ref[0, i0, i1, t] = yt[:HEAD_DIM, tq * t:tq * (t + 1)]
            ref[0, i0, i1 + 1, t] = yt[HEAD_DIM:, tq * t:tq * (t + 1)]

    def put_vt(ref, g0, y, n_tiles):
        yt = y.T.astype(BF16)
        tk = tm // n_tiles
        for g in range(2):
            for t in range(n_tiles):
                ref[0, g0 + g, t, 0:HEAD_DIM, :] = yt[HEAD_DIM * g:HEAD_DIM * (g + 1), tk * t:tk * (t + 1)]
                ref[0, g0 + g, t, HEAD_DIM:HEAD_DIM + ONES_ROWS, :] = ones_pad(tk)

    for ch in range(2):
        for half, y in enumerate(pairs(C_AQ + 256 * ch)):
            put_qt(qa_ref, ch * 2 + half, 0, norm_pair(y, 0, QK_SCALE))
        for half, y in enumerate(pairs(C_AK + 256 * ch)):
            y = norm_pair(y, 1).astype(BF16)
            ka_ref[0, ch * 2 + half, 0] = y[:, :HEAD_DIM]
            ka_ref[0, ch * 2 + half, 1] = y[:, HEAD_DIM:]
        for half, y in enumerate(pairs(C_AV + 256 * ch)):
            yt = y.T.astype(BF16)
            for t in range(tm // T_ROW):
                va_ref[0, ch * 2 + half, t, 0:A_VDIM, :] = yt[:, T_ROW * t:T_ROW * (t + 1)]
                va_ref[0, ch * 2 + half, t, A_VDIM:A_VDIM + ONES_ROWS, :] = ones_pad(T_ROW)
        for half, y in enumerate(pairs(C_BQ + 256 * ch)):
            put_qt(qb_ref, ch, 2 * half, norm_pair(y, 2, QK_SCALE))
        for half, y in enumerate(pairs(C_CQ + 256 * ch)):
            put_qt(qc_ref, ch, 2 * half, norm_pair(y, 4, QK_SCALE))

    def put_k(ref, y):
        y = y.astype(BF16)
        ref[0, 0] = y[:, :HEAD_DIM]
        ref[0, 1] = y[:, HEAD_DIM:]

    yk, yv = pairs(C_BKV)
    put_k(kb_ref, norm_pair(yk, 3))
    put_vt(vb_ref, 0, yv, tm // TV_SWA)
    y0, y1 = pairs(C_CKV)
    for kv, y in enumerate((y0, y1)):
        tkv_ref[0, kv, 0] = y[:, :HEAD_DIM]
        tkv_ref[0, kv, 1] = y[:, HEAD_DIM:]
    yk, yv = pairs(C_CKV + 256)
    yk = norm_pair(yk, 6)
    row = lax.broadcasted_iota(jnp.int32, (tm, LANES), 0)
    hot = jnp.where(lane == HEAD_DIM + (row // SEL_BLOCK) % (T_ROW // SEL_BLOCK), 1.0, 0.0)
    ksel_ref[0, 0] = jnp.where(first, yk, hot).astype(BF16)
    ksel_ref[0, 1] = jnp.where(first, pltpu.roll(yk, HEAD_DIM, 1), hot).astype(BF16)
    put_vt(vsel_ref, 0, yv, tm // T_ROW)
    yk, yv = pairs(C_CKV + 512)
    put_k(kwin_ref, norm_pair(yk, 7))
    put_vt(vwin_ref, 0, yv, tm // T_ROW)

    gc_ref[0] = jax.nn.sigmoid(jnp.dot(hb, wcg_ref[...], preferred_element_type=F32))


def _inproj(x, nrm, w, wcg, gains2, layer):
    B, S, _ = x.shape
    tm = T_IN
    nt = S // tm
    sd = jax.ShapeDtypeStruct
    qt = lambda a, b, tq: sd((B, a, b, S // tq, HEAD_DIM, tq), BF16)
    kk = lambda n: sd((B, n, S, HEAD_DIM), BF16)
    vt = lambda n, dv, tk: sd((B, n, S // tk, dv + ONES_ROWS, tk), BF16)
    out_shape = (
        qt(A_HEADS, 2, T_ROW), sd((B, A_HEADS, 2, S, HEAD_DIM), BF16), vt(A_HEADS, A_VDIM, T_ROW),
        qt(KV_GROUPS, GROUP, T_SWA), kk(KV_GROUPS), vt(KV_GROUPS, HEAD_DIM, TV_SWA),
        qt(KV_GROUPS, GROUP, T_ROW),
        sd((B, 2, KV_GROUPS, S, HEAD_DIM), F32),
        sd((B, KV_GROUPS, S, LANES), BF16), vt(KV_GROUPS, HEAD_DIM, T_ROW),
        kk(KV_GROUPS), vt(KV_GROUPS, HEAD_DIM, T_ROW),
        sd((B, S, CG_PAD), F32),
    )
    s_qt = lambda a, b, tq: pl.BlockSpec((1, a, b, tm // tq, HEAD_DIM, tq), lambda b_, i: (b_, 0, 0, i, 0, 0))
    s_k = lambda n: pl.BlockSpec((1, n, tm, HEAD_DIM), lambda b_, i: (b_, 0, i, 0))
    s_k5 = lambda a, c: pl.BlockSpec((1, a, c, tm, HEAD_DIM), lambda b_, i: (b_, 0, 0, i, 0))
    s_vt = lambda n, dv, tk: pl.BlockSpec((1, n, tm // tk, dv + ONES_ROWS, tk), lambda b_, i: (b_, 0, i, 0, 0))
    row = lambda d: pl.BlockSpec((1, tm, d), lambda b_, i: (b_, i, 0))
    out_specs = (
        s_qt(A_HEADS, 2, T_ROW), s_k5(A_HEADS, 2), s_vt(A_HEADS, A_VDIM, T_ROW),
        s_qt(KV_GROUPS, GROUP, T_SWA), s_k(KV_GROUPS), s_vt(KV_GROUPS, HEAD_DIM, TV_SWA),
        s_qt(KV_GROUPS, GROUP, T_ROW), s_k5(2, KV_GROUPS),
        pl.BlockSpec((1, KV_GROUPS, tm, LANES), lambda b_, i: (b_, 0, i, 0)), s_vt(KV_GROUPS, HEAD_DIM, T_ROW),
        s_k(KV_GROUPS), s_vt(KV_GROUPS, HEAD_DIM, T_ROW),
        row(CG_PAD),
    )
    return pl.pallas_call(
        _inproj_kernel,
        grid=(B, nt),
        in_specs=[
            row(D_MODEL),
            pl.BlockSpec((1, D_MODEL), lambda b_, i: (0, 0)),
            pl.BlockSpec((None, D_MODEL, C_CG), lambda b_, i: (layer, 0, 0), pipeline_mode=pl.Buffered(1)),
            pl.BlockSpec((None, D_MODEL, CG_PAD), lambda b_, i: (layer, 0, 0)),
            pl.BlockSpec((8, LANES), lambda b_, i: (0, 0)),
        ],
        out_specs=out_specs,
        out_shape=out_shape,
        compiler_params=_params(2),
        name="inproj",
    )(x, nrm, w, wcg, gains2)


def _compress_kernel(t_ref, pos_ref, w1_ref, w2_ref, gain_ref, o_ref, ot_ref):
    t = t_ref[0, 0, 0]
    lo = (t + pos_ref[0, 0:1, :]).astype(BF16)
    hi = (t + pos_ref[0, 1:2, :]).astype(BF16)
    v = jnp.dot(lo, w1_ref[0, 0], preferred_element_type=F32)
    u = jnp.dot(hi, w1_ref[0, 1], preferred_element_type=F32)
    pre = v + pltpu.roll(u, NC_PAD - 1, 0)
    hcur = jax.nn.gelu(pre).astype(BF16)
    out = jnp.dot(hcur, w2_ref[0], preferred_element_type=F32)
    o64 = out[:, :HEAD_DIM]
    is_key = pl.program_id(1) == 0
    o_ref[0, 0, 0] = jnp.where(is_key, _rms(o64, gain_ref[...]), o64).astype(BF16)
    ot_ref[0, 0, 0] = out.T[:HEAD_DIM].astype(BF16)


def _compress(tkv16, pos, w1, w2p, gain):
    B = tkv16.shape[0]
    half = CMP_BLOCK // 2 * HEAD_DIM
    return pl.pallas_call(
        _compress_kernel,
        grid=(B, 2, KV_GROUPS),
        in_specs=[
            pl.BlockSpec((1, 1, 1, NC_PAD, half), lambda b, kv, g: (b, kv, g, 0, 0)),
            pl.BlockSpec((1, 2, half), lambda b, kv, g: (kv, 0, 0)),
            pl.BlockSpec((1, 2, half, CMP_HIDDEN), lambda b, kv, g: (kv, 0, 0, 0)),
            pl.BlockSpec((1, CMP_HIDDEN, LANES), lambda b, kv, g: (kv, 0, 0)),
            pl.BlockSpec((1, HEAD_DIM), lambda b, kv, g: (0, 0)),
        ],
        out_specs=(pl.BlockSpec((1, 1, 1, NC_PAD, HEAD_DIM), lambda b, kv, g: (b, kv, g, 0, 0)),
                   pl.BlockSpec((1, 1, 1, HEAD_DIM, NC_PAD), lambda b, kv, g: (b, kv, g, 0, 0))),
        out_shape=(jax.ShapeDtypeStruct((B, 2, KV_GROUPS, NC_PAD, HEAD_DIM), BF16),
                   jax.ShapeDtypeStruct((B, 2, KV_GROUPS, HEAD_DIM, NC_PAD), BF16)),
        compiler_params=_params(3),
        name="compress",
    )(tkv16, pos, w1, w2p, gain)


def _cmp_attn_kernel(qt_ref, kc_ref, vct_ref, ovl_ref, o_ref, sel_ref, score_ref, rank_ref, *, T, n_cmp):
    R = GROUP
    I = pl.program_id(2)
    j_idx = lax.broadcasted_iota(jnp.int32, (N_SEL_BLK, T), 0)
    tq = I * T + lax.broadcasted_iota(jnp.int32, (N_SEL_BLK, T), 1)
    cur = tq // SEL_BLOCK
    valid = j_idx * SEL_BLOCK <= tq
    forced = (j_idx == 0) | (j_idx == cur) | (j_idx == cur - 1)

    def attend(rows):
        kc = kc_ref[0, 0, 0, 0:rows, :]
        vct = vct_ref[0, 0, 0, :, 0:rows]
        c_idx = lax.broadcasted_iota(jnp.int32, (rows, T), 0)
        t_pos = I * T + lax.broadcasted_iota(jnp.int32, (rows, T), 1)
        cmask = (c_idx * CMP_STRIDE + (CMP_BLOCK - 1) <= t_pos) & (c_idx < n_cmp)
        scores = [jnp.dot(kc, qt_ref[0, 0, r, 0], preferred_element_type=F32) for r in range(R)]
        probs = []
        for r in range(R):
            s = jnp.where(cmask, scores[r], NEG_INF)
            m = jnp.max(s, axis=0, keepdims=True)
            e = jnp.where(cmask, jnp.exp2(s - m), 0.0)
            l = jnp.sum(e, axis=0, keepdims=True)
            probs.append(e * (1.0 / jnp.where(l > 0.0, l, 1.0)))
        outs = [jnp.dot(vct, p.astype(BF16), preferred_element_type=F32) for p in probs]
        o_ref[0] = jnp.concatenate(outs, axis=0).T
        psum = (probs[0] + probs[1]) + (probs[2] + probs[3])
        p_hi = psum.astype(BF16)
        p_lo = (psum - p_hi.astype(F32)).astype(BF16)
        ovl = ovl_ref[:, 0:rows]
        imp = jnp.dot(ovl, p_hi, preferred_element_type=F32) + jnp.dot(ovl, p_lo, preferred_element_type=F32)
        score_ref[...] = jnp.where(valid, imp + jnp.where(forced, FORCE_BONUS, 0.0), NEG_INF)

    chunk = NC_PAD // 4
    vis_per_tile = T // CMP_STRIDE
    for v in range(4):
        lo = -(-(v * chunk) // vis_per_tile)
        hi = -(-((v + 1) * chunk) // vis_per_tile)

        @pl.when((I >= lo) & (I < hi))
        def _(v=v):
            attend((v + 1) * chunk)

    rank_ref[...] = jnp.zeros(rank_ref.shape, F32)
    n_live = (I + 1) * (T // SEL_BLOCK)
    sub = 8
    j_loc = lax.broadcasted_iota(jnp.int32, (sub, T), 0)
    for c in range(N_SEL_BLK // sub):
        @pl.when(c * sub < n_live)
        def _(c=c):
            n_grp = N_SEL_BLK // sub
            grps = [score_ref[g * sub:(g + 1) * sub, :] for g in range(n_grp)]
            cnts = [rank_ref[g * sub:(g + 1) * sub, :] for g in range(n_grp)]
            for jp in range(c * sub, (c + 1) * sub):
                row = jnp.broadcast_to(score_ref[jp:jp + 1, :], (sub, T))
                for g in range(n_grp):
                    if g * sub > jp:
                        beats = row >= grps[g]
                    elif (g + 1) * sub - 1 < jp:
                        beats = row > grps[g]
                    else:
                        beats = (row > grps[g]) | ((row == grps[g]) & (j_loc + g * sub > jp))
                    cnts[g] = cnts[g] + jnp.where(beats, 1.0, 0.0)
            for g in range(n_grp):
                rank_ref[g * sub:(g + 1) * sub, :] = cnts[g]
    blk = T // SEL_BLOCK
    for j in range(N_SEL_BLK // blk):
        keep = rank_ref[j * blk:(j + 1) * blk, :] < N_SELECT
        slab = jnp.concatenate([jnp.where(keep, 0.0, NEG_INF), jnp.zeros((MASK_ROWS - blk, T), F32)], axis=0)
        sel_ref[0, 0, 0, j] = slab.astype(BF16)


def _cmp_attn(qct, kc, vct, ovl, n_cmp):
    B, G, R, NQ, _, T = qct.shape
    S = NQ * T
    return pl.pallas_call(
        functools.partial(_cmp_attn_kernel, T=T, n_cmp=n_cmp),
        grid=(B, G, S // T),
        in_specs=[
            pl.BlockSpec((1, 1, R, 1, HEAD_DIM, T), lambda b, g, i: (b, g, 0, i, 0, 0)),
            pl.BlockSpec((1, 1, 1, NC_PAD, HEAD_DIM), lambda b, g, i: (b, 0, g, 0, 0)),
            pl.BlockSpec((1, 1, 1, HEAD_DIM, NC_PAD), lambda b, g, i: (b, 1, g, 0, 0)),
            pl.BlockSpec((N_SEL_BLK, NC_PAD), lambda b, g, i: (0, 0)),
        ],
        out_specs=(
            pl.BlockSpec((1, T, R * HEAD_DIM), lambda b, g, i: (b, i, g)),
            pl.BlockSpec((1, 1, 1, NQ, MASK_ROWS, T), lambda b, g, i: (b, g, i, 0, 0, 0)),
        ),
        out_shape=(
            jax.ShapeDtypeStruct((B, S, G * R * HEAD_DIM), F32),
            jax.ShapeDtypeStruct((B, G, NQ, NQ, MASK_ROWS, T), BF16),
        ),
        scratch_shapes=[pltpu.VMEM((N_SEL_BLK, T), F32), pltpu.VMEM((N_SEL_BLK, T), F32)],
        compiler_params=_params(3),
        name="cmp_attn",
    )(qct, kc, vct, ovl)


def _flash_kernel(*refs, R, T, NQ, Dv, nrel, k_per_r, shared_bias, diff, has_sel, early_bias, lam_init):
    it = iter(refs)
    qt_ref, k_ref, vt_ref, bt_ref = (next(it) for _ in range(4))
    sel_ref = next(it) if has_sel else None
    dl_ref = next(it) if diff else None
    sub_ref = next(it) if diff else None
    o_ref, m_ref, acc_ref, s_ref, p_ref, al_ref, tmax_ref = (next(it) for _ in range(7))

    m_ref[...] = jnp.full(m_ref.shape, NEG_INF, F32)
    acc_ref[...] = jnp.zeros(acc_ref.shape, F32)
    bias_r = (lambda r: 0) if shared_bias else (lambda r: r)
    if has_sel:
        pad_rows = jnp.zeros((k_ref.shape[-1] - HEAD_DIM - MASK_ROWS, T), BF16)
    n_steps = NQ * (NQ + 1) // 2
    N_S = SCORE_SLOTS
    assert N_S % 2 == 0 and n_steps % N_S == 0

    def score_stage(I, J, slot):
        off = pl.multiple_of(J * T, T)
        rel = jnp.minimum(I - J, nrel - 1)
        for r in range(R):
            kt = k_ref[0, 0, r, pl.ds(off, T), :] if k_per_r else k_ref[0, 0, pl.ds(off, T), :]
            w = qt_ref[0, 0, r, I]
            if has_sel:
                w = jnp.concatenate([w, sel_ref[0, 0, I, J], pad_rows], axis=0)
            s = jnp.dot(kt, w, preferred_element_type=F32)
            if early_bias:
                s = s + bt_ref[0, rel, bias_r(r)]
                tmax_ref[slot, r] = jnp.max(s, axis=0, keepdims=True)
            s_ref[slot, r] = s

    def softmax_stage(I, J, s_slot, slot):
        for r in range(R):
            m_prev = m_ref[I, r]
            if early_bias:
                s, tile_max = s_ref[s_slot, r], tmax_ref[s_slot, r]
            else:
                s = s_ref[s_slot, r] + bt_ref[0, jnp.minimum(I - J, nrel - 1), bias_r(r)]
                tile_max = jnp.max(s, axis=0, keepdims=True)
            m_new = jnp.maximum(m_prev, tile_max)
            al_ref[slot, r] = jnp.exp2(m_prev - m_new)
            p_ref[slot, r] = jnp.exp2(s - m_new).astype(BF16)
            m_ref[I, r] = m_new

    def value_stage(J, slot):
        vt = vt_ref[0, 0, J]
        return [jnp.dot(vt, p_ref[slot, r], preferred_element_type=F32) for r in range(R)]

    def accumulate(I, slot, pvs):
        for r in range(R):
            acc_ref[I, r] = al_ref[slot, r] * acc_ref[I, r] + pvs[r]

    def advance(I, J):
        last = J == I
        return jnp.where(last, I + 1, I), jnp.where(last, 0, J + 1)

    def step(cur, prev, t):
        other = 1 - t % 2
        nxt = advance(*cur)
        pvs = value_stage(prev[1], other)
        score_stage(jnp.minimum(nxt[0], NQ - 1), nxt[1], (t + 1) % N_S)
        softmax_stage(cur[0], cur[1], t % N_S, t % 2)
        accumulate(prev[0], other, pvs)
        return nxt

    def trip(u, carry):
        cur, prev = carry[:2], carry[2:]
        for t in range(N_S):
            cur, prev = step(cur, prev, t), cur
        return (*cur, *prev)

    zero = jnp.int32(0)
    score_stage(zero, zero, 0)
    p_ref[1] = jnp.zeros(p_ref.shape[1:], BF16)
    al_ref[1] = jnp.ones(al_ref.shape[1:], F32)
    lax.fori_loop(0, n_steps // N_S, trip, (zero, zero, zero, zero))
    accumulate(NQ - 1, 1, value_stage(NQ - 1, 1))

    if diff:
        dl = dl_ref[...]
        lam = (jnp.exp(jnp.sum(dl[0:1] * dl[1:2], keepdims=True))
               - jnp.exp(jnp.sum(dl[2:3] * dl[3:4], keepdims=True)) + lam_init)

    def finish(I, carry):
        outs = []
        for r in range(R):
            acc = acc_ref[I, r]
            outs.append(acc[:Dv] / acc[Dv:Dv + 1])
        rows = pl.ds(pl.multiple_of(I * T, T), T)
        if diff:
            o = (outs[0] - lam * outs[1]).T
            o_ref[0, rows, :] = (_rms(o, sub_ref[...]) * (1.0 - lam_init)).astype(o_ref.dtype)
        else:
            o_ref[0, rows, :] = jnp.concatenate(outs, axis=0).T
        return carry

    lax.fori_loop(0, NQ, finish, 0)


def _flash_attn(qt, k, vt, bias_tab, *, sel=None, dl=None, subln=None, lam_init=0.0, name="attn"):
    B, G, R, NQ, _, T = qt.shape
    S = NQ * T
    Dv = vt.shape[-2] - ONES_ROWS
    nrel, bias_heads = bias_tab.shape[1], bias_tab.shape[2]
    k_per_r = k.ndim == 5
    has_sel = sel is not None
    diff = dl is not None
    ins = [qt, k, vt, bias_tab]
    in_specs = [
        pl.BlockSpec((1, 1, R, NQ, HEAD_DIM, T), lambda b, g: (b, g, 0, 0, 0, 0)),
        (pl.BlockSpec((1, 1, R, S, HEAD_DIM), lambda b, g: (b, g, 0, 0, 0)) if k_per_r
         else pl.BlockSpec((1, 1, S, k.shape[-1]), lambda b, g: (b, g, 0, 0))),
        pl.BlockSpec((1, 1, NQ, Dv + ONES_ROWS, T), lambda b, g: (b, g, 0, 0, 0)),
        pl.BlockSpec((1, nrel, bias_heads, T, T), lambda b, g: (g, 0, 0, 0, 0)),
    ]
    if has_sel:
        ins.append(sel)
        in_specs.append(pl.BlockSpec((1, 1, NQ, NQ, MASK_ROWS, T), lambda b, g: (b, g, 0, 0, 0, 0)))
    if diff:
        ins += [dl, subln]
        in_specs += [pl.BlockSpec((4, HEAD_DIM), lambda b, g: (0, 0)),
                     pl.BlockSpec((1, A_VDIM), lambda b, g: (0, 0))]
    out_w = Dv if diff else R * Dv
    kern = functools.partial(_flash_kernel, R=R, T=T, NQ=NQ, Dv=Dv, nrel=nrel, k_per_r=k_per_r,
                             shared_bias=bias_heads == 1, diff=diff, has_sel=has_sel, early_bias=not diff,
                             lam_init=lam_init)
    return pl.pallas_call(
        kern,
        grid=(B, G),
        in_specs=in_specs,
        out_specs=pl.BlockSpec((1, S, out_w), lambda b, g: (b, 0, g)),
        out_shape=jax.ShapeDtypeStruct((B, S, G * out_w), BF16 if diff else F32),
        scratch_shapes=[pltpu.VMEM((NQ, R, 1, T), F32), pltpu.VMEM((NQ, R, Dv + ONES_ROWS, T), F32),
                        pltpu.VMEM((SCORE_SLOTS, R, T, T), F32), pltpu.VMEM((2, R, T, T), BF16),
                        pltpu.VMEM((2, R, 1, T), F32), pltpu.VMEM((SCORE_SLOTS, R, 1, T), F32)],
        compiler_params=_params(2),
        name=name,
    )(*ins)


def _window_kernel(*refs, R, T, NK, TV, Dv, n_sub, has_sink):
    it = iter(refs)
    qt_ref, k_ref, vt_ref, bt_ref = (next(it) for _ in range(4))
    sink_ref = next(it) if has_sink else None
    o_ref = next(it)
    heads = [(t, r) for t in range(n_sub) for r in range(R)]
    tile = [pl.program_id(2) * n_sub + t for t in range(n_sub)]
    entry = [jnp.minimum(I, -(-(NK - T) // T)) for I in tile]
    first_key = [pl.multiple_of(I * T - jnp.minimum(e * T, NK - T), TV) for I, e in zip(tile, entry)]
    keys = [k_ref[0, 0, pl.ds(fk, NK), :] for fk in first_key]
    scores = {(t, r): jnp.dot(keys[t], qt_ref[0, 0, r, t], preferred_element_type=F32) for t, r in heads}
    probs, maxes = {}, {}
    for t, r in heads:
        s = scores[t, r] + bt_ref[0, entry[t], r]
        maxes[t, r] = jnp.max(s, axis=0, keepdims=True)
        probs[t, r] = jnp.exp2(s - maxes[t, r]).astype(BF16)
    accs = {}
    for t, r in heads:
        acc = None
        for c in range(NK // TV):
            part = jnp.dot(vt_ref[0, 0, first_key[t] // TV + c], probs[t, r][c * TV:(c + 1) * TV],
                           preferred_element_type=F32)
            acc = part if acc is None else acc + part
        accs[t, r] = acc
    for t in range(n_sub):
        outs = []
        for r in range(R):
            num, l = accs[t, r][:Dv], accs[t, r][Dv:Dv + 1]
            if has_sink:
                sk = sink_ref[0, r] * LOG2E
                m_f = jnp.maximum(maxes[t, r], sk)
                w = jnp.exp2(maxes[t, r] - m_f)
                outs.append(num * w / (l * w + jnp.exp2(sk - m_f)))
            else:
                outs.append(num / l)
        o_ref[0, t * T:(t + 1) * T, :] = jnp.concatenate(outs, axis=0).T.astype(o_ref.dtype)


def _window_attn(qt, k, vt, bias_tab, *, n_sub=2, sink=None, out_dtype=F32, name="attn_window"):
    B, G, R, NQ, _, T = qt.shape
    S = NQ * T
    TV = vt.shape[-1]
    Dv = vt.shape[-2] - ONES_ROWS
    entries, NK = bias_tab.shape[1], bias_tab.shape[3]
    has_sink = sink is not None
    ins = [qt, k, vt, bias_tab]
    in_specs = [
        pl.BlockSpec((1, 1, R, n_sub, HEAD_DIM, T), lambda b, g, i: (b, g, 0, i, 0, 0)),
        pl.BlockSpec((1, 1, S, HEAD_DIM), lambda b, g, i: (b, g, 0, 0)),
        pl.BlockSpec((1, 1, S // TV, Dv + ONES_ROWS, TV), lambda b, g, i: (b, g, 0, 0, 0)),
        pl.BlockSpec((1, entries, R, NK, T), lambda b, g, i: (g, 0, 0, 0, 0)),
    ]
    if has_sink:
        ins.append(sink)
        in_specs.append(pl.BlockSpec((1, R, 1, 1), lambda b, g, i: (g, 0, 0, 0)))
    kern = functools.partial(_window_kernel, R=R, T=T, NK=NK, TV=TV, Dv=Dv, n_sub=n_sub, has_sink=has_sink)
    return pl.pallas_call(
        kern,
        grid=(B, G, NQ // n_sub),
        in_specs=in_specs,
        out_specs=pl.BlockSpec((1, n_sub * T, R * Dv), lambda b, g, i: (b, i, g)),
        out_shape=jax.ShapeDtypeStruct((B, S, G * R * Dv), out_dtype),
        compiler_params=_params(3),
        name=name,
    )(*ins)


def _bucket_np(dist):
    n = np.maximum(dist, 0)
    max_exact = NUM_BUCKETS // 2
    nf = np.maximum(n, 1).astype(np.float32)
    large = max_exact + (np.log(nf / max_exact) / math.log(MAX_DISTANCE / max_exact)
                         * (NUM_BUCKETS - max_exact)).astype(np.int32)
    large = np.minimum(large, NUM_BUCKETS - 1)
    return np.where(n < max_exact, n, large)


def _bucket_starts():
    b = _bucket_np(np.arange(4 * MAX_DISTANCE))
    assert (np.diff(b) >= 0).all() and b[-1] == NUM_BUCKETS - 1
    return [int(np.argmax(b >= i)) for i in range(NUM_BUCKETS)]


def _bias_kernel(tab_ref, o_ref, *, T, rows, nrel, window, back_max, starts):
    h = pl.program_id(0)
    key = lax.broadcasted_iota(jnp.int32, (rows, T), 0)
    qry = lax.broadcasted_iota(jnp.int32, (rows, T), 1)
    for rel in range(nrel):
        dist = min(rel * T, back_max) + qry - key
        val = jnp.full((rows, T), tab_ref[0, h] * LOG2E, F32)
        for b in range(1, NUM_BUCKETS):
            val = jnp.where(dist >= starts[b], tab_ref[b, h] * LOG2E, val)
        allowed = dist >= 0
        if window is not None:
            allowed = allowed & (dist < window)
        o_ref[0, rel, 0] = jnp.where(allowed, val, NEG_INF)


def _bias_tiles(tab, T, nrel, window, R, rows=None):
    H = tab.shape[1]
    back_max = nrel * T if rows is None else rows - T
    rows = T if rows is None else rows
    return pl.pallas_call(
        functools.partial(_bias_kernel, T=T, rows=rows, nrel=nrel, window=window, back_max=back_max,
                          starts=_bucket_starts()),
        grid=(H,),
        in_specs=[pl.BlockSpec(memory_space=pltpu.SMEM)],
        out_specs=pl.BlockSpec((1, nrel, 1, rows, T), lambda h: (h // R, 0, h % R, 0, 0)),
        out_shape=jax.ShapeDtypeStruct((H // R, nrel, R, rows, T), F32),
        compiler_params=_params(1),
        name="bias_tiles",
    )(tab)


def _merge_kernel(x_ref, nrm_ref, oa_ref, ob_ref, oc_ref, os_ref, ow_ref, gc_ref, ex_ref, wmg_ref, wb_ref, wo_ref,
                  o_ref):
    x = x_ref[...]
    hb = _rms(x, nrm_ref[...]).astype(BF16)
    gc = gc_ref[...]
    g1 = gc.astype(BF16)
    r1 = gc - g1.astype(F32)
    g2 = r1.astype(BF16)
    g3 = (r1 - g2.astype(F32)).astype(BF16)
    ex = ex_ref[...]
    gx = (jnp.dot(g1, ex, preferred_element_type=F32) + jnp.dot(g2, ex, preferred_element_type=F32)
          + jnp.dot(g3, ex, preferred_element_type=F32))
    oc = (gx[:, 0:MIX_WIDTH] * oc_ref[...] + gx[:, MIX_WIDTH:2 * MIX_WIDTH] * os_ref[...]
          + gx[:, 2 * MIX_WIDTH:3 * MIX_WIDTH] * ow_ref[...])
    z = None
    for n, br in enumerate((oa_ref[...], ob_ref[...], oc)):
        y = jnp.dot(br.astype(BF16), wb_ref[n], preferred_element_type=F32)
        gate = jax.nn.sigmoid(jnp.dot(hb, wmg_ref[:, n * D_MODEL:(n + 1) * D_MODEL], preferred_element_type=F32))
        t = gate * y
        z = t if z is None else z + t
    o_ref[...] = x + jnp.dot(z.astype(BF16), wo_ref[...], preferred_element_type=F32)


def _merge(x2, nrm, oa, ob, oc, osel, ow, gc, expand_g, wmg, wb, wo, layer, tm=512):
    Tn = x2.shape[0]
    row = lambda d: pl.BlockSpec((tm, d), lambda i: (i, 0))
    const = lambda shape: pl.BlockSpec(shape, lambda i: (0,) * len(shape), pipeline_mode=pl.Buffered(1))
    weight = lambda shape: pl.BlockSpec((None,) + shape, lambda i: (layer,) + (0,) * len(shape),
                                        pipeline_mode=pl.Buffered(1))
    return pl.pallas_call(
        _merge_kernel,
        grid=(Tn // tm,),
        in_specs=[row(D_MODEL), const((1, D_MODEL)),
                  row(MIX_WIDTH), row(MIX_WIDTH), row(MIX_WIDTH), row(MIX_WIDTH), row(MIX_WIDTH),
                  row(CG_PAD),
                  const((CG_PAD, 3 * MIX_WIDTH)),
                  weight((D_MODEL, 3 * D_MODEL)),
                  weight((3, MIX_WIDTH, D_MODEL)),
                  weight((D_MODEL, D_MODEL))],
        out_specs=row(D_MODEL),
        out_shape=jax.ShapeDtypeStruct((Tn, D_MODEL), F32),
        compiler_params=_params(1),
        name="merge",
    )(x2, nrm, oa, ob, oc, osel, ow, gc, expand_g, wmg, wb, wo)


def _mlp_kernel(x_ref, nrm_ref, wu_ref, wd_ref, o_ref, h_ref, acc_ref):
    f = pl.program_id(1)

    @pl.when(f == 0)
    def _():
        h_ref[...] = _rms(x_ref[...], nrm_ref[...]).astype(BF16)
        acc_ref[...] = jnp.zeros(acc_ref.shape, F32)

    u = jnp.dot(h_ref[...], wu_ref[...], preferred_element_type=F32)
    u = jnp.square(jnp.maximum(u, 0.0)).astype(BF16)
    acc_ref[...] += jnp.dot(u, wd_ref[...], preferred_element_type=F32)

    @pl.when(f == pl.num_programs(1) - 1)
    def _():
        o_ref[...] = x_ref[...] + acc_ref[...]


def _mlp(x2, nrm, wu, wd, layer, tm=1024, tf=1024):
    Tn = x2.shape[0]
    return pl.pallas_call(
        _mlp_kernel,
        grid=(Tn // tm, D_FF // tf),
        in_specs=[pl.BlockSpec((tm, D_MODEL), lambda i, f: (i, 0)),
                  pl.BlockSpec((1, D_MODEL), lambda i, f: (0, 0)),
                  pl.BlockSpec((None, D_MODEL, tf), lambda i, f: (layer, 0, f)),
                  pl.BlockSpec((None, tf, D_MODEL), lambda i, f: (layer, f, 0))],
        out_specs=pl.BlockSpec((tm, D_MODEL), lambda i, f: (i, 0)),
        out_shape=jax.ShapeDtypeStruct((Tn, D_MODEL), F32),
        scratch_shapes=[pltpu.VMEM((tm, D_MODEL), BF16), pltpu.VMEM((tm, D_MODEL), F32)],
        compiler_params=_params(2),
        name="mlp",
    )(x2, nrm, wu, wd)


def _overlap(n_cmp):
    c_start = np.arange(NC_PAD) * CMP_STRIDE
    j_start = np.arange(N_SEL_BLK) * SEL_BLOCK
    ov = ((c_start[None, :] < j_start[:, None] + SEL_BLOCK) & (c_start[None, :] + CMP_BLOCK > j_start[:, None])
          & (np.arange(NC_PAD)[None, :] < n_cmp))
    return jnp.asarray(ov.astype(np.float32), BF16)


def _gate_expand():
    e = np.zeros((CG_PAD, 3 * MIX_WIDTH), np.float32)
    for h in range(C_HEADS):
        for j in range(3):
            e[h * 3 + j, j * MIX_WIDTH + h * HEAD_DIM:j * MIX_WIDTH + (h + 1) * HEAD_DIM] = 1.0
    return jnp.asarray(e, BF16)


def kernel(x, w_in, qk_gain, diff_lambda, diff_subln, sinks, cmp_pos, cmp_w1, cmp_w2,
           w_branch, w_out, norm_mix, norm_mlp, w_up, w_down, rel_bias):
    B, S, _ = x.shape
    depth = w_in.shape[0]
    n_cmp = (S - CMP_BLOCK) // CMP_STRIDE + 1
    assert S % T_ROW == 0 and S // CMP_STRIDE == NC_PAD and S // SEL_BLOCK == N_SEL_BLK
    half = CMP_BLOCK // 2 * HEAD_DIM

    w_heads = w_in[:, :, :C_CG].astype(BF16)
    w_cg = jnp.pad(w_in[:, :, C_CG:C_CG + N_CG], ((0, 0), (0, 0), (0, CG_PAD - N_CG))).astype(BF16)
    w_mg = w_in[:, :, C_CG + N_CG:].astype(BF16)
    gains2 = jnp.concatenate([qk_gain, qk_gain], axis=-1)
    w1 = cmp_w1.astype(BF16).reshape(depth, 2, 2, half, CMP_HIDDEN)
    w2p = jnp.pad(cmp_w2, ((0, 0), (0, 0), (0, 0), (0, LANES - HEAD_DIM))).astype(BF16)
    pos = cmp_pos.reshape(depth, 2, 2, half)
    wb = w_branch.astype(BF16)
    wo = w_out.astype(BF16)
    wu = w_up.astype(BF16)
    wd = w_down.astype(BF16)

    bias_a = rel_bias[:, :A_HEADS]
    bias_b = rel_bias[:, A_HEADS:A_HEADS + B_HEADS]
    bias_c = rel_bias[:, A_HEADS + B_HEADS:]
    bt_a = _bias_tiles(bias_a, T_ROW, 3, None, 1)
    bt_b = _bias_tiles(bias_b, T_SWA, -(-B_WINDOW // T_SWA) + 1, B_WINDOW, GROUP, rows=B_WINDOW + T_SWA)
    bt_sel = _bias_tiles(bias_c, T_ROW, 3, None, GROUP)
    bt_win = _bias_tiles(bias_c, T_ROW, C_WINDOW // T_ROW + 1, C_WINDOW, GROUP, rows=C_WINDOW + T_ROW)
    ovl = _overlap(n_cmp)
    expand_g = _gate_expand()

    for layer in range(depth):
        lam_init = 0.8 - 0.6 * math.exp(-0.3 * layer)
        (qat, ka, vat, qbt, kb, vbt, qct, tkv, ksel, vselt, kwin, vwint, gc) = _inproj(
            x, norm_mix[layer][None], w_heads, w_cg, gains2[layer], layer)
        oa = _flash_attn(qat, ka, vat, bt_a, dl=diff_lambda[layer],
                   subln=diff_subln[layer][None], lam_init=lam_init, name="attn_diff")
        ob = _window_attn(qbt, kb, vbt, bt_b, n_sub=4, out_dtype=BF16,
                          sink=sinks[layer].reshape(KV_GROUPS, GROUP, 1, 1), name="attn_swa")
        kc, vct = _compress(tkv.reshape(B, 2, KV_GROUPS, NC_PAD, half), pos[layer],
                            w1[layer], w2p[layer], qk_gain[layer][5:6])
        ocmp, sel = _cmp_attn(qct, kc, vct, ovl, n_cmp)
        osel = _flash_attn(qct, ksel, vselt, bt_sel, sel=sel, name="attn_sel")
        owin = _window_attn(qct, kwin, vwint, bt_win, n_sub=4, name="attn_win")
        f2 = lambda a: a.reshape(B * S, a.shape[-1])
        x2 = _merge(f2(x), norm_mix[layer][None], f2(oa), f2(ob), f2(ocmp), f2(osel), f2(owin), f2(gc),
                    expand_g, w_mg, wb, wo, layer)
        x2 = _mlp(x2, norm_mlp[layer][None], wu, wd, layer)
        x = x2.reshape(B, S, D_MODEL)
    return x
```

```python
import functools
import math

import numpy as np
import jax
import jax.numpy as jnp
from jax import lax
from jax.experimental import pallas as pl
from jax.experimental.pallas import tpu as pltpu

F32 = jnp.float32
BF16 = jnp.bfloat16

D_MODEL = 1024
HEAD_DIM = 64
A_HEADS = 4
A_VDIM = 128
B_HEADS = 8
B_WINDOW = 128
C_HEADS = 8
KV_GROUPS = 2
GROUP = 4
CMP_BLOCK = 32
CMP_STRIDE = 16
CMP_HIDDEN = 256
SEL_BLOCK = 64
N_SELECT = 16
C_WINDOW = 512
MIX_WIDTH = 512
D_FF = 4096
NUM_BUCKETS = 32
MAX_DISTANCE = 128
NEG_INF = -1e30
FORCE_BONUS = 1e4
EPS = 1e-6
LOG2E = 1.4426950408889634
QK_SCALE = HEAD_DIM ** -0.5 * LOG2E

C_AQ, C_AK, C_AV, C_BQ, C_BKV, C_CQ, C_CKV, C_CG = 0, 512, 1024, 1536, 2048, 2304, 2816, 3584
N_CG = C_HEADS * 3
LANES = 128
CG_PAD = LANES
NC_PAD = 256
N_SEL_BLK = 64
ONES_ROWS = 16
MASK_ROWS = 16

SCORE_SLOTS = 8

T_IN = 512
T_ROW = 256
T_SWA = 256
TV_SWA = 128

VMEM_LIMIT = 48 * 1024 * 1024


def _rms(x, gain):
    ms = jnp.mean(x * x, axis=-1, keepdims=True)
    return (x * lax.rsqrt(ms + EPS)) * gain


def _params(n_axes):
    return pltpu.CompilerParams(dimension_semantics=("arbitrary",) * n_axes,
                                vmem_limit_bytes=VMEM_LIMIT)


def _inproj_kernel(x_ref, nrm_ref, w_ref, wcg_ref, gains_ref,
                   qa_ref, ka_ref, va_ref, qb_ref, kb_ref, vb_ref, qc_ref,
                   tkv_ref, ksel_ref, vsel_ref, kwin_ref, vwin_ref, gc_ref):
    tm = x_ref.shape[1]
    hb = _rms(x_ref[0], nrm_ref[...]).astype(BF16)
    lane = lax.broadcasted_iota(jnp.int32, (tm, LANES), 1)
    first = lane < HEAD_DIM

    def ones_pad(width):
        return jnp.where(lax.broadcasted_iota(jnp.int32, (ONES_ROWS, width), 0) == 0, 1.0, 0.0).astype(BF16)

    def mm(c0, n):
        return jnp.dot(hb, w_ref[:, c0:c0 + n], preferred_element_type=F32)

    def norm_pair(y, gain_idx, scale=None):
        sq = y * y
        ms0 = jnp.sum(jnp.where(first, sq, 0.0), axis=-1, keepdims=True) * (1.0 / HEAD_DIM)
        ms1 = jnp.sum(jnp.where(first, 0.0, sq), axis=-1, keepdims=True) * (1.0 / HEAD_DIM)
        inv = jnp.where(first, lax.rsqrt(ms0 + EPS), lax.rsqrt(ms1 + EPS))
        out = (y * inv) * gains_ref[gain_idx:gain_idx + 1, :]
        return out if scale is None else out * scale

    def pairs(c0):
        r = mm(c0, 256)
        return r[:, :LANES], r[:, LANES:]

    def put_qt(ref, i0, i1, y):
        yt = y.T.astype(BF16)
        n_tiles = ref.shape[3]
        tq = tm // n_tiles
        for t in range(n_tiles):
            ref[0, i0, i1, t] = yt[:HEAD_DIM, tq * t:tq * (t + 1)]
            ref[0, i0, i1 + 1, t] = yt[HEAD_DIM:, tq * t:tq * (t + 1)]

    def put_vt(ref, g0, y, n_tiles):
        yt = y.T.astype(BF16)
        tk = tm // n_tiles
        for g in range(2):
            for t in range(n_tiles):
                ref[0, g0 + g, t, 0:HEAD_DIM, :] = yt[HEAD_DIM * g:HEAD_DIM * (g + 1), tk * t:tk * (t + 1)]
                ref[0, g0 + g, t, HEAD_DIM:HEAD_DIM + ONES_ROWS, :] = ones_pad(tk)

    for ch in range(2):
        for half, y in enumerate(pairs(C_AQ + 256 * ch)):
            put_qt(qa_ref, ch * 2 + half, 0, norm_pair(y, 0, QK_SCALE))
        for half, y in enumerate(pairs(C_AK + 256 * ch)):
            y = norm_pair(y, 1).astype(BF16)
            ka_ref[0, ch * 2 + half, 0] = y[:, :HEAD_DIM]
            ka_ref[0, ch * 2 + half, 1] = y[:, HEAD_DIM:]
        for half, y in enumerate(pairs(C_AV + 256 * ch)):
            yt = y.T.astype(BF16)
            for t in range(tm // T_ROW):
                va_ref[0, ch * 2 + half, t, 0:A_VDIM, :] = yt[:, T_ROW * t:T_ROW * (t + 1)]
                va_ref[0, ch * 2 + half, t, A_VDIM:A_VDIM + ONES_ROWS, :] = ones_pad(T_ROW)
        for half, y in enumerate(pairs(C_BQ + 256 * ch)):
            put_qt(qb_ref, ch, 2 * half, norm_pair(y, 2, QK_SCALE))
        for half, y in enumerate(pairs(C_CQ + 256 * ch)):
            put_qt(qc_ref, ch, 2 * half, norm_pair(y, 4, QK_SCALE))

    def put_k(ref, y):
        y = y.astype(BF16)
        ref[0, 0] = y[:, :HEAD_DIM]
        ref[0, 1] = y[:, HEAD_DIM:]

    yk, yv = pairs(C_BKV)
    put_k(kb_ref, norm_pair(yk, 3))
    put_vt(vb_ref, 0, yv, tm // TV_SWA)
    y0, y1 = pairs(C_CKV)
    for kv, y in enumerate((y0, y1)):
        tkv_ref[0, kv, 0] = y[:, :HEAD_DIM]
        tkv_ref[0, kv, 1] = y[:, HEAD_DIM:]
    yk, yv = pairs(C_CKV + 256)
    yk = norm_pair(yk, 6)
    row = lax.broadcasted_iota(jnp.int32, (tm, LANES), 0)
    hot = jnp.where(lane == HEAD_DIM + (row // SEL_BLOCK) % (T_ROW // SEL_BLOCK), 1.0, 0.0)
    ksel_ref[0, 0] = jnp.where(first, yk, hot).astype(BF16)
    ksel_ref[0, 1] = jnp.where(first, pltpu.roll(yk, HEAD_DIM, 1), hot).astype(BF16)
    put_vt(vsel_ref, 0, yv, tm // T_ROW)
    yk, yv = pairs(C_CKV + 512)
    put_k(kwin_ref, norm_pair(yk, 7))
    put_vt(vwin_ref, 0, yv, tm // T_ROW)

    gc_ref[0] = jax.nn.sigmoid(jnp.dot(hb, wcg_ref[...], preferred_element_type=F32))


def _inproj(x, nrm, w, wcg, gains2, layer):
    B, S, _ = x.shape
    tm = T_IN
    nt = S // tm
    sd = jax.ShapeDtypeStruct
    qt = lambda a, b, tq: sd((B, a, b, S // tq, HEAD_DIM, tq), BF16)
    kk = lambda n: sd((B, n, S, HEAD_DIM), BF16)
    vt = lambda n, dv, tk: sd((B, n, S // tk, dv + ONES_ROWS, tk), BF16)
    out_shape = (
        qt(A_HEADS, 2, T_ROW), sd((B, A_HEADS, 2, S, HEAD_DIM), BF16), vt(A_HEADS, A_VDIM, T_ROW),
        qt(KV_GROUPS, GROUP, T_SWA), kk(KV_GROUPS), vt(KV_GROUPS, HEAD_DIM, TV_SWA),
        qt(KV_GROUPS, GROUP, T_ROW),
        sd((B, 2, KV_GROUPS, S, HEAD_DIM), F32),
        sd((B, KV_GROUPS, S, LANES), BF16), vt(KV_GROUPS, HEAD_DIM, T_ROW),
        kk(KV_GROUPS), vt(KV_GROUPS, HEAD_DIM, T_ROW),
        sd((B, S, CG_PAD), F32),
    )
    s_qt = lambda a, b, tq: pl.BlockSpec((1, a, b, tm // tq, HEAD_DIM, tq), lambda b_, i: (b_, 0, 0, i, 0, 0))
    s_k = lambda n: pl.BlockSpec((1, n, tm, HEAD_DIM), lambda b_, i: (b_, 0, i, 0))
    s_k5 = lambda a, c: pl.BlockSpec((1, a, c, tm, HEAD_DIM), lambda b_, i: (b_, 0, 0, i, 0))
    s_vt = lambda n, dv, tk: pl.BlockSpec((1, n, tm // tk, dv + ONES_ROWS, tk), lambda b_, i: (b_, 0, i, 0, 0))
    row = lambda d: pl.BlockSpec((1, tm, d), lambda b_, i: (b_, i, 0))
    out_specs = (
        s_qt(A_HEADS, 2, T_ROW), s_k5(A_HEADS, 2), s_vt(A_HEADS, A_VDIM, T_ROW),
        s_qt(KV_GROUPS, GROUP, T_SWA), s_k(KV_GROUPS), s_vt(KV_GROUPS, HEAD_DIM, TV_SWA),
        s_qt(KV_GROUPS, GROUP, T_ROW), s_k5(2, KV_GROUPS),
        pl.BlockSpec((1, KV_GROUPS, tm, LANES), lambda b_, i: (b_, 0, i, 0)), s_vt(KV_GROUPS, HEAD_DIM, T_ROW),
        s_k(KV_GROUPS), s_vt(KV_GROUPS, HEAD_DIM, T_ROW),
        row(CG_PAD),
    )
    return pl.pallas_call(
        _inproj_kernel,
        grid=(B, nt),
        in_specs=[
            row(D_MODEL),
            pl.BlockSpec((1, D_MODEL), lambda b_, i: (0, 0)),
            pl.BlockSpec((None, D_MODEL, C_CG), lambda b_, i: (layer, 0, 0), pipeline_mode=pl.Buffered(1)),
            pl.BlockSpec((None, D_MODEL, CG_PAD), lambda b_, i: (layer, 0, 0)),
            pl.BlockSpec((8, LANES), lambda b_, i: (0, 0)),
        ],
        out_specs=out_specs,
        out_shape=out_shape,
        compiler_params=_params(2),
        name="inproj",
    )(x, nrm, w, wcg, gains2)


def _compress_kernel(t_ref, pos_ref, w1_ref, w2_ref, gain_ref, o_ref, ot_ref):
    t = t_ref[0, 0, 0]
    lo = (t + pos_ref[0, 0:1, :]).astype(BF16)
    hi = (t + pos_ref[0, 1:2, :]).astype(BF16)
    v = jnp.dot(lo, w1_ref[0, 0], preferred_element_type=F32)
    u = jnp.dot(hi, w1_ref[0, 1], preferred_element_type=F32)
    pre = v + pltpu.roll(u, NC_PAD - 1, 0)
    hcur = jax.nn.gelu(pre).astype(BF16)
    out = jnp.dot(hcur, w2_ref[0], preferred_element_type=F32)
    o64 = out[:, :HEAD_DIM]
    is_key = pl.program_id(1) == 0
    o_ref[0, 0, 0] = jnp.where(is_key, _rms(o64, gain_ref[...]), o64).astype(BF16)
    ot_ref[0, 0, 0] = out.T[:HEAD_DIM].astype(BF16)


def _compress(tkv16, pos, w1, w2p, gain):
    B = tkv16.shape[0]
    half = CMP_BLOCK // 2 * HEAD_DIM
    return pl.pallas_call(
        _compress_kernel,
        grid=(B, 2, KV_GROUPS),
        in_specs=[
            pl.BlockSpec((1, 1, 1, NC_PAD, half), lambda b, kv, g: (b, kv, g, 0, 0)),
            pl.BlockSpec((1, 2, half), lambda b, kv, g: (kv, 0, 0)),
            pl.BlockSpec((1, 2, half, CMP_HIDDEN), lambda b, kv, g: (kv, 0, 0, 0)),
            pl.BlockSpec((1, CMP_HIDDEN, LANES), lambda b, kv, g: (kv, 0, 0)),
            pl.BlockSpec((1, HEAD_DIM), lambda b, kv, g: (0, 0)),
        ],
        out_specs=(pl.BlockSpec((1, 1, 1, NC_PAD, HEAD_DIM), lambda b, kv, g: (b, kv, g, 0, 0)),
                   pl.BlockSpec((1, 1, 1, HEAD_DIM, NC_PAD), lambda b, kv, g: (b, kv, g, 0, 0))),
        out_shape=(jax.ShapeDtypeStruct((B, 2, KV_GROUPS, NC_PAD, HEAD_DIM), BF16),
                   jax.ShapeDtypeStruct((B, 2, KV_GROUPS, HEAD_DIM, NC_PAD), BF16)),
        compiler_params=_params(3),
        name="compress",
    )(tkv16, pos, w1, w2p, gain)


def _cmp_attn_kernel(qt_ref, kc_ref, vct_ref, ovl_ref, o_ref, sel_ref, score_ref, rank_ref, *, T, n_cmp):
    R = GROUP
    I = pl.program_id(2)
    j_idx = lax.broadcasted_iota(jnp.int32, (N_SEL_BLK, T), 0)
    tq = I * T + lax.broadcasted_iota(jnp.int32, (N_SEL_BLK, T), 1)
    cur = tq // SEL_BLOCK
    valid = j_idx * SEL_BLOCK <= tq
    forced = (j_idx == 0) | (j_idx == cur) | (j_idx == cur - 1)

    def attend(rows):
        kc = kc_ref[0, 0, 0, 0:rows, :]
        vct = vct_ref[0, 0, 0, :, 0:rows]
        c_idx = lax.broadcasted_iota(jnp.int32, (rows, T), 0)
        t_pos = I * T + lax.broadcasted_iota(jnp.int32, (rows, T), 1)
        cmask = (c_idx * CMP_STRIDE + (CMP_BLOCK - 1) <= t_pos) & (c_idx < n_cmp)
        scores = [jnp.dot(kc, qt_ref[0, 0, r, 0], preferred_element_type=F32) for r in range(R)]
        probs = []
        for r in range(R):
            s = jnp.where(cmask, scores[r], NEG_INF)
            m = jnp.max(s, axis=0, keepdims=True)
            e = jnp.where(cmask, jnp.exp2(s - m), 0.0)
            l = jnp.sum(e, axis=0, keepdims=True)
            probs.append(e * (1.0 / jnp.where(l > 0.0, l, 1.0)))
        outs = [jnp.dot(vct, p.astype(BF16), preferred_element_type=F32) for p in probs]
        o_ref[0] = jnp.concatenate(outs, axis=0).T
        psum = (probs[0] + probs[1]) + (probs[2] + probs[3])
        p_hi = psum.astype(BF16)
        p_lo = (psum - p_hi.astype(F32)).astype(BF16)
        ovl = ovl_ref[:, 0:rows]
        imp = jnp.dot(ovl, p_hi, preferred_element_type=F32) + jnp.dot(ovl, p_lo, preferred_element_type=F32)
        score_ref[...] = jnp.where(valid, imp + jnp.where(forced, FORCE_BONUS, 0.0), NEG_INF)

    chunk = NC_PAD // 4
    vis_per_tile = T // CMP_STRIDE
    for v in range(4):
        lo = -(-(v * chunk) // vis_per_tile)
        hi = -(-((v + 1) * chunk) // vis_per_tile)

        @pl.when((I >= lo) & (I < hi))
        def _(v=v):
            attend((v + 1) * chunk)

    rank_ref[...] = jnp.zeros(rank_ref.shape, F32)
    n_live = (I + 1) * (T // SEL_BLOCK)
    sub = 8
    j_loc = lax.broadcasted_iota(jnp.int32, (sub, T), 0)
    for c in range(N_SEL_BLK // sub):
        @pl.when(c * sub < n_live)
        def _(c=c):
            n_grp = N_SEL_BLK // sub
            grps = [score_ref[g * sub:(g + 1) * sub, :] for g in range(n_grp)]
            cnts = [rank_ref[g * sub:(g + 1) * sub, :] for g in range(n_grp)]
            for jp in range(c * sub, (c + 1) * sub):
                row = jnp.broadcast_to(score_ref[jp:jp + 1, :], (sub, T))
                for g in range(n_grp):
                    if g * sub > jp:
                        beats = row >= grps[g]
                    elif (g + 1) * sub - 1 < jp:
                        beats = row > grps[g]
                    else:
                        beats = (row > grps[g]) | ((row == grps[g]) & (j_loc + g * sub > jp))
                    cnts[g] = cnts[g] + jnp.where(beats, 1.0, 0.0)
            for g in range(n_grp):
                rank_ref[g * sub:(g + 1) * sub, :] = cnts[g]
    blk = T // SEL_BLOCK
    for j in range(N_SEL_BLK // blk):
        keep = rank_ref[j * blk:(j + 1) * blk, :] < N_SELECT
        slab = jnp.concatenate([jnp.where(keep, 0.0, NEG_INF), jnp.zeros((MASK_ROWS - blk, T), F32)], axis=0)
        sel_ref[0, 0, 0, j] = slab.astype(BF16)


def _cmp_attn(qct, kc, vct, ovl, n_cmp):
    B, G, R, NQ, _, T = qct.shape
    S = NQ * T
    return pl.pallas_call(
        functools.partial(_cmp_attn_kernel, T=T, n_cmp=n_cmp),
        grid=(B, G, S // T),
        in_specs=[
            pl.BlockSpec((1, 1, R, 1, HEAD_DIM, T), lambda b, g, i: (b, g, 0, i, 0, 0)),
            pl.BlockSpec((1, 1, 1, NC_PAD, HEAD_DIM), lambda b, g, i: (b, 0, g, 0, 0)),
            pl.BlockSpec((1, 1, 1, HEAD_DIM, NC_PAD), lambda b, g, i: (b, 1, g, 0, 0)),
            pl.BlockSpec((N_SEL_BLK, NC_PAD), lambda b, g, i: (0, 0)),
        ],
        out_specs=(
            pl.BlockSpec((1, T, R * HEAD_DIM), lambda b, g, i: (b, i, g)),
            pl.BlockSpec((1, 1, 1, NQ, MASK_ROWS, T), lambda b, g, i: (b, g, i, 0, 0, 0)),
        ),
        out_shape=(
            jax.ShapeDtypeStruct((B, S, G * R * HEAD_DIM), F32),
            jax.ShapeDtypeStruct((B, G, NQ, NQ, MASK_ROWS, T), BF16),
        ),
        scratch_shapes=[pltpu.VMEM((N_SEL_BLK, T), F32), pltpu.VMEM((N_SEL_BLK, T), F32)],
        compiler_params=_params(3),
        name="cmp_attn",
    )(qct, kc, vct, ovl)


def _flash_kernel(*refs, R, T, NQ, Dv, nrel, k_per_r, shared_bias, diff, has_sel, early_bias, lam_init):
    it = iter(refs)
    qt_ref, k_ref, vt_ref, bt_ref = (next(it) for _ in range(4))
    sel_ref = next(it) if has_sel else None
    dl_ref = next(it) if diff else None
    sub_ref = next(it) if diff else None
    o_ref, m_ref, acc_ref, s_ref, p_ref, al_ref, tmax_ref = (next(it) for _ in range(7))

    m_ref[...] = jnp.full(m_ref.shape, NEG_INF, F32)
    acc_ref[...] = jnp.zeros(acc_ref.shape, F32)
    bias_r = (lambda r: 0) if shared_bias else (lambda r: r)
    if has_sel:
        pad_rows = jnp.zeros((k_ref.shape[-1] - HEAD_DIM - MASK_ROWS, T), BF16)
    n_steps = NQ * (NQ + 1) // 2
    N_S = SCORE_SLOTS
    assert N_S % 2 == 0 and n_steps % N_S == 0

    def score_stage(I, J, slot, table):
        off = pl.multiple_of(J * T, T)
        rel = jnp.minimum(I - J, nrel - 1)
        for r in range(R):
            kt = k_ref[0, 0, r, pl.ds(off, T), :] if k_per_r else k_ref[0, 0, pl.ds(off, T), :]
            w = qt_ref[0, 0, r, I]
            if has_sel:
                w = jnp.concatenate([w, sel_ref[0, 0, I, J], pad_rows], axis=0)
            s = jnp.dot(kt, w, preferred_element_type=F32)
            if early_bias:
                if table:
                    s = s + bt_ref[0, rel, bias_r(r)]
                tmax_ref[slot, r] = jnp.max(s, axis=0, keepdims=True)
            s_ref[slot, r] = s

    def softmax_stage(I, J, s_slot, slot, table):
        for r in range(R):
            m_prev = m_ref[I, r]
            s = s_ref[s_slot, r]
            if early_bias:
                tile_max = tmax_ref[s_slot, r]
            else:
                if table:
                    s = s + bt_ref[0, jnp.minimum(I - J, nrel - 1), bias_r(r)]
                tile_max = jnp.max(s, axis=0, keepdims=True)
            m_new = jnp.maximum(m_prev, tile_max)
            al_ref[slot, r] = jnp.exp2(m_prev - m_new)
            p_ref[slot, r] = jnp.exp2(s - m_new).astype(BF16)
            m_ref[I, r] = m_new

    def value_stage(J, slot):
        vt = vt_ref[0, 0, J]
        return [jnp.dot(vt, p_ref[slot, r], preferred_element_type=F32) for r in range(R)]

    def accumulate(I, slot, pvs):
        for r in range(R):
            acc_ref[I, r] = al_ref[slot, r] * acc_ref[I, r] + pvs[r]

    def far_advance(I, J):
        wrap = J + 1 > I - (nrel - 1)
        return jnp.where(wrap, I + 1, I), jnp.where(wrap, 0, J + 1)

    def near_advance(I, J):
        on_diag, before_diag = J == I, J == I - 1
        nxt_i = jnp.where(on_diag, I + 1, jnp.where(before_diag, I, 0))
        nxt_j = jnp.where(on_diag, I, jnp.where(before_diag, I, 0))
        return nxt_i, nxt_j

    assert nrel == 3
    n_far = (NQ - 1) * (NQ - 2) // 2
    n_near = 2 * NQ - 1
    far_trips, far_left = divmod(n_far, N_S)
    assert far_left == 1 and (far_left + n_near) % N_S == 0
    near_trips = (far_left + n_near) // N_S

    def step(cur, prev, t, advance, table):
        other = 1 - t % 2
        nxt = advance(*cur)
        pvs = value_stage(prev[1], other)
        score_stage(jnp.minimum(nxt[0], NQ - 1), jnp.minimum(nxt[1], NQ - 1), (t + 1) % N_S, table)
        softmax_stage(cur[0], cur[1], t % N_S, t % 2, table)
        accumulate(prev[0], other, pvs)
        return nxt

    def make_trip(advance, table):
        def trip(u, carry):
            cur, prev = carry[:2], carry[2:]
            for t in range(N_S):
                cur, prev = step(cur, prev, t, advance, table), cur
            return (*cur, *prev)
        return trip

    first = (jnp.int32(nrel - 1), jnp.int32(0))
    score_stage(*first, 0, False)
    p_ref[1] = jnp.zeros(p_ref.shape[1:], BF16)
    al_ref[1] = jnp.ones(al_ref.shape[1:], F32)
    carry = lax.fori_loop(0, far_trips, make_trip(far_advance, False), (*first, *first))
    lax.fori_loop(0, near_trips, make_trip(near_advance, True), carry)
    accumulate(NQ - 1, 1, value_stage(NQ - 1, 1))

    if diff:
        dl = dl_ref[...]
        lam = (jnp.exp(jnp.sum(dl[0:1] * dl[1:2], keepdims=True))
               - jnp.exp(jnp.sum(dl[2:3] * dl[3:4], keepdims=True)) + lam_init)

    def finish(I, carry):
        outs = []
        for r in range(R):
            acc = acc_ref[I, r]
            outs.append(acc[:Dv] / acc[Dv:Dv + 1])
        rows = pl.ds(pl.multiple_of(I * T, T), T)
        if diff:
            o = (outs[0] - lam * outs[1]).T
            o_ref[0, rows, :] = (_rms(o, sub_ref[...]) * (1.0 - lam_init)).astype(o_ref.dtype)
        else:
            o_ref[0, rows, :] = jnp.concatenate(outs, axis=0).T
        return carry

    lax.fori_loop(0, NQ, finish, 0)


def _flash_attn(qt, k, vt, bias_tab, *, sel=None, dl=None, subln=None, lam_init=0.0, name="attn"):
    B, G, R, NQ, _, T = qt.shape
    S = NQ * T
    Dv = vt.shape[-2] - ONES_ROWS
    nrel, bias_heads = bias_tab.shape[1], bias_tab.shape[2]
    k_per_r = k.ndim == 5
    has_sel = sel is not None
    diff = dl is not None
    ins = [qt, k, vt, bias_tab]
    in_specs = [
        pl.BlockSpec((1, 1, R, NQ, HEAD_DIM, T), lambda b, g: (b, g, 0, 0, 0, 0)),
        (pl.BlockSpec((1, 1, R, S, HEAD_DIM), lambda b, g: (b, g, 0, 0, 0)) if k_per_r
         else pl.BlockSpec((1, 1, S, k.shape[-1]), lambda b, g: (b, g, 0, 0))),
        pl.BlockSpec((1, 1, NQ, Dv + ONES_ROWS, T), lambda b, g: (b, g, 0, 0, 0)),
        pl.BlockSpec((1, nrel, bias_heads, T, T), lambda b, g: (g, 0, 0, 0, 0)),
    ]
    if has_sel:
        ins.append(sel)
        in_specs.append(pl.BlockSpec((1, 1, NQ, NQ, MASK_ROWS, T), lambda b, g: (b, g, 0, 0, 0, 0)))
    if diff:
        ins += [dl, subln]
        in_specs += [pl.BlockSpec((4, HEAD_DIM), lambda b, g: (0, 0)),
                     pl.BlockSpec((1, A_VDIM), lambda b, g: (0, 0))]
    out_w = Dv if diff else R * Dv
    kern = functools.partial(_flash_kernel, R=R, T=T, NQ=NQ, Dv=Dv, nrel=nrel, k_per_r=k_per_r,
                             shared_bias=bias_heads == 1, diff=diff, has_sel=has_sel, early_bias=not diff,
                             lam_init=lam_init)
    return pl.pallas_call(
        kern,
        grid=(B, G),
        in_specs=in_specs,
        out_specs=pl.BlockSpec((1, S, out_w), lambda b, g: (b, 0, g)),
        out_shape=jax.ShapeDtypeStruct((B, S, G * out_w), BF16 if diff else F32),
        scratch_shapes=[pltpu.VMEM((NQ, R, 1, T), F32), pltpu.VMEM((NQ, R, Dv + ONES_ROWS, T), F32),
                        pltpu.VMEM((SCORE_SLOTS, R, T, T), F32), pltpu.VMEM((2, R, T, T), BF16),
                        pltpu.VMEM((2, R, 1, T), F32), pltpu.VMEM((SCORE_SLOTS, R, 1, T), F32)],
        compiler_params=_params(2),
        name=name,
    )(*ins)


def _window_kernel(*refs, R, T, NK, TV, Dv, n_sub, has_sink):
    it = iter(refs)
    qt_ref, k_ref, vt_ref, bt_ref = (next(it) for _ in range(4))
    sink_ref = next(it) if has_sink else None
    o_ref = next(it)
    heads = [(t, r) for t in range(n_sub) for r in range(R)]
    tile = [pl.program_id(2) * n_sub + t for t in range(n_sub)]
    entry = [jnp.minimum(I, -(-(NK - T) // T)) for I in tile]
    first_key = [pl.multiple_of(I * T - jnp.minimum(e * T, NK - T), TV) for I, e in zip(tile, entry)]
    keys = [k_ref[0, 0, pl.ds(fk, NK), :] for fk in first_key]
    scores = {(t, r): jnp.dot(keys[t], qt_ref[0, 0, r, t], preferred_element_type=F32) for t, r in heads}
    probs, maxes = {}, {}
    for t, r in heads:
        s = scores[t, r] + bt_ref[0, entry[t], r]
        maxes[t, r] = jnp.max(s, axis=0, keepdims=True)
        probs[t, r] = jnp.exp2(s - maxes[t, r]).astype(BF16)
    accs = {}
    for t, r in heads:
        acc = None
        for c in range(NK // TV):
            part = jnp.dot(vt_ref[0, 0, first_key[t] // TV + c], probs[t, r][c * TV:(c + 1) * TV],
                           preferred_element_type=F32)
            acc = part if acc is None else acc + part
        accs[t, r] = acc
    for t in range(n_sub):
        outs = []
        for r in range(R):
            num, l = accs[t, r][:Dv], accs[t, r][Dv:Dv + 1]
            if has_sink:
                sk = sink_ref[0, r] * LOG2E
                m_f = jnp.maximum(maxes[t, r], sk)
                w = jnp.exp2(maxes[t, r] - m_f)
                outs.append(num * w / (l * w + jnp.exp2(sk - m_f)))
            else:
                outs.append(num / l)
        o_ref[0, t * T:(t + 1) * T, :] = jnp.concatenate(outs, axis=0).T.astype(o_ref.dtype)


def _window_attn(qt, k, vt, bias_tab, *, n_sub=2, sink=None, out_dtype=F32, name="attn_window"):
    B, G, R, NQ, _, T = qt.shape
    S = NQ * T
    TV = vt.shape[-1]
    Dv = vt.shape[-2] - ONES_ROWS
    entries, NK = bias_tab.shape[1], bias_tab.shape[3]
    has_sink = sink is not None
    ins = [qt, k, vt, bias_tab]
    in_specs = [
        pl.BlockSpec((1, 1, R, n_sub, HEAD_DIM, T), lambda b, g, i: (b, g, 0, i, 0, 0)),
        pl.BlockSpec((1, 1, S, HEAD_DIM), lambda b, g, i: (b, g, 0, 0)),
        pl.BlockSpec((1, 1, S // TV, Dv + ONES_ROWS, TV), lambda b, g, i: (b, g, 0, 0, 0)),
        pl.BlockSpec((1, entries, R, NK, T), lambda b, g, i: (g, 0, 0, 0, 0)),
    ]
    if has_sink:
        ins.append(sink)
        in_specs.append(pl.BlockSpec((1, R, 1, 1), lambda b, g, i: (g, 0, 0, 0)))
    kern = functools.partial(_window_kernel, R=R, T=T, NK=NK, TV=TV, Dv=Dv, n_sub=n_sub, has_sink=has_sink)
    return pl.pallas_call(
        kern,
        grid=(B, G, NQ // n_sub),
        in_specs=in_specs,
        out_specs=pl.BlockSpec((1, n_sub * T, R * Dv), lambda b, g, i: (b, i, g)),
        out_shape=jax.ShapeDtypeStruct((B, S, G * R * Dv), out_dtype),
        compiler_params=_params(3),
        name=name,
    )(*ins)


def _bucket_np(dist):
    n = np.maximum(dist, 0)
    max_exact = NUM_BUCKETS // 2
    nf = np.maximum(n, 1).astype(np.float32)
    large = max_exact + (np.log(nf / max_exact) / math.log(MAX_DISTANCE / max_exact)
                         * (NUM_BUCKETS - max_exact)).astype(np.int32)
    large = np.minimum(large, NUM_BUCKETS - 1)
    return np.where(n < max_exact, n, large)


def _bucket_starts():
    b = _bucket_np(np.arange(4 * MAX_DISTANCE))
    assert (np.diff(b) >= 0).all() and b[-1] == NUM_BUCKETS - 1
    return [int(np.argmax(b >= i)) for i in range(NUM_BUCKETS)]


def _bias_kernel(tab_ref, o_ref, *, T, rows, nrel, window, back_max, center, starts):
    h = pl.program_id(0)
    base = tab_ref[NUM_BUCKETS - 1, h] * LOG2E if center else 0.0
    key = lax.broadcasted_iota(jnp.int32, (rows, T), 0)
    qry = lax.broadcasted_iota(jnp.int32, (rows, T), 1)
    for rel in range(nrel):
        dist = min(rel * T, back_max) + qry - key
        val = jnp.full((rows, T), tab_ref[0, h] * LOG2E - base, F32)
        for b in range(1, NUM_BUCKETS):
            val = jnp.where(dist >= starts[b], tab_ref[b, h] * LOG2E - base, val)
        allowed = dist >= 0
        if window is not None:
            allowed = allowed & (dist < window)
        o_ref[0, rel, 0] = jnp.where(allowed, val, NEG_INF)


def _bias_tiles(tab, T, nrel, window, R, rows=None, center=False):
    H = tab.shape[1]
    back_max = nrel * T if rows is None else rows - T
    rows = T if rows is None else rows
    return pl.pallas_call(
        functools.partial(_bias_kernel, T=T, rows=rows, nrel=nrel, window=window, back_max=back_max,
                          center=center, starts=_bucket_starts()),
        grid=(H,),
        in_specs=[pl.BlockSpec(memory_space=pltpu.SMEM)],
        out_specs=pl.BlockSpec((1, nrel, 1, rows, T), lambda h: (h // R, 0, h % R, 0, 0)),
        out_shape=jax.ShapeDtypeStruct((H // R, nrel, R, rows, T), F32),
        compiler_params=_params(1),
        name="bias_tiles",
    )(tab)


def _merge_kernel(x_ref, nrm_ref, oa_ref, ob_ref, oc_ref, os_ref, ow_ref, gc_ref, ex_ref, wmg_ref, wb_ref, wo_ref,
                  o_ref):
    x = x_ref[...]
    hb = _rms(x, nrm_ref[...]).astype(BF16)
    gc = gc_ref[...]
    g_hi = gc.astype(BF16)
    g_lo = (gc - g_hi.astype(F32)).astype(BF16)
    gx = jnp.dot(jnp.concatenate([g_hi, g_lo], axis=1), ex_ref[...], preferred_element_type=F32)
    oc = (gx[:, 0:MIX_WIDTH] * oc_ref[...] + gx[:, MIX_WIDTH:2 * MIX_WIDTH] * os_ref[...]
          + gx[:, 2 * MIX_WIDTH:3 * MIX_WIDTH] * ow_ref[...])
    z = None
    for n, br in enumerate((oa_ref[...], ob_ref[...], oc)):
        y = jnp.dot(br.astype(BF16), wb_ref[n], preferred_element_type=F32)
        gate = jax.nn.sigmoid(jnp.dot(hb, wmg_ref[:, n * D_MODEL:(n + 1) * D_MODEL], preferred_element_type=F32))
        t = gate * y
        z = t if z is None else z + t
    o_ref[...] = x + jnp.dot(z.astype(BF16), wo_ref[...], preferred_element_type=F32)


def _merge(x2, nrm, oa, ob, oc, osel, ow, gc, expand_g, wmg, wb, wo, layer, tm=512):
    Tn = x2.shape[0]
    row = lambda d: pl.BlockSpec((tm, d), lambda i: (i, 0))
    const = lambda shape: pl.BlockSpec(shape, lambda i: (0,) * len(shape), pipeline_mode=pl.Buffered(1))
    weight = lambda shape: pl.BlockSpec((None,) + shape, lambda i: (layer,) + (0,) * len(shape),
                                        pipeline_mode=pl.Buffered(1))
    return pl.pallas_call(
        _merge_kernel,
        grid=(Tn // tm,),
        in_specs=[row(D_MODEL), const((1, D_MODEL)),
                  row(MIX_WIDTH), row(MIX_WIDTH), row(MIX_WIDTH), row(MIX_WIDTH), row(MIX_WIDTH),
                  row(CG_PAD),
                  const((2 * CG_PAD, 3 * MIX_WIDTH)),
                  weight((D_MODEL, 3 * D_MODEL)),
                  weight((3, MIX_WIDTH, D_MODEL)),
                  weight((D_MODEL, D_MODEL))],
        out_specs=row(D_MODEL),
        out_shape=jax.ShapeDtypeStruct((Tn, D_MODEL), F32),
        compiler_params=_params(1),
        name="merge",
    )(x2, nrm, oa, ob, oc, osel, ow, gc, expand_g, wmg, wb, wo)


def _mlp_kernel(x_ref, nrm_ref, wu_ref, wd_ref, o_ref, h_ref, acc_ref):
    f = pl.program_id(1)

    @pl.when(f == 0)
    def _():
        h_ref[...] = _rms(x_ref[...], nrm_ref[...]).astype(BF16)
        acc_ref[...] = jnp.zeros(acc_ref.shape, F32)

    u = jnp.dot(h_ref[...], wu_ref[...], preferred_element_type=F32)
    u = jnp.square(jnp.maximum(u, 0.0)).astype(BF16)
    acc_ref[...] += jnp.dot(u, wd_ref[...], preferred_element_type=F32)

    @pl.when(f == pl.num_programs(1) - 1)
    def _():
        o_ref[...] = x_ref[...] + acc_ref[...]


def _mlp(x2, nrm, wu, wd, layer, tm=1024, tf=1024):
    Tn = x2.shape[0]
    return pl.pallas_call(
        _mlp_kernel,
        grid=(Tn // tm, D_FF // tf),
        in_specs=[pl.BlockSpec((tm, D_MODEL), lambda i, f: (i, 0)),
                  pl.BlockSpec((1, D_MODEL), lambda i, f: (0, 0)),
                  pl.BlockSpec((None, D_MODEL, tf), lambda i, f: (layer, 0, f)),
                  pl.BlockSpec((None, tf, D_MODEL), lambda i, f: (layer, f, 0))],
        out_specs=pl.BlockSpec((tm, D_MODEL), lambda i, f: (i, 0)),
        out_shape=jax.ShapeDtypeStruct((Tn, D_MODEL), F32),
        scratch_shapes=[pltpu.VMEM((tm, D_MODEL), BF16), pltpu.VMEM((tm, D_MODEL), F32)],
        compiler_params=_params(2),
        name="mlp",
    )(x2, nrm, wu, wd)


def _overlap(n_cmp):
    c_start = np.arange(NC_PAD) * CMP_STRIDE
    j_start = np.arange(N_SEL_BLK) * SEL_BLOCK
    ov = ((c_start[None, :] < j_start[:, None] + SEL_BLOCK) & (c_start[None, :] + CMP_BLOCK > j_start[:, None])
          & (np.arange(NC_PAD)[None, :] < n_cmp))
    return jnp.asarray(ov.astype(np.float32), BF16)


def _gate_expand():
    e = np.zeros((2, CG_PAD, 3 * MIX_WIDTH), np.float32)
    for h in range(C_HEADS):
        for j in range(3):
            e[:, h * 3 + j, j * MIX_WIDTH + h * HEAD_DIM:j * MIX_WIDTH + (h + 1) * HEAD_DIM] = 1.0
    return jnp.asarray(e.reshape(2 * CG_PAD, 3 * MIX_WIDTH), BF16)


def kernel(x, w_in, qk_gain, diff_lambda, diff_subln, sinks, cmp_pos, cmp_w1, cmp_w2,
           w_branch, w_out, norm_mix, norm_mlp, w_up, w_down, rel_bias):
    B, S, _ = x.shape
    depth = w_in.shape[0]
    n_cmp = (S - CMP_BLOCK) // CMP_STRIDE + 1
    assert S % T_ROW == 0 and S // CMP_STRIDE == NC_PAD and S // SEL_BLOCK == N_SEL_BLK
    half = CMP_BLOCK // 2 * HEAD_DIM

    w_heads = w_in[:, :, :C_CG].astype(BF16)
    w_cg = jnp.pad(w_in[:, :, C_CG:C_CG + N_CG], ((0, 0), (0, 0), (0, CG_PAD - N_CG))).astype(BF16)
    w_mg = w_in[:, :, C_CG + N_CG:].astype(BF16)
    gains2 = jnp.concatenate([qk_gain, qk_gain], axis=-1)
    w1 = cmp_w1.astype(BF16).reshape(depth, 2, 2, half, CMP_HIDDEN)
    w2p = jnp.pad(cmp_w2, ((0, 0), (0, 0), (0, 0), (0, LANES - HEAD_DIM))).astype(BF16)
    pos = cmp_pos.reshape(depth, 2, 2, half)
    wb = w_branch.astype(BF16)
    wo = w_out.astype(BF16)
    wu = w_up.astype(BF16)
    wd = w_down.astype(BF16)

    bias_a = rel_bias[:, :A_HEADS]
    bias_b = rel_bias[:, A_HEADS:A_HEADS + B_HEADS]
    bias_c = rel_bias[:, A_HEADS + B_HEADS:]
    bt_a = _bias_tiles(bias_a, T_ROW, 3, None, 1, center=True)
    bt_b = _bias_tiles(bias_b, T_SWA, -(-B_WINDOW // T_SWA) + 1, B_WINDOW, GROUP, rows=B_WINDOW + T_SWA)
    bt_sel = _bias_tiles(bias_c, T_ROW, 3, None, GROUP, center=True)
    bt_win = _bias_tiles(bias_c, T_ROW, C_WINDOW // T_ROW + 1, C_WINDOW, GROUP, rows=C_WINDOW + T_ROW)
    ovl = _overlap(n_cmp)
    expand_g = _gate_expand()

    for layer in range(depth):
        lam_init = 0.8 - 0.6 * math.exp(-0.3 * layer)
        (qat, ka, vat, qbt, kb, vbt, qct, tkv, ksel, vselt, kwin, vwint, gc) = _inproj(
            x, norm_mix[layer][None], w_heads, w_cg, gains2[layer], layer)
        oa = _flash_attn(qat, ka, vat, bt_a, dl=diff_lambda[layer],
                   subln=diff_subln[layer][None], lam_init=lam_init, name="attn_diff")
        ob = _window_attn(qbt, kb, vbt, bt_b, n_sub=4, out_dtype=BF16,
                          sink=sinks[layer].reshape(KV_GROUPS, GROUP, 1, 1), name="attn_swa")
        kc, vct = _compress(tkv.reshape(B, 2, KV_GROUPS, NC_PAD, half), pos[layer],
                            w1[layer], w2p[layer], qk_gain[layer][5:6])
        ocmp, sel = _cmp_attn(qct, kc, vct, ovl, n_cmp)
        osel = _flash_attn(qct, ksel, vselt, bt_sel, sel=sel, name="attn_sel")
        owin = _window_attn(qct, kwin, vwint, bt_win, n_sub=4, name="attn_win")
        f2 = lambda a: a.reshape(B * S, a.shape[-1])
        x2 = _merge(f2(x), norm_mix[layer][None], f2(oa), f2(ob), f2(ocmp), f2(osel), f2(owin), f2(gc),
                    expand_g, w_mg, wb, wo, layer)
        x2 = _mlp(x2, norm_mlp[layer][None], wu, wd, layer)
        x = x2.reshape(B, S, D_MODEL)
    return x
```

```python
import functools
import math

import numpy as np
import jax
import jax.numpy as jnp
from jax import lax
from jax.experimental import pallas as pl
from jax.experimental.pallas import tpu as pltpu

F32 = jnp.float32
BF16 = jnp.bfloat16

D_MODEL = 1024
HEAD_DIM = 64
A_HEADS = 4
A_VDIM = 128
B_HEADS = 8
B_WINDOW = 128
C_HEADS = 8
KV_GROUPS = 2
GROUP = 4
CMP_BLOCK = 32
CMP_STRIDE = 16
CMP_HIDDEN = 256
SEL_BLOCK = 64
N_SELECT = 16
C_WINDOW = 512
MIX_WIDTH = 512
D_FF = 4096
NUM_BUCKETS = 32
MAX_DISTANCE = 128
NEG_INF = -1e30
FORCE_BONUS = 1e4
EPS = 1e-6
LOG2E = 1.4426950408889634
QK_SCALE = HEAD_DIM ** -0.5 * LOG2E

C_AQ, C_AK, C_AV, C_BQ, C_BKV, C_CQ, C_CKV, C_CG = 0, 512, 1024, 1536, 2048, 2304, 2816, 3584
N_CG = C_HEADS * 3
LANES = 128
CG_PAD = LANES
NC_PAD = 256
N_SEL_BLK = 64
ONES_ROWS = 16
MASK_ROWS = 16

SCORE_SLOTS = 8

T_IN = 512
T_ROW = 256
T_SWA = 256
TV_SWA = 128

VMEM_LIMIT = 48 * 1024 * 1024


def _rms(x, gain):
    ms = jnp.mean(x * x, axis=-1, keepdims=True)
    return (x * lax.rsqrt(ms + EPS)) * gain


def _params(n_axes):
    return pltpu.CompilerParams(dimension_semantics=("arbitrary",) * n_axes,
                                vmem_limit_bytes=VMEM_LIMIT)


def _inproj_kernel(x_ref, nrm_ref, w_ref, wcg_ref, gains_ref,
                   qa_ref, ka_ref, va_ref, qb_ref, kb_ref, vb_ref, qc_ref,
                   tkv_ref, ksel_ref, vsel_ref, kwin_ref, vwin_ref, gc_ref):
    tm = x_ref.shape[1]
    hb = _rms(x_ref[0], nrm_ref[...]).astype(BF16)
    lane = lax.broadcasted_iota(jnp.int32, (tm, LANES), 1)
    first = lane < HEAD_DIM

    def ones_pad(width):
        return jnp.where(lax.broadcasted_iota(jnp.int32, (ONES_ROWS, width), 0) == 0, 1.0, 0.0).astype(BF16)

    def mm(c0, n):
        return jnp.dot(hb, w_ref[:, c0:c0 + n], preferred_element_type=F32)

    def norm_pair(y, gain_idx, scale=None):
        sq = y * y
        ms0 = jnp.sum(jnp.where(first, sq, 0.0), axis=-1, keepdims=True) * (1.0 / HEAD_DIM)
        ms1 = jnp.sum(jnp.where(first, 0.0, sq), axis=-1, keepdims=True) * (1.0 / HEAD_DIM)
        inv = jnp.where(first, lax.rsqrt(ms0 + EPS), lax.rsqrt(ms1 + EPS))
        out = (y * inv) * gains_ref[gain_idx:gain_idx + 1, :]
        return out if scale is None else out * scale

    def pairs(c0):
        r = mm(c0, 256)
        return r[:, :LANES], r[:, LANES:]

    def put_qt(ref, i0, i1, y):
        yt = y.T.astype(BF16)
        n_tiles = ref.shape[3]
        tq = tm // n_tiles
        for t in range(n_tiles):
            ref[0, i0, i1, t] = yt[:HEAD_DIM, tq * t:tq * (t + 1)]
            ref[0, i0, i1 + 1, t] = yt[HEAD_DIM:, tq * t:tq * (t + 1)]

    def put_vt(ref, g0, y, n_tiles):
        yt = y.T.astype(BF16)
        tk = tm // n_tiles
        for g in range(2):
            for t in range(n_tiles):
                ref[0, g0 + g, t, 0:HEAD_DIM, :] = yt[HEAD_DIM * g:HEAD_DIM * (g + 1), tk * t:tk * (t + 1)]
                ref[0, g0 + g, t, HEAD_DIM:HEAD_DIM + ONES_ROWS, :] = ones_pad(tk)

    for ch in range(2):
        for half, y in enumerate(pairs(C_AQ + 256 * ch)):
            put_qt(qa_ref, ch * 2 + half, 0, norm_pair(y, 0, QK_SCALE))
        for half, y in enumerate(pairs(C_AK + 256 * ch)):
            y = norm_pair(y, 1).astype(BF16)
            ka_ref[0, ch * 2 + half, 0] = y[:, :HEAD_DIM]
            ka_ref[0, ch * 2 + half, 1] = y[:, HEAD_DIM:]
        for half, y in enumerate(pairs(C_AV + 256 * ch)):
            yt = y.T.astype(BF16)
            for t in range(tm // T_ROW):
                va_ref[0, ch * 2 + half, t, 0:A_VDIM, :] = yt[:, T_ROW * t:T_ROW * (t + 1)]
                va_ref[0, ch * 2 + half, t, A_VDIM:A_VDIM + ONES_ROWS, :] = ones_pad(T_ROW)
        for half, y in enumerate(pairs(C_BQ + 256 * ch)):
            put_qt(qb_ref, ch, 2 * half, norm_pair(y, 2, QK_SCALE))
        for half, y in enumerate(pairs(C_CQ + 256 * ch)):
            put_qt(qc_ref, ch, 2 * half, norm_pair(y, 4, QK_SCALE))

    def put_k(ref, y):
        y = y.astype(BF16)
        ref[0, 0] = y[:, :HEAD_DIM]
        ref[0, 1] = y[:, HEAD_DIM:]

    yk, yv = pairs(C_BKV)
    put_k(kb_ref, norm_pair(yk, 3))
    put_vt(vb_ref, 0, yv, tm // TV_SWA)
    y0, y1 = pairs(C_CKV)
    for kv, y in enumerate((y0, y1)):
        tkv_ref[0, kv, 0] = y[:, :HEAD_DIM]
        tkv_ref[0, kv, 1] = y[:, HEAD_DIM:]
    yk, yv = pairs(C_CKV + 256)
    yk = norm_pair(yk, 6)
    row = lax.broadcasted_iota(jnp.int32, (tm, LANES), 0)
    hot = jnp.where(lane == HEAD_DIM + (row // SEL_BLOCK) % (T_ROW // SEL_BLOCK), 1.0, 0.0)
    ksel_ref[0, 0] = jnp.where(first, yk, hot).astype(BF16)
    ksel_ref[0, 1] = jnp.where(first, pltpu.roll(yk, HEAD_DIM, 1), hot).astype(BF16)
    put_vt(vsel_ref, 0, yv, tm // T_ROW)
    yk, yv = pairs(C_CKV + 512)
    put_k(kwin_ref, norm_pair(yk, 7))
    put_vt(vwin_ref, 0, yv, tm // T_ROW)

    gc_ref[0] = jax.nn.sigmoid(jnp.dot(hb, wcg_ref[...], preferred_element_type=F32))


def _inproj(x, nrm, w, wcg, gains2, layer):
    B, S, _ = x.shape
    tm = T_IN
    nt = S // tm
    sd = jax.ShapeDtypeStruct
    qt = lambda a, b, tq: sd((B, a, b, S // tq, HEAD_DIM, tq), BF16)
    kk = lambda n: sd((B, n, S, HEAD_DIM), BF16)
    vt = lambda n, dv, tk: sd((B, n, S // tk, dv + ONES_ROWS, tk), BF16)
    out_shape = (
        qt(A_HEADS, 2, T_ROW), sd((B, A_HEADS, 2, S, HEAD_DIM), BF16), vt(A_HEADS, A_VDIM, T_ROW),
        qt(KV_GROUPS, GROUP, T_SWA), kk(KV_GROUPS), vt(KV_GROUPS, HEAD_DIM, TV_SWA),
        qt(KV_GROUPS, GROUP, T_ROW),
        sd((B, 2, KV_GROUPS, S, HEAD_DIM), F32),
        sd((B, KV_GROUPS, S, LANES), BF16), vt(KV_GROUPS, HEAD_DIM, T_ROW),
        kk(KV_GROUPS), vt(KV_GROUPS, HEAD_DIM, T_ROW),
        sd((B, S, CG_PAD), F32),
    )
    s_qt = lambda a, b, tq: pl.BlockSpec((1, a, b, tm // tq, HEAD_DIM, tq), lambda b_, i: (b_, 0, 0, i, 0, 0))
    s_k = lambda n: pl.BlockSpec((1, n, tm, HEAD_DIM), lambda b_, i: (b_, 0, i, 0))
    s_k5 = lambda a, c: pl.BlockSpec((1, a, c, tm, HEAD_DIM), lambda b_, i: (b_, 0, 0, i, 0))
    s_vt = lambda n, dv, tk: pl.BlockSpec((1, n, tm // tk, dv + ONES_ROWS, tk), lambda b_, i: (b_, 0, i, 0, 0))
    row = lambda d: pl.BlockSpec((1, tm, d), lambda b_, i: (b_, i, 0))
    out_specs = (
        s_qt(A_HEADS, 2, T_ROW), s_k5(A_HEADS, 2), s_vt(A_HEADS, A_VDIM, T_ROW),
        s_qt(KV_GROUPS, GROUP, T_SWA), s_k(KV_GROUPS), s_vt(KV_GROUPS, HEAD_DIM, TV_SWA),
        s_qt(KV_GROUPS, GROUP, T_ROW), s_k5(2, KV_GROUPS),
        pl.BlockSpec((1, KV_GROUPS, tm, LANES), lambda b_, i: (b_, 0, i, 0)), s_vt(KV_GROUPS, HEAD_DIM, T_ROW),
        s_k(KV_GROUPS), s_vt(KV_GROUPS, HEAD_DIM, T_ROW),
        row(CG_PAD),
    )
    return pl.pallas_call(
        _inproj_kernel,
        grid=(B, nt),
        in_specs=[
            row(D_MODEL),
            pl.BlockSpec((1, D_MODEL), lambda b_, i: (0, 0)),
            pl.BlockSpec((None, D_MODEL, C_CG), lambda b_, i: (layer, 0, 0), pipeline_mode=pl.Buffered(1)),
            pl.BlockSpec((None, D_MODEL, CG_PAD), lambda b_, i: (layer, 0, 0)),
            pl.BlockSpec((8, LANES), lambda b_, i: (0, 0)),
        ],
        out_specs=out_specs,
        out_shape=out_shape,
        compiler_params=_params(2),
        name="inproj",
    )(x, nrm, w, wcg, gains2)


def _compress_kernel(t_ref, pos_ref, w1_ref, w2_ref, gain_ref, o_ref, ot_ref):
    t = t_ref[0, 0, 0]
    lo = (t + pos_ref[0, 0:1, :]).astype(BF16)
    hi = (t + pos_ref[0, 1:2, :]).astype(BF16)
    v = jnp.dot(lo, w1_ref[0, 0], preferred_element_type=F32)
    u = jnp.dot(hi, w1_ref[0, 1], preferred_element_type=F32)
    pre = v + pltpu.roll(u, NC_PAD - 1, 0)
    hcur = jax.nn.gelu(pre).astype(BF16)
    out = jnp.dot(hcur, w2_ref[0], preferred_element_type=F32)
    o64 = out[:, :HEAD_DIM]
    is_key = pl.program_id(1) == 0
    o_ref[0, 0, 0] = jnp.where(is_key, _rms(o64, gain_ref[...]), o64).astype(BF16)
    ot_ref[0, 0, 0] = out.T[:HEAD_DIM].astype(BF16)


def _compress(tkv16, pos, w1, w2p, gain):
    B = tkv16.shape[0]
    half = CMP_BLOCK // 2 * HEAD_DIM
    return pl.pallas_call(
        _compress_kernel,
        grid=(B, 2, KV_GROUPS),
        in_specs=[
            pl.BlockSpec((1, 1, 1, NC_PAD, half), lambda b, kv, g: (b, kv, g, 0, 0)),
            pl.BlockSpec((1, 2, half), lambda b, kv, g: (kv, 0, 0)),
            pl.BlockSpec((1, 2, half, CMP_HIDDEN), lambda b, kv, g: (kv, 0, 0, 0)),
            pl.BlockSpec((1, CMP_HIDDEN, LANES), lambda b, kv, g: (kv, 0, 0)),
            pl.BlockSpec((1, HEAD_DIM), lambda b, kv, g: (0, 0)),
        ],
        out_specs=(pl.BlockSpec((1, 1, 1, NC_PAD, HEAD_DIM), lambda b, kv, g: (b, kv, g, 0, 0)),
                   pl.BlockSpec((1, 1, 1, HEAD_DIM, NC_PAD), lambda b, kv, g: (b, kv, g, 0, 0))),
        out_shape=(jax.ShapeDtypeStruct((B, 2, KV_GROUPS, NC_PAD, HEAD_DIM), BF16),
                   jax.ShapeDtypeStruct((B, 2, KV_GROUPS, HEAD_DIM, NC_PAD), BF16)),
        compiler_params=_params(3),
        name="compress",
    )(tkv16, pos, w1, w2p, gain)


def _cmp_attn_kernel(qt_ref, kc_ref, vct_ref, ovl_ref, o_ref, sel_ref, score_ref, rank_ref, *, T, n_cmp, n_sub):
    R = GROUP
    tiles = [pl.program_id(2) * n_sub + t for t in range(n_sub)]
    j_idx = lax.broadcasted_iota(jnp.int32, (N_SEL_BLK, T), 0)
    lane_q = lax.broadcasted_iota(jnp.int32, (N_SEL_BLK, T), 1)

    def attend(rows):
        kc = kc_ref[0, 0, 0, 0:rows, :]
        vct = vct_ref[0, 0, 0, :, 0:rows]
        ovl = ovl_ref[:, 0:rows]
        c_idx = lax.broadcasted_iota(jnp.int32, (rows, T), 0)
        lane_k = lax.broadcasted_iota(jnp.int32, (rows, T), 1)
        for t, I in enumerate(tiles):
            cmask = (c_idx * CMP_STRIDE + (CMP_BLOCK - 1) <= I * T + lane_k) & (c_idx < n_cmp)
            scores = [jnp.dot(kc, qt_ref[0, 0, r, t], preferred_element_type=F32) for r in range(R)]
            probs = []
            for r in range(R):
                s = jnp.where(cmask, scores[r], NEG_INF)
                m = jnp.max(s, axis=0, keepdims=True)
                e = jnp.where(cmask, jnp.exp2(s - m), 0.0)
                l = jnp.sum(e, axis=0, keepdims=True)
                probs.append(e * (1.0 / jnp.where(l > 0.0, l, 1.0)))
            outs = [jnp.dot(vct, p.astype(BF16), preferred_element_type=F32) for p in probs]
            o_ref[0, t * T:(t + 1) * T, :] = jnp.concatenate(outs, axis=0).T
            psum = (probs[0] + probs[1]) + (probs[2] + probs[3])
            p_hi = psum.astype(BF16)
            p_lo = (psum - p_hi.astype(F32)).astype(BF16)
            imp = jnp.dot(ovl, p_hi, preferred_element_type=F32) + jnp.dot(ovl, p_lo, preferred_element_type=F32)
            tq = I * T + lane_q
            cur = tq // SEL_BLOCK
            valid = j_idx * SEL_BLOCK <= tq
            forced = (j_idx == 0) | (j_idx == cur) | (j_idx == cur - 1)
            score_ref[t] = jnp.where(valid, imp + jnp.where(forced, FORCE_BONUS, 0.0), NEG_INF)

    chunk = NC_PAD // 4
    vis_per_tile = T // CMP_STRIDE
    for v in range(4):
        lo = -(-(v * chunk) // vis_per_tile)
        hi = -(-((v + 1) * chunk) // vis_per_tile)
        assert lo % n_sub == 0 and hi % n_sub == 0

        @pl.when((tiles[0] >= lo) & (tiles[0] < hi))
        def _(v=v):
            attend((v + 1) * chunk)

    rank_ref[...] = jnp.zeros(rank_ref.shape, F32)
    n_live = (tiles[-1] + 1) * (T // SEL_BLOCK)
    sub = 8
    n_grp = N_SEL_BLK // sub
    j_loc = lax.broadcasted_iota(jnp.int32, (sub, T), 0)
    for c in range(n_grp):
        @pl.when(c * sub < n_live)
        def _(c=c):
            for t in range(n_sub):
                grps = [score_ref[t, g * sub:(g + 1) * sub, :] for g in range(n_grp)]
                cnts = [rank_ref[t, g * sub:(g + 1) * sub, :] for g in range(n_grp)]
                for jp in range(c * sub, (c + 1) * sub):
                    row = jnp.broadcast_to(score_ref[t, jp:jp + 1, :], (sub, T))
                    for g in range(n_grp):
                        if g * sub > jp:
                            beats = row >= grps[g]
                        elif (g + 1) * sub - 1 < jp:
                            beats = row > grps[g]
                        else:
                            beats = (row > grps[g]) | ((row == grps[g]) & (j_loc + g * sub > jp))
                        cnts[g] = cnts[g] + jnp.where(beats, 1.0, 0.0)
                for g in range(n_grp):
                    rank_ref[t, g * sub:(g + 1) * sub, :] = cnts[g]
    blk = T // SEL_BLOCK
    for t in range(n_sub):
        for j in range(N_SEL_BLK // blk):
            keep = rank_ref[t, j * blk:(j + 1) * blk, :] < N_SELECT
            slab = jnp.concatenate([jnp.where(keep, 0.0, NEG_INF), jnp.zeros((MASK_ROWS - blk, T), F32)], axis=0)
            sel_ref[0, 0, t, j] = slab.astype(BF16)


def _cmp_attn(qct, kc, vct, ovl, n_cmp, n_sub=2):
    B, G, R, NQ, _, T = qct.shape
    S = NQ * T
    return pl.pallas_call(
        functools.partial(_cmp_attn_kernel, T=T, n_cmp=n_cmp, n_sub=n_sub),
        grid=(B, G, NQ // n_sub),
        in_specs=[
            pl.BlockSpec((1, 1, R, n_sub, HEAD_DIM, T), lambda b, g, i: (b, g, 0, i, 0, 0)),
            pl.BlockSpec((1, 1, 1, NC_PAD, HEAD_DIM), lambda b, g, i: (b, 0, g, 0, 0)),
            pl.BlockSpec((1, 1, 1, HEAD_DIM, NC_PAD), lambda b, g, i: (b, 1, g, 0, 0)),
            pl.BlockSpec((N_SEL_BLK, NC_PAD), lambda b, g, i: (0, 0)),
        ],
        out_specs=(
            pl.BlockSpec((1, n_sub * T, R * HEAD_DIM), lambda b, g, i: (b, i, g)),
            pl.BlockSpec((1, 1, n_sub, NQ, MASK_ROWS, T), lambda b, g, i: (b, g, i, 0, 0, 0)),
        ),
        out_shape=(
            jax.ShapeDtypeStruct((B, S, G * R * HEAD_DIM), F32),
            jax.ShapeDtypeStruct((B, G, NQ, NQ, MASK_ROWS, T), BF16),
        ),
        scratch_shapes=[pltpu.VMEM((n_sub, N_SEL_BLK, T), F32), pltpu.VMEM((n_sub, N_SEL_BLK, T), F32)],
        compiler_params=_params(3),
        name="cmp_attn",
    )(qct, kc, vct, ovl)


def _flash_kernel(*refs, R, T, NQ, Dv, nrel, k_per_r, shared_bias, diff, has_sel, early_bias, lam_init):
    it = iter(refs)
    qt_ref, k_ref, vt_ref, bt_ref = (next(it) for _ in range(4))
    sel_ref = next(it) if has_sel else None
    dl_ref = next(it) if diff else None
    sub_ref = next(it) if diff else None
    o_ref, m_ref, acc_ref, s_ref, p_ref, al_ref, tmax_ref = (next(it) for _ in range(7))

    m_ref[...] = jnp.full(m_ref.shape, NEG_INF, F32)
    acc_ref[...] = jnp.zeros(acc_ref.shape, F32)
    bias_r = (lambda r: 0) if shared_bias else (lambda r: r)
    if has_sel:
        pad_rows = jnp.zeros((k_ref.shape[-1] - HEAD_DIM - MASK_ROWS, T), BF16)
    n_steps = NQ * (NQ + 1) // 2
    N_S = SCORE_SLOTS
    assert N_S % 2 == 0 and n_steps % N_S == 0

    def score_stage(I, J, slot, table):
        off = pl.multiple_of(J * T, T)
        rel = jnp.minimum(I - J, nrel - 1)
        for r in range(R):
            kt = k_ref[0, 0, r, pl.ds(off, T), :] if k_per_r else k_ref[0, 0, pl.ds(off, T), :]
            w = qt_ref[0, 0, r, I]
            if has_sel:
                w = jnp.concatenate([w, sel_ref[0, 0, I, J], pad_rows], axis=0)
            s = jnp.dot(kt, w, preferred_element_type=F32)
            if early_bias:
                if table:
                    s = s + bt_ref[0, rel, bias_r(r)]
                tmax_ref[slot, r] = jnp.max(s, axis=0, keepdims=True)
            s_ref[slot, r] = s

    def softmax_stage(I, J, s_slot, slot, table):
        for r in range(R):
            m_prev = m_ref[I, r]
            s = s_ref[s_slot, r]
            if early_bias:
                tile_max = tmax_ref[s_slot, r]
            else:
                if table:
                    s = s + bt_ref[0, jnp.minimum(I - J, nrel - 1), bias_r(r)]
                tile_max = jnp.max(s, axis=0, keepdims=True)
            m_new = jnp.maximum(m_prev, tile_max)
            al_ref[slot, r] = jnp.exp2(m_prev - m_new)
            p_ref[slot, r] = jnp.exp2(s - m_new).astype(BF16)
            m_ref[I, r] = m_new

    def value_stage(J, slot):
        vt = vt_ref[0, 0, J]
        return [jnp.dot(vt, p_ref[slot, r], preferred_element_type=F32) for r in range(R)]

    def accumulate(I, slot, pvs):
        for r in range(R):
            acc_ref[I, r] = al_ref[slot, r] * acc_ref[I, r] + pvs[r]

    def far_advance(I, J):
        wrap = J + 1 > I - (nrel - 1)
        return jnp.where(wrap, I + 1, I), jnp.where(wrap, 0, J + 1)

    def near_advance(I, J):
        on_diag, before_diag = J == I, J == I - 1
        nxt_i = jnp.where(on_diag, I + 1, jnp.where(before_diag, I, 0))
        nxt_j = jnp.where(on_diag, I, jnp.where(before_diag, I, 0))
        return nxt_i, nxt_j

    assert nrel == 3
    n_far = (NQ - 1) * (NQ - 2) // 2
    n_near = 2 * NQ - 1
    far_trips, far_left = divmod(n_far, N_S)
    assert far_left == 1 and (far_left + n_near) % N_S == 0
    near_trips = (far_left + n_near) // N_S

    def step(cur, prev, t, advance, table):
        other = 1 - t % 2
        nxt = advance(*cur)
        pvs = value_stage(prev[1], other)
        score_stage(jnp.minimum(nxt[0], NQ - 1), jnp.minimum(nxt[1], NQ - 1), (t + 1) % N_S, table)
        softmax_stage(cur[0], cur[1], t % N_S, t % 2, table)
        accumulate(prev[0], other, pvs)
        return nxt

    def make_trip(advance, table):
        def trip(u, carry):
            cur, prev = carry[:2], carry[2:]
            for t in range(N_S):
                cur, prev = step(cur, prev, t, advance, table), cur
            return (*cur, *prev)
        return trip

    first = (jnp.int32(nrel - 1), jnp.int32(0))
    score_stage(*first, 0, False)
    p_ref[1] = jnp.zeros(p_ref.shape[1:], BF16)
    al_ref[1] = jnp.ones(al_ref.shape[1:], F32)
    carry = lax.fori_loop(0, far_trips, make_trip(far_advance, False), (*first, *first))
    lax.fori_loop(0, near_trips, make_trip(near_advance, True), carry)
    accumulate(NQ - 1, 1, value_stage(NQ - 1, 1))

    if diff:
        dl = dl_ref[...]
        lam = (jnp.exp(jnp.sum(dl[0:1] * dl[1:2], keepdims=True))
               - jnp.exp(jnp.sum(dl[2:3] * dl[3:4], keepdims=True)) + lam_init)

    def finish(I, carry):
        outs = []
        for r in range(R):
            acc = acc_ref[I, r]
            outs.append(acc[:Dv] / acc[Dv:Dv + 1])
        rows = pl.ds(pl.multiple_of(I * T, T), T)
        if diff:
            o = (outs[0] - lam * outs[1]).T
            o_ref[0, rows, :] = (_rms(o, sub_ref[...]) * (1.0 - lam_init)).astype(o_ref.dtype)
        else:
            o_ref[0, rows, :] = jnp.concatenate(outs, axis=0).T
        return carry

    lax.fori_loop(0, NQ, finish, 0)


def _flash_attn(qt, k, vt, bias_tab, *, sel=None, dl=None, subln=None, lam_init=0.0, name="attn"):
    B, G, R, NQ, _, T = qt.shape
    S = NQ * T
    Dv = vt.shape[-2] - ONES_ROWS
    nrel, bias_heads = bias_tab.shape[1], bias_tab.shape[2]
    k_per_r = k.ndim == 5
    has_sel = sel is not None
    diff = dl is not None
    ins = [qt, k, vt, bias_tab]
    in_specs = [
        pl.BlockSpec((1, 1, R, NQ, HEAD_DIM, T), lambda b, g: (b, g, 0, 0, 0, 0)),
        (pl.BlockSpec((1, 1, R, S, HEAD_DIM), lambda b, g: (b, g, 0, 0, 0)) if k_per_r
         else pl.BlockSpec((1, 1, S, k.shape[-1]), lambda b, g: (b, g, 0, 0))),
        pl.BlockSpec((1, 1, NQ, Dv + ONES_ROWS, T), lambda b, g: (b, g, 0, 0, 0)),
        pl.BlockSpec((1, nrel, bias_heads, T, T), lambda b, g: (g, 0, 0, 0, 0)),
    ]
    if has_sel:
        ins.append(sel)
        in_specs.append(pl.BlockSpec((1, 1, NQ, NQ, MASK_ROWS, T), lambda b, g: (b, g, 0, 0, 0, 0)))
    if diff:
        ins += [dl, subln]
        in_specs += [pl.BlockSpec((4, HEAD_DIM), lambda b, g: (0, 0)),
                     pl.BlockSpec((1, A_VDIM), lambda b, g: (0, 0))]
    out_w = Dv if diff else R * Dv
    kern = functools.partial(_flash_kernel, R=R, T=T, NQ=NQ, Dv=Dv, nrel=nrel, k_per_r=k_per_r,
                             shared_bias=bias_heads == 1, diff=diff, has_sel=has_sel, early_bias=not diff,
                             lam_init=lam_init)
    return pl.pallas_call(
        kern,
        grid=(B, G),
        in_specs=in_specs,
        out_specs=pl.BlockSpec((1, S, out_w), lambda b, g: (b, 0, g)),
        out_shape=jax.ShapeDtypeStruct((B, S, G * out_w), BF16 if diff else F32),
        scratch_shapes=[pltpu.VMEM((NQ, R, 1, T), F32), pltpu.VMEM((NQ, R, Dv + ONES_ROWS, T), F32),
                        pltpu.VMEM((SCORE_SLOTS, R, T, T), F32), pltpu.VMEM((2, R, T, T), BF16),
                        pltpu.VMEM((2, R, 1, T), F32), pltpu.VMEM((SCORE_SLOTS, R, 1, T), F32)],
        compiler_params=_params(2),
        name=name,
    )(*ins)


def _window_kernel(*refs, R, T, NK, TV, Dv, n_sub, has_sink):
    it = iter(refs)
    qt_ref, k_ref, vt_ref, bt_ref = (next(it) for _ in range(4))
    sink_ref = next(it) if has_sink else None
    o_ref = next(it)
    heads = [(t, r) for t in range(n_sub) for r in range(R)]
    tile = [pl.program_id(2) * n_sub + t for t in range(n_sub)]
    entry = [jnp.minimum(I, -(-(NK - T) // T)) for I in tile]
    first_key = [pl.multiple_of(I * T - jnp.minimum(e * T, NK - T), TV) for I, e in zip(tile, entry)]
    keys = [k_ref[0, 0, pl.ds(fk, NK), :] for fk in first_key]
    scores = {(t, r): jnp.dot(keys[t], qt_ref[0, 0, r, t], preferred_element_type=F32) for t, r in heads}
    probs, maxes = {}, {}
    for t, r in heads:
        s = scores[t, r] + bt_ref[0, entry[t], r]
        maxes[t, r] = jnp.max(s, axis=0, keepdims=True)
        probs[t, r] = jnp.exp2(s - maxes[t, r]).astype(BF16)
    accs = {}
    for t, r in heads:
        acc = None
        for c in range(NK // TV):
            part = jnp.dot(vt_ref[0, 0, first_key[t] // TV + c], probs[t, r][c * TV:(c + 1) * TV],
                           preferred_element_type=F32)
            acc = part if acc is None else acc + part
        accs[t, r] = acc
    for t in range(n_sub):
        outs = []
        for r in range(R):
            num, l = accs[t, r][:Dv], accs[t, r][Dv:Dv + 1]
            if has_sink:
                sk = sink_ref[0, r] * LOG2E
                m_f = jnp.maximum(maxes[t, r], sk)
                w = jnp.exp2(maxes[t, r] - m_f)
                outs.append(num * w / (l * w + jnp.exp2(sk - m_f)))
            else:
                outs.append(num / l)
        o_ref[0, t * T:(t + 1) * T, :] = jnp.concatenate(outs, axis=0).T.astype(o_ref.dtype)


def _window_attn(qt, k, vt, bias_tab, *, n_sub=2, sink=None, out_dtype=F32, name="attn_window"):
    B, G, R, NQ, _, T = qt.shape
    S = NQ * T
    TV = vt.shape[-1]
    Dv = vt.shape[-2] - ONES_ROWS
    entries, NK = bias_tab.shape[1], bias_tab.shape[3]
    has_sink = sink is not None
    ins = [qt, k, vt, bias_tab]
    in_specs = [
        pl.BlockSpec((1, 1, R, n_sub, HEAD_DIM, T), lambda b, g, i: (b, g, 0, i, 0, 0)),
        pl.BlockSpec((1, 1, S, HEAD_DIM), lambda b, g, i: (b, g, 0, 0)),
        pl.BlockSpec((1, 1, S // TV, Dv + ONES_ROWS, TV), lambda b, g, i: (b, g, 0, 0, 0)),
        pl.BlockSpec((1, entries, R, NK, T), lambda b, g, i: (g, 0, 0, 0, 0)),
    ]
    if has_sink:
        ins.append(sink)
        in_specs.append(pl.BlockSpec((1, R, 1, 1), lambda b, g, i: (g, 0, 0, 0)))
    kern = functools.partial(_window_kernel, R=R, T=T, NK=NK, TV=TV, Dv=Dv, n_sub=n_sub, has_sink=has_sink)
    return pl.pallas_call(
        kern,
        grid=(B, G, NQ // n_sub),
        in_specs=in_specs,
        out_specs=pl.BlockSpec((1, n_sub * T, R * Dv), lambda b, g, i: (b, i, g)),
        out_shape=jax.ShapeDtypeStruct((B, S, G * R * Dv), out_dtype),
        compiler_params=_params(3),
        name=name,
    )(*ins)


def _bucket_np(dist):
    n = np.maximum(dist, 0)
    max_exact = NUM_BUCKETS // 2
    nf = np.maximum(n, 1).astype(np.float32)
    large = max_exact + (np.log(nf / max_exact) / math.log(MAX_DISTANCE / max_exact)
                         * (NUM_BUCKETS - max_exact)).astype(np.int32)
    large = np.minimum(large, NUM_BUCKETS - 1)
    return np.where(n < max_exact, n, large)


def _bucket_starts():
    b = _bucket_np(np.arange(4 * MAX_DISTANCE))
    assert (np.diff(b) >= 0).all() and b[-1] == NUM_BUCKETS - 1
    return [int(np.argmax(b >= i)) for i in range(NUM_BUCKETS)]


def _bias_kernel(tab_ref, o_ref, *, T, rows, nrel, window, back_max, center, starts):
    h = pl.program_id(0)
    base = tab_ref[NUM_BUCKETS - 1, h] * LOG2E if center else 0.0
    key = lax.broadcasted_iota(jnp.int32, (rows, T), 0)
    qry = lax.broadcasted_iota(jnp.int32, (rows, T), 1)
    for rel in range(nrel):
        dist = min(rel * T, back_max) + qry - key
        val = jnp.full((rows, T), tab_ref[0, h] * LOG2E - base, F32)
        for b in range(1, NUM_BUCKETS):
            val = jnp.where(dist >= starts[b], tab_ref[b, h] * LOG2E - base, val)
        allowed = dist >= 0
        if window is not None:
            allowed = allowed & (dist < window)
        o_ref[0, rel, 0] = jnp.where(allowed, val, NEG_INF)


def _bias_tiles(tab, T, nrel, window, R, rows=None, center=False):
    H = tab.shape[1]
    back_max = nrel * T if rows is None else rows - T
    rows = T if rows is None else rows
    return pl.pallas_call(
        functools.partial(_bias_kernel, T=T, rows=rows, nrel=nrel, window=window, back_max=back_max,
                          center=center, starts=_bucket_starts()),
        grid=(H,),
        in_specs=[pl.BlockSpec(memory_space=pltpu.SMEM)],
        out_specs=pl.BlockSpec((1, nrel, 1, rows, T), lambda h: (h // R, 0, h % R, 0, 0)),
        out_shape=jax.ShapeDtypeStruct((H // R, nrel, R, rows, T), F32),
        compiler_params=_params(1),
        name="bias_tiles",
    )(tab)


def _merge_kernel(x_ref, nrm_ref, oa_ref, ob_ref, oc_ref, os_ref, ow_ref, gc_ref, ex_ref, wmg_ref, wb_ref, wo_ref,
                  o_ref):
    x = x_ref[...]
    hb = _rms(x, nrm_ref[...]).astype(BF16)
    gc = gc_ref[...]
    g_hi = gc.astype(BF16)
    g_lo = (gc - g_hi.astype(F32)).astype(BF16)
    gx = jnp.dot(jnp.concatenate([g_hi, g_lo], axis=1), ex_ref[...], preferred_element_type=F32)
    oc = (gx[:, 0:MIX_WIDTH] * oc_ref[...] + gx[:, MIX_WIDTH:2 * MIX_WIDTH] * os_ref[...]
          + gx[:, 2 * MIX_WIDTH:3 * MIX_WIDTH] * ow_ref[...])
    z = None
    for n, br in enumerate((oa_ref[...], ob_ref[...], oc)):
        y = jnp.dot(br.astype(BF16), wb_ref[n], preferred_element_type=F32)
        gate = jax.nn.sigmoid(jnp.dot(hb, wmg_ref[:, n * D_MODEL:(n + 1) * D_MODEL], preferred_element_type=F32))
        t = gate * y
        z = t if z is None else z + t
    o_ref[...] = x + jnp.dot(z.astype(BF16), wo_ref[...], preferred_element_type=F32)


def _merge(x2, nrm, oa, ob, oc, osel, ow, gc, expand_g, wmg, wb, wo, layer, tm=512):
    Tn = x2.shape[0]
    row = lambda d: pl.BlockSpec((tm, d), lambda i: (i, 0))
    const = lambda shape: pl.BlockSpec(shape, lambda i: (0,) * len(shape), pipeline_mode=pl.Buffered(1))
    weight = lambda shape: pl.BlockSpec((None,) + shape, lambda i: (layer,) + (0,) * len(shape),
                                        pipeline_mode=pl.Buffered(1))
    return pl.pallas_call(
        _merge_kernel,
        grid=(Tn // tm,),
        in_specs=[row(D_MODEL), const((1, D_MODEL)),
                  row(MIX_WIDTH), row(MIX_WIDTH), row(MIX_WIDTH), row(MIX_WIDTH), row(MIX_WIDTH),
                  row(CG_PAD),
                  const((2 * CG_PAD, 3 * MIX_WIDTH)),
                  weight((D_MODEL, 3 * D_MODEL)),
                  weight((3, MIX_WIDTH, D_MODEL)),
                  weight((D_MODEL, D_MODEL))],
        out_specs=row(D_MODEL),
        out_shape=jax.ShapeDtypeStruct((Tn, D_MODEL), F32),
        compiler_params=_params(1),
        name="merge",
    )(x2, nrm, oa, ob, oc, osel, ow, gc, expand_g, wmg, wb, wo)


def _mlp_kernel(x_ref, nrm_ref, wu_ref, wd_ref, o_ref, h_ref, acc_ref):
    f = pl.program_id(1)

    @pl.when(f == 0)
    def _():
        h_ref[...] = _rms(x_ref[...], nrm_ref[...]).astype(BF16)
        acc_ref[...] = jnp.zeros(acc_ref.shape, F32)

    u = jnp.dot(h_ref[...], wu_ref[...], preferred_element_type=F32)
    u = jnp.square(jnp.maximum(u, 0.0)).astype(BF16)
    acc_ref[...] += jnp.dot(u, wd_ref[...], preferred_element_type=F32)

    @pl.when(f == pl.num_programs(1) - 1)
    def _():
        o_ref[...] = x_ref[...] + acc_ref[...]


def _mlp(x2, nrm, wu, wd, layer, tm=1024, tf=1024):
    Tn = x2.shape[0]
    return pl.pallas_call(
        _mlp_kernel,
        grid=(Tn // tm, D_FF // tf),
        in_specs=[pl.BlockSpec((tm, D_MODEL), lambda i, f: (i, 0)),
                  pl.BlockSpec((1, D_MODEL), lambda i, f: (0, 0)),
                  pl.BlockSpec((None, D_MODEL, tf), lambda i, f: (layer, 0, f)),
                  pl.BlockSpec((None, tf, D_MODEL), lambda i, f: (layer, f, 0))],
        out_specs=pl.BlockSpec((tm, D_MODEL), lambda i, f: (i, 0)),
        out_shape=jax.ShapeDtypeStruct((Tn, D_MODEL), F32),
        scratch_shapes=[pltpu.VMEM((tm, D_MODEL), BF16), pltpu.VMEM((tm, D_MODEL), F32)],
        compiler_params=_params(2),
        name="mlp",
    )(x2, nrm, wu, wd)


def _overlap(n_cmp):
    c_start = np.arange(NC_PAD) * CMP_STRIDE
    j_start = np.arange(N_SEL_BLK) * SEL_BLOCK
    ov = ((c_start[None, :] < j_start[:, None] + SEL_BLOCK) & (c_start[None, :] + CMP_BLOCK > j_start[:, None])
          & (np.arange(NC_PAD)[None, :] < n_cmp))
    return jnp.asarray(ov.astype(np.float32), BF16)


def _gate_expand():
    e = np.zeros((2, CG_PAD, 3 * MIX_WIDTH), np.float32)
    for h in range(C_HEADS):
        for j in range(3):
            e[:, h * 3 + j, j * MIX_WIDTH + h * HEAD_DIM:j * MIX_WIDTH + (h + 1) * HEAD_DIM] = 1.0
    return jnp.asarray(e.reshape(2 * CG_PAD, 3 * MIX_WIDTH), BF16)


def kernel(x, w_in, qk_gain, diff_lambda, diff_subln, sinks, cmp_pos, cmp_w1, cmp_w2,
           w_branch, w_out, norm_mix, norm_mlp, w_up, w_down, rel_bias):
    B, S, _ = x.shape
    depth = w_in.shape[0]
    n_cmp = (S - CMP_BLOCK) // CMP_STRIDE + 1
    assert S % T_ROW == 0 and S // CMP_STRIDE == NC_PAD and S // SEL_BLOCK == N_SEL_BLK
    half = CMP_BLOCK // 2 * HEAD_DIM

    w_heads = w_in[:, :, :C_CG].astype(BF16)
    w_cg = jnp.pad(w_in[:, :, C_CG:C_CG + N_CG], ((0, 0), (0, 0), (0, CG_PAD - N_CG))).astype(BF16)
    w_mg = w_in[:, :, C_CG + N_CG:].astype(BF16)
    gains2 = jnp.concatenate([qk_gain, qk_gain], axis=-1)
    w1 = cmp_w1.astype(BF16).reshape(depth, 2, 2, half, CMP_HIDDEN)
    w2p = jnp.pad(cmp_w2, ((0, 0), (0, 0), (0, 0), (0, LANES - HEAD_DIM))).astype(BF16)
    pos = cmp_pos.reshape(depth, 2, 2, half)
    wb = w_branch.astype(BF16)
    wo = w_out.astype(BF16)
    wu = w_up.astype(BF16)
    wd = w_down.astype(BF16)

    bias_a = rel_bias[:, :A_HEADS]
    bias_b = rel_bias[:, A_HEADS:A_HEADS + B_HEADS]
    bias_c = rel_bias[:, A_HEADS + B_HEADS:]
    bt_a = _bias_tiles(bias_a, T_ROW, 3, None, 1, center=True)
    bt_b = _bias_tiles(bias_b, T_SWA, -(-B_WINDOW // T_SWA) + 1, B_WINDOW, GROUP, rows=B_WINDOW + T_SWA)
    bt_sel = _bias_tiles(bias_c, T_ROW, 3, None, GROUP, center=True)
    bt_win = _bias_tiles(bias_c, T_ROW, C_WINDOW // T_ROW + 1, C_WINDOW, GROUP, rows=C_WINDOW + T_ROW)
    ovl = _overlap(n_cmp)
    expand_g = _gate_expand()

    for layer in range(depth):
        lam_init = 0.8 - 0.6 * math.exp(-0.3 * layer)
        (qat, ka, vat, qbt, kb, vbt, qct, tkv, ksel, vselt, kwin, vwint, gc) = _inproj(
            x, norm_mix[layer][None], w_heads, w_cg, gains2[layer], layer)
        oa = _flash_attn(qat, ka, vat, bt_a, dl=diff_lambda[layer],
                   subln=diff_subln[layer][None], lam_init=lam_init, name="attn_diff")
        ob = _window_attn(qbt, kb, vbt, bt_b, n_sub=4, out_dtype=BF16,
                          sink=sinks[layer].reshape(KV_GROUPS, GROUP, 1, 1), name="attn_swa")
        kc, vct = _compress(tkv.reshape(B, 2, KV_GROUPS, NC_PAD, half), pos[layer],
                            w1[layer], w2p[layer], qk_gain[layer][5:6])
        ocmp, sel = _cmp_attn(qct, kc, vct, ovl, n_cmp)
        osel = _flash_attn(qct, ksel, vselt, bt_sel, sel=sel, name="attn_sel")
        owin = _window_attn(qct, kwin, vwint, bt_win, n_sub=4, name="attn_win")
        f2 = lambda a: a.reshape(B * S, a.shape[-1])
        x2 = _merge(f2(x), norm_mix[layer][None], f2(oa), f2(ob), f2(ocmp), f2(osel), f2(owin), f2(gc),
                    expand_g, w_mg, wb, wo, layer)
        x2 = _mlp(x2, norm_mlp[layer][None], wu, wd, layer)
        x = x2.reshape(B, S, D_MODEL)
    return x
```

```python
import functools
import math

import numpy as np
import jax
import jax.numpy as jnp
from jax import lax
from jax.experimental import pallas as pl
from jax.experimental.pallas import tpu as pltpu

F32 = jnp.float32
BF16 = jnp.bfloat16

D_MODEL = 1024
HEAD_DIM = 64
A_HEADS = 4
A_VDIM = 128
B_HEADS = 8
B_WINDOW = 128
C_HEADS = 8
KV_GROUPS = 2
GROUP = 4
CMP_BLOCK = 32
CMP_STRIDE = 16
CMP_HIDDEN = 256
SEL_BLOCK = 64
N_SELECT = 16
C_WINDOW = 512
MIX_WIDTH = 512
D_FF = 4096
NUM_BUCKETS = 32
MAX_DISTANCE = 128
NEG_INF = -1e30
FORCE_BONUS = 1e4
EPS = 1e-6
LOG2E = 1.4426950408889634
QK_SCALE = HEAD_DIM ** -0.5 * LOG2E

C_AQ, C_AK, C_AV, C_BQ, C_BKV, C_CQ, C_CKV, C_CG = 0, 512, 1024, 1536, 2048, 2304, 2816, 3584
N_CG = C_HEADS * 3
LANES = 128
CG_PAD = LANES
NC_PAD = 256
N_SEL_BLK = 64
ONES_ROWS = 16
MASK_ROWS = 16

SCORE_SLOTS = 8

T_IN = 512
T_ROW = 256
T_SWA = 256
TV_SWA = 128

VMEM_LIMIT = 48 * 1024 * 1024


def _rms(x, gain):
    ms = jnp.mean(x * x, axis=-1, keepdims=True)
    return (x * lax.rsqrt(ms + EPS)) * gain


def _params(n_axes):
    return pltpu.CompilerParams(dimension_semantics=("arbitrary",) * n_axes,
                                vmem_limit_bytes=VMEM_LIMIT)


def _inproj_kernel(x_ref, nrm_ref, w_ref, wcg_ref, gains_ref,
                   qa_ref, ka_ref, va_ref, qb_ref, kb_ref, vb_ref, qc_ref,
                   tkv_ref, ksel_ref, vsel_ref, kwin_ref, vwin_ref, gc_ref):
    tm = x_ref.shape[1]
    hb = _rms(x_ref[0], nrm_ref[...]).astype(BF16)
    lane = lax.broadcasted_iota(jnp.int32, (tm, LANES), 1)
    first = lane < HEAD_DIM

    def ones_pad(width):
        return jnp.where(lax.broadcasted_iota(jnp.int32, (ONES_ROWS, width), 0) == 0, 1.0, 0.0).astype(BF16)

    def mm(c0, n):
        return jnp.dot(hb, w_ref[:, c0:c0 + n], preferred_element_type=F32)

    def norm_pair(y, gain_idx, scale=None):
        sq = y * y
        ms0 = jnp.sum(jnp.where(first, sq, 0.0), axis=-1, keepdims=True) * (1.0 / HEAD_DIM)
        ms1 = jnp.sum(jnp.where(first, 0.0, sq), axis=-1, keepdims=True) * (1.0 / HEAD_DIM)
        inv = jnp.where(first, lax.rsqrt(ms0 + EPS), lax.rsqrt(ms1 + EPS))
        out = (y * inv) * gains_ref[gain_idx:gain_idx + 1, :]
        return out if scale is None else out * scale

    def pairs(c0):
        r = mm(c0, 256)
        return r[:, :LANES], r[:, LANES:]

    def put_qt(ref, i0, i1, y):
        yt = y.T.astype(BF16)
        n_tiles = ref.shape[3]
        tq = tm // n_tiles
        for t in range(n_tiles):
            ref[0, i0, i1, t] = yt[:HEAD_DIM, tq * t:tq * (t + 1)]
            ref[0, i0, i1 + 1, t] = yt[HEAD_DIM:, tq * t:tq * (t + 1)]

    def put_vt(ref, g0, y, n_tiles):
        yt = y.T.astype(BF16)
        tk = tm // n_tiles
        for g in range(2):
            for t in range(n_tiles):
                ref[0, g0 + g, t, 0:HEAD_DIM, :] = yt[HEAD_DIM * g:HEAD_DIM * (g + 1), tk * t:tk * (t + 1)]
                ref[0, g0 + g, t, HEAD_DIM:HEAD_DIM + ONES_ROWS, :] = ones_pad(tk)

    for ch in range(2):
        for half, y in enumerate(pairs(C_AQ + 256 * ch)):
            put_qt(qa_ref, ch * 2 + half, 0, norm_pair(y, 0, QK_SCALE))
        for half, y in enumerate(pairs(C_AK + 256 * ch)):
            y = norm_pair(y, 1).astype(BF16)
            ka_ref[0, ch * 2 + half, 0] = y[:, :HEAD_DIM]
            ka_ref[0, ch * 2 + half, 1] = y[:, HEAD_DIM:]
        for half, y in enumerate(pairs(C_AV + 256 * ch)):
            yt = y.T.astype(BF16)
            for t in range(tm // T_ROW):
                va_ref[0, ch * 2 + half, t, 0:A_VDIM, :] = yt[:, T_ROW * t:T_ROW * (t + 1)]
                va_ref[0, ch * 2 + half, t, A_VDIM:A_VDIM + ONES_ROWS, :] = ones_pad(T_ROW)
        for half, y in enumerate(pairs(C_BQ + 256 * ch)):
            put_qt(qb_ref, ch, 2 * half, norm_pair(y, 2, QK_SCALE))
        for half, y in enumerate(pairs(C_CQ + 256 * ch)):
            put_qt(qc_ref, ch, 2 * half, norm_pair(y, 4, QK_SCALE))

    def put_k(ref, y):
        y = y.astype(BF16)
        ref[0, 0] = y[:, :HEAD_DIM]
        ref[0, 1] = y[:, HEAD_DIM:]

    yk, yv = pairs(C_BKV)
    put_k(kb_ref, norm_pair(yk, 3))
    put_vt(vb_ref, 0, yv, tm // TV_SWA)
    y0, y1 = pairs(C_CKV)
    for kv, y in enumerate((y0, y1)):
        tkv_ref[0, kv, 0] = y[:, :HEAD_DIM]
        tkv_ref[0, kv, 1] = y[:, HEAD_DIM:]
    yk, yv = pairs(C_CKV + 256)
    yk = norm_pair(yk, 6)
    row = lax.broadcasted_iota(jnp.int32, (tm, LANES), 0)
    hot = jnp.where(lane == HEAD_DIM + (row // SEL_BLOCK) % (T_ROW // SEL_BLOCK), 1.0, 0.0)
    ksel_ref[0, 0] = jnp.where(first, yk, hot).astype(BF16)
    ksel_ref[0, 1] = jnp.where(first, pltpu.roll(yk, HEAD_DIM, 1), hot).astype(BF16)
    put_vt(vsel_ref, 0, yv, tm // T_ROW)
    yk, yv = pairs(C_CKV + 512)
    put_k(kwin_ref, norm_pair(yk, 7))
    put_vt(vwin_ref, 0, yv, tm // T_ROW)

    gc_ref[0] = jax.nn.sigmoid(jnp.dot(hb, wcg_ref[...], preferred_element_type=F32))


def _inproj(x, nrm, w, wcg, gains2, layer):
    B, S, _ = x.shape
    tm = T_IN
    nt = S // tm
    sd = jax.ShapeDtypeStruct
    qt = lambda a, b, tq: sd((B, a, b, S // tq, HEAD_DIM, tq), BF16)
    kk = lambda n: sd((B, n, S, HEAD_DIM), BF16)
    vt = lambda n, dv, tk: sd((B, n, S // tk, dv + ONES_ROWS, tk), BF16)
    out_shape = (
        qt(A_HEADS, 2, T_ROW), sd((B, A_HEADS, 2, S, HEAD_DIM), BF16), vt(A_HEADS, A_VDIM, T_ROW),
        qt(KV_GROUPS, GROUP, T_SWA), kk(KV_GROUPS), vt(KV_GROUPS, HEAD_DIM, TV_SWA),
        qt(KV_GROUPS, GROUP, T_ROW),
        sd((B, 2, KV_GROUPS, S, HEAD_DIM), F32),
        sd((B, KV_GROUPS, S, LANES), BF16), vt(KV_GROUPS, HEAD_DIM, T_ROW),
        kk(KV_GROUPS), vt(KV_GROUPS, HEAD_DIM, T_ROW),
        sd((B, S, CG_PAD), F32),
    )
    s_qt = lambda a, b, tq: pl.BlockSpec((1, a, b, tm // tq, HEAD_DIM, tq), lambda b_, i: (b_, 0, 0, i, 0, 0))
    s_k = lambda n: pl.BlockSpec((1, n, tm, HEAD_DIM), lambda b_, i: (b_, 0, i, 0))
    s_k5 = lambda a, c: pl.BlockSpec((1, a, c, tm, HEAD_DIM), lambda b_, i: (b_, 0, 0, i, 0))
    s_vt = lambda n, dv, tk: pl.BlockSpec((1, n, tm // tk, dv + ONES_ROWS, tk), lambda b_, i: (b_, 0, i, 0, 0))
    row = lambda d: pl.BlockSpec((1, tm, d), lambda b_, i: (b_, i, 0))
    out_specs = (
        s_qt(A_HEADS, 2, T_ROW), s_k5(A_HEADS, 2), s_vt(A_HEADS, A_VDIM, T_ROW),
        s_qt(KV_GROUPS, GROUP, T_SWA), s_k(KV_GROUPS), s_vt(KV_GROUPS, HEAD_DIM, TV_SWA),
        s_qt(KV_GROUPS, GROUP, T_ROW), s_k5(2, KV_GROUPS),
        pl.BlockSpec((1, KV_GROUPS, tm, LANES), lambda b_, i: (b_, 0, i, 0)), s_vt(KV_GROUPS, HEAD_DIM, T_ROW),
        s_k(KV_GROUPS), s_vt(KV_GROUPS, HEAD_DIM, T_ROW),
        row(CG_PAD),
    )
    return pl.pallas_call(
        _inproj_kernel,
        grid=(B, nt),
        in_specs=[
            row(D_MODEL),
            pl.BlockSpec((1, D_MODEL), lambda b_, i: (0, 0)),
            pl.BlockSpec((None, D_MODEL, C_CG), lambda b_, i: (layer, 0, 0), pipeline_mode=pl.Buffered(1)),
            pl.BlockSpec((None, D_MODEL, CG_PAD), lambda b_, i: (layer, 0, 0)),
            pl.BlockSpec((8, LANES), lambda b_, i: (0, 0)),
        ],
        out_specs=out_specs,
        out_shape=out_shape,
        compiler_params=_params(2),
        name="inproj",
    )(x, nrm, w, wcg, gains2)


def _compress_kernel(t_ref, pos_ref, w1_ref, w2_ref, gain_ref, o_ref, ot_ref):
    t = t_ref[0, 0, 0]
    lo = (t + pos_ref[0, 0:1, :]).astype(BF16)
    hi = (t + pos_ref[0, 1:2, :]).astype(BF16)
    v = jnp.dot(lo, w1_ref[0, 0], preferred_element_type=F32)
    u = jnp.dot(hi, w1_ref[0, 1], preferred_element_type=F32)
    pre = v + pltpu.roll(u, NC_PAD - 1, 0)
    hcur = jax.nn.gelu(pre).astype(BF16)
    out = jnp.dot(hcur, w2_ref[0], preferred_element_type=F32)
    o64 = out[:, :HEAD_DIM]
    is_key = pl.program_id(1) == 0
    o_ref[0, 0, 0] = jnp.where(is_key, _rms(o64, gain_ref[...]), o64).astype(BF16)
    ot_ref[0, 0, 0] = out.T[:HEAD_DIM].astype(BF16)


def _compress(tkv16, pos, w1, w2p, gain):
    B = tkv16.shape[0]
    half = CMP_BLOCK // 2 * HEAD_DIM
    return pl.pallas_call(
        _compress_kernel,
        grid=(B, 2, KV_GROUPS),
        in_specs=[
            pl.BlockSpec((1, 1, 1, NC_PAD, half), lambda b, kv, g: (b, kv, g, 0, 0)),
            pl.BlockSpec((1, 2, half), lambda b, kv, g: (kv, 0, 0)),
            pl.BlockSpec((1, 2, half, CMP_HIDDEN), lambda b, kv, g: (kv, 0, 0, 0)),
            pl.BlockSpec((1, CMP_HIDDEN, LANES), lambda b, kv, g: (kv, 0, 0)),
            pl.BlockSpec((1, HEAD_DIM), lambda b, kv, g: (0, 0)),
        ],
        out_specs=(pl.BlockSpec((1, 1, 1, NC_PAD, HEAD_DIM), lambda b, kv, g: (b, kv, g, 0, 0)),
                   pl.BlockSpec((1, 1, 1, HEAD_DIM, NC_PAD), lambda b, kv, g: (b, kv, g, 0, 0))),
        out_shape=(jax.ShapeDtypeStruct((B, 2, KV_GROUPS, NC_PAD, HEAD_DIM), BF16),
                   jax.ShapeDtypeStruct((B, 2, KV_GROUPS, HEAD_DIM, NC_PAD), BF16)),
        compiler_params=_params(3),
        name="compress",
    )(tkv16, pos, w1, w2p, gain)


def _cmp_attn_kernel(qt_ref, kc_ref, vct_ref, ovl_ref, o_ref, sel_ref, score_ref, rank_ref, *, T, n_cmp, n_sub):
    R = GROUP
    tiles = [pl.program_id(2) * n_sub + t for t in range(n_sub)]
    j_idx = lax.broadcasted_iota(jnp.int32, (N_SEL_BLK, T), 0)
    lane_q = lax.broadcasted_iota(jnp.int32, (N_SEL_BLK, T), 1)

    def attend(rows):
        kc = kc_ref[0, 0, 0, 0:rows, :]
        vct = vct_ref[0, 0, 0, :, 0:rows]
        ovl = ovl_ref[:, 0:rows]
        c_idx = lax.broadcasted_iota(jnp.int32, (rows, T), 0)
        lane_k = lax.broadcasted_iota(jnp.int32, (rows, T), 1)
        for t, I in enumerate(tiles):
            cmask = (c_idx * CMP_STRIDE + (CMP_BLOCK - 1) <= I * T + lane_k) & (c_idx < n_cmp)
            scores = [jnp.dot(kc, qt_ref[0, 0, r, t], preferred_element_type=F32) for r in range(R)]
            probs = []
            for r in range(R):
                s = jnp.where(cmask, scores[r], NEG_INF)
                m = jnp.max(s, axis=0, keepdims=True)
                e = jnp.where(cmask, jnp.exp2(s - m), 0.0)
                l = jnp.sum(e, axis=0, keepdims=True)
                probs.append(e * (1.0 / jnp.where(l > 0.0, l, 1.0)))
            outs = [jnp.dot(vct, p.astype(BF16), preferred_element_type=F32) for p in probs]
            o_ref[0, t * T:(t + 1) * T, :] = jnp.concatenate(outs, axis=0).T
            psum = (probs[0] + probs[1]) + (probs[2] + probs[3])
            p_hi = psum.astype(BF16)
            p_lo = (psum - p_hi.astype(F32)).astype(BF16)
            imp = jnp.dot(ovl, p_hi, preferred_element_type=F32) + jnp.dot(ovl, p_lo, preferred_element_type=F32)
            tq = I * T + lane_q
            cur = tq // SEL_BLOCK
            valid = j_idx * SEL_BLOCK <= tq
            forced = (j_idx == 0) | (j_idx == cur) | (j_idx == cur - 1)
            score_ref[t] = jnp.where(valid, imp + jnp.where(forced, FORCE_BONUS, 0.0), NEG_INF)

    chunk = NC_PAD // 4
    vis_per_tile = T // CMP_STRIDE
    for v in range(4):
        lo = -(-(v * chunk) // vis_per_tile)
        hi = -(-((v + 1) * chunk) // vis_per_tile)
        assert lo % n_sub == 0 and hi % n_sub == 0

        @pl.when((tiles[0] >= lo) & (tiles[0] < hi))
        def _(v=v):
            attend((v + 1) * chunk)

    rank_ref[...] = jnp.zeros(rank_ref.shape, F32)
    n_live = (tiles[-1] + 1) * (T // SEL_BLOCK)
    sub = 8
    n_grp = N_SEL_BLK // sub
    j_loc = lax.broadcasted_iota(jnp.int32, (sub, T), 0)
    for c in range(n_grp):
        @pl.when(c * sub < n_live)
        def _(c=c):
            for t in range(n_sub):
                grps = [score_ref[t, g * sub:(g + 1) * sub, :] for g in range(n_grp)]
                cnts = [rank_ref[t, g * sub:(g + 1) * sub, :] for g in range(n_grp)]
                for jp in range(c * sub, (c + 1) * sub):
                    row = jnp.broadcast_to(score_ref[t, jp:jp + 1, :], (sub, T))
                    for g in range(n_grp):
                        if g * sub > jp:
                            beats = row >= grps[g]
                        elif (g + 1) * sub - 1 < jp:
                            beats = row > grps[g]
                        else:
                            beats = (row > grps[g]) | ((row == grps[g]) & (j_loc + g * sub > jp))
                        cnts[g] = cnts[g] + jnp.where(beats, 1.0, 0.0)
                for g in range(n_grp):
                    rank_ref[t, g * sub:(g + 1) * sub, :] = cnts[g]
    blk = T // SEL_BLOCK
    for t in range(n_sub):
        for j in range(N_SEL_BLK // blk):
            keep = rank_ref[t, j * blk:(j + 1) * blk, :] < N_SELECT
            slab = jnp.concatenate([jnp.where(keep, 0.0, NEG_INF), jnp.zeros((MASK_ROWS - blk, T), F32)], axis=0)
            sel_ref[0, 0, t, j] = slab.astype(BF16)


def _cmp_attn(qct, kc, vct, ovl, n_cmp, n_sub=2):
    B, G, R, NQ, _, T = qct.shape
    S = NQ * T
    return pl.pallas_call(
        functools.partial(_cmp_attn_kernel, T=T, n_cmp=n_cmp, n_sub=n_sub),
        grid=(B, G, NQ // n_sub),
        in_specs=[
            pl.BlockSpec((1, 1, R, n_sub, HEAD_DIM, T), lambda b, g, i: (b, g, 0, i, 0, 0)),
            pl.BlockSpec((1, 1, 1, NC_PAD, HEAD_DIM), lambda b, g, i: (b, 0, g, 0, 0)),
            pl.BlockSpec((1, 1, 1, HEAD_DIM, NC_PAD), lambda b, g, i: (b, 1, g, 0, 0)),
            pl.BlockSpec((N_SEL_BLK, NC_PAD), lambda b, g, i: (0, 0)),
        ],
        out_specs=(
            pl.BlockSpec((1, n_sub * T, R * HEAD_DIM), lambda b, g, i: (b, i, g)),
            pl.BlockSpec((1, 1, n_sub, NQ, MASK_ROWS, T), lambda b, g, i: (b, g, i, 0, 0, 0)),
        ),
        out_shape=(
            jax.ShapeDtypeStruct((B, S, G * R * HEAD_DIM), F32),
            jax.ShapeDtypeStruct((B, G, NQ, NQ, MASK_ROWS, T), BF16),
        ),
        scratch_shapes=[pltpu.VMEM((n_sub, N_SEL_BLK, T), F32), pltpu.VMEM((n_sub, N_SEL_BLK, T), F32)],
        compiler_params=_params(3),
        name="cmp_attn",
    )(qct, kc, vct, ovl)


def _flash_kernel(*refs, R, T, NQ, Dv, nrel, k_per_r, shared_bias, diff, has_sel, early_bias, lam_init):
    it = iter(refs)
    qt_ref, k_ref, vt_ref, bt_ref = (next(it) for _ in range(4))
    sel_ref = next(it) if has_sel else None
    dl_ref = next(it) if diff else None
    sub_ref = next(it) if diff else None
    o_ref, m_ref, acc_ref, s_ref, p_ref, al_ref, tmax_ref = (next(it) for _ in range(7))

    m_ref[...] = jnp.full(m_ref.shape, NEG_INF, F32)
    acc_ref[...] = jnp.zeros(acc_ref.shape, F32)
    bias_r = (lambda r: 0) if shared_bias else (lambda r: r)
    if has_sel:
        pad_rows = jnp.zeros((k_ref.shape[-1] - HEAD_DIM - MASK_ROWS, T), BF16)
    n_steps = NQ * (NQ + 1) // 2
    N_S = SCORE_SLOTS
    assert N_S % 2 == 0 and n_steps % N_S == 0

    def score_stage(I, J, slot, table):
        off = pl.multiple_of(J * T, T)
        rel = jnp.minimum(I - J, nrel - 1)
        for r in range(R):
            kt = k_ref[0, 0, r, pl.ds(off, T), :] if k_per_r else k_ref[0, 0, pl.ds(off, T), :]
            w = qt_ref[0, 0, r, I]
            if has_sel:
                w = jnp.concatenate([w, sel_ref[0, 0, I, J], pad_rows], axis=0)
            s = jnp.dot(kt, w, preferred_element_type=F32)
            if early_bias:
                if table:
                    s = s + bt_ref[0, rel, bias_r(r)]
                tmax_ref[slot, r] = jnp.max(s, axis=0, keepdims=True)
            s_ref[slot, r] = s

    def softmax_stage(I, J, s_slot, slot, table):
        for r in range(R):
            m_prev = m_ref[I, r]
            s = s_ref[s_slot, r]
            if early_bias:
                tile_max = tmax_ref[s_slot, r]
            else:
                if table:
                    s = s + bt_ref[0, jnp.minimum(I - J, nrel - 1), bias_r(r)]
                tile_max = jnp.max(s, axis=0, keepdims=True)
            m_new = jnp.maximum(m_prev, tile_max)
            al_ref[slot, r] = jnp.exp2(m_prev - m_new)
            p_ref[slot, r] = jnp.exp2(s - m_new).astype(BF16)
            m_ref[I, r] = m_new

    def value_stage(J, slot):
        vt = vt_ref[0, 0, J]
        return [jnp.dot(vt, p_ref[slot, r], preferred_element_type=F32) for r in range(R)]

    def accumulate(I, slot, pvs):
        for r in range(R):
            acc_ref[I, r] = al_ref[slot, r] * acc_ref[I, r] + pvs[r]

    def far_advance(I, J):
        wrap = J + 1 > I - (nrel - 1)
        return jnp.where(wrap, I + 1, I), jnp.where(wrap, 0, J + 1)

    def near_advance(I, J):
        on_diag, before_diag = J == I, J == I - 1
        nxt_i = jnp.where(on_diag, I + 1, jnp.where(before_diag, I, 0))
        nxt_j = jnp.where(on_diag, I, jnp.where(before_diag, I, 0))
        return nxt_i, nxt_j

    assert nrel == 3
    n_far = (NQ - 1) * (NQ - 2) // 2
    n_near = 2 * NQ - 1
    far_trips, far_left = divmod(n_far, N_S)
    assert far_left == 1 and (far_left + n_near) % N_S == 0
    near_trips = (far_left + n_near) // N_S

    def step(cur, prev, t, advance, table):
        other = 1 - t % 2
        nxt = advance(*cur)
        pvs = value_stage(prev[1], other)
        score_stage(jnp.minimum(nxt[0], NQ - 1), jnp.minimum(nxt[1], NQ - 1), (t + 1) % N_S, table)
        softmax_stage(cur[0], cur[1], t % N_S, t % 2, table)
        accumulate(prev[0], other, pvs)
        return nxt

    def make_trip(advance, table):
        def trip(u, carry):
            cur, prev = carry[:2], carry[2:]
            for t in range(N_S):
                cur, prev = step(cur, prev, t, advance, table), cur
            return (*cur, *prev)
        return trip

    first = (jnp.int32(nrel - 1), jnp.int32(0))
    score_stage(*first, 0, False)
    p_ref[1] = jnp.zeros(p_ref.shape[1:], BF16)
    al_ref[1] = jnp.ones(al_ref.shape[1:], F32)
    carry = lax.fori_loop(0, far_trips, make_trip(far_advance, False), (*first, *first))
    lax.fori_loop(0, near_trips, make_trip(near_advance, True), carry)
    accumulate(NQ - 1, 1, value_stage(NQ - 1, 1))

    if diff:
        dl = dl_ref[...]
        lam = (jnp.exp(jnp.sum(dl[0:1] * dl[1:2], keepdims=True))
               - jnp.exp(jnp.sum(dl[2:3] * dl[3:4], keepdims=True)) + lam_init)

    def finish(I, carry):
        outs = []
        for r in range(R):
            acc = acc_ref[I, r]
            outs.append(acc[:Dv] / acc[Dv:Dv + 1])
        rows = pl.ds(pl.multiple_of(I * T, T), T)
        if diff:
            o = (outs[0] - lam * outs[1]).T
            o_ref[0, rows, :] = (_rms(o, sub_ref[...]) * (1.0 - lam_init)).astype(o_ref.dtype)
        else:
            o_ref[0, rows, :] = jnp.concatenate(outs, axis=0).T
        return carry

    lax.fori_loop(0, NQ, finish, 0)


def _flash_attn(qt, k, vt, bias_tab, *, sel=None, dl=None, subln=None, lam_init=0.0, name="attn"):
    B, G, R, NQ, _, T = qt.shape
    S = NQ * T
    Dv = vt.shape[-2] - ONES_ROWS
    nrel, bias_heads = bias_tab.shape[1], bias_tab.shape[2]
    k_per_r = k.ndim == 5
    has_sel = sel is not None
    diff = dl is not None
    ins = [qt, k, vt, bias_tab]
    in_specs = [
        pl.BlockSpec((1, 1, R, NQ, HEAD_DIM, T), lambda b, g: (b, g, 0, 0, 0, 0)),
        (pl.BlockSpec((1, 1, R, S, HEAD_DIM), lambda b, g: (b, g, 0, 0, 0)) if k_per_r
         else pl.BlockSpec((1, 1, S, k.shape[-1]), lambda b, g: (b, g, 0, 0))),
        pl.BlockSpec((1, 1, NQ, Dv + ONES_ROWS, T), lambda b, g: (b, g, 0, 0, 0)),
        pl.BlockSpec((1, nrel, bias_heads, T, T), lambda b, g: (g, 0, 0, 0, 0)),
    ]
    if has_sel:
        ins.append(sel)
        in_specs.append(pl.BlockSpec((1, 1, NQ, NQ, MASK_ROWS, T), lambda b, g: (b, g, 0, 0, 0, 0)))
    if diff:
        ins += [dl, subln]
        in_specs += [pl.BlockSpec((4, HEAD_DIM), lambda b, g: (0, 0)),
                     pl.BlockSpec((1, A_VDIM), lambda b, g: (0, 0))]
    out_w = Dv if diff else R * Dv
    kern = functools.partial(_flash_kernel, R=R, T=T, NQ=NQ, Dv=Dv, nrel=nrel, k_per_r=k_per_r,
                             shared_bias=bias_heads == 1, diff=diff, has_sel=has_sel, early_bias=not diff,
                             lam_init=lam_init)
    return pl.pallas_call(
        kern,
        grid=(B, G),
        in_specs=in_specs,
        out_specs=pl.BlockSpec((1, S, out_w), lambda b, g: (b, 0, g)),
        out_shape=jax.ShapeDtypeStruct((B, S, G * out_w), BF16 if diff else F32),
        scratch_shapes=[pltpu.VMEM((NQ, R, 1, T), F32), pltpu.VMEM((NQ, R, Dv + ONES_ROWS, T), F32),
                        pltpu.VMEM((SCORE_SLOTS, R, T, T), F32), pltpu.VMEM((2, R, T, T), BF16),
                        pltpu.VMEM((2, R, 1, T), F32), pltpu.VMEM((SCORE_SLOTS, R, 1, T), F32)],
        compiler_params=_params(2),
        name=name,
    )(*ins)


def _window_kernel(*refs, R, T, NK, TV, Dv, n_sub, has_sink):
    it = iter(refs)
    qt_ref, k_ref, vt_ref, bt_ref = (next(it) for _ in range(4))
    sink_ref = next(it) if has_sink else None
    o_ref = next(it)
    heads = [(t, r) for t in range(n_sub) for r in range(R)]
    tile = [pl.program_id(2) * n_sub + t for t in range(n_sub)]
    entry = [jnp.minimum(I, -(-(NK - T) // T)) for I in tile]
    first_key = [pl.multiple_of(I * T - jnp.minimum(e * T, NK - T), TV) for I, e in zip(tile, entry)]
    keys = [k_ref[0, 0, pl.ds(fk, NK), :] for fk in first_key]
    scores = {(t, r): jnp.dot(keys[t], qt_ref[0, 0, r, t], preferred_element_type=F32) for t, r in heads}
    probs, maxes = {}, {}
    for t, r in heads:
        s = scores[t, r] + bt_ref[0, entry[t], r]
        maxes[t, r] = jnp.max(s, axis=0, keepdims=True)
        probs[t, r] = jnp.exp2(s - maxes[t, r]).astype(BF16)
    accs = {}
    for t, r in heads:
        acc = None
        for c in range(NK // TV):
            part = jnp.dot(vt_ref[0, 0, first_key[t] // TV + c], probs[t, r][c * TV:(c + 1) * TV],
                           preferred_element_type=F32)
            acc = part if acc is None else acc + part
        accs[t, r] = acc
    for t in range(n_sub):
        outs = []
        for r in range(R):
            num, l = accs[t, r][:Dv], accs[t, r][Dv:Dv + 1]
            if has_sink:
                sk = sink_ref[0, r] * LOG2E
                m_f = jnp.maximum(maxes[t, r], sk)
                w = jnp.exp2(maxes[t, r] - m_f)
                outs.append(num * w / (l * w + jnp.exp2(sk - m_f)))
            else:
                outs.append(num / l)
        o_ref[0, t * T:(t + 1) * T, :] = jnp.concatenate(outs, axis=0).T.astype(o_ref.dtype)


def _window_attn(qt, k, vt, bias_tab, *, n_sub=2, sink=None, out_dtype=F32, name="attn_window"):
    B, G, R, NQ, _, T = qt.shape
    S = NQ * T
    TV = vt.shape[-1]
    Dv = vt.shape[-2] - ONES_ROWS
    entries, NK = bias_tab.shape[1], bias_tab.shape[3]
    has_sink = sink is not None
    ins = [qt, k, vt, bias_tab]
    in_specs = [
        pl.BlockSpec((1, 1, R, n_sub, HEAD_DIM, T), lambda b, g, i: (b, g, 0, i, 0, 0)),
        pl.BlockSpec((1, 1, S, HEAD_DIM), lambda b, g, i: (b, g, 0, 0)),
        pl.BlockSpec((1, 1, S // TV, Dv + ONES_ROWS, TV), lambda b, g, i: (b, g, 0, 0, 0)),
        pl.BlockSpec((1, entries, R, NK, T), lambda b, g, i: (g, 0, 0, 0, 0)),
    ]
    if has_sink:
        ins.append(sink)
        in_specs.append(pl.BlockSpec((1, R, 1, 1), lambda b, g, i: (g, 0, 0, 0)))
    kern = functools.partial(_window_kernel, R=R, T=T, NK=NK, TV=TV, Dv=Dv, n_sub=n_sub, has_sink=has_sink)
    return pl.pallas_call(
        kern,
        grid=(B, G, NQ // n_sub),
        in_specs=in_specs,
        out_specs=pl.BlockSpec((1, n_sub * T, R * Dv), lambda b, g, i: (b, i, g)),
        out_shape=jax.ShapeDtypeStruct((B, S, G * R * Dv), out_dtype),
        compiler_params=_params(3),
        name=name,
    )(*ins)


def _bucket_np(dist):
    n = np.maximum(dist, 0)
    max_exact = NUM_BUCKETS // 2
    nf = np.maximum(n, 1).astype(np.float32)
    large = max_exact + (np.log(nf / max_exact) / math.log(MAX_DISTANCE / max_exact)
                         * (NUM_BUCKETS - max_exact)).astype(np.int32)
    large = np.minimum(large, NUM_BUCKETS - 1)
    return np.where(n < max_exact, n, large)


def _bucket_starts():
    b = _bucket_np(np.arange(4 * MAX_DISTANCE))
    assert (np.diff(b) >= 0).all() and b[-1] == NUM_BUCKETS - 1
    return [int(np.argmax(b >= i)) for i in range(NUM_BUCKETS)]


def _bias_kernel(tab_ref, o_ref, *, T, rows, nrel, window, back_max, center, starts):
    h = pl.program_id(0)
    base = tab_ref[NUM_BUCKETS - 1, h] * LOG2E if center else 0.0
    key = lax.broadcasted_iota(jnp.int32, (rows, T), 0)
    qry = lax.broadcasted_iota(jnp.int32, (rows, T), 1)
    for rel in range(nrel):
        dist = min(rel * T, back_max) + qry - key
        val = jnp.full((rows, T), tab_ref[0, h] * LOG2E - base, F32)
        for b in range(1, NUM_BUCKETS):
            val = jnp.where(dist >= starts[b], tab_ref[b, h] * LOG2E - base, val)
        allowed = dist >= 0
        if window is not None:
            allowed = allowed & (dist < window)
        o_ref[0, rel, 0] = jnp.where(allowed, val, NEG_INF)


def _bias_tiles(tab, T, nrel, window, R, rows=None, center=False):
    H = tab.shape[1]
    back_max = nrel * T if rows is None else rows - T
    rows = T if rows is None else rows
    return pl.pallas_call(
        functools.partial(_bias_kernel, T=T, rows=rows, nrel=nrel, window=window, back_max=back_max,
                          center=center, starts=_bucket_starts()),
        grid=(H,),
        in_specs=[pl.BlockSpec(memory_space=pltpu.SMEM)],
        out_specs=pl.BlockSpec((1, nrel, 1, rows, T), lambda h: (h // R, 0, h % R, 0, 0)),
        out_shape=jax.ShapeDtypeStruct((H // R, nrel, R, rows, T), F32),
        compiler_params=_params(1),
        name="bias_tiles",
    )(tab)


def _merge_kernel(x_ref, nrm_ref, oa_ref, ob_ref, oc_ref, os_ref, ow_ref, gc_ref, ex_ref, wmg_ref, wb_ref, wo_ref,
                  o_ref):
    x = x_ref[...]
    hb = _rms(x, nrm_ref[...]).astype(BF16)
    gc = gc_ref[...]
    g_hi = gc.astype(BF16)
    g_lo = (gc - g_hi.astype(F32)).astype(BF16)
    gx = jnp.dot(jnp.concatenate([g_hi, g_lo], axis=1), ex_ref[...], preferred_element_type=F32)
    oc = (gx[:, 0:MIX_WIDTH] * oc_ref[...] + gx[:, MIX_WIDTH:2 * MIX_WIDTH] * os_ref[...]
          + gx[:, 2 * MIX_WIDTH:3 * MIX_WIDTH] * ow_ref[...])
    z = None
    for n, br in enumerate((oa_ref[...], ob_ref[...], oc)):
        y = jnp.dot(br.astype(BF16), wb_ref[n], preferred_element_type=F32)
        gate = jax.nn.sigmoid(jnp.dot(hb, wmg_ref[:, n * D_MODEL:(n + 1) * D_MODEL], preferred_element_type=F32))
        t = gate * y
        z = t if z is None else z + t
    o_ref[...] = x + jnp.dot(z.astype(BF16), wo_ref[...], preferred_element_type=F32)


def _merge(x2, nrm, oa, ob, oc, osel, ow, gc, expand_g, wmg, wb, wo, layer, tm=512):
    Tn = x2.shape[0]
    row = lambda d: pl.BlockSpec((tm, d), lambda i: (i, 0))
    const = lambda shape: pl.BlockSpec(shape, lambda i: (0,) * len(shape), pipeline_mode=pl.Buffered(1))
    weight = lambda shape: pl.BlockSpec((None,) + shape, lambda i: (layer,) + (0,) * len(shape),
                                        pipeline_mode=pl.Buffered(1))
    return pl.pallas_call(
        _merge_kernel,
        grid=(Tn // tm,),
        in_specs=[row(D_MODEL), const((1, D_MODEL)),
                  row(MIX_WIDTH), row(MIX_WIDTH), row(MIX_WIDTH), row(MIX_WIDTH), row(MIX_WIDTH),
                  row(CG_PAD),
                  const((2 * CG_PAD, 3 * MIX_WIDTH)),
                  weight((D_MODEL, 3 * D_MODEL)),
                  weight((3, MIX_WIDTH, D_MODEL)),
                  weight((D_MODEL, D_MODEL))],
        out_specs=row(D_MODEL),
        out_shape=jax.ShapeDtypeStruct((Tn, D_MODEL), F32),
        compiler_params=_params(1),
        name="merge",
    )(x2, nrm, oa, ob, oc, osel, ow, gc, expand_g, wmg, wb, wo)


def _mlp_kernel(x_ref, nrm_ref, wu_ref, wd_ref, o_ref, h_ref, acc_ref):
    f = pl.program_id(1)

    @pl.when(f == 0)
    def _():
        h_ref[...] = _rms(x_ref[...], nrm_ref[...]).astype(BF16)
        acc_ref[...] = jnp.zeros(acc_ref.shape, F32)

    u = jnp.dot(h_ref[...], wu_ref[...], preferred_element_type=F32)
    u = jnp.square(jnp.maximum(u, 0.0)).astype(BF16)
    acc_ref[...] += jnp.dot(u, wd_ref[...], preferred_element_type=F32)

    @pl.when(f == pl.num_programs(1) - 1)
    def _():
        o_ref[...] = x_ref[...] + acc_ref[...]


def _mlp(x2, nrm, wu, wd, layer, tm=1024, tf=1024):
    Tn = x2.shape[0]
    return pl.pallas_call(
        _mlp_kernel,
        grid=(Tn // tm, D_FF // tf),
        in_specs=[pl.BlockSpec((tm, D_MODEL), lambda i, f: (i, 0)),
                  pl.BlockSpec((1, D_MODEL), lambda i, f: (0, 0)),
                  pl.BlockSpec((None, D_MODEL, tf), lambda i, f: (layer, 0, f)),
                  pl.BlockSpec((None, tf, D_MODEL), lambda i, f: (layer, f, 0))],
        out_specs=pl.BlockSpec((tm, D_MODEL), lambda i, f: (i, 0)),
        out_shape=jax.ShapeDtypeStruct((Tn, D_MODEL), F32),
        scratch_shapes=[pltpu.VMEM((tm, D_MODEL), BF16), pltpu.VMEM((tm, D_MODEL), F32)],
        compiler_params=_params(2),
        name="mlp",
    )(x2, nrm, wu, wd)


def _overlap(n_cmp):
    c_start = np.arange(NC_PAD) * CMP_STRIDE
    j_start = np.arange(N_SEL_BLK) * SEL_BLOCK
    ov = ((c_start[None, :] < j_start[:, None] + SEL_BLOCK) & (c_start[None, :] + CMP_BLOCK > j_start[:, None])
          & (np.arange(NC_PAD)[None, :] < n_cmp))
    return jnp.asarray(ov.astype(np.float32), BF16)


def _gate_expand():
    e = np.zeros((2, CG_PAD, 3 * MIX_WIDTH), np.float32)
    for h in range(C_HEADS):
        for j in range(3):
            e[:, h * 3 + j, j * MIX_WIDTH + h * HEAD_DIM:j * MIX_WIDTH + (h + 1) * HEAD_DIM] = 1.0
    return jnp.asarray(e.reshape(2 * CG_PAD, 3 * MIX_WIDTH), BF16)


def kernel(x, w_in, qk_gain, diff_lambda, diff_subln, sinks, cmp_pos, cmp_w1, cmp_w2,
           w_branch, w_out, norm_mix, norm_mlp, w_up, w_down, rel_bias):
    B, S, _ = x.shape
    depth = w_in.shape[0]
    n_cmp = (S - CMP_BLOCK) // CMP_STRIDE + 1
    assert S % T_ROW == 0 and S // CMP_STRIDE == NC_PAD and S // SEL_BLOCK == N_SEL_BLK
    half = CMP_BLOCK // 2 * HEAD_DIM

    w_heads = w_in[:, :, :C_CG].astype(BF16)
    w_cg = jnp.pad(w_in[:, :, C_CG:C_CG + N_CG], ((0, 0), (0, 0), (0, CG_PAD - N_CG))).astype(BF16)
    w_mg = w_in[:, :, C_CG + N_CG:].astype(BF16)
    gains2 = jnp.concatenate([qk_gain, qk_gain], axis=-1)
    w1 = cmp_w1.astype(BF16).reshape(depth, 2, 2, half, CMP_HIDDEN)
    w2p = jnp.pad(cmp_w2, ((0, 0), (0, 0), (0, 0), (0, LANES - HEAD_DIM))).astype(BF16)
    pos = cmp_pos.reshape(depth, 2, 2, half)
    wb = w_branch.astype(BF16)
    wo = w_out.astype(BF16)
    wu = w_up.astype(BF16)
    wd = w_down.astype(BF16)

    bias_a = rel_bias[:, :A_HEADS]
    bias_b = rel_bias[:, A_HEADS:A_HEADS + B_HEADS]
    bias_c = rel_bias[:, A_HEADS + B_HEADS:]
    bt_a = _bias_tiles(bias_a, T_ROW, 3, None, 1, center=True)
    bt_b = _bias_tiles(bias_b, T_SWA, -(-B_WINDOW // T_SWA) + 1, B_WINDOW, GROUP, rows=B_WINDOW + T_SWA)
    bt_sel = _bias_tiles(bias_c, T_ROW, 3, None, GROUP, center=True)
    bt_win = _bias_tiles(bias_c, T_ROW, C_WINDOW // T_ROW + 1, C_WINDOW, GROUP, rows=C_WINDOW + T_ROW)
    ovl = _overlap(n_cmp)
    expand_g = _gate_expand()

    for layer in range(depth):
        lam_init = 0.8 - 0.6 * math.exp(-0.3 * layer)
        (qat, ka, vat, qbt, kb, vbt, qct, tkv, ksel, vselt, kwin, vwint, gc) = _inproj(
            x, norm_mix[layer][None], w_heads, w_cg, gains2[layer], layer)
        oa = _flash_attn(qat, ka, vat, bt_a, dl=diff_lambda[layer],
                   subln=diff_subln[layer][None], lam_init=lam_init, name="attn_diff")
        ob = _window_attn(qbt, kb, vbt, bt_b, n_sub=8, out_dtype=BF16,
                          sink=sinks[layer].reshape(KV_GROUPS, GROUP, 1, 1), name="attn_swa")
        kc, vct = _compress(tkv.reshape(B, 2, KV_GROUPS, NC_PAD, half), pos[layer],
                            w1[layer], w2p[layer], qk_gain[layer][5:6])
        ocmp, sel = _cmp_attn(qct, kc, vct, ovl, n_cmp, n_sub=4)
        osel = _flash_attn(qct, ksel, vselt, bt_sel, sel=sel, name="attn_sel")
        owin = _window_attn(qct, kwin, vwint, bt_win, n_sub=4, name="attn_win")
        f2 = lambda a: a.reshape(B * S, a.shape[-1])
        x2 = _merge(f2(x), norm_mix[layer][None], f2(oa), f2(ob), f2(ocmp), f2(osel), f2(owin), f2(gc),
                    expand_g, w_mg, wb, wo, layer)
        x2 = _mlp(x2, norm_mlp[layer][None], wu, wd, layer)
        x = x2.reshape(B, S, D_MODEL)
    return x
```

```python
import functools
import math

import numpy as np
import jax
import jax.numpy as jnp
from jax import lax
from jax.experimental import pallas as pl
from jax.experimental.pallas import tpu as pltpu

F32 = jnp.float32
BF16 = jnp.bfloat16

D_MODEL = 1024
HEAD_DIM = 64
A_HEADS = 4
A_VDIM = 128
B_HEADS = 8
B_WINDOW = 128
C_HEADS = 8
KV_GROUPS = 2
GROUP = 4
CMP_BLOCK = 32
CMP_STRIDE = 16
CMP_HIDDEN = 256
SEL_BLOCK = 64
N_SELECT = 16
C_WINDOW = 512
MIX_WIDTH = 512
D_FF = 4096
NUM_BUCKETS = 32
MAX_DISTANCE = 128
NEG_INF = -1e30
FORCE_BONUS = 1e4
EPS = 1e-6
LOG2E = 1.4426950408889634
QK_SCALE = HEAD_DIM ** -0.5 * LOG2E

C_AQ, C_AK, C_AV, C_BQ, C_BKV, C_CQ, C_CKV, C_CG = 0, 512, 1024, 1536, 2048, 2304, 2816, 3584
N_CG = C_HEADS * 3
LANES = 128
CG_PAD = LANES
NC_PAD = 256
N_SEL_BLK = 64
ONES_ROWS = 16
MASK_ROWS = 16

SCORE_SLOTS = 8

T_IN = 512
T_ROW = 256
T_SWA = 256
TV_SWA = 128

VMEM_LIMIT = 48 * 1024 * 1024


def _rms(x, gain):
    ms = jnp.mean(x * x, axis=-1, keepdims=True)
    return (x * lax.rsqrt(ms + EPS)) * gain


def _params(n_axes):
    return pltpu.CompilerParams(dimension_semantics=("arbitrary",) * n_axes,
                                vmem_limit_bytes=VMEM_LIMIT)


def _inproj_kernel(x_ref, nrm_ref, w_ref, wcg_ref, gains_ref,
                   qa_ref, ka_ref, va_ref, qb_ref, kb_ref, vb_ref, qc_ref,
                   tkv_ref, ksel_ref, vsel_ref, kwin_ref, vwin_ref, gc_ref):
    tm = x_ref.shape[1]
    hb = _rms(x_ref[0], nrm_ref[...]).astype(BF16)
    lane = lax.broadcasted_iota(jnp.int32, (tm, LANES), 1)
    first = lane < HEAD_DIM

    def ones_pad(width):
        return jnp.where(lax.broadcasted_iota(jnp.int32, (ONES_ROWS, width), 0) == 0, 1.0, 0.0).astype(BF16)

    def mm(c0, n):
        return jnp.dot(hb, w_ref[:, c0:c0 + n], preferred_element_type=F32)

    def norm_pair(y, gain_idx, scale=None):
        sq = y * y
        ms0 = jnp.sum(jnp.where(first, sq, 0.0), axis=-1, keepdims=True) * (1.0 / HEAD_DIM)
        ms1 = jnp.sum(jnp.where(first, 0.0, sq), axis=-1, keepdims=True) * (1.0 / HEAD_DIM)
        inv = jnp.where(first, lax.rsqrt(ms0 + EPS), lax.rsqrt(ms1 + EPS))
        out = (y * inv) * gains_ref[gain_idx:gain_idx + 1, :]
        return out if scale is None else out * scale

    def pairs(c0):
        r = mm(c0, 256)
        return r[:, :LANES], r[:, LANES:]

    def put_qt(ref, i0, i1, y):
        yt = y.T.astype(BF16)
        n_tiles = ref.shape[3]
        tq = tm // n_tiles
        for t in range(n_tiles):
            ref[0, i0, i1, t] = yt[:HEAD_DIM, tq * t:tq * (t + 1)]
            ref[0, i0, i1 + 1, t] = yt[HEAD_DIM:, tq * t:tq * (t + 1)]

    def put_vt(ref, g0, y, n_tiles):
        yt = y.T.astype(BF16)
        tk = tm // n_tiles
        for g in range(2):
            for t in range(n_tiles):
                ref[0, g0 + g, t, 0:HEAD_DIM, :] = yt[HEAD_DIM * g:HEAD_DIM * (g + 1), tk * t:tk * (t + 1)]
                ref[0, g0 + g, t, HEAD_DIM:HEAD_DIM + ONES_ROWS, :] = ones_pad(tk)

    for ch in range(2):
        for half, y in enumerate(pairs(C_AQ + 256 * ch)):
            put_qt(qa_ref, ch * 2 + half, 0, norm_pair(y, 0, QK_SCALE))
        for half, y in enumerate(pairs(C_AK + 256 * ch)):
            y = norm_pair(y, 1).astype(BF16)
            ka_ref[0, ch * 2 + half, 0] = y[:, :HEAD_DIM]
            ka_ref[0, ch * 2 + half, 1] = y[:, HEAD_DIM:]
        for half, y in enumerate(pairs(C_AV + 256 * ch)):
            yt = y.T.astype(BF16)
            for t in range(tm // T_ROW):
                va_ref[0, ch * 2 + half, t, 0:A_VDIM, :] = yt[:, T_ROW * t:T_ROW * (t + 1)]
                va_ref[0, ch * 2 + half, t, A_VDIM:A_VDIM + ONES_ROWS, :] = ones_pad(T_ROW)
        for half, y in enumerate(pairs(C_BQ + 256 * ch)):
            put_qt(qb_ref, ch, 2 * half, norm_pair(y, 2, QK_SCALE))
        for half, y in enumerate(pairs(C_CQ + 256 * ch)):
            put_qt(qc_ref, ch, 2 * half, norm_pair(y, 4, QK_SCALE))

    def put_k(ref, y):
        y = y.astype(BF16)
        ref[0, 0] = y[:, :HEAD_DIM]
        ref[0, 1] = y[:, HEAD_DIM:]

    yk, yv = pairs(C_BKV)
    put_k(kb_ref, norm_pair(yk, 3))
    put_vt(vb_ref, 0, yv, tm // TV_SWA)
    y0, y1 = pairs(C_CKV)
    for kv, y in enumerate((y0, y1)):
        tkv_ref[0, kv, 0] = y[:, :HEAD_DIM]
        tkv_ref[0, kv, 1] = y[:, HEAD_DIM:]
    yk, yv = pairs(C_CKV + 256)
    yk = norm_pair(yk, 6)
    row = lax.broadcasted_iota(jnp.int32, (tm, LANES), 0)
    hot = jnp.where(lane == HEAD_DIM + (row // SEL_BLOCK) % (T_ROW // SEL_BLOCK), 1.0, 0.0)
    ksel_ref[0, 0] = jnp.where(first, yk, hot).astype(BF16)
    ksel_ref[0, 1] = jnp.where(first, pltpu.roll(yk, HEAD_DIM, 1), hot).astype(BF16)
    put_vt(vsel_ref, 0, yv, tm // T_ROW)
    yk, yv = pairs(C_CKV + 512)
    put_k(kwin_ref, norm_pair(yk, 7))
    put_vt(vwin_ref, 0, yv, tm // T_ROW)

    gc_ref[0] = jax.nn.sigmoid(jnp.dot(hb, wcg_ref[...], preferred_element_type=F32))


def _inproj(x, nrm, w, wcg, gains2, layer):
    B, S, _ = x.shape
    tm = T_IN
    nt = S // tm
    sd = jax.ShapeDtypeStruct
    qt = lambda a, b, tq: sd((B, a, b, S // tq, HEAD_DIM, tq), BF16)
    kk = lambda n: sd((B, n, S, HEAD_DIM), BF16)
    vt = lambda n, dv, tk: sd((B, n, S // tk, dv + ONES_ROWS, tk), BF16)
    out_shape = (
        qt(A_HEADS, 2, T_ROW), sd((B, A_HEADS, 2, S, HEAD_DIM), BF16), vt(A_HEADS, A_VDIM, T_ROW),
        qt(KV_GROUPS, GROUP, T_SWA), kk(KV_GROUPS), vt(KV_GROUPS, HEAD_DIM, TV_SWA),
        qt(KV_GROUPS, GROUP, T_ROW),
        sd((B, 2, KV_GROUPS, S, HEAD_DIM), F32),
        sd((B, KV_GROUPS, S, LANES), BF16), vt(KV_GROUPS, HEAD_DIM, T_ROW),
        kk(KV_GROUPS), vt(KV_GROUPS, HEAD_DIM, T_ROW),
        sd((B, S, CG_PAD), F32),
    )
    s_qt = lambda a, b, tq: pl.BlockSpec((1, a, b, tm // tq, HEAD_DIM, tq), lambda b_, i: (b_, 0, 0, i, 0, 0))
    s_k = lambda n: pl.BlockSpec((1, n, tm, HEAD_DIM), lambda b_, i: (b_, 0, i, 0))
    s_k5 = lambda a, c: pl.BlockSpec((1, a, c, tm, HEAD_DIM), lambda b_, i: (b_, 0, 0, i, 0))
    s_vt = lambda n, dv, tk: pl.BlockSpec((1, n, tm // tk, dv + ONES_ROWS, tk), lambda b_, i: (b_, 0, i, 0, 0))
    row = lambda d: pl.BlockSpec((1, tm, d), lambda b_, i: (b_, i, 0))
    out_specs = (
        s_qt(A_HEADS, 2, T_ROW), s_k5(A_HEADS, 2), s_vt(A_HEADS, A_VDIM, T_ROW),
        s_qt(KV_GROUPS, GROUP, T_SWA), s_k(KV_GROUPS), s_vt(KV_GROUPS, HEAD_DIM, TV_SWA),
        s_qt(KV_GROUPS, GROUP, T_ROW), s_k5(2, KV_GROUPS),
        pl.BlockSpec((1, KV_GROUPS, tm, LANES), lambda b_, i: (b_, 0, i, 0)), s_vt(KV_GROUPS, HEAD_DIM, T_ROW),
        s_k(KV_GROUPS), s_vt(KV_GROUPS, HEAD_DIM, T_ROW),
        row(CG_PAD),
    )
    return pl.pallas_call(
        _inproj_kernel,
        grid=(B, nt),
        in_specs=[
            row(D_MODEL),
            pl.BlockSpec((1, D_MODEL), lambda b_, i: (0, 0)),
            pl.BlockSpec((None, D_MODEL, C_CG), lambda b_, i: (layer, 0, 0), pipeline_mode=pl.Buffered(1)),
            pl.BlockSpec((None, D_MODEL, CG_PAD), lambda b_, i: (layer, 0, 0)),
            pl.BlockSpec((8, LANES), lambda b_, i: (0, 0)),
        ],
        out_specs=out_specs,
        out_shape=out_shape,
        compiler_params=_params(2),
        name="inproj",
    )(x, nrm, w, wcg, gains2)


def _compress_kernel(t_ref, pos_ref, w1_ref, w2_ref, gain_ref, o_ref, ot_ref):
    t = t_ref[0, 0, 0]
    lo = (t + pos_ref[0, 0:1, :]).astype(BF16)
    hi = (t + pos_ref[0, 1:2, :]).astype(BF16)
    v = jnp.dot(lo, w1_ref[0, 0], preferred_element_type=F32)
    u = jnp.dot(hi, w1_ref[0, 1], preferred_element_type=F32)
    pre = v + pltpu.roll(u, NC_PAD - 1, 0)
    hcur = jax.nn.gelu(pre).astype(BF16)
    out = jnp.dot(hcur, w2_ref[0], preferred_element_type=F32)
    o64 = out[:, :HEAD_DIM]
    is_key = pl.program_id(1) == 0
    o_ref[0, 0, 0] = jnp.where(is_key, _rms(o64, gain_ref[...]), o64).astype(BF16)
    ot_ref[0, 0, 0] = out.T[:HEAD_DIM].astype(BF16)


def _compress(tkv16, pos, w1, w2p, gain):
    B = tkv16.shape[0]
    half = CMP_BLOCK // 2 * HEAD_DIM
    return pl.pallas_call(
        _compress_kernel,
        grid=(B, 2, KV_GROUPS),
        in_specs=[
            pl.BlockSpec((1, 1, 1, NC_PAD, half), lambda b, kv, g: (b, kv, g, 0, 0)),
            pl.BlockSpec((1, 2, half), lambda b, kv, g: (kv, 0, 0)),
            pl.BlockSpec((1, 2, half, CMP_HIDDEN), lambda b, kv, g: (kv, 0, 0, 0)),
            pl.BlockSpec((1, CMP_HIDDEN, LANES), lambda b, kv, g: (kv, 0, 0)),
            pl.BlockSpec((1, HEAD_DIM), lambda b, kv, g: (0, 0)),
        ],
        out_specs=(pl.BlockSpec((1, 1, 1, NC_PAD, HEAD_DIM), lambda b, kv, g: (b, kv, g, 0, 0)),
                   pl.BlockSpec((1, 1, 1, HEAD_DIM, NC_PAD), lambda b, kv, g: (b, kv, g, 0, 0))),
        out_shape=(jax.ShapeDtypeStruct((B, 2, KV_GROUPS, NC_PAD, HEAD_DIM), BF16),
                   jax.ShapeDtypeStruct((B, 2, KV_GROUPS, HEAD_DIM, NC_PAD), BF16)),
        compiler_params=_params(3),
        name="compress",
    )(tkv16, pos, w1, w2p, gain)


def _cmp_attn_kernel(qt_ref, kc_ref, vct_ref, ovl_ref, o_ref, sel_ref, score_ref, rank_ref, *, T, n_cmp, n_sub):
    R = GROUP
    tiles = [pl.program_id(2) * n_sub + t for t in range(n_sub)]
    j_idx = lax.broadcasted_iota(jnp.int32, (N_SEL_BLK, T), 0)
    lane_q = lax.broadcasted_iota(jnp.int32, (N_SEL_BLK, T), 1)

    def attend(rows):
        kc = kc_ref[0, 0, 0, 0:rows, :]
        vct = vct_ref[0, 0, 0, :, 0:rows]
        ovl = ovl_ref[:, 0:rows]
        c_idx = lax.broadcasted_iota(jnp.int32, (rows, T), 0)
        lane_k = lax.broadcasted_iota(jnp.int32, (rows, T), 1)
        for t, I in enumerate(tiles):
            cmask = (c_idx * CMP_STRIDE + (CMP_BLOCK - 1) <= I * T + lane_k) & (c_idx < n_cmp)
            scores = [jnp.dot(kc, qt_ref[0, 0, r, t], preferred_element_type=F32) for r in range(R)]
            probs = []
            for r in range(R):
                s = jnp.where(cmask, scores[r], NEG_INF)
                m = jnp.max(s, axis=0, keepdims=True)
                e = jnp.where(cmask, jnp.exp2(s - m), 0.0)
                l = jnp.sum(e, axis=0, keepdims=True)
                probs.append(e * (1.0 / jnp.where(l > 0.0, l, 1.0)))
            outs = [jnp.dot(vct, p.astype(BF16), preferred_element_type=F32) for p in probs]
            o_ref[0, t * T:(t + 1) * T, :] = jnp.concatenate(outs, axis=0).T
            psum = (probs[0] + probs[1]) + (probs[2] + probs[3])
            p_hi = psum.astype(BF16)
            p_lo = (psum - p_hi.astype(F32)).astype(BF16)
            imp = jnp.dot(ovl, p_hi, preferred_element_type=F32) + jnp.dot(ovl, p_lo, preferred_element_type=F32)
            tq = I * T + lane_q
            cur = tq // SEL_BLOCK
            valid = j_idx * SEL_BLOCK <= tq
            forced = (j_idx == 0) | (j_idx == cur) | (j_idx == cur - 1)
            score_ref[t] = jnp.where(valid, imp + jnp.where(forced, FORCE_BONUS, 0.0), NEG_INF)

    chunk = NC_PAD // 4
    vis_per_tile = T // CMP_STRIDE
    for v in range(4):
        lo = -(-(v * chunk) // vis_per_tile)
        hi = -(-((v + 1) * chunk) // vis_per_tile)
        assert lo % n_sub == 0 and hi % n_sub == 0

        @pl.when((tiles[0] >= lo) & (tiles[0] < hi))
        def _(v=v):
            attend((v + 1) * chunk)

    rank_ref[...] = jnp.zeros(rank_ref.shape, F32)
    n_live = (tiles[-1] + 1) * (T // SEL_BLOCK)
    sub = 8
    n_grp = N_SEL_BLK // sub
    j_loc = lax.broadcasted_iota(jnp.int32, (sub, T), 0)
    for c in range(n_grp):
        @pl.when(c * sub < n_live)
        def _(c=c):
            for t in range(n_sub):
                grps = [score_ref[t, g * sub:(g + 1) * sub, :] for g in range(n_grp)]
                cnts = [rank_ref[t, g * sub:(g + 1) * sub, :] for g in range(n_grp)]
                for jp in range(c * sub, (c + 1) * sub):
                    row = jnp.broadcast_to(score_ref[t, jp:jp + 1, :], (sub, T))
                    for g in range(n_grp):
                        if g * sub > jp:
                            beats = row >= grps[g]
                        elif (g + 1) * sub - 1 < jp:
                            beats = row > grps[g]
                        else:
                            beats = (row > grps[g]) | ((row == grps[g]) & (j_loc + g * sub > jp))
                        cnts[g] = cnts[g] + jnp.where(beats, 1.0, 0.0)
                for g in range(n_grp):
                    rank_ref[t, g * sub:(g + 1) * sub, :] = cnts[g]
    blk = T // SEL_BLOCK
    for t in range(n_sub):
        for j in range(N_SEL_BLK // blk):
            keep = rank_ref[t, j * blk:(j + 1) * blk, :] < N_SELECT
            slab = jnp.concatenate([jnp.where(keep, 0.0, NEG_INF), jnp.zeros((MASK_ROWS - blk, T), F32)], axis=0)
            sel_ref[0, 0, t, j] = slab.astype(BF16)


def _cmp_attn(qct, kc, vct, ovl, n_cmp, n_sub=2):
    B, G, R, NQ, _, T = qct.shape
    S = NQ * T
    return pl.pallas_call(
        functools.partial(_cmp_attn_kernel, T=T, n_cmp=n_cmp, n_sub=n_sub),
        grid=(B, G, NQ // n_sub),
        in_specs=[
            pl.BlockSpec((1, 1, R, n_sub, HEAD_DIM, T), lambda b, g, i: (b, g, 0, i, 0, 0)),
            pl.BlockSpec((1, 1, 1, NC_PAD, HEAD_DIM), lambda b, g, i: (b, 0, g, 0, 0)),
            pl.BlockSpec((1, 1, 1, HEAD_DIM, NC_PAD), lambda b, g, i: (b, 1, g, 0, 0)),
            pl.BlockSpec((N_SEL_BLK, NC_PAD), lambda b, g, i: (0, 0)),
        ],
        out_specs=(
            pl.BlockSpec((1, n_sub * T, R * HEAD_DIM), lambda b, g, i: (b, i, g)),
            pl.BlockSpec((1, 1, n_sub, NQ, MASK_ROWS, T), lambda b, g, i: (b, g, i, 0, 0, 0)),
        ),
        out_shape=(
            jax.ShapeDtypeStruct((B, S, G * R * HEAD_DIM), F32),
            jax.ShapeDtypeStruct((B, G, NQ, NQ, MASK_ROWS, T), BF16),
        ),
        scratch_shapes=[pltpu.VMEM((n_sub, N_SEL_BLK, T), F32), pltpu.VMEM((n_sub, N_SEL_BLK, T), F32)],
        compiler_params=_params(3),
        name="cmp_attn",
    )(qct, kc, vct, ovl)


def _flash_kernel(*refs, R, T, NQ, Dv, nrel, k_per_r, shared_bias, diff, has_sel, early_bias, lam_init):
    it = iter(refs)
    qt_ref, k_ref, vt_ref, bt_ref = (next(it) for _ in range(4))
    sel_ref = next(it) if has_sel else None
    dl_ref = next(it) if diff else None
    sub_ref = next(it) if diff else None
    o_ref, m_ref, acc_ref, s_ref, p_ref, al_ref, tmax_ref = (next(it) for _ in range(7))

    m_ref[...] = jnp.full(m_ref.shape, NEG_INF, F32)
    acc_ref[...] = jnp.zeros(acc_ref.shape, F32)
    bias_r = (lambda r: 0) if shared_bias else (lambda r: r)
    if has_sel:
        pad_rows = jnp.zeros((k_ref.shape[-1] - HEAD_DIM - MASK_ROWS, T), BF16)
    n_steps = NQ * (NQ + 1) // 2
    N_S = SCORE_SLOTS
    assert N_S % 2 == 0 and n_steps % N_S == 0

    def score_stage(I, J, slot, table):
        off = pl.multiple_of(J * T, T)
        rel = jnp.minimum(I - J, nrel - 1)
        for r in range(R):
            kt = k_ref[0, 0, r, pl.ds(off, T), :] if k_per_r else k_ref[0, 0, pl.ds(off, T), :]
            w = qt_ref[0, 0, r, I]
            if has_sel:
                w = jnp.concatenate([w, sel_ref[0, 0, I, J], pad_rows], axis=0)
            s = jnp.dot(kt, w, preferred_element_type=F32)
            if early_bias:
                if table:
                    s = s + bt_ref[0, rel, bias_r(r)]
                tmax_ref[slot, r] = jnp.max(s, axis=0, keepdims=True)
            s_ref[slot, r] = s

    def softmax_stage(I, J, s_slot, slot, table):
        for r in range(R):
            m_prev = m_ref[I, r]
            s = s_ref[s_slot, r]
            if early_bias:
                tile_max = tmax_ref[s_slot, r]
            else:
                if table:
                    s = s + bt_ref[0, jnp.minimum(I - J, nrel - 1), bias_r(r)]
                tile_max = jnp.max(s, axis=0, keepdims=True)
            m_new = jnp.maximum(m_prev, tile_max)
            al_ref[slot, r] = jnp.exp2(m_prev - m_new)
            p_ref[slot, r] = jnp.exp2(s - m_new).astype(BF16)
            m_ref[I, r] = m_new

    def value_stage(J, slot):
        vt = vt_ref[0, 0, J]
        return [jnp.dot(vt, p_ref[slot, r], preferred_element_type=F32) for r in range(R)]

    def accumulate(I, slot, pvs):
        for r in range(R):
            acc_ref[I, r] = al_ref[slot, r] * acc_ref[I, r] + pvs[r]

    def far_advance(I, J):
        wrap = J + 1 > I - (nrel - 1)
        return jnp.where(wrap, I + 1, I), jnp.where(wrap, 0, J + 1)

    def near_advance(I, J):
        on_diag, before_diag = J == I, J == I - 1
        nxt_i = jnp.where(on_diag, I + 1, jnp.where(before_diag, I, 0))
        nxt_j = jnp.where(on_diag, I, jnp.where(before_diag, I, 0))
        return nxt_i, nxt_j

    assert nrel == 3
    n_far = (NQ - 1) * (NQ - 2) // 2
    n_near = 2 * NQ - 1
    far_trips, far_left = divmod(n_far, N_S)
    assert far_left == 1 and (far_left + n_near) % N_S == 0
    near_trips = (far_left + n_near) // N_S

    def step(cur, prev, t, advance, table):
        other = 1 - t % 2
        nxt = advance(*cur)
        pvs = value_stage(prev[1], other)
        score_stage(jnp.minimum(nxt[0], NQ - 1), jnp.minimum(nxt[1], NQ - 1), (t + 1) % N_S, table)
        softmax_stage(cur[0], cur[1], t % N_S, t % 2, table)
        accumulate(prev[0], other, pvs)
        return nxt

    def make_trip(advance, table):
        def trip(u, carry):
            cur, prev = carry[:2], carry[2:]
            for t in range(N_S):
                cur, prev = step(cur, prev, t, advance, table), cur
            return (*cur, *prev)
        return trip

    first = (jnp.int32(nrel - 1), jnp.int32(0))
    score_stage(*first, 0, False)
    p_ref[1] = jnp.zeros(p_ref.shape[1:], BF16)
    al_ref[1] = jnp.ones(al_ref.shape[1:], F32)
    carry = lax.fori_loop(0, far_trips, make_trip(far_advance, False), (*first, *first))
    lax.fori_loop(0, near_trips, make_trip(near_advance, True), carry)
    accumulate(NQ - 1, 1, value_stage(NQ - 1, 1))

    if diff:
        dl = dl_ref[...]
        lam = (jnp.exp(jnp.sum(dl[0:1] * dl[1:2], keepdims=True))
               - jnp.exp(jnp.sum(dl[2:3] * dl[3:4], keepdims=True)) + lam_init)

    def finish(I, carry):
        outs = []
        for r in range(R):
            acc = acc_ref[I, r]
            outs.append(acc[:Dv] / acc[Dv:Dv + 1])
        rows = pl.ds(pl.multiple_of(I * T, T), T)
        if diff:
            o = (outs[0] - lam * outs[1]).T
            o_ref[0, rows, :] = (_rms(o, sub_ref[...]) * (1.0 - lam_init)).astype(o_ref.dtype)
        else:
            o_ref[0, rows, :] = jnp.concatenate(outs, axis=0).T
        return carry

    lax.fori_loop(0, NQ, finish, 0)


def _flash_attn(qt, k, vt, bias_tab, *, sel=None, dl=None, subln=None, lam_init=0.0, name="attn"):
    B, G, R, NQ, _, T = qt.shape
    S = NQ * T
    Dv = vt.shape[-2] - ONES_ROWS
    nrel, bias_heads = bias_tab.shape[1], bias_tab.shape[2]
    k_per_r = k.ndim == 5
    has_sel = sel is not None
    diff = dl is not None
    ins = [qt, k, vt, bias_tab]
    in_specs = [
        pl.BlockSpec((1, 1, R, NQ, HEAD_DIM, T), lambda b, g: (b, g, 0, 0, 0, 0)),
        (pl.BlockSpec((1, 1, R, S, HEAD_DIM), lambda b, g: (b, g, 0, 0, 0)) if k_per_r
         else pl.BlockSpec((1, 1, S, k.shape[-1]), lambda b, g: (b, g, 0, 0))),
        pl.BlockSpec((1, 1, NQ, Dv + ONES_ROWS, T), lambda b, g: (b, g, 0, 0, 0)),
        pl.BlockSpec((1, nrel, bias_heads, T, T), lambda b, g: (g, 0, 0, 0, 0)),
    ]
    if has_sel:
        ins.append(sel)
        in_specs.append(pl.BlockSpec((1, 1, NQ, NQ, MASK_ROWS, T), lambda b, g: (b, g, 0, 0, 0, 0)))
    if diff:
        ins += [dl, subln]
        in_specs += [pl.BlockSpec((4, HEAD_DIM), lambda b, g: (0, 0)),
                     pl.BlockSpec((1, A_VDIM), lambda b, g: (0, 0))]
    out_w = Dv if diff else R * Dv
    kern = functools.partial(_flash_kernel, R=R, T=T, NQ=NQ, Dv=Dv, nrel=nrel, k_per_r=k_per_r,
                             shared_bias=bias_heads == 1, diff=diff, has_sel=has_sel, early_bias=not diff,
                             lam_init=lam_init)
    return pl.pallas_call(
        kern,
        grid=(B, G),
        in_specs=in_specs,
        out_specs=pl.BlockSpec((1, S, out_w), lambda b, g: (b, 0, g)),
        out_shape=jax.ShapeDtypeStruct((B, S, G * out_w), BF16 if diff else F32),
        scratch_shapes=[pltpu.VMEM((NQ, R, 1, T), F32), pltpu.VMEM((NQ, R, Dv + ONES_ROWS, T), F32),
                        pltpu.VMEM((SCORE_SLOTS, R, T, T), F32), pltpu.VMEM((2, R, T, T), BF16),
                        pltpu.VMEM((2, R, 1, T), F32), pltpu.VMEM((SCORE_SLOTS, R, 1, T), F32)],
        compiler_params=_params(2),
        name=name,
    )(*ins)


def _window_kernel(*refs, R, T, NK, TV, Dv, n_sub, has_sink):
    it = iter(refs)
    qt_ref, k_ref, vt_ref, bt_ref = (next(it) for _ in range(4))
    sink_ref = next(it) if has_sink else None
    o_ref = next(it)
    heads = [(t, r) for t in range(n_sub) for r in range(R)]
    tile = [pl.program_id(2) * n_sub + t for t in range(n_sub)]
    entry = [jnp.minimum(I, -(-(NK - T) // T)) for I in tile]
    first_key = [pl.multiple_of(I * T - jnp.minimum(e * T, NK - T), TV) for I, e in zip(tile, entry)]
    keys = [k_ref[0, 0, pl.ds(fk, NK), :] for fk in first_key]
    scores = {(t, r): jnp.dot(keys[t], qt_ref[0, 0, r, t], preferred_element_type=F32) for t, r in heads}
    probs, maxes = {}, {}
    for t, r in heads:
        s = scores[t, r] + bt_ref[0, entry[t], r]
        maxes[t, r] = jnp.max(s, axis=0, keepdims=True)
        probs[t, r] = jnp.exp2(s - maxes[t, r]).astype(BF16)
    accs = {}
    for t, r in heads:
        acc = None
        for c in range(NK // TV):
            part = jnp.dot(vt_ref[0, 0, first_key[t] // TV + c], probs[t, r][c * TV:(c + 1) * TV],
                           preferred_element_type=F32)
            acc = part if acc is None else acc + part
        accs[t, r] = acc
    for t in range(n_sub):
        outs = []
        for r in range(R):
            num, l = accs[t, r][:Dv], accs[t, r][Dv:Dv + 1]
            if has_sink:
                sk = sink_ref[0, r] * LOG2E
                m_f = jnp.maximum(maxes[t, r], sk)
                w = jnp.exp2(maxes[t, r] - m_f)
                outs.append(num * w / (l * w + jnp.exp2(sk - m_f)))
            else:
                outs.append(num / l)
        o_ref[0, t * T:(t + 1) * T, :] = jnp.concatenate(outs, axis=0).T.astype(o_ref.dtype)


def _window_attn(qt, k, vt, bias_tab, *, n_sub=2, sink=None, out_dtype=F32, name="attn_window"):
    B, G, R, NQ, _, T = qt.shape
    S = NQ * T
    TV = vt.shape[-1]
    Dv = vt.shape[-2] - ONES_ROWS
    entries, NK = bias_tab.shape[1], bias_tab.shape[3]
    has_sink = sink is not None
    ins = [qt, k, vt, bias_tab]
    in_specs = [
        pl.BlockSpec((1, 1, R, n_sub, HEAD_DIM, T), lambda b, g, i: (b, g, 0, i, 0, 0)),
        pl.BlockSpec((1, 1, S, HEAD_DIM), lambda b, g, i: (b, g, 0, 0)),
        pl.BlockSpec((1, 1, S // TV, Dv + ONES_ROWS, TV), lambda b, g, i: (b, g, 0, 0, 0)),
        pl.BlockSpec((1, entries, R, NK, T), lambda b, g, i: (g, 0, 0, 0, 0)),
    ]
    if has_sink:
        ins.append(sink)
        in_specs.append(pl.BlockSpec((1, R, 1, 1), lambda b, g, i: (g, 0, 0, 0)))
    kern = functools.partial(_window_kernel, R=R, T=T, NK=NK, TV=TV, Dv=Dv, n_sub=n_sub, has_sink=has_sink)
    return pl.pallas_call(
        kern,
        grid=(B, G, NQ // n_sub),
        in_specs=in_specs,
        out_specs=pl.BlockSpec((1, n_sub * T, R * Dv), lambda b, g, i: (b, i, g)),
        out_shape=jax.ShapeDtypeStruct((B, S, G * R * Dv), out_dtype),
        compiler_params=_params(3),
        name=name,
    )(*ins)


def _bucket_np(dist):
    n = np.maximum(dist, 0)
    max_exact = NUM_BUCKETS // 2
    nf = np.maximum(n, 1).astype(np.float32)
    large = max_exact + (np.log(nf / max_exact) / math.log(MAX_DISTANCE / max_exact)
                         * (NUM_BUCKETS - max_exact)).astype(np.int32)
    large = np.minimum(large, NUM_BUCKETS - 1)
    return np.where(n < max_exact, n, large)


def _bucket_starts():
    b = _bucket_np(np.arange(4 * MAX_DISTANCE))
    assert (np.diff(b) >= 0).all() and b[-1] == NUM_BUCKETS - 1
    return [int(np.argmax(b >= i)) for i in range(NUM_BUCKETS)]


def _bias_kernel(tab_ref, o_ref, *, T, rows, nrel, window, back_max, center, starts):
    h = pl.program_id(0)
    base = tab_ref[NUM_BUCKETS - 1, h] * LOG2E if center else 0.0
    key = lax.broadcasted_iota(jnp.int32, (rows, T), 0)
    qry = lax.broadcasted_iota(jnp.int32, (rows, T), 1)
    for rel in range(nrel):
        dist = min(rel * T, back_max) + qry - key
        val = jnp.full((rows, T), tab_ref[0, h] * LOG2E - base, F32)
        for b in range(1, NUM_BUCKETS):
            val = jnp.where(dist >= starts[b], tab_ref[b, h] * LOG2E - base, val)
        allowed = dist >= 0
        if window is not None:
            allowed = allowed & (dist < window)
        o_ref[0, rel, 0] = jnp.where(allowed, val, NEG_INF)


def _bias_tiles(tab, T, nrel, window, R, rows=None, center=False):
    H = tab.shape[1]
    back_max = nrel * T if rows is None else rows - T
    rows = T if rows is None else rows
    return pl.pallas_call(
        functools.partial(_bias_kernel, T=T, rows=rows, nrel=nrel, window=window, back_max=back_max,
                          center=center, starts=_bucket_starts()),
        grid=(H,),
        in_specs=[pl.BlockSpec(memory_space=pltpu.SMEM)],
        out_specs=pl.BlockSpec((1, nrel, 1, rows, T), lambda h: (h // R, 0, h % R, 0, 0)),
        out_shape=jax.ShapeDtypeStruct((H // R, nrel, R, rows, T), F32),
        compiler_params=_params(1),
        name="bias_tiles",
    )(tab)


def _merge_kernel(x_ref, nrm_ref, oa_ref, ob_ref, oc_ref, os_ref, ow_ref, gc_ref, ex_ref, wmg_ref, wb_ref, wo_ref,
                  o_ref):
    x = x_ref[...]
    hb = _rms(x, nrm_ref[...]).astype(BF16)
    gc = gc_ref[...]
    g_hi = gc.astype(BF16)
    g_lo = (gc - g_hi.astype(F32)).astype(BF16)
    gx = jnp.dot(jnp.concatenate([g_hi, g_lo], axis=1), ex_ref[...], preferred_element_type=F32)
    oc = (gx[:, 0:MIX_WIDTH] * oc_ref[...] + gx[:, MIX_WIDTH:2 * MIX_WIDTH] * os_ref[...]
          + gx[:, 2 * MIX_WIDTH:3 * MIX_WIDTH] * ow_ref[...])
    z = None
    for n, br in enumerate((oa_ref[...], ob_ref[...], oc)):
        y = jnp.dot(br.astype(BF16), wb_ref[n], preferred_element_type=F32)
        gate = jax.nn.sigmoid(jnp.dot(hb, wmg_ref[:, n * D_MODEL:(n + 1) * D_MODEL], preferred_element_type=F32))
        t = gate * y
        z = t if z is None else z + t
    o_ref[...] = x + jnp.dot(z.astype(BF16), wo_ref[...], preferred_element_type=F32)


def _merge(x2, nrm, oa, ob, oc, osel, ow, gc, expand_g, wmg, wb, wo, layer, tm=512):
    Tn = x2.shape[0]
    row = lambda d: pl.BlockSpec((tm, d), lambda i: (i, 0))
    const = lambda shape: pl.BlockSpec(shape, lambda i: (0,) * len(shape), pipeline_mode=pl.Buffered(1))
    weight = lambda shape: pl.BlockSpec((None,) + shape, lambda i: (layer,) + (0,) * len(shape),
                                        pipeline_mode=pl.Buffered(1))
    return pl.pallas_call(
        _merge_kernel,
        grid=(Tn // tm,),
        in_specs=[row(D_MODEL), const((1, D_MODEL)),
                  row(MIX_WIDTH), row(MIX_WIDTH), row(MIX_WIDTH), row(MIX_WIDTH), row(MIX_WIDTH),
                  row(CG_PAD),
                  const((2 * CG_PAD, 3 * MIX_WIDTH)),
                  weight((D_MODEL, 3 * D_MODEL)),
                  weight((3, MIX_WIDTH, D_MODEL)),
                  weight((D_MODEL, D_MODEL))],
        out_specs=row(D_MODEL),
        out_shape=jax.ShapeDtypeStruct((Tn, D_MODEL), F32),
        compiler_params=_params(1),
        name="merge",
    )(x2, nrm, oa, ob, oc, osel, ow, gc, expand_g, wmg, wb, wo)


def _mlp_kernel(x_ref, nrm_ref, wu_ref, wd_ref, o_ref, h_ref, acc_ref):
    f = pl.program_id(1)

    @pl.when(f == 0)
    def _():
        h_ref[...] = _rms(x_ref[...], nrm_ref[...]).astype(BF16)
        acc_ref[...] = jnp.zeros(acc_ref.shape, F32)

    u = jnp.dot(h_ref[...], wu_ref[...], preferred_element_type=F32)
    u = jnp.square(jnp.maximum(u, 0.0)).astype(BF16)
    acc_ref[...] += jnp.dot(u, wd_ref[...], preferred_element_type=F32)

    @pl.when(f == pl.num_programs(1) - 1)
    def _():
        o_ref[...] = x_ref[...] + acc_ref[...]


def _mlp(x2, nrm, wu, wd, layer, tm=1024, tf=1024):
    Tn = x2.shape[0]
    return pl.pallas_call(
        _mlp_kernel,
        grid=(Tn // tm, D_FF // tf),
        in_specs=[pl.BlockSpec((tm, D_MODEL), lambda i, f: (i, 0)),
                  pl.BlockSpec((1, D_MODEL), lambda i, f: (0, 0)),
                  pl.BlockSpec((None, D_MODEL, tf), lambda i, f: (layer, 0, f)),
                  pl.BlockSpec((None, tf, D_MODEL), lambda i, f: (layer, f, 0))],
        out_specs=pl.BlockSpec((tm, D_MODEL), lambda i, f: (i, 0)),
        out_shape=jax.ShapeDtypeStruct((Tn, D_MODEL), F32),
        scratch_shapes=[pltpu.VMEM((tm, D_MODEL), BF16), pltpu.VMEM((tm, D_MODEL), F32)],
        compiler_params=_params(2),
        name="mlp",
    )(x2, nrm, wu, wd)


def _split_w_in_kernel(w_ref, heads_ref, cg_ref, mg_ref):
    w = w_ref[...]
    heads_ref[...] = w[:, :C_CG].astype(BF16)
    lane = lax.broadcasted_iota(jnp.int32, (w.shape[0], CG_PAD), 1)
    cg_ref[...] = jnp.where(lane < N_CG, w[:, C_CG:C_CG + CG_PAD], 0.0).astype(BF16)
    mg_ref[...] = w[:, C_CG + N_CG:].astype(BF16)


def _split_w_in(w_in, rows=256):
    depth, d, cols = w_in.shape
    n_mg = cols - C_CG - N_CG
    spec = lambda n: pl.BlockSpec((None, rows, n), lambda l, i: (l, i, 0))
    return pl.pallas_call(
        _split_w_in_kernel,
        grid=(depth, d // rows),
        in_specs=[spec(cols)],
        out_specs=(spec(C_CG), spec(CG_PAD), spec(n_mg)),
        out_shape=(jax.ShapeDtypeStruct((depth, d, C_CG), BF16), jax.ShapeDtypeStruct((depth, d, CG_PAD), BF16),
                   jax.ShapeDtypeStruct((depth, d, n_mg), BF16)),
        compiler_params=_params(2),
        name="split_w_in",
    )(w_in)


def _overlap(n_cmp):
    c_start = np.arange(NC_PAD) * CMP_STRIDE
    j_start = np.arange(N_SEL_BLK) * SEL_BLOCK
    ov = ((c_start[None, :] < j_start[:, None] + SEL_BLOCK) & (c_start[None, :] + CMP_BLOCK > j_start[:, None])
          & (np.arange(NC_PAD)[None, :] < n_cmp))
    return jnp.asarray(ov.astype(np.float32), BF16)


def _gate_expand():
    e = np.zeros((2, CG_PAD, 3 * MIX_WIDTH), np.float32)
    for h in range(C_HEADS):
        for j in range(3):
            e[:, h * 3 + j, j * MIX_WIDTH + h * HEAD_DIM:j * MIX_WIDTH + (h + 1) * HEAD_DIM] = 1.0
    return jnp.asarray(e.reshape(2 * CG_PAD, 3 * MIX_WIDTH), BF16)


def kernel(x, w_in, qk_gain, diff_lambda, diff_subln, sinks, cmp_pos, cmp_w1, cmp_w2,
           w_branch, w_out, norm_mix, norm_mlp, w_up, w_down, rel_bias):
    B, S, _ = x.shape
    depth = w_in.shape[0]
    n_cmp = (S - CMP_BLOCK) // CMP_STRIDE + 1
    assert S % T_ROW == 0 and S // CMP_STRIDE == NC_PAD and S // SEL_BLOCK == N_SEL_BLK
    half = CMP_BLOCK // 2 * HEAD_DIM

    w_heads, w_cg, w_mg = _split_w_in(w_in)
    gains2 = jnp.concatenate([qk_gain, qk_gain], axis=-1)
    w1 = cmp_w1.astype(BF16).reshape(depth, 2, 2, half, CMP_HIDDEN)
    w2p = jnp.pad(cmp_w2, ((0, 0), (0, 0), (0, 0), (0, LANES - HEAD_DIM))).astype(BF16)
    pos = cmp_pos.reshape(depth, 2, 2, half)
    wb = w_branch.astype(BF16)
    wo = w_out.astype(BF16)
    wu = w_up.astype(BF16)
    wd = w_down.astype(BF16)

    bias_a = rel_bias[:, :A_HEADS]
    bias_b = rel_bias[:, A_HEADS:A_HEADS + B_HEADS]
    bias_c = rel_bias[:, A_HEADS + B_HEADS:]
    bt_a = _bias_tiles(bias_a, T_ROW, 3, None, 1, center=True)
    bt_b = _bias_tiles(bias_b, T_SWA, -(-B_WINDOW // T_SWA) + 1, B_WINDOW, GROUP, rows=B_WINDOW + T_SWA)
    bt_sel = _bias_tiles(bias_c, T_ROW, 3, None, GROUP, center=True)
    bt_win = _bias_tiles(bias_c, T_ROW, C_WINDOW // T_ROW + 1, C_WINDOW, GROUP, rows=C_WINDOW + T_ROW)
    ovl = _overlap(n_cmp)
    expand_g = _gate_expand()

    for layer in range(depth):
        lam_init = 0.8 - 0.6 * math.exp(-0.3 * layer)
        (qat, ka, vat, qbt, kb, vbt, qct, tkv, ksel, vselt, kwin, vwint, gc) = _inproj(
            x, norm_mix[layer][None], w_heads, w_cg, gains2[layer], layer)
        oa = _flash_attn(qat, ka, vat, bt_a, dl=diff_lambda[layer],
                   subln=diff_subln[layer][None], lam_init=lam_init, name="attn_diff")
        ob = _window_attn(qbt, kb, vbt, bt_b, n_sub=8, out_dtype=BF16,
                          sink=sinks[layer].reshape(KV_GROUPS, GROUP, 1, 1), name="attn_swa")
        kc, vct = _compress(tkv.reshape(B, 2, KV_GROUPS, NC_PAD, half), pos[layer],
                            w1[layer], w2p[layer], qk_gain[layer][5:6])
        ocmp, sel = _cmp_attn(qct, kc, vct, ovl, n_cmp, n_sub=4)
        osel = _flash_attn(qct, ksel, vselt, bt_sel, sel=sel, name="attn_sel")
        owin = _window_attn(qct, kwin, vwint, bt_win, n_sub=4, name="attn_win")
        f2 = lambda a: a.reshape(B * S, a.shape[-1])
        x2 = _merge(f2(x), norm_mix[layer][None], f2(oa), f2(ob), f2(ocmp), f2(osel), f2(owin), f2(gc),
                    expand_g, w_mg, wb, wo, layer)
        x2 = _mlp(x2, norm_mlp[layer][None], wu, wd, layer)
        x = x2.reshape(B, S, D_MODEL)
    return x
```

```python
import functools
import math

import numpy as np
import jax
import jax.numpy as jnp
from jax import lax
from jax.experimental import pallas as pl
from jax.experimental.pallas import tpu as pltpu

F32 = jnp.float32
BF16 = jnp.bfloat16

D_MODEL = 1024
HEAD_DIM = 64
A_HEADS = 4
A_VDIM = 128
B_HEADS = 8
B_WINDOW = 128
C_HEADS = 8
KV_GROUPS = 2
GROUP = 4
CMP_BLOCK = 32
CMP_STRIDE = 16
CMP_HIDDEN = 256
SEL_BLOCK = 64
N_SELECT = 16
C_WINDOW = 512
MIX_WIDTH = 512
D_FF = 4096
NUM_BUCKETS = 32
MAX_DISTANCE = 128
NEG_INF = -1e30
FORCE_BONUS = 1e4
EPS = 1e-6
LOG2E = 1.4426950408889634
QK_SCALE = HEAD_DIM ** -0.5 * LOG2E

C_AQ, C_AK, C_AV, C_BQ, C_BKV, C_CQ, C_CKV, C_CG = 0, 512, 1024, 1536, 2048, 2304, 2816, 3584
N_CG = C_HEADS * 3
LANES = 128
CG_PAD = LANES
NC_PAD = 256
N_SEL_BLK = 64
ONES_ROWS = 16
MASK_ROWS = 16

SCORE_SLOTS = 8

T_IN = 512
T_ROW = 256
T_SWA = 256
TV_SWA = 128

VMEM_LIMIT = 48 * 1024 * 1024


def _rms(x, gain):
    ms = jnp.mean(x * x, axis=-1, keepdims=True)
    return (x * lax.rsqrt(ms + EPS)) * gain


def _params(n_axes):
    return pltpu.CompilerParams(dimension_semantics=("arbitrary",) * n_axes,
                                vmem_limit_bytes=VMEM_LIMIT)


def _inproj_kernel(x_ref, nrm_ref, w_ref, wcg_ref, gains_ref,
                   qa_ref, ka_ref, va_ref, qb_ref, kb_ref, vb_ref, qc_ref,
                   tkv_ref, ksel_ref, vsel_ref, kwin_ref, vwin_ref, gc_ref):
    tm = x_ref.shape[1]
    hb = _rms(x_ref[0], nrm_ref[...]).astype(BF16)
    lane = lax.broadcasted_iota(jnp.int32, (tm, LANES), 1)
    first = lane < HEAD_DIM

    def ones_pad(width):
        return jnp.where(lax.broadcasted_iota(jnp.int32, (ONES_ROWS, width), 0) == 0, 1.0, 0.0).astype(BF16)

    def mm(c0, n):
        return jnp.dot(hb, w_ref[:, c0:c0 + n], preferred_element_type=F32)

    def norm_pair(y, gain_idx, scale=None):
        sq = y * y
        ms0 = jnp.sum(jnp.where(first, sq, 0.0), axis=-1, keepdims=True) * (1.0 / HEAD_DIM)
        ms1 = jnp.sum(jnp.where(first, 0.0, sq), axis=-1, keepdims=True) * (1.0 / HEAD_DIM)
        inv = jnp.where(first, lax.rsqrt(ms0 + EPS), lax.rsqrt(ms1 + EPS))
        out = (y * inv) * gains_ref[gain_idx:gain_idx + 1, :]
        return out if scale is None else out * scale

    def pairs(c0):
        r = mm(c0, 256)
        return r[:, :LANES], r[:, LANES:]

    def put_qt(ref, i0, i1, y):
        yt = y.T.astype(BF16)
        n_tiles = ref.shape[3]
        tq = tm // n_tiles
        for t in range(n_tiles):
            ref[0, i0, i1, t] = yt[:HEAD_DIM, tq * t:tq * (t + 1)]
            ref[0, i0, i1 + 1, t] = yt[HEAD_DIM:, tq * t:tq * (t + 1)]

    def put_vt(ref, g0, y, n_tiles):
        yt = y.T.astype(BF16)
        tk = tm // n_tiles
        for g in range(2):
            for t in range(n_tiles):
                ref[0, g0 + g, t, 0:HEAD_DIM, :] = yt[HEAD_DIM * g:HEAD_DIM * (g + 1), tk * t:tk * (t + 1)]
                ref[0, g0 + g, t, HEAD_DIM:HEAD_DIM + ONES_ROWS, :] = ones_pad(tk)

    for ch in range(2):
        for half, y in enumerate(pairs(C_AQ + 256 * ch)):
            put_qt(qa_ref, ch * 2 + half, 0, norm_pair(y, 0, QK_SCALE))
        for half, y in enumerate(pairs(C_AK + 256 * ch)):
            y = norm_pair(y, 1).astype(BF16)
            ka_ref[0, ch * 2 + half, 0] = y[:, :HEAD_DIM]
            ka_ref[0, ch * 2 + half, 1] = y[:, HEAD_DIM:]
        for half, y in enumerate(pairs(C_AV + 256 * ch)):
            yt = y.T.astype(BF16)
            for t in range(tm // T_ROW):
                va_ref[0, ch * 2 + half, t, 0:A_VDIM, :] = yt[:, T_ROW * t:T_ROW * (t + 1)]
                va_ref[0, ch * 2 + half, t, A_VDIM:A_VDIM + ONES_ROWS, :] = ones_pad(T_ROW)
        for half, y in enumerate(pairs(C_BQ + 256 * ch)):
            put_qt(qb_ref, ch, 2 * half, norm_pair(y, 2, QK_SCALE))
        for half, y in enumerate(pairs(C_CQ + 256 * ch)):
            put_qt(qc_ref, ch, 2 * half, norm_pair(y, 4, QK_SCALE))

    def put_k(ref, y):
        y = y.astype(BF16)
        ref[0, 0] = y[:, :HEAD_DIM]
        ref[0, 1] = y[:, HEAD_DIM:]

    yk, yv = pairs(C_BKV)
    put_k(kb_ref, norm_pair(yk, 3))
    put_vt(vb_ref, 0, yv, tm // TV_SWA)
    y0, y1 = pairs(C_CKV)
    for kv, y in enumerate((y0, y1)):
        tkv_ref[0, kv, 0] = y[:, :HEAD_DIM]
        tkv_ref[0, kv, 1] = y[:, HEAD_DIM:]
    yk, yv = pairs(C_CKV + 256)
    yk = norm_pair(yk, 6)
    row = lax.broadcasted_iota(jnp.int32, (tm, LANES), 0)
    hot = jnp.where(lane == HEAD_DIM + (row // SEL_BLOCK) % (T_ROW // SEL_BLOCK), 1.0, 0.0)
    ksel_ref[0, 0] = jnp.where(first, yk, hot).astype(BF16)
    ksel_ref[0, 1] = jnp.where(first, pltpu.roll(yk, HEAD_DIM, 1), hot).astype(BF16)
    put_vt(vsel_ref, 0, yv, tm // T_ROW)
    yk, yv = pairs(C_CKV + 512)
    put_k(kwin_ref, norm_pair(yk, 7))
    put_vt(vwin_ref, 0, yv, tm // T_ROW)

    gc_ref[0] = jax.nn.sigmoid(jnp.dot(hb, wcg_ref[...], preferred_element_type=F32))


def _inproj(x, nrm, w, wcg, gains2, layer):
    B, S, _ = x.shape
    tm = T_IN
    nt = S // tm
    sd = jax.ShapeDtypeStruct
    qt = lambda a, b, tq: sd((B, a, b, S // tq, HEAD_DIM, tq), BF16)
    kk = lambda n: sd((B, n, S, HEAD_DIM), BF16)
    vt = lambda n, dv, tk: sd((B, n, S // tk, dv + ONES_ROWS, tk), BF16)
    out_shape = (
        qt(A_HEADS, 2, T_ROW), sd((B, A_HEADS, 2, S, HEAD_DIM), BF16), vt(A_HEADS, A_VDIM, T_ROW),
        qt(KV_GROUPS, GROUP, T_SWA), kk(KV_GROUPS), vt(KV_GROUPS, HEAD_DIM, TV_SWA),
        qt(KV_GROUPS, GROUP, T_ROW),
        sd((B, 2, KV_GROUPS, S, HEAD_DIM), F32),
        sd((B, KV_GROUPS, S, LANES), BF16), vt(KV_GROUPS, HEAD_DIM, T_ROW),
        kk(KV_GROUPS), vt(KV_GROUPS, HEAD_DIM, T_ROW),
        sd((B, S, CG_PAD), F32),
    )
    s_qt = lambda a, b, tq: pl.BlockSpec((1, a, b, tm // tq, HEAD_DIM, tq), lambda b_, i: (b_, 0, 0, i, 0, 0))
    s_k = lambda n: pl.BlockSpec((1, n, tm, HEAD_DIM), lambda b_, i: (b_, 0, i, 0))
    s_k5 = lambda a, c: pl.BlockSpec((1, a, c, tm, HEAD_DIM), lambda b_, i: (b_, 0, 0, i, 0))
    s_vt = lambda n, dv, tk: pl.BlockSpec((1, n, tm // tk, dv + ONES_ROWS, tk), lambda b_, i: (b_, 0, i, 0, 0))
    row = lambda d: pl.BlockSpec((1, tm, d), lambda b_, i: (b_, i, 0))
    out_specs = (
        s_qt(A_HEADS, 2, T_ROW), s_k5(A_HEADS, 2), s_vt(A_HEADS, A_VDIM, T_ROW),
        s_qt(KV_GROUPS, GROUP, T_SWA), s_k(KV_GROUPS), s_vt(KV_GROUPS, HEAD_DIM, TV_SWA),
        s_qt(KV_GROUPS, GROUP, T_ROW), s_k5(2, KV_GROUPS),
        pl.BlockSpec((1, KV_GROUPS, tm, LANES), lambda b_, i: (b_, 0, i, 0)), s_vt(KV_GROUPS, HEAD_DIM, T_ROW),
        s_k(KV_GROUPS), s_vt(KV_GROUPS, HEAD_DIM, T_ROW),
        row(CG_PAD),
    )
    return pl.pallas_call(
        _inproj_kernel,
        grid=(B, nt),
        in_specs=[
            row(D_MODEL),
            pl.BlockSpec((1, D_MODEL), lambda b_, i: (0, 0)),
            pl.BlockSpec((None, D_MODEL, C_CG), lambda b_, i: (layer, 0, 0), pipeline_mode=pl.Buffered(1)),
            pl.BlockSpec((None, D_MODEL, CG_PAD), lambda b_, i: (layer, 0, 0)),
            pl.BlockSpec((8, LANES), lambda b_, i: (0, 0)),
        ],
        out_specs=out_specs,
        out_shape=out_shape,
        compiler_params=_params(2),
        name="inproj",
    )(x, nrm, w, wcg, gains2)


def _compress_kernel(t_ref, pos_ref, w1_ref, w2_ref, gain_ref, o_ref, ot_ref):
    t = jnp.concatenate([t_ref[0, 0, 0, pl.ds(b, NC_PAD, stride=CMP_STRIDE), :] for b in range(CMP_STRIDE)], axis=1)
    lo = (t + pos_ref[0, 0:1, :]).astype(BF16)
    hi = (t + pos_ref[0, 1:2, :]).astype(BF16)
    v = jnp.dot(lo, w1_ref[0, 0], preferred_element_type=F32)
    u = jnp.dot(hi, w1_ref[0, 1], preferred_element_type=F32)
    pre = v + pltpu.roll(u, NC_PAD - 1, 0)
    hcur = jax.nn.gelu(pre).astype(BF16)
    out = jnp.dot(hcur, w2_ref[0], preferred_element_type=F32)
    o64 = out[:, :HEAD_DIM]
    is_key = pl.program_id(1) == 0
    o_ref[0, 0, 0] = jnp.where(is_key, _rms(o64, gain_ref[...]), o64).astype(BF16)
    ot_ref[0, 0, 0] = out.T[:HEAD_DIM].astype(BF16)


def _compress(tkv, pos, w1, w2p, gain):
    B, S = tkv.shape[0], tkv.shape[3]
    half = CMP_BLOCK // 2 * HEAD_DIM
    return pl.pallas_call(
        _compress_kernel,
        grid=(B, 2, KV_GROUPS),
        in_specs=[
            pl.BlockSpec((1, 1, 1, S, HEAD_DIM), lambda b, kv, g: (b, kv, g, 0, 0)),
            pl.BlockSpec((1, 2, half), lambda b, kv, g: (kv, 0, 0)),
            pl.BlockSpec((1, 2, half, CMP_HIDDEN), lambda b, kv, g: (kv, 0, 0, 0)),
            pl.BlockSpec((1, CMP_HIDDEN, LANES), lambda b, kv, g: (kv, 0, 0)),
            pl.BlockSpec((1, HEAD_DIM), lambda b, kv, g: (0, 0)),
        ],
        out_specs=(pl.BlockSpec((1, 1, 1, NC_PAD, HEAD_DIM), lambda b, kv, g: (b, kv, g, 0, 0)),
                   pl.BlockSpec((1, 1, 1, HEAD_DIM, NC_PAD), lambda b, kv, g: (b, kv, g, 0, 0))),
        out_shape=(jax.ShapeDtypeStruct((B, 2, KV_GROUPS, NC_PAD, HEAD_DIM), BF16),
                   jax.ShapeDtypeStruct((B, 2, KV_GROUPS, HEAD_DIM, NC_PAD), BF16)),
        compiler_params=_params(3),
        name="compress",
    )(tkv, pos, w1, w2p, gain)


def _cmp_attn_kernel(qt_ref, kc_ref, vct_ref, ovl_ref, o_ref, sel_ref, score_ref, rank_ref, *, T, n_cmp, n_sub):
    R = GROUP
    tiles = [pl.program_id(2) * n_sub + t for t in range(n_sub)]
    j_idx = lax.broadcasted_iota(jnp.int32, (N_SEL_BLK, T), 0)
    lane_q = lax.broadcasted_iota(jnp.int32, (N_SEL_BLK, T), 1)

    def attend(rows):
        kc = kc_ref[0, 0, 0, 0:rows, :]
        vct = vct_ref[0, 0, 0, :, 0:rows]
        ovl = ovl_ref[:, 0:rows]
        c_idx = lax.broadcasted_iota(jnp.int32, (rows, T), 0)
        lane_k = lax.broadcasted_iota(jnp.int32, (rows, T), 1)
        for t, I in enumerate(tiles):
            cmask = (c_idx * CMP_STRIDE + (CMP_BLOCK - 1) <= I * T + lane_k) & (c_idx < n_cmp)
            scores = [jnp.dot(kc, qt_ref[0, 0, r, t], preferred_element_type=F32) for r in range(R)]
            probs = []
            for r in range(R):
                s = jnp.where(cmask, scores[r], NEG_INF)
                m = jnp.max(s, axis=0, keepdims=True)
                e = jnp.where(cmask, jnp.exp2(s - m), 0.0)
                l = jnp.sum(e, axis=0, keepdims=True)
                probs.append(e * (1.0 / jnp.where(l > 0.0, l, 1.0)))
            outs = [jnp.dot(vct, p.astype(BF16), preferred_element_type=F32) for p in probs]
            o_ref[0, t * T:(t + 1) * T, :] = jnp.concatenate(outs, axis=0).T
            psum = (probs[0] + probs[1]) + (probs[2] + probs[3])
            p_hi = psum.astype(BF16)
            p_lo = (psum - p_hi.astype(F32)).astype(BF16)
            imp = jnp.dot(ovl, p_hi, preferred_element_type=F32) + jnp.dot(ovl, p_lo, preferred_element_type=F32)
            tq = I * T + lane_q
            cur = tq // SEL_BLOCK
            valid = j_idx * SEL_BLOCK <= tq
            forced = (j_idx == 0) | (j_idx == cur) | (j_idx == cur - 1)
            score_ref[t] = jnp.where(valid, imp + jnp.where(forced, FORCE_BONUS, 0.0), NEG_INF)

    chunk = NC_PAD // 4
    vis_per_tile = T // CMP_STRIDE
    for v in range(4):
        lo = -(-(v * chunk) // vis_per_tile)
        hi = -(-((v + 1) * chunk) // vis_per_tile)
        assert lo % n_sub == 0 and hi % n_sub == 0

        @pl.when((tiles[0] >= lo) & (tiles[0] < hi))
        def _(v=v):
            attend((v + 1) * chunk)

    rank_ref[...] = jnp.zeros(rank_ref.shape, F32)
    n_live = (tiles[-1] + 1) * (T // SEL_BLOCK)
    sub = 8
    n_grp = N_SEL_BLK // sub
    j_loc = lax.broadcasted_iota(jnp.int32, (sub, T), 0)
    for c in range(n_grp):
        @pl.when(c * sub < n_live)
        def _(c=c):
            for t in range(n_sub):
                grps = [score_ref[t, g * sub:(g + 1) * sub, :] for g in range(n_grp)]
                cnts = [rank_ref[t, g * sub:(g + 1) * sub, :] for g in range(n_grp)]
                for jp in range(c * sub, (c + 1) * sub):
                    row = jnp.broadcast_to(score_ref[t, jp:jp + 1, :], (sub, T))
                    for g in range(n_grp):
                        if g * sub > jp:
                            beats = row >= grps[g]
                        elif (g + 1) * sub - 1 < jp:
                            beats = row > grps[g]
                        else:
                            beats = (row > grps[g]) | ((row == grps[g]) & (j_loc + g * sub > jp))
                        cnts[g] = cnts[g] + jnp.where(beats, 1.0, 0.0)
                for g in range(n_grp):
                    rank_ref[t, g * sub:(g + 1) * sub, :] = cnts[g]
    blk = T // SEL_BLOCK
    for t in range(n_sub):
        for j in range(N_SEL_BLK // blk):
            keep = rank_ref[t, j * blk:(j + 1) * blk, :] < N_SELECT
            slab = jnp.concatenate([jnp.where(keep, 0.0, NEG_INF), jnp.zeros((MASK_ROWS - blk, T), F32)], axis=0)
            sel_ref[0, 0, t, j] = slab.astype(BF16)


def _cmp_attn(qct, kc, vct, ovl, n_cmp, n_sub=2):
    B, G, R, NQ, _, T = qct.shape
    S = NQ * T
    return pl.pallas_call(
        functools.partial(_cmp_attn_kernel, T=T, n_cmp=n_cmp, n_sub=n_sub),
        grid=(B, G, NQ // n_sub),
        in_specs=[
            pl.BlockSpec((1, 1, R, n_sub, HEAD_DIM, T), lambda b, g, i: (b, g, 0, i, 0, 0)),
            pl.BlockSpec((1, 1, 1, NC_PAD, HEAD_DIM), lambda b, g, i: (b, 0, g, 0, 0)),
            pl.BlockSpec((1, 1, 1, HEAD_DIM, NC_PAD), lambda b, g, i: (b, 1, g, 0, 0)),
            pl.BlockSpec((N_SEL_BLK, NC_PAD), lambda b, g, i: (0, 0)),
        ],
        out_specs=(
            pl.BlockSpec((1, n_sub * T, R * HEAD_DIM), lambda b, g, i: (b, i, g)),
            pl.BlockSpec((1, 1, n_sub, NQ, MASK_ROWS, T), lambda b, g, i: (b, g, i, 0, 0, 0)),
        ),
        out_shape=(
            jax.ShapeDtypeStruct((B, S, G * R * HEAD_DIM), F32),
            jax.ShapeDtypeStruct((B, G, NQ, NQ, MASK_ROWS, T), BF16),
        ),
        scratch_shapes=[pltpu.VMEM((n_sub, N_SEL_BLK, T), F32), pltpu.VMEM((n_sub, N_SEL_BLK, T), F32)],
        compiler_params=_params(3),
        name="cmp_attn",
    )(qct, kc, vct, ovl)


def _flash_kernel(*refs, R, T, NQ, Dv, nrel, k_per_r, shared_bias, diff, has_sel, early_bias, lam_init):
    it = iter(refs)
    qt_ref, k_ref, vt_ref, bt_ref = (next(it) for _ in range(4))
    sel_ref = next(it) if has_sel else None
    dl_ref = next(it) if diff else None
    sub_ref = next(it) if diff else None
    o_ref, m_ref, acc_ref, s_ref, p_ref, al_ref, tmax_ref = (next(it) for _ in range(7))

    m_ref[...] = jnp.full(m_ref.shape, NEG_INF, F32)
    acc_ref[...] = jnp.zeros(acc_ref.shape, F32)
    bias_r = (lambda r: 0) if shared_bias else (lambda r: r)
    if has_sel:
        pad_rows = jnp.zeros((k_ref.shape[-1] - HEAD_DIM - MASK_ROWS, T), BF16)
    n_steps = NQ * (NQ + 1) // 2
    N_S = SCORE_SLOTS
    assert N_S % 2 == 0 and n_steps % N_S == 0

    def score_stage(I, J, slot, table):
        off = pl.multiple_of(J * T, T)
        rel = jnp.minimum(I - J, nrel - 1)
        for r in range(R):
            kt = k_ref[0, 0, r, pl.ds(off, T), :] if k_per_r else k_ref[0, 0, pl.ds(off, T), :]
            w = qt_ref[0, 0, r, I]
            if has_sel:
                w = jnp.concatenate([w, sel_ref[0, 0, I, J], pad_rows], axis=0)
            s = jnp.dot(kt, w, preferred_element_type=F32)
            if early_bias:
                if table:
                    s = s + bt_ref[0, rel, bias_r(r)]
                tmax_ref[slot, r] = jnp.max(s, axis=0, keepdims=True)
            s_ref[slot, r] = s

    def softmax_stage(I, J, s_slot, slot, table):
        for r in range(R):
            m_prev = m_ref[I, r]
            s = s_ref[s_slot, r]
            if early_bias:
                tile_max = tmax_ref[s_slot, r]
            else:
                if table:
                    s = s + bt_ref[0, jnp.minimum(I - J, nrel - 1), bias_r(r)]
                tile_max = jnp.max(s, axis=0, keepdims=True)
            m_new = jnp.maximum(m_prev, tile_max)
            al_ref[slot, r] = jnp.exp2(m_prev - m_new)
            p_ref[slot, r] = jnp.exp2(s - m_new).astype(BF16)
            m_ref[I, r] = m_new

    def value_stage(J, slot):
        vt = vt_ref[0, 0, J]
        return [jnp.dot(vt, p_ref[slot, r], preferred_element_type=F32) for r in range(R)]

    def accumulate(I, slot, pvs):
        for r in range(R):
            acc_ref[I, r] = al_ref[slot, r] * acc_ref[I, r] + pvs[r]

    def far_advance(I, J):
        wrap = J + 1 > I - (nrel - 1)
        return jnp.where(wrap, I + 1, I), jnp.where(wrap, 0, J + 1)

    def near_advance(I, J):
        on_diag, before_diag = J == I, J == I - 1
        nxt_i = jnp.where(on_diag, I + 1, jnp.where(before_diag, I, 0))
        nxt_j = jnp.where(on_diag, I, jnp.where(before_diag, I, 0))
        return nxt_i, nxt_j

    assert nrel == 3
    n_far = (NQ - 1) * (NQ - 2) // 2
    n_near = 2 * NQ - 1
    far_trips, far_left = divmod(n_far, N_S)
    assert far_left == 1 and (far_left + n_near) % N_S == 0
    near_trips = (far_left + n_near) // N_S

    def step(cur, prev, t, advance, table):
        other = 1 - t % 2
        nxt = advance(*cur)
        pvs = value_stage(prev[1], other)
        score_stage(jnp.minimum(nxt[0], NQ - 1), jnp.minimum(nxt[1], NQ - 1), (t + 1) % N_S, table)
        softmax_stage(cur[0], cur[1], t % N_S, t % 2, table)
        accumulate(prev[0], other, pvs)
        return nxt

    def make_trip(advance, table):
        def trip(u, carry):
            cur, prev = carry[:2], carry[2:]
            for t in range(N_S):
                cur, prev = step(cur, prev, t, advance, table), cur
            return (*cur, *prev)
        return trip

    first = (jnp.int32(nrel - 1), jnp.int32(0))
    score_stage(*first, 0, False)
    p_ref[1] = jnp.zeros(p_ref.shape[1:], BF16)
    al_ref[1] = jnp.ones(al_ref.shape[1:], F32)
    carry = lax.fori_loop(0, far_trips, make_trip(far_advance, False), (*first, *first))
    lax.fori_loop(0, near_trips, make_trip(near_advance, True), carry)
    accumulate(NQ - 1, 1, value_stage(NQ - 1, 1))

    if diff:
        dl = dl_ref[...]
        lam = (jnp.exp(jnp.sum(dl[0:1] * dl[1:2], keepdims=True))
               - jnp.exp(jnp.sum(dl[2:3] * dl[3:4], keepdims=True)) + lam_init)

    def finish(I, carry):
        outs = []
        for r in range(R):
            acc = acc_ref[I, r]
            outs.append(acc[:Dv] / acc[Dv:Dv + 1])
        rows = pl.ds(pl.multiple_of(I * T, T), T)
        if diff:
            o = (outs[0] - lam * outs[1]).T
            o_ref[0, rows, :] = (_rms(o, sub_ref[...]) * (1.0 - lam_init)).astype(o_ref.dtype)
        else:
            o_ref[0, rows, :] = jnp.concatenate(outs, axis=0).T
        return carry

    lax.fori_loop(0, NQ, finish, 0)


def _flash_attn(qt, k, vt, bias_tab, *, sel=None, dl=None, subln=None, lam_init=0.0, name="attn"):
    B, G, R, NQ, _, T = qt.shape
    S = NQ * T
    Dv = vt.shape[-2] - ONES_ROWS
    nrel, bias_heads = bias_tab.shape[1], bias_tab.shape[2]
    k_per_r = k.ndim == 5
    has_sel = sel is not None
    diff = dl is not None
    ins = [qt, k, vt, bias_tab]
    in_specs = [
        pl.BlockSpec((1, 1, R, NQ, HEAD_DIM, T), lambda b, g: (b, g, 0, 0, 0, 0)),
        (pl.BlockSpec((1, 1, R, S, HEAD_DIM), lambda b, g: (b, g, 0, 0, 0)) if k_per_r
         else pl.BlockSpec((1, 1, S, k.shape[-1]), lambda b, g: (b, g, 0, 0))),
        pl.BlockSpec((1, 1, NQ, Dv + ONES_ROWS, T), lambda b, g: (b, g, 0, 0, 0)),
        pl.BlockSpec((1, nrel, bias_heads, T, T), lambda b, g: (g, 0, 0, 0, 0)),
    ]
    if has_sel:
        ins.append(sel)
        in_specs.append(pl.BlockSpec((1, 1, NQ, NQ, MASK_ROWS, T), lambda b, g: (b, g, 0, 0, 0, 0)))
    if diff:
        ins += [dl, subln]
        in_specs += [pl.BlockSpec((4, HEAD_DIM), lambda b, g: (0, 0)),
                     pl.BlockSpec((1, A_VDIM), lambda b, g: (0, 0))]
    out_w = Dv if diff else R * Dv
    kern = functools.partial(_flash_kernel, R=R, T=T, NQ=NQ, Dv=Dv, nrel=nrel, k_per_r=k_per_r,
                             shared_bias=bias_heads == 1, diff=diff, has_sel=has_sel, early_bias=not diff,
                             lam_init=lam_init)
    return pl.pallas_call(
        kern,
        grid=(B, G),
        in_specs=in_specs,
        out_specs=pl.BlockSpec((1, S, out_w), lambda b, g: (b, 0, g)),
        out_shape=jax.ShapeDtypeStruct((B, S, G * out_w), BF16 if diff else F32),
        scratch_shapes=[pltpu.VMEM((NQ, R, 1, T), F32), pltpu.VMEM((NQ, R, Dv + ONES_ROWS, T), F32),
                        pltpu.VMEM((SCORE_SLOTS, R, T, T), F32), pltpu.VMEM((2, R, T, T), BF16),
                        pltpu.VMEM((2, R, 1, T), F32), pltpu.VMEM((SCORE_SLOTS, R, 1, T), F32)],
        compiler_params=_params(2),
        name=name,
    )(*ins)


def _window_kernel(*refs, R, T, NK, TV, Dv, n_sub, has_sink):
    it = iter(refs)
    qt_ref, k_ref, vt_ref, bt_ref = (next(it) for _ in range(4))
    sink_ref = next(it) if has_sink else None
    o_ref = next(it)
    heads = [(t, r) for t in range(n_sub) for r in range(R)]
    tile = [pl.program_id(2) * n_sub + t for t in range(n_sub)]
    entry = [jnp.minimum(I, -(-(NK - T) // T)) for I in tile]
    first_key = [pl.multiple_of(I * T - jnp.minimum(e * T, NK - T), TV) for I, e in zip(tile, entry)]
    keys = [k_ref[0, 0, pl.ds(fk, NK), :] for fk in first_key]
    scores = {(t, r): jnp.dot(keys[t], qt_ref[0, 0, r, t], preferred_element_type=F32) for t, r in heads}
    probs, maxes = {}, {}
    for t, r in heads:
        s = scores[t, r] + bt_ref[0, entry[t], r]
        maxes[t, r] = jnp.max(s, axis=0, keepdims=True)
        probs[t, r] = jnp.exp2(s - maxes[t, r]).astype(BF16)
    accs = {}
    for t, r in heads:
        acc = None
        for c in range(NK // TV):
            part = jnp.dot(vt_ref[0, 0, first_key[t] // TV + c], probs[t, r][c * TV:(c + 1) * TV],
                           preferred_element_type=F32)
            acc = part if acc is None else acc + part
        accs[t, r] = acc
    for t in range(n_sub):
        outs = []
        for r in range(R):
            num, l = accs[t, r][:Dv], accs[t, r][Dv:Dv + 1]
            if has_sink:
                sk = sink_ref[0, r] * LOG2E
                m_f = jnp.maximum(maxes[t, r], sk)
                w = jnp.exp2(maxes[t, r] - m_f)
                outs.append(num * w / (l * w + jnp.exp2(sk - m_f)))
            else:
                outs.append(num / l)
        o_ref[0, t * T:(t + 1) * T, :] = jnp.concatenate(outs, axis=0).T.astype(o_ref.dtype)


def _window_attn(qt, k, vt, bias_tab, *, n_sub=2, sink=None, out_dtype=F32, name="attn_window"):
    B, G, R, NQ, _, T = qt.shape
    S = NQ * T
    TV = vt.shape[-1]
    Dv = vt.shape[-2] - ONES_ROWS
    entries, NK = bias_tab.shape[1], bias_tab.shape[3]
    has_sink = sink is not None
    ins = [qt, k, vt, bias_tab]
    in_specs = [
        pl.BlockSpec((1, 1, R, n_sub, HEAD_DIM, T), lambda b, g, i: (b, g, 0, i, 0, 0)),
        pl.BlockSpec((1, 1, S, HEAD_DIM), lambda b, g, i: (b, g, 0, 0)),
        pl.BlockSpec((1, 1, S // TV, Dv + ONES_ROWS, TV), lambda b, g, i: (b, g, 0, 0, 0)),
        pl.BlockSpec((1, entries, R, NK, T), lambda b, g, i: (g, 0, 0, 0, 0)),
    ]
    if has_sink:
        ins.append(sink)
        in_specs.append(pl.BlockSpec((1, R, 1, 1), lambda b, g, i: (g, 0, 0, 0)))
    kern = functools.partial(_window_kernel, R=R, T=T, NK=NK, TV=TV, Dv=Dv, n_sub=n_sub, has_sink=has_sink)
    return pl.pallas_call(
        kern,
        grid=(B, G, NQ // n_sub),
        in_specs=in_specs,
        out_specs=pl.BlockSpec((1, n_sub * T, R * Dv), lambda b, g, i: (b, i, g)),
        out_shape=jax.ShapeDtypeStruct((B, S, G * R * Dv), out_dtype),
        compiler_params=_params(3),
        name=name,
    )(*ins)


def _bucket_np(dist):
    n = np.maximum(dist, 0)
    max_exact = NUM_BUCKETS // 2
    nf = np.maximum(n, 1).astype(np.float32)
    large = max_exact + (np.log(nf / max_exact) / math.log(MAX_DISTANCE / max_exact)
                         * (NUM_BUCKETS - max_exact)).astype(np.int32)
    large = np.minimum(large, NUM_BUCKETS - 1)
    return np.where(n < max_exact, n, large)


def _bucket_starts():
    b = _bucket_np(np.arange(4 * MAX_DISTANCE))
    assert (np.diff(b) >= 0).all() and b[-1] == NUM_BUCKETS - 1
    return [int(np.argmax(b >= i)) for i in range(NUM_BUCKETS)]


def _bias_kernel(tab_ref, o_ref, *, T, rows, nrel, window, back_max, center, starts):
    h = pl.program_id(0)
    base = tab_ref[NUM_BUCKETS - 1, h] * LOG2E if center else 0.0
    key = lax.broadcasted_iota(jnp.int32, (rows, T), 0)
    qry = lax.broadcasted_iota(jnp.int32, (rows, T), 1)
    for rel in range(nrel):
        dist = min(rel * T, back_max) + qry - key
        val = jnp.full((rows, T), tab_ref[0, h] * LOG2E - base, F32)
        for b in range(1, NUM_BUCKETS):
            val = jnp.where(dist >= starts[b], tab_ref[b, h] * LOG2E - base, val)
        allowed = dist >= 0
        if window is not None:
            allowed = allowed & (dist < window)
        o_ref[0, rel, 0] = jnp.where(allowed, val, NEG_INF)


def _bias_tiles(tab, T, nrel, window, R, rows=None, center=False):
    H = tab.shape[1]
    back_max = nrel * T if rows is None else rows - T
    rows = T if rows is None else rows
    return pl.pallas_call(
        functools.partial(_bias_kernel, T=T, rows=rows, nrel=nrel, window=window, back_max=back_max,
                          center=center, starts=_bucket_starts()),
        grid=(H,),
        in_specs=[pl.BlockSpec(memory_space=pltpu.SMEM)],
        out_specs=pl.BlockSpec((1, nrel, 1, rows, T), lambda h: (h // R, 0, h % R, 0, 0)),
        out_shape=jax.ShapeDtypeStruct((H // R, nrel, R, rows, T), F32),
        compiler_params=_params(1),
        name="bias_tiles",
    )(tab)


def _merge_kernel(x_ref, nrm_ref, oa_ref, ob_ref, oc_ref, os_ref, ow_ref, gc_ref, ex_ref, wmg_ref, wb_ref, wo_ref,
                  o_ref):
    x = x_ref[...]
    hb = _rms(x, nrm_ref[...]).astype(BF16)
    gc = gc_ref[...]
    g_hi = gc.astype(BF16)
    g_lo = (gc - g_hi.astype(F32)).astype(BF16)
    gx = jnp.dot(jnp.concatenate([g_hi, g_lo], axis=1), ex_ref[...], preferred_element_type=F32)
    oc = (gx[:, 0:MIX_WIDTH] * oc_ref[...] + gx[:, MIX_WIDTH:2 * MIX_WIDTH] * os_ref[...]
          + gx[:, 2 * MIX_WIDTH:3 * MIX_WIDTH] * ow_ref[...])
    z = None
    for n, br in enumerate((oa_ref[...], ob_ref[...], oc)):
        y = jnp.dot(br.astype(BF16), wb_ref[n], preferred_element_type=F32)
        gate = jax.nn.sigmoid(jnp.dot(hb, wmg_ref[:, n * D_MODEL:(n + 1) * D_MODEL], preferred_element_type=F32))
        t = gate * y
        z = t if z is None else z + t
    o_ref[...] = x + jnp.dot(z.astype(BF16), wo_ref[...], preferred_element_type=F32)


def _merge(x2, nrm, oa, ob, oc, osel, ow, gc, expand_g, wmg, wb, wo, layer, tm=512):
    Tn = x2.shape[0]
    row = lambda d: pl.BlockSpec((tm, d), lambda i: (i, 0))
    const = lambda shape: pl.BlockSpec(shape, lambda i: (0,) * len(shape), pipeline_mode=pl.Buffered(1))
    weight = lambda shape: pl.BlockSpec((None,) + shape, lambda i: (layer,) + (0,) * len(shape),
                                        pipeline_mode=pl.Buffered(1))
    return pl.pallas_call(
        _merge_kernel,
        grid=(Tn // tm,),
        in_specs=[row(D_MODEL), const((1, D_MODEL)),
                  row(MIX_WIDTH), row(MIX_WIDTH), row(MIX_WIDTH), row(MIX_WIDTH), row(MIX_WIDTH),
                  row(CG_PAD),
                  const((2 * CG_PAD, 3 * MIX_WIDTH)),
                  weight((D_MODEL, 3 * D_MODEL)),
                  weight((3, MIX_WIDTH, D_MODEL)),
                  weight((D_MODEL, D_MODEL))],
        out_specs=row(D_MODEL),
        out_shape=jax.ShapeDtypeStruct((Tn, D_MODEL), F32),
        compiler_params=_params(1),
        name="merge",
    )(x2, nrm, oa, ob, oc, osel, ow, gc, expand_g, wmg, wb, wo)


def _mlp_kernel(x_ref, nrm_ref, wu_ref, wd_ref, o_ref, h_ref, acc_ref):
    f = pl.program_id(1)

    @pl.when(f == 0)
    def _():
        h_ref[...] = _rms(x_ref[...], nrm_ref[...]).astype(BF16)
        acc_ref[...] = jnp.zeros(acc_ref.shape, F32)

    u = jnp.dot(h_ref[...], wu_ref[...], preferred_element_type=F32)
    u = jnp.square(jnp.maximum(u, 0.0)).astype(BF16)
    acc_ref[...] += jnp.dot(u, wd_ref[...], preferred_element_type=F32)

    @pl.when(f == pl.num_programs(1) - 1)
    def _():
        o_ref[...] = x_ref[...] + acc_ref[...]


def _mlp(x2, nrm, wu, wd, layer, tm=1024, tf=1024):
    Tn = x2.shape[0]
    return pl.pallas_call(
        _mlp_kernel,
        grid=(Tn // tm, D_FF // tf),
        in_specs=[pl.BlockSpec((tm, D_MODEL), lambda i, f: (i, 0)),
                  pl.BlockSpec((1, D_MODEL), lambda i, f: (0, 0)),
                  pl.BlockSpec((None, D_MODEL, tf), lambda i, f: (layer, 0, f)),
                  pl.BlockSpec((None, tf, D_MODEL), lambda i, f: (layer, f, 0))],
        out_specs=pl.BlockSpec((tm, D_MODEL), lambda i, f: (i, 0)),
        out_shape=jax.ShapeDtypeStruct((Tn, D_MODEL), F32),
        scratch_shapes=[pltpu.VMEM((tm, D_MODEL), BF16), pltpu.VMEM((tm, D_MODEL), F32)],
        compiler_params=_params(2),
        name="mlp",
    )(x2, nrm, wu, wd)


def _overlap(n_cmp):
    c_start = np.arange(NC_PAD) * CMP_STRIDE
    j_start = np.arange(N_SEL_BLK) * SEL_BLOCK
    ov = ((c_start[None, :] < j_start[:, None] + SEL_BLOCK) & (c_start[None, :] + CMP_BLOCK > j_start[:, None])
          & (np.arange(NC_PAD)[None, :] < n_cmp))
    return jnp.asarray(ov.astype(np.float32), BF16)


def _gate_expand():
    e = np.zeros((2, CG_PAD, 3 * MIX_WIDTH), np.float32)
    for h in range(C_HEADS):
        for j in range(3):
            e[:, h * 3 + j, j * MIX_WIDTH + h * HEAD_DIM:j * MIX_WIDTH + (h + 1) * HEAD_DIM] = 1.0
    return jnp.asarray(e.reshape(2 * CG_PAD, 3 * MIX_WIDTH), BF16)


def kernel(x, w_in, qk_gain, diff_lambda, diff_subln, sinks, cmp_pos, cmp_w1, cmp_w2,
           w_branch, w_out, norm_mix, norm_mlp, w_up, w_down, rel_bias):
    B, S, _ = x.shape
    depth = w_in.shape[0]
    n_cmp = (S - CMP_BLOCK) // CMP_STRIDE + 1
    assert S % T_ROW == 0 and S // CMP_STRIDE == NC_PAD and S // SEL_BLOCK == N_SEL_BLK
    half = CMP_BLOCK // 2 * HEAD_DIM

    w_heads = w_in[:, :, :C_CG].astype(BF16)
    w_cg = jnp.pad(w_in[:, :, C_CG:C_CG + N_CG], ((0, 0), (0, 0), (0, CG_PAD - N_CG))).astype(BF16)
    w_mg = w_in[:, :, C_CG + N_CG:].astype(BF16)
    gains2 = jnp.concatenate([qk_gain, qk_gain], axis=-1)
    w1 = cmp_w1.astype(BF16).reshape(depth, 2, 2, half, CMP_HIDDEN)
    w2p = jnp.pad(cmp_w2, ((0, 0), (0, 0), (0, 0), (0, LANES - HEAD_DIM))).astype(BF16)
    pos = cmp_pos.reshape(depth, 2, 2, half)
    wb = w_branch.astype(BF16)
    wo = w_out.astype(BF16)
    wu = w_up.astype(BF16)
    wd = w_down.astype(BF16)

    bias_a = rel_bias[:, :A_HEADS]
    bias_b = rel_bias[:, A_HEADS:A_HEADS + B_HEADS]
    bias_c = rel_bias[:, A_HEADS + B_HEADS:]
    bt_a = _bias_tiles(bias_a, T_ROW, 3, None, 1, center=True)
    bt_b = _bias_tiles(bias_b, T_SWA, -(-B_WINDOW // T_SWA) + 1, B_WINDOW, GROUP, rows=B_WINDOW + T_SWA)
    bt_sel = _bias_tiles(bias_c, T_ROW, 3, None, GROUP, center=True)
    bt_win = _bias_tiles(bias_c, T_ROW, C_WINDOW // T_ROW + 1, C_WINDOW, GROUP, rows=C_WINDOW + T_ROW)
    ovl = _overlap(n_cmp)
    expand_g = _gate_expand()

    for layer in range(depth):
        lam_init = 0.8 - 0.6 * math.exp(-0.3 * layer)
        (qat, ka, vat, qbt, kb, vbt, qct, tkv, ksel, vselt, kwin, vwint, gc) = _inproj(
            x, norm_mix[layer][None], w_heads, w_cg, gains2[layer], layer)
        oa = _flash_attn(qat, ka, vat, bt_a, dl=diff_lambda[layer],
                   subln=diff_subln[layer][None], lam_init=lam_init, name="attn_diff")
        ob = _window_attn(qbt, kb, vbt, bt_b, n_sub=8, out_dtype=BF16,
                          sink=sinks[layer].reshape(KV_GROUPS, GROUP, 1, 1), name="attn_swa")
        kc, vct = _compress(tkv, pos[layer], w1[layer], w2p[layer], qk_gain[layer][5:6])
        ocmp, sel = _cmp_attn(qct, kc, vct, ovl, n_cmp, n_sub=4)
        osel = _flash_attn(qct, ksel, vselt, bt_sel, sel=sel, name="attn_sel")
        owin = _window_attn(qct, kwin, vwint, bt_win, n_sub=4, name="attn_win")
        f2 = lambda a: a.reshape(B * S, a.shape[-1])
        x2 = _merge(f2(x), norm_mix[layer][None], f2(oa), f2(ob), f2(ocmp), f2(osel), f2(owin), f2(gc),
                    expand_g, w_mg, wb, wo, layer)
        x2 = _mlp(x2, norm_mlp[layer][None], wu, wd, layer)
        x = x2.reshape(B, S, D_MODEL)
    return x
```

```python
import functools
import math

import numpy as np
import jax
import jax.numpy as jnp
from jax import lax
from jax.experimental import pallas as pl
from jax.experimental.pallas import tpu as pltpu

F32 = jnp.float32
BF16 = jnp.bfloat16

D_MODEL = 1024
HEAD_DIM = 64
A_HEADS = 4
A_VDIM = 128
B_HEADS = 8
B_WINDOW = 128
C_HEADS = 8
KV_GROUPS = 2
GROUP = 4
CMP_BLOCK = 32
CMP_STRIDE = 16
CMP_HIDDEN = 256
SEL_BLOCK = 64
N_SELECT = 16
C_WINDOW = 512
MIX_WIDTH = 512
D_FF = 4096
NUM_BUCKETS = 32
MAX_DISTANCE = 128
NEG_INF = -1e30
FORCE_BONUS = 1e4
EPS = 1e-6
LOG2E = 1.4426950408889634
QK_SCALE = HEAD_DIM ** -0.5 * LOG2E

C_AQ, C_AK, C_AV, C_BQ, C_BKV, C_CQ, C_CKV, C_CG = 0, 512, 1024, 1536, 2048, 2304, 2816, 3584
N_CG = C_HEADS * 3
LANES = 128
CG_PAD = LANES
NC_PAD = 256
N_SEL_BLK = 64
ONES_ROWS = 16
MASK_ROWS = 16

SCORE_SLOTS = 8

T_IN = 512
T_ROW = 256
T_SWA = 256
TV_SWA = 128

VMEM_LIMIT = 48 * 1024 * 1024


def _rms(x, gain):
    ms = jnp.mean(x * x, axis=-1, keepdims=True)
    return (x * lax.rsqrt(ms + EPS)) * gain


def _params(n_axes):
    return pltpu.CompilerParams(dimension_semantics=("arbitrary",) * n_axes,
                                vmem_limit_bytes=VMEM_LIMIT)


def _inproj_kernel(x_ref, nrm_ref, w_ref, wcg_ref, gains_ref,
                   qa_ref, ka_ref, va_ref, qb_ref, kb_ref, vb_ref, qc_ref,
                   tkv_ref, ksel_ref, vsel_ref, kwin_ref, vwin_ref, gc_ref):
    tm = x_ref.shape[1]
    hb = _rms(x_ref[0], nrm_ref[...]).astype(BF16)
    lane = lax.broadcasted_iota(jnp.int32, (tm, LANES), 1)
    first = lane < HEAD_DIM

    def ones_pad(width):
        return jnp.where(lax.broadcasted_iota(jnp.int32, (ONES_ROWS, width), 0) == 0, 1.0, 0.0).astype(BF16)

    def mm(c0, n):
        return jnp.dot(hb, w_ref[:, c0:c0 + n], preferred_element_type=F32)

    def norm_pair(y, gain_idx, scale=None):
        sq = y * y
        ms0 = jnp.sum(jnp.where(first, sq, 0.0), axis=-1, keepdims=True) * (1.0 / HEAD_DIM)
        ms1 = jnp.sum(jnp.where(first, 0.0, sq), axis=-1, keepdims=True) * (1.0 / HEAD_DIM)
        inv = jnp.where(first, lax.rsqrt(ms0 + EPS), lax.rsqrt(ms1 + EPS))
        out = (y * inv) * gains_ref[gain_idx:gain_idx + 1, :]
        return out if scale is None else out * scale

    def pairs(c0):
        r = mm(c0, 256)
        return r[:, :LANES], r[:, LANES:]

    def put_qt(ref, i0, i1, y):
        yt = y.T.astype(BF16)
        n_tiles = ref.shape[3]
        tq = tm // n_tiles
        for t in range(n_tiles):
            ref[0, i0, i1, t] = yt[:HEAD_DIM, tq * t:tq * (t + 1)]
            ref[0, i0, i1 + 1, t] = yt[HEAD_DIM:, tq * t:tq * (t + 1)]

    def put_vt(ref, g0, y, n_tiles):
        yt = y.T.astype(BF16)
        tk = tm // n_tiles
        for g in range(2):
            for t in range(n_tiles):
                ref[0, g0 + g, t, 0:HEAD_DIM, :] = yt[HEAD_DIM * g:HEAD_DIM * (g + 1), tk * t:tk * (t + 1)]
                ref[0, g0 + g, t, HEAD_DIM:HEAD_DIM + ONES_ROWS, :] = ones_pad(tk)

    for ch in range(2):
        for half, y in enumerate(pairs(C_AQ + 256 * ch)):
            put_qt(qa_ref, ch * 2 + half, 0, norm_pair(y, 0, QK_SCALE))
        for half, y in enumerate(pairs(C_AK + 256 * ch)):
            y = norm_pair(y, 1).astype(BF16)
            ka_ref[0, ch * 2 + half, 0] = y[:, :HEAD_DIM]
            ka_ref[0, ch * 2 + half, 1] = y[:, HEAD_DIM:]
        for half, y in enumerate(pairs(C_AV + 256 * ch)):
            yt = y.T.astype(BF16)
            for t in range(tm // T_ROW):
                va_ref[0, ch * 2 + half, t, 0:A_VDIM, :] = yt[:, T_ROW * t:T_ROW * (t + 1)]
                va_ref[0, ch * 2 + half, t, A_VDIM:A_VDIM + ONES_ROWS, :] = ones_pad(T_ROW)
        for half, y in enumerate(pairs(C_BQ + 256 * ch)):
            put_qt(qb_ref, ch, 2 * half, norm_pair(y, 2, QK_SCALE))
        for half, y in enumerate(pairs(C_CQ + 256 * ch)):
            put_qt(qc_ref, ch, 2 * half, norm_pair(y, 4, QK_SCALE))

    def put_k(ref, y):
        y = y.astype(BF16)
        ref[0, 0] = y[:, :HEAD_DIM]
        ref[0, 1] = y[:, HEAD_DIM:]

    yk, yv = pairs(C_BKV)
    put_k(kb_ref, norm_pair(yk, 3))
    put_vt(vb_ref, 0, yv, tm // TV_SWA)
    y0, y1 = pairs(C_CKV)
    for kv, y in enumerate((y0, y1)):
        tkv_ref[0, kv, 0] = y[:, :HEAD_DIM]
        tkv_ref[0, kv, 1] = y[:, HEAD_DIM:]
    yk, yv = pairs(C_CKV + 256)
    yk = norm_pair(yk, 6)
    row = lax.broadcasted_iota(jnp.int32, (tm, LANES), 0)
    hot = jnp.where(lane == HEAD_DIM + (row // SEL_BLOCK) % (T_ROW // SEL_BLOCK), 1.0, 0.0)
    ksel_ref[0, 0] = jnp.where(first, yk, hot).astype(BF16)
    ksel_ref[0, 1] = jnp.where(first, pltpu.roll(yk, HEAD_DIM, 1), hot).astype(BF16)
    put_vt(vsel_ref, 0, yv, tm // T_ROW)
    yk, yv = pairs(C_CKV + 512)
    put_k(kwin_ref, norm_pair(yk, 7))
    put_vt(vwin_ref, 0, yv, tm // T_ROW)

    gc_ref[0] = jax.nn.sigmoid(jnp.dot(hb, wcg_ref[...], preferred_element_type=F32))


def _inproj(x, nrm, w, wcg, gains2, layer):
    B, S, _ = x.shape
    tm = T_IN
    nt = S // tm
    sd = jax.ShapeDtypeStruct
    qt = lambda a, b, tq: sd((B, a, b, S // tq, HEAD_DIM, tq), BF16)
    kk = lambda n: sd((B, n, S, HEAD_DIM), BF16)
    vt = lambda n, dv, tk: sd((B, n, S // tk, dv + ONES_ROWS, tk), BF16)
    out_shape = (
        qt(A_HEADS, 2, T_ROW), sd((B, A_HEADS, 2, S, HEAD_DIM), BF16), vt(A_HEADS, A_VDIM, T_ROW),
        qt(KV_GROUPS, GROUP, T_SWA), kk(KV_GROUPS), vt(KV_GROUPS, HEAD_DIM, TV_SWA),
        qt(KV_GROUPS, GROUP, T_ROW),
        sd((B, 2, KV_GROUPS, S, HEAD_DIM), F32),
        sd((B, KV_GROUPS, S, LANES), BF16), vt(KV_GROUPS, HEAD_DIM, T_ROW),
        kk(KV_GROUPS), vt(KV_GROUPS, HEAD_DIM, T_ROW),
        sd((B, S, CG_PAD), F32),
    )
    s_qt = lambda a, b, tq: pl.BlockSpec((1, a, b, tm // tq, HEAD_DIM, tq), lambda b_, i: (b_, 0, 0, i, 0, 0))
    s_k = lambda n: pl.BlockSpec((1, n, tm, HEAD_DIM), lambda b_, i: (b_, 0, i, 0))
    s_k5 = lambda a, c: pl.BlockSpec((1, a, c, tm, HEAD_DIM), lambda b_, i: (b_, 0, 0, i, 0))
    s_vt = lambda n, dv, tk: pl.BlockSpec((1, n, tm // tk, dv + ONES_ROWS, tk), lambda b_, i: (b_, 0, i, 0, 0))
    row = lambda d: pl.BlockSpec((1, tm, d), lambda b_, i: (b_, i, 0))
    out_specs = (
        s_qt(A_HEADS, 2, T_ROW), s_k5(A_HEADS, 2), s_vt(A_HEADS, A_VDIM, T_ROW),
        s_qt(KV_GROUPS, GROUP, T_SWA), s_k(KV_GROUPS), s_vt(KV_GROUPS, HEAD_DIM, TV_SWA),
        s_qt(KV_GROUPS, GROUP, T_ROW), s_k5(2, KV_GROUPS),
        pl.BlockSpec((1, KV_GROUPS, tm, LANES), lambda b_, i: (b_, 0, i, 0)), s_vt(KV_GROUPS, HEAD_DIM, T_ROW),
        s_k(KV_GROUPS), s_vt(KV_GROUPS, HEAD_DIM, T_ROW),
        row(CG_PAD),
    )
    return pl.pallas_call(
        _inproj_kernel,
        grid=(B, nt),
        in_specs=[
            row(D_MODEL),
            pl.BlockSpec((1, D_MODEL), lambda b_, i: (0, 0)),
            pl.BlockSpec((None, D_MODEL, C_CG), lambda b_, i: (layer, 0, 0), pipeline_mode=pl.Buffered(1)),
            pl.BlockSpec((None, D_MODEL, CG_PAD), lambda b_, i: (layer, 0, 0)),
            pl.BlockSpec((8, LANES), lambda b_, i: (0, 0)),
        ],
        out_specs=out_specs,
        out_shape=out_shape,
        compiler_params=_params(2),
        name="inproj",
    )(x, nrm, w, wcg, gains2)


def _compress_kernel(t_ref, pos_ref, w1_ref, w2_ref, gain_ref, o_ref, ot_ref):
    t = jnp.concatenate([t_ref[0, 0, 0, pl.ds(b, NC_PAD, stride=CMP_STRIDE), :] for b in range(CMP_STRIDE)], axis=1)
    lo = (t + pos_ref[0, 0:1, :]).astype(BF16)
    hi = (t + pos_ref[0, 1:2, :]).astype(BF16)
    v = jnp.dot(lo, w1_ref[0, 0], preferred_element_type=F32)
    u = jnp.dot(hi, w1_ref[0, 1], preferred_element_type=F32)
    pre = v + pltpu.roll(u, NC_PAD - 1, 0)
    hcur = jax.nn.gelu(pre).astype(BF16)
    out = jnp.dot(hcur, w2_ref[0], preferred_element_type=F32)
    o64 = out[:, :HEAD_DIM]
    is_key = pl.program_id(1) == 0
    o_ref[0, 0, 0] = jnp.where(is_key, _rms(o64, gain_ref[...]), o64).astype(BF16)
    ot_ref[0, 0, 0] = out.T[:HEAD_DIM].astype(BF16)


def _compress(tkv, pos, w1, w2p, gain):
    B, S = tkv.shape[0], tkv.shape[3]
    half = CMP_BLOCK // 2 * HEAD_DIM
    return pl.pallas_call(
        _compress_kernel,
        grid=(B, 2, KV_GROUPS),
        in_specs=[
            pl.BlockSpec((1, 1, 1, S, HEAD_DIM), lambda b, kv, g: (b, kv, g, 0, 0)),
            pl.BlockSpec((1, 2, half), lambda b, kv, g: (kv, 0, 0)),
            pl.BlockSpec((1, 2, half, CMP_HIDDEN), lambda b, kv, g: (kv, 0, 0, 0)),
            pl.BlockSpec((1, CMP_HIDDEN, LANES), lambda b, kv, g: (kv, 0, 0)),
            pl.BlockSpec((1, HEAD_DIM), lambda b, kv, g: (0, 0)),
        ],
        out_specs=(pl.BlockSpec((1, 1, 1, NC_PAD, HEAD_DIM), lambda b, kv, g: (b, kv, g, 0, 0)),
                   pl.BlockSpec((1, 1, 1, HEAD_DIM, NC_PAD), lambda b, kv, g: (b, kv, g, 0, 0))),
        out_shape=(jax.ShapeDtypeStruct((B, 2, KV_GROUPS, NC_PAD, HEAD_DIM), BF16),
                   jax.ShapeDtypeStruct((B, 2, KV_GROUPS, HEAD_DIM, NC_PAD), BF16)),
        compiler_params=_params(3),
        name="compress",
    )(tkv, pos, w1, w2p, gain)


def _cmp_attn_kernel(qt_ref, kc_ref, vct_ref, ovl_ref, o_ref, sel_ref, score_ref, rank_ref, *, T, n_cmp, n_sub):
    R = GROUP
    tiles = [pl.program_id(2) * n_sub + t for t in range(n_sub)]
    j_idx = lax.broadcasted_iota(jnp.int32, (N_SEL_BLK, T), 0)
    lane_q = lax.broadcasted_iota(jnp.int32, (N_SEL_BLK, T), 1)

    def attend(rows):
        kc = kc_ref[0, 0, 0, 0:rows, :]
        vct = vct_ref[0, 0, 0, :, 0:rows]
        ovl = ovl_ref[:, 0:rows]
        c_idx = lax.broadcasted_iota(jnp.int32, (rows, T), 0)
        lane_k = lax.broadcasted_iota(jnp.int32, (rows, T), 1)
        for t, I in enumerate(tiles):
            cmask = (c_idx * CMP_STRIDE + (CMP_BLOCK - 1) <= I * T + lane_k) & (c_idx < n_cmp)
            scores = [jnp.dot(kc, qt_ref[0, 0, r, t], preferred_element_type=F32) for r in range(R)]
            probs = []
            for r in range(R):
                s = jnp.where(cmask, scores[r], NEG_INF)
                m = jnp.max(s, axis=0, keepdims=True)
                e = jnp.where(cmask, jnp.exp2(s - m), 0.0)
                l = jnp.sum(e, axis=0, keepdims=True)
                probs.append(e * (1.0 / jnp.where(l > 0.0, l, 1.0)))
            outs = [jnp.dot(vct, p.astype(BF16), preferred_element_type=F32) for p in probs]
            o_ref[0, t * T:(t + 1) * T, :] = jnp.concatenate(outs, axis=0).T
            psum = (probs[0] + probs[1]) + (probs[2] + probs[3])
            p_hi = psum.astype(BF16)
            p_lo = (psum - p_hi.astype(F32)).astype(BF16)
            imp = jnp.dot(ovl, p_hi, preferred_element_type=F32) + jnp.dot(ovl, p_lo, preferred_element_type=F32)
            tq = I * T + lane_q
            cur = tq // SEL_BLOCK
            valid = j_idx * SEL_BLOCK <= tq
            forced = (j_idx == 0) | (j_idx == cur) | (j_idx == cur - 1)
            score_ref[t] = jnp.where(valid, imp + jnp.where(forced, FORCE_BONUS, 0.0), NEG_INF)

    chunk = NC_PAD // 4
    vis_per_tile = T // CMP_STRIDE
    for v in range(4):
        lo = -(-(v * chunk) // vis_per_tile)
        hi = -(-((v + 1) * chunk) // vis_per_tile)
        assert lo % n_sub == 0 and hi % n_sub == 0

        @pl.when((tiles[0] >= lo) & (tiles[0] < hi))
        def _(v=v):
            attend((v + 1) * chunk)

    rank_ref[...] = jnp.zeros(rank_ref.shape, F32)
    n_live = (tiles[-1] + 1) * (T // SEL_BLOCK)
    sub = 8
    n_grp = N_SEL_BLK // sub
    j_loc = lax.broadcasted_iota(jnp.int32, (sub, T), 0)
    for c in range(n_grp):
        @pl.when(c * sub < n_live)
        def _(c=c):
            for t in range(n_sub):
                grps = [score_ref[t, g * sub:(g + 1) * sub, :] for g in range(n_grp)]
                cnts = [rank_ref[t, g * sub:(g + 1) * sub, :] for g in range(n_grp)]
                for jp in range(c * sub, (c + 1) * sub):
                    row = jnp.broadcast_to(score_ref[t, jp:jp + 1, :], (sub, T))
                    for g in range(n_grp):
                        if g * sub > jp:
                            beats = row >= grps[g]
                        elif (g + 1) * sub - 1 < jp:
                            beats = row > grps[g]
                        else:
                            beats = (row > grps[g]) | ((row == grps[g]) & (j_loc + g * sub > jp))
                        cnts[g] = cnts[g] + jnp.where(beats, 1.0, 0.0)
                for g in range(n_grp):
                    rank_ref[t, g * sub:(g + 1) * sub, :] = cnts[g]
    blk = T // SEL_BLOCK
    for t in range(n_sub):
        for j in range(N_SEL_BLK // blk):
            keep = rank_ref[t, j * blk:(j + 1) * blk, :] < N_SELECT
            slab = jnp.concatenate([jnp.where(keep, 0.0, NEG_INF), jnp.zeros((MASK_ROWS - blk, T), F32)], axis=0)
            sel_ref[0, 0, t, j] = slab.astype(BF16)


def _cmp_attn(qct, kc, vct, ovl, n_cmp, n_sub=2):
    B, G, R, NQ, _, T = qct.shape
    S = NQ * T
    return pl.pallas_call(
        functools.partial(_cmp_attn_kernel, T=T, n_cmp=n_cmp, n_sub=n_sub),
        grid=(B, G, NQ // n_sub),
        in_specs=[
            pl.BlockSpec((1, 1, R, n_sub, HEAD_DIM, T), lambda b, g, i: (b, g, 0, i, 0, 0)),
            pl.BlockSpec((1, 1, 1, NC_PAD, HEAD_DIM), lambda b, g, i: (b, 0, g, 0, 0)),
            pl.BlockSpec((1, 1, 1, HEAD_DIM, NC_PAD), lambda b, g, i: (b, 1, g, 0, 0)),
            pl.BlockSpec((N_SEL_BLK, NC_PAD), lambda b, g, i: (0, 0)),
        ],
        out_specs=(
            pl.BlockSpec((1, n_sub * T, R * HEAD_DIM), lambda b, g, i: (b, i, g)),
            pl.BlockSpec((1, 1, n_sub, NQ, MASK_ROWS, T), lambda b, g, i: (b, g, i, 0, 0, 0)),
        ),
        out_shape=(
            jax.ShapeDtypeStruct((B, S, G * R * HEAD_DIM), F32),
            jax.ShapeDtypeStruct((B, G, NQ, NQ, MASK_ROWS, T), BF16),
        ),
        scratch_shapes=[pltpu.VMEM((n_sub, N_SEL_BLK, T), F32), pltpu.VMEM((n_sub, N_SEL_BLK, T), F32)],
        compiler_params=_params(3),
        name="cmp_attn",
    )(qct, kc, vct, ovl)


def _flash_kernel(*refs, R, T, NQ, Dv, nrel, k_per_r, shared_bias, diff, has_sel, early_bias, lam_init):
    it = iter(refs)
    qt_ref, k_ref, vt_ref, bt_ref = (next(it) for _ in range(4))
    sel_ref = next(it) if has_sel else None
    dl_ref = next(it) if diff else None
    sub_ref = next(it) if diff else None
    o_ref, m_ref, acc_ref, s_ref, p_ref, al_ref, tmax_ref = (next(it) for _ in range(7))

    m_ref[...] = jnp.full(m_ref.shape, NEG_INF, F32)
    acc_ref[...] = jnp.zeros(acc_ref.shape, F32)
    bias_r = (lambda r: 0) if shared_bias else (lambda r: r)
    if has_sel:
        pad_rows = jnp.zeros((k_ref.shape[-1] - HEAD_DIM - MASK_ROWS, T), BF16)
    n_steps = NQ * (NQ + 1) // 2
    N_S = SCORE_SLOTS
    assert N_S % 2 == 0 and n_steps % N_S == 0

    def score_stage(I, J, slot, table):
        off = pl.multiple_of(J * T, T)
        rel = jnp.minimum(I - J, nrel - 1)
        for r in range(R):
            kt = k_ref[0, 0, r, pl.ds(off, T), :] if k_per_r else k_ref[0, 0, pl.ds(off, T), :]
            w = qt_ref[0, 0, r, I]
            if has_sel:
                w = jnp.concatenate([w, sel_ref[0, 0, I, J], pad_rows], axis=0)
            s = jnp.dot(kt, w, preferred_element_type=F32)
            if early_bias:
                if table:
                    s = s + bt_ref[0, rel, bias_r(r)]
                tmax_ref[slot, r] = jnp.max(s, axis=0, keepdims=True)
            s_ref[slot, r] = s

    def softmax_stage(I, J, s_slot, slot, table):
        for r in range(R):
            m_prev = m_ref[I, r]
            s = s_ref[s_slot, r]
            if early_bias:
                tile_max = tmax_ref[s_slot, r]
            else:
                if table:
                    s = s + bt_ref[0, jnp.minimum(I - J, nrel - 1), bias_r(r)]
                tile_max = jnp.max(s, axis=0, keepdims=True)
            m_new = jnp.maximum(m_prev, tile_max)
            al_ref[slot, r] = jnp.exp2(m_prev - m_new)
            p_ref[slot, r] = jnp.exp2(s - m_new).astype(BF16)
            m_ref[I, r] = m_new

    def value_stage(J, slot):
        vt = vt_ref[0, 0, J]
        return [jnp.dot(vt, p_ref[slot, r], preferred_element_type=F32) for r in range(R)]

    def accumulate(I, slot, pvs):
        for r in range(R):
            acc_ref[I, r] = al_ref[slot, r] * acc_ref[I, r] + pvs[r]

    def far_advance(I, J):
        wrap = J + 1 > I - (nrel - 1)
        return jnp.where(wrap, I + 1, I), jnp.where(wrap, 0, J + 1)

    def near_advance(I, J):
        on_diag, before_diag = J == I, J == I - 1
        nxt_i = jnp.where(on_diag, I + 1, jnp.where(before_diag, I, 0))
        nxt_j = jnp.where(on_diag, I, jnp.where(before_diag, I, 0))
        return nxt_i, nxt_j

    assert nrel == 3
    n_far = (NQ - 1) * (NQ - 2) // 2
    n_near = 2 * NQ - 1
    far_trips, far_left = divmod(n_far, N_S)
    assert far_left == 1 and (far_left + n_near) % N_S == 0
    near_trips = (far_left + n_near) // N_S

    def step(cur, prev, t, advance, table):
        other = 1 - t % 2
        nxt = advance(*cur)
        pvs = value_stage(prev[1], other)
        score_stage(jnp.minimum(nxt[0], NQ - 1), jnp.minimum(nxt[1], NQ - 1), (t + 1) % N_S, table)
        softmax_stage(cur[0], cur[1], t % N_S, t % 2, table)
        accumulate(prev[0], other, pvs)
        return nxt

    def make_trip(advance, table):
        def trip(u, carry):
            cur, prev = carry[:2], carry[2:]
            for t in range(N_S):
                cur, prev = step(cur, prev, t, advance, table), cur
            return (*cur, *prev)
        return trip

    first = (jnp.int32(nrel - 1), jnp.int32(0))
    score_stage(*first, 0, False)
    p_ref[1] = jnp.zeros(p_ref.shape[1:], BF16)
    al_ref[1] = jnp.ones(al_ref.shape[1:], F32)
    carry = lax.fori_loop(0, far_trips, make_trip(far_advance, False), (*first, *first))
    lax.fori_loop(0, near_trips, make_trip(near_advance, True), carry)
    accumulate(NQ - 1, 1, value_stage(NQ - 1, 1))

    if diff:
        dl = dl_ref[...]
        lam = (jnp.exp(jnp.sum(dl[0:1] * dl[1:2], keepdims=True))
               - jnp.exp(jnp.sum(dl[2:3] * dl[3:4], keepdims=True)) + lam_init)

    def finish(I, carry):
        outs = []
        for r in range(R):
            acc = acc_ref[I, r]
            outs.append(acc[:Dv] * (1.0 / acc[Dv:Dv + 1]))
        rows = pl.ds(pl.multiple_of(I * T, T), T)
        if diff:
            o = (outs[0] - lam * outs[1]).T
            o_ref[0, rows, :] = (_rms(o, sub_ref[...]) * (1.0 - lam_init)).astype(o_ref.dtype)
        else:
            o_ref[0, rows, :] = jnp.concatenate(outs, axis=0).T
        return carry

    lax.fori_loop(0, NQ, finish, 0)


def _flash_attn(qt, k, vt, bias_tab, *, sel=None, dl=None, subln=None, lam_init=0.0, name="attn"):
    B, G, R, NQ, _, T = qt.shape
    S = NQ * T
    Dv = vt.shape[-2] - ONES_ROWS
    nrel, bias_heads = bias_tab.shape[1], bias_tab.shape[2]
    k_per_r = k.ndim == 5
    has_sel = sel is not None
    diff = dl is not None
    ins = [qt, k, vt, bias_tab]
    in_specs = [
        pl.BlockSpec((1, 1, R, NQ, HEAD_DIM, T), lambda b, g: (b, g, 0, 0, 0, 0)),
        (pl.BlockSpec((1, 1, R, S, HEAD_DIM), lambda b, g: (b, g, 0, 0, 0)) if k_per_r
         else pl.BlockSpec((1, 1, S, k.shape[-1]), lambda b, g: (b, g, 0, 0))),
        pl.BlockSpec((1, 1, NQ, Dv + ONES_ROWS, T), lambda b, g: (b, g, 0, 0, 0)),
        pl.BlockSpec((1, nrel, bias_heads, T, T), lambda b, g: (g, 0, 0, 0, 0)),
    ]
    if has_sel:
        ins.append(sel)
        in_specs.append(pl.BlockSpec((1, 1, NQ, NQ, MASK_ROWS, T), lambda b, g: (b, g, 0, 0, 0, 0)))
    if diff:
        ins += [dl, subln]
        in_specs += [pl.BlockSpec((4, HEAD_DIM), lambda b, g: (0, 0)),
                     pl.BlockSpec((1, A_VDIM), lambda b, g: (0, 0))]
    out_w = Dv if diff else R * Dv
    kern = functools.partial(_flash_kernel, R=R, T=T, NQ=NQ, Dv=Dv, nrel=nrel, k_per_r=k_per_r,
                             shared_bias=bias_heads == 1, diff=diff, has_sel=has_sel, early_bias=not diff,
                             lam_init=lam_init)
    return pl.pallas_call(
        kern,
        grid=(B, G),
        in_specs=in_specs,
        out_specs=pl.BlockSpec((1, S, out_w), lambda b, g: (b, 0, g)),
        out_shape=jax.ShapeDtypeStruct((B, S, G * out_w), BF16 if diff else F32),
        scratch_shapes=[pltpu.VMEM((NQ, R, 1, T), F32), pltpu.VMEM((NQ, R, Dv + ONES_ROWS, T), F32),
                        pltpu.VMEM((SCORE_SLOTS, R, T, T), F32), pltpu.VMEM((2, R, T, T), BF16),
                        pltpu.VMEM((2, R, 1, T), F32), pltpu.VMEM((SCORE_SLOTS, R, 1, T), F32)],
        compiler_params=_params(2),
        name=name,
    )(*ins)


def _window_kernel(*refs, R, T, NK, TV, Dv, n_sub, has_sink):
    it = iter(refs)
    qt_ref, k_ref, vt_ref, bt_ref = (next(it) for _ in range(4))
    sink_ref = next(it) if has_sink else None
    o_ref = next(it)
    heads = [(t, r) for t in range(n_sub) for r in range(R)]
    tile = [pl.program_id(2) * n_sub + t for t in range(n_sub)]
    entry = [jnp.minimum(I, -(-(NK - T) // T)) for I in tile]
    first_key = [pl.multiple_of(I * T - jnp.minimum(e * T, NK - T), TV) for I, e in zip(tile, entry)]
    keys = [k_ref[0, 0, pl.ds(fk, NK), :] for fk in first_key]
    scores = {(t, r): jnp.dot(keys[t], qt_ref[0, 0, r, t], preferred_element_type=F32) for t, r in heads}
    probs, maxes = {}, {}
    for t, r in heads:
        s = scores[t, r] + bt_ref[0, entry[t], r]
        maxes[t, r] = jnp.max(s, axis=0, keepdims=True)
        probs[t, r] = jnp.exp2(s - maxes[t, r]).astype(BF16)
    accs = {}
    for t, r in heads:
        acc = None
        for c in range(NK // TV):
            part = jnp.dot(vt_ref[0, 0, first_key[t] // TV + c], probs[t, r][c * TV:(c + 1) * TV],
                           preferred_element_type=F32)
            acc = part if acc is None else acc + part
        accs[t, r] = acc
    for t in range(n_sub):
        outs = []
        for r in range(R):
            num, l = accs[t, r][:Dv], accs[t, r][Dv:Dv + 1]
            if has_sink:
                sk = sink_ref[0, r] * LOG2E
                m_f = jnp.maximum(maxes[t, r], sk)
                w = jnp.exp2(maxes[t, r] - m_f)
                outs.append(num * (w / (l * w + jnp.exp2(sk - m_f))))
            else:
                outs.append(num * (1.0 / l))
        o_ref[0, t * T:(t + 1) * T, :] = jnp.concatenate(outs, axis=0).T.astype(o_ref.dtype)


def _window_attn(qt, k, vt, bias_tab, *, n_sub=2, sink=None, out_dtype=F32, name="attn_window"):
    B, G, R, NQ, _, T = qt.shape
    S = NQ * T
    TV = vt.shape[-1]
    Dv = vt.shape[-2] - ONES_ROWS
    entries, NK = bias_tab.shape[1], bias_tab.shape[3]
    has_sink = sink is not None
    ins = [qt, k, vt, bias_tab]
    in_specs = [
        pl.BlockSpec((1, 1, R, n_sub, HEAD_DIM, T), lambda b, g, i: (b, g, 0, i, 0, 0)),
        pl.BlockSpec((1, 1, S, HEAD_DIM), lambda b, g, i: (b, g, 0, 0)),
        pl.BlockSpec((1, 1, S // TV, Dv + ONES_ROWS, TV), lambda b, g, i: (b, g, 0, 0, 0)),
        pl.BlockSpec((1, entries, R, NK, T), lambda b, g, i: (g, 0, 0, 0, 0)),
    ]
    if has_sink:
        ins.append(sink)
        in_specs.append(pl.BlockSpec((1, R, 1, 1), lambda b, g, i: (g, 0, 0, 0)))
    kern = functools.partial(_window_kernel, R=R, T=T, NK=NK, TV=TV, Dv=Dv, n_sub=n_sub, has_sink=has_sink)
    return pl.pallas_call(
        kern,
        grid=(B, G, NQ // n_sub),
        in_specs=in_specs,
        out_specs=pl.BlockSpec((1, n_sub * T, R * Dv), lambda b, g, i: (b, i, g)),
        out_shape=jax.ShapeDtypeStruct((B, S, G * R * Dv), out_dtype),
        compiler_params=_params(3),
        name=name,
    )(*ins)


def _bucket_np(dist):
    n = np.maximum(dist, 0)
    max_exact = NUM_BUCKETS // 2
    nf = np.maximum(n, 1).astype(np.float32)
    large = max_exact + (np.log(nf / max_exact) / math.log(MAX_DISTANCE / max_exact)
                         * (NUM_BUCKETS - max_exact)).astype(np.int32)
    large = np.minimum(large, NUM_BUCKETS - 1)
    return np.where(n < max_exact, n, large)


def _bucket_starts():
    b = _bucket_np(np.arange(4 * MAX_DISTANCE))
    assert (np.diff(b) >= 0).all() and b[-1] == NUM_BUCKETS - 1
    return [int(np.argmax(b >= i)) for i in range(NUM_BUCKETS)]


def _bias_kernel(tab_ref, o_ref, *, T, rows, nrel, window, back_max, center, starts):
    h = pl.program_id(0)
    base = tab_ref[NUM_BUCKETS - 1, h] * LOG2E if center else 0.0
    key = lax.broadcasted_iota(jnp.int32, (rows, T), 0)
    qry = lax.broadcasted_iota(jnp.int32, (rows, T), 1)
    for rel in range(nrel):
        dist = min(rel * T, back_max) + qry - key
        val = jnp.full((rows, T), tab_ref[0, h] * LOG2E - base, F32)
        for b in range(1, NUM_BUCKETS):
            val = jnp.where(dist >= starts[b], tab_ref[b, h] * LOG2E - base, val)
        allowed = dist >= 0
        if window is not None:
            allowed = allowed & (dist < window)
        o_ref[0, rel, 0] = jnp.where(allowed, val, NEG_INF)


def _bias_tiles(tab, T, nrel, window, R, rows=None, center=False):
    H = tab.shape[1]
    back_max = nrel * T if rows is None else rows - T
    rows = T if rows is None else rows
    return pl.pallas_call(
        functools.partial(_bias_kernel, T=T, rows=rows, nrel=nrel, window=window, back_max=back_max,
                          center=center, starts=_bucket_starts()),
        grid=(H,),
        in_specs=[pl.BlockSpec(memory_space=pltpu.SMEM)],
        out_specs=pl.BlockSpec((1, nrel, 1, rows, T), lambda h: (h // R, 0, h % R, 0, 0)),
        out_shape=jax.ShapeDtypeStruct((H // R, nrel, R, rows, T), F32),
        compiler_params=_params(1),
        name="bias_tiles",
    )(tab)


def _merge_kernel(x_ref, nrm_ref, oa_ref, ob_ref, oc_ref, os_ref, ow_ref, gc_ref, ex_ref, wmg_ref, wb_ref, wo_ref,
                  o_ref):
    x = x_ref[...]
    hb = _rms(x, nrm_ref[...]).astype(BF16)
    gc = gc_ref[...]
    g_hi = gc.astype(BF16)
    g_lo = (gc - g_hi.astype(F32)).astype(BF16)
    gx = jnp.dot(jnp.concatenate([g_hi, g_lo], axis=1), ex_ref[...], preferred_element_type=F32)
    oc = (gx[:, 0:MIX_WIDTH] * oc_ref[...] + gx[:, MIX_WIDTH:2 * MIX_WIDTH] * os_ref[...]
          + gx[:, 2 * MIX_WIDTH:3 * MIX_WIDTH] * ow_ref[...])
    z = None
    for n, br in enumerate((oa_ref[...], ob_ref[...], oc)):
        y = jnp.dot(br.astype(BF16), wb_ref[n], preferred_element_type=F32)
        gate = jax.nn.sigmoid(jnp.dot(hb, wmg_ref[:, n * D_MODEL:(n + 1) * D_MODEL], preferred_element_type=F32))
        t = gate * y
        z = t if z is None else z + t
    o_ref[...] = x + jnp.dot(z.astype(BF16), wo_ref[...], preferred_element_type=F32)


def _merge(x2, nrm, oa, ob, oc, osel, ow, gc, expand_g, wmg, wb, wo, layer, tm=512):
    Tn = x2.shape[0]
    row = lambda d: pl.BlockSpec((tm, d), lambda i: (i, 0))
    const = lambda shape: pl.BlockSpec(shape, lambda i: (0,) * len(shape), pipeline_mode=pl.Buffered(1))
    weight = lambda shape: pl.BlockSpec((None,) + shape, lambda i: (layer,) + (0,) * len(shape),
                                        pipeline_mode=pl.Buffered(1))
    return pl.pallas_call(
        _merge_kernel,
        grid=(Tn // tm,),
        in_specs=[row(D_MODEL), const((1, D_MODEL)),
                  row(MIX_WIDTH), row(MIX_WIDTH), row(MIX_WIDTH), row(MIX_WIDTH), row(MIX_WIDTH),
                  row(CG_PAD),
                  const((2 * CG_PAD, 3 * MIX_WIDTH)),
                  weight((D_MODEL, 3 * D_MODEL)),
                  weight((3, MIX_WIDTH, D_MODEL)),
                  weight((D_MODEL, D_MODEL))],
        out_specs=row(D_MODEL),
        out_shape=jax.ShapeDtypeStruct((Tn, D_MODEL), F32),
        compiler_params=_params(1),
        name="merge",
    )(x2, nrm, oa, ob, oc, osel, ow, gc, expand_g, wmg, wb, wo)


def _mlp_kernel(x_ref, nrm_ref, wu_ref, wd_ref, o_ref, h_ref, acc_ref):
    f = pl.program_id(1)

    @pl.when(f == 0)
    def _():
        h_ref[...] = _rms(x_ref[...], nrm_ref[...]).astype(BF16)
        acc_ref[...] = jnp.zeros(acc_ref.shape, F32)

    u = jnp.dot(h_ref[...], wu_ref[...], preferred_element_type=F32)
    u = jnp.square(jnp.maximum(u, 0.0)).astype(BF16)
    acc_ref[...] += jnp.dot(u, wd_ref[...], preferred_element_type=F32)

    @pl.when(f == pl.num_programs(1) - 1)
    def _():
        o_ref[...] = x_ref[...] + acc_ref[...]


def _mlp(x2, nrm, wu, wd, layer, tm=1024, tf=1024):
    Tn = x2.shape[0]
    return pl.pallas_call(
        _mlp_kernel,
        grid=(Tn // tm, D_FF // tf),
        in_specs=[pl.BlockSpec((tm, D_MODEL), lambda i, f: (i, 0)),
                  pl.BlockSpec((1, D_MODEL), lambda i, f: (0, 0)),
                  pl.BlockSpec((None, D_MODEL, tf), lambda i, f: (layer, 0, f)),
                  pl.BlockSpec((None, tf, D_MODEL), lambda i, f: (layer, f, 0))],
        out_specs=pl.BlockSpec((tm, D_MODEL), lambda i, f: (i, 0)),
        out_shape=jax.ShapeDtypeStruct((Tn, D_MODEL), F32),
        scratch_shapes=[pltpu.VMEM((tm, D_MODEL), BF16), pltpu.VMEM((tm, D_MODEL), F32)],
        compiler_params=_params(2),
        name="mlp",
    )(x2, nrm, wu, wd)


def _overlap(n_cmp):
    c_start = np.arange(NC_PAD) * CMP_STRIDE
    j_start = np.arange(N_SEL_BLK) * SEL_BLOCK
    ov = ((c_start[None, :] < j_start[:, None] + SEL_BLOCK) & (c_start[None, :] + CMP_BLOCK > j_start[:, None])
          & (np.arange(NC_PAD)[None, :] < n_cmp))
    return jnp.asarray(ov.astype(np.float32), BF16)


def _gate_expand():
    e = np.zeros((2, CG_PAD, 3 * MIX_WIDTH), np.float32)
    for h in range(C_HEADS):
        for j in range(3):
            e[:, h * 3 + j, j * MIX_WIDTH + h * HEAD_DIM:j * MIX_WIDTH + (h + 1) * HEAD_DIM] = 1.0
    return jnp.asarray(e.reshape(2 * CG_PAD, 3 * MIX_WIDTH), BF16)


def kernel(x, w_in, qk_gain, diff_lambda, diff_subln, sinks, cmp_pos, cmp_w1, cmp_w2,
           w_branch, w_out, norm_mix, norm_mlp, w_up, w_down, rel_bias):
    B, S, _ = x.shape
    depth = w_in.shape[0]
    n_cmp = (S - CMP_BLOCK) // CMP_STRIDE + 1
    assert S % T_ROW == 0 and S // CMP_STRIDE == NC_PAD and S // SEL_BLOCK == N_SEL_BLK
    half = CMP_BLOCK // 2 * HEAD_DIM

    w_heads = w_in[:, :, :C_CG].astype(BF16)
    w_cg = jnp.pad(w_in[:, :, C_CG:C_CG + N_CG], ((0, 0), (0, 0), (0, CG_PAD - N_CG))).astype(BF16)
    w_mg = w_in[:, :, C_CG + N_CG:].astype(BF16)
    gains2 = jnp.concatenate([qk_gain, qk_gain], axis=-1)
    w1 = cmp_w1.astype(BF16).reshape(depth, 2, 2, half, CMP_HIDDEN)
    w2p = jnp.pad(cmp_w2, ((0, 0), (0, 0), (0, 0), (0, LANES - HEAD_DIM))).astype(BF16)
    pos = cmp_pos.reshape(depth, 2, 2, half)
    wb = w_branch.astype(BF16)
    wo = w_out.astype(BF16)
    wu = w_up.astype(BF16)
    wd = w_down.astype(BF16)

    bias_a = rel_bias[:, :A_HEADS]
    bias_b = rel_bias[:, A_HEADS:A_HEADS + B_HEADS]
    bias_c = rel_bias[:, A_HEADS + B_HEADS:]
    bt_a = _bias_tiles(bias_a, T_ROW, 3, None, 1, center=True)
    bt_b = _bias_tiles(bias_b, T_SWA, -(-B_WINDOW // T_SWA) + 1, B_WINDOW, GROUP, rows=B_WINDOW + T_SWA)
    bt_sel = _bias_tiles(bias_c, T_ROW, 3, None, GROUP, center=True)
    bt_win = _bias_tiles(bias_c, T_ROW, C_WINDOW // T_ROW + 1, C_WINDOW, GROUP, rows=C_WINDOW + T_ROW)
    ovl = _overlap(n_cmp)
    expand_g = _gate_expand()

    for layer in range(depth):
        lam_init = 0.8 - 0.6 * math.exp(-0.3 * layer)
        (qat, ka, vat, qbt, kb, vbt, qct, tkv, ksel, vselt, kwin, vwint, gc) = _inproj(
            x, norm_mix[layer][None], w_heads, w_cg, gains2[layer], layer)
        oa = _flash_attn(qat, ka, vat, bt_a, dl=diff_lambda[layer],
                   subln=diff_subln[layer][None], lam_init=lam_init, name="attn_diff")
        ob = _window_attn(qbt, kb, vbt, bt_b, n_sub=8, out_dtype=BF16,
                          sink=sinks[layer].reshape(KV_GROUPS, GROUP, 1, 1), name="attn_swa")
        kc, vct = _compress(tkv, pos[layer], w1[layer], w2p[layer], qk_gain[layer][5:6])
        ocmp, sel = _cmp_attn(qct, kc, vct, ovl, n_cmp, n_sub=4)
        osel = _flash_attn(qct, ksel, vselt, bt_sel, sel=sel, name="attn_sel")
        owin = _window_attn(qct, kwin, vwint, bt_win, n_sub=4, name="attn_win")
        f2 = lambda a: a.reshape(B * S, a.shape[-1])
        x2 = _merge(f2(x), norm_mix[layer][None], f2(oa), f2(ob), f2(ocmp), f2(osel), f2(owin), f2(gc),
                    expand_g, w_mg, wb, wo, layer)
        x2 = _mlp(x2, norm_mlp[layer][None], wu, wd, layer)
        x = x2.reshape(B, S, D_MODEL)
    return x
```

```python
import functools
import math

import numpy as np
import jax
import jax.numpy as jnp
from jax import lax
from jax.experimental import pallas as pl
from jax.experimental.pallas import tpu as pltpu

F32 = jnp.float32
BF16 = jnp.bfloat16

D_MODEL = 1024
HEAD_DIM = 64
A_HEADS = 4
A_VDIM = 128
B_HEADS = 8
B_WINDOW = 128
C_HEADS = 8
KV_GROUPS = 2
GROUP = 4
CMP_BLOCK = 32
CMP_STRIDE = 16
CMP_HIDDEN = 256
SEL_BLOCK = 64
N_SELECT = 16
C_WINDOW = 512
MIX_WIDTH = 512
D_FF = 4096
NUM_BUCKETS = 32
MAX_DISTANCE = 128
NEG_INF = -1e30
FORCE_BONUS = 1e4
EPS = 1e-6
LOG2E = 1.4426950408889634
QK_SCALE = HEAD_DIM ** -0.5 * LOG2E

C_AQ, C_AK, C_AV, C_BQ, C_BKV, C_CQ, C_CKV, C_CG = 0, 512, 1024, 1536, 2048, 2304, 2816, 3584
N_CG = C_HEADS * 3
LANES = 128
CG_PAD = LANES
NC_PAD = 256
N_SEL_BLK = 64
ONES_ROWS = 16
MASK_ROWS = 16

SCORE_SLOTS = 8

T_IN = 512
T_ROW = 256
T_SWA = 256
TV_SWA = 128

VMEM_LIMIT = 48 * 1024 * 1024


def _rms(x, gain):
    ms = jnp.mean(x * x, axis=-1, keepdims=True)
    return (x * lax.rsqrt(ms + EPS)) * gain


def _params(n_axes):
    return pltpu.CompilerParams(dimension_semantics=("arbitrary",) * n_axes,
                                vmem_limit_bytes=VMEM_LIMIT)


def _inproj_kernel(x_ref, nrm_ref, w_ref, wcg_ref, gains_ref,
                   qa_ref, ka_ref, va_ref, qb_ref, kb_ref, vb_ref, qc_ref,
                   tkv_ref, ksel_ref, vsel_ref, kwin_ref, vwin_ref, gc_ref):
    tm = x_ref.shape[1]
    hb = _rms(x_ref[0], nrm_ref[...]).astype(BF16)
    lane = lax.broadcasted_iota(jnp.int32, (tm, LANES), 1)
    first = lane < HEAD_DIM

    def ones_pad(width):
        return jnp.where(lax.broadcasted_iota(jnp.int32, (ONES_ROWS, width), 0) == 0, 1.0, 0.0).astype(BF16)

    def mm(c0, n):
        return jnp.dot(hb, w_ref[:, c0:c0 + n], preferred_element_type=F32)

    def norm_pair(y, gain_idx, scale=None):
        sq = y * y
        ms0 = jnp.sum(jnp.where(first, sq, 0.0), axis=-1, keepdims=True) * (1.0 / HEAD_DIM)
        ms1 = jnp.sum(jnp.where(first, 0.0, sq), axis=-1, keepdims=True) * (1.0 / HEAD_DIM)
        inv = jnp.where(first, lax.rsqrt(ms0 + EPS), lax.rsqrt(ms1 + EPS))
        out = (y * inv) * gains_ref[gain_idx:gain_idx + 1, :]
        return out if scale is None else out * scale

    def pairs(c0):
        r = mm(c0, 256)
        return r[:, :LANES], r[:, LANES:]

    def put_qt(ref, i0, i1, y):
        yt = y.T.astype(BF16)
        n_tiles = ref.shape[3]
        tq = tm // n_tiles
        for t in range(n_tiles):
            ref[0, i0, i1, t] = yt[:HEAD_DIM, tq * t:tq * (t + 1)]
            ref[0, i0, i1 + 1, t] = yt[HEAD_DIM:, tq * t:tq * (t + 1)]

    def put_vt(ref, g0, y, n_tiles):
        yt = y.T.astype(BF16)
        tk = tm // n_tiles
        for g in range(2):
            for t in range(n_tiles):
                ref[0, g0 + g, t, 0:HEAD_DIM, :] = yt[HEAD_DIM * g:HEAD_DIM * (g + 1), tk * t:tk * (t + 1)]
                ref[0, g0 + g, t, HEAD_DIM:HEAD_DIM + ONES_ROWS, :] = ones_pad(tk)

    for ch in range(2):
        for half, y in enumerate(pairs(C_AQ + 256 * ch)):
            put_qt(qa_ref, ch * 2 + half, 0, norm_pair(y, 0, QK_SCALE))
        for half, y in enumerate(pairs(C_AK + 256 * ch)):
            y = norm_pair(y, 1).astype(BF16)
            ka_ref[0, ch * 2 + half, 0] = y[:, :HEAD_DIM]
            ka_ref[0, ch * 2 + half, 1] = y[:, HEAD_DIM:]
        for half, y in enumerate(pairs(C_AV + 256 * ch)):
            yt = y.T.astype(BF16)
            for t in range(tm // T_ROW):
                va_ref[0, ch * 2 + half, t, 0:A_VDIM, :] = yt[:, T_ROW * t:T_ROW * (t + 1)]
                va_ref[0, ch * 2 + half, t, A_VDIM:A_VDIM + ONES_ROWS, :] = ones_pad(T_ROW)
        for half, y in enumerate(pairs(C_BQ + 256 * ch)):
            put_qt(qb_ref, ch, 2 * half, norm_pair(y, 2, QK_SCALE))
        for half, y in enumerate(pairs(C_CQ + 256 * ch)):
            put_qt(qc_ref, ch, 2 * half, norm_pair(y, 4, QK_SCALE))

    def put_k(ref, y):
        y = y.astype(BF16)
        ref[0, 0] = y[:, :HEAD_DIM]
        ref[0, 1] = y[:, HEAD_DIM:]

    yk, yv = pairs(C_BKV)
    put_k(kb_ref, norm_pair(yk, 3))
    put_vt(vb_ref, 0, yv, tm // TV_SWA)
    y0, y1 = pairs(C_CKV)
    for kv, y in enumerate((y0, y1)):
        tkv_ref[0, kv, 0] = y[:, :HEAD_DIM]
        tkv_ref[0, kv, 1] = y[:, HEAD_DIM:]
    yk, yv = pairs(C_CKV + 256)
    yk = norm_pair(yk, 6)
    row = lax.broadcasted_iota(jnp.int32, (tm, LANES), 0)
    hot = jnp.where(lane == HEAD_DIM + (row // SEL_BLOCK) % (T_ROW // SEL_BLOCK), 1.0, 0.0)
    ksel_ref[0, 0] = jnp.where(first, yk, hot).astype(BF16)
    ksel_ref[0, 1] = jnp.where(first, pltpu.roll(yk, HEAD_DIM, 1), hot).astype(BF16)
    put_vt(vsel_ref, 0, yv, tm // T_ROW)
    yk, yv = pairs(C_CKV + 512)
    put_k(kwin_ref, norm_pair(yk, 7))
    put_vt(vwin_ref, 0, yv, tm // T_ROW)

    gc_ref[0] = jax.nn.sigmoid(jnp.dot(hb, wcg_ref[...], preferred_element_type=F32))


def _inproj(x, nrm, w, wcg, gains2, layer):
    B, S, _ = x.shape
    tm = T_IN
    nt = S // tm
    sd = jax.ShapeDtypeStruct
    qt = lambda a, b, tq: sd((B, a, b, S // tq, HEAD_DIM, tq), BF16)
    kk = lambda n: sd((B, n, S, HEAD_DIM), BF16)
    vt = lambda n, dv, tk: sd((B, n, S // tk, dv + ONES_ROWS, tk), BF16)
    out_shape = (
        qt(A_HEADS, 2, T_ROW), sd((B, A_HEADS, 2, S, HEAD_DIM), BF16), vt(A_HEADS, A_VDIM, T_ROW),
        qt(KV_GROUPS, GROUP, T_SWA), kk(KV_GROUPS), vt(KV_GROUPS, HEAD_DIM, TV_SWA),
        qt(KV_GROUPS, GROUP, T_ROW),
        sd((B, 2, KV_GROUPS, S, HEAD_DIM), F32),
        sd((B, KV_GROUPS, S, LANES), BF16), vt(KV_GROUPS, HEAD_DIM, T_ROW),
        kk(KV_GROUPS), vt(KV_GROUPS, HEAD_DIM, T_ROW),
        sd((B, S, CG_PAD), F32),
    )
    s_qt = lambda a, b, tq: pl.BlockSpec((1, a, b, tm // tq, HEAD_DIM, tq), lambda b_, i: (b_, 0, 0, i, 0, 0))
    s_k = lambda n: pl.BlockSpec((1, n, tm, HEAD_DIM), lambda b_, i: (b_, 0, i, 0))
    s_k5 = lambda a, c: pl.BlockSpec((1, a, c, tm, HEAD_DIM), lambda b_, i: (b_, 0, 0, i, 0))
    s_vt = lambda n, dv, tk: pl.BlockSpec((1, n, tm // tk, dv + ONES_ROWS, tk), lambda b_, i: (b_, 0, i, 0, 0))
    row = lambda d: pl.BlockSpec((1, tm, d), lambda b_, i: (b_, i, 0))
    out_specs = (
        s_qt(A_HEADS, 2, T_ROW), s_k5(A_HEADS, 2), s_vt(A_HEADS, A_VDIM, T_ROW),
        s_qt(KV_GROUPS, GROUP, T_SWA), s_k(KV_GROUPS), s_vt(KV_GROUPS, HEAD_DIM, TV_SWA),
        s_qt(KV_GROUPS, GROUP, T_ROW), s_k5(2, KV_GROUPS),
        pl.BlockSpec((1, KV_GROUPS, tm, LANES), lambda b_, i: (b_, 0, i, 0)), s_vt(KV_GROUPS, HEAD_DIM, T_ROW),
        s_k(KV_GROUPS), s_vt(KV_GROUPS, HEAD_DIM, T_ROW),
        row(CG_PAD),
    )
    return pl.pallas_call(
        _inproj_kernel,
        grid=(B, nt),
        in_specs=[
            row(D_MODEL),
            pl.BlockSpec((1, D_MODEL), lambda b_, i: (0, 0)),
            pl.BlockSpec((None, D_MODEL, C_CG), lambda b_, i: (layer, 0, 0), pipeline_mode=pl.Buffered(1)),
            pl.BlockSpec((None, D_MODEL, CG_PAD), lambda b_, i: (layer, 0, 0)),
            pl.BlockSpec((8, LANES), lambda b_, i: (0, 0)),
        ],
        out_specs=out_specs,
        out_shape=out_shape,
        compiler_params=_params(2),
        name="inproj",
    )(x, nrm, w, wcg, gains2)


def _compress_kernel(t_ref, pos_ref, w1_ref, w2_ref, gain_ref, o_ref, ot_ref):
    t = jnp.concatenate([t_ref[0, 0, 0, pl.ds(b, NC_PAD, stride=CMP_STRIDE), :] for b in range(CMP_STRIDE)], axis=1)
    lo = (t + pos_ref[0, 0:1, :]).astype(BF16)
    hi = (t + pos_ref[0, 1:2, :]).astype(BF16)
    v = jnp.dot(lo, w1_ref[0, 0], preferred_element_type=F32)
    u = jnp.dot(hi, w1_ref[0, 1], preferred_element_type=F32)
    pre = v + pltpu.roll(u, NC_PAD - 1, 0)
    hcur = jax.nn.gelu(pre).astype(BF16)
    out = jnp.dot(hcur, w2_ref[0], preferred_element_type=F32)
    o64 = out[:, :HEAD_DIM]
    is_key = pl.program_id(1) == 0
    o_ref[0, 0, 0] = jnp.where(is_key, _rms(o64, gain_ref[...]), o64).astype(BF16)
    ot_ref[0, 0, 0] = out.T[:HEAD_DIM].astype(BF16)


def _compress(tkv, pos, w1, w2p, gain):
    B, S = tkv.shape[0], tkv.shape[3]
    half = CMP_BLOCK // 2 * HEAD_DIM
    return pl.pallas_call(
        _compress_kernel,
        grid=(B, 2, KV_GROUPS),
        in_specs=[
            pl.BlockSpec((1, 1, 1, S, HEAD_DIM), lambda b, kv, g: (b, kv, g, 0, 0)),
            pl.BlockSpec((1, 2, half), lambda b, kv, g: (kv, 0, 0)),
            pl.BlockSpec((1, 2, half, CMP_HIDDEN), lambda b, kv, g: (kv, 0, 0, 0)),
            pl.BlockSpec((1, CMP_HIDDEN, LANES), lambda b, kv, g: (kv, 0, 0)),
            pl.BlockSpec((1, HEAD_DIM), lambda b, kv, g: (0, 0)),
        ],
        out_specs=(pl.BlockSpec((1, 1, 1, NC_PAD, HEAD_DIM), lambda b, kv, g: (b, kv, g, 0, 0)),
                   pl.BlockSpec((1, 1, 1, HEAD_DIM, NC_PAD), lambda b, kv, g: (b, kv, g, 0, 0))),
        out_shape=(jax.ShapeDtypeStruct((B, 2, KV_GROUPS, NC_PAD, HEAD_DIM), BF16),
                   jax.ShapeDtypeStruct((B, 2, KV_GROUPS, HEAD_DIM, NC_PAD), BF16)),
        compiler_params=_params(3),
        name="compress",
    )(tkv, pos, w1, w2p, gain)


def _cmp_attn_kernel(qt_ref, kc_ref, vct_ref, ovl_ref, o_ref, sel_ref, score_ref, rank_ref, *, T, n_cmp, n_sub):
    R = GROUP
    tiles = [pl.program_id(2) * n_sub + t for t in range(n_sub)]
    j_idx = lax.broadcasted_iota(jnp.int32, (N_SEL_BLK, T), 0)
    lane_q = lax.broadcasted_iota(jnp.int32, (N_SEL_BLK, T), 1)

    def attend(rows):
        kc = kc_ref[0, 0, 0, 0:rows, :]
        vct = vct_ref[0, 0, 0, :, 0:rows]
        ovl = ovl_ref[:, 0:rows]
        c_idx = lax.broadcasted_iota(jnp.int32, (rows, T), 0)
        lane_k = lax.broadcasted_iota(jnp.int32, (rows, T), 1)
        for t, I in enumerate(tiles):
            cmask = (c_idx * CMP_STRIDE + (CMP_BLOCK - 1) <= I * T + lane_k) & (c_idx < n_cmp)
            scores = [jnp.dot(kc, qt_ref[0, 0, r, t], preferred_element_type=F32) for r in range(R)]
            probs = []
            for r in range(R):
                s = jnp.where(cmask, scores[r], NEG_INF)
                m = jnp.max(s, axis=0, keepdims=True)
                e = jnp.where(cmask, jnp.exp2(s - m), 0.0)
                l = jnp.sum(e, axis=0, keepdims=True)
                probs.append(e * (1.0 / jnp.where(l > 0.0, l, 1.0)))
            outs = [jnp.dot(vct, p.astype(BF16), preferred_element_type=F32) for p in probs]
            o_ref[0, t * T:(t + 1) * T, :] = jnp.concatenate(outs, axis=0).T
            psum = (probs[0] + probs[1]) + (probs[2] + probs[3])
            p_hi = psum.astype(BF16)
            p_lo = (psum - p_hi.astype(F32)).astype(BF16)
            imp = jnp.dot(ovl, p_hi, preferred_element_type=F32) + jnp.dot(ovl, p_lo, preferred_element_type=F32)
            tq = I * T + lane_q
            cur = tq // SEL_BLOCK
            valid = j_idx * SEL_BLOCK <= tq
            forced = (j_idx == 0) | (j_idx == cur) | (j_idx == cur - 1)
            score_ref[t] = jnp.where(valid, imp + jnp.where(forced, FORCE_BONUS, 0.0), NEG_INF)

    chunk = NC_PAD // 4
    vis_per_tile = T // CMP_STRIDE
    for v in range(4):
        lo = -(-(v * chunk) // vis_per_tile)
        hi = -(-((v + 1) * chunk) // vis_per_tile)
        assert lo % n_sub == 0 and hi % n_sub == 0

        @pl.when((tiles[0] >= lo) & (tiles[0] < hi))
        def _(v=v):
            attend((v + 1) * chunk)

    rank_ref[...] = jnp.zeros(rank_ref.shape, F32)
    n_live = (tiles[-1] + 1) * (T // SEL_BLOCK)
    sub = 8
    n_grp = N_SEL_BLK // sub
    j_loc = lax.broadcasted_iota(jnp.int32, (sub, T), 0)
    for c in range(n_grp):
        @pl.when(c * sub < n_live)
        def _(c=c):
            for t in range(n_sub):
                grps = [score_ref[t, g * sub:(g + 1) * sub, :] for g in range(n_grp)]
                cnts = [rank_ref[t, g * sub:(g + 1) * sub, :] for g in range(n_grp)]
                for jp in range(c * sub, (c + 1) * sub):
                    row = jnp.broadcast_to(score_ref[t, jp:jp + 1, :], (sub, T))
                    for g in range(n_grp):
                        if g * sub > jp:
                            beats = row >= grps[g]
                        elif (g + 1) * sub - 1 < jp:
                            beats = row > grps[g]
                        else:
                            beats = (row > grps[g]) | ((row == grps[g]) & (j_loc + g * sub > jp))
                        cnts[g] = cnts[g] + jnp.where(beats, 1.0, 0.0)
                for g in range(n_grp):
                    rank_ref[t, g * sub:(g + 1) * sub, :] = cnts[g]
    blk = T // SEL_BLOCK
    for t in range(n_sub):
        for j in range(N_SEL_BLK // blk):
            keep = rank_ref[t, j * blk:(j + 1) * blk, :] < N_SELECT
            slab = jnp.concatenate([jnp.where(keep, 0.0, NEG_INF), jnp.zeros((MASK_ROWS - blk, T), F32)], axis=0)
            sel_ref[0, 0, t, j] = slab.astype(BF16)


def _cmp_attn(qct, kc, vct, ovl, n_cmp, n_sub=2):
    B, G, R, NQ, _, T = qct.shape
    S = NQ * T
    return pl.pallas_call(
        functools.partial(_cmp_attn_kernel, T=T, n_cmp=n_cmp, n_sub=n_sub),
        grid=(B, G, NQ // n_sub),
        in_specs=[
            pl.BlockSpec((1, 1, R, n_sub, HEAD_DIM, T), lambda b, g, i: (b, g, 0, i, 0, 0)),
            pl.BlockSpec((1, 1, 1, NC_PAD, HEAD_DIM), lambda b, g, i: (b, 0, g, 0, 0)),
            pl.BlockSpec((1, 1, 1, HEAD_DIM, NC_PAD), lambda b, g, i: (b, 1, g, 0, 0)),
            pl.BlockSpec((N_SEL_BLK, NC_PAD), lambda b, g, i: (0, 0)),
        ],
        out_specs=(
            pl.BlockSpec((1, n_sub * T, R * HEAD_DIM), lambda b, g, i: (b, i, g)),
            pl.BlockSpec((1, 1, n_sub, NQ, MASK_ROWS, T), lambda b, g, i: (b, g, i, 0, 0, 0)),
        ),
        out_shape=(
            jax.ShapeDtypeStruct((B, S, G * R * HEAD_DIM), F32),
            jax.ShapeDtypeStruct((B, G, NQ, NQ, MASK_ROWS, T), BF16),
        ),
        scratch_shapes=[pltpu.VMEM((n_sub, N_SEL_BLK, T), F32), pltpu.VMEM((n_sub, N_SEL_BLK, T), F32)],
        compiler_params=_params(3),
        name="cmp_attn",
    )(qct, kc, vct, ovl)


def _flash_kernel(*refs, R, T, NQ, Dv, nrel, k_per_r, shared_bias, diff, has_sel, early_bias, lam_init):
    it = iter(refs)
    qt_ref, k_ref, vt_ref, bt_ref = (next(it) for _ in range(4))
    sel_ref = next(it) if has_sel else None
    dl_ref = next(it) if diff else None
    sub_ref = next(it) if diff else None
    o_ref, m_ref, acc_ref, s_ref, p_ref, al_ref, tmax_ref = (next(it) for _ in range(7))

    m_ref[...] = jnp.full(m_ref.shape, NEG_INF, F32)
    acc_ref[...] = jnp.zeros(acc_ref.shape, F32)
    bias_r = (lambda r: 0) if shared_bias else (lambda r: r)
    if has_sel:
        pad_rows = jnp.zeros((k_ref.shape[-1] - HEAD_DIM - MASK_ROWS, T), BF16)
    n_steps = NQ * (NQ + 1) // 2
    N_S = SCORE_SLOTS
    assert N_S % 2 == 0 and n_steps % N_S == 0

    def score_stage(I, J, slot, table):
        off = pl.multiple_of(J * T, T)
        rel = jnp.minimum(I - J, nrel - 1)
        for r in range(R):
            kt = k_ref[0, 0, r, pl.ds(off, T), :] if k_per_r else k_ref[0, 0, pl.ds(off, T), :]
            w = qt_ref[0, 0, r, I]
            if has_sel:
                w = jnp.concatenate([w, sel_ref[0, 0, I, J], pad_rows], axis=0)
            s = jnp.dot(kt, w, preferred_element_type=F32)
            if early_bias:
                if table:
                    s = s + bt_ref[0, rel, bias_r(r)]
                tmax_ref[slot, r] = jnp.max(s, axis=0, keepdims=True)
            s_ref[slot, r] = s

    def softmax_stage(I, J, s_slot, slot, table):
        for r in range(R):
            m_prev = m_ref[I, r]
            s = s_ref[s_slot, r]
            if early_bias:
                tile_max = tmax_ref[s_slot, r]
            else:
                if table:
                    s = s + bt_ref[0, jnp.minimum(I - J, nrel - 1), bias_r(r)]
                tile_max = jnp.max(s, axis=0, keepdims=True)
            m_new = jnp.maximum(m_prev, tile_max)
            al_ref[slot, r] = jnp.exp2(m_prev - m_new)
            p_ref[slot, r] = jnp.exp2(s - m_new).astype(BF16)
            m_ref[I, r] = m_new

    def value_stage(J, slot):
        vt = vt_ref[0, 0, J]
        return [jnp.dot(vt, p_ref[slot, r], preferred_element_type=F32) for r in range(R)]

    def accumulate(I, slot, pvs):
        for r in range(R):
            acc_ref[I, r] = al_ref[slot, r] * acc_ref[I, r] + pvs[r]

    def far_advance(I, J):
        wrap = J + 1 > I - (nrel - 1)
        return jnp.where(wrap, I + 1, I), jnp.where(wrap, 0, J + 1)

    def near_advance(I, J):
        on_diag, before_diag = J == I, J == I - 1
        nxt_i = jnp.where(on_diag, I + 1, jnp.where(before_diag, I, 0))
        nxt_j = jnp.where(on_diag, I, jnp.where(before_diag, I, 0))
        return nxt_i, nxt_j

    assert nrel == 3
    n_far = (NQ - 1) * (NQ - 2) // 2
    n_near = 2 * NQ - 1
    far_trips, far_left = divmod(n_far, N_S)
    assert far_left == 1 and (far_left + n_near) % N_S == 0
    near_trips = (far_left + n_near) // N_S

    def step(cur, prev, t, advance, table):
        other = 1 - t % 2
        nxt = advance(*cur)
        pvs = value_stage(prev[1], other)
        score_stage(jnp.minimum(nxt[0], NQ - 1), jnp.minimum(nxt[1], NQ - 1), (t + 1) % N_S, table)
        softmax_stage(cur[0], cur[1], t % N_S, t % 2, table)
        accumulate(prev[0], other, pvs)
        return nxt

    def make_trip(advance, table):
        def trip(u, carry):
            cur, prev = carry[:2], carry[2:]
            for t in range(N_S):
                cur, prev = step(cur, prev, t, advance, table), cur
            return (*cur, *prev)
        return trip

    first = (jnp.int32(nrel - 1), jnp.int32(0))
    score_stage(*first, 0, False)
    p_ref[1] = jnp.zeros(p_ref.shape[1:], BF16)
    al_ref[1] = jnp.ones(al_ref.shape[1:], F32)
    carry = lax.fori_loop(0, far_trips, make_trip(far_advance, False), (*first, *first))
    lax.fori_loop(0, near_trips, make_trip(near_advance, True), carry)
    accumulate(NQ - 1, 1, value_stage(NQ - 1, 1))

    if diff:
        dl = dl_ref[...]
        lam = (jnp.exp(jnp.sum(dl[0:1] * dl[1:2], keepdims=True))
               - jnp.exp(jnp.sum(dl[2:3] * dl[3:4], keepdims=True)) + lam_init)

    def finish(I, carry):
        outs = []
        for r in range(R):
            acc = acc_ref[I, r]
            outs.append(acc[:Dv] * (1.0 / acc[Dv:Dv + 1]))
        rows = pl.ds(pl.multiple_of(I * T, T), T)
        if diff:
            o = (outs[0] - lam * outs[1]).T
            o_ref[0, rows, :] = (_rms(o, sub_ref[...]) * (1.0 - lam_init)).astype(o_ref.dtype)
        else:
            o_ref[0, rows, :] = jnp.concatenate(outs, axis=0).T
        return carry

    lax.fori_loop(0, NQ, finish, 0)


def _flash_attn(qt, k, vt, bias_tab, *, sel=None, dl=None, subln=None, lam_init=0.0, name="attn"):
    B, G, R, NQ, _, T = qt.shape
    S = NQ * T
    Dv = vt.shape[-2] - ONES_ROWS
    nrel, bias_heads = bias_tab.shape[1], bias_tab.shape[2]
    k_per_r = k.ndim == 5
    has_sel = sel is not None
    diff = dl is not None
    ins = [qt, k, vt, bias_tab]
    in_specs = [
        pl.BlockSpec((1, 1, R, NQ, HEAD_DIM, T), lambda b, g: (b, g, 0, 0, 0, 0)),
        (pl.BlockSpec((1, 1, R, S, HEAD_DIM), lambda b, g: (b, g, 0, 0, 0)) if k_per_r
         else pl.BlockSpec((1, 1, S, k.shape[-1]), lambda b, g: (b, g, 0, 0))),
        pl.BlockSpec((1, 1, NQ, Dv + ONES_ROWS, T), lambda b, g: (b, g, 0, 0, 0)),
        pl.BlockSpec((1, nrel, bias_heads, T, T), lambda b, g: (g, 0, 0, 0, 0)),
    ]
    if has_sel:
        ins.append(sel)
        in_specs.append(pl.BlockSpec((1, 1, NQ, NQ, MASK_ROWS, T), lambda b, g: (b, g, 0, 0, 0, 0)))
    if diff:
        ins += [dl, subln]
        in_specs += [pl.BlockSpec((4, HEAD_DIM), lambda b, g: (0, 0)),
                     pl.BlockSpec((1, A_VDIM), lambda b, g: (0, 0))]
    out_w = Dv if diff else R * Dv
    kern = functools.partial(_flash_kernel, R=R, T=T, NQ=NQ, Dv=Dv, nrel=nrel, k_per_r=k_per_r,
                             shared_bias=bias_heads == 1, diff=diff, has_sel=has_sel, early_bias=True,
                             lam_init=lam_init)
    return pl.pallas_call(
        kern,
        grid=(B, G),
        in_specs=in_specs,
        out_specs=pl.BlockSpec((1, S, out_w), lambda b, g: (b, 0, g)),
        out_shape=jax.ShapeDtypeStruct((B, S, G * out_w), BF16 if diff else F32),
        scratch_shapes=[pltpu.VMEM((NQ, R, 1, T), F32), pltpu.VMEM((NQ, R, Dv + ONES_ROWS, T), F32),
                        pltpu.VMEM((SCORE_SLOTS, R, T, T), F32), pltpu.VMEM((2, R, T, T), BF16),
                        pltpu.VMEM((2, R, 1, T), F32), pltpu.VMEM((SCORE_SLOTS, R, 1, T), F32)],
        compiler_params=_params(2),
        name=name,
    )(*ins)


def _window_kernel(*refs, R, T, NK, TV, Dv, n_sub, has_sink):
    it = iter(refs)
    qt_ref, k_ref, vt_ref, bt_ref = (next(it) for _ in range(4))
    sink_ref = next(it) if has_sink else None
    o_ref = next(it)
    heads = [(t, r) for t in range(n_sub) for r in range(R)]
    tile = [pl.program_id(2) * n_sub + t for t in range(n_sub)]
    entry = [jnp.minimum(I, -(-(NK - T) // T)) for I in tile]
    first_key = [pl.multiple_of(I * T - jnp.minimum(e * T, NK - T), TV) for I, e in zip(tile, entry)]
    keys = [k_ref[0, 0, pl.ds(fk, NK), :] for fk in first_key]
    scores = {(t, r): jnp.dot(keys[t], qt_ref[0, 0, r, t], preferred_element_type=F32) for t, r in heads}
    probs, maxes = {}, {}
    for t, r in heads:
        s = scores[t, r] + bt_ref[0, entry[t], r]
        maxes[t, r] = jnp.max(s, axis=0, keepdims=True)
        probs[t, r] = jnp.exp2(s - maxes[t, r]).astype(BF16)
    accs = {}
    for t, r in heads:
        acc = None
        for c in range(NK // TV):
            part = jnp.dot(vt_ref[0, 0, first_key[t] // TV + c], probs[t, r][c * TV:(c + 1) * TV],
                           preferred_element_type=F32)
            acc = part if acc is None else acc + part
        accs[t, r] = acc
    for t in range(n_sub):
        outs = []
        for r in range(R):
            num, l = accs[t, r][:Dv], accs[t, r][Dv:Dv + 1]
            if has_sink:
                sk = sink_ref[0, r] * LOG2E
                m_f = jnp.maximum(maxes[t, r], sk)
                w = jnp.exp2(maxes[t, r] - m_f)
                outs.append(num * (w / (l * w + jnp.exp2(sk - m_f))))
            else:
                outs.append(num * (1.0 / l))
        o_ref[0, t * T:(t + 1) * T, :] = jnp.concatenate(outs, axis=0).T.astype(o_ref.dtype)


def _window_attn(qt, k, vt, bias_tab, *, n_sub=2, sink=None, out_dtype=F32, name="attn_window"):
    B, G, R, NQ, _, T = qt.shape
    S = NQ * T
    TV = vt.shape[-1]
    Dv = vt.shape[-2] - ONES_ROWS
    entries, NK = bias_tab.shape[1], bias_tab.shape[3]
    has_sink = sink is not None
    ins = [qt, k, vt, bias_tab]
    in_specs = [
        pl.BlockSpec((1, 1, R, n_sub, HEAD_DIM, T), lambda b, g, i: (b, g, 0, i, 0, 0)),
        pl.BlockSpec((1, 1, S, HEAD_DIM), lambda b, g, i: (b, g, 0, 0)),
        pl.BlockSpec((1, 1, S // TV, Dv + ONES_ROWS, TV), lambda b, g, i: (b, g, 0, 0, 0)),
        pl.BlockSpec((1, entries, R, NK, T), lambda b, g, i: (g, 0, 0, 0, 0)),
    ]
    if has_sink:
        ins.append(sink)
        in_specs.append(pl.BlockSpec((1, R, 1, 1), lambda b, g, i: (g, 0, 0, 0)))
    kern = functools.partial(_window_kernel, R=R, T=T, NK=NK, TV=TV, Dv=Dv, n_sub=n_sub, has_sink=has_sink)
    return pl.pallas_call(
        kern,
        grid=(B, G, NQ // n_sub),
        in_specs=in_specs,
        out_specs=pl.BlockSpec((1, n_sub * T, R * Dv), lambda b, g, i: (b, i, g)),
        out_shape=jax.ShapeDtypeStruct((B, S, G * R * Dv), out_dtype),
        compiler_params=_params(3),
        name=name,
    )(*ins)


def _bucket_np(dist):
    n = np.maximum(dist, 0)
    max_exact = NUM_BUCKETS // 2
    nf = np.maximum(n, 1).astype(np.float32)
    large = max_exact + (np.log(nf / max_exact) / math.log(MAX_DISTANCE / max_exact)
                         * (NUM_BUCKETS - max_exact)).astype(np.int32)
    large = np.minimum(large, NUM_BUCKETS - 1)
    return np.where(n < max_exact, n, large)


def _bucket_starts():
    b = _bucket_np(np.arange(4 * MAX_DISTANCE))
    assert (np.diff(b) >= 0).all() and b[-1] == NUM_BUCKETS - 1
    return [int(np.argmax(b >= i)) for i in range(NUM_BUCKETS)]


def _bias_kernel(tab_ref, o_ref, *, T, rows, nrel, window, back_max, center, starts):
    h = pl.program_id(0)
    base = tab_ref[NUM_BUCKETS - 1, h] * LOG2E if center else 0.0
    key = lax.broadcasted_iota(jnp.int32, (rows, T), 0)
    qry = lax.broadcasted_iota(jnp.int32, (rows, T), 1)
    for rel in range(nrel):
        dist = min(rel * T, back_max) + qry - key
        val = jnp.full((rows, T), tab_ref[0, h] * LOG2E - base, F32)
        for b in range(1, NUM_BUCKETS):
            val = jnp.where(dist >= starts[b], tab_ref[b, h] * LOG2E - base, val)
        allowed = dist >= 0
        if window is not None:
            allowed = allowed & (dist < window)
        o_ref[0, rel, 0] = jnp.where(allowed, val, NEG_INF)


def _bias_tiles(tab, T, nrel, window, R, rows=None, center=False):
    H = tab.shape[1]
    back_max = nrel * T if rows is None else rows - T
    rows = T if rows is None else rows
    return pl.pallas_call(
        functools.partial(_bias_kernel, T=T, rows=rows, nrel=nrel, window=window, back_max=back_max,
                          center=center, starts=_bucket_starts()),
        grid=(H,),
        in_specs=[pl.BlockSpec(memory_space=pltpu.SMEM)],
        out_specs=pl.BlockSpec((1, nrel, 1, rows, T), lambda h: (h // R, 0, h % R, 0, 0)),
        out_shape=jax.ShapeDtypeStruct((H // R, nrel, R, rows, T), F32),
        compiler_params=_params(1),
        name="bias_tiles",
    )(tab)


def _merge_kernel(x_ref, nrm_ref, oa_ref, ob_ref, oc_ref, os_ref, ow_ref, gc_ref, ex_ref, wmg_ref, wb_ref, wo_ref,
                  o_ref):
    x = x_ref[...]
    hb = _rms(x, nrm_ref[...]).astype(BF16)
    gc = gc_ref[...]
    g_hi = gc.astype(BF16)
    g_lo = (gc - g_hi.astype(F32)).astype(BF16)
    gx = jnp.dot(jnp.concatenate([g_hi, g_lo], axis=1), ex_ref[...], preferred_element_type=F32)
    oc = (gx[:, 0:MIX_WIDTH] * oc_ref[...] + gx[:, MIX_WIDTH:2 * MIX_WIDTH] * os_ref[...]
          + gx[:, 2 * MIX_WIDTH:3 * MIX_WIDTH] * ow_ref[...])
    z = None
    for n, br in enumerate((oa_ref[...], ob_ref[...], oc)):
        y = jnp.dot(br.astype(BF16), wb_ref[n], preferred_element_type=F32)
        gate = jax.nn.sigmoid(jnp.dot(hb, wmg_ref[:, n * D_MODEL:(n + 1) * D_MODEL], preferred_element_type=F32))
        t = gate * y
        z = t if z is None else z + t
    o_ref[...] = x + jnp.dot(z.astype(BF16), wo_ref[...], preferred_element_type=F32)


def _merge(x2, nrm, oa, ob, oc, osel, ow, gc, expand_g, wmg, wb, wo, layer, tm=512):
    Tn = x2.shape[0]
    row = lambda d: pl.BlockSpec((tm, d), lambda i: (i, 0))
    const = lambda shape: pl.BlockSpec(shape, lambda i: (0,) * len(shape), pipeline_mode=pl.Buffered(1))
    weight = lambda shape: pl.BlockSpec((None,) + shape, lambda i: (layer,) + (0,) * len(shape),
                                        pipeline_mode=pl.Buffered(1))
    return pl.pallas_call(
        _merge_kernel,
        grid=(Tn // tm,),
        in_specs=[row(D_MODEL), const((1, D_MODEL)),
                  row(MIX_WIDTH), row(MIX_WIDTH), row(MIX_WIDTH), row(MIX_WIDTH), row(MIX_WIDTH),
                  row(CG_PAD),
                  const((2 * CG_PAD, 3 * MIX_WIDTH)),
                  weight((D_MODEL, 3 * D_MODEL)),
                  weight((3, MIX_WIDTH, D_MODEL)),
                  weight((D_MODEL, D_MODEL))],
        out_specs=row(D_MODEL),
        out_shape=jax.ShapeDtypeStruct((Tn, D_MODEL), F32),
        compiler_params=_params(1),
        name="merge",
    )(x2, nrm, oa, ob, oc, osel, ow, gc, expand_g, wmg, wb, wo)


def _mlp_kernel(x_ref, nrm_ref, wu_ref, wd_ref, o_ref, h_ref, acc_ref):
    f = pl.program_id(1)

    @pl.when(f == 0)
    def _():
        h_ref[...] = _rms(x_ref[...], nrm_ref[...]).astype(BF16)
        acc_ref[...] = jnp.zeros(acc_ref.shape, F32)

    u = jnp.dot(h_ref[...], wu_ref[...], preferred_element_type=F32)
    u = jnp.square(jnp.maximum(u, 0.0)).astype(BF16)
    acc_ref[...] += jnp.dot(u, wd_ref[...], preferred_element_type=F32)

    @pl.when(f == pl.num_programs(1) - 1)
    def _():
        o_ref[...] = x_ref[...] + acc_ref[...]


def _mlp(x2, nrm, wu, wd, layer, tm=1024, tf=1024):
    Tn = x2.shape[0]
    return pl.pallas_call(
        _mlp_kernel,
        grid=(Tn // tm, D_FF // tf),
        in_specs=[pl.BlockSpec((tm, D_MODEL), lambda i, f: (i, 0)),
                  pl.BlockSpec((1, D_MODEL), lambda i, f: (0, 0)),
                  pl.BlockSpec((None, D_MODEL, tf), lambda i, f: (layer, 0, f)),
                  pl.BlockSpec((None, tf, D_MODEL), lambda i, f: (layer, f, 0))],
        out_specs=pl.BlockSpec((tm, D_MODEL), lambda i, f: (i, 0)),
        out_shape=jax.ShapeDtypeStruct((Tn, D_MODEL), F32),
        scratch_shapes=[pltpu.VMEM((tm, D_MODEL), BF16), pltpu.VMEM((tm, D_MODEL), F32)],
        compiler_params=_params(2),
        name="mlp",
    )(x2, nrm, wu, wd)


def _overlap(n_cmp):
    c_start = np.arange(NC_PAD) * CMP_STRIDE
    j_start = np.arange(N_SEL_BLK) * SEL_BLOCK
    ov = ((c_start[None, :] < j_start[:, None] + SEL_BLOCK) & (c_start[None, :] + CMP_BLOCK > j_start[:, None])
          & (np.arange(NC_PAD)[None, :] < n_cmp))
    return jnp.asarray(ov.astype(np.float32), BF16)


def _gate_expand():
    e = np.zeros((2, CG_PAD, 3 * MIX_WIDTH), np.float32)
    for h in range(C_HEADS):
        for j in range(3):
            e[:, h * 3 + j, j * MIX_WIDTH + h * HEAD_DIM:j * MIX_WIDTH + (h + 1) * HEAD_DIM] = 1.0
    return jnp.asarray(e.reshape(2 * CG_PAD, 3 * MIX_WIDTH), BF16)


def kernel(x, w_in, qk_gain, diff_lambda, diff_subln, sinks, cmp_pos, cmp_w1, cmp_w2,
           w_branch, w_out, norm_mix, norm_mlp, w_up, w_down, rel_bias):
    B, S, _ = x.shape
    depth = w_in.shape[0]
    n_cmp = (S - CMP_BLOCK) // CMP_STRIDE + 1
    assert S % T_ROW == 0 and S // CMP_STRIDE == NC_PAD and S // SEL_BLOCK == N_SEL_BLK
    half = CMP_BLOCK // 2 * HEAD_DIM

    w_heads = w_in[:, :, :C_CG].astype(BF16)
    w_cg = jnp.pad(w_in[:, :, C_CG:C_CG + N_CG], ((0, 0), (0, 0), (0, CG_PAD - N_CG))).astype(BF16)
    w_mg = w_in[:, :, C_CG + N_CG:].astype(BF16)
    gains2 = jnp.concatenate([qk_gain, qk_gain], axis=-1)
    w1 = cmp_w1.astype(BF16).reshape(depth, 2, 2, half, CMP_HIDDEN)
    w2p = jnp.pad(cmp_w2, ((0, 0), (0, 0), (0, 0), (0, LANES - HEAD_DIM))).astype(BF16)
    pos = cmp_pos.reshape(depth, 2, 2, half)
    wb = w_branch.astype(BF16)
    wo = w_out.astype(BF16)
    wu = w_up.astype(BF16)
    wd = w_down.astype(BF16)

    bias_a = rel_bias[:, :A_HEADS]
    bias_b = rel_bias[:, A_HEADS:A_HEADS + B_HEADS]
    bias_c = rel_bias[:, A_HEADS + B_HEADS:]
    bt_a = _bias_tiles(bias_a, T_ROW, 3, None, 1, center=True)
    bt_b = _bias_tiles(bias_b, T_SWA, -(-B_WINDOW // T_SWA) + 1, B_WINDOW, GROUP, rows=B_WINDOW + T_SWA)
    bt_sel = _bias_tiles(bias_c, T_ROW, 3, None, GROUP, center=True)
    bt_win = _bias_tiles(bias_c, T_ROW, C_WINDOW // T_ROW + 1, C_WINDOW, GROUP, rows=C_WINDOW + T_ROW)
    ovl = _overlap(n_cmp)
    expand_g = _gate_expand()

    for layer in range(depth):
        lam_init = 0.8 - 0.6 * math.exp(-0.3 * layer)
        (qat, ka, vat, qbt, kb, vbt, qct, tkv, ksel, vselt, kwin, vwint, gc) = _inproj(
            x, norm_mix[layer][None], w_heads, w_cg, gains2[layer], layer)
        oa = _flash_attn(qat, ka, vat, bt_a, dl=diff_lambda[layer],
                   subln=diff_subln[layer][None], lam_init=lam_init, name="attn_diff")
        ob = _window_attn(qbt, kb, vbt, bt_b, n_sub=8, out_dtype=BF16,
                          sink=sinks[layer].reshape(KV_GROUPS, GROUP, 1, 1), name="attn_swa")
        kc, vct = _compress(tkv, pos[layer], w1[layer], w2p[layer], qk_gain[layer][5:6])
        ocmp, sel = _cmp_attn(qct, kc, vct, ovl, n_cmp, n_sub=4)
        osel = _flash_attn(qct, ksel, vselt, bt_sel, sel=sel, name="attn_sel")
        owin = _window_attn(qct, kwin, vwint, bt_win, n_sub=4, name="attn_win")
        f2 = lambda a: a.reshape(B * S, a.shape[-1])
        x2 = _merge(f2(x), norm_mix[layer][None], f2(oa), f2(ob), f2(ocmp), f2(osel), f2(owin), f2(gc),
                    expand_g, w_mg, wb, wo, layer)
        x2 = _mlp(x2, norm_mlp[layer][None], wu, wd, layer)
        x = x2.reshape(B, S, D_MODEL)
    return x
```

```python
import functools
import math

import numpy as np
import jax
import jax.numpy as jnp
from jax import lax
from jax.experimental import pallas as pl
from jax.experimental.pallas import tpu as pltpu

F32 = jnp.float32
BF16 = jnp.bfloat16

D_MODEL = 1024
HEAD_DIM = 64
A_HEADS = 4
A_VDIM = 128
B_HEADS = 8
B_WINDOW = 128
C_HEADS = 8
KV_GROUPS = 2
GROUP = 4
CMP_BLOCK = 32
CMP_STRIDE = 16
CMP_HIDDEN = 256
SEL_BLOCK = 64
N_SELECT = 16
C_WINDOW = 512
MIX_WIDTH = 512
D_FF = 4096
NUM_BUCKETS = 32
MAX_DISTANCE = 128
NEG_INF = -1e30
FORCE_BONUS = 1e4
EPS = 1e-6
LOG2E = 1.4426950408889634
QK_SCALE = HEAD_DIM ** -0.5 * LOG2E

C_AQ, C_AK, C_AV, C_BQ, C_BKV, C_CQ, C_CKV, C_CG = 0, 512, 1024, 1536, 2048, 2304, 2816, 3584
N_CG = C_HEADS * 3
LANES = 128
CG_PAD = LANES
NC_PAD = 256
N_SEL_BLK = 64
ONES_ROWS = 16
MASK_ROWS = 16

SCORE_SLOTS = 8

T_IN = 512
T_ROW = 256
T_SWA = 256
TV_SWA = 128

VMEM_LIMIT = 48 * 1024 * 1024


def _rms(x, gain):
    ms = jnp.mean(x * x, axis=-1, keepdims=True)
    return (x * lax.rsqrt(ms + EPS)) * gain


def _params(n_axes):
    return pltpu.CompilerParams(dimension_semantics=("arbitrary",) * n_axes,
                                vmem_limit_bytes=VMEM_LIMIT)


def _inproj_kernel(x_ref, nrm_ref, w_ref, wcg_ref, gains_ref,
                   qa_ref, ka_ref, va_ref, qb_ref, kb_ref, vb_ref, qc_ref,
                   tkv_ref, ksel_ref, vsel_ref, kwin_ref, vwin_ref, gc_ref):
    tm = x_ref.shape[1]
    hb = _rms(x_ref[0], nrm_ref[...]).astype(BF16)
    lane = lax.broadcasted_iota(jnp.int32, (tm, LANES), 1)
    first = lane < HEAD_DIM

    def ones_pad(width):
        return jnp.where(lax.broadcasted_iota(jnp.int32, (ONES_ROWS, width), 0) == 0, 1.0, 0.0).astype(BF16)

    def mm(c0, n):
        return jnp.dot(hb, w_ref[:, c0:c0 + n], preferred_element_type=F32)

    def norm_pair(y, gain_idx, scale=None):
        sq = y * y
        ms0 = jnp.sum(jnp.where(first, sq, 0.0), axis=-1, keepdims=True) * (1.0 / HEAD_DIM)
        ms1 = jnp.sum(jnp.where(first, 0.0, sq), axis=-1, keepdims=True) * (1.0 / HEAD_DIM)
        inv = jnp.where(first, lax.rsqrt(ms0 + EPS), lax.rsqrt(ms1 + EPS))
        out = (y * inv) * gains_ref[gain_idx:gain_idx + 1, :]
        return out if scale is None else out * scale

    def pairs(c0):
        r = mm(c0, 256)
        return r[:, :LANES], r[:, LANES:]

    def put_qt(ref, i0, i1, y):
        yt = y.T.astype(BF16)
        n_tiles = ref.shape[3]
        tq = tm // n_tiles
        for t in range(n_tiles):
            ref[0, i0, i1, t] = yt[:HEAD_DIM, tq * t:tq * (t + 1)]
            ref[0, i0, i1 + 1, t] = yt[HEAD_DIM:, tq * t:tq * (t + 1)]

    def put_vt(ref, g0, y, n_tiles):
        yt = y.T.astype(BF16)
        tk = tm // n_tiles
        for g in range(2):
            for t in range(n_tiles):
                ref[0, g0 + g, t, 0:HEAD_DIM, :] = yt[HEAD_DIM * g:HEAD_DIM * (g + 1), tk * t:tk * (t + 1)]
                ref[0, g0 + g, t, HEAD_DIM:HEAD_DIM + ONES_ROWS, :] = ones_pad(tk)

    for ch in range(2):
        for half, y in enumerate(pairs(C_AQ + 256 * ch)):
            put_qt(qa_ref, ch * 2 + half, 0, norm_pair(y, 0, QK_SCALE))
        for half, y in enumerate(pairs(C_AK + 256 * ch)):
            y = norm_pair(y, 1).astype(BF16)
            ka_ref[0, ch * 2 + half, 0] = y[:, :HEAD_DIM]
            ka_ref[0, ch * 2 + half, 1] = y[:, HEAD_DIM:]
        for half, y in enumerate(pairs(C_AV + 256 * ch)):
            yt = y.T.astype(BF16)
            for t in range(tm // T_ROW):
                va_ref[0, ch * 2 + half, t, 0:A_VDIM, :] = yt[:, T_ROW * t:T_ROW * (t + 1)]
                va_ref[0, ch * 2 + half, t, A_VDIM:A_VDIM + ONES_ROWS, :] = ones_pad(T_ROW)
        for half, y in enumerate(pairs(C_BQ + 256 * ch)):
            put_qt(qb_ref, ch, 2 * half, norm_pair(y, 2, QK_SCALE))
        for half, y in enumerate(pairs(C_CQ + 256 * ch)):
            put_qt(qc_ref, ch, 2 * half, norm_pair(y, 4, QK_SCALE))

    def put_k(ref, y):
        y = y.astype(BF16)
        ref[0, 0] = y[:, :HEAD_DIM]
        ref[0, 1] = y[:, HEAD_DIM:]

    yk, yv = pairs(C_BKV)
    put_k(kb_ref, norm_pair(yk, 3))
    put_vt(vb_ref, 0, yv, tm // TV_SWA)
    y0, y1 = pairs(C_CKV)
    for kv, y in enumerate((y0, y1)):
        tkv_ref[0, kv, 0] = y[:, :HEAD_DIM]
        tkv_ref[0, kv, 1] = y[:, HEAD_DIM:]
    yk, yv = pairs(C_CKV + 256)
    yk = norm_pair(yk, 6)
    row = lax.broadcasted_iota(jnp.int32, (tm, LANES), 0)
    hot = jnp.where(lane == HEAD_DIM + (row // SEL_BLOCK) % (T_ROW // SEL_BLOCK), 1.0, 0.0)
    ksel_ref[0, 0] = jnp.where(first, yk, hot).astype(BF16)
    ksel_ref[0, 1] = jnp.where(first, pltpu.roll(yk, HEAD_DIM, 1), hot).astype(BF16)
    put_vt(vsel_ref, 0, yv, tm // T_ROW)
    yk, yv = pairs(C_CKV + 512)
    put_k(kwin_ref, norm_pair(yk, 7))
    put_vt(vwin_ref, 0, yv, tm // T_ROW)

    gc_ref[0] = jax.nn.sigmoid(jnp.dot(hb, wcg_ref[...], preferred_element_type=F32))


def _inproj(x, nrm, w, wcg, gains2, layer):
    B, S, _ = x.shape
    tm = T_IN
    nt = S // tm
    sd = jax.ShapeDtypeStruct
    qt = lambda a, b, tq: sd((B, a, b, S // tq, HEAD_DIM, tq), BF16)
    kk = lambda n: sd((B, n, S, HEAD_DIM), BF16)
    vt = lambda n, dv, tk: sd((B, n, S // tk, dv + ONES_ROWS, tk), BF16)
    out_shape = (
        qt(A_HEADS, 2, T_ROW), sd((B, A_HEADS, 2, S, HEAD_DIM), BF16), vt(A_HEADS, A_VDIM, T_ROW),
        qt(KV_GROUPS, GROUP, T_SWA), kk(KV_GROUPS), vt(KV_GROUPS, HEAD_DIM, TV_SWA),
        qt(KV_GROUPS, GROUP, T_ROW),
        sd((B, 2, KV_GROUPS, S, HEAD_DIM), F32),
        sd((B, KV_GROUPS, S, LANES), BF16), vt(KV_GROUPS, HEAD_DIM, T_ROW),
        kk(KV_GROUPS), vt(KV_GROUPS, HEAD_DIM, T_ROW),
        sd((B, S, CG_PAD), F32),
    )
    s_qt = lambda a, b, tq: pl.BlockSpec((1, a, b, tm // tq, HEAD_DIM, tq), lambda b_, i: (b_, 0, 0, i, 0, 0))
    s_k = lambda n: pl.BlockSpec((1, n, tm, HEAD_DIM), lambda b_, i: (b_, 0, i, 0))
    s_k5 = lambda a, c: pl.BlockSpec((1, a, c, tm, HEAD_DIM), lambda b_, i: (b_, 0, 0, i, 0))
    s_vt = lambda n, dv, tk: pl.BlockSpec((1, n, tm // tk, dv + ONES_ROWS, tk), lambda b_, i: (b_, 0, i, 0, 0))
    row = lambda d: pl.BlockSpec((1, tm, d), lambda b_, i: (b_, i, 0))
    out_specs = (
        s_qt(A_HEADS, 2, T_ROW), s_k5(A_HEADS, 2), s_vt(A_HEADS, A_VDIM, T_ROW),
        s_qt(KV_GROUPS, GROUP, T_SWA), s_k(KV_GROUPS), s_vt(KV_GROUPS, HEAD_DIM, TV_SWA),
        s_qt(KV_GROUPS, GROUP, T_ROW), s_k5(2, KV_GROUPS),
        pl.BlockSpec((1, KV_GROUPS, tm, LANES), lambda b_, i: (b_, 0, i, 0)), s_vt(KV_GROUPS, HEAD_DIM, T_ROW),
        s_k(KV_GROUPS), s_vt(KV_GROUPS, HEAD_DIM, T_ROW),
        row(CG_PAD),
    )
    return pl.pallas_call(
        _inproj_kernel,
        grid=(B, nt),
        in_specs=[
            row(D_MODEL),
            pl.BlockSpec((1, D_MODEL), lambda b_, i: (0, 0)),
            pl.BlockSpec((None, D_MODEL, C_CG), lambda b_, i: (layer, 0, 0), pipeline_mode=pl.Buffered(1)),
            pl.BlockSpec((None, D_MODEL, CG_PAD), lambda b_, i: (layer, 0, 0)),
            pl.BlockSpec((8, LANES), lambda b_, i: (0, 0)),
        ],
        out_specs=out_specs,
        out_shape=out_shape,
        compiler_params=_params(2),
        name="inproj",
    )(x, nrm, w, wcg, gains2)


def _compress_kernel(t_ref, pos_ref, w1_ref, w2_ref, gain_ref, o_ref, ot_ref):
    t = jnp.concatenate([t_ref[0, 0, 0, pl.ds(b, NC_PAD, stride=CMP_STRIDE), :] for b in range(CMP_STRIDE)], axis=1)
    lo = (t + pos_ref[0, 0:1, :]).astype(BF16)
    hi = (t + pos_ref[0, 1:2, :]).astype(BF16)
    v = jnp.dot(lo, w1_ref[0, 0], preferred_element_type=F32)
    u = jnp.dot(hi, w1_ref[0, 1], preferred_element_type=F32)
    pre = v + pltpu.roll(u, NC_PAD - 1, 0)
    hcur = jax.nn.gelu(pre).astype(BF16)
    out = jnp.dot(hcur, w2_ref[0], preferred_element_type=F32)
    o64 = out[:, :HEAD_DIM]
    is_key = pl.program_id(1) == 0
    o_ref[0, 0, 0] = jnp.where(is_key, _rms(o64, gain_ref[...]), o64).astype(BF16)
    ot_ref[0, 0, 0] = out.T[:HEAD_DIM].astype(BF16)


def _compress(tkv, pos, w1, w2p, gain):
    B, S = tkv.shape[0], tkv.shape[3]
    half = CMP_BLOCK // 2 * HEAD_DIM
    return pl.pallas_call(
        _compress_kernel,
        grid=(B, 2, KV_GROUPS),
        in_specs=[
            pl.BlockSpec((1, 1, 1, S, HEAD_DIM), lambda b, kv, g: (b, kv, g, 0, 0)),
            pl.BlockSpec((1, 2, half), lambda b, kv, g: (kv, 0, 0)),
            pl.BlockSpec((1, 2, half, CMP_HIDDEN), lambda b, kv, g: (kv, 0, 0, 0)),
            pl.BlockSpec((1, CMP_HIDDEN, LANES), lambda b, kv, g: (kv, 0, 0)),
            pl.BlockSpec((1, HEAD_DIM), lambda b, kv, g: (0, 0)),
        ],
        out_specs=(pl.BlockSpec((1, 1, 1, NC_PAD, HEAD_DIM), lambda b, kv, g: (b, kv, g, 0, 0)),
                   pl.BlockSpec((1, 1, 1, HEAD_DIM, NC_PAD), lambda b, kv, g: (b, kv, g, 0, 0))),
        out_shape=(jax.ShapeDtypeStruct((B, 2, KV_GROUPS, NC_PAD, HEAD_DIM), BF16),
                   jax.ShapeDtypeStruct((B, 2, KV_GROUPS, HEAD_DIM, NC_PAD), BF16)),
        compiler_params=_params(3),
        name="compress",
    )(tkv, pos, w1, w2p, gain)


def _cmp_attn_kernel(qt_ref, kc_ref, vct_ref, ovl_ref, o_ref, sel_ref, score_ref, rank_ref, *, T, n_cmp, n_sub):
    R = GROUP
    tiles = [pl.program_id(2) * n_sub + t for t in range(n_sub)]
    j_idx = lax.broadcasted_iota(jnp.int32, (N_SEL_BLK, T), 0)
    lane_q = lax.broadcasted_iota(jnp.int32, (N_SEL_BLK, T), 1)

    def attend(rows):
        kc = kc_ref[0, 0, 0, 0:rows, :]
        vct = vct_ref[0, 0, 0, :, 0:rows]
        ovl = ovl_ref[:, 0:rows]
        c_idx = lax.broadcasted_iota(jnp.int32, (rows, T), 0)
        lane_k = lax.broadcasted_iota(jnp.int32, (rows, T), 1)
        for t, I in enumerate(tiles):
            cmask = (c_idx * CMP_STRIDE + (CMP_BLOCK - 1) <= I * T + lane_k) & (c_idx < n_cmp)
            scores = [jnp.dot(kc, qt_ref[0, 0, r, t], preferred_element_type=F32) for r in range(R)]
            probs = []
            for r in range(R):
                s = jnp.where(cmask, scores[r], NEG_INF)
                m = jnp.max(s, axis=0, keepdims=True)
                e = jnp.where(cmask, jnp.exp2(s - m), 0.0)
                l = jnp.sum(e, axis=0, keepdims=True)
                probs.append(e * (1.0 / jnp.where(l > 0.0, l, 1.0)))
            outs = [jnp.dot(vct, p.astype(BF16), preferred_element_type=F32) for p in probs]
            o_ref[0, t * T:(t + 1) * T, :] = jnp.concatenate(outs, axis=0).T
            psum = (probs[0] + probs[1]) + (probs[2] + probs[3])
            p_hi = psum.astype(BF16)
            p_lo = (psum - p_hi.astype(F32)).astype(BF16)
            imp = jnp.dot(ovl, p_hi, preferred_element_type=F32) + jnp.dot(ovl, p_lo, preferred_element_type=F32)
            tq = I * T + lane_q
            cur = tq // SEL_BLOCK
            valid = j_idx * SEL_BLOCK <= tq
            forced = (j_idx == 0) | (j_idx == cur) | (j_idx == cur - 1)
            score_ref[t] = jnp.where(valid, imp + jnp.where(forced, FORCE_BONUS, 0.0), NEG_INF)

    chunk = NC_PAD // 4
    vis_per_tile = T // CMP_STRIDE
    for v in range(4):
        lo = -(-(v * chunk) // vis_per_tile)
        hi = -(-((v + 1) * chunk) // vis_per_tile)
        assert lo % n_sub == 0 and hi % n_sub == 0

        @pl.when((tiles[0] >= lo) & (tiles[0] < hi))
        def _(v=v):
            attend((v + 1) * chunk)

    rank_ref[...] = jnp.zeros(rank_ref.shape, F32)
    n_live = (tiles[-1] + 1) * (T // SEL_BLOCK)
    sub = 8
    n_grp = N_SEL_BLK // sub
    j_loc = lax.broadcasted_iota(jnp.int32, (sub, T), 0)
    for c in range(n_grp):
        @pl.when(c * sub < n_live)
        def _(c=c):
            for t in range(n_sub):
                grps = [score_ref[t, g * sub:(g + 1) * sub, :] for g in range(n_grp)]
                cnts = [rank_ref[t, g * sub:(g + 1) * sub, :] for g in range(n_grp)]
                for jp in range(c * sub, (c + 1) * sub):
                    row = jnp.broadcast_to(score_ref[t, jp:jp + 1, :], (sub, T))
                    for g in range(n_grp):
                        if g * sub > jp:
                            beats = row >= grps[g]
                        elif (g + 1) * sub - 1 < jp:
                            beats = row > grps[g]
                        else:
                            beats = (row > grps[g]) | ((row == grps[g]) & (j_loc + g * sub > jp))
                        cnts[g] = cnts[g] + jnp.where(beats, 1.0, 0.0)
                for g in range(n_grp):
                    rank_ref[t, g * sub:(g + 1) * sub, :] = cnts[g]
    blk = T // SEL_BLOCK
    for t in range(n_sub):
        for j in range(N_SEL_BLK // blk):
            keep = rank_ref[t, j * blk:(j + 1) * blk, :] < N_SELECT
            slab = jnp.concatenate([jnp.where(keep, 0.0, NEG_INF), jnp.zeros((MASK_ROWS - blk, T), F32)], axis=0)
            sel_ref[0, 0, t, j] = slab.astype(BF16)


def _cmp_attn(qct, kc, vct, ovl, n_cmp, n_sub=2):
    B, G, R, NQ, _, T = qct.shape
    S = NQ * T
    return pl.pallas_call(
        functools.partial(_cmp_attn_kernel, T=T, n_cmp=n_cmp, n_sub=n_sub),
        grid=(B, G, NQ // n_sub),
        in_specs=[
            pl.BlockSpec((1, 1, R, n_sub, HEAD_DIM, T), lambda b, g, i: (b, g, 0, i, 0, 0)),
            pl.BlockSpec((1, 1, 1, NC_PAD, HEAD_DIM), lambda b, g, i: (b, 0, g, 0, 0)),
            pl.BlockSpec((1, 1, 1, HEAD_DIM, NC_PAD), lambda b, g, i: (b, 1, g, 0, 0)),
            pl.BlockSpec((N_SEL_BLK, NC_PAD), lambda b, g, i: (0, 0)),
        ],
        out_specs=(
            pl.BlockSpec((1, n_sub * T, R * HEAD_DIM), lambda b, g, i: (b, i, g)),
            pl.BlockSpec((1, 1, n_sub, NQ, MASK_ROWS, T), lambda b, g, i: (b, g, i, 0, 0, 0)),
        ),
        out_shape=(
            jax.ShapeDtypeStruct((B, S, G * R * HEAD_DIM), F32),
            jax.ShapeDtypeStruct((B, G, NQ, NQ, MASK_ROWS, T), BF16),
        ),
        scratch_shapes=[pltpu.VMEM((n_sub, N_SEL_BLK, T), F32), pltpu.VMEM((n_sub, N_SEL_BLK, T), F32)],
        compiler_params=_params(3),
        name="cmp_attn",
    )(qct, kc, vct, ovl)


def _flash_kernel(*refs, R, T, NQ, Dv, nrel, k_per_r, shared_bias, diff, has_sel, early_bias, lam_init):
    it = iter(refs)
    qt_ref, k_ref, vt_ref, bt_ref = (next(it) for _ in range(4))
    sel_ref = next(it) if has_sel else None
    dl_ref = next(it) if diff else None
    sub_ref = next(it) if diff else None
    o_ref, m_ref, acc_ref, s_ref, p_ref, al_ref, tmax_ref = (next(it) for _ in range(7))

    m_ref[...] = jnp.full(m_ref.shape, NEG_INF, F32)
    acc_ref[...] = jnp.zeros(acc_ref.shape, F32)
    bias_r = (lambda r: 0) if shared_bias else (lambda r: r)
    if has_sel:
        pad_rows = jnp.zeros((k_ref.shape[-1] - HEAD_DIM - MASK_ROWS, T), BF16)
    n_steps = NQ * (NQ + 1) // 2
    N_S = SCORE_SLOTS
    assert N_S % 2 == 0 and n_steps % N_S == 0

    def score_stage(I, J, slot, table):
        off = pl.multiple_of(J * T, T)
        rel = jnp.minimum(I - J, nrel - 1)
        for r in range(R):
            kt = k_ref[0, 0, r, pl.ds(off, T), :] if k_per_r else k_ref[0, 0, pl.ds(off, T), :]
            w = qt_ref[0, 0, r, I]
            if has_sel:
                w = jnp.concatenate([w, sel_ref[0, 0, I, J], pad_rows], axis=0)
            s = jnp.dot(kt, w, preferred_element_type=F32)
            if early_bias:
                if table:
                    s = s + bt_ref[0, rel, bias_r(r)]
                tmax_ref[slot, r] = jnp.max(s, axis=0, keepdims=True)
            s_ref[slot, r] = s

    def softmax_stage(I, J, s_slot, slot, table):
        for r in range(R):
            m_prev = m_ref[I, r]
            s = s_ref[s_slot, r]
            if early_bias:
                tile_max = tmax_ref[s_slot, r]
            else:
                if table:
                    s = s + bt_ref[0, jnp.minimum(I - J, nrel - 1), bias_r(r)]
                tile_max = jnp.max(s, axis=0, keepdims=True)
            m_new = jnp.maximum(m_prev, tile_max)
            al_ref[slot, r] = jnp.exp2(m_prev - m_new)
            p_ref[slot, r] = jnp.exp2(s - m_new).astype(BF16)
            m_ref[I, r] = m_new

    def value_stage(J, slot):
        vt = vt_ref[0, 0, J]
        return [jnp.dot(vt, p_ref[slot, r], preferred_element_type=F32) for r in range(R)]

    def accumulate(I, slot, pvs):
        for r in range(R):
            acc_ref[I, r] = al_ref[slot, r] * acc_ref[I, r] + pvs[r]

    def far_advance(I, J):
        wrap = J + 1 > I - (nrel - 1)
        return jnp.where(wrap, I + 1, I), jnp.where(wrap, 0, J + 1)

    def near_advance(I, J):
        on_diag, before_diag = J == I, J == I - 1
        nxt_i = jnp.where(on_diag, I + 1, jnp.where(before_diag, I, 0))
        nxt_j = jnp.where(on_diag, I, jnp.where(before_diag, I, 0))
        return nxt_i, nxt_j

    assert nrel == 3
    n_far = (NQ - 1) * (NQ - 2) // 2
    n_near = 2 * NQ - 1
    far_trips, far_left = divmod(n_far, N_S)
    assert far_left == 1 and (far_left + n_near) % N_S == 0
    near_trips = (far_left + n_near) // N_S

    def step(cur, prev, t, advance, table):
        other = 1 - t % 2
        nxt = advance(*cur)
        pvs = value_stage(prev[1], other)
        score_stage(jnp.minimum(nxt[0], NQ - 1), jnp.minimum(nxt[1], NQ - 1), (t + 1) % N_S, table)
        softmax_stage(cur[0], cur[1], t % N_S, t % 2, table)
        accumulate(prev[0], other, pvs)
        return nxt

    def make_trip(advance, table):
        def trip(u, carry):
            cur, prev = carry[:2], carry[2:]
            for t in range(N_S):
                cur, prev = step(cur, prev, t, advance, table), cur
            return (*cur, *prev)
        return trip

    first = (jnp.int32(nrel - 1), jnp.int32(0))
    score_stage(*first, 0, False)
    p_ref[1] = jnp.zeros(p_ref.shape[1:], BF16)
    al_ref[1] = jnp.ones(al_ref.shape[1:], F32)
    carry = lax.fori_loop(0, far_trips, make_trip(far_advance, False), (*first, *first))
    lax.fori_loop(0, near_trips, make_trip(near_advance, True), carry)
    accumulate(NQ - 1, 1, value_stage(NQ - 1, 1))

    if diff:
        dl = dl_ref[...]
        lam = (jnp.exp(jnp.sum(dl[0:1] * dl[1:2], keepdims=True))
               - jnp.exp(jnp.sum(dl[2:3] * dl[3:4], keepdims=True)) + lam_init)

    def finish(I, carry):
        outs = []
        for r in range(R):
            acc = acc_ref[I, r]
            outs.append(acc[:Dv] * (1.0 / acc[Dv:Dv + 1]))
        rows = pl.ds(pl.multiple_of(I * T, T), T)
        if diff:
            o = (outs[0] - lam * outs[1]).T
            o_ref[0, rows, :] = (_rms(o, sub_ref[...]) * (1.0 - lam_init)).astype(o_ref.dtype)
        else:
            o_ref[0, rows, :] = jnp.concatenate(outs, axis=0).T
        return carry

    lax.fori_loop(0, NQ, finish, 0, unroll=4)


def _flash_attn(qt, k, vt, bias_tab, *, sel=None, dl=None, subln=None, lam_init=0.0, name="attn"):
    B, G, R, NQ, _, T = qt.shape
    S = NQ * T
    Dv = vt.shape[-2] - ONES_ROWS
    nrel, bias_heads = bias_tab.shape[1], bias_tab.shape[2]
    k_per_r = k.ndim == 5
    has_sel = sel is not None
    diff = dl is not None
    ins = [qt, k, vt, bias_tab]
    in_specs = [
        pl.BlockSpec((1, 1, R, NQ, HEAD_DIM, T), lambda b, g: (b, g, 0, 0, 0, 0)),
        (pl.BlockSpec((1, 1, R, S, HEAD_DIM), lambda b, g: (b, g, 0, 0, 0)) if k_per_r
         else pl.BlockSpec((1, 1, S, k.shape[-1]), lambda b, g: (b, g, 0, 0))),
        pl.BlockSpec((1, 1, NQ, Dv + ONES_ROWS, T), lambda b, g: (b, g, 0, 0, 0)),
        pl.BlockSpec((1, nrel, bias_heads, T, T), lambda b, g: (g, 0, 0, 0, 0)),
    ]
    if has_sel:
        ins.append(sel)
        in_specs.append(pl.BlockSpec((1, 1, NQ, NQ, MASK_ROWS, T), lambda b, g: (b, g, 0, 0, 0, 0)))
    if diff:
        ins += [dl, subln]
        in_specs += [pl.BlockSpec((4, HEAD_DIM), lambda b, g: (0, 0)),
                     pl.BlockSpec((1, A_VDIM), lambda b, g: (0, 0))]
    out_w = Dv if diff else R * Dv
    kern = functools.partial(_flash_kernel, R=R, T=T, NQ=NQ, Dv=Dv, nrel=nrel, k_per_r=k_per_r,
                             shared_bias=bias_heads == 1, diff=diff, has_sel=has_sel, early_bias=not diff,
                             lam_init=lam_init)
    return pl.pallas_call(
        kern,
        grid=(B, G),
        in_specs=in_specs,
        out_specs=pl.BlockSpec((1, S, out_w), lambda b, g: (b, 0, g)),
        out_shape=jax.ShapeDtypeStruct((B, S, G * out_w), BF16 if diff else F32),
        scratch_shapes=[pltpu.VMEM((NQ, R, 1, T), F32), pltpu.VMEM((NQ, R, Dv + ONES_ROWS, T), F32),
                        pltpu.VMEM((SCORE_SLOTS, R, T, T), F32), pltpu.VMEM((2, R, T, T), BF16),
                        pltpu.VMEM((2, R, 1, T), F32), pltpu.VMEM((SCORE_SLOTS, R, 1, T), F32)],
        compiler_params=_params(2),
        name=name,
    )(*ins)


def _window_kernel(*refs, R, T, NK, TV, Dv, n_sub, has_sink):
    it = iter(refs)
    qt_ref, k_ref, vt_ref, bt_ref = (next(it) for _ in range(4))
    sink_ref = next(it) if has_sink else None
    o_ref = next(it)
    heads = [(t, r) for t in range(n_sub) for r in range(R)]
    tile = [pl.program_id(2) * n_sub + t for t in range(n_sub)]
    entry = [jnp.minimum(I, -(-(NK - T) // T)) for I in tile]
    first_key = [pl.multiple_of(I * T - jnp.minimum(e * T, NK - T), TV) for I, e in zip(tile, entry)]
    keys = [k_ref[0, 0, pl.ds(fk, NK), :] for fk in first_key]
    scores = {(t, r): jnp.dot(keys[t], qt_ref[0, 0, r, t], preferred_element_type=F32) for t, r in heads}
    probs, maxes = {}, {}
    for t, r in heads:
        s = scores[t, r] + bt_ref[0, entry[t], r]
        maxes[t, r] = jnp.max(s, axis=0, keepdims=True)
        probs[t, r] = jnp.exp2(s - maxes[t, r]).astype(BF16)
    accs = {}
    for t, r in heads:
        acc = None
        for c in range(NK // TV):
            part = jnp.dot(vt_ref[0, 0, first_key[t] // TV + c], probs[t, r][c * TV:(c + 1) * TV],
                           preferred_element_type=F32)
            acc = part if acc is None else acc + part
        accs[t, r] = acc
    for t in range(n_sub):
        outs = []
        for r in range(R):
            num, l = accs[t, r][:Dv], accs[t, r][Dv:Dv + 1]
            if has_sink:
                sk = sink_ref[0, r] * LOG2E
                m_f = jnp.maximum(maxes[t, r], sk)
                w = jnp.exp2(maxes[t, r] - m_f)
                outs.append(num * (w / (l * w + jnp.exp2(sk - m_f))))
            else:
                outs.append(num * (1.0 / l))
        o_ref[0, t * T:(t + 1) * T, :] = jnp.concatenate(outs, axis=0).T.astype(o_ref.dtype)


def _window_attn(qt, k, vt, bias_tab, *, n_sub=2, sink=None, out_dtype=F32, name="attn_window"):
    B, G, R, NQ, _, T = qt.shape
    S = NQ * T
    TV = vt.shape[-1]
    Dv = vt.shape[-2] - ONES_ROWS
    entries, NK = bias_tab.shape[1], bias_tab.shape[3]
    has_sink = sink is not None
    ins = [qt, k, vt, bias_tab]
    in_specs = [
        pl.BlockSpec((1, 1, R, n_sub, HEAD_DIM, T), lambda b, g, i: (b, g, 0, i, 0, 0)),
        pl.BlockSpec((1, 1, S, HEAD_DIM), lambda b, g, i: (b, g, 0, 0)),
        pl.BlockSpec((1, 1, S // TV, Dv + ONES_ROWS, TV), lambda b, g, i: (b, g, 0, 0, 0)),
        pl.BlockSpec((1, entries, R, NK, T), lambda b, g, i: (g, 0, 0, 0, 0)),
    ]
    if has_sink:
        ins.append(sink)
        in_specs.append(pl.BlockSpec((1, R, 1, 1), lambda b, g, i: (g, 0, 0, 0)))
    kern = functools.partial(_window_kernel, R=R, T=T, NK=NK, TV=TV, Dv=Dv, n_sub=n_sub, has_sink=has_sink)
    return pl.pallas_call(
        kern,
        grid=(B, G, NQ // n_sub),
        in_specs=in_specs,
        out_specs=pl.BlockSpec((1, n_sub * T, R * Dv), lambda b, g, i: (b, i, g)),
        out_shape=jax.ShapeDtypeStruct((B, S, G * R * Dv), out_dtype),
        compiler_params=_params(3),
        name=name,
    )(*ins)


def _bucket_np(dist):
    n = np.maximum(dist, 0)
    max_exact = NUM_BUCKETS // 2
    nf = np.maximum(n, 1).astype(np.float32)
    large = max_exact + (np.log(nf / max_exact) / math.log(MAX_DISTANCE / max_exact)
                         * (NUM_BUCKETS - max_exact)).astype(np.int32)
    large = np.minimum(large, NUM_BUCKETS - 1)
    return np.where(n < max_exact, n, large)


def _bucket_starts():
    b = _bucket_np(np.arange(4 * MAX_DISTANCE))
    assert (np.diff(b) >= 0).all() and b[-1] == NUM_BUCKETS - 1
    return [int(np.argmax(b >= i)) for i in range(NUM_BUCKETS)]


def _bias_kernel(tab_ref, o_ref, *, T, rows, nrel, window, back_max, center, starts):
    h = pl.program_id(0)
    base = tab_ref[NUM_BUCKETS - 1, h] * LOG2E if center else 0.0
    key = lax.broadcasted_iota(jnp.int32, (rows, T), 0)
    qry = lax.broadcasted_iota(jnp.int32, (rows, T), 1)
    for rel in range(nrel):
        dist = min(rel * T, back_max) + qry - key
        val = jnp.full((rows, T), tab_ref[0, h] * LOG2E - base, F32)
        for b in range(1, NUM_BUCKETS):
            val = jnp.where(dist >= starts[b], tab_ref[b, h] * LOG2E - base, val)
        allowed = dist >= 0
        if window is not None:
            allowed = allowed & (dist < window)
        o_ref[0, rel, 0] = jnp.where(allowed, val, NEG_INF)


def _bias_tiles(tab, T, nrel, window, R, rows=None, center=False):
    H = tab.shape[1]
    back_max = nrel * T if rows is None else rows - T
    rows = T if rows is None else rows
    return pl.pallas_call(
        functools.partial(_bias_kernel, T=T, rows=rows, nrel=nrel, window=window, back_max=back_max,
                          center=center, starts=_bucket_starts()),
        grid=(H,),
        in_specs=[pl.BlockSpec(memory_space=pltpu.SMEM)],
        out_specs=pl.BlockSpec((1, nrel, 1, rows, T), lambda h: (h // R, 0, h % R, 0, 0)),
        out_shape=jax.ShapeDtypeStruct((H // R, nrel, R, rows, T), F32),
        compiler_params=_params(1),
        name="bias_tiles",
    )(tab)


def _merge_kernel(x_ref, nrm_ref, oa_ref, ob_ref, oc_ref, os_ref, ow_ref, gc_ref, ex_ref, wmg_ref, wb_ref, wo_ref,
                  o_ref):
    x = x_ref[...]
    hb = _rms(x, nrm_ref[...]).astype(BF16)
    gc = gc_ref[...]
    g_hi = gc.astype(BF16)
    g_lo = (gc - g_hi.astype(F32)).astype(BF16)
    gx = jnp.dot(jnp.concatenate([g_hi, g_lo], axis=1), ex_ref[...], preferred_element_type=F32)
    oc = (gx[:, 0:MIX_WIDTH] * oc_ref[...] + gx[:, MIX_WIDTH:2 * MIX_WIDTH] * os_ref[...]
          + gx[:, 2 * MIX_WIDTH:3 * MIX_WIDTH] * ow_ref[...])
    z = None
    for n, br in enumerate((oa_ref[...], ob_ref[...], oc)):
        y = jnp.dot(br.astype(BF16), wb_ref[n], preferred_element_type=F32)
        gate = jax.nn.sigmoid(jnp.dot(hb, wmg_ref[:, n * D_MODEL:(n + 1) * D_MODEL], preferred_element_type=F32))
        t = gate * y
        z = t if z is None else z + t
    o_ref[...] = x + jnp.dot(z.astype(BF16), wo_ref[...], preferred_element_type=F32)


def _merge(x2, nrm, oa, ob, oc, osel, ow, gc, expand_g, wmg, wb, wo, layer, tm=512):
    Tn = x2.shape[0]
    row = lambda d: pl.BlockSpec((tm, d), lambda i: (i, 0))
    const = lambda shape: pl.BlockSpec(shape, lambda i: (0,) * len(shape), pipeline_mode=pl.Buffered(1))
    weight = lambda shape: pl.BlockSpec((None,) + shape, lambda i: (layer,) + (0,) * len(shape),
                                        pipeline_mode=pl.Buffered(1))
    return pl.pallas_call(
        _merge_kernel,
        grid=(Tn // tm,),
        in_specs=[row(D_MODEL), const((1, D_MODEL)),
                  row(MIX_WIDTH), row(MIX_WIDTH), row(MIX_WIDTH), row(MIX_WIDTH), row(MIX_WIDTH),
                  row(CG_PAD),
                  const((2 * CG_PAD, 3 * MIX_WIDTH)),
                  weight((D_MODEL, 3 * D_MODEL)),
                  weight((3, MIX_WIDTH, D_MODEL)),
                  weight((D_MODEL, D_MODEL))],
        out_specs=row(D_MODEL),
        out_shape=jax.ShapeDtypeStruct((Tn, D_MODEL), F32),
        compiler_params=_params(1),
        name="merge",
    )(x2, nrm, oa, ob, oc, osel, ow, gc, expand_g, wmg, wb, wo)


def _mlp_kernel(x_ref, nrm_ref, wu_ref, wd_ref, o_ref, h_ref, acc_ref):
    f = pl.program_id(1)

    @pl.when(f == 0)
    def _():
        h_ref[...] = _rms(x_ref[...], nrm_ref[...]).astype(BF16)
        acc_ref[...] = jnp.zeros(acc_ref.shape, F32)

    u = jnp.dot(h_ref[...], wu_ref[...], preferred_element_type=F32)
    u = jnp.square(jnp.maximum(u, 0.0)).astype(BF16)
    acc_ref[...] += jnp.dot(u, wd_ref[...], preferred_element_type=F32)

    @pl.when(f == pl.num_programs(1) - 1)
    def _():
        o_ref[...] = x_ref[...] + acc_ref[...]


def _mlp(x2, nrm, wu, wd, layer, tm=1024, tf=1024):
    Tn = x2.shape[0]
    return pl.pallas_call(
        _mlp_kernel,
        grid=(Tn // tm, D_FF // tf),
        in_specs=[pl.BlockSpec((tm, D_MODEL), lambda i, f: (i, 0)),
                  pl.BlockSpec((1, D_MODEL), lambda i, f: (0, 0)),
                  pl.BlockSpec((None, D_MODEL, tf), lambda i, f: (layer, 0, f)),
                  pl.BlockSpec((None, tf, D_MODEL), lambda i, f: (layer, f, 0))],
        out_specs=pl.BlockSpec((tm, D_MODEL), lambda i, f: (i, 0)),
        out_shape=jax.ShapeDtypeStruct((Tn, D_MODEL), F32),
        scratch_shapes=[pltpu.VMEM((tm, D_MODEL), BF16), pltpu.VMEM((tm, D_MODEL), F32)],
        compiler_params=_params(2),
        name="mlp",
    )(x2, nrm, wu, wd)


def _overlap(n_cmp):
    c_start = np.arange(NC_PAD) * CMP_STRIDE
    j_start = np.arange(N_SEL_BLK) * SEL_BLOCK
    ov = ((c_start[None, :] < j_start[:, None] + SEL_BLOCK) & (c_start[None, :] + CMP_BLOCK > j_start[:, None])
          & (np.arange(NC_PAD)[None, :] < n_cmp))
    return jnp.asarray(ov.astype(np.float32), BF16)


def _gate_expand():
    e = np.zeros((2, CG_PAD, 3 * MIX_WIDTH), np.float32)
    for h in range(C_HEADS):
        for j in range(3):
            e[:, h * 3 + j, j * MIX_WIDTH + h * HEAD_DIM:j * MIX_WIDTH + (h + 1) * HEAD_DIM] = 1.0
    return jnp.asarray(e.reshape(2 * CG_PAD, 3 * MIX_WIDTH), BF16)


def kernel(x, w_in, qk_gain, diff_lambda, diff_subln, sinks, cmp_pos, cmp_w1, cmp_w2,
           w_branch, w_out, norm_mix, norm_mlp, w_up, w_down, rel_bias):
    B, S, _ = x.shape
    depth = w_in.shape[0]
    n_cmp = (S - CMP_BLOCK) // CMP_STRIDE + 1
    assert S % T_ROW == 0 and S // CMP_STRIDE == NC_PAD and S // SEL_BLOCK == N_SEL_BLK
    half = CMP_BLOCK // 2 * HEAD_DIM

    w_heads = w_in[:, :, :C_CG].astype(BF16)
    w_cg = jnp.pad(w_in[:, :, C_CG:C_CG + N_CG], ((0, 0), (0, 0), (0, CG_PAD - N_CG))).astype(BF16)
    w_mg = w_in[:, :, C_CG + N_CG:].astype(BF16)
    gains2 = jnp.concatenate([qk_gain, qk_gain], axis=-1)
    w1 = cmp_w1.astype(BF16).reshape(depth, 2, 2, half, CMP_HIDDEN)
    w2p = jnp.pad(cmp_w2, ((0, 0), (0, 0), (0, 0), (0, LANES - HEAD_DIM))).astype(BF16)
    pos = cmp_pos.reshape(depth, 2, 2, half)
    wb = w_branch.astype(BF16)
    wo = w_out.astype(BF16)
    wu = w_up.astype(BF16)
    wd = w_down.astype(BF16)

    bias_a = rel_bias[:, :A_HEADS]
    bias_b = rel_bias[:, A_HEADS:A_HEADS + B_HEADS]
    bias_c = rel_bias[:, A_HEADS + B_HEADS:]
    bt_a = _bias_tiles(bias_a, T_ROW, 3, None, 1, center=True)
    bt_b = _bias_tiles(bias_b, T_SWA, -(-B_WINDOW // T_SWA) + 1, B_WINDOW, GROUP, rows=B_WINDOW + T_SWA)
    bt_sel = _bias_tiles(bias_c, T_ROW, 3, None, GROUP, center=True)
    bt_win = _bias_tiles(bias_c, T_ROW, C_WINDOW // T_ROW + 1, C_WINDOW, GROUP, rows=C_WINDOW + T_ROW)
    ovl = _overlap(n_cmp)
    expand_g = _gate_expand()

    for layer in range(depth):
        lam_init = 0.8 - 0.6 * math.exp(-0.3 * layer)
        (qat, ka, vat, qbt, kb, vbt, qct, tkv, ksel, vselt, kwin, vwint, gc) = _inproj(
            x, norm_mix[layer][None], w_heads, w_cg, gains2[layer], layer)
        oa = _flash_attn(qat, ka, vat, bt_a, dl=diff_lambda[layer],
                   subln=diff_subln[layer][None], lam_init=lam_init, name="attn_diff")
        ob = _window_attn(qbt, kb, vbt, bt_b, n_sub=8, out_dtype=BF16,
                          sink=sinks[layer].reshape(KV_GROUPS, GROUP, 1, 1), name="attn_swa")
        kc, vct = _compress(tkv, pos[layer], w1[layer], w2p[layer], qk_gain[layer][5:6])
        ocmp, sel = _cmp_attn(qct, kc, vct, ovl, n_cmp, n_sub=4)
        osel = _flash_attn(qct, ksel, vselt, bt_sel, sel=sel, name="attn_sel")
        owin = _window_attn(qct, kwin, vwint, bt_win, n_sub=4, name="attn_win")
        f2 = lambda a: a.reshape(B * S, a.shape[-1])
        x2 = _merge(f2(x), norm_mix[layer][None], f2(oa), f2(ob), f2(ocmp), f2(osel), f2(owin), f2(gc),
                    expand_g, w_mg, wb, wo, layer)
        x2 = _mlp(x2, norm_mlp[layer][None], wu, wd, layer)
        x = x2.reshape(B, S, D_MODEL)
    return x
```

```python
import functools
import math

import numpy as np
import jax
import jax.numpy as jnp
from jax import lax
from jax.experimental import pallas as pl
from jax.experimental.pallas import tpu as pltpu

F32 = jnp.float32
BF16 = jnp.bfloat16

D_MODEL = 1024
HEAD_DIM = 64
A_HEADS = 4
A_VDIM = 128
B_HEADS = 8
B_WINDOW = 128
C_HEADS = 8
KV_GROUPS = 2
GROUP = 4
CMP_BLOCK = 32
CMP_STRIDE = 16
CMP_HIDDEN = 256
SEL_BLOCK = 64
N_SELECT = 16
C_WINDOW = 512
MIX_WIDTH = 512
D_FF = 4096
NUM_BUCKETS = 32
MAX_DISTANCE = 128
NEG_INF = -1e30
FORCE_BONUS = 1e4
EPS = 1e-6
LOG2E = 1.4426950408889634
QK_SCALE = HEAD_DIM ** -0.5 * LOG2E

C_AQ, C_AK, C_AV, C_BQ, C_BKV, C_CQ, C_CKV, C_CG = 0, 512, 1024, 1536, 2048, 2304, 2816, 3584
N_CG = C_HEADS * 3
LANES = 128
CG_PAD = LANES
NC_PAD = 256
N_SEL_BLK = 64
ONES_ROWS = 16
MASK_ROWS = 16

SCORE_SLOTS = 8

T_IN = 512
T_ROW = 256
T_SWA = 256
TV_SWA = 128

VMEM_LIMIT = 48 * 1024 * 1024


def _rms(x, gain):
    ms = jnp.mean(x * x, axis=-1, keepdims=True)
    return (x * lax.rsqrt(ms + EPS)) * gain


def _params(n_axes):
    return pltpu.CompilerParams(dimension_semantics=("arbitrary",) * n_axes,
                                vmem_limit_bytes=VMEM_LIMIT)


def _inproj_kernel(x_ref, nrm_ref, w_ref, wcg_ref, gains_ref,
                   qa_ref, ka_ref, va_ref, qb_ref, kb_ref, vb_ref, qc_ref,
                   tkv_ref, ksel_ref, vsel_ref, kwin_ref, vwin_ref, gc_ref):
    tm = x_ref.shape[1]
    hb = _rms(x_ref[0], nrm_ref[...]).astype(BF16)
    lane = lax.broadcasted_iota(jnp.int32, (tm, LANES), 1)
    first = lane < HEAD_DIM

    def ones_pad(width):
        return jnp.where(lax.broadcasted_iota(jnp.int32, (ONES_ROWS, width), 0) == 0, 1.0, 0.0).astype(BF16)

    def mm(c0, n):
        return jnp.dot(hb, w_ref[:, c0:c0 + n], preferred_element_type=F32)

    def norm_pair(y, gain_idx, scale=None):
        sq = y * y
        ms0 = jnp.sum(jnp.where(first, sq, 0.0), axis=-1, keepdims=True) * (1.0 / HEAD_DIM)
        ms1 = jnp.sum(jnp.where(first, 0.0, sq), axis=-1, keepdims=True) * (1.0 / HEAD_DIM)
        inv = jnp.where(first, lax.rsqrt(ms0 + EPS), lax.rsqrt(ms1 + EPS))
        out = (y * inv) * gains_ref[gain_idx:gain_idx + 1, :]
        return out if scale is None else out * scale

    def pairs(c0):
        r = mm(c0, 256)
        return r[:, :LANES], r[:, LANES:]

    def put_qt(ref, i0, i1, y):
        yt = y.T.astype(BF16)
        n_tiles = ref.shape[3]
        tq = tm // n_tiles
        for t in range(n_tiles):
            ref[0, i0, i1, t] = yt[:HEAD_DIM, tq * t:tq * (t + 1)]
            ref[0, i0, i1 + 1, t] = yt[HEAD_DIM:, tq * t:tq * (t + 1)]

    def put_vt(ref, g0, y, n_tiles):
        yt = y.T.astype(BF16)
        tk = tm // n_tiles
        for g in range(2):
            for t in range(n_tiles):
                ref[0, g0 + g, t, 0:HEAD_DIM, :] = yt[HEAD_DIM * g:HEAD_DIM * (g + 1), tk * t:tk * (t + 1)]
                ref[0, g0 + g, t, HEAD_DIM:HEAD_DIM + ONES_ROWS, :] = ones_pad(tk)

    for ch in range(2):
        for half, y in enumerate(pairs(C_AQ + 256 * ch)):
            put_qt(qa_ref, ch * 2 + half, 0, norm_pair(y, 0, QK_SCALE))
        for half, y in enumerate(pairs(C_AK + 256 * ch)):
            y = norm_pair(y, 1).astype(BF16)
            ka_ref[0, ch * 2 + half, 0] = y[:, :HEAD_DIM]
            ka_ref[0, ch * 2 + half, 1] = y[:, HEAD_DIM:]
        for half, y in enumerate(pairs(C_AV + 256 * ch)):
            yt = y.T.astype(BF16)
            for t in range(tm // T_ROW):
                va_ref[0, ch * 2 + half, t, 0:A_VDIM, :] = yt[:, T_ROW * t:T_ROW * (t + 1)]
                va_ref[0, ch * 2 + half, t, A_VDIM:A_VDIM + ONES_ROWS, :] = ones_pad(T_ROW)
        for half, y in enumerate(pairs(C_BQ + 256 * ch)):
            put_qt(qb_ref, ch, 2 * half, norm_pair(y, 2, QK_SCALE))
        for half, y in enumerate(pairs(C_CQ + 256 * ch)):
            put_qt(qc_ref, ch, 2 * half, norm_pair(y, 4, QK_SCALE))

    def put_k(ref, y):
        y = y.astype(BF16)
        ref[0, 0] = y[:, :HEAD_DIM]
        ref[0, 1] = y[:, HEAD_DIM:]

    yk, yv = pairs(C_BKV)
    put_k(kb_ref, norm_pair(yk, 3))
    put_vt(vb_ref, 0, yv, tm // TV_SWA)
    y0, y1 = pairs(C_CKV)
    for kv, y in enumerate((y0, y1)):
        tkv_ref[0, kv, 0] = y[:, :HEAD_DIM]
        tkv_ref[0, kv, 1] = y[:, HEAD_DIM:]
    yk, yv = pairs(C_CKV + 256)
    yk = norm_pair(yk, 6)
    row = lax.broadcasted_iota(jnp.int32, (tm, LANES), 0)
    hot = jnp.where(lane == HEAD_DIM + (row // SEL_BLOCK) % (T_ROW // SEL_BLOCK), 1.0, 0.0)
    ksel_ref[0, 0] = jnp.where(first, yk, hot).astype(BF16)
    ksel_ref[0, 1] = jnp.where(first, pltpu.roll(yk, HEAD_DIM, 1), hot).astype(BF16)
    put_vt(vsel_ref, 0, yv, tm // T_ROW)
    yk, yv = pairs(C_CKV + 512)
    put_k(kwin_ref, norm_pair(yk, 7))
    put_vt(vwin_ref, 0, yv, tm // T_ROW)

    gc_ref[0] = jax.nn.sigmoid(jnp.dot(hb, wcg_ref[...], preferred_element_type=F32))


def _inproj(x, nrm, w, wcg, gains2, layer):
    B, S, _ = x.shape
    tm = T_IN
    nt = S // tm
    sd = jax.ShapeDtypeStruct
    qt = lambda a, b, tq: sd((B, a, b, S // tq, HEAD_DIM, tq), BF16)
    kk = lambda n: sd((B, n, S, HEAD_DIM), BF16)
    vt = lambda n, dv, tk: sd((B, n, S // tk, dv + ONES_ROWS, tk), BF16)
    out_shape = (
        qt(A_HEADS, 2, T_ROW), sd((B, A_HEADS, 2, S, HEAD_DIM), BF16), vt(A_HEADS, A_VDIM, T_ROW),
        qt(KV_GROUPS, GROUP, T_SWA), kk(KV_GROUPS), vt(KV_GROUPS, HEAD_DIM, TV_SWA),
        qt(KV_GROUPS, GROUP, T_ROW),
        sd((B, 2, KV_GROUPS, S, HEAD_DIM), F32),
        sd((B, KV_GROUPS, S, LANES), BF16), vt(KV_GROUPS, HEAD_DIM, T_ROW),
        kk(KV_GROUPS), vt(KV_GROUPS, HEAD_DIM, T_ROW),
        sd((B, S, CG_PAD), F32),
    )
    s_qt = lambda a, b, tq: pl.BlockSpec((1, a, b, tm // tq, HEAD_DIM, tq), lambda b_, i: (b_, 0, 0, i, 0, 0))
    s_k = lambda n: pl.BlockSpec((1, n, tm, HEAD_DIM), lambda b_, i: (b_, 0, i, 0))
    s_k5 = lambda a, c: pl.BlockSpec((1, a, c, tm, HEAD_DIM), lambda b_, i: (b_, 0, 0, i, 0))
    s_vt = lambda n, dv, tk: pl.BlockSpec((1, n, tm // tk, dv + ONES_ROWS, tk), lambda b_, i: (b_, 0, i, 0, 0))
    row = lambda d: pl.BlockSpec((1, tm, d), lambda b_, i: (b_, i, 0))
    out_specs = (
        s_qt(A_HEADS, 2, T_ROW), s_k5(A_HEADS, 2), s_vt(A_HEADS, A_VDIM, T_ROW),
        s_qt(KV_GROUPS, GROUP, T_SWA), s_k(KV_GROUPS), s_vt(KV_GROUPS, HEAD_DIM, TV_SWA),
        s_qt(KV_GROUPS, GROUP, T_ROW), s_k5(2, KV_GROUPS),
        pl.BlockSpec((1, KV_GROUPS, tm, LANES), lambda b_, i: (b_, 0, i, 0)), s_vt(KV_GROUPS, HEAD_DIM, T_ROW),
        s_k(KV_GROUPS), s_vt(KV_GROUPS, HEAD_DIM, T_ROW),
        row(CG_PAD),
    )
    return pl.pallas_call(
        _inproj_kernel,
        grid=(B, nt),
        in_specs=[
            row(D_MODEL),
            pl.BlockSpec((1, D_MODEL), lambda b_, i: (0, 0)),
            pl.BlockSpec((None, D_MODEL, C_CG), lambda b_, i: (layer, 0, 0), pipeline_mode=pl.Buffered(1)),
            pl.BlockSpec((None, D_MODEL, CG_PAD), lambda b_, i: (layer, 0, 0)),
            pl.BlockSpec((8, LANES), lambda b_, i: (0, 0)),
        ],
        out_specs=out_specs,
        out_shape=out_shape,
        compiler_params=_params(2),
        name="inproj",
    )(x, nrm, w, wcg, gains2)


def _compress_kernel(t_ref, pos_ref, w1_ref, w2_ref, gain_ref, o_ref, ot_ref):
    t = jnp.concatenate([t_ref[0, 0, 0, pl.ds(b, NC_PAD, stride=CMP_STRIDE), :] for b in range(CMP_STRIDE)], axis=1)
    lo = (t + pos_ref[0, 0:1, :]).astype(BF16)
    hi = (t + pos_ref[0, 1:2, :]).astype(BF16)
    v = jnp.dot(lo, w1_ref[0, 0], preferred_element_type=F32)
    u = jnp.dot(hi, w1_ref[0, 1], preferred_element_type=F32)
    pre = v + pltpu.roll(u, NC_PAD - 1, 0)
    hcur = jax.nn.gelu(pre).astype(BF16)
    out = jnp.dot(hcur, w2_ref[0], preferred_element_type=F32)
    o64 = out[:, :HEAD_DIM]
    is_key = pl.program_id(1) == 0
    o_ref[0, 0, 0] = jnp.where(is_key, _rms(o64, gain_ref[...]), o64).astype(BF16)
    ot_ref[0, 0, 0] = out.T[:HEAD_DIM].astype(BF16)


def _compress(tkv, pos, w1, w2p, gain):
    B, S = tkv.shape[0], tkv.shape[3]
    half = CMP_BLOCK // 2 * HEAD_DIM
    return pl.pallas_call(
        _compress_kernel,
        grid=(B, 2, KV_GROUPS),
        in_specs=[
            pl.BlockSpec((1, 1, 1, S, HEAD_DIM), lambda b, kv, g: (b, kv, g, 0, 0)),
            pl.BlockSpec((1, 2, half), lambda b, kv, g: (kv, 0, 0)),
            pl.BlockSpec((1, 2, half, CMP_HIDDEN), lambda b, kv, g: (kv, 0, 0, 0)),
            pl.BlockSpec((1, CMP_HIDDEN, LANES), lambda b, kv, g: (kv, 0, 0)),
            pl.BlockSpec((1, HEAD_DIM), lambda b, kv, g: (0, 0)),
        ],
        out_specs=(pl.BlockSpec((1, 1, 1, NC_PAD, HEAD_DIM), lambda b, kv, g: (b, kv, g, 0, 0)),
                   pl.BlockSpec((1, 1, 1, HEAD_DIM, NC_PAD), lambda b, kv, g: (b, kv, g, 0, 0))),
        out_shape=(jax.ShapeDtypeStruct((B, 2, KV_GROUPS, NC_PAD, HEAD_DIM), BF16),
                   jax.ShapeDtypeStruct((B, 2, KV_GROUPS, HEAD_DIM, NC_PAD), BF16)),
        compiler_params=_params(3),
        name="compress",
    )(tkv, pos, w1, w2p, gain)


def _cmp_attn_kernel(qt_ref, kc_ref, vct_ref, ovl_ref, o_ref, sel_ref, score_ref, rank_ref, *, T, n_cmp, n_sub):
    R = GROUP
    tiles = [pl.program_id(2) * n_sub + t for t in range(n_sub)]
    j_idx = lax.broadcasted_iota(jnp.int32, (N_SEL_BLK, T), 0)
    lane_q = lax.broadcasted_iota(jnp.int32, (N_SEL_BLK, T), 1)

    def attend(rows):
        kc = kc_ref[0, 0, 0, 0:rows, :]
        vct = vct_ref[0, 0, 0, :, 0:rows]
        ovl = ovl_ref[:, 0:rows]
        c_idx = lax.broadcasted_iota(jnp.int32, (rows, T), 0)
        lane_k = lax.broadcasted_iota(jnp.int32, (rows, T), 1)
        for t, I in enumerate(tiles):
            cmask = (c_idx * CMP_STRIDE + (CMP_BLOCK - 1) <= I * T + lane_k) & (c_idx < n_cmp)
            scores = [jnp.dot(kc, qt_ref[0, 0, r, t], preferred_element_type=F32) for r in range(R)]
            probs = []
            for r in range(R):
                s = jnp.where(cmask, scores[r], NEG_INF)
                m = jnp.max(s, axis=0, keepdims=True)
                e = jnp.where(cmask, jnp.exp2(s - m), 0.0)
                l = jnp.sum(e, axis=0, keepdims=True)
                probs.append(e * (1.0 / jnp.where(l > 0.0, l, 1.0)))
            outs = [jnp.dot(vct, p.astype(BF16), preferred_element_type=F32) for p in probs]
            o_ref[0, t * T:(t + 1) * T, :] = jnp.concatenate(outs, axis=0).T
            psum = (probs[0] + probs[1]) + (probs[2] + probs[3])
            p_hi = psum.astype(BF16)
            p_lo = (psum - p_hi.astype(F32)).astype(BF16)
            imp = jnp.dot(ovl, p_hi, preferred_element_type=F32) + jnp.dot(ovl, p_lo, preferred_element_type=F32)
            tq = I * T + lane_q
            cur = tq // SEL_BLOCK
            valid = j_idx * SEL_BLOCK <= tq
            forced = (j_idx == 0) | (j_idx == cur) | (j_idx == cur - 1)
            score_ref[t] = jnp.where(valid, imp + jnp.where(forced, FORCE_BONUS, 0.0), NEG_INF)

    chunk = NC_PAD // 4
    vis_per_tile = T // CMP_STRIDE
    for v in range(4):
        lo = -(-(v * chunk) // vis_per_tile)
        hi = -(-((v + 1) * chunk) // vis_per_tile)
        assert lo % n_sub == 0 and hi % n_sub == 0

        @pl.when((tiles[0] >= lo) & (tiles[0] < hi))
        def _(v=v):
            attend((v + 1) * chunk)

    rank_ref[...] = jnp.zeros(rank_ref.shape, F32)
    n_live = (tiles[-1] + 1) * (T // SEL_BLOCK)
    sub = 8
    n_grp = N_SEL_BLK // sub
    j_loc = lax.broadcasted_iota(jnp.int32, (sub, T), 0)
    for c in range(n_grp):
        @pl.when(c * sub < n_live)
        def _(c=c):
            for t in range(n_sub):
                grps = [score_ref[t, g * sub:(g + 1) * sub, :] for g in range(n_grp)]
                cnts = [rank_ref[t, g * sub:(g + 1) * sub, :] for g in range(n_grp)]
                for jp in range(c * sub, (c + 1) * sub):
                    row = jnp.broadcast_to(score_ref[t, jp:jp + 1, :], (sub, T))
                    for g in range(n_grp):
                        if g * sub > jp:
                            beats = row >= grps[g]
                        elif (g + 1) * sub - 1 < jp:
                            beats = row > grps[g]
                        else:
                            beats = (row > grps[g]) | ((row == grps[g]) & (j_loc + g * sub > jp))
                        cnts[g] = cnts[g] + jnp.where(beats, 1.0, 0.0)
                for g in range(n_grp):
                    rank_ref[t, g * sub:(g + 1) * sub, :] = cnts[g]
    blk = T // SEL_BLOCK
    for t in range(n_sub):
        for j in range(N_SEL_BLK // blk):
            keep = rank_ref[t, j * blk:(j + 1) * blk, :] < N_SELECT
            slab = jnp.concatenate([jnp.where(keep, 0.0, NEG_INF), jnp.zeros((MASK_ROWS - blk, T), F32)], axis=0)
            sel_ref[0, 0, t, j] = slab.astype(BF16)


def _cmp_attn(qct, kc, vct, ovl, n_cmp, n_sub=2):
    B, G, R, NQ, _, T = qct.shape
    S = NQ * T
    return pl.pallas_call(
        functools.partial(_cmp_attn_kernel, T=T, n_cmp=n_cmp, n_sub=n_sub),
        grid=(B, G, NQ // n_sub),
        in_specs=[
            pl.BlockSpec((1, 1, R, n_sub, HEAD_DIM, T), lambda b, g, i: (b, g, 0, i, 0, 0)),
            pl.BlockSpec((1, 1, 1, NC_PAD, HEAD_DIM), lambda b, g, i: (b, 0, g, 0, 0)),
            pl.BlockSpec((1, 1, 1, HEAD_DIM, NC_PAD), lambda b, g, i: (b, 1, g, 0, 0)),
            pl.BlockSpec((N_SEL_BLK, NC_PAD), lambda b, g, i: (0, 0)),
        ],
        out_specs=(
            pl.BlockSpec((1, n_sub * T, R * HEAD_DIM), lambda b, g, i: (b, i, g)),
            pl.BlockSpec((1, 1, n_sub, NQ, MASK_ROWS, T), lambda b, g, i: (b, g, i, 0, 0, 0)),
        ),
        out_shape=(
            jax.ShapeDtypeStruct((B, S, G * R * HEAD_DIM), F32),
            jax.ShapeDtypeStruct((B, G, NQ, NQ, MASK_ROWS, T), BF16),
        ),
        scratch_shapes=[pltpu.VMEM((n_sub, N_SEL_BLK, T), F32), pltpu.VMEM((n_sub, N_SEL_BLK, T), F32)],
        compiler_params=_params(3),
        name="cmp_attn",
    )(qct, kc, vct, ovl)


def _flash_kernel(*refs, R, T, NQ, Dv, nrel, k_per_r, shared_bias, diff, has_sel, early_bias, lam_init):
    it = iter(refs)
    qt_ref, k_ref, vt_ref, bt_ref = (next(it) for _ in range(4))
    sel_ref = next(it) if has_sel else None
    dl_ref = next(it) if diff else None
    sub_ref = next(it) if diff else None
    o_ref, m_ref, acc_ref, s_ref, p_ref, al_ref, tmax_ref = (next(it) for _ in range(7))

    m_ref[...] = jnp.full(m_ref.shape, NEG_INF, F32)
    acc_ref[...] = jnp.zeros(acc_ref.shape, F32)
    bias_r = (lambda r: 0) if shared_bias else (lambda r: r)
    if has_sel:
        pad_rows = jnp.zeros((k_ref.shape[-1] - HEAD_DIM - MASK_ROWS, T), BF16)
    n_steps = NQ * (NQ + 1) // 2
    N_S = SCORE_SLOTS
    assert N_S % 2 == 0 and n_steps % N_S == 0

    def score_stage(I, J, slot, table):
        off = pl.multiple_of(J * T, T)
        rel = jnp.minimum(I - J, nrel - 1)
        for r in range(R):
            kt = k_ref[0, 0, r, pl.ds(off, T), :] if k_per_r else k_ref[0, 0, pl.ds(off, T), :]
            w = qt_ref[0, 0, r, I]
            if has_sel:
                w = jnp.concatenate([w, sel_ref[0, 0, I, J], pad_rows], axis=0)
            s = jnp.dot(kt, w, preferred_element_type=F32)
            if early_bias:
                if table:
                    s = s + bt_ref[0, rel, bias_r(r)]
                tmax_ref[slot, r] = jnp.max(s, axis=0, keepdims=True)
            s_ref[slot, r] = s

    def softmax_stage(I, J, s_slot, slot, table):
        for r in range(R):
            m_prev = m_ref[I, r]
            s = s_ref[s_slot, r]
            if early_bias:
                tile_max = tmax_ref[s_slot, r]
            else:
                if table:
                    s = s + bt_ref[0, jnp.minimum(I - J, nrel - 1), bias_r(r)]
                tile_max = jnp.max(s, axis=0, keepdims=True)
            m_new = jnp.maximum(m_prev, tile_max)
            al_ref[slot, r] = jnp.exp2(m_prev - m_new)
            p_ref[slot, r] = jnp.exp2(s - m_new).astype(BF16)
            m_ref[I, r] = m_new

    def value_stage(J, slot):
        vt = vt_ref[0, 0, J]
        return [jnp.dot(vt, p_ref[slot, r], preferred_element_type=F32) for r in range(R)]

    def accumulate(I, slot, pvs):
        for r in range(R):
            acc_ref[I, r] = al_ref[slot, r] * acc_ref[I, r] + pvs[r]

    def far_advance(I, J):
        wrap = J + 1 > I - (nrel - 1)
        return jnp.where(wrap, I + 1, I), jnp.where(wrap, 0, J + 1)

    def near_advance(I, J):
        on_diag, before_diag = J == I, J == I - 1
        nxt_i = jnp.where(on_diag, I + 1, jnp.where(before_diag, I, 0))
        nxt_j = jnp.where(on_diag, I, jnp.where(before_diag, I, 0))
        return nxt_i, nxt_j

    assert nrel == 3
    n_far = (NQ - 1) * (NQ - 2) // 2
    n_near = 2 * NQ - 1
    far_trips, far_left = divmod(n_far, N_S)
    assert far_left == 1 and (far_left + n_near) % N_S == 0
    near_trips = (far_left + n_near) // N_S

    def step(cur, prev, t, advance, table):
        other = 1 - t % 2
        nxt = advance(*cur)
        pvs = value_stage(prev[1], other)
        score_stage(jnp.minimum(nxt[0], NQ - 1), jnp.minimum(nxt[1], NQ - 1), (t + 1) % N_S, table)
        softmax_stage(cur[0], cur[1], t % N_S, t % 2, table)
        accumulate(prev[0], other, pvs)
        return nxt

    def make_trip(advance, table):
        def trip(u, carry):
            cur, prev = carry[:2], carry[2:]
            for t in range(N_S):
                cur, prev = step(cur, prev, t, advance, table), cur
            return (*cur, *prev)
        return trip

    first = (jnp.int32(nrel - 1), jnp.int32(0))
    score_stage(*first, 0, False)
    p_ref[1] = jnp.zeros(p_ref.shape[1:], BF16)
    al_ref[1] = jnp.ones(al_ref.shape[1:], F32)
    carry = lax.fori_loop(0, far_trips, make_trip(far_advance, False), (*first, *first))
    lax.fori_loop(0, near_trips, make_trip(near_advance, True), carry)
    accumulate(NQ - 1, 1, value_stage(NQ - 1, 1))

    if diff:
        dl = dl_ref[...]
        lam = (jnp.exp(jnp.sum(dl[0:1] * dl[1:2], keepdims=True))
               - jnp.exp(jnp.sum(dl[2:3] * dl[3:4], keepdims=True)) + lam_init)

    def finish(I, carry):
        outs = []
        for r in range(R):
            acc = acc_ref[I, r]
            outs.append(acc[:Dv] * (1.0 / acc[Dv:Dv + 1]))
        rows = pl.ds(pl.multiple_of(I * T, T), T)
        if diff:
            o = (outs[0] - lam * outs[1]).T
            o_ref[0, rows, :] = (_rms(o, sub_ref[...]) * (1.0 - lam_init)).astype(o_ref.dtype)
        else:
            o_ref[0, rows, :] = jnp.concatenate(outs, axis=0).T
        return carry

    lax.fori_loop(0, NQ, finish, 0, unroll=8)


def _flash_attn(qt, k, vt, bias_tab, *, sel=None, dl=None, subln=None, lam_init=0.0, name="attn"):
    B, G, R, NQ, _, T = qt.shape
    S = NQ * T
    Dv = vt.shape[-2] - ONES_ROWS
    nrel, bias_heads = bias_tab.shape[1], bias_tab.shape[2]
    k_per_r = k.ndim == 5
    has_sel = sel is not None
    diff = dl is not None
    ins = [qt, k, vt, bias_tab]
    in_specs = [
        pl.BlockSpec((1, 1, R, NQ, HEAD_DIM, T), lambda b, g: (b, g, 0, 0, 0, 0)),
        (pl.BlockSpec((1, 1, R, S, HEAD_DIM), lambda b, g: (b, g, 0, 0, 0)) if k_per_r
         else pl.BlockSpec((1, 1, S, k.shape[-1]), lambda b, g: (b, g, 0, 0))),
        pl.BlockSpec((1, 1, NQ, Dv + ONES_ROWS, T), lambda b, g: (b, g, 0, 0, 0)),
        pl.BlockSpec((1, nrel, bias_heads, T, T), lambda b, g: (g, 0, 0, 0, 0)),
    ]
    if has_sel:
        ins.append(sel)
        in_specs.append(pl.BlockSpec((1, 1, NQ, NQ, MASK_ROWS, T), lambda b, g: (b, g, 0, 0, 0, 0)))
    if diff:
        ins += [dl, subln]
        in_specs += [pl.BlockSpec((4, HEAD_DIM), lambda b, g: (0, 0)),
                     pl.BlockSpec((1, A_VDIM), lambda b, g: (0, 0))]
    out_w = Dv if diff else R * Dv
    kern = functools.partial(_flash_kernel, R=R, T=T, NQ=NQ, Dv=Dv, nrel=nrel, k_per_r=k_per_r,
                             shared_bias=bias_heads == 1, diff=diff, has_sel=has_sel, early_bias=not diff,
                             lam_init=lam_init)
    return pl.pallas_call(
        kern,
        grid=(B, G),
        in_specs=in_specs,
        out_specs=pl.BlockSpec((1, S, out_w), lambda b, g: (b, 0, g)),
        out_shape=jax.ShapeDtypeStruct((B, S, G * out_w), BF16 if diff else F32),
        scratch_shapes=[pltpu.VMEM((NQ, R, 1, T), F32), pltpu.VMEM((NQ, R, Dv + ONES_ROWS, T), F32),
                        pltpu.VMEM((SCORE_SLOTS, R, T, T), F32), pltpu.VMEM((2, R, T, T), BF16),
                        pltpu.VMEM((2, R, 1, T), F32), pltpu.VMEM((SCORE_SLOTS, R, 1, T), F32)],
        compiler_params=_params(2),
        name=name,
    )(*ins)


def _window_kernel(*refs, R, T, NK, TV, Dv, n_sub, has_sink):
    it = iter(refs)
    qt_ref, k_ref, vt_ref, bt_ref = (next(it) for _ in range(4))
    sink_ref = next(it) if has_sink else None
    o_ref = next(it)
    heads = [(t, r) for t in range(n_sub) for r in range(R)]
    tile = [pl.program_id(2) * n_sub + t for t in range(n_sub)]
    entry = [jnp.minimum(I, -(-(NK - T) // T)) for I in tile]
    first_key = [pl.multiple_of(I * T - jnp.minimum(e * T, NK - T), TV) for I, e in zip(tile, entry)]
    keys = [k_ref[0, 0, pl.ds(fk, NK), :] for fk in first_key]
    scores = {(t, r): jnp.dot(keys[t], qt_ref[0, 0, r, t], preferred_element_type=F32) for t, r in heads}
    probs, maxes = {}, {}
    for t, r in heads:
        s = scores[t, r] + bt_ref[0, entry[t], r]
        maxes[t, r] = jnp.max(s, axis=0, keepdims=True)
        probs[t, r] = jnp.exp2(s - maxes[t, r]).astype(BF16)
    accs = {}
    for t, r in heads:
        acc = None
        for c in range(NK // TV):
            part = jnp.dot(vt_ref[0, 0, first_key[t] // TV + c], probs[t, r][c * TV:(c + 1) * TV],
                           preferred_element_type=F32)
            acc = part if acc is None else acc + part
        accs[t, r] = acc
    for t in range(n_sub):
        outs = []
        for r in range(R):
            num, l = accs[t, r][:Dv], accs[t, r][Dv:Dv + 1]
            if has_sink:
                sk = sink_ref[0, r] * LOG2E
                m_f = jnp.maximum(maxes[t, r], sk)
                w = jnp.exp2(maxes[t, r] - m_f)
                outs.append(num * (w / (l * w + jnp.exp2(sk - m_f))))
            else:
                outs.append(num * (1.0 / l))
        o_ref[0, t * T:(t + 1) * T, :] = jnp.concatenate(outs, axis=0).T.astype(o_ref.dtype)


def _window_attn(qt, k, vt, bias_tab, *, n_sub=2, sink=None, out_dtype=F32, name="attn_window"):
    B, G, R, NQ, _, T = qt.shape
    S = NQ * T
    TV = vt.shape[-1]
    Dv = vt.shape[-2] - ONES_ROWS
    entries, NK = bias_tab.shape[1], bias_tab.shape[3]
    has_sink = sink is not None
    ins = [qt, k, vt, bias_tab]
    in_specs = [
        pl.BlockSpec((1, 1, R, n_sub, HEAD_DIM, T), lambda b, g, i: (b, g, 0, i, 0, 0)),
        pl.BlockSpec((1, 1, S, HEAD_DIM), lambda b, g, i: (b, g, 0, 0)),
        pl.BlockSpec((1, 1, S // TV, Dv + ONES_ROWS, TV), lambda b, g, i: (b, g, 0, 0, 0)),
        pl.BlockSpec((1, entries, R, NK, T), lambda b, g, i: (g, 0, 0, 0, 0)),
    ]
    if has_sink:
        ins.append(sink)
        in_specs.append(pl.BlockSpec((1, R, 1, 1), lambda b, g, i: (g, 0, 0, 0)))
    kern = functools.partial(_window_kernel, R=R, T=T, NK=NK, TV=TV, Dv=Dv, n_sub=n_sub, has_sink=has_sink)
    return pl.pallas_call(
        kern,
        grid=(B, G, NQ // n_sub),
        in_specs=in_specs,
        out_specs=pl.BlockSpec((1, n_sub * T, R * Dv), lambda b, g, i: (b, i, g)),
        out_shape=jax.ShapeDtypeStruct((B, S, G * R * Dv), out_dtype),
        compiler_params=_params(3),
        name=name,
    )(*ins)


def _bucket_np(dist):
    n = np.maximum(dist, 0)
    max_exact = NUM_BUCKETS // 2
    nf = np.maximum(n, 1).astype(np.float32)
    large = max_exact + (np.log(nf / max_exact) / math.log(MAX_DISTANCE / max_exact)
                         * (NUM_BUCKETS - max_exact)).astype(np.int32)
    large = np.minimum(large, NUM_BUCKETS - 1)
    return np.where(n < max_exact, n, large)


def _bucket_starts():
    b = _bucket_np(np.arange(4 * MAX_DISTANCE))
    assert (np.diff(b) >= 0).all() and b[-1] == NUM_BUCKETS - 1
    return [int(np.argmax(b >= i)) for i in range(NUM_BUCKETS)]


def _bias_kernel(tab_ref, o_ref, *, T, rows, nrel, window, back_max, center, starts):
    h = pl.program_id(0)
    base = tab_ref[NUM_BUCKETS - 1, h] * LOG2E if center else 0.0
    key = lax.broadcasted_iota(jnp.int32, (rows, T), 0)
    qry = lax.broadcasted_iota(jnp.int32, (rows, T), 1)
    for rel in range(nrel):
        dist = min(rel * T, back_max) + qry - key
        val = jnp.full((rows, T), tab_ref[0, h] * LOG2E - base, F32)
        for b in range(1, NUM_BUCKETS):
            val = jnp.where(dist >= starts[b], tab_ref[b, h] * LOG2E - base, val)
        allowed = dist >= 0
        if window is not None:
            allowed = allowed & (dist < window)
        o_ref[0, rel, 0] = jnp.where(allowed, val, NEG_INF)


def _bias_tiles(tab, T, nrel, window, R, rows=None, center=False):
    H = tab.shape[1]
    back_max = nrel * T if rows is None else rows - T
    rows = T if rows is None else rows
    return pl.pallas_call(
        functools.partial(_bias_kernel, T=T, rows=rows, nrel=nrel, window=window, back_max=back_max,
                          center=center, starts=_bucket_starts()),
        grid=(H,),
        in_specs=[pl.BlockSpec(memory_space=pltpu.SMEM)],
        out_specs=pl.BlockSpec((1, nrel, 1, rows, T), lambda h: (h // R, 0, h % R, 0, 0)),
        out_shape=jax.ShapeDtypeStruct((H // R, nrel, R, rows, T), F32),
        compiler_params=_params(1),
        name="bias_tiles",
    )(tab)


def _merge_kernel(x_ref, nrm_ref, oa_ref, ob_ref, oc_ref, os_ref, ow_ref, gc_ref, ex_ref, wmg_ref, wb_ref, wo_ref,
                  o_ref):
    x = x_ref[...]
    hb = _rms(x, nrm_ref[...]).astype(BF16)
    gc = gc_ref[...]
    g_hi = gc.astype(BF16)
    g_lo = (gc - g_hi.astype(F32)).astype(BF16)
    gx = jnp.dot(jnp.concatenate([g_hi, g_lo], axis=1), ex_ref[...], preferred_element_type=F32)
    oc = (gx[:, 0:MIX_WIDTH] * oc_ref[...] + gx[:, MIX_WIDTH:2 * MIX_WIDTH] * os_ref[...]
          + gx[:, 2 * MIX_WIDTH:3 * MIX_WIDTH] * ow_ref[...])
    z = None
    for n, br in enumerate((oa_ref[...], ob_ref[...], oc)):
        y = jnp.dot(br.astype(BF16), wb_ref[n], preferred_element_type=F32)
        gate = jax.nn.sigmoid(jnp.dot(hb, wmg_ref[:, n * D_MODEL:(n + 1) * D_MODEL], preferred_element_type=F32))
        t = gate * y
        z = t if z is None else z + t
    o_ref[...] = x + jnp.dot(z.astype(BF16), wo_ref[...], preferred_element_type=F32)


def _merge(x2, nrm, oa, ob, oc, osel, ow, gc, expand_g, wmg, wb, wo, layer, tm=512):
    Tn = x2.shape[0]
    row = lambda d: pl.BlockSpec((tm, d), lambda i: (i, 0))
    const = lambda shape: pl.BlockSpec(shape, lambda i: (0,) * len(shape), pipeline_mode=pl.Buffered(1))
    weight = lambda shape: pl.BlockSpec((None,) + shape, lambda i: (layer,) + (0,) * len(shape),
                                        pipeline_mode=pl.Buffered(1))
    return pl.pallas_call(
        _merge_kernel,
        grid=(Tn // tm,),
        in_specs=[row(D_MODEL), const((1, D_MODEL)),
                  row(MIX_WIDTH), row(MIX_WIDTH), row(MIX_WIDTH), row(MIX_WIDTH), row(MIX_WIDTH),
                  row(CG_PAD),
                  const((2 * CG_PAD, 3 * MIX_WIDTH)),
                  weight((D_MODEL, 3 * D_MODEL)),
                  weight((3, MIX_WIDTH, D_MODEL)),
                  weight((D_MODEL, D_MODEL))],
        out_specs=row(D_MODEL),
        out_shape=jax.ShapeDtypeStruct((Tn, D_MODEL), F32),
        compiler_params=_params(1),
        name="merge",
    )(x2, nrm, oa, ob, oc, osel, ow, gc, expand_g, wmg, wb, wo)


def _mlp_kernel(x_ref, nrm_ref, wu_ref, wd_ref, o_ref, h_ref, acc_ref):
    f = pl.program_id(1)

    @pl.when(f == 0)
    def _():
        h_ref[...] = _rms(x_ref[...], nrm_ref[...]).astype(BF16)
        acc_ref[...] = jnp.zeros(acc_ref.shape, F32)

    u = jnp.dot(h_ref[...], wu_ref[...], preferred_element_type=F32)
    u = jnp.square(jnp.maximum(u, 0.0)).astype(BF16)
    acc_ref[...] += jnp.dot(u, wd_ref[...], preferred_element_type=F32)

    @pl.when(f == pl.num_programs(1) - 1)
    def _():
        o_ref[...] = x_ref[...] + acc_ref[...]


def _mlp(x2, nrm, wu, wd, layer, tm=1024, tf=1024):
    Tn = x2.shape[0]
    return pl.pallas_call(
        _mlp_kernel,
        grid=(Tn // tm, D_FF // tf),
        in_specs=[pl.BlockSpec((tm, D_MODEL), lambda i, f: (i, 0)),
                  pl.BlockSpec((1, D_MODEL), lambda i, f: (0, 0)),
                  pl.BlockSpec((None, D_MODEL, tf), lambda i, f: (layer, 0, f)),
                  pl.BlockSpec((None, tf, D_MODEL), lambda i, f: (layer, f, 0))],
        out_specs=pl.BlockSpec((tm, D_MODEL), lambda i, f: (i, 0)),
        out_shape=jax.ShapeDtypeStruct((Tn, D_MODEL), F32),
        scratch_shapes=[pltpu.VMEM((tm, D_MODEL), BF16), pltpu.VMEM((tm, D_MODEL), F32)],
        compiler_params=_params(2),
        name="mlp",
    )(x2, nrm, wu, wd)


def _overlap(n_cmp):
    c_start = np.arange(NC_PAD) * CMP_STRIDE
    j_start = np.arange(N_SEL_BLK) * SEL_BLOCK
    ov = ((c_start[None, :] < j_start[:, None] + SEL_BLOCK) & (c_start[None, :] + CMP_BLOCK > j_start[:, None])
          & (np.arange(NC_PAD)[None, :] < n_cmp))
    return jnp.asarray(ov.astype(np.float32), BF16)


def _gate_expand():
    e = np.zeros((2, CG_PAD, 3 * MIX_WIDTH), np.float32)
    for h in range(C_HEADS):
        for j in range(3):
            e[:, h * 3 + j, j * MIX_WIDTH + h * HEAD_DIM:j * MIX_WIDTH + (h + 1) * HEAD_DIM] = 1.0
    return jnp.asarray(e.reshape(2 * CG_PAD, 3 * MIX_WIDTH), BF16)


def kernel(x, w_in, qk_gain, diff_lambda, diff_subln, sinks, cmp_pos, cmp_w1, cmp_w2,
           w_branch, w_out, norm_mix, norm_mlp, w_up, w_down, rel_bias):
    B, S, _ = x.shape
    depth = w_in.shape[0]
    n_cmp = (S - CMP_BLOCK) // CMP_STRIDE + 1
    assert S % T_ROW == 0 and S // CMP_STRIDE == NC_PAD and S // SEL_BLOCK == N_SEL_BLK
    half = CMP_BLOCK // 2 * HEAD_DIM

    w_heads = w_in[:, :, :C_CG].astype(BF16)
    w_cg = jnp.pad(w_in[:, :, C_CG:C_CG + N_CG], ((0, 0), (0, 0), (0, CG_PAD - N_CG))).astype(BF16)
    w_mg = w_in[:, :, C_CG + N_CG:].astype(BF16)
    gains2 = jnp.concatenate([qk_gain, qk_gain], axis=-1)
    w1 = cmp_w1.astype(BF16).reshape(depth, 2, 2, half, CMP_HIDDEN)
    w2p = jnp.pad(cmp_w2, ((0, 0), (0, 0), (0, 0), (0, LANES - HEAD_DIM))).astype(BF16)
    pos = cmp_pos.reshape(depth, 2, 2, half)
    wb = w_branch.astype(BF16)
    wo = w_out.astype(BF16)
    wu = w_up.astype(BF16)
    wd = w_down.astype(BF16)

    bias_a = rel_bias[:, :A_HEADS]
    bias_b = rel_bias[:, A_HEADS:A_HEADS + B_HEADS]
    bias_c = rel_bias[:, A_HEADS + B_HEADS:]
    bt_a = _bias_tiles(bias_a, T_ROW, 3, None, 1, center=True)
    bt_b = _bias_tiles(bias_b, T_SWA, -(-B_WINDOW // T_SWA) + 1, B_WINDOW, GROUP, rows=B_WINDOW + T_SWA)
    bt_sel = _bias_tiles(bias_c, T_ROW, 3, None, GROUP, center=True)
    bt_win = _bias_tiles(bias_c, T_ROW, C_WINDOW // T_ROW + 1, C_WINDOW, GROUP, rows=C_WINDOW + T_ROW)
    ovl = _overlap(n_cmp)
    expand_g = _gate_expand()

    for layer in range(depth):
        lam_init = 0.8 - 0.6 * math.exp(-0.3 * layer)
        (qat, ka, vat, qbt, kb, vbt, qct, tkv, ksel, vselt, kwin, vwint, gc) = _inproj(
            x, norm_mix[layer][None], w_heads, w_cg, gains2[layer], layer)
        oa = _flash_attn(qat, ka, vat, bt_a, dl=diff_lambda[layer],
                   subln=diff_subln[layer][None], lam_init=lam_init, name="attn_diff")
        ob = _window_attn(qbt, kb, vbt, bt_b, n_sub=8, out_dtype=BF16,
                          sink=sinks[layer].reshape(KV_GROUPS, GROUP, 1, 1), name="attn_swa")
        kc, vct = _compress(tkv, pos[layer], w1[layer], w2p[layer], qk_gain[layer][5:6])
        ocmp, sel = _cmp_attn(qct, kc, vct, ovl, n_cmp, n_sub=4)
        osel = _flash_attn(qct, ksel, vselt, bt_sel, sel=sel, name="attn_sel")
        owin = _window_attn(qct, kwin, vwint, bt_win, n_sub=4, name="attn_win")
        f2 = lambda a: a.reshape(B * S, a.shape[-1])
        x2 = _merge(f2(x), norm_mix[layer][None], f2(oa), f2(ob), f2(ocmp), f2(osel), f2(owin), f2(gc),
                    expand_g, w_mg, wb, wo, layer)
        x2 = _mlp(x2, norm_mlp[layer][None], wu, wd, layer)
        x = x2.reshape(B, S, D_MODEL)
    return x
```

```python
import functools
import math

import numpy as np
import jax
import jax.numpy as jnp
from jax import lax
from jax.experimental import pallas as pl
from jax.experimental.pallas import tpu as pltpu

F32 = jnp.float32
BF16 = jnp.bfloat16

D_MODEL = 1024
HEAD_DIM = 64
A_HEADS = 4
A_VDIM = 128
B_HEADS = 8
B_WINDOW = 128
C_HEADS = 8
KV_GROUPS = 2
GROUP = 4
CMP_BLOCK = 32
CMP_STRIDE = 16
CMP_HIDDEN = 256
SEL_BLOCK = 64
N_SELECT = 16
C_WINDOW = 512
MIX_WIDTH = 512
D_FF = 4096
NUM_BUCKETS = 32
MAX_DISTANCE = 128
NEG_INF = -1e30
FORCE_BONUS = 1e4
EPS = 1e-6
LOG2E = 1.4426950408889634
QK_SCALE = HEAD_DIM ** -0.5 * LOG2E

C_AQ, C_AK, C_AV, C_BQ, C_BKV, C_CQ, C_CKV, C_CG = 0, 512, 1024, 1536, 2048, 2304, 2816, 3584
N_CG = C_HEADS * 3
LANES = 128
CG_PAD = LANES
NC_PAD = 256
N_SEL_BLK = 64
ONES_ROWS = 16
MASK_ROWS = 16

SCORE_SLOTS = 8

T_IN = 512
T_ROW = 256
T_SWA = 256
TV_SWA = 128

VMEM_LIMIT = 48 * 1024 * 1024


def _rms(x, gain):
    ms = jnp.mean(x * x, axis=-1, keepdims=True)
    return (x * lax.rsqrt(ms + EPS)) * gain


def _params(n_axes):
    return pltpu.CompilerParams(dimension_semantics=("arbitrary",) * n_axes,
                                vmem_limit_bytes=VMEM_LIMIT)


def _inproj_kernel(x_ref, nrm_ref, w_ref, wcg_ref, gains_ref,
                   qa_ref, ka_ref, va_ref, qb_ref, kb_ref, vb_ref, qc_ref,
                   tkv_ref, ksel_ref, vsel_ref, kwin_ref, vwin_ref, gc_ref):
    tm = x_ref.shape[1]
    hb = _rms(x_ref[0], nrm_ref[...]).astype(BF16)
    lane = lax.broadcasted_iota(jnp.int32, (tm, LANES), 1)
    first = lane < HEAD_DIM

    def ones_pad(width):
        return jnp.where(lax.broadcasted_iota(jnp.int32, (ONES_ROWS, width), 0) == 0, 1.0, 0.0).astype(BF16)

    def mm(c0, n):
        return jnp.dot(hb, w_ref[:, c0:c0 + n], preferred_element_type=F32)

    def norm_pair(y, gain_idx, scale=None):
        sq = y * y
        ms0 = jnp.sum(jnp.where(first, sq, 0.0), axis=-1, keepdims=True) * (1.0 / HEAD_DIM)
        ms1 = jnp.sum(jnp.where(first, 0.0, sq), axis=-1, keepdims=True) * (1.0 / HEAD_DIM)
        inv = jnp.where(first, lax.rsqrt(ms0 + EPS), lax.rsqrt(ms1 + EPS))
        out = (y * inv) * gains_ref[gain_idx:gain_idx + 1, :]
        return out if scale is None else out * scale

    def pairs(c0):
        r = mm(c0, 256)
        return r[:, :LANES], r[:, LANES:]

    def put_qt(ref, i0, i1, y):
        yt = y.T.astype(BF16)
        n_tiles = ref.shape[3]
        tq = tm // n_tiles
        for t in range(n_tiles):
            ref[0, i0, i1, t] = yt[:HEAD_DIM, tq * t:tq * (t + 1)]
            ref[0, i0, i1 + 1, t] = yt[HEAD_DIM:, tq * t:tq * (t + 1)]

    def put_vt(ref, g0, y, n_tiles):
        yt = y.T.astype(BF16)
        tk = tm // n_tiles
        for g in range(2):
            for t in range(n_tiles):
                ref[0, g0 + g, t, 0:HEAD_DIM, :] = yt[HEAD_DIM * g:HEAD_DIM * (g + 1), tk * t:tk * (t + 1)]
                ref[0, g0 + g, t, HEAD_DIM:HEAD_DIM + ONES_ROWS, :] = ones_pad(tk)

    for ch in range(2):
        for half, y in enumerate(pairs(C_AQ + 256 * ch)):
            put_qt(qa_ref, ch * 2 + half, 0, norm_pair(y, 0, QK_SCALE))
        for half, y in enumerate(pairs(C_AK + 256 * ch)):
            y = norm_pair(y, 1).astype(BF16)
            ka_ref[0, ch * 2 + half, 0] = y[:, :HEAD_DIM]
            ka_ref[0, ch * 2 + half, 1] = y[:, HEAD_DIM:]
        for half, y in enumerate(pairs(C_AV + 256 * ch)):
            yt = y.T.astype(BF16)
            for t in range(tm // T_ROW):
                va_ref[0, ch * 2 + half, t, 0:A_VDIM, :] = yt[:, T_ROW * t:T_ROW * (t + 1)]
                va_ref[0, ch * 2 + half, t, A_VDIM:A_VDIM + ONES_ROWS, :] = ones_pad(T_ROW)
        for half, y in enumerate(pairs(C_BQ + 256 * ch)):
            put_qt(qb_ref, ch, 2 * half, norm_pair(y, 2, QK_SCALE))
        for half, y in enumerate(pairs(C_CQ + 256 * ch)):
            put_qt(qc_ref, ch, 2 * half, norm_pair(y, 4, QK_SCALE))

    def put_k(ref, y):
        y = y.astype(BF16)
        ref[0, 0] = y[:, :HEAD_DIM]
        ref[0, 1] = y[:, HEAD_DIM:]

    yk, yv = pairs(C_BKV)
    put_k(kb_ref, norm_pair(yk, 3))
    put_vt(vb_ref, 0, yv, tm // TV_SWA)
    y0, y1 = pairs(C_CKV)
    for kv, y in enumerate((y0, y1)):
        tkv_ref[0, kv, 0] = y[:, :HEAD_DIM]
        tkv_ref[0, kv, 1] = y[:, HEAD_DIM:]
    yk, yv = pairs(C_CKV + 256)
    yk = norm_pair(yk, 6)
    row = lax.broadcasted_iota(jnp.int32, (tm, LANES), 0)
    hot = jnp.where(lane == HEAD_DIM + (row // SEL_BLOCK) % (T_ROW // SEL_BLOCK), 1.0, 0.0)
    ksel_ref[0, 0] = jnp.where(first, yk, hot).astype(BF16)
    ksel_ref[0, 1] = jnp.where(first, pltpu.roll(yk, HEAD_DIM, 1), hot).astype(BF16)
    put_vt(vsel_ref, 0, yv, tm // T_ROW)
    yk, yv = pairs(C_CKV + 512)
    put_k(kwin_ref, norm_pair(yk, 7))
    put_vt(vwin_ref, 0, yv, tm // T_ROW)

    gc_ref[0] = jax.nn.sigmoid(jnp.dot(hb, wcg_ref[...], preferred_element_type=F32))


def _inproj(x, nrm, w, wcg, gains2, layer):
    B, S, _ = x.shape
    tm = T_IN
    nt = S // tm
    sd = jax.ShapeDtypeStruct
    qt = lambda a, b, tq: sd((B, a, b, S // tq, HEAD_DIM, tq), BF16)
    kk = lambda n: sd((B, n, S, HEAD_DIM), BF16)
    vt = lambda n, dv, tk: sd((B, n, S // tk, dv + ONES_ROWS, tk), BF16)
    out_shape = (
        qt(A_HEADS, 2, T_ROW), sd((B, A_HEADS, 2, S, HEAD_DIM), BF16), vt(A_HEADS, A_VDIM, T_ROW),
        qt(KV_GROUPS, GROUP, T_SWA), kk(KV_GROUPS), vt(KV_GROUPS, HEAD_DIM, TV_SWA),
        qt(KV_GROUPS, GROUP, T_ROW),
        sd((B, 2, KV_GROUPS, S, HEAD_DIM), F32),
        sd((B, KV_GROUPS, S, LANES), BF16), vt(KV_GROUPS, HEAD_DIM, T_ROW),
        kk(KV_GROUPS), vt(KV_GROUPS, HEAD_DIM, T_ROW),
        sd((B, S, CG_PAD), F32),
    )
    s_qt = lambda a, b, tq: pl.BlockSpec((1, a, b, tm // tq, HEAD_DIM, tq), lambda b_, i: (b_, 0, 0, i, 0, 0))
    s_k = lambda n: pl.BlockSpec((1, n, tm, HEAD_DIM), lambda b_, i: (b_, 0, i, 0))
    s_k5 = lambda a, c: pl.BlockSpec((1, a, c, tm, HEAD_DIM), lambda b_, i: (b_, 0, 0, i, 0))
    s_vt = lambda n, dv, tk: pl.BlockSpec((1, n, tm // tk, dv + ONES_ROWS, tk), lambda b_, i: (b_, 0, i, 0, 0))
    row = lambda d: pl.BlockSpec((1, tm, d), lambda b_, i: (b_, i, 0))
    out_specs = (
        s_qt(A_HEADS, 2, T_ROW), s_k5(A_HEADS, 2), s_vt(A_HEADS, A_VDIM, T_ROW),
        s_qt(KV_GROUPS, GROUP, T_SWA), s_k(KV_GROUPS), s_vt(KV_GROUPS, HEAD_DIM, TV_SWA),
        s_qt(KV_GROUPS, GROUP, T_ROW), s_k5(2, KV_GROUPS),
        pl.BlockSpec((1, KV_GROUPS, tm, LANES), lambda b_, i: (b_, 0, i, 0)), s_vt(KV_GROUPS, HEAD_DIM, T_ROW),
        s_k(KV_GROUPS), s_vt(KV_GROUPS, HEAD_DIM, T_ROW),
        row(CG_PAD),
    )
    return pl.pallas_call(
        _inproj_kernel,
        grid=(B, nt),
        in_specs=[
            row(D_MODEL),
            pl.BlockSpec((1, D_MODEL), lambda b_, i: (0, 0)),
            pl.BlockSpec((None, D_MODEL, C_CG), lambda b_, i: (layer, 0, 0), pipeline_mode=pl.Buffered(1)),
            pl.BlockSpec((None, D_MODEL, CG_PAD), lambda b_, i: (layer, 0, 0)),
            pl.BlockSpec((8, LANES), lambda b_, i: (0, 0)),
        ],
        out_specs=out_specs,
        out_shape=out_shape,
        compiler_params=_params(2),
        name="inproj",
    )(x, nrm, w, wcg, gains2)


def _compress_kernel(t_ref, pos_ref, w1_ref, w2_ref, gain_ref, o_ref, ot_ref):
    t = jnp.concatenate([t_ref[0, 0, 0, pl.ds(b, NC_PAD, stride=CMP_STRIDE), :] for b in range(CMP_STRIDE)], axis=1)
    lo = (t + pos_ref[0, 0:1, :]).astype(BF16)
    hi = (t + pos_ref[0, 1:2, :]).astype(BF16)
    v = jnp.dot(lo, w1_ref[0, 0], preferred_element_type=F32)
    u = jnp.dot(hi, w1_ref[0, 1], preferred_element_type=F32)
    pre = v + pltpu.roll(u, NC_PAD - 1, 0)
    hcur = jax.nn.gelu(pre).astype(BF16)
    out = jnp.dot(hcur, w2_ref[0], preferred_element_type=F32)
    o64 = out[:, :HEAD_DIM]
    is_key = pl.program_id(1) == 0
    o_ref[0, 0, 0] = jnp.where(is_key, _rms(o64, gain_ref[...]), o64).astype(BF16)
    ot_ref[0, 0, 0] = out.T[:HEAD_DIM].astype(BF16)


def _compress(tkv, pos, w1, w2p, gain):
    B, S = tkv.shape[0], tkv.shape[3]
    half = CMP_BLOCK // 2 * HEAD_DIM
    return pl.pallas_call(
        _compress_kernel,
        grid=(B, 2, KV_GROUPS),
        in_specs=[
            pl.BlockSpec((1, 1, 1, S, HEAD_DIM), lambda b, kv, g: (b, kv, g, 0, 0)),
            pl.BlockSpec((1, 2, half), lambda b, kv, g: (kv, 0, 0)),
            pl.BlockSpec((1, 2, half, CMP_HIDDEN), lambda b, kv, g: (kv, 0, 0, 0)),
            pl.BlockSpec((1, CMP_HIDDEN, LANES), lambda b, kv, g: (kv, 0, 0)),
            pl.BlockSpec((1, HEAD_DIM), lambda b, kv, g: (0, 0)),
        ],
        out_specs=(pl.BlockSpec((1, 1, 1, NC_PAD, HEAD_DIM), lambda b, kv, g: (b, kv, g, 0, 0)),
                   pl.BlockSpec((1, 1, 1, HEAD_DIM, NC_PAD), lambda b, kv, g: (b, kv, g, 0, 0))),
        out_shape=(jax.ShapeDtypeStruct((B, 2, KV_GROUPS, NC_PAD, HEAD_DIM), BF16),
                   jax.ShapeDtypeStruct((B, 2, KV_GROUPS, HEAD_DIM, NC_PAD), BF16)),
        compiler_params=_params(3),
        name="compress",
    )(tkv, pos, w1, w2p, gain)


def _cmp_attn_kernel(qt_ref, kc_ref, vct_ref, ovl_ref, o_ref, sel_ref, score_ref, rank_ref, *, T, n_cmp, n_sub):
    R = GROUP
    tiles = [pl.program_id(2) * n_sub + t for t in range(n_sub)]
    j_idx = lax.broadcasted_iota(jnp.int32, (N_SEL_BLK, T), 0)
    lane_q = lax.broadcasted_iota(jnp.int32, (N_SEL_BLK, T), 1)

    def attend(rows):
        kc = kc_ref[0, 0, 0, 0:rows, :]
        vct = vct_ref[0, 0, 0, :, 0:rows]
        ovl = ovl_ref[:, 0:rows]
        c_idx = lax.broadcasted_iota(jnp.int32, (rows, T), 0)
        lane_k = lax.broadcasted_iota(jnp.int32, (rows, T), 1)
        for t, I in enumerate(tiles):
            cmask = (c_idx * CMP_STRIDE + (CMP_BLOCK - 1) <= I * T + lane_k) & (c_idx < n_cmp)
            scores = [jnp.dot(kc, qt_ref[0, 0, r, t], preferred_element_type=F32) for r in range(R)]
            probs = []
            for r in range(R):
                s = jnp.where(cmask, scores[r], NEG_INF)
                m = jnp.max(s, axis=0, keepdims=True)
                e = jnp.where(cmask, jnp.exp2(s - m), 0.0)
                l = jnp.sum(e, axis=0, keepdims=True)
                probs.append(e * (1.0 / jnp.where(l > 0.0, l, 1.0)))
            outs = [jnp.dot(vct, p.astype(BF16), preferred_element_type=F32) for p in probs]
            o_ref[0, t * T:(t + 1) * T, :] = jnp.concatenate(outs, axis=0).T
            psum = (probs[0] + probs[1]) + (probs[2] + probs[3])
            p_hi = psum.astype(BF16)
            p_lo = (psum - p_hi.astype(F32)).astype(BF16)
            imp = jnp.dot(ovl, p_hi, preferred_element_type=F32) + jnp.dot(ovl, p_lo, preferred_element_type=F32)
            tq = I * T + lane_q
            cur = tq // SEL_BLOCK
            valid = j_idx * SEL_BLOCK <= tq
            forced = (j_idx == 0) | (j_idx == cur) | (j_idx == cur - 1)
            score_ref[t] = jnp.where(valid, imp + jnp.where(forced, FORCE_BONUS, 0.0), NEG_INF)

    chunk = NC_PAD // 4
    vis_per_tile = T // CMP_STRIDE
    for v in range(4):
        lo = -(-(v * chunk) // vis_per_tile)
        hi = -(-((v + 1) * chunk) // vis_per_tile)
        assert lo % n_sub == 0 and hi % n_sub == 0

        @pl.when((tiles[0] >= lo) & (tiles[0] < hi))
        def _(v=v):
            attend((v + 1) * chunk)

    rank_ref[...] = jnp.zeros(rank_ref.shape, F32)
    n_live = (tiles[-1] + 1) * (T // SEL_BLOCK)
    sub = 8
    n_grp = N_SEL_BLK // sub
    j_loc = lax.broadcasted_iota(jnp.int32, (sub, T), 0)
    for c in range(n_grp):
        @pl.when(c * sub < n_live)
        def _(c=c):
            for t in range(n_sub):
                grps = [score_ref[t, g * sub:(g + 1) * sub, :] for g in range(n_grp)]
                cnts = [rank_ref[t, g * sub:(g + 1) * sub, :] for g in range(n_grp)]
                for jp in range(c * sub, (c + 1) * sub):
                    row = jnp.broadcast_to(score_ref[t, jp:jp + 1, :], (sub, T))
                    for g in range(n_grp):
                        if g * sub > jp:
                            beats = row >= grps[g]
                        elif (g + 1) * sub - 1 < jp:
                            beats = row > grps[g]
                        else:
                            beats = (row > grps[g]) | ((row == grps[g]) & (j_loc + g * sub > jp))
                        cnts[g] = cnts[g] + jnp.where(beats, 1.0, 0.0)
                for g in range(n_grp):
                    rank_ref[t, g * sub:(g + 1) * sub, :] = cnts[g]
    blk = T // SEL_BLOCK
    for t in range(n_sub):
        for j in range(N_SEL_BLK // blk):
            keep = rank_ref[t, j * blk:(j + 1) * blk, :] < N_SELECT
            slab = jnp.concatenate([jnp.where(keep, 0.0, NEG_INF), jnp.zeros((MASK_ROWS - blk, T), F32)], axis=0)
            sel_ref[0, 0, t, j] = slab.astype(BF16)


def _cmp_attn(qct, kc, vct, ovl, n_cmp, n_sub=2):
    B, G, R, NQ, _, T = qct.shape
    S = NQ * T
    return pl.pallas_call(
        functools.partial(_cmp_attn_kernel, T=T, n_cmp=n_cmp, n_sub=n_sub),
        grid=(B, G, NQ // n_sub),
        in_specs=[
            pl.BlockSpec((1, 1, R, n_sub, HEAD_DIM, T), lambda b, g, i: (b, g, 0, i, 0, 0)),
            pl.BlockSpec((1, 1, 1, NC_PAD, HEAD_DIM), lambda b, g, i: (b, 0, g, 0, 0)),
            pl.BlockSpec((1, 1, 1, HEAD_DIM, NC_PAD), lambda b, g, i: (b, 1, g, 0, 0)),
            pl.BlockSpec((N_SEL_BLK, NC_PAD), lambda b, g, i: (0, 0)),
        ],
        out_specs=(
            pl.BlockSpec((1, n_sub * T, R * HEAD_DIM), lambda b, g, i: (b, i, g)),
            pl.BlockSpec((1, 1, n_sub, NQ, MASK_ROWS, T), lambda b, g, i: (b, g, i, 0, 0, 0)),
        ),
        out_shape=(
            jax.ShapeDtypeStruct((B, S, G * R * HEAD_DIM), F32),
            jax.ShapeDtypeStruct((B, G, NQ, NQ, MASK_ROWS, T), BF16),
        ),
        scratch_shapes=[pltpu.VMEM((n_sub, N_SEL_BLK, T), F32), pltpu.VMEM((n_sub, N_SEL_BLK, T), F32)],
        compiler_params=_params(3),
        name="cmp_attn",
    )(qct, kc, vct, ovl)


def _flash_kernel(*refs, R, T, NQ, Dv, nrel, k_per_r, shared_bias, diff, has_sel, early_bias, lam_init):
    it = iter(refs)
    qt_ref, k_ref, vt_ref, bt_ref = (next(it) for _ in range(4))
    sel_ref = next(it) if has_sel else None
    dl_ref = next(it) if diff else None
    sub_ref = next(it) if diff else None
    o_ref, m_ref, acc_ref, s_ref, p_ref, al_ref, tmax_ref = (next(it) for _ in range(7))

    m_ref[...] = jnp.full(m_ref.shape, NEG_INF, F32)
    acc_ref[...] = jnp.zeros(acc_ref.shape, F32)
    bias_r = (lambda r: 0) if shared_bias else (lambda r: r)
    if has_sel:
        pad_rows = jnp.zeros((k_ref.shape[-1] - HEAD_DIM - MASK_ROWS, T), BF16)
    n_steps = NQ * (NQ + 1) // 2
    N_S = SCORE_SLOTS
    assert N_S % 2 == 0 and n_steps % N_S == 0

    def score_stage(I, J, slot, table):
        off = pl.multiple_of(J * T, T)
        rel = jnp.minimum(I - J, nrel - 1)
        for r in range(R):
            kt = k_ref[0, 0, r, pl.ds(off, T), :] if k_per_r else k_ref[0, 0, pl.ds(off, T), :]
            w = qt_ref[0, 0, r, I]
            if has_sel:
                w = jnp.concatenate([w, sel_ref[0, 0, I, J], pad_rows], axis=0)
            s = jnp.dot(kt, w, preferred_element_type=F32)
            if early_bias:
                if table:
                    s = s + bt_ref[0, rel, bias_r(r)]
                tmax_ref[slot, r] = jnp.max(s, axis=0, keepdims=True)
            s_ref[slot, r] = s

    def softmax_stage(I, J, s_slot, slot, table):
        for r in range(R):
            m_prev = m_ref[I, r]
            s = s_ref[s_slot, r]
            if early_bias:
                tile_max = tmax_ref[s_slot, r]
            else:
                if table:
                    s = s + bt_ref[0, jnp.minimum(I - J, nrel - 1), bias_r(r)]
                tile_max = jnp.max(s, axis=0, keepdims=True)
            m_new = jnp.maximum(m_prev, tile_max)
            al_ref[slot, r] = jnp.exp2(m_prev - m_new)
            p_ref[slot, r] = jnp.exp2(s - m_new).astype(BF16)
            m_ref[I, r] = m_new

    def value_stage(J, slot):
        vt = vt_ref[0, 0, J]
        return [jnp.dot(vt, p_ref[slot, r], preferred_element_type=F32) for r in range(R)]

    def accumulate(I, slot, pvs):
        for r in range(R):
            acc_ref[I, r] = al_ref[slot, r] * acc_ref[I, r] + pvs[r]

    def far_advance(I, J):
        wrap = J + 1 > I - (nrel - 1)
        return jnp.where(wrap, I + 1, I), jnp.where(wrap, 0, J + 1)

    def near_advance(I, J):
        on_diag, before_diag = J == I, J == I - 1
        nxt_i = jnp.where(on_diag, I + 1, jnp.where(before_diag, I, 0))
        nxt_j = jnp.where(on_diag, I, jnp.where(before_diag, I, 0))
        return nxt_i, nxt_j

    assert nrel == 3
    n_far = (NQ - 1) * (NQ - 2) // 2
    n_near = 2 * NQ - 1
    far_trips, far_left = divmod(n_far, N_S)
    assert far_left == 1 and (far_left + n_near) % N_S == 0
    near_trips = (far_left + n_near) // N_S

    def step(cur, prev, t, advance, table):
        other = 1 - t % 2
        nxt = advance(*cur)
        pvs = value_stage(prev[1], other)
        score_stage(jnp.minimum(nxt[0], NQ - 1), jnp.minimum(nxt[1], NQ - 1), (t + 1) % N_S, table)
        softmax_stage(cur[0], cur[1], t % N_S, t % 2, table)
        accumulate(prev[0], other, pvs)
        return nxt

    def make_trip(advance, table):
        def trip(u, carry):
            cur, prev = carry[:2], carry[2:]
            for t in range(N_S):
                cur, prev = step(cur, prev, t, advance, table), cur
            return (*cur, *prev)
        return trip

    first = (jnp.int32(nrel - 1), jnp.int32(0))
    score_stage(*first, 0, False)
    p_ref[1] = jnp.zeros(p_ref.shape[1:], BF16)
    al_ref[1] = jnp.ones(al_ref.shape[1:], F32)
    carry = lax.fori_loop(0, far_trips, make_trip(far_advance, False), (*first, *first))
    lax.fori_loop(0, near_trips, make_trip(near_advance, True), carry)
    accumulate(NQ - 1, 1, value_stage(NQ - 1, 1))

    if diff:
        dl = dl_ref[...]
        lam = (jnp.exp(jnp.sum(dl[0:1] * dl[1:2], keepdims=True))
               - jnp.exp(jnp.sum(dl[2:3] * dl[3:4], keepdims=True)) + lam_init)

    def finish(I, carry):
        outs = []
        for r in range(R):
            acc = acc_ref[I, r]
            outs.append(acc[:Dv] * (1.0 / acc[Dv:Dv + 1]))
        rows = pl.ds(pl.multiple_of(I * T, T), T)
        if diff:
            o = (outs[0] - lam * outs[1]).T
            o_ref[0, rows, :] = (_rms(o, sub_ref[...]) * (1.0 - lam_init)).astype(o_ref.dtype)
        else:
            o_ref[0, rows, :] = jnp.concatenate(outs, axis=0).T
        return carry

    lax.fori_loop(0, NQ, finish, 0, unroll=4)


def _flash_attn(qt, k, vt, bias_tab, *, sel=None, dl=None, subln=None, lam_init=0.0, kv_share=1, name="attn"):
    B, G, R, NQ, _, T = qt.shape
    S = NQ * T
    Dv = vt.shape[-2] - ONES_ROWS
    nrel, bias_heads = bias_tab.shape[1], bias_tab.shape[2]
    k_per_r = k.ndim == 5
    has_sel = sel is not None
    diff = dl is not None
    ins = [qt, k, vt, bias_tab]
    in_specs = [
        pl.BlockSpec((1, 1, R, NQ, HEAD_DIM, T), lambda b, g: (b, g, 0, 0, 0, 0)),
        (pl.BlockSpec((1, 1, R, S, HEAD_DIM), lambda b, g: (b, g, 0, 0, 0)) if k_per_r
         else pl.BlockSpec((1, 1, S, k.shape[-1]), lambda b, g: (b, g // kv_share, 0, 0))),
        pl.BlockSpec((1, 1, NQ, Dv + ONES_ROWS, T), lambda b, g: (b, g // kv_share, 0, 0, 0)),
        pl.BlockSpec((1, nrel, bias_heads, T, T), lambda b, g: (g, 0, 0, 0, 0)),
    ]
    if has_sel:
        ins.append(sel)
        in_specs.append(pl.BlockSpec((1, 1, NQ, NQ, MASK_ROWS, T), lambda b, g: (b, g // kv_share, 0, 0, 0, 0)))
    if diff:
        ins += [dl, subln]
        in_specs += [pl.BlockSpec((4, HEAD_DIM), lambda b, g: (0, 0)),
                     pl.BlockSpec((1, A_VDIM), lambda b, g: (0, 0))]
    out_w = Dv if diff else R * Dv
    kern = functools.partial(_flash_kernel, R=R, T=T, NQ=NQ, Dv=Dv, nrel=nrel, k_per_r=k_per_r,
                             shared_bias=bias_heads == 1, diff=diff, has_sel=has_sel, early_bias=not diff,
                             lam_init=lam_init)
    return pl.pallas_call(
        kern,
        grid=(B, G),
        in_specs=in_specs,
        out_specs=pl.BlockSpec((1, S, out_w), lambda b, g: (b, 0, g)),
        out_shape=jax.ShapeDtypeStruct((B, S, G * out_w), BF16 if diff else F32),
        scratch_shapes=[pltpu.VMEM((NQ, R, 1, T), F32), pltpu.VMEM((NQ, R, Dv + ONES_ROWS, T), F32),
                        pltpu.VMEM((SCORE_SLOTS, R, T, T), F32), pltpu.VMEM((2, R, T, T), BF16),
                        pltpu.VMEM((2, R, 1, T), F32), pltpu.VMEM((SCORE_SLOTS, R, 1, T), F32)],
        compiler_params=_params(2),
        name=name,
    )(*ins)


def _window_kernel(*refs, R, T, NK, TV, Dv, n_sub, has_sink):
    it = iter(refs)
    qt_ref, k_ref, vt_ref, bt_ref = (next(it) for _ in range(4))
    sink_ref = next(it) if has_sink else None
    o_ref = next(it)
    heads = [(t, r) for t in range(n_sub) for r in range(R)]
    tile = [pl.program_id(2) * n_sub + t for t in range(n_sub)]
    entry = [jnp.minimum(I, -(-(NK - T) // T)) for I in tile]
    first_key = [pl.multiple_of(I * T - jnp.minimum(e * T, NK - T), TV) for I, e in zip(tile, entry)]
    keys = [k_ref[0, 0, pl.ds(fk, NK), :] for fk in first_key]
    scores = {(t, r): jnp.dot(keys[t], qt_ref[0, 0, r, t], preferred_element_type=F32) for t, r in heads}
    probs, maxes = {}, {}
    for t, r in heads:
        s = scores[t, r] + bt_ref[0, entry[t], r]
        maxes[t, r] = jnp.max(s, axis=0, keepdims=True)
        probs[t, r] = jnp.exp2(s - maxes[t, r]).astype(BF16)
    accs = {}
    for t, r in heads:
        acc = None
        for c in range(NK // TV):
            part = jnp.dot(vt_ref[0, 0, first_key[t] // TV + c], probs[t, r][c * TV:(c + 1) * TV],
                           preferred_element_type=F32)
            acc = part if acc is None else acc + part
        accs[t, r] = acc
    for t in range(n_sub):
        outs = []
        for r in range(R):
            num, l = accs[t, r][:Dv], accs[t, r][Dv:Dv + 1]
            if has_sink:
                sk = sink_ref[0, r] * LOG2E
                m_f = jnp.maximum(maxes[t, r], sk)
                w = jnp.exp2(maxes[t, r] - m_f)
                outs.append(num * (w / (l * w + jnp.exp2(sk - m_f))))
            else:
                outs.append(num * (1.0 / l))
        o_ref[0, t * T:(t + 1) * T, :] = jnp.concatenate(outs, axis=0).T.astype(o_ref.dtype)


def _window_attn(qt, k, vt, bias_tab, *, n_sub=2, sink=None, out_dtype=F32, name="attn_window"):
    B, G, R, NQ, _, T = qt.shape
    S = NQ * T
    TV = vt.shape[-1]
    Dv = vt.shape[-2] - ONES_ROWS
    entries, NK = bias_tab.shape[1], bias_tab.shape[3]
    has_sink = sink is not None
    ins = [qt, k, vt, bias_tab]
    in_specs = [
        pl.BlockSpec((1, 1, R, n_sub, HEAD_DIM, T), lambda b, g, i: (b, g, 0, i, 0, 0)),
        pl.BlockSpec((1, 1, S, HEAD_DIM), lambda b, g, i: (b, g, 0, 0)),
        pl.BlockSpec((1, 1, S // TV, Dv + ONES_ROWS, TV), lambda b, g, i: (b, g, 0, 0, 0)),
        pl.BlockSpec((1, entries, R, NK, T), lambda b, g, i: (g, 0, 0, 0, 0)),
    ]
    if has_sink:
        ins.append(sink)
        in_specs.append(pl.BlockSpec((1, R, 1, 1), lambda b, g, i: (g, 0, 0, 0)))
    kern = functools.partial(_window_kernel, R=R, T=T, NK=NK, TV=TV, Dv=Dv, n_sub=n_sub, has_sink=has_sink)
    return pl.pallas_call(
        kern,
        grid=(B, G, NQ // n_sub),
        in_specs=in_specs,
        out_specs=pl.BlockSpec((1, n_sub * T, R * Dv), lambda b, g, i: (b, i, g)),
        out_shape=jax.ShapeDtypeStruct((B, S, G * R * Dv), out_dtype),
        compiler_params=_params(3),
        name=name,
    )(*ins)


def _bucket_np(dist):
    n = np.maximum(dist, 0)
    max_exact = NUM_BUCKETS // 2
    nf = np.maximum(n, 1).astype(np.float32)
    large = max_exact + (np.log(nf / max_exact) / math.log(MAX_DISTANCE / max_exact)
                         * (NUM_BUCKETS - max_exact)).astype(np.int32)
    large = np.minimum(large, NUM_BUCKETS - 1)
    return np.where(n < max_exact, n, large)


def _bucket_starts():
    b = _bucket_np(np.arange(4 * MAX_DISTANCE))
    assert (np.diff(b) >= 0).all() and b[-1] == NUM_BUCKETS - 1
    return [int(np.argmax(b >= i)) for i in range(NUM_BUCKETS)]


def _bias_kernel(tab_ref, o_ref, *, T, rows, nrel, window, back_max, center, starts):
    h = pl.program_id(0)
    base = tab_ref[NUM_BUCKETS - 1, h] * LOG2E if center else 0.0
    key = lax.broadcasted_iota(jnp.int32, (rows, T), 0)
    qry = lax.broadcasted_iota(jnp.int32, (rows, T), 1)
    for rel in range(nrel):
        dist = min(rel * T, back_max) + qry - key
        val = jnp.full((rows, T), tab_ref[0, h] * LOG2E - base, F32)
        for b in range(1, NUM_BUCKETS):
            val = jnp.where(dist >= starts[b], tab_ref[b, h] * LOG2E - base, val)
        allowed = dist >= 0
        if window is not None:
            allowed = allowed & (dist < window)
        o_ref[0, rel, 0] = jnp.where(allowed, val, NEG_INF)


def _bias_tiles(tab, T, nrel, window, R, rows=None, center=False):
    H = tab.shape[1]
    back_max = nrel * T if rows is None else rows - T
    rows = T if rows is None else rows
    return pl.pallas_call(
        functools.partial(_bias_kernel, T=T, rows=rows, nrel=nrel, window=window, back_max=back_max,
                          center=center, starts=_bucket_starts()),
        grid=(H,),
        in_specs=[pl.BlockSpec(memory_space=pltpu.SMEM)],
        out_specs=pl.BlockSpec((1, nrel, 1, rows, T), lambda h: (h // R, 0, h % R, 0, 0)),
        out_shape=jax.ShapeDtypeStruct((H // R, nrel, R, rows, T), F32),
        compiler_params=_params(1),
        name="bias_tiles",
    )(tab)


def _merge_kernel(x_ref, nrm_ref, oa_ref, ob_ref, oc_ref, os_ref, ow_ref, gc_ref, ex_ref, wmg_ref, wb_ref, wo_ref,
                  o_ref):
    x = x_ref[...]
    hb = _rms(x, nrm_ref[...]).astype(BF16)
    gc = gc_ref[...]
    g_hi = gc.astype(BF16)
    g_lo = (gc - g_hi.astype(F32)).astype(BF16)
    gx = jnp.dot(jnp.concatenate([g_hi, g_lo], axis=1), ex_ref[...], preferred_element_type=F32)
    oc = (gx[:, 0:MIX_WIDTH] * oc_ref[...] + gx[:, MIX_WIDTH:2 * MIX_WIDTH] * os_ref[...]
          + gx[:, 2 * MIX_WIDTH:3 * MIX_WIDTH] * ow_ref[...])
    z = None
    for n, br in enumerate((oa_ref[...], ob_ref[...], oc)):
        y = jnp.dot(br.astype(BF16), wb_ref[n], preferred_element_type=F32)
        gate = jax.nn.sigmoid(jnp.dot(hb, wmg_ref[:, n * D_MODEL:(n + 1) * D_MODEL], preferred_element_type=F32))
        t = gate * y
        z = t if z is None else z + t
    o_ref[...] = x + jnp.dot(z.astype(BF16), wo_ref[...], preferred_element_type=F32)


def _merge(x2, nrm, oa, ob, oc, osel, ow, gc, expand_g, wmg, wb, wo, layer, tm=512):
    Tn = x2.shape[0]
    row = lambda d: pl.BlockSpec((tm, d), lambda i: (i, 0))
    const = lambda shape: pl.BlockSpec(shape, lambda i: (0,) * len(shape), pipeline_mode=pl.Buffered(1))
    weight = lambda shape: pl.BlockSpec((None,) + shape, lambda i: (layer,) + (0,) * len(shape),
                                        pipeline_mode=pl.Buffered(1))
    return pl.pallas_call(
        _merge_kernel,
        grid=(Tn // tm,),
        in_specs=[row(D_MODEL), const((1, D_MODEL)),
                  row(MIX_WIDTH), row(MIX_WIDTH), row(MIX_WIDTH), row(MIX_WIDTH), row(MIX_WIDTH),
                  row(CG_PAD),
                  const((2 * CG_PAD, 3 * MIX_WIDTH)),
                  weight((D_MODEL, 3 * D_MODEL)),
                  weight((3, MIX_WIDTH, D_MODEL)),
                  weight((D_MODEL, D_MODEL))],
        out_specs=row(D_MODEL),
        out_shape=jax.ShapeDtypeStruct((Tn, D_MODEL), F32),
        compiler_params=_params(1),
        name="merge",
    )(x2, nrm, oa, ob, oc, osel, ow, gc, expand_g, wmg, wb, wo)


def _mlp_kernel(x_ref, nrm_ref, wu_ref, wd_ref, o_ref, h_ref, acc_ref):
    f = pl.program_id(1)

    @pl.when(f == 0)
    def _():
        h_ref[...] = _rms(x_ref[...], nrm_ref[...]).astype(BF16)
        acc_ref[...] = jnp.zeros(acc_ref.shape, F32)

    u = jnp.dot(h_ref[...], wu_ref[...], preferred_element_type=F32)
    u = jnp.square(jnp.maximum(u, 0.0)).astype(BF16)
    acc_ref[...] += jnp.dot(u, wd_ref[...], preferred_element_type=F32)

    @pl.when(f == pl.num_programs(1) - 1)
    def _():
        o_ref[...] = x_ref[...] + acc_ref[...]


def _mlp(x2, nrm, wu, wd, layer, tm=1024, tf=1024):
    Tn = x2.shape[0]
    return pl.pallas_call(
        _mlp_kernel,
        grid=(Tn // tm, D_FF // tf),
        in_specs=[pl.BlockSpec((tm, D_MODEL), lambda i, f: (i, 0)),
                  pl.BlockSpec((1, D_MODEL), lambda i, f: (0, 0)),
                  pl.BlockSpec((None, D_MODEL, tf), lambda i, f: (layer, 0, f)),
                  pl.BlockSpec((None, tf, D_MODEL), lambda i, f: (layer, f, 0))],
        out_specs=pl.BlockSpec((tm, D_MODEL), lambda i, f: (i, 0)),
        out_shape=jax.ShapeDtypeStruct((Tn, D_MODEL), F32),
        scratch_shapes=[pltpu.VMEM((tm, D_MODEL), BF16), pltpu.VMEM((tm, D_MODEL), F32)],
        compiler_params=_params(2),
        name="mlp",
    )(x2, nrm, wu, wd)


def _overlap(n_cmp):
    c_start = np.arange(NC_PAD) * CMP_STRIDE
    j_start = np.arange(N_SEL_BLK) * SEL_BLOCK
    ov = ((c_start[None, :] < j_start[:, None] + SEL_BLOCK) & (c_start[None, :] + CMP_BLOCK > j_start[:, None])
          & (np.arange(NC_PAD)[None, :] < n_cmp))
    return jnp.asarray(ov.astype(np.float32), BF16)


def _gate_expand():
    e = np.zeros((2, CG_PAD, 3 * MIX_WIDTH), np.float32)
    for h in range(C_HEADS):
        for j in range(3):
            e[:, h * 3 + j, j * MIX_WIDTH + h * HEAD_DIM:j * MIX_WIDTH + (h + 1) * HEAD_DIM] = 1.0
    return jnp.asarray(e.reshape(2 * CG_PAD, 3 * MIX_WIDTH), BF16)


def kernel(x, w_in, qk_gain, diff_lambda, diff_subln, sinks, cmp_pos, cmp_w1, cmp_w2,
           w_branch, w_out, norm_mix, norm_mlp, w_up, w_down, rel_bias):
    B, S, _ = x.shape
    depth = w_in.shape[0]
    n_cmp = (S - CMP_BLOCK) // CMP_STRIDE + 1
    assert S % T_ROW == 0 and S // CMP_STRIDE == NC_PAD and S // SEL_BLOCK == N_SEL_BLK
    half = CMP_BLOCK // 2 * HEAD_DIM

    w_heads = w_in[:, :, :C_CG].astype(BF16)
    w_cg = jnp.pad(w_in[:, :, C_CG:C_CG + N_CG], ((0, 0), (0, 0), (0, CG_PAD - N_CG))).astype(BF16)
    w_mg = w_in[:, :, C_CG + N_CG:].astype(BF16)
    gains2 = jnp.concatenate([qk_gain, qk_gain], axis=-1)
    w1 = cmp_w1.astype(BF16).reshape(depth, 2, 2, half, CMP_HIDDEN)
    w2p = jnp.pad(cmp_w2, ((0, 0), (0, 0), (0, 0), (0, LANES - HEAD_DIM))).astype(BF16)
    pos = cmp_pos.reshape(depth, 2, 2, half)
    wb = w_branch.astype(BF16)
    wo = w_out.astype(BF16)
    wu = w_up.astype(BF16)
    wd = w_down.astype(BF16)

    bias_a = rel_bias[:, :A_HEADS]
    bias_b = rel_bias[:, A_HEADS:A_HEADS + B_HEADS]
    bias_c = rel_bias[:, A_HEADS + B_HEADS:]
    bt_a = _bias_tiles(bias_a, T_ROW, 3, None, 1, center=True)
    bt_b = _bias_tiles(bias_b, T_SWA, -(-B_WINDOW // T_SWA) + 1, B_WINDOW, GROUP, rows=B_WINDOW + T_SWA)
    bt_sel2 = _bias_tiles(bias_c, T_ROW, 3, None, GROUP // 2, center=True)
    bt_win = _bias_tiles(bias_c, T_ROW, C_WINDOW // T_ROW + 1, C_WINDOW, GROUP, rows=C_WINDOW + T_ROW)
    ovl = _overlap(n_cmp)
    expand_g = _gate_expand()

    for layer in range(depth):
        lam_init = 0.8 - 0.6 * math.exp(-0.3 * layer)
        (qat, ka, vat, qbt, kb, vbt, qct, tkv, ksel, vselt, kwin, vwint, gc) = _inproj(
            x, norm_mix[layer][None], w_heads, w_cg, gains2[layer], layer)
        oa = _flash_attn(qat, ka, vat, bt_a, dl=diff_lambda[layer],
                   subln=diff_subln[layer][None], lam_init=lam_init, name="attn_diff")
        ob = _window_attn(qbt, kb, vbt, bt_b, n_sub=8, out_dtype=BF16,
                          sink=sinks[layer].reshape(KV_GROUPS, GROUP, 1, 1), name="attn_swa")
        kc, vct = _compress(tkv, pos[layer], w1[layer], w2p[layer], qk_gain[layer][5:6])
        ocmp, sel = _cmp_attn(qct, kc, vct, ovl, n_cmp, n_sub=4)
        osel = _flash_attn(qct.reshape(B, 2 * KV_GROUPS, GROUP // 2, *qct.shape[3:]), ksel, vselt, bt_sel2, sel=sel,
                           kv_share=2, name="attn_sel")
        owin = _window_attn(qct, kwin, vwint, bt_win, n_sub=4, name="attn_win")
        f2 = lambda a: a.reshape(B * S, a.shape[-1])
        x2 = _merge(f2(x), norm_mix[layer][None], f2(oa), f2(ob), f2(ocmp), f2(osel), f2(owin), f2(gc),
                    expand_g, w_mg, wb, wo, layer)
        x2 = _mlp(x2, norm_mlp[layer][None], wu, wd, layer)
        x = x2.reshape(B, S, D_MODEL)
    return x
```
